```python
import math
import jax, jax.numpy as jnp
from jax import lax
import numpy as np

D_MODEL = 1024
BATCH = 16
SEQ = 256
DEPTH = 1
DEC_BATCH = 8
DEC_SEQ = 2048
PAST_LEN = 512

GRID_W = 64
EPS = 1e-6
A_HEADS = 4
A_HEAD_DIM = 128
A_WIDTH = A_HEADS * A_HEAD_DIM
CHUNK = 128
MLA_HEADS = 4
QK_NOPE = 128
QK_ROPE = 64
V_HEAD = 128
B_WIDTH = MLA_HEADS * V_HEAD
Q_RANK = 256
KV_RANK = 128
AXIS_PAIRS = QK_ROPE // 4
ROPE_THETA = 10000.0
Q_BLOCK = 128
ATTN_SCALE = 1.0 / math.sqrt(QK_NOPE + QK_ROPE)
IN_WIDTHS = (A_WIDTH, A_WIDTH, A_WIDTH, Q_RANK, KV_RANK, QK_ROPE, B_WIDTH)
IN_WIDTH = 3 * A_WIDTH + Q_RANK + KV_RANK + QK_ROPE + B_WIDTH
MIX_WIDTH = A_WIDTH + B_WIDTH

kernel_name = 'hymba_chunkmlp_mla_diffusion_step'


def _rmsnorm(x, g):
    xf = x.astype(jnp.float32)
    y = xf * lax.rsqrt(jnp.mean(xf * xf, axis=-1, keepdims=True) + EPS)
    return (y * g.astype(jnp.float32)).astype(x.dtype)


def _split_in(z):
    idx = [int(i) for i in np.cumsum(IN_WIDTHS)[:-1]]
    return jnp.split(z, idx, axis=-1)


def _grid_rope(n_tokens, dtype):
    rows = n_tokens // GRID_W
    r, cc = jnp.meshgrid(jnp.arange(rows), jnp.arange(GRID_W), indexing='ij')
    r = r.reshape(-1).astype(jnp.float32)
    cc = cc.reshape(-1).astype(jnp.float32)
    inv = ROPE_THETA ** (-jnp.arange(AXIS_PAIRS, dtype=jnp.float32) / AXIS_PAIRS)
    ang = jnp.concatenate([r[:, None] * inv, cc[:, None] * inv], axis=-1)
    return jnp.cos(ang).astype(dtype), jnp.sin(ang).astype(dtype)


def _rope(x, cos, sin):
    half = QK_ROPE // 2
    x1, x2 = x[..., :half], x[..., half:]
    return jnp.concatenate([x1 * cos - x2 * sin, x1 * sin + x2 * cos], axis=-1)


def _chunk_mlp(u, v, w_s, b_s, g_v):
    B, L, _ = u.shape
    n = L // CHUNK
    u = jax.nn.gelu(u)
    v = _rmsnorm(jax.nn.gelu(v).reshape(B, L, A_HEADS, A_HEAD_DIM), g_v)
    v = v.reshape(B, n, CHUNK, A_HEADS, A_HEAD_DIM)
    mixed = jnp.einsum('hpq,bnqhd->bnphd', w_s, v) + b_s.T[:, :, None]
    return u * mixed.reshape(B, L, A_WIDTH)


def _expand_kv(ckv, w_ukv):
    B, L, _ = ckv.shape
    kv = (ckv @ w_ukv).reshape(B, L, MLA_HEADS, QK_NOPE + V_HEAD)
    return kv[..., :QK_NOPE], kv[..., QK_NOPE:]


def _block_attention(q_nope, q_ropes, key_sets):
    B, Lq, H, _ = q_nope.shape
    nb = Lq // Q_BLOCK
    v_all = jnp.concatenate([ks[2] for ks in key_sets], axis=1)

    def one_block(i):
        s0 = i * Q_BLOCK
        qn = lax.dynamic_slice_in_dim(q_nope, s0, Q_BLOCK, axis=1)
        logits = []
        for qr, (kn, kr, _) in zip(q_ropes, key_sets):
            qrb = lax.dynamic_slice_in_dim(qr, s0, Q_BLOCK, axis=1)
            logits.append(jnp.einsum('bqhn,bkhn->bhqk', qn, kn)
                          + jnp.einsum('bqhr,bkr->bhqk', qrb, kr))
        s = jnp.concatenate(logits, axis=-1).astype(jnp.float32) * ATTN_SCALE
        p = jax.nn.softmax(s, axis=-1).astype(v_all.dtype)
        return jnp.einsum('bhqk,bkhv->bqhv', p, v_all)

    out = lax.map(one_block, jnp.arange(nb))
    return jnp.moveaxis(out, 0, 1).reshape(B, Lq, H * V_HEAD)


def _front(x, cond, norm_g, w_ada, b_ada, w_in, w_s, b_s, g_v, q_norm_g, w_uq, kv_norm_g):
    B, L, _ = x.shape
    mod = jax.nn.silu(cond) @ w_ada + b_ada
    shift, scale, gate = jnp.split(mod[:, None, :], 3, axis=-1)
    h = _rmsnorm(x, norm_g) * (1 + scale) + shift
    u, v, g_a, c_q, c_kv, k_rope, g_b = _split_in(h @ w_in)
    a_out = _chunk_mlp(u, v, w_s, b_s, g_v) * jax.nn.silu(g_a)
    q = (_rmsnorm(c_q, q_norm_g) @ w_uq).reshape(B, L, MLA_HEADS, QK_NOPE + QK_ROPE)
    ckv = _rmsnorm(c_kv, kv_norm_g)
    return gate, a_out, g_b, q[..., :QK_NOPE], q[..., QK_NOPE:], ckv, k_rope


def _back(x, gate, a_out, attn, g_b, w_o):
    y = jnp.concatenate([a_out, attn * jax.nn.silu(g_b)], axis=-1) @ w_o
    return x + gate * y


def setup_inputs(seed: int = 0) -> dict:
    key = jax.random.key(seed)
    ks = jax.random.split(key, 20)

    def nrm(k, shape, scale=1.0):
        return jax.random.normal(k, shape, jnp.float32) * scale

    return {
        'x_prompt': nrm(ks[0], (BATCH, SEQ, D_MODEL)),
        'x_sample': nrm(ks[1], (DEC_BATCH, DEC_SEQ, D_MODEL)),
        'cache_ckv': nrm(ks[2], (DEC_BATCH, DEPTH, PAST_LEN, KV_RANK)),
        'cache_krope': nrm(ks[3], (DEC_BATCH, DEPTH, PAST_LEN, QK_ROPE)),
        'c': nrm(ks[4], (DEC_BATCH, D_MODEL)),
        'c_ctx': nrm(ks[5], (D_MODEL,)),
        'norm_g': 1.0 + nrm(ks[6], (DEPTH, D_MODEL), 0.02),
        'w_ada': nrm(ks[7], (DEPTH, D_MODEL, 3 * D_MODEL), D_MODEL ** -0.5),
        'b_ada': nrm(ks[8], (DEPTH, 3 * D_MODEL), 0.01),
        'w_in': nrm(ks[9], (DEPTH, D_MODEL, IN_WIDTH), D_MODEL ** -0.5),
        'w_s': nrm(ks[10], (DEPTH, A_HEADS, CHUNK, CHUNK), CHUNK ** -0.5),
        'b_s': nrm(ks[11], (DEPTH, A_HEADS, CHUNK), 0.02),
        'g_v': 1.0 + nrm(ks[12], (DEPTH, A_HEADS, A_HEAD_DIM), 0.02),
        'q_norm_g': 1.0 + nrm(ks[13], (DEPTH, Q_RANK), 0.02),
        'w_uq': nrm(ks[14], (DEPTH, Q_RANK, MLA_HEADS * (QK_NOPE + QK_ROPE)), Q_RANK ** -0.5),
        'kv_norm_g': 1.0 + nrm(ks[15], (DEPTH, KV_RANK), 0.02),
        'w_ukv': nrm(ks[16], (DEPTH, KV_RANK, MLA_HEADS * (QK_NOPE + V_HEAD)), KV_RANK ** -0.5),
        'w_o': nrm(ks[17], (DEPTH, MIX_WIDTH, D_MODEL), MIX_WIDTH ** -0.5),
        'final_g': 1.0 + nrm(ks[18], (D_MODEL,), 0.02),
    }


def reference(x_prompt, x_sample, cache_ckv, cache_krope, c, c_ctx, norm_g, w_ada, b_ada,
              w_in, w_s, b_s, g_v, q_norm_g, w_uq, kv_norm_g, w_ukv, w_o, final_g):
    xp, xs = x_prompt, x_sample
    cos, sin = _grid_rope(x_sample.shape[1], x_sample.dtype)
    new_ckv, new_kr = [], []
    for l in range(DEPTH):
        p = (norm_g[l], w_ada[l], b_ada[l], w_in[l], w_s[l], b_s[l], g_v[l],
             q_norm_g[l], w_uq[l], kv_norm_g[l])
        gate, a_out, g_b, qn, qr, ckv, kr = _front(xp, c_ctx[None, :], *p)
        kn, vv = _expand_kv(ckv, w_ukv[l])
        attn = _block_attention(qn, (qr,), ((kn, kr, vv),))
        xp = _back(xp, gate, a_out, attn, g_b, w_o[l])
        new_ckv.append(ckv)
        new_kr.append(kr)
        gate, a_out, g_b, qn, qr, ckv, kr = _front(xs, c, *p)
        kn, vv = _expand_kv(ckv, w_ukv[l])
        ckv_c = cache_ckv[:, l]
        kn_c, v_c = _expand_kv(ckv_c, w_ukv[l])
        kr_c = cache_krope[:, l]
        qr_rot = _rope(qr, cos[:, None, :], sin[:, None, :])
        kr_rot = _rope(kr, cos, sin)
        attn = _block_attention(qn, (qr_rot, qr), ((kn, kr_rot, vv), (kn_c, kr_c, v_c)))
        xs = _back(xs, gate, a_out, attn, g_b, w_o[l])
    y_prompt = _rmsnorm(xp, final_g)
    y_sample = _rmsnorm(xs, final_g)
    return (y_prompt, y_sample, jnp.stack(new_ckv, axis=1), jnp.stack(new_kr, axis=1))
```

```python
import functools
import math

import jax
import jax.numpy as jnp
from jax import lax
from jax.experimental import pallas as pl
from jax.experimental.pallas import tpu as pltpu

D_MODEL = 1024
GRID_W = 64
EPS = 1e-6
A_HEADS = 4
A_HEAD_DIM = 128
A_WIDTH = A_HEADS * A_HEAD_DIM
CHUNK = 128
MLA_HEADS = 4
QK_NOPE = 128
QK_ROPE = 64
V_HEAD = 128
B_WIDTH = MLA_HEADS * V_HEAD
Q_RANK = 256
KV_RANK = 128
AXIS_PAIRS = QK_ROPE // 4
ROPE_THETA = 10000.0
ATTN_SCALE = 1.0 / math.sqrt(QK_NOPE + QK_ROPE)

QK_WIDTH = KV_RANK + 2 * QK_ROPE
OFF_U = 0
OFF_V = OFF_U + A_WIDTH
OFF_GA = OFF_V + A_WIDTH
OFF_CQ = OFF_GA + A_WIDTH
OFF_CKV = OFF_CQ + Q_RANK
OFF_KR = OFF_CKV + KV_RANK
OFF_GB = OFF_KR + 2 * QK_ROPE
IN_WIDTH2 = OFF_GB + B_WIDTH

TOKEN_TILE = 256
KEY_BLOCK = 512
VMEM_LIMIT_BYTES = 48 * 1024 * 1024

_F32 = jnp.float32
_BF16 = jnp.bfloat16
_NT_DIMS = (((1,), (1,)), ((), ()))


def _silu(x):
    return x * (1.0 / (1.0 + jnp.exp(-x)))


def _gelu_tanh(x):
    return x * (0.5 * (1.0 + jnp.tanh(0.7978845608028654 * (x + 0.044715 * (x * x * x)))))


def _rmsnorm(x, g):
    ms = jnp.mean(x * x, axis=-1, keepdims=True)
    return (x * lax.rsqrt(ms + EPS)) * g


def _mod_body(cond_ref, w_ref, b_ref, o_ref):
    s = _silu(cond_ref[...])
    o_ref[...] = jnp.dot(s.astype(_BF16), w_ref[...].astype(_BF16),
                         preferred_element_type=_F32) + b_ref[...]


def _mod_call(cond, w_ada, b_ada):
    rows = cond.shape[0]
    n_out = w_ada.shape[1]
    col_block = 512
    return pl.pallas_call(
        _mod_body,
        grid=(n_out // col_block,),
        in_specs=[
            pl.BlockSpec((rows, D_MODEL), lambda j: (0, 0)),
            pl.BlockSpec((D_MODEL, col_block), lambda j: (0, j)),
            pl.BlockSpec((1, col_block), lambda j: (0, j)),
        ],
        out_specs=pl.BlockSpec((rows, col_block), lambda j: (0, j)),
        out_shape=jax.ShapeDtypeStruct((rows, n_out), _F32),
        name="adaln_mod",
    )(cond, w_ada, b_ada.reshape(1, n_out))


def _fold_body(wq_ref, wk_ref, o_ref):
    o_ref[0] = lax.dot_general(wq_ref[0], wk_ref[0], _NT_DIMS,
                               precision=lax.Precision.HIGHEST,
                               preferred_element_type=_F32)


def _fold_call(wq_nope, w_uk):
    return pl.pallas_call(
        _fold_body,
        grid=(MLA_HEADS,),
        in_specs=[
            pl.BlockSpec((1, Q_RANK, QK_NOPE), lambda h: (h, 0, 0)),
            pl.BlockSpec((1, KV_RANK, QK_NOPE), lambda h: (h, 0, 0)),
        ],
        out_specs=pl.BlockSpec((1, Q_RANK, KV_RANK), lambda h: (h, 0, 0)),
        out_shape=jax.ShapeDtypeStruct((MLA_HEADS, Q_RANK, KV_RANK), _F32),
        name="fold_q_uk",
    )(wq_nope, w_uk)


def _front_body(x_ref, mod_ref, ng_ref, win_ref, ws_ref, bs_ref, gv_ref, qg_ref, wq_ref,
                kvg_ref, t1_ref, t2k_ref, t2q_ref, *out_refs, emit_cache):
    if emit_cache:
        a_ref, gb_ref, q_ref, k_ref, ckv_ref, kr_ref = out_refs
    else:
        a_ref, gb_ref, q_ref, k_ref = out_refs
    tile = x_ref.shape[1]

    x = x_ref[0]
    shift = mod_ref[0, :, 0:D_MODEL]
    scale = mod_ref[0, :, D_MODEL:2 * D_MODEL]
    h = _rmsnorm(x, ng_ref[...]) * (1.0 + scale) + shift
    hb = h.astype(_BF16)

    def proj(lo, width):
        return jnp.dot(hb, win_ref[:, lo:lo + width], preferred_element_type=_F32)

    u = _gelu_tanh(proj(OFF_U, A_WIDTH))
    v = _gelu_tanh(proj(OFF_V, A_WIDTH))
    gv = gv_ref[...]
    vn = [
        _rmsnorm(v[:, hd * A_HEAD_DIM:(hd + 1) * A_HEAD_DIM],
                 gv[:, hd * A_HEAD_DIM:(hd + 1) * A_HEAD_DIM]).astype(_BF16)
        for hd in range(A_HEADS)
    ]
    bs = bs_ref[...]
    mixed_rows = []
    for c in range(tile // CHUNK):
        cols = []
        for hd in range(A_HEADS):
            mix = jnp.dot(ws_ref[hd], vn[hd][c * CHUNK:(c + 1) * CHUNK, :],
                          preferred_element_type=_F32)
            cols.append(mix + bs[:, hd:hd + 1])
        mixed_rows.append(jnp.concatenate(cols, axis=1))
    mixed = jnp.concatenate(mixed_rows, axis=0)
    a_ref[0] = (u * mixed * _silu(proj(OFF_GA, A_WIDTH))).astype(_BF16)

    gb_ref[0] = _silu(proj(OFF_GB, B_WIDTH))

    t1 = t1_ref[...]
    cq = _rmsnorm(proj(OFF_CQ, Q_RANK), qg_ref[...]).astype(_BF16)
    qa = jnp.dot(cq, wq_ref[...], preferred_element_type=_F32) * ATTN_SCALE
    t2q = t2q_ref[...]
    for hd in range(MLA_HEADS):
        base = hd * QK_WIDTH
        pair = qa[:, base + KV_RANK:base + QK_WIDTH]
        rope = pair * t1 + pltpu.roll(pair, QK_ROPE, 1) * t2q
        q_ref[0, hd] = jnp.concatenate([qa[:, base:base + KV_RANK], rope], axis=1).astype(_BF16)

    ckv = _rmsnorm(proj(OFF_CKV, KV_RANK), kvg_ref[...])
    kpair = proj(OFF_KR, 2 * QK_ROPE)
    krot = kpair * t1 + pltpu.roll(kpair, QK_ROPE, 1) * t2k_ref[...]
    k_ref[0] = jnp.concatenate([ckv, krot], axis=1).astype(_BF16)
    if emit_cache:
        ckv_ref[0, 0] = ckv
        kr_ref[0, 0] = kpair[:, 0:QK_ROPE]


def _front_call(x, mod, mod_index, norm_g, win2, ws_bf, bs_t, g_v, q_norm_g, wq2, kv_norm_g,
                tables, emit_cache):
    batch, seq, _ = x.shape
    tile = TOKEN_TILE
    t1, t2k, t2q = tables
    full = lambda shape: pl.BlockSpec(shape, lambda b, i: (0,) * len(shape))
    tab = pl.BlockSpec((tile, 2 * QK_ROPE), lambda b, i: (i, 0))
    in_specs = [
        pl.BlockSpec((1, tile, D_MODEL), lambda b, i: (b, i, 0)),
        pl.BlockSpec((1, 1, 3 * D_MODEL), lambda b, i: (mod_index(b), 0, 0)),
        full((1, D_MODEL)),
        full((D_MODEL, IN_WIDTH2)),
        full((A_HEADS, CHUNK, CHUNK)),
        full((CHUNK, A_HEADS)),
        full((1, A_WIDTH)),
        full((1, Q_RANK)),
        full((Q_RANK, MLA_HEADS * QK_WIDTH)),
        full((1, KV_RANK)),
        tab, tab, tab,
    ]
    out_shape = [
        jax.ShapeDtypeStruct((batch, seq, A_WIDTH), _BF16),
        jax.ShapeDtypeStruct((batch, seq, B_WIDTH), _F32),
        jax.ShapeDtypeStruct((batch, MLA_HEADS, seq, QK_WIDTH), _BF16),
        jax.ShapeDtypeStruct((batch, seq, QK_WIDTH), _BF16),
    ]
    out_specs = [
        pl.BlockSpec((1, tile, A_WIDTH), lambda b, i: (b, i, 0)),
        pl.BlockSpec((1, tile, B_WIDTH), lambda b, i: (b, i, 0)),
        pl.BlockSpec((1, MLA_HEADS, tile, QK_WIDTH), lambda b, i: (b, 0, i, 0)),
        pl.BlockSpec((1, tile, QK_WIDTH), lambda b, i: (b, i, 0)),
    ]
    if emit_cache:
        out_shape += [
            jax.ShapeDtypeStruct((batch, 1, seq, KV_RANK), _F32),
            jax.ShapeDtypeStruct((batch, 1, seq, QK_ROPE), _F32),
        ]
        out_specs += [
            pl.BlockSpec((1, 1, tile, KV_RANK), lambda b, i: (b, 0, i, 0)),
            pl.BlockSpec((1, 1, tile, QK_ROPE), lambda b, i: (b, 0, i, 0)),
        ]
    return pl.pallas_call(
        functools.partial(_front_body, emit_cache=emit_cache),
        grid=(batch, seq // tile),
        in_specs=in_specs,
        out_specs=out_specs,
        out_shape=out_shape,
        compiler_params=pltpu.CompilerParams(
            dimension_semantics=("arbitrary", "arbitrary"),
            vmem_limit_bytes=VMEM_LIMIT_BYTES),
        name="front_ctx" if emit_cache else "front_lat",
    )(x, mod, norm_g, win2, ws_bf, bs_t, g_v, q_norm_g, wq2, kv_norm_g, t1, t2k, t2q)


def _attend(q, key_blocks):
    m = l = acc = None
    for k in key_blocks:
        s = lax.dot_general(q, k, _NT_DIMS, preferred_element_type=_F32)
        bm = jnp.max(s, axis=1, keepdims=True)
        if m is None:
            m_new = bm
            p = jnp.exp(s - m_new)
            l = jnp.sum(p, axis=1, keepdims=True)
            acc = jnp.dot(p.astype(_BF16), k[:, 0:KV_RANK], preferred_element_type=_F32)
        else:
            m_new = jnp.maximum(m, bm)
            alpha = jnp.exp(m - m_new)
            p = jnp.exp(s - m_new)
            l = alpha * l + jnp.sum(p, axis=1, keepdims=True)
            acc = alpha * acc + jnp.dot(p.astype(_BF16), k[:, 0:KV_RANK],
                                        preferred_element_type=_F32)
        m = m_new
    return acc / l


def _back_body(*refs, has_ctx):
    if has_ctx:
        (q_ref, klat_ref, kctx_ref, a_ref, gb_ref, x_ref, mod_ref, wuv_ref, wo_ref, fg_ref,
         y_ref) = refs
    else:
        q_ref, klat_ref, a_ref, gb_ref, x_ref, mod_ref, wuv_ref, wo_ref, fg_ref, y_ref = refs
        kctx_ref = None
    tq = q_ref.shape[2]
    n_lat = klat_ref.shape[1]
    kb = min(KEY_BLOCK, n_lat)

    q = q_ref[0].reshape(MLA_HEADS * tq, QK_WIDTH)
    blocks = []
    if has_ctx:
        blocks.append(kctx_ref[0])
    for j in range(n_lat // kb):
        blocks.append(klat_ref[0, j * kb:(j + 1) * kb, :])
    o = _attend(q, blocks)

    attn_cols = []
    for pair in range(MLA_HEADS // 2):
        o2 = jnp.concatenate([o[(2 * pair) * tq:(2 * pair + 1) * tq, :],
                              o[(2 * pair + 1) * tq:(2 * pair + 2) * tq, :]], axis=1)
        attn_cols.append(jnp.dot(o2.astype(_BF16), wuv_ref[pair], preferred_element_type=_F32))
    attn = jnp.concatenate(attn_cols, axis=1) * gb_ref[0]

    y = jnp.dot(a_ref[0], wo_ref[0:A_WIDTH, :], preferred_element_type=_F32)
    y = y + jnp.dot(attn.astype(_BF16), wo_ref[A_WIDTH:A_WIDTH + B_WIDTH, :],
                    preferred_element_type=_F32)
    gate = mod_ref[0, :, 2 * D_MODEL:3 * D_MODEL]
    out = x_ref[0] + gate * y
    y_ref[0] = _rmsnorm(out, fg_ref[...])


def _back_call(q, klat, kctx, a_out, gb, x, mod, mod_index, wuv2, wo_bf, final_g):
    batch, seq, _ = x.shape
    tq = TOKEN_TILE
    has_ctx = kctx is not None
    full = lambda shape: pl.BlockSpec(shape, lambda b, i: (0,) * len(shape))
    in_specs = [
        pl.BlockSpec((1, MLA_HEADS, tq, QK_WIDTH), lambda b, i: (b, 0, i, 0)),
        pl.BlockSpec((1, klat.shape[1], QK_WIDTH), lambda b, i: (b, 0, 0)),
    ]
    args = [q, klat]
    if has_ctx:
        in_specs.append(pl.BlockSpec((1, kctx.shape[1], QK_WIDTH), lambda b, i: (b, 0, 0)))
        args.append(kctx)
    in_specs += [
        pl.BlockSpec((1, tq, A_WIDTH), lambda b, i: (b, i, 0)),
        pl.BlockSpec((1, tq, B_WIDTH), lambda b, i: (b, i, 0)),
        pl.BlockSpec((1, tq, D_MODEL), lambda b, i: (b, i, 0)),
        pl.BlockSpec((1, 1, 3 * D_MODEL), lambda b, i: (mod_index(b), 0, 0)),
        full((MLA_HEADS // 2, 2 * KV_RANK, 2 * V_HEAD)),
        full((D_MODEL, D_MODEL)),
        full((1, D_MODEL)),
    ]
    args += [a_out, gb, x, mod, wuv2, wo_bf, final_g]
    return pl.pallas_call(
        functools.partial(_back_body, has_ctx=has_ctx),
        grid=(batch, seq // tq),
        in_specs=in_specs,
        out_specs=pl.BlockSpec((1, tq, D_MODEL), lambda b, i: (b, i, 0)),
        out_shape=jax.ShapeDtypeStruct((batch, seq, D_MODEL), _F32),
        compiler_params=pltpu.CompilerParams(
            dimension_semantics=("arbitrary", "arbitrary"),
            vmem_limit_bytes=VMEM_LIMIT_BYTES),
        name="back_lat" if has_ctx else "back_ctx",
    )(*args)


def _swap_halves(w):
    half = QK_ROPE // 2
    return jnp.concatenate([-w[..., half:], w[..., :half]], axis=-1)


def _rope_tables(n_tokens, rotate):
    ones = jnp.ones((n_tokens, QK_ROPE), _F32)
    zeros = jnp.zeros((n_tokens, QK_ROPE), _F32)
    if rotate:
        pos = jnp.arange(n_tokens)
        r = (pos // GRID_W).astype(_F32)
        cc = (pos % GRID_W).astype(_F32)
        inv = ROPE_THETA ** (-jnp.arange(AXIS_PAIRS, dtype=_F32) / AXIS_PAIRS)
        ang = jnp.concatenate([r[:, None] * inv, cc[:, None] * inv], axis=-1)
        cos = jnp.concatenate([jnp.cos(ang), jnp.cos(ang)], axis=-1)
        sin = jnp.concatenate([jnp.sin(ang), jnp.sin(ang)], axis=-1)
    else:
        cos, sin = ones, zeros
    t1 = jnp.concatenate([cos, zeros], axis=-1)
    t2k = jnp.concatenate([sin, zeros], axis=-1)
    t2q = jnp.concatenate([sin, ones], axis=-1)
    return t1, t2k, t2q


def kernel(x_prompt, x_sample, cache_ckv, cache_krope, c, c_ctx, norm_g, w_ada, b_ada, w_in, w_s,
           b_s, g_v, q_norm_g, w_uq, kv_norm_g, w_ukv, w_o, final_g):
    depth = norm_g.shape[0]
    assert depth == 1
    dec_batch = x_sample.shape[0]
    xp, xs = x_prompt, x_sample
    new_ckv, new_kr = [], []
    for l in range(depth):
        cond = jnp.concatenate(
            [c, c_ctx[None, :], jnp.zeros((16 - dec_batch - 1, D_MODEL), _F32)], axis=0)
        mod = _mod_call(cond, w_ada[l], b_ada[l]).reshape(16, 1, 3 * D_MODEL)

        w_in_l = w_in[l]
        kr_cols = w_in_l[:, 1920:1984]
        win2 = jnp.concatenate(
            [w_in_l[:, :1984], _swap_halves(kr_cols), w_in_l[:, 1984:]], axis=1).astype(_BF16)
        wq3 = w_uq[l].reshape(Q_RANK, MLA_HEADS, QK_NOPE + QK_ROPE)
        wkv3 = w_ukv[l].reshape(KV_RANK, MLA_HEADS, QK_NOPE + V_HEAD)
        wq_nope = jnp.transpose(wq3[:, :, :QK_NOPE], (1, 0, 2))
        w_uk = jnp.transpose(wkv3[:, :, :QK_NOPE], (1, 0, 2))
        w_uv = jnp.transpose(wkv3[:, :, QK_NOPE:], (1, 0, 2))
        wq_abs = _fold_call(wq_nope, w_uk)
        wq_rope = jnp.transpose(wq3[:, :, QK_NOPE:], (1, 0, 2))
        wq2 = jnp.concatenate([wq_abs, wq_rope, _swap_halves(wq_rope)], axis=-1)
        wq2 = jnp.transpose(wq2, (1, 0, 2)).reshape(Q_RANK, MLA_HEADS * QK_WIDTH).astype(_BF16)
        zero_blk = jnp.zeros((KV_RANK, V_HEAD), _F32)
        wuv2 = jnp.stack([
            jnp.block([[w_uv[2 * p], zero_blk], [zero_blk, w_uv[2 * p + 1]]])
            for p in range(MLA_HEADS // 2)]).astype(_BF16)
        wo_bf = w_o[l].astype(_BF16)
        ws_bf = w_s[l].astype(_BF16)
        bs_t = b_s[l].T
        gv_row = g_v[l].reshape(1, A_WIDTH)
        shared = (norm_g[l].reshape(1, D_MODEL), win2, ws_bf, bs_t, gv_row,
                  q_norm_g[l].reshape(1, Q_RANK), wq2, kv_norm_g[l].reshape(1, KV_RANK))
        fg = final_g.reshape(1, D_MODEL)

        ctx_index = lambda b: dec_batch
        a_c, gb_c, q_c, k_c, ckv_c, kr_c = _front_call(
            xp, mod, ctx_index, *shared, _rope_tables(xp.shape[1], False), True)
        xp = _back_call(q_c, k_c, None, a_c, gb_c, xp, mod, ctx_index, wuv2, wo_bf, fg)
        new_ckv.append(ckv_c)
        new_kr.append(kr_c)

        lat_index = lambda b: b
        a_s, gb_s, q_s, k_s = _front_call(
            xs, mod, lat_index, *shared, _rope_tables(xs.shape[1], True), False)
        cache_k = jnp.concatenate(
            [cache_ckv[:, l], jnp.zeros(cache_krope[:, l].shape, _F32), cache_krope[:, l]],
            axis=-1).astype(_BF16)
        xs = _back_call(q_s, k_s, cache_k, a_s, gb_s, xs, mod, lat_index, wuv2, wo_bf, fg)
    return (xp, xs, jnp.concatenate(new_ckv, axis=1), jnp.concatenate(new_kr, axis=1))
```

```python
import functools
import math

import jax
import jax.numpy as jnp
from jax import lax
from jax.experimental import pallas as pl
from jax.experimental.pallas import tpu as pltpu

D_MODEL = 1024
GRID_W = 64
EPS = 1e-6
A_HEADS = 4
A_HEAD_DIM = 128
A_WIDTH = A_HEADS * A_HEAD_DIM
CHUNK = 128
MLA_HEADS = 4
QK_NOPE = 128
QK_ROPE = 64
V_HEAD = 128
B_WIDTH = MLA_HEADS * V_HEAD
Q_RANK = 256
KV_RANK = 128
AXIS_PAIRS = QK_ROPE // 4
ROPE_THETA = 10000.0
ATTN_SCALE = 1.0 / math.sqrt(QK_NOPE + QK_ROPE)
LOG2_E = 1.4426950408889634

QK_WIDTH = KV_RANK + 2 * QK_ROPE
OFF_U = 0
OFF_V = OFF_U + A_WIDTH
OFF_GA = OFF_V + A_WIDTH
OFF_CQ = OFF_GA + A_WIDTH
OFF_CKV = OFF_CQ + Q_RANK
OFF_KR = OFF_CKV + KV_RANK
OFF_GB = OFF_KR + 2 * QK_ROPE
IN_WIDTH = OFF_GB - QK_ROPE + B_WIDTH
IN_WIDTH2 = OFF_GB + B_WIDTH

LANES = 128
TOKEN_TILE = 256
KEY_BLOCK = 512
PROJ_BLOCK = 256
VMEM_LIMIT_BYTES = 48 * 1024 * 1024

_F32 = jnp.float32
_BF16 = jnp.bfloat16
_NT_DIMS = (((1,), (1,)), ((), ()))


def _silu(x):
    hx = 0.5 * x
    return hx + hx * jnp.tanh(hx)


def _gelu_tanh(x):
    return x * (0.5 * (1.0 + jnp.tanh(0.7978845608028654 * (x + 0.044715 * (x * x * x)))))


def _rmsnorm(x, g):
    ms = jnp.mean(x * x, axis=-1, keepdims=True)
    return (x * lax.rsqrt(ms + EPS)) * g


def _mod_body(cond_ref, w_ref, b_ref, o_ref):
    s = _silu(cond_ref[...])
    o_ref[...] = jnp.dot(s.astype(_BF16), w_ref[...].astype(_BF16),
                         preferred_element_type=_F32) + b_ref[...]


def _mod_call(cond, w_ada, b_ada):
    rows = cond.shape[0]
    n_out = w_ada.shape[1]
    col_block = 512
    return pl.pallas_call(
        _mod_body,
        grid=(n_out // col_block,),
        in_specs=[
            pl.BlockSpec((rows, D_MODEL), lambda j: (0, 0)),
            pl.BlockSpec((D_MODEL, col_block), lambda j: (0, j)),
            pl.BlockSpec((1, col_block), lambda j: (0, j)),
        ],
        out_specs=pl.BlockSpec((rows, col_block), lambda j: (0, j)),
        out_shape=jax.ShapeDtypeStruct((rows, n_out), _F32),
        name="adaln_mod",
    )(cond, w_ada, b_ada.reshape(1, n_out))


def _fold_body(wq_ref, wk_ref, o_ref):
    o_ref[0] = lax.dot_general(wq_ref[0], wk_ref[0], _NT_DIMS,
                               precision=lax.Precision.HIGHEST,
                               preferred_element_type=_F32)


def _fold_call(wq_nope, w_uk):
    return pl.pallas_call(
        _fold_body,
        grid=(MLA_HEADS,),
        in_specs=[
            pl.BlockSpec((1, Q_RANK, QK_NOPE), lambda h: (h, 0, 0)),
            pl.BlockSpec((1, KV_RANK, QK_NOPE), lambda h: (h, 0, 0)),
        ],
        out_specs=pl.BlockSpec((1, Q_RANK, KV_RANK), lambda h: (h, 0, 0)),
        out_shape=jax.ShapeDtypeStruct((MLA_HEADS, Q_RANK, KV_RANK), _F32),
        name="fold_q_uk",
    )(wq_nope, w_uk)


_KR_BLOCK = OFF_KR // LANES
_KR_HALF_ROW = OFF_KR // QK_ROPE


def _prep_body(a_ref, b_ref, o_ref):
    a = a_ref[...]
    b = b_ref[...]
    half = QK_ROPE // 2
    partner = jnp.concatenate([-b[half:, :], b[:half, :]], axis=0)
    is_kr = pl.program_id(0) == _KR_BLOCK
    b = jnp.where(is_kr, partner, b)
    o_ref[...] = jnp.concatenate([a, b], axis=0).T.astype(_BF16)


def _prep_call(w_in_t):
    def ra(j):
        return (jnp.where(j <= _KR_BLOCK, 2 * j, 2 * j - 1), 0)

    def rb(j):
        return (jnp.where(j < _KR_BLOCK, 2 * j + 1, 2 * j), 0)

    assert _KR_HALF_ROW == 2 * _KR_BLOCK
    return pl.pallas_call(
        _prep_body,
        grid=(IN_WIDTH2 // LANES,),
        in_specs=[
            pl.BlockSpec((QK_ROPE, D_MODEL), ra),
            pl.BlockSpec((QK_ROPE, D_MODEL), rb),
        ],
        out_specs=pl.BlockSpec((D_MODEL, LANES), lambda j: (0, j)),
        out_shape=jax.ShapeDtypeStruct((D_MODEL, IN_WIDTH2), _BF16),
        name="prep_w_in",
    )(w_in_t, w_in_t)


def _front_step(x_ref, mod_ref, ng_ref, win_ref, ws_ref, bs_ref, gv_ref, qg_ref, wq_ref, kvg_ref,
                t1_ref, t2k_ref, t2q_ref, out_refs, h_write, h_read, z_write, z_read, emit_cache):
    if emit_cache:
        a_ref, gb_ref, q_ref, k_ref, ckv_ref, kr_ref = out_refs
    else:
        a_ref, gb_ref, q_ref, k_ref = out_refs
    tile = z_read.shape[0]
    half = tile // 2
    assert IN_WIDTH2 == 10 * PROJ_BLOCK

    def project(j):
        lo = j * PROJ_BLOCK
        z_write[:, lo:lo + PROJ_BLOCK] = jnp.dot(h_read[...], win_ref[:, lo:lo + PROJ_BLOCK],
                                                 preferred_element_type=_F32)

    shift = mod_ref[0, :, 0:D_MODEL]
    gain = ng_ref[...] * (1.0 + mod_ref[0, :, D_MODEL:2 * D_MODEL])

    def norm_rows(lo):
        x = x_ref[0, lo:lo + half, :]
        ms = jnp.mean(x * x, axis=-1, keepdims=True)
        h_write[lo:lo + half, :] = ((x * lax.rsqrt(ms + EPS)) * gain + shift).astype(_BF16)

    def vn_head(hd):
        lo = hd * A_HEAD_DIM
        v = _gelu_tanh(z_read[:, OFF_V + lo:OFF_V + lo + A_HEAD_DIM])
        return _rmsnorm(v, gv_ref[:, lo:lo + A_HEAD_DIM]).astype(_BF16)

    def mix_head(hd, vn):
        bias = bs_ref[:, hd:hd + 1]
        rows = [jnp.dot(ws_ref[hd], vn[c * CHUNK:(c + 1) * CHUNK, :],
                        preferred_element_type=_F32) + bias for c in range(tile // CHUNK)]
        return jnp.concatenate(rows, axis=0)

    def gate_a(hd, mixed):
        lo = hd * A_HEAD_DIM
        u = _gelu_tanh(z_read[:, OFF_U + lo:OFF_U + lo + A_HEAD_DIM])
        g = _silu(z_read[:, OFF_GA + lo:OFF_GA + lo + A_HEAD_DIM])
        a_ref[0, :, lo:lo + A_HEAD_DIM] = (u * mixed * g).astype(_BF16)

    def gate_b(lo, width):
        gb_ref[0, :, lo:lo + width] = _silu(z_read[:, OFF_GB + lo:OFF_GB + lo + width])

    t1 = t1_ref[...]

    def rope_q(qa, hd):
        base = hd * QK_WIDTH
        pair = qa[:, base + KV_RANK:base + QK_WIDTH]
        rope = pair * t1 + pltpu.roll(pair, QK_ROPE, 1) * t2q_ref[...]
        q_ref[0, hd] = jnp.concatenate([qa[:, base:base + KV_RANK], rope], axis=1).astype(_BF16)

    project(0)
    norm_rows(0)
    project(1)
    norm_rows(half)
    project(2)

    cq = _rmsnorm(z_read[:, OFF_CQ:OFF_CQ + Q_RANK],
                  qg_ref[...] * (ATTN_SCALE * LOG2_E)).astype(_BF16)
    ckv = _rmsnorm(z_read[:, OFF_CKV:OFF_CKV + KV_RANK], kvg_ref[...])
    kpair = z_read[:, OFF_KR:OFF_KR + 2 * QK_ROPE]
    krot = kpair * t1 + pltpu.roll(kpair, QK_ROPE, 1) * t2k_ref[...]
    k_ref[0] = jnp.concatenate([ckv, krot], axis=1).astype(_BF16)
    if emit_cache:
        ckv_ref[0, 0] = ckv
        kr_ref[0, 0] = kpair[:, 0:QK_ROPE]
    qa = jnp.dot(cq, wq_ref[...], preferred_element_type=_F32)

    vn0 = vn_head(0)
    project(3)
    vn1 = vn_head(1)
    mixed0 = mix_head(0, vn0)
    vn2 = vn_head(2)
    project(4)
    vn3 = vn_head(3)
    mixed1 = mix_head(1, vn1)
    rope_q(qa, 0)
    rope_q(qa, 1)
    project(5)
    rope_q(qa, 2)
    rope_q(qa, 3)
    mixed2 = mix_head(2, vn2)
    gate_a(0, mixed0)
    project(6)
    gate_a(1, mixed1)
    mixed3 = mix_head(3, vn3)
    gate_a(2, mixed2)
    project(7)
    gate_a(3, mixed3)
    project(8)
    gate_b(0, B_WIDTH // 2)
    project(9)
    gate_b(B_WIDTH // 2, B_WIDTH // 2)


def _front_body(x_ref, mod_ref, ng_ref, win_ref, ws_ref, bs_ref, gv_ref, qg_ref, wq_ref,
                kvg_ref, t1_ref, t2k_ref, t2q_ref, *rest, emit_cache):
    out_refs, (h_even, h_odd, z_even, z_odd) = rest[:-4], rest[-4:]
    t = pl.program_id(0)

    @pl.when(t == 0)
    def _():
        h_odd[...] = jnp.zeros(h_odd.shape, _BF16)
        z_odd[...] = jnp.zeros(z_odd.shape, _F32)

    def step(h_write, h_read, z_write, z_read):
        _front_step(x_ref, mod_ref, ng_ref, win_ref, ws_ref, bs_ref, gv_ref, qg_ref, wq_ref,
                    kvg_ref, t1_ref, t2k_ref, t2q_ref, out_refs, h_write, h_read, z_write, z_read,
                    emit_cache)

    @pl.when(t % 2 == 0)
    def _():
        step(h_even, h_odd, z_even, z_odd)

    @pl.when(t % 2 == 1)
    def _():
        step(h_odd, h_even, z_odd, z_even)


def _front_call(x, mod, mod_index, norm_g, win2, ws_bf, bs_t, g_v, q_norm_g, wq2, kv_norm_g,
                tables, emit_cache):
    batch, seq, _ = x.shape
    tile = TOKEN_TILE
    tiles_per_row = seq // tile
    n_tiles = batch * tiles_per_row
    t1, t2k, t2q = tables

    def proj_tile(t):
        tt = jnp.minimum(t, n_tiles - 1)
        return tt // tiles_per_row, tt % tiles_per_row

    def fin_tile(t):
        tt = jnp.maximum(t - 2, 0)
        return tt // tiles_per_row, tt % tiles_per_row

    full = lambda shape: pl.BlockSpec(shape, lambda t: (0,) * len(shape))
    tab = pl.BlockSpec((tile, 2 * QK_ROPE), lambda t: (fin_tile(t)[1], 0))
    in_specs = [
        pl.BlockSpec((1, tile, D_MODEL), lambda t: (*proj_tile(t), 0)),
        pl.BlockSpec((1, 1, 3 * D_MODEL), lambda t: (mod_index(proj_tile(t)[0]), 0, 0)),
        full((1, D_MODEL)),
        full((D_MODEL, IN_WIDTH2)),
        full((A_HEADS, CHUNK, CHUNK)),
        full((CHUNK, A_HEADS)),
        full((1, A_WIDTH)),
        full((1, Q_RANK)),
        full((Q_RANK, MLA_HEADS * QK_WIDTH)),
        full((1, KV_RANK)),
        tab, tab, tab,
    ]
    out_shape = [
        jax.ShapeDtypeStruct((batch, seq, A_WIDTH), _BF16),
        jax.ShapeDtypeStruct((batch, seq, B_WIDTH), _F32),
        jax.ShapeDtypeStruct((batch, MLA_HEADS, seq, QK_WIDTH), _BF16),
        jax.ShapeDtypeStruct((batch, seq, QK_WIDTH), _BF16),
    ]
    out_specs = [
        pl.BlockSpec((1, tile, A_WIDTH), lambda t: (*fin_tile(t), 0)),
        pl.BlockSpec((1, tile, B_WIDTH), lambda t: (*fin_tile(t), 0)),
        pl.BlockSpec((1, MLA_HEADS, tile, QK_WIDTH),
                     lambda t: (fin_tile(t)[0], 0, fin_tile(t)[1], 0)),
        pl.BlockSpec((1, tile, QK_WIDTH), lambda t: (*fin_tile(t), 0)),
    ]
    if emit_cache:
        out_shape += [
            jax.ShapeDtypeStruct((batch, 1, seq, KV_RANK), _F32),
            jax.ShapeDtypeStruct((batch, 1, seq, QK_ROPE), _F32),
        ]
        out_specs += [
            pl.BlockSpec((1, 1, tile, KV_RANK), lambda t: (fin_tile(t)[0], 0, fin_tile(t)[1], 0)),
            pl.BlockSpec((1, 1, tile, QK_ROPE), lambda t: (fin_tile(t)[0], 0, fin_tile(t)[1], 0)),
        ]
    return pl.pallas_call(
        functools.partial(_front_body, emit_cache=emit_cache),
        grid=(n_tiles + 2,),
        in_specs=in_specs,
        out_specs=out_specs,
        out_shape=out_shape,
        scratch_shapes=[pltpu.VMEM((tile, D_MODEL), _BF16), pltpu.VMEM((tile, D_MODEL), _BF16),
                        pltpu.VMEM((tile, IN_WIDTH2), _F32), pltpu.VMEM((tile, IN_WIDTH2), _F32)],
        compiler_params=pltpu.CompilerParams(
            dimension_semantics=("arbitrary",),
            vmem_limit_bytes=VMEM_LIMIT_BYTES),
        name="front_ctx" if emit_cache else "front_lat",
    )(x, mod, norm_g, win2, ws_bf, bs_t, g_v, q_norm_g, wq2, kv_norm_g, t1, t2k, t2q)


def _attend(q, key_blocks):
    m = l = acc = None
    for k in key_blocks:
        s = lax.dot_general(q, k, _NT_DIMS, preferred_element_type=_F32)
        bm = jnp.max(s, axis=1, keepdims=True)
        if m is None:
            m_new = bm
            p = jnp.exp2(s - m_new)
            l = jnp.sum(p, axis=1, keepdims=True)
            acc = jnp.dot(p.astype(_BF16), k[:, 0:KV_RANK], preferred_element_type=_F32)
        else:
            m_new = jnp.maximum(m, bm)
            alpha = jnp.exp2(m - m_new)
            p = jnp.exp2(s - m_new)
            l = alpha * l + jnp.sum(p, axis=1, keepdims=True)
            acc = alpha * acc + jnp.dot(p.astype(_BF16), k[:, 0:KV_RANK],
                                        preferred_element_type=_F32)
        m = m_new
    return acc / l


def _back_body(*refs, has_ctx):
    if has_ctx:
        (q_ref, klat_ref, kctx_ref, a_ref, gb_ref, x_ref, mod_ref, wuv_ref, wo_ref, fg_ref,
         y_ref) = refs
    else:
        q_ref, klat_ref, a_ref, gb_ref, x_ref, mod_ref, wuv_ref, wo_ref, fg_ref, y_ref = refs
        kctx_ref = None
    tq = q_ref.shape[2]
    n_lat = klat_ref.shape[1]
    kb = min(KEY_BLOCK, n_lat)

    q = q_ref[0].reshape(MLA_HEADS * tq, QK_WIDTH)
    blocks = []
    if has_ctx:
        blocks.append(kctx_ref[0])
    for j in range(n_lat // kb):
        blocks.append(klat_ref[0, j * kb:(j + 1) * kb, :])
    o = _attend(q, blocks)

    attn_cols = []
    for pair in range(MLA_HEADS // 2):
        o2 = jnp.concatenate([o[(2 * pair) * tq:(2 * pair + 1) * tq, :],
                              o[(2 * pair + 1) * tq:(2 * pair + 2) * tq, :]], axis=1)
        attn_cols.append(jnp.dot(o2.astype(_BF16), wuv_ref[pair], preferred_element_type=_F32))
    attn = jnp.concatenate(attn_cols, axis=1) * gb_ref[0]

    y = jnp.dot(a_ref[0], wo_ref[0:A_WIDTH, :], preferred_element_type=_F32)
    y = y + jnp.dot(attn.astype(_BF16), wo_ref[A_WIDTH:A_WIDTH + B_WIDTH, :],
                    preferred_element_type=_F32)
    gate = mod_ref[0, :, 2 * D_MODEL:3 * D_MODEL]
    out = x_ref[0] + gate * y
    y_ref[0] = _rmsnorm(out, fg_ref[...])


def _back_call(q, klat, kctx, a_out, gb, x, mod, mod_index, wuv2, wo_bf, final_g):
    batch, seq, _ = x.shape
    tq = TOKEN_TILE
    has_ctx = kctx is not None
    full = lambda shape: pl.BlockSpec(shape, lambda b, i: (0,) * len(shape))
    in_specs = [
        pl.BlockSpec((1, MLA_HEADS, tq, QK_WIDTH), lambda b, i: (b, 0, i, 0)),
        pl.BlockSpec((1, klat.shape[1], QK_WIDTH), lambda b, i: (b, 0, 0)),
    ]
    args = [q, klat]
    if has_ctx:
        in_specs.append(pl.BlockSpec((1, kctx.shape[1], QK_WIDTH), lambda b, i: (b, 0, 0)))
        args.append(kctx)
    in_specs += [
        pl.BlockSpec((1, tq, A_WIDTH), lambda b, i: (b, i, 0)),
        pl.BlockSpec((1, tq, B_WIDTH), lambda b, i: (b, i, 0)),
        pl.BlockSpec((1, tq, D_MODEL), lambda b, i: (b, i, 0)),
        pl.BlockSpec((1, 1, 3 * D_MODEL), lambda b, i: (mod_index(b), 0, 0)),
        full((MLA_HEADS // 2, 2 * KV_RANK, 2 * V_HEAD)),
        full((D_MODEL, D_MODEL)),
        full((1, D_MODEL)),
    ]
    args += [a_out, gb, x, mod, wuv2, wo_bf, final_g]
    return pl.pallas_call(
        functools.partial(_back_body, has_ctx=has_ctx),
        grid=(batch, seq // tq),
        in_specs=in_specs,
        out_specs=pl.BlockSpec((1, tq, D_MODEL), lambda b, i: (b, i, 0)),
        out_shape=jax.ShapeDtypeStruct((batch, seq, D_MODEL), _F32),
        compiler_params=pltpu.CompilerParams(
            dimension_semantics=("arbitrary", "arbitrary"),
            vmem_limit_bytes=VMEM_LIMIT_BYTES),
        name="back_lat" if has_ctx else "back_ctx",
    )(*args)


def _swap_halves(w):
    half = QK_ROPE // 2
    return jnp.concatenate([-w[..., half:], w[..., :half]], axis=-1)


def _rope_tables(n_tokens, rotate):
    ones = jnp.ones((n_tokens, QK_ROPE), _F32)
    zeros = jnp.zeros((n_tokens, QK_ROPE), _F32)
    if rotate:
        rows = n_tokens // GRID_W
        inv = ROPE_THETA ** (-jnp.arange(AXIS_PAIRS, dtype=_F32) / AXIS_PAIRS)
        ang_r = jnp.arange(rows, dtype=_F32)[:, None] * inv
        ang_c = jnp.arange(GRID_W, dtype=_F32)[:, None] * inv

        def grid_table(fn):
            tr = jnp.broadcast_to(fn(ang_r)[:, None, :], (rows, GRID_W, AXIS_PAIRS))
            tc = jnp.broadcast_to(fn(ang_c)[None, :, :], (rows, GRID_W, AXIS_PAIRS))
            half = jnp.concatenate([tr, tc], axis=-1).reshape(n_tokens, 2 * AXIS_PAIRS)
            return jnp.concatenate([half, half], axis=-1)

        cos, sin = grid_table(jnp.cos), grid_table(jnp.sin)
    else:
        cos, sin = ones, zeros
    t1 = jnp.concatenate([cos, zeros], axis=-1)
    t2k = jnp.concatenate([sin, zeros], axis=-1)
    t2q = jnp.concatenate([sin, ones], axis=-1)
    return t1, t2k, t2q


def kernel(x_prompt, x_sample, cache_ckv, cache_krope, c, c_ctx, norm_g, w_ada, b_ada, w_in, w_s,
           b_s, g_v, q_norm_g, w_uq, kv_norm_g, w_ukv, w_o, final_g):
    depth = norm_g.shape[0]
    assert depth == 1 and w_in.shape[2] == IN_WIDTH
    dec_batch = x_sample.shape[0]
    xp, xs = x_prompt, x_sample
    new_ckv, new_kr = [], []
    for l in range(depth):
        cond = jnp.concatenate(
            [c, c_ctx[None, :], jnp.zeros((16 - dec_batch - 1, D_MODEL), _F32)], axis=0)
        mod = _mod_call(cond, w_ada[l], b_ada[l]).reshape(16, 1, 3 * D_MODEL)

        win2 = _prep_call(jnp.transpose(w_in[l]))
        wq3 = w_uq[l].reshape(Q_RANK, MLA_HEADS, QK_NOPE + QK_ROPE)
        wkv3 = w_ukv[l].reshape(KV_RANK, MLA_HEADS, QK_NOPE + V_HEAD)
        wq_nope = jnp.transpose(wq3[:, :, :QK_NOPE], (1, 0, 2))
        w_uk = jnp.transpose(wkv3[:, :, :QK_NOPE], (1, 0, 2))
        w_uv = jnp.transpose(wkv3[:, :, QK_NOPE:], (1, 0, 2))
        wq_abs = _fold_call(wq_nope, w_uk)
        wq_rope = jnp.transpose(wq3[:, :, QK_NOPE:], (1, 0, 2))
        wq2 = jnp.concatenate([wq_abs, wq_rope, _swap_halves(wq_rope)], axis=-1)
        wq2 = jnp.transpose(wq2, (1, 0, 2)).reshape(Q_RANK, MLA_HEADS * QK_WIDTH).astype(_BF16)
        zero_blk = jnp.zeros((KV_RANK, V_HEAD), _F32)
        wuv2 = jnp.stack([
            jnp.block([[w_uv[2 * p], zero_blk], [zero_blk, w_uv[2 * p + 1]]])
            for p in range(MLA_HEADS // 2)]).astype(_BF16)
        wo_bf = w_o[l].astype(_BF16)
        ws_bf = w_s[l].astype(_BF16)
        bs_t = b_s[l].T
        gv_row = g_v[l].reshape(1, A_WIDTH)
        shared = (norm_g[l].reshape(1, D_MODEL), win2, ws_bf, bs_t, gv_row,
                  q_norm_g[l].reshape(1, Q_RANK), wq2, kv_norm_g[l].reshape(1, KV_RANK))
        fg = final_g.reshape(1, D_MODEL)

        ctx_index = lambda b: dec_batch
        a_c, gb_c, q_c, k_c, ckv_c, kr_c = _front_call(
            xp, mod, ctx_index, *shared, _rope_tables(xp.shape[1], False), True)
        xp = _back_call(q_c, k_c, None, a_c, gb_c, xp, mod, ctx_index, wuv2, wo_bf, fg)
        new_ckv.append(ckv_c)
        new_kr.append(kr_c)

        lat_index = lambda b: b
        a_s, gb_s, q_s, k_s = _front_call(
            xs, mod, lat_index, *shared, _rope_tables(xs.shape[1], True), False)
        cache_k = jnp.concatenate(
            [cache_ckv[:, l], jnp.zeros(cache_krope[:, l].shape, _F32), cache_krope[:, l]],
            axis=-1).astype(_BF16)
        xs = _back_call(q_s, k_s, cache_k, a_s, gb_s, xs, mod, lat_index, wuv2, wo_bf, fg)
    return (xp, xs, jnp.concatenate(new_ckv, axis=1), jnp.concatenate(new_kr, axis=1))
```

```python
import functools
import math

import jax
import jax.numpy as jnp
from jax import lax
from jax.experimental import pallas as pl
from jax.experimental.pallas import tpu as pltpu

D_MODEL = 1024
GRID_W = 64
EPS = 1e-6
A_HEADS = 4
A_HEAD_DIM = 128
A_WIDTH = A_HEADS * A_HEAD_DIM
CHUNK = 128
MLA_HEADS = 4
QK_NOPE = 128
QK_ROPE = 64
V_HEAD = 128
B_WIDTH = MLA_HEADS * V_HEAD
Q_RANK = 256
KV_RANK = 128
AXIS_PAIRS = QK_ROPE // 4
ROPE_THETA = 10000.0
ATTN_SCALE = 1.0 / math.sqrt(QK_NOPE + QK_ROPE)
LOG2_E = 1.4426950408889634

QK_WIDTH = KV_RANK + 2 * QK_ROPE
OFF_U = 0
OFF_V = OFF_U + A_WIDTH
OFF_GA = OFF_V + A_WIDTH
OFF_CQ = OFF_GA + A_WIDTH
OFF_CKV = OFF_CQ + Q_RANK
OFF_KR = OFF_CKV + KV_RANK
OFF_GB = OFF_KR + 2 * QK_ROPE
IN_WIDTH = OFF_GB - QK_ROPE + B_WIDTH
IN_WIDTH2 = OFF_GB + B_WIDTH

LANES = 128
TOKEN_TILE = 256
PROJ_BLOCK = 256
VMEM_LIMIT_BYTES = 48 * 1024 * 1024

_F32 = jnp.float32
_BF16 = jnp.bfloat16
_NT_DIMS = (((1,), (1,)), ((), ()))


def _silu(x):
    hx = 0.5 * x
    return hx + hx * jnp.tanh(hx)


def _gelu_tanh(x):
    return x * (0.5 * (1.0 + jnp.tanh(0.7978845608028654 * (x + 0.044715 * (x * x * x)))))


def _rmsnorm(x, g):
    ms = jnp.mean(x * x, axis=-1, keepdims=True)
    return (x * lax.rsqrt(ms + EPS)) * g


def _mod_body(cond_ref, w_ref, b_ref, o_ref):
    s = _silu(cond_ref[...])
    o_ref[...] = jnp.dot(s.astype(_BF16), w_ref[...].astype(_BF16),
                         preferred_element_type=_F32) + b_ref[...]


def _mod_call(cond, w_ada, b_ada):
    rows = cond.shape[0]
    n_out = w_ada.shape[1]
    col_block = 512
    return pl.pallas_call(
        _mod_body,
        grid=(n_out // col_block,),
        in_specs=[
            pl.BlockSpec((rows, D_MODEL), lambda j: (0, 0)),
            pl.BlockSpec((D_MODEL, col_block), lambda j: (0, j)),
            pl.BlockSpec((1, col_block), lambda j: (0, j)),
        ],
        out_specs=pl.BlockSpec((rows, col_block), lambda j: (0, j)),
        out_shape=jax.ShapeDtypeStruct((rows, n_out), _F32),
        name="adaln_mod",
    )(cond, w_ada, b_ada.reshape(1, n_out))


def _fold_body(wq_ref, wk_ref, o_ref):
    o_ref[0] = lax.dot_general(wq_ref[0], wk_ref[0], _NT_DIMS,
                               precision=lax.Precision.HIGHEST,
                               preferred_element_type=_F32)


def _fold_call(wq_nope, w_uk):
    return pl.pallas_call(
        _fold_body,
        grid=(MLA_HEADS,),
        in_specs=[
            pl.BlockSpec((1, Q_RANK, QK_NOPE), lambda h: (h, 0, 0)),
            pl.BlockSpec((1, KV_RANK, QK_NOPE), lambda h: (h, 0, 0)),
        ],
        out_specs=pl.BlockSpec((1, Q_RANK, KV_RANK), lambda h: (h, 0, 0)),
        out_shape=jax.ShapeDtypeStruct((MLA_HEADS, Q_RANK, KV_RANK), _F32),
        name="fold_q_uk",
    )(wq_nope, w_uk)


_KR_BLOCK = OFF_KR // LANES
_KR_HALF_ROW = OFF_KR // QK_ROPE


def _prep_body(a_ref, b_ref, o_ref):
    a = a_ref[...]
    b = b_ref[...]
    half = QK_ROPE // 2
    partner = jnp.concatenate([-b[half:, :], b[:half, :]], axis=0)
    is_kr = pl.program_id(0) == _KR_BLOCK
    b = jnp.where(is_kr, partner, b)
    o_ref[...] = jnp.concatenate([a, b], axis=0).T.astype(_BF16)


def _prep_call(w_in_t):
    def ra(j):
        return (jnp.where(j <= _KR_BLOCK, 2 * j, 2 * j - 1), 0)

    def rb(j):
        return (jnp.where(j < _KR_BLOCK, 2 * j + 1, 2 * j), 0)

    assert _KR_HALF_ROW == 2 * _KR_BLOCK
    return pl.pallas_call(
        _prep_body,
        grid=(IN_WIDTH2 // LANES,),
        in_specs=[
            pl.BlockSpec((QK_ROPE, D_MODEL), ra),
            pl.BlockSpec((QK_ROPE, D_MODEL), rb),
        ],
        out_specs=pl.BlockSpec((D_MODEL, LANES), lambda j: (0, j)),
        out_shape=jax.ShapeDtypeStruct((D_MODEL, IN_WIDTH2), _BF16),
        name="prep_w_in",
    )(w_in_t, w_in_t)


def _front_step(x_ref, mod_ref, ng_ref, win_ref, ws_ref, bs_ref, gv_ref, qg_ref, wq_ref, kvg_ref,
                t1_ref, t2k_ref, t2q_ref, out_refs, h_write, h_read, z_write, z_read, emit_cache):
    if emit_cache:
        a_ref, gb_ref, q_ref, k_ref, v_ref, ckv_ref, kr_ref = out_refs
    else:
        a_ref, gb_ref, q_ref, k_ref, v_ref = out_refs
    tile = z_read.shape[0]
    half = tile // 2
    assert IN_WIDTH2 == 10 * PROJ_BLOCK

    def project(j):
        lo = j * PROJ_BLOCK
        z_write[:, lo:lo + PROJ_BLOCK] = jnp.dot(h_read[...], win_ref[:, lo:lo + PROJ_BLOCK],
                                                 preferred_element_type=_F32)

    shift = mod_ref[0, :, 0:D_MODEL]
    gain = ng_ref[...] * (1.0 + mod_ref[0, :, D_MODEL:2 * D_MODEL])

    def norm_rows(lo):
        x = x_ref[0, lo:lo + half, :]
        ms = jnp.mean(x * x, axis=-1, keepdims=True)
        h_write[lo:lo + half, :] = ((x * lax.rsqrt(ms + EPS)) * gain + shift).astype(_BF16)

    def vn_head(hd):
        lo = hd * A_HEAD_DIM
        v = _gelu_tanh(z_read[:, OFF_V + lo:OFF_V + lo + A_HEAD_DIM])
        return _rmsnorm(v, gv_ref[:, lo:lo + A_HEAD_DIM]).astype(_BF16)

    def mix_head(hd, vn):
        bias = bs_ref[:, hd:hd + 1]
        rows = [jnp.dot(ws_ref[hd], vn[c * CHUNK:(c + 1) * CHUNK, :],
                        preferred_element_type=_F32) + bias for c in range(tile // CHUNK)]
        return jnp.concatenate(rows, axis=0)

    def gate_a(hd, mixed):
        lo = hd * A_HEAD_DIM
        u = _gelu_tanh(z_read[:, OFF_U + lo:OFF_U + lo + A_HEAD_DIM])
        g = _silu(z_read[:, OFF_GA + lo:OFF_GA + lo + A_HEAD_DIM])
        a_ref[0, :, lo:lo + A_HEAD_DIM] = (u * mixed * g).astype(_BF16)

    def gate_b(lo, width):
        gb_ref[0, :, lo:lo + width] = _silu(z_read[:, OFF_GB + lo:OFF_GB + lo + width])

    t1 = t1_ref[...]

    def rope_q(qa, hd):
        base = hd * QK_WIDTH
        pair = qa[:, base + KV_RANK:base + QK_WIDTH]
        rope = pair * t1 + pltpu.roll(pair, QK_ROPE, 1) * t2q_ref[...]
        q_ref[0, hd] = jnp.concatenate([qa[:, base:base + KV_RANK], rope], axis=1).astype(_BF16)

    project(0)
    norm_rows(0)
    project(1)
    norm_rows(half)
    project(2)

    cq = _rmsnorm(z_read[:, OFF_CQ:OFF_CQ + Q_RANK],
                  qg_ref[...] * (ATTN_SCALE * LOG2_E)).astype(_BF16)
    ckv = _rmsnorm(z_read[:, OFF_CKV:OFF_CKV + KV_RANK], kvg_ref[...])
    kpair = z_read[:, OFF_KR:OFF_KR + 2 * QK_ROPE]
    krot = kpair * t1 + pltpu.roll(kpair, QK_ROPE, 1) * t2k_ref[...]
    k_ref[0] = jnp.concatenate([ckv, krot], axis=1).astype(_BF16)
    v_ref[0] = jnp.concatenate([ckv, jnp.ones_like(ckv)], axis=1).astype(_BF16)
    if emit_cache:
        ckv_ref[0, 0] = ckv
        kr_ref[0, 0] = kpair[:, 0:QK_ROPE]
    qa = jnp.dot(cq, wq_ref[...], preferred_element_type=_F32)

    vn0 = vn_head(0)
    project(3)
    vn1 = vn_head(1)
    mixed0 = mix_head(0, vn0)
    vn2 = vn_head(2)
    project(4)
    vn3 = vn_head(3)
    mixed1 = mix_head(1, vn1)
    rope_q(qa, 0)
    rope_q(qa, 1)
    project(5)
    rope_q(qa, 2)
    rope_q(qa, 3)
    mixed2 = mix_head(2, vn2)
    gate_a(0, mixed0)
    project(6)
    gate_a(1, mixed1)
    mixed3 = mix_head(3, vn3)
    gate_a(2, mixed2)
    project(7)
    gate_a(3, mixed3)
    project(8)
    gate_b(0, B_WIDTH // 2)
    project(9)
    gate_b(B_WIDTH // 2, B_WIDTH // 2)


def _front_body(x_ref, mod_ref, ng_ref, win_ref, ws_ref, bs_ref, gv_ref, qg_ref, wq_ref,
                kvg_ref, t1_ref, t2k_ref, t2q_ref, *rest, emit_cache):
    out_refs, (h_even, h_odd, z_even, z_odd) = rest[:-4], rest[-4:]
    t = pl.program_id(0)

    @pl.when(t == 0)
    def _():
        h_odd[...] = jnp.zeros(h_odd.shape, _BF16)
        z_odd[...] = jnp.zeros(z_odd.shape, _F32)

    def step(h_write, h_read, z_write, z_read):
        _front_step(x_ref, mod_ref, ng_ref, win_ref, ws_ref, bs_ref, gv_ref, qg_ref, wq_ref,
                    kvg_ref, t1_ref, t2k_ref, t2q_ref, out_refs, h_write, h_read, z_write, z_read,
                    emit_cache)

    @pl.when(t % 2 == 0)
    def _():
        step(h_even, h_odd, z_even, z_odd)

    @pl.when(t % 2 == 1)
    def _():
        step(h_odd, h_even, z_odd, z_even)


def _front_call(x, mod, mod_index, norm_g, win2, ws_bf, bs_t, g_v, q_norm_g, wq2, kv_norm_g,
                tables, emit_cache):
    batch, seq, _ = x.shape
    tile = TOKEN_TILE
    tiles_per_row = seq // tile
    n_tiles = batch * tiles_per_row
    t1, t2k, t2q = tables

    def proj_tile(t):
        tt = jnp.minimum(t, n_tiles - 1)
        return tt // tiles_per_row, tt % tiles_per_row

    def fin_tile(t):
        tt = jnp.maximum(t - 2, 0)
        return tt // tiles_per_row, tt % tiles_per_row

    full = lambda shape: pl.BlockSpec(shape, lambda t: (0,) * len(shape))
    tab = pl.BlockSpec((tile, 2 * QK_ROPE), lambda t: (fin_tile(t)[1], 0))
    in_specs = [
        pl.BlockSpec((1, tile, D_MODEL), lambda t: (*proj_tile(t), 0)),
        pl.BlockSpec((1, 1, 3 * D_MODEL), lambda t: (mod_index(proj_tile(t)[0]), 0, 0)),
        full((1, D_MODEL)),
        full((D_MODEL, IN_WIDTH2)),
        full((A_HEADS, CHUNK, CHUNK)),
        full((CHUNK, A_HEADS)),
        full((1, A_WIDTH)),
        full((1, Q_RANK)),
        full((Q_RANK, MLA_HEADS * QK_WIDTH)),
        full((1, KV_RANK)),
        tab, tab, tab,
    ]
    out_shape = [
        jax.ShapeDtypeStruct((batch, seq, A_WIDTH), _BF16),
        jax.ShapeDtypeStruct((batch, seq, B_WIDTH), _F32),
        jax.ShapeDtypeStruct((batch, MLA_HEADS, seq, QK_WIDTH), _BF16),
        jax.ShapeDtypeStruct((batch, seq, QK_WIDTH), _BF16),
        jax.ShapeDtypeStruct((batch, seq, 2 * KV_RANK), _BF16),
    ]
    out_specs = [
        pl.BlockSpec((1, tile, A_WIDTH), lambda t: (*fin_tile(t), 0)),
        pl.BlockSpec((1, tile, B_WIDTH), lambda t: (*fin_tile(t), 0)),
        pl.BlockSpec((1, MLA_HEADS, tile, QK_WIDTH),
                     lambda t: (fin_tile(t)[0], 0, fin_tile(t)[1], 0)),
        pl.BlockSpec((1, tile, QK_WIDTH), lambda t: (*fin_tile(t), 0)),
        pl.BlockSpec((1, tile, 2 * KV_RANK), lambda t: (*fin_tile(t), 0)),
    ]
    if emit_cache:
        out_shape += [
            jax.ShapeDtypeStruct((batch, 1, seq, KV_RANK), _F32),
            jax.ShapeDtypeStruct((batch, 1, seq, QK_ROPE), _F32),
        ]
        out_specs += [
            pl.BlockSpec((1, 1, tile, KV_RANK), lambda t: (fin_tile(t)[0], 0, fin_tile(t)[1], 0)),
            pl.BlockSpec((1, 1, tile, QK_ROPE), lambda t: (fin_tile(t)[0], 0, fin_tile(t)[1], 0)),
        ]
    return pl.pallas_call(
        functools.partial(_front_body, emit_cache=emit_cache),
        grid=(n_tiles + 2,),
        in_specs=in_specs,
        out_specs=out_specs,
        out_shape=out_shape,
        scratch_shapes=[pltpu.VMEM((tile, D_MODEL), _BF16), pltpu.VMEM((tile, D_MODEL), _BF16),
                        pltpu.VMEM((tile, IN_WIDTH2), _F32), pltpu.VMEM((tile, IN_WIDTH2), _F32)],
        compiler_params=pltpu.CompilerParams(
            dimension_semantics=("arbitrary",),
            vmem_limit_bytes=VMEM_LIMIT_BYTES),
        name="front_ctx" if emit_cache else "front_lat",
    )(x, mod, norm_g, win2, ws_bf, bs_t, g_v, q_norm_g, wq2, kv_norm_g, t1, t2k, t2q)


def _scores(q, k_refs):
    return [lax.dot_general(q, k_ref[0], _NT_DIMS, preferred_element_type=_F32)
            for k_ref in k_refs]


def _softmax_values(scores, v_refs):
    m = functools.reduce(jnp.maximum, [jnp.max(s, axis=1, keepdims=True) for s in scores])
    acc = None
    for s, v_ref in zip(scores, v_refs):
        p = jnp.exp2(s - m).astype(_BF16)
        part = jnp.dot(p, v_ref[0], preferred_element_type=_F32)
        acc = part if acc is None else acc + part
    return acc[:, 0:KV_RANK] / acc[:, KV_RANK:2 * KV_RANK]


def _back_body(*refs, has_ctx):
    if has_ctx:
        (q_ref, klat_ref, vlat_ref, kctx_ref, vctx_ref, a_ref, gb_ref, x_ref, mod_ref, wuv_ref,
         wo_ref, fg_ref, y_ref) = refs
        k_refs, v_refs = (kctx_ref, klat_ref), (vctx_ref, vlat_ref)
    else:
        (q_ref, klat_ref, vlat_ref, a_ref, gb_ref, x_ref, mod_ref, wuv_ref, wo_ref, fg_ref,
         y_ref) = refs
        k_refs, v_refs = (klat_ref,), (vlat_ref,)
    tq = q_ref.shape[2]

    groups = [q_ref[0, 2 * g:2 * g + 2].reshape(2 * tq, QK_WIDTH) for g in range(MLA_HEADS // 2)]
    scores = [_scores(qg, k_refs) for qg in groups]
    outs = [_softmax_values(s, v_refs) for s in scores]

    attn_cols = []
    for pair, o in enumerate(outs):
        o2 = jnp.concatenate([o[0:tq, :], o[tq:2 * tq, :]], axis=1)
        attn_cols.append(jnp.dot(o2.astype(_BF16), wuv_ref[pair], preferred_element_type=_F32))
    attn = jnp.concatenate(attn_cols, axis=1) * gb_ref[0]

    y = jnp.dot(a_ref[0], wo_ref[0:A_WIDTH, :], preferred_element_type=_F32)
    y = y + jnp.dot(attn.astype(_BF16), wo_ref[A_WIDTH:A_WIDTH + B_WIDTH, :],
                    preferred_element_type=_F32)
    gate = mod_ref[0, :, 2 * D_MODEL:3 * D_MODEL]
    out = x_ref[0] + gate * y
    y_ref[0] = _rmsnorm(out, fg_ref[...])


def _back_call(q, klat, vlat, kctx, vctx, a_out, gb, x, mod, mod_index, wuv2, wo_bf, final_g):
    batch, seq, _ = x.shape
    tq = TOKEN_TILE
    has_ctx = kctx is not None
    full = lambda shape: pl.BlockSpec(shape, lambda b, i: (0,) * len(shape))
    keys = lambda arr: pl.BlockSpec((1, arr.shape[1], arr.shape[2]), lambda b, i: (b, 0, 0))
    in_specs = [
        pl.BlockSpec((1, MLA_HEADS, tq, QK_WIDTH), lambda b, i: (b, 0, i, 0)),
        keys(klat), keys(vlat),
    ]
    args = [q, klat, vlat]
    if has_ctx:
        in_specs += [keys(kctx), keys(vctx)]
        args += [kctx, vctx]
    in_specs += [
        pl.BlockSpec((1, tq, A_WIDTH), lambda b, i: (b, i, 0)),
        pl.BlockSpec((1, tq, B_WIDTH), lambda b, i: (b, i, 0)),
        pl.BlockSpec((1, tq, D_MODEL), lambda b, i: (b, i, 0)),
        pl.BlockSpec((1, 1, 3 * D_MODEL), lambda b, i: (mod_index(b), 0, 0)),
        full((MLA_HEADS // 2, 2 * KV_RANK, 2 * V_HEAD)),
        full((D_MODEL, D_MODEL)),
        full((1, D_MODEL)),
    ]
    args += [a_out, gb, x, mod, wuv2, wo_bf, final_g]
    return pl.pallas_call(
        functools.partial(_back_body, has_ctx=has_ctx),
        grid=(batch, seq // tq),
        in_specs=in_specs,
        out_specs=pl.BlockSpec((1, tq, D_MODEL), lambda b, i: (b, i, 0)),
        out_shape=jax.ShapeDtypeStruct((batch, seq, D_MODEL), _F32),
        compiler_params=pltpu.CompilerParams(
            dimension_semantics=("arbitrary", "arbitrary"),
            vmem_limit_bytes=VMEM_LIMIT_BYTES),
        name="back_lat" if has_ctx else "back_ctx",
    )(*args)


def _swap_halves(w):
    half = QK_ROPE // 2
    return jnp.concatenate([-w[..., half:], w[..., :half]], axis=-1)


def _rope_tables(n_tokens, rotate):
    ones = jnp.ones((n_tokens, QK_ROPE), _F32)
    zeros = jnp.zeros((n_tokens, QK_ROPE), _F32)
    if rotate:
        rows = n_tokens // GRID_W
        inv = ROPE_THETA ** (-jnp.arange(AXIS_PAIRS, dtype=_F32) / AXIS_PAIRS)
        ang_r = jnp.arange(rows, dtype=_F32)[:, None] * inv
        ang_c = jnp.arange(GRID_W, dtype=_F32)[:, None] * inv

        def grid_table(fn):
            tr = jnp.broadcast_to(fn(ang_r)[:, None, :], (rows, GRID_W, AXIS_PAIRS))
            tc = jnp.broadcast_to(fn(ang_c)[None, :, :], (rows, GRID_W, AXIS_PAIRS))
            half = jnp.concatenate([tr, tc], axis=-1).reshape(n_tokens, 2 * AXIS_PAIRS)
            return jnp.concatenate([half, half], axis=-1)

        cos, sin = grid_table(jnp.cos), grid_table(jnp.sin)
    else:
        cos, sin = ones, zeros
    t1 = jnp.concatenate([cos, zeros], axis=-1)
    t2k = jnp.concatenate([sin, zeros], axis=-1)
    t2q = jnp.concatenate([sin, ones], axis=-1)
    return t1, t2k, t2q


def kernel(x_prompt, x_sample, cache_ckv, cache_krope, c, c_ctx, norm_g, w_ada, b_ada, w_in, w_s,
           b_s, g_v, q_norm_g, w_uq, kv_norm_g, w_ukv, w_o, final_g):
    depth = norm_g.shape[0]
    assert depth == 1 and w_in.shape[2] == IN_WIDTH
    dec_batch = x_sample.shape[0]
    xp, xs = x_prompt, x_sample
    new_ckv, new_kr = [], []
    for l in range(depth):
        cond = jnp.concatenate(
            [c, c_ctx[None, :], jnp.zeros((16 - dec_batch - 1, D_MODEL), _F32)], axis=0)
        mod = _mod_call(cond, w_ada[l], b_ada[l]).reshape(16, 1, 3 * D_MODEL)

        win2 = _prep_call(jnp.transpose(w_in[l]))
        wq3 = w_uq[l].reshape(Q_RANK, MLA_HEADS, QK_NOPE + QK_ROPE)
        wkv3 = w_ukv[l].reshape(KV_RANK, MLA_HEADS, QK_NOPE + V_HEAD)
        wq_nope = jnp.transpose(wq3[:, :, :QK_NOPE], (1, 0, 2))
        w_uk = jnp.transpose(wkv3[:, :, :QK_NOPE], (1, 0, 2))
        w_uv = jnp.transpose(wkv3[:, :, QK_NOPE:], (1, 0, 2))
        wq_abs = _fold_call(wq_nope, w_uk)
        wq_rope = jnp.transpose(wq3[:, :, QK_NOPE:], (1, 0, 2))
        wq2 = jnp.concatenate([wq_abs, wq_rope, _swap_halves(wq_rope)], axis=-1)
        wq2 = jnp.transpose(wq2, (1, 0, 2)).reshape(Q_RANK, MLA_HEADS * QK_WIDTH).astype(_BF16)
        zero_blk = jnp.zeros((KV_RANK, V_HEAD), _F32)
        wuv2 = jnp.stack([
            jnp.block([[w_uv[2 * p], zero_blk], [zero_blk, w_uv[2 * p + 1]]])
            for p in range(MLA_HEADS // 2)]).astype(_BF16)
        wo_bf = w_o[l].astype(_BF16)
        ws_bf = w_s[l].astype(_BF16)
        bs_t = b_s[l].T
        gv_row = g_v[l].reshape(1, A_WIDTH)
        shared = (norm_g[l].reshape(1, D_MODEL), win2, ws_bf, bs_t, gv_row,
                  q_norm_g[l].reshape(1, Q_RANK), wq2, kv_norm_g[l].reshape(1, KV_RANK))
        fg = final_g.reshape(1, D_MODEL)

        ctx_index = lambda b: dec_batch
        a_c, gb_c, q_c, k_c, v_c, ckv_c, kr_c = _front_call(
            xp, mod, ctx_index, *shared, _rope_tables(xp.shape[1], False), True)
        xp = _back_call(q_c, k_c, v_c, None, None, a_c, gb_c, xp, mod, ctx_index, wuv2, wo_bf, fg)
        new_ckv.append(ckv_c)
        new_kr.append(kr_c)

        lat_index = lambda b: b
        a_s, gb_s, q_s, k_s, v_s = _front_call(
            xs, mod, lat_index, *shared, _rope_tables(xs.shape[1], True), False)
        cache_k = jnp.concatenate(
            [cache_ckv[:, l], jnp.zeros(cache_krope[:, l].shape, _F32), cache_krope[:, l]],
            axis=-1).astype(_BF16)
        cache_v = jnp.concatenate(
            [cache_ckv[:, l], jnp.ones(cache_ckv[:, l].shape, _F32)], axis=-1).astype(_BF16)
        xs = _back_call(q_s, k_s, v_s, cache_k, cache_v, a_s, gb_s, xs, mod, lat_index, wuv2,
                        wo_bf, fg)
    return (xp, xs, jnp.concatenate(new_ckv, axis=1), jnp.concatenate(new_kr, axis=1))
```

```python
import functools
import math

import jax
import jax.numpy as jnp
from jax import lax
from jax.experimental import pallas as pl
from jax.experimental.pallas import tpu as pltpu

D_MODEL = 1024
GRID_W = 64
EPS = 1e-6
A_HEADS = 4
A_HEAD_DIM = 128
A_WIDTH = A_HEADS * A_HEAD_DIM
CHUNK = 128
MLA_HEADS = 4
QK_NOPE = 128
QK_ROPE = 64
V_HEAD = 128
B_WIDTH = MLA_HEADS * V_HEAD
Q_RANK = 256
KV_RANK = 128
AXIS_PAIRS = QK_ROPE // 4
ROPE_THETA = 10000.0
ATTN_SCALE = 1.0 / math.sqrt(QK_NOPE + QK_ROPE)
LOG2_E = 1.4426950408889634

QK_WIDTH = KV_RANK + 2 * QK_ROPE
OFF_U = 0
OFF_V = OFF_U + A_WIDTH
OFF_GA = OFF_V + A_WIDTH
OFF_CQ = OFF_GA + A_WIDTH
OFF_CKV = OFF_CQ + Q_RANK
OFF_KR = OFF_CKV + KV_RANK
OFF_GB = OFF_KR + 2 * QK_ROPE
IN_WIDTH = OFF_GB - QK_ROPE + B_WIDTH
IN_WIDTH2 = OFF_GB + B_WIDTH

LANES = 128
TOKEN_TILE = 512
PROJ_BLOCK = 256
VMEM_LIMIT_BYTES = 56 * 1024 * 1024

_F32 = jnp.float32
_BF16 = jnp.bfloat16
_NT_DIMS = (((1,), (1,)), ((), ()))


def _silu(x):
    hx = 0.5 * x
    return hx + hx * jnp.tanh(hx)


def _gelu_tanh(x):
    return x * (0.5 * (1.0 + jnp.tanh(0.7978845608028654 * (x + 0.044715 * (x * x * x)))))


def _rmsnorm(x, g):
    ms = jnp.mean(x * x, axis=-1, keepdims=True)
    return (x * lax.rsqrt(ms + EPS)) * g


def _mod_body(cond_ref, w_ref, b_ref, o_ref):
    s = _silu(cond_ref[...])
    o_ref[...] = jnp.dot(s.astype(_BF16), w_ref[...].astype(_BF16),
                         preferred_element_type=_F32) + b_ref[...]


def _mod_call(cond, w_ada, b_ada):
    rows = cond.shape[0]
    n_out = w_ada.shape[1]
    col_block = 512
    return pl.pallas_call(
        _mod_body,
        grid=(n_out // col_block,),
        in_specs=[
            pl.BlockSpec((rows, D_MODEL), lambda j: (0, 0)),
            pl.BlockSpec((D_MODEL, col_block), lambda j: (0, j)),
            pl.BlockSpec((1, col_block), lambda j: (0, j)),
        ],
        out_specs=pl.BlockSpec((rows, col_block), lambda j: (0, j)),
        out_shape=jax.ShapeDtypeStruct((rows, n_out), _F32),
        name="adaln_mod",
    )(cond, w_ada, b_ada.reshape(1, n_out))


def _fold_body(wq_ref, wk_ref, o_ref):
    o_ref[0] = lax.dot_general(wq_ref[0], wk_ref[0], _NT_DIMS,
                               precision=lax.Precision.HIGHEST,
                               preferred_element_type=_F32)


def _fold_call(wq_nope, w_uk):
    return pl.pallas_call(
        _fold_body,
        grid=(MLA_HEADS,),
        in_specs=[
            pl.BlockSpec((1, Q_RANK, QK_NOPE), lambda h: (h, 0, 0)),
            pl.BlockSpec((1, KV_RANK, QK_NOPE), lambda h: (h, 0, 0)),
        ],
        out_specs=pl.BlockSpec((1, Q_RANK, KV_RANK), lambda h: (h, 0, 0)),
        out_shape=jax.ShapeDtypeStruct((MLA_HEADS, Q_RANK, KV_RANK), _F32),
        name="fold_q_uk",
    )(wq_nope, w_uk)


_KR_BLOCK = OFF_KR // LANES
_KR_HALF_ROW = OFF_KR // QK_ROPE


def _prep_body(a_ref, b_ref, o_ref):
    a = a_ref[...]
    b = b_ref[...]
    half = QK_ROPE // 2
    partner = jnp.concatenate([-b[half:, :], b[:half, :]], axis=0)
    is_kr = pl.program_id(0) == _KR_BLOCK
    b = jnp.where(is_kr, partner, b)
    o_ref[...] = jnp.concatenate([a, b], axis=0).T.astype(_BF16)


def _prep_call(w_in_t):
    def ra(j):
        return (jnp.where(j <= _KR_BLOCK, 2 * j, 2 * j - 1), 0)

    def rb(j):
        return (jnp.where(j < _KR_BLOCK, 2 * j + 1, 2 * j), 0)

    assert _KR_HALF_ROW == 2 * _KR_BLOCK
    return pl.pallas_call(
        _prep_body,
        grid=(IN_WIDTH2 // LANES,),
        in_specs=[
            pl.BlockSpec((QK_ROPE, D_MODEL), ra),
            pl.BlockSpec((QK_ROPE, D_MODEL), rb),
        ],
        out_specs=pl.BlockSpec((D_MODEL, LANES), lambda j: (0, j)),
        out_shape=jax.ShapeDtypeStruct((D_MODEL, IN_WIDTH2), _BF16),
        name="prep_w_in",
    )(w_in_t, w_in_t)


def _front_step(x_ref, mod_ref, ng_ref, win_ref, ws_ref, bs_ref, gv_ref, qg_ref, wq_ref, kvg_ref,
                t1_ref, t2k_ref, t2q_ref, out_refs, h_write, h_read, z_write, z_read, emit_cache):
    if emit_cache:
        a_ref, gb_ref, q_ref, k_ref, v_ref, ckv_ref, kr_ref = out_refs
    else:
        a_ref, gb_ref, q_ref, k_ref, v_ref = out_refs
    tile = z_read.shape[0]
    half = tile // 2
    assert IN_WIDTH2 == 10 * PROJ_BLOCK

    def project(j):
        lo = j * PROJ_BLOCK
        z_write[:, lo:lo + PROJ_BLOCK] = jnp.dot(h_read[...], win_ref[:, lo:lo + PROJ_BLOCK],
                                                 preferred_element_type=_F32)

    shift = mod_ref[0, :, 0:D_MODEL]
    gain = ng_ref[...] * (1.0 + mod_ref[0, :, D_MODEL:2 * D_MODEL])

    def norm_rows(lo):
        x = x_ref[0, lo:lo + half, :]
        ms = jnp.mean(x * x, axis=-1, keepdims=True)
        h_write[lo:lo + half, :] = ((x * lax.rsqrt(ms + EPS)) * gain + shift).astype(_BF16)

    def vn_head(hd):
        lo = hd * A_HEAD_DIM
        v = _gelu_tanh(z_read[:, OFF_V + lo:OFF_V + lo + A_HEAD_DIM])
        return _rmsnorm(v, gv_ref[:, lo:lo + A_HEAD_DIM]).astype(_BF16)

    def mix_head(hd, vn):
        bias = bs_ref[:, hd:hd + 1]
        rows = [jnp.dot(ws_ref[hd], vn[c * CHUNK:(c + 1) * CHUNK, :],
                        preferred_element_type=_F32) + bias for c in range(tile // CHUNK)]
        return jnp.concatenate(rows, axis=0)

    def gate_a(hd, mixed):
        lo = hd * A_HEAD_DIM
        u = _gelu_tanh(z_read[:, OFF_U + lo:OFF_U + lo + A_HEAD_DIM])
        g = _silu(z_read[:, OFF_GA + lo:OFF_GA + lo + A_HEAD_DIM])
        a_ref[0, :, lo:lo + A_HEAD_DIM] = (u * mixed * g).astype(_BF16)

    def gate_b(lo, width):
        gb_ref[0, :, lo:lo + width] = _silu(z_read[:, OFF_GB + lo:OFF_GB + lo + width])

    t1 = t1_ref[...]

    def rope_q(qa, hd):
        base = hd * QK_WIDTH
        pair = qa[:, base + KV_RANK:base + QK_WIDTH]
        rope = pair * t1 + pltpu.roll(pair, QK_ROPE, 1) * t2q_ref[...]
        q_ref[0, hd] = jnp.concatenate([qa[:, base:base + KV_RANK], rope], axis=1).astype(_BF16)

    project(0)
    norm_rows(0)
    project(1)
    norm_rows(half)
    project(2)

    cq = _rmsnorm(z_read[:, OFF_CQ:OFF_CQ + Q_RANK],
                  qg_ref[...] * (ATTN_SCALE * LOG2_E)).astype(_BF16)
    ckv = _rmsnorm(z_read[:, OFF_CKV:OFF_CKV + KV_RANK], kvg_ref[...])
    kpair = z_read[:, OFF_KR:OFF_KR + 2 * QK_ROPE]
    krot = kpair * t1 + pltpu.roll(kpair, QK_ROPE, 1) * t2k_ref[...]
    k_ref[0] = jnp.concatenate([ckv, krot], axis=1).astype(_BF16)
    v_ref[0] = jnp.concatenate([ckv, jnp.ones_like(ckv)], axis=1).astype(_BF16)
    if emit_cache:
        ckv_ref[0, 0] = ckv
        kr_ref[0, 0] = kpair[:, 0:QK_ROPE]
    qa = jnp.dot(cq, wq_ref[...], preferred_element_type=_F32)

    vn0 = vn_head(0)
    project(3)
    vn1 = vn_head(1)
    mixed0 = mix_head(0, vn0)
    vn2 = vn_head(2)
    project(4)
    vn3 = vn_head(3)
    mixed1 = mix_head(1, vn1)
    rope_q(qa, 0)
    rope_q(qa, 1)
    project(5)
    rope_q(qa, 2)
    rope_q(qa, 3)
    mixed2 = mix_head(2, vn2)
    gate_a(0, mixed0)
    project(6)
    gate_a(1, mixed1)
    mixed3 = mix_head(3, vn3)
    gate_a(2, mixed2)
    project(7)
    gate_a(3, mixed3)
    project(8)
    gate_b(0, B_WIDTH // 2)
    project(9)
    gate_b(B_WIDTH // 2, B_WIDTH // 2)


def _front_body(x_ref, mod_ref, ng_ref, win_ref, ws_ref, bs_ref, gv_ref, qg_ref, wq_ref,
                kvg_ref, t1_ref, t2k_ref, t2q_ref, *rest, emit_cache):
    out_refs, (h_even, h_odd, z_even, z_odd) = rest[:-4], rest[-4:]
    t = pl.program_id(0)

    @pl.when(t == 0)
    def _():
        h_odd[...] = jnp.zeros(h_odd.shape, _BF16)
        z_odd[...] = jnp.zeros(z_odd.shape, _F32)

    def step(h_write, h_read, z_write, z_read):
        _front_step(x_ref, mod_ref, ng_ref, win_ref, ws_ref, bs_ref, gv_ref, qg_ref, wq_ref,
                    kvg_ref, t1_ref, t2k_ref, t2q_ref, out_refs, h_write, h_read, z_write, z_read,
                    emit_cache)

    @pl.when(t % 2 == 0)
    def _():
        step(h_even, h_odd, z_even, z_odd)

    @pl.when(t % 2 == 1)
    def _():
        step(h_odd, h_even, z_odd, z_even)


def _front_call(x, mod, mod_index, norm_g, win2, ws_bf, bs_t, g_v, q_norm_g, wq2, kv_norm_g,
                tables, emit_cache):
    batch, seq, _ = x.shape
    tile = min(TOKEN_TILE, seq)
    tiles_per_row = seq // tile
    n_tiles = batch * tiles_per_row
    t1, t2k, t2q = tables

    def proj_tile(t):
        tt = jnp.minimum(t, n_tiles - 1)
        return tt // tiles_per_row, tt % tiles_per_row

    def fin_tile(t):
        tt = jnp.maximum(t - 2, 0)
        return tt // tiles_per_row, tt % tiles_per_row

    full = lambda shape: pl.BlockSpec(shape, lambda t: (0,) * len(shape))
    tab = pl.BlockSpec((tile, 2 * QK_ROPE), lambda t: (fin_tile(t)[1], 0))
    in_specs = [
        pl.BlockSpec((1, tile, D_MODEL), lambda t: (*proj_tile(t), 0)),
        pl.BlockSpec((1, 1, 3 * D_MODEL), lambda t: (mod_index(proj_tile(t)[0]), 0, 0)),
        full((1, D_MODEL)),
        full((D_MODEL, IN_WIDTH2)),
        full((A_HEADS, CHUNK, CHUNK)),
        full((CHUNK, A_HEADS)),
        full((1, A_WIDTH)),
        full((1, Q_RANK)),
        full((Q_RANK, MLA_HEADS * QK_WIDTH)),
        full((1, KV_RANK)),
        tab, tab, tab,
    ]
    out_shape = [
        jax.ShapeDtypeStruct((batch, seq, A_WIDTH), _BF16),
        jax.ShapeDtypeStruct((batch, seq, B_WIDTH), _F32),
        jax.ShapeDtypeStruct((batch, MLA_HEADS, seq, QK_WIDTH), _BF16),
        jax.ShapeDtypeStruct((batch, seq, QK_WIDTH), _BF16),
        jax.ShapeDtypeStruct((batch, seq, 2 * KV_RANK), _BF16),
    ]
    out_specs = [
        pl.BlockSpec((1, tile, A_WIDTH), lambda t: (*fin_tile(t), 0)),
        pl.BlockSpec((1, tile, B_WIDTH), lambda t: (*fin_tile(t), 0)),
        pl.BlockSpec((1, MLA_HEADS, tile, QK_WIDTH),
                     lambda t: (fin_tile(t)[0], 0, fin_tile(t)[1], 0)),
        pl.BlockSpec((1, tile, QK_WIDTH), lambda t: (*fin_tile(t), 0)),
        pl.BlockSpec((1, tile, 2 * KV_RANK), lambda t: (*fin_tile(t), 0)),
    ]
    if emit_cache:
        out_shape += [
            jax.ShapeDtypeStruct((batch, 1, seq, KV_RANK), _F32),
            jax.ShapeDtypeStruct((batch, 1, seq, QK_ROPE), _F32),
        ]
        out_specs += [
            pl.BlockSpec((1, 1, tile, KV_RANK), lambda t: (fin_tile(t)[0], 0, fin_tile(t)[1], 0)),
            pl.BlockSpec((1, 1, tile, QK_ROPE), lambda t: (fin_tile(t)[0], 0, fin_tile(t)[1], 0)),
        ]
    return pl.pallas_call(
        functools.partial(_front_body, emit_cache=emit_cache),
        grid=(n_tiles + 2,),
        in_specs=in_specs,
        out_specs=out_specs,
        out_shape=out_shape,
        scratch_shapes=[pltpu.VMEM((tile, D_MODEL), _BF16), pltpu.VMEM((tile, D_MODEL), _BF16),
                        pltpu.VMEM((tile, IN_WIDTH2), _F32), pltpu.VMEM((tile, IN_WIDTH2), _F32)],
        compiler_params=pltpu.CompilerParams(
            dimension_semantics=("arbitrary",),
            vmem_limit_bytes=VMEM_LIMIT_BYTES),
        name="front_ctx" if emit_cache else "front_lat",
    )(x, mod, norm_g, win2, ws_bf, bs_t, g_v, q_norm_g, wq2, kv_norm_g, t1, t2k, t2q)


def _scores(q, k_refs):
    return [lax.dot_general(q, k_ref[0], _NT_DIMS, preferred_element_type=_F32)
            for k_ref in k_refs]


def _softmax_values(scores, v_refs):
    m = functools.reduce(jnp.maximum, [jnp.max(s, axis=1, keepdims=True) for s in scores])
    acc = None
    for s, v_ref in zip(scores, v_refs):
        p = jnp.exp2(s - m).astype(_BF16)
        part = jnp.dot(p, v_ref[0], preferred_element_type=_F32)
        acc = part if acc is None else acc + part
    return acc[:, 0:KV_RANK] / acc[:, KV_RANK:2 * KV_RANK]


def _back_body(*refs, has_ctx):
    if has_ctx:
        (q_ref, klat_ref, vlat_ref, kctx_ref, vctx_ref, a_ref, gb_ref, x_ref, mod_ref, wuv_ref,
         wo_ref, fg_ref, y_ref) = refs
        k_refs, v_refs = (kctx_ref, klat_ref), (vctx_ref, vlat_ref)
    else:
        (q_ref, klat_ref, vlat_ref, a_ref, gb_ref, x_ref, mod_ref, wuv_ref, wo_ref, fg_ref,
         y_ref) = refs
        k_refs, v_refs = (klat_ref,), (vlat_ref,)
    tq = q_ref.shape[2]

    groups = [q_ref[0, 2 * g:2 * g + 2].reshape(2 * tq, QK_WIDTH) for g in range(MLA_HEADS // 2)]
    scores = [_scores(qg, k_refs) for qg in groups]
    outs = [_softmax_values(s, v_refs) for s in scores]

    attn_cols = []
    for pair, o in enumerate(outs):
        o2 = jnp.concatenate([o[0:tq, :], o[tq:2 * tq, :]], axis=1)
        attn_cols.append(jnp.dot(o2.astype(_BF16), wuv_ref[pair], preferred_element_type=_F32))
    attn = jnp.concatenate(attn_cols, axis=1) * gb_ref[0]

    y = jnp.dot(a_ref[0], wo_ref[0:A_WIDTH, :], preferred_element_type=_F32)
    y = y + jnp.dot(attn.astype(_BF16), wo_ref[A_WIDTH:A_WIDTH + B_WIDTH, :],
                    preferred_element_type=_F32)
    gate = mod_ref[0, :, 2 * D_MODEL:3 * D_MODEL]
    out = x_ref[0] + gate * y
    y_ref[0] = _rmsnorm(out, fg_ref[...])


def _back_call(q, klat, vlat, kctx, vctx, a_out, gb, x, mod, mod_index, wuv2, wo_bf, final_g):
    batch, seq, _ = x.shape
    tq = min(TOKEN_TILE, seq)
    has_ctx = kctx is not None
    full = lambda shape: pl.BlockSpec(shape, lambda b, i: (0,) * len(shape))
    keys = lambda arr: pl.BlockSpec((1, arr.shape[1], arr.shape[2]), lambda b, i: (b, 0, 0))
    in_specs = [
        pl.BlockSpec((1, MLA_HEADS, tq, QK_WIDTH), lambda b, i: (b, 0, i, 0)),
        keys(klat), keys(vlat),
    ]
    args = [q, klat, vlat]
    if has_ctx:
        in_specs += [keys(kctx), keys(vctx)]
        args += [kctx, vctx]
    in_specs += [
        pl.BlockSpec((1, tq, A_WIDTH), lambda b, i: (b, i, 0)),
        pl.BlockSpec((1, tq, B_WIDTH), lambda b, i: (b, i, 0)),
        pl.BlockSpec((1, tq, D_MODEL), lambda b, i: (b, i, 0)),
        pl.BlockSpec((1, 1, 3 * D_MODEL), lambda b, i: (mod_index(b), 0, 0)),
        full((MLA_HEADS // 2, 2 * KV_RANK, 2 * V_HEAD)),
        full((D_MODEL, D_MODEL)),
        full((1, D_MODEL)),
    ]
    args += [a_out, gb, x, mod, wuv2, wo_bf, final_g]
    return pl.pallas_call(
        functools.partial(_back_body, has_ctx=has_ctx),
        grid=(batch, seq // tq),
        in_specs=in_specs,
        out_specs=pl.BlockSpec((1, tq, D_MODEL), lambda b, i: (b, i, 0)),
        out_shape=jax.ShapeDtypeStruct((batch, seq, D_MODEL), _F32),
        compiler_params=pltpu.CompilerParams(
            dimension_semantics=("arbitrary", "arbitrary"),
            vmem_limit_bytes=VMEM_LIMIT_BYTES),
        name="back_lat" if has_ctx else "back_ctx",
    )(*args)


def _swap_halves(w):
    half = QK_ROPE // 2
    return jnp.concatenate([-w[..., half:], w[..., :half]], axis=-1)


def _rope_tables(n_tokens, rotate):
    ones = jnp.ones((n_tokens, QK_ROPE), _F32)
    zeros = jnp.zeros((n_tokens, QK_ROPE), _F32)
    if rotate:
        rows = n_tokens // GRID_W
        inv = ROPE_THETA ** (-jnp.arange(AXIS_PAIRS, dtype=_F32) / AXIS_PAIRS)
        ang_r = jnp.arange(rows, dtype=_F32)[:, None] * inv
        ang_c = jnp.arange(GRID_W, dtype=_F32)[:, None] * inv

        def grid_table(fn):
            tr = jnp.broadcast_to(fn(ang_r)[:, None, :], (rows, GRID_W, AXIS_PAIRS))
            tc = jnp.broadcast_to(fn(ang_c)[None, :, :], (rows, GRID_W, AXIS_PAIRS))
            half = jnp.concatenate([tr, tc], axis=-1).reshape(n_tokens, 2 * AXIS_PAIRS)
            return jnp.concatenate([half, half], axis=-1)

        cos, sin = grid_table(jnp.cos), grid_table(jnp.sin)
    else:
        cos, sin = ones, zeros
    t1 = jnp.concatenate([cos, zeros], axis=-1)
    t2k = jnp.concatenate([sin, zeros], axis=-1)
    t2q = jnp.concatenate([sin, ones], axis=-1)
    return t1, t2k, t2q


def kernel(x_prompt, x_sample, cache_ckv, cache_krope, c, c_ctx, norm_g, w_ada, b_ada, w_in, w_s,
           b_s, g_v, q_norm_g, w_uq, kv_norm_g, w_ukv, w_o, final_g):
    depth = norm_g.shape[0]
    assert depth == 1 and w_in.shape[2] == IN_WIDTH
    dec_batch = x_sample.shape[0]
    xp, xs = x_prompt, x_sample
    new_ckv, new_kr = [], []
    for l in range(depth):
        cond = jnp.concatenate(
            [c, c_ctx[None, :], jnp.zeros((16 - dec_batch - 1, D_MODEL), _F32)], axis=0)
        mod = _mod_call(cond, w_ada[l], b_ada[l]).reshape(16, 1, 3 * D_MODEL)

        win2 = _prep_call(jnp.transpose(w_in[l]))
        wq3 = w_uq[l].reshape(Q_RANK, MLA_HEADS, QK_NOPE + QK_ROPE)
        wkv3 = w_ukv[l].reshape(KV_RANK, MLA_HEADS, QK_NOPE + V_HEAD)
        wq_nope = jnp.transpose(wq3[:, :, :QK_NOPE], (1, 0, 2))
        w_uk = jnp.transpose(wkv3[:, :, :QK_NOPE], (1, 0, 2))
        w_uv = jnp.transpose(wkv3[:, :, QK_NOPE:], (1, 0, 2))
        wq_abs = _fold_call(wq_nope, w_uk)
        wq_rope = jnp.transpose(wq3[:, :, QK_NOPE:], (1, 0, 2))
        wq2 = jnp.concatenate([wq_abs, wq_rope, _swap_halves(wq_rope)], axis=-1)
        wq2 = jnp.transpose(wq2, (1, 0, 2)).reshape(Q_RANK, MLA_HEADS * QK_WIDTH).astype(_BF16)
        zero_blk = jnp.zeros((KV_RANK, V_HEAD), _F32)
        wuv2 = jnp.stack([
            jnp.block([[w_uv[2 * p], zero_blk], [zero_blk, w_uv[2 * p + 1]]])
            for p in range(MLA_HEADS // 2)]).astype(_BF16)
        wo_bf = w_o[l].astype(_BF16)
        ws_bf = w_s[l].astype(_BF16)
        bs_t = b_s[l].T
        gv_row = g_v[l].reshape(1, A_WIDTH)
        shared = (norm_g[l].reshape(1, D_MODEL), win2, ws_bf, bs_t, gv_row,
                  q_norm_g[l].reshape(1, Q_RANK), wq2, kv_norm_g[l].reshape(1, KV_RANK))
        fg = final_g.reshape(1, D_MODEL)

        ctx_index = lambda b: dec_batch
        a_c, gb_c, q_c, k_c, v_c, ckv_c, kr_c = _front_call(
            xp, mod, ctx_index, *shared, _rope_tables(xp.shape[1], False), True)
        xp = _back_call(q_c, k_c, v_c, None, None, a_c, gb_c, xp, mod, ctx_index, wuv2, wo_bf, fg)
        new_ckv.append(ckv_c)
        new_kr.append(kr_c)

        lat_index = lambda b: b
        a_s, gb_s, q_s, k_s, v_s = _front_call(
            xs, mod, lat_index, *shared, _rope_tables(xs.shape[1], True), False)
        cache_k = jnp.concatenate(
            [cache_ckv[:, l], jnp.zeros(cache_krope[:, l].shape, _F32), cache_krope[:, l]],
            axis=-1).astype(_BF16)
        cache_v = jnp.concatenate(
            [cache_ckv[:, l], jnp.ones(cache_ckv[:, l].shape, _F32)], axis=-1).astype(_BF16)
        xs = _back_call(q_s, k_s, v_s, cache_k, cache_v, a_s, gb_s, xs, mod, lat_index, wuv2,
                        wo_bf, fg)
    return (xp, xs, jnp.concatenate(new_ckv, axis=1), jnp.concatenate(new_kr, axis=1))
```

```python
import functools
import math

import jax
import jax.numpy as jnp
from jax import lax
from jax.experimental import pallas as pl
from jax.experimental.pallas import tpu as pltpu

D_MODEL = 1024
GRID_W = 64
EPS = 1e-6
A_HEADS = 4
A_HEAD_DIM = 128
A_WIDTH = A_HEADS * A_HEAD_DIM
CHUNK = 128
MLA_HEADS = 4
QK_NOPE = 128
QK_ROPE = 64
V_HEAD = 128
B_WIDTH = MLA_HEADS * V_HEAD
Q_RANK = 256
KV_RANK = 128
AXIS_PAIRS = QK_ROPE // 4
ROPE_THETA = 10000.0
ATTN_SCALE = 1.0 / math.sqrt(QK_NOPE + QK_ROPE)
LOG2_E = 1.4426950408889634

QK_WIDTH = KV_RANK + 2 * QK_ROPE
OFF_U = 0
OFF_V = OFF_U + A_WIDTH
OFF_GA = OFF_V + A_WIDTH
OFF_CQ = OFF_GA + A_WIDTH
OFF_CKV = OFF_CQ + Q_RANK
OFF_KR = OFF_CKV + KV_RANK
OFF_GB = OFF_KR + 2 * QK_ROPE
IN_WIDTH = OFF_GB - QK_ROPE + B_WIDTH
IN_WIDTH2 = OFF_GB + B_WIDTH

LANES = 128
SUB_TILE = 256
TOKEN_TILE = 512
PROJ_BLOCK = 256
VMEM_LIMIT_BYTES = 56 * 1024 * 1024

_F32 = jnp.float32
_BF16 = jnp.bfloat16
_NT_DIMS = (((1,), (1,)), ((), ()))


def _silu(x):
    hx = 0.5 * x
    return hx + hx * jnp.tanh(hx)


def _gelu_tanh(x):
    return x * (0.5 * (1.0 + jnp.tanh(0.7978845608028654 * (x + 0.044715 * (x * x * x)))))


def _rmsnorm(x, g):
    ms = jnp.mean(x * x, axis=-1, keepdims=True)
    return (x * lax.rsqrt(ms + EPS)) * g


def _mod_body(cond_ref, w_ref, b_ref, o_ref):
    s = _silu(cond_ref[...])
    o_ref[...] = jnp.dot(s.astype(_BF16), w_ref[...].astype(_BF16),
                         preferred_element_type=_F32) + b_ref[...]


def _mod_call(cond, w_ada, b_ada):
    rows = cond.shape[0]
    n_out = w_ada.shape[1]
    col_block = 512
    return pl.pallas_call(
        _mod_body,
        grid=(n_out // col_block,),
        in_specs=[
            pl.BlockSpec((rows, D_MODEL), lambda j: (0, 0)),
            pl.BlockSpec((D_MODEL, col_block), lambda j: (0, j)),
            pl.BlockSpec((1, col_block), lambda j: (0, j)),
        ],
        out_specs=pl.BlockSpec((rows, col_block), lambda j: (0, j)),
        out_shape=jax.ShapeDtypeStruct((rows, n_out), _F32),
        name="adaln_mod",
    )(cond, w_ada, b_ada.reshape(1, n_out))


def _fold_body(wq_ref, wk_ref, o_ref):
    o_ref[0] = lax.dot_general(wq_ref[0], wk_ref[0], _NT_DIMS,
                               precision=lax.Precision.HIGHEST,
                               preferred_element_type=_F32)


def _fold_call(wq_nope, w_uk):
    return pl.pallas_call(
        _fold_body,
        grid=(MLA_HEADS,),
        in_specs=[
            pl.BlockSpec((1, Q_RANK, QK_NOPE), lambda h: (h, 0, 0)),
            pl.BlockSpec((1, KV_RANK, QK_NOPE), lambda h: (h, 0, 0)),
        ],
        out_specs=pl.BlockSpec((1, Q_RANK, KV_RANK), lambda h: (h, 0, 0)),
        out_shape=jax.ShapeDtypeStruct((MLA_HEADS, Q_RANK, KV_RANK), _F32),
        name="fold_q_uk",
    )(wq_nope, w_uk)


_KR_BLOCK = OFF_KR // LANES
_KR_HALF_ROW = OFF_KR // QK_ROPE


def _prep_body(a_ref, b_ref, o_ref):
    a = a_ref[...]
    b = b_ref[...]
    half = QK_ROPE // 2
    partner = jnp.concatenate([-b[half:, :], b[:half, :]], axis=0)
    is_kr = pl.program_id(0) == _KR_BLOCK
    b = jnp.where(is_kr, partner, b)
    o_ref[...] = jnp.concatenate([a, b], axis=0).T.astype(_BF16)


def _prep_call(w_in_t):
    def ra(j):
        return (jnp.where(j <= _KR_BLOCK, 2 * j, 2 * j - 1), 0)

    def rb(j):
        return (jnp.where(j < _KR_BLOCK, 2 * j + 1, 2 * j), 0)

    assert _KR_HALF_ROW == 2 * _KR_BLOCK
    return pl.pallas_call(
        _prep_body,
        grid=(IN_WIDTH2 // LANES,),
        in_specs=[
            pl.BlockSpec((QK_ROPE, D_MODEL), ra),
            pl.BlockSpec((QK_ROPE, D_MODEL), rb),
        ],
        out_specs=pl.BlockSpec((D_MODEL, LANES), lambda j: (0, j)),
        out_shape=jax.ShapeDtypeStruct((D_MODEL, IN_WIDTH2), _BF16),
        name="prep_w_in",
    )(w_in_t, w_in_t)


def _front_substep(x_ref, mod_ref, ng_ref, win_ref, ws_ref, bs_ref, gv_ref, qg_ref, wq_ref,
                   kvg_ref, tables, out_refs, row0, h_write, h_read, z_write, z_read, emit_cache):
    if emit_cache:
        a_ref, gb_ref, q_ref, k_ref, v_ref, ckv_ref, kr_ref = out_refs
    else:
        a_ref, gb_ref, q_ref, k_ref, v_ref = out_refs
    t1, t2k, t2q = tables
    rows = slice(row0, row0 + SUB_TILE)
    half = SUB_TILE // 2
    assert IN_WIDTH2 == 10 * PROJ_BLOCK

    def project(j):
        lo = j * PROJ_BLOCK
        z_write[:, lo:lo + PROJ_BLOCK] = jnp.dot(h_read[...], win_ref[:, lo:lo + PROJ_BLOCK],
                                                 preferred_element_type=_F32)

    shift = mod_ref[0, :, 0:D_MODEL]
    gain = ng_ref[...] * (1.0 + mod_ref[0, :, D_MODEL:2 * D_MODEL])

    def norm_rows(lo):
        x = x_ref[0, row0 + lo:row0 + lo + half, :]
        ms = jnp.mean(x * x, axis=-1, keepdims=True)
        h_write[lo:lo + half, :] = ((x * lax.rsqrt(ms + EPS)) * gain + shift).astype(_BF16)

    def vn_head(hd):
        lo = hd * A_HEAD_DIM
        v = _gelu_tanh(z_read[:, OFF_V + lo:OFF_V + lo + A_HEAD_DIM])
        return _rmsnorm(v, gv_ref[:, lo:lo + A_HEAD_DIM]).astype(_BF16)

    def mix_head(hd, vn):
        bias = bs_ref[:, hd:hd + 1]
        parts = [jnp.dot(ws_ref[hd], vn[c * CHUNK:(c + 1) * CHUNK, :],
                         preferred_element_type=_F32) + bias for c in range(SUB_TILE // CHUNK)]
        return jnp.concatenate(parts, axis=0)

    def gate_a(hd, mixed):
        lo = hd * A_HEAD_DIM
        u = _gelu_tanh(z_read[:, OFF_U + lo:OFF_U + lo + A_HEAD_DIM])
        g = _silu(z_read[:, OFF_GA + lo:OFF_GA + lo + A_HEAD_DIM])
        a_ref[0, rows, lo:lo + A_HEAD_DIM] = (u * mixed * g).astype(_BF16)

    def gate_b(lo, width):
        gb_ref[0, rows, lo:lo + width] = _silu(z_read[:, OFF_GB + lo:OFF_GB + lo + width])

    def rope_q(qa, hd):
        base = hd * QK_WIDTH
        pair = qa[:, base + KV_RANK:base + QK_WIDTH]
        rope = pair * t1 + pltpu.roll(pair, QK_ROPE, 1) * t2q
        q_ref[0, hd, rows, :] = jnp.concatenate(
            [qa[:, base:base + KV_RANK], rope], axis=1).astype(_BF16)

    project(0)
    cq = _rmsnorm(z_read[:, OFF_CQ:OFF_CQ + Q_RANK],
                  qg_ref[...] * (ATTN_SCALE * LOG2_E)).astype(_BF16)
    vn0 = vn_head(0)
    project(1)
    vn1 = vn_head(1)
    gate_b(0, B_WIDTH // 2)
    project(2)
    vn2 = vn_head(2)
    gate_b(B_WIDTH // 2, B_WIDTH // 2)
    project(3)
    vn3 = vn_head(3)
    ckv = _rmsnorm(z_read[:, OFF_CKV:OFF_CKV + KV_RANK], kvg_ref[...])
    kpair = z_read[:, OFF_KR:OFF_KR + 2 * QK_ROPE]
    krot = kpair * t1 + pltpu.roll(kpair, QK_ROPE, 1) * t2k
    k_ref[0, rows, :] = jnp.concatenate([ckv, krot], axis=1).astype(_BF16)
    v_ref[0, rows, :] = jnp.concatenate([ckv, jnp.ones_like(ckv)], axis=1).astype(_BF16)
    if emit_cache:
        ckv_ref[0, 0, rows, :] = ckv
        kr_ref[0, 0, rows, :] = kpair[:, 0:QK_ROPE]
    qa = jnp.dot(cq, wq_ref[...], preferred_element_type=_F32)
    norm_rows(0)
    project(4)
    rope_q(qa, 0)
    rope_q(qa, 1)
    mixed0 = mix_head(0, vn0)
    mixed1 = mix_head(1, vn1)
    rope_q(qa, 2)
    rope_q(qa, 3)
    project(5)
    gate_a(0, mixed0)
    mixed2 = mix_head(2, vn2)
    mixed3 = mix_head(3, vn3)
    project(6)
    gate_a(1, mixed1)
    norm_rows(half)
    project(7)
    gate_a(2, mixed2)
    project(8)
    gate_a(3, mixed3)
    project(9)


def _front_body(x_ref, mod_ref, ng_ref, win_ref, ws_ref, bs_ref, gv_ref, qg_ref, wq_ref,
                kvg_ref, t1_ref, t2k_ref, t2q_ref, *rest, emit_cache, tiles_per_row):
    out_refs, (h_a, h_b, z_a, z_b) = rest[:-4], rest[-4:]
    t = pl.program_id(0)
    tile = x_ref.shape[1]
    assert tile == 2 * SUB_TILE
    n_pos = t1_ref.shape[0]

    @pl.when(t == 0)
    def _():
        h_b[...] = jnp.zeros(h_b.shape, _BF16)
        z_b[...] = jnp.zeros(z_b.shape, _F32)

    pos_base = (jnp.maximum(t - 1, 0) % tiles_per_row) * tile
    for sub, (h_write, h_read, z_write, z_read) in enumerate(
            ((h_a, h_b, z_a, z_b), (h_b, h_a, z_b, z_a))):
        row0 = sub * SUB_TILE
        pos = pl.multiple_of((pos_base + row0) % n_pos, SUB_TILE)

        @pl.when(t >= -sub)
        def _(row0=row0, pos=pos, bufs=(h_write, h_read, z_write, z_read)):
            tables = tuple(r[pl.ds(pos, SUB_TILE), :] for r in (t1_ref, t2k_ref, t2q_ref))
            _front_substep(x_ref, mod_ref, ng_ref, win_ref, ws_ref, bs_ref, gv_ref, qg_ref,
                           wq_ref, kvg_ref, tables, out_refs, row0, *bufs, emit_cache)


def _front_call(x, mod, mod_index, norm_g, win2, ws_bf, bs_t, g_v, q_norm_g, wq2, kv_norm_g,
                tables, emit_cache):
    batch, seq, _ = x.shape
    tile = 2 * SUB_TILE
    tiles_per_row = seq // tile
    n_tiles = batch * tiles_per_row
    t1, t2k, t2q = tables

    def in_tile(t):
        tt = jnp.minimum(t, n_tiles - 1)
        return tt // tiles_per_row, tt % tiles_per_row

    def out_tile(t):
        tt = jnp.maximum(t - 1, 0)
        return tt // tiles_per_row, tt % tiles_per_row

    full = lambda shape: pl.BlockSpec(shape, lambda t: (0,) * len(shape))
    in_specs = [
        pl.BlockSpec((1, tile, D_MODEL), lambda t: (*in_tile(t), 0)),
        pl.BlockSpec((1, 1, 3 * D_MODEL), lambda t: (mod_index(in_tile(t)[0]), 0, 0)),
        full((1, D_MODEL)),
        full((D_MODEL, IN_WIDTH2)),
        full((A_HEADS, CHUNK, CHUNK)),
        full((CHUNK, A_HEADS)),
        full((1, A_WIDTH)),
        full((1, Q_RANK)),
        full((Q_RANK, MLA_HEADS * QK_WIDTH)),
        full((1, KV_RANK)),
        full(t1.shape), full(t2k.shape), full(t2q.shape),
    ]
    out_shape = [
        jax.ShapeDtypeStruct((batch, seq, A_WIDTH), _BF16),
        jax.ShapeDtypeStruct((batch, seq, B_WIDTH), _F32),
        jax.ShapeDtypeStruct((batch, MLA_HEADS, seq, QK_WIDTH), _BF16),
        jax.ShapeDtypeStruct((batch, seq, QK_WIDTH), _BF16),
        jax.ShapeDtypeStruct((batch, seq, 2 * KV_RANK), _BF16),
    ]
    out_specs = [
        pl.BlockSpec((1, tile, A_WIDTH), lambda t: (*out_tile(t), 0)),
        pl.BlockSpec((1, tile, B_WIDTH), lambda t: (*out_tile(t), 0)),
        pl.BlockSpec((1, MLA_HEADS, tile, QK_WIDTH),
                     lambda t: (out_tile(t)[0], 0, out_tile(t)[1], 0)),
        pl.BlockSpec((1, tile, QK_WIDTH), lambda t: (*out_tile(t), 0)),
        pl.BlockSpec((1, tile, 2 * KV_RANK), lambda t: (*out_tile(t), 0)),
    ]
    if emit_cache:
        out_shape += [
            jax.ShapeDtypeStruct((batch, 1, seq, KV_RANK), _F32),
            jax.ShapeDtypeStruct((batch, 1, seq, QK_ROPE), _F32),
        ]
        out_specs += [
            pl.BlockSpec((1, 1, tile, KV_RANK), lambda t: (out_tile(t)[0], 0, out_tile(t)[1], 0)),
            pl.BlockSpec((1, 1, tile, QK_ROPE), lambda t: (out_tile(t)[0], 0, out_tile(t)[1], 0)),
        ]
    return pl.pallas_call(
        functools.partial(_front_body, emit_cache=emit_cache, tiles_per_row=tiles_per_row),
        grid=(n_tiles + 1,),
        in_specs=in_specs,
        out_specs=out_specs,
        out_shape=out_shape,
        scratch_shapes=[pltpu.VMEM((SUB_TILE, D_MODEL), _BF16), pltpu.VMEM((SUB_TILE, D_MODEL), _BF16),
                        pltpu.VMEM((SUB_TILE, IN_WIDTH2), _F32),
                        pltpu.VMEM((SUB_TILE, IN_WIDTH2), _F32)],
        compiler_params=pltpu.CompilerParams(
            dimension_semantics=("arbitrary",),
            vmem_limit_bytes=VMEM_LIMIT_BYTES),
        name="front_ctx" if emit_cache else "front_lat",
    )(x, mod, norm_g, win2, ws_bf, bs_t, g_v, q_norm_g, wq2, kv_norm_g, t1, t2k, t2q)


def _scores(q, k_refs):
    return [lax.dot_general(q, k_ref[0], _NT_DIMS, preferred_element_type=_F32)
            for k_ref in k_refs]


def _softmax_values(scores, v_refs):
    m = functools.reduce(jnp.maximum, [jnp.max(s, axis=1, keepdims=True) for s in scores])
    acc = None
    for s, v_ref in zip(scores, v_refs):
        p = jnp.exp2(s - m).astype(_BF16)
        part = jnp.dot(p, v_ref[0], preferred_element_type=_F32)
        acc = part if acc is None else acc + part
    return acc[:, 0:KV_RANK] / acc[:, KV_RANK:2 * KV_RANK]


def _back_body(*refs, has_ctx):
    if has_ctx:
        (q_ref, klat_ref, vlat_ref, kctx_ref, vctx_ref, a_ref, gb_ref, x_ref, mod_ref, wuv_ref,
         wo_ref, fg_ref, y_ref) = refs
        k_refs, v_refs = (kctx_ref, klat_ref), (vctx_ref, vlat_ref)
    else:
        (q_ref, klat_ref, vlat_ref, a_ref, gb_ref, x_ref, mod_ref, wuv_ref, wo_ref, fg_ref,
         y_ref) = refs
        k_refs, v_refs = (klat_ref,), (vlat_ref,)
    tq = q_ref.shape[2]

    groups = [q_ref[0, 2 * g:2 * g + 2].reshape(2 * tq, QK_WIDTH) for g in range(MLA_HEADS // 2)]
    scores = [_scores(qg, k_refs) for qg in groups]
    outs = [_softmax_values(s, v_refs) for s in scores]

    attn_cols = []
    for pair, o in enumerate(outs):
        o2 = jnp.concatenate([o[0:tq, :], o[tq:2 * tq, :]], axis=1)
        attn_cols.append(jnp.dot(o2.astype(_BF16), wuv_ref[pair], preferred_element_type=_F32))
    attn = jnp.concatenate(attn_cols, axis=1) * gb_ref[0]

    y = jnp.dot(a_ref[0], wo_ref[0:A_WIDTH, :], preferred_element_type=_F32)
    y = y + jnp.dot(attn.astype(_BF16), wo_ref[A_WIDTH:A_WIDTH + B_WIDTH, :],
                    preferred_element_type=_F32)
    gate = mod_ref[0, :, 2 * D_MODEL:3 * D_MODEL]
    out = x_ref[0] + gate * y
    y_ref[0] = _rmsnorm(out, fg_ref[...])


def _back_call(q, klat, vlat, kctx, vctx, a_out, gb, x, mod, mod_index, wuv2, wo_bf, final_g):
    batch, seq, _ = x.shape
    tq = min(TOKEN_TILE, seq)
    has_ctx = kctx is not None
    rows_per_q = q.shape[2] // seq
    q_blocks = seq // tq
    full = lambda shape: pl.BlockSpec(shape, lambda b, i: (0,) * len(shape))
    keys = lambda arr: pl.BlockSpec((1, arr.shape[1], arr.shape[2]), lambda b, i: (b, 0, 0))
    in_specs = [
        pl.BlockSpec((1, MLA_HEADS, tq, QK_WIDTH),
                     lambda b, i: (b // rows_per_q, 0, (b % rows_per_q) * q_blocks + i, 0)),
        keys(klat), keys(vlat),
    ]
    args = [q, klat, vlat]
    if has_ctx:
        in_specs += [keys(kctx), keys(vctx)]
        args += [kctx, vctx]
    in_specs += [
        pl.BlockSpec((1, tq, A_WIDTH), lambda b, i: (b, i, 0)),
        pl.BlockSpec((1, tq, B_WIDTH), lambda b, i: (b, i, 0)),
        pl.BlockSpec((1, tq, D_MODEL), lambda b, i: (b, i, 0)),
        pl.BlockSpec((1, 1, 3 * D_MODEL), lambda b, i: (mod_index(b), 0, 0)),
        full((MLA_HEADS // 2, 2 * KV_RANK, 2 * V_HEAD)),
        full((D_MODEL, D_MODEL)),
        full((1, D_MODEL)),
    ]
    args += [a_out, gb, x, mod, wuv2, wo_bf, final_g]
    return pl.pallas_call(
        functools.partial(_back_body, has_ctx=has_ctx),
        grid=(batch, seq // tq),
        in_specs=in_specs,
        out_specs=pl.BlockSpec((1, tq, D_MODEL), lambda b, i: (b, i, 0)),
        out_shape=jax.ShapeDtypeStruct((batch, seq, D_MODEL), _F32),
        compiler_params=pltpu.CompilerParams(
            dimension_semantics=("arbitrary", "arbitrary"),
            vmem_limit_bytes=VMEM_LIMIT_BYTES),
        name="back_lat" if has_ctx else "back_ctx",
    )(*args)


def _swap_halves(w):
    half = QK_ROPE // 2
    return jnp.concatenate([-w[..., half:], w[..., :half]], axis=-1)


def _rope_tables(n_tokens, rotate):
    ones = jnp.ones((n_tokens, QK_ROPE), _F32)
    zeros = jnp.zeros((n_tokens, QK_ROPE), _F32)
    if rotate:
        rows = n_tokens // GRID_W
        inv = ROPE_THETA ** (-jnp.arange(AXIS_PAIRS, dtype=_F32) / AXIS_PAIRS)
        ang_r = jnp.arange(rows, dtype=_F32)[:, None] * inv
        ang_c = jnp.arange(GRID_W, dtype=_F32)[:, None] * inv

        def grid_table(fn):
            tr = jnp.broadcast_to(fn(ang_r)[:, None, :], (rows, GRID_W, AXIS_PAIRS))
            tc = jnp.broadcast_to(fn(ang_c)[None, :, :], (rows, GRID_W, AXIS_PAIRS))
            half = jnp.concatenate([tr, tc], axis=-1).reshape(n_tokens, 2 * AXIS_PAIRS)
            return jnp.concatenate([half, half], axis=-1)

        cos, sin = grid_table(jnp.cos), grid_table(jnp.sin)
    else:
        cos, sin = ones, zeros
    t1 = jnp.concatenate([cos, zeros], axis=-1)
    t2k = jnp.concatenate([sin, zeros], axis=-1)
    t2q = jnp.concatenate([sin, ones], axis=-1)
    return t1, t2k, t2q


def kernel(x_prompt, x_sample, cache_ckv, cache_krope, c, c_ctx, norm_g, w_ada, b_ada, w_in, w_s,
           b_s, g_v, q_norm_g, w_uq, kv_norm_g, w_ukv, w_o, final_g):
    depth = norm_g.shape[0]
    assert depth == 1 and w_in.shape[2] == IN_WIDTH
    dec_batch = x_sample.shape[0]
    xp, xs = x_prompt, x_sample
    new_ckv, new_kr = [], []
    for l in range(depth):
        cond = jnp.concatenate(
            [c, c_ctx[None, :], jnp.zeros((16 - dec_batch - 1, D_MODEL), _F32)], axis=0)
        mod = _mod_call(cond, w_ada[l], b_ada[l]).reshape(16, 1, 3 * D_MODEL)

        win2 = _prep_call(jnp.transpose(w_in[l]))
        wq3 = w_uq[l].reshape(Q_RANK, MLA_HEADS, QK_NOPE + QK_ROPE)
        wkv3 = w_ukv[l].reshape(KV_RANK, MLA_HEADS, QK_NOPE + V_HEAD)
        wq_nope = jnp.transpose(wq3[:, :, :QK_NOPE], (1, 0, 2))
        w_uk = jnp.transpose(wkv3[:, :, :QK_NOPE], (1, 0, 2))
        w_uv = jnp.transpose(wkv3[:, :, QK_NOPE:], (1, 0, 2))
        wq_abs = _fold_call(wq_nope, w_uk)
        wq_rope = jnp.transpose(wq3[:, :, QK_NOPE:], (1, 0, 2))
        wq2 = jnp.concatenate([wq_abs, wq_rope, _swap_halves(wq_rope)], axis=-1)
        wq2 = jnp.transpose(wq2, (1, 0, 2)).reshape(Q_RANK, MLA_HEADS * QK_WIDTH).astype(_BF16)
        zero_blk = jnp.zeros((KV_RANK, V_HEAD), _F32)
        wuv2 = jnp.stack([
            jnp.block([[w_uv[2 * p], zero_blk], [zero_blk, w_uv[2 * p + 1]]])
            for p in range(MLA_HEADS // 2)]).astype(_BF16)
        wo_bf = w_o[l].astype(_BF16)
        ws_bf = w_s[l].astype(_BF16)
        bs_t = b_s[l].T
        gv_row = g_v[l].reshape(1, A_WIDTH)
        shared = (norm_g[l].reshape(1, D_MODEL), win2, ws_bf, bs_t, gv_row,
                  q_norm_g[l].reshape(1, Q_RANK), wq2, kv_norm_g[l].reshape(1, KV_RANK))
        fg = final_g.reshape(1, D_MODEL)

        ctx_index = lambda b: dec_batch
        ctx_batch, ctx_seq, _ = xp.shape
        pack = 2 * SUB_TILE // ctx_seq
        a_c, gb_c, q_c, k_c, v_c, ckv_c, kr_c = _front_call(
            xp.reshape(ctx_batch // pack, pack * ctx_seq, D_MODEL), mod, ctx_index, *shared,
            _rope_tables(ctx_seq, False), True)
        unpack = lambda arr: arr.reshape(ctx_batch, ctx_seq, arr.shape[-1])
        xp = _back_call(q_c, unpack(k_c), unpack(v_c), None, None, unpack(a_c), unpack(gb_c), xp,
                        mod, ctx_index, wuv2, wo_bf, fg)
        new_ckv.append(ckv_c.reshape(ctx_batch, 1, ctx_seq, KV_RANK))
        new_kr.append(kr_c.reshape(ctx_batch, 1, ctx_seq, QK_ROPE))

        lat_index = lambda b: b
        a_s, gb_s, q_s, k_s, v_s = _front_call(
            xs, mod, lat_index, *shared, _rope_tables(xs.shape[1], True), False)
        cache_k = jnp.concatenate(
            [cache_ckv[:, l], jnp.zeros(cache_krope[:, l].shape, _F32), cache_krope[:, l]],
            axis=-1).astype(_BF16)
        cache_v = jnp.concatenate(
            [cache_ckv[:, l], jnp.ones(cache_ckv[:, l].shape, _F32)], axis=-1).astype(_BF16)
        xs = _back_call(q_s, k_s, v_s, cache_k, cache_v, a_s, gb_s, xs, mod, lat_index, wuv2,
                        wo_bf, fg)
    return (xp, xs, jnp.concatenate(new_ckv, axis=1), jnp.concatenate(new_kr, axis=1))
```

```python
import functools
import math

import jax
import jax.numpy as jnp
from jax import lax
from jax.experimental import pallas as pl
from jax.experimental.pallas import tpu as pltpu

D_MODEL = 1024
GRID_W = 64
EPS = 1e-6
A_HEADS = 4
A_HEAD_DIM = 128
A_WIDTH = A_HEADS * A_HEAD_DIM
CHUNK = 128
MLA_HEADS = 4
QK_NOPE = 128
QK_ROPE = 64
V_HEAD = 128
B_WIDTH = MLA_HEADS * V_HEAD
Q_RANK = 256
KV_RANK = 128
AXIS_PAIRS = QK_ROPE // 4
ROPE_THETA = 10000.0
ATTN_SCALE = 1.0 / math.sqrt(QK_NOPE + QK_ROPE)
LOG2_E = 1.4426950408889634

QK_WIDTH = KV_RANK + 2 * QK_ROPE
OFF_U = 0
OFF_V = OFF_U + A_WIDTH
OFF_GA = OFF_V + A_WIDTH
OFF_CQ = OFF_GA + A_WIDTH
OFF_CKV = OFF_CQ + Q_RANK
OFF_KR = OFF_CKV + KV_RANK
OFF_GB = OFF_KR + 2 * QK_ROPE
IN_WIDTH = OFF_GB - QK_ROPE + B_WIDTH
IN_WIDTH2 = OFF_GB + B_WIDTH

PK_A = 0
PK_K = PK_A + A_WIDTH
PK_V = PK_K + QK_WIDTH
PK_Q = PK_V + 2 * KV_RANK
PK_WIDTH = PK_Q + MLA_HEADS * QK_WIDTH

LANES = 128
SUB_TILE = 256
TOKEN_TILE = 512
PROJ_BLOCK = 256
VMEM_LIMIT_BYTES = 56 * 1024 * 1024

_F32 = jnp.float32
_BF16 = jnp.bfloat16
_NT_DIMS = (((1,), (1,)), ((), ()))


def _silu(x):
    hx = 0.5 * x
    return hx + hx * jnp.tanh(hx)


def _gelu_tanh(x):
    return x * (0.5 * (1.0 + jnp.tanh(0.7978845608028654 * (x + 0.044715 * (x * x * x)))))


def _rmsnorm(x, g):
    ms = jnp.mean(x * x, axis=-1, keepdims=True)
    return (x * lax.rsqrt(ms + EPS)) * g


def _mod_body(cond_ref, w_ref, b_ref, o_ref):
    s = _silu(cond_ref[...])
    o_ref[...] = jnp.dot(s.astype(_BF16), w_ref[...].astype(_BF16),
                         preferred_element_type=_F32) + b_ref[...]


def _mod_call(cond, w_ada, b_ada):
    rows = cond.shape[0]
    n_out = w_ada.shape[1]
    col_block = 512
    return pl.pallas_call(
        _mod_body,
        grid=(n_out // col_block,),
        in_specs=[
            pl.BlockSpec((rows, D_MODEL), lambda j: (0, 0)),
            pl.BlockSpec((D_MODEL, col_block), lambda j: (0, j)),
            pl.BlockSpec((1, col_block), lambda j: (0, j)),
        ],
        out_specs=pl.BlockSpec((rows, col_block), lambda j: (0, j)),
        out_shape=jax.ShapeDtypeStruct((rows, n_out), _F32),
        name="adaln_mod",
    )(cond, w_ada, b_ada.reshape(1, n_out))


def _fold_body(wq_ref, wk_ref, o_ref):
    o_ref[0] = lax.dot_general(wq_ref[0], wk_ref[0], _NT_DIMS,
                               precision=lax.Precision.HIGHEST,
                               preferred_element_type=_F32)


def _fold_call(wq_nope, w_uk):
    return pl.pallas_call(
        _fold_body,
        grid=(MLA_HEADS,),
        in_specs=[
            pl.BlockSpec((1, Q_RANK, QK_NOPE), lambda h: (h, 0, 0)),
            pl.BlockSpec((1, KV_RANK, QK_NOPE), lambda h: (h, 0, 0)),
        ],
        out_specs=pl.BlockSpec((1, Q_RANK, KV_RANK), lambda h: (h, 0, 0)),
        out_shape=jax.ShapeDtypeStruct((MLA_HEADS, Q_RANK, KV_RANK), _F32),
        name="fold_q_uk",
    )(wq_nope, w_uk)


_KR_BLOCK = OFF_KR // LANES


_PREP_SPLIT = IN_WIDTH2 // 2


def _prep_body(w_ref, o_ref):
    half = QK_ROPE // 2
    blocks_per_step = _PREP_SPLIT // LANES

    def emit(step):
        base = step * _PREP_SPLIT
        for j in range(step * blocks_per_step, (step + 1) * blocks_per_step):
            if j < _KR_BLOCK:
                blk = w_ref[j * LANES - base:(j + 1) * LANES - base, :]
            elif j == _KR_BLOCK:
                kr = w_ref[OFF_KR - base:OFF_KR - base + QK_ROPE, :]
                partner = jnp.concatenate([-kr[half:, :], kr[:half, :]], axis=0)
                blk = jnp.concatenate([kr, partner], axis=0)
            else:
                lo = j * LANES - QK_ROPE - base
                blk = w_ref[lo:lo + LANES, :]
            o_ref[:, j * LANES - base:(j + 1) * LANES - base] = blk.T.astype(_BF16)

    for step in range(IN_WIDTH2 // _PREP_SPLIT):
        pl.when(pl.program_id(0) == step)(functools.partial(emit, step))


def _prep_call(w_in_t):
    assert _KR_BLOCK * LANES == OFF_KR and OFF_KR >= _PREP_SPLIT
    return pl.pallas_call(
        _prep_body,
        grid=(IN_WIDTH2 // _PREP_SPLIT,),
        in_specs=[pl.BlockSpec((_PREP_SPLIT, D_MODEL), lambda s: (s, 0))],
        out_specs=pl.BlockSpec((D_MODEL, _PREP_SPLIT), lambda s: (0, s)),
        out_shape=jax.ShapeDtypeStruct((D_MODEL, IN_WIDTH2), _BF16),
        compiler_params=pltpu.CompilerParams(vmem_limit_bytes=VMEM_LIMIT_BYTES),
        name="prep_w_in",
    )(w_in_t)


def _front_substep(x_ref, mod_ref, ng_ref, win_ref, ws_ref, bs_ref, gv_ref, qg_ref, wq_ref,
                   kvg_ref, tables, out_refs, row0, h_write, h_read, z_write, z_read, emit_cache):
    if emit_cache:
        pk_ref, gb_ref, ckv_ref, kr_ref = out_refs
    else:
        pk_ref, gb_ref = out_refs
    t1, t2k, t2q = tables
    rows = slice(row0, row0 + SUB_TILE)
    half = SUB_TILE // 2
    assert IN_WIDTH2 == 10 * PROJ_BLOCK

    def project(j):
        lo = j * PROJ_BLOCK
        z_write[:, lo:lo + PROJ_BLOCK] = jnp.dot(h_read[...], win_ref[:, lo:lo + PROJ_BLOCK],
                                                 preferred_element_type=_F32)

    shift = mod_ref[0, :, 0:D_MODEL]
    gain = ng_ref[...] * (1.0 + mod_ref[0, :, D_MODEL:2 * D_MODEL])

    def norm_rows(lo):
        x = x_ref[0, row0 + lo:row0 + lo + half, :]
        ms = jnp.mean(x * x, axis=-1, keepdims=True)
        h_write[lo:lo + half, :] = ((x * lax.rsqrt(ms + EPS)) * gain + shift).astype(_BF16)

    def vn_head(hd):
        lo = hd * A_HEAD_DIM
        v = _gelu_tanh(z_read[:, OFF_V + lo:OFF_V + lo + A_HEAD_DIM])
        return _rmsnorm(v, gv_ref[:, lo:lo + A_HEAD_DIM]).astype(_BF16)

    def mix_head(hd, vn):
        bias = bs_ref[:, hd:hd + 1]
        parts = [jnp.dot(ws_ref[hd], vn[c * CHUNK:(c + 1) * CHUNK, :],
                         preferred_element_type=_F32) + bias for c in range(SUB_TILE // CHUNK)]
        return jnp.concatenate(parts, axis=0)

    def gate_a(hd, mixed):
        lo = hd * A_HEAD_DIM
        u = _gelu_tanh(z_read[:, OFF_U + lo:OFF_U + lo + A_HEAD_DIM])
        g = _silu(z_read[:, OFF_GA + lo:OFF_GA + lo + A_HEAD_DIM])
        pk_ref[0, rows, PK_A + lo:PK_A + lo + A_HEAD_DIM] = (u * mixed * g).astype(_BF16)

    def gate_b(lo, width):
        gb_ref[0, rows, lo:lo + width] = _silu(z_read[:, OFF_GB + lo:OFF_GB + lo + width])

    def rope_q(qa, hd):
        base = hd * QK_WIDTH
        pair = qa[:, base + KV_RANK:base + QK_WIDTH]
        rope = pair * t1 + pltpu.roll(pair, QK_ROPE, 1) * t2q
        pk_ref[0, rows, PK_Q + base:PK_Q + base + QK_WIDTH] = jnp.concatenate(
            [qa[:, base:base + KV_RANK], rope], axis=1).astype(_BF16)

    project(0)
    cq = _rmsnorm(z_read[:, OFF_CQ:OFF_CQ + Q_RANK],
                  qg_ref[...] * (ATTN_SCALE * LOG2_E)).astype(_BF16)
    vn0 = vn_head(0)
    project(1)
    vn1 = vn_head(1)
    gate_b(0, B_WIDTH // 2)
    project(2)
    vn2 = vn_head(2)
    gate_b(B_WIDTH // 2, B_WIDTH // 2)
    project(3)
    vn3 = vn_head(3)
    ckv = _rmsnorm(z_read[:, OFF_CKV:OFF_CKV + KV_RANK], kvg_ref[...])
    kpair = z_read[:, OFF_KR:OFF_KR + 2 * QK_ROPE]
    krot = kpair * t1 + pltpu.roll(kpair, QK_ROPE, 1) * t2k
    pk_ref[0, rows, PK_K:PK_K + QK_WIDTH] = jnp.concatenate([ckv, krot], axis=1).astype(_BF16)
    pk_ref[0, rows, PK_V:PK_V + 2 * KV_RANK] = jnp.concatenate(
        [ckv, jnp.ones_like(ckv)], axis=1).astype(_BF16)
    if emit_cache:
        ckv_ref[0, 0, rows, :] = ckv
        kr_ref[0, 0, rows, :] = kpair[:, 0:QK_ROPE]
    qa = jnp.dot(cq, wq_ref[...], preferred_element_type=_F32)
    norm_rows(0)
    project(4)
    rope_q(qa, 0)
    rope_q(qa, 1)
    mixed0 = mix_head(0, vn0)
    mixed1 = mix_head(1, vn1)
    rope_q(qa, 2)
    rope_q(qa, 3)
    project(5)
    gate_a(0, mixed0)
    mixed2 = mix_head(2, vn2)
    mixed3 = mix_head(3, vn3)
    project(6)
    gate_a(1, mixed1)
    norm_rows(half)
    project(7)
    gate_a(2, mixed2)
    project(8)
    gate_a(3, mixed3)
    project(9)


def _front_body(x_ref, mod_ref, ng_ref, win_ref, ws_ref, bs_ref, gv_ref, qg_ref, wq_ref,
                kvg_ref, t1_ref, t2k_ref, t2q_ref, *rest, emit_cache, tiles_per_row):
    out_refs, (h_a, h_b, z_a, z_b) = rest[:-4], rest[-4:]
    t = pl.program_id(0)
    tile = x_ref.shape[1]
    assert tile == 2 * SUB_TILE
    n_pos = t1_ref.shape[0]

    @pl.when(t == 0)
    def _():
        h_b[...] = jnp.zeros(h_b.shape, _BF16)
        z_b[...] = jnp.zeros(z_b.shape, _F32)

    pos_base = (jnp.maximum(t - 1, 0) % tiles_per_row) * tile
    for sub, (h_write, h_read, z_write, z_read) in enumerate(
            ((h_a, h_b, z_a, z_b), (h_b, h_a, z_b, z_a))):
        row0 = sub * SUB_TILE
        pos = pl.multiple_of((pos_base + row0) % n_pos, SUB_TILE)

        @pl.when(t >= -sub)
        def _(row0=row0, pos=pos, bufs=(h_write, h_read, z_write, z_read)):
            tables = tuple(r[pl.ds(pos, SUB_TILE), :] for r in (t1_ref, t2k_ref, t2q_ref))
            _front_substep(x_ref, mod_ref, ng_ref, win_ref, ws_ref, bs_ref, gv_ref, qg_ref,
                           wq_ref, kvg_ref, tables, out_refs, row0, *bufs, emit_cache)


def _front_call(x, mod, mod_index, norm_g, win2, ws_bf, bs_t, g_v, q_norm_g, wq2, kv_norm_g,
                tables, emit_cache):
    batch, seq, _ = x.shape
    tile = 2 * SUB_TILE
    tiles_per_row = seq // tile
    n_tiles = batch * tiles_per_row
    t1, t2k, t2q = tables

    def in_tile(t):
        tt = jnp.minimum(t, n_tiles - 1)
        return tt // tiles_per_row, tt % tiles_per_row

    def out_tile(t):
        tt = jnp.maximum(t - 1, 0)
        return tt // tiles_per_row, tt % tiles_per_row

    full = lambda shape: pl.BlockSpec(shape, lambda t: (0,) * len(shape))
    in_specs = [
        pl.BlockSpec((1, tile, D_MODEL), lambda t: (*in_tile(t), 0)),
        pl.BlockSpec((1, 1, 3 * D_MODEL), lambda t: (mod_index(in_tile(t)[0]), 0, 0)),
        full((1, D_MODEL)),
        full((D_MODEL, IN_WIDTH2)),
        full((A_HEADS, CHUNK, CHUNK)),
        full((CHUNK, A_HEADS)),
        full((1, A_WIDTH)),
        full((1, Q_RANK)),
        full((Q_RANK, MLA_HEADS * QK_WIDTH)),
        full((1, KV_RANK)),
        full(t1.shape), full(t2k.shape), full(t2q.shape),
    ]
    out_shape = [
        jax.ShapeDtypeStruct((batch, seq, PK_WIDTH), _BF16),
        jax.ShapeDtypeStruct((batch, seq, B_WIDTH), _F32),
    ]
    out_specs = [
        pl.BlockSpec((1, tile, PK_WIDTH), lambda t: (*out_tile(t), 0)),
        pl.BlockSpec((1, tile, B_WIDTH), lambda t: (*out_tile(t), 0)),
    ]
    if emit_cache:
        out_shape += [
            jax.ShapeDtypeStruct((batch, 1, seq, KV_RANK), _F32),
            jax.ShapeDtypeStruct((batch, 1, seq, QK_ROPE), _F32),
        ]
        out_specs += [
            pl.BlockSpec((1, 1, tile, KV_RANK), lambda t: (out_tile(t)[0], 0, out_tile(t)[1], 0)),
            pl.BlockSpec((1, 1, tile, QK_ROPE), lambda t: (out_tile(t)[0], 0, out_tile(t)[1], 0)),
        ]
    return pl.pallas_call(
        functools.partial(_front_body, emit_cache=emit_cache, tiles_per_row=tiles_per_row),
        grid=(n_tiles + 1,),
        in_specs=in_specs,
        out_specs=out_specs,
        out_shape=out_shape,
        scratch_shapes=[pltpu.VMEM((SUB_TILE, D_MODEL), _BF16), pltpu.VMEM((SUB_TILE, D_MODEL), _BF16),
                        pltpu.VMEM((SUB_TILE, IN_WIDTH2), _F32),
                        pltpu.VMEM((SUB_TILE, IN_WIDTH2), _F32)],
        compiler_params=pltpu.CompilerParams(
            dimension_semantics=("arbitrary",),
            vmem_limit_bytes=VMEM_LIMIT_BYTES),
        name="front_ctx" if emit_cache else "front_lat",
    )(x, mod, norm_g, win2, ws_bf, bs_t, g_v, q_norm_g, wq2, kv_norm_g, t1, t2k, t2q)


def _scores(q, k_refs):
    return [lax.dot_general(q, k_ref[0], _NT_DIMS, preferred_element_type=_F32)
            for k_ref in k_refs]


def _softmax_values(scores, v_refs):
    m = functools.reduce(jnp.maximum, [jnp.max(s, axis=1, keepdims=True) for s in scores])
    acc = None
    for s, v_ref in zip(scores, v_refs):
        p = jnp.exp2(s - m).astype(_BF16)
        part = jnp.dot(p, v_ref[0], preferred_element_type=_F32)
        acc = part if acc is None else acc + part
    return acc[:, 0:KV_RANK] / acc[:, KV_RANK:2 * KV_RANK]


def _back_body(*refs, has_ctx):
    if has_ctx:
        (q_ref, klat_ref, vlat_ref, kctx_ref, vctx_ref, a_ref, gb_ref, x_ref, mod_ref, wuv_ref,
         wo_ref, fg_ref, y_ref) = refs
        k_refs, v_refs = (kctx_ref, klat_ref), (vctx_ref, vlat_ref)
    else:
        (q_ref, klat_ref, vlat_ref, a_ref, gb_ref, x_ref, mod_ref, wuv_ref, wo_ref, fg_ref,
         y_ref) = refs
        k_refs, v_refs = (klat_ref,), (vlat_ref,)
    tq = q_ref.shape[1]

    groups = [
        jnp.concatenate([q_ref[0, :, hd * QK_WIDTH:(hd + 1) * QK_WIDTH] for hd in (2 * g, 2 * g + 1)],
                        axis=0) for g in range(MLA_HEADS // 2)]
    scores = [_scores(qg, k_refs) for qg in groups]
    outs = [_softmax_values(s, v_refs) for s in scores]

    attn_cols = []
    for pair, o in enumerate(outs):
        o2 = jnp.concatenate([o[0:tq, :], o[tq:2 * tq, :]], axis=1)
        attn_cols.append(jnp.dot(o2.astype(_BF16), wuv_ref[pair], preferred_element_type=_F32))
    attn = jnp.concatenate(attn_cols, axis=1) * gb_ref[0]

    y = jnp.dot(a_ref[0], wo_ref[0:A_WIDTH, :], preferred_element_type=_F32)
    y = y + jnp.dot(attn.astype(_BF16), wo_ref[A_WIDTH:A_WIDTH + B_WIDTH, :],
                    preferred_element_type=_F32)
    gate = mod_ref[0, :, 2 * D_MODEL:3 * D_MODEL]
    out = x_ref[0] + gate * y
    y_ref[0] = _rmsnorm(out, fg_ref[...])


def _back_call(packed, kctx, vctx, gb, x, mod, mod_index, wuv2, wo_bf, final_g):
    batch, seq, _ = x.shape
    tq = min(TOKEN_TILE, seq)
    has_ctx = kctx is not None
    full = lambda shape: pl.BlockSpec(shape, lambda b, i: (0,) * len(shape))
    keys = lambda arr: pl.BlockSpec((1, arr.shape[1], arr.shape[2]), lambda b, i: (b, 0, 0))
    in_specs = [
        pl.BlockSpec((1, tq, MLA_HEADS * QK_WIDTH), lambda b, i: (b, i, PK_Q // (MLA_HEADS * QK_WIDTH))),
        pl.BlockSpec((1, seq, QK_WIDTH), lambda b, i: (b, 0, PK_K // QK_WIDTH)),
        pl.BlockSpec((1, seq, 2 * KV_RANK), lambda b, i: (b, 0, PK_V // (2 * KV_RANK))),
    ]
    args = [packed, packed, packed]
    if has_ctx:
        in_specs += [keys(kctx), keys(vctx)]
        args += [kctx, vctx]
    in_specs += [
        pl.BlockSpec((1, tq, A_WIDTH), lambda b, i: (b, i, PK_A // A_WIDTH)),
        pl.BlockSpec((1, tq, B_WIDTH), lambda b, i: (b, i, 0)),
        pl.BlockSpec((1, tq, D_MODEL), lambda b, i: (b, i, 0)),
        pl.BlockSpec((1, 1, 3 * D_MODEL), lambda b, i: (mod_index(b), 0, 0)),
        full((MLA_HEADS // 2, 2 * KV_RANK, 2 * V_HEAD)),
        full((D_MODEL, D_MODEL)),
        full((1, D_MODEL)),
    ]
    args += [packed, gb, x, mod, wuv2, wo_bf, final_g]
    return pl.pallas_call(
        functools.partial(_back_body, has_ctx=has_ctx),
        grid=(batch, seq // tq),
        in_specs=in_specs,
        out_specs=pl.BlockSpec((1, tq, D_MODEL), lambda b, i: (b, i, 0)),
        out_shape=jax.ShapeDtypeStruct((batch, seq, D_MODEL), _F32),
        compiler_params=pltpu.CompilerParams(
            dimension_semantics=("arbitrary", "arbitrary"),
            vmem_limit_bytes=VMEM_LIMIT_BYTES),
        name="back_lat" if has_ctx else "back_ctx",
    )(*args)


def _swap_halves(w):
    half = QK_ROPE // 2
    return jnp.concatenate([-w[..., half:], w[..., :half]], axis=-1)


def _rope_tables(n_tokens, rotate):
    ones = jnp.ones((n_tokens, QK_ROPE), _F32)
    zeros = jnp.zeros((n_tokens, QK_ROPE), _F32)
    if rotate:
        rows = n_tokens // GRID_W
        inv = ROPE_THETA ** (-jnp.arange(AXIS_PAIRS, dtype=_F32) / AXIS_PAIRS)
        ang_r = jnp.arange(rows, dtype=_F32)[:, None] * inv
        ang_c = jnp.arange(GRID_W, dtype=_F32)[:, None] * inv

        def grid_table(fn):
            tr = jnp.broadcast_to(fn(ang_r)[:, None, :], (rows, GRID_W, AXIS_PAIRS))
            tc = jnp.broadcast_to(fn(ang_c)[None, :, :], (rows, GRID_W, AXIS_PAIRS))
            half = jnp.concatenate([tr, tc], axis=-1).reshape(n_tokens, 2 * AXIS_PAIRS)
            return jnp.concatenate([half, half], axis=-1)

        cos, sin = grid_table(jnp.cos), grid_table(jnp.sin)
    else:
        cos, sin = ones, zeros
    t1 = jnp.concatenate([cos, zeros], axis=-1)
    t2k = jnp.concatenate([sin, zeros], axis=-1)
    t2q = jnp.concatenate([sin, ones], axis=-1)
    return t1, t2k, t2q


def kernel(x_prompt, x_sample, cache_ckv, cache_krope, c, c_ctx, norm_g, w_ada, b_ada, w_in, w_s,
           b_s, g_v, q_norm_g, w_uq, kv_norm_g, w_ukv, w_o, final_g):
    depth = norm_g.shape[0]
    assert depth == 1 and w_in.shape[2] == IN_WIDTH
    dec_batch = x_sample.shape[0]
    xp, xs = x_prompt, x_sample
    new_ckv, new_kr = [], []
    for l in range(depth):
        cond = jnp.concatenate(
            [c, c_ctx[None, :], jnp.zeros((16 - dec_batch - 1, D_MODEL), _F32)], axis=0)
        mod = _mod_call(cond, w_ada[l], b_ada[l]).reshape(16, 1, 3 * D_MODEL)

        win2 = _prep_call(jnp.transpose(w_in[l]))
        wq3 = w_uq[l].reshape(Q_RANK, MLA_HEADS, QK_NOPE + QK_ROPE)
        wkv3 = w_ukv[l].reshape(KV_RANK, MLA_HEADS, QK_NOPE + V_HEAD)
        wq_nope = jnp.transpose(wq3[:, :, :QK_NOPE], (1, 0, 2))
        w_uk = jnp.transpose(wkv3[:, :, :QK_NOPE], (1, 0, 2))
        w_uv = jnp.transpose(wkv3[:, :, QK_NOPE:], (1, 0, 2))
        wq_abs = _fold_call(wq_nope, w_uk)
        wq_rope = jnp.transpose(wq3[:, :, QK_NOPE:], (1, 0, 2))
        wq2 = jnp.concatenate([wq_abs, wq_rope, _swap_halves(wq_rope)], axis=-1)
        wq2 = jnp.transpose(wq2, (1, 0, 2)).reshape(Q_RANK, MLA_HEADS * QK_WIDTH).astype(_BF16)
        zero_blk = jnp.zeros((KV_RANK, V_HEAD), _F32)
        wuv2 = jnp.stack([
            jnp.block([[w_uv[2 * p], zero_blk], [zero_blk, w_uv[2 * p + 1]]])
            for p in range(MLA_HEADS // 2)]).astype(_BF16)
        wo_bf = w_o[l].astype(_BF16)
        ws_bf = w_s[l].astype(_BF16)
        bs_t = b_s[l].T
        gv_row = g_v[l].reshape(1, A_WIDTH)
        shared = (norm_g[l].reshape(1, D_MODEL), win2, ws_bf, bs_t, gv_row,
                  q_norm_g[l].reshape(1, Q_RANK), wq2, kv_norm_g[l].reshape(1, KV_RANK))
        fg = final_g.reshape(1, D_MODEL)

        ctx_index = lambda b: dec_batch
        ctx_batch, ctx_seq, _ = xp.shape
        pack = 2 * SUB_TILE // ctx_seq
        pk_c, gb_c, ckv_c, kr_c = _front_call(
            xp.reshape(ctx_batch // pack, pack * ctx_seq, D_MODEL), mod, ctx_index, *shared,
            _rope_tables(ctx_seq, False), True)
        unpack = lambda arr: arr.reshape(ctx_batch, ctx_seq, arr.shape[-1])
        xp = _back_call(unpack(pk_c), None, None, unpack(gb_c), xp, mod, ctx_index, wuv2, wo_bf,
                        fg)
        new_ckv.append(ckv_c.reshape(ctx_batch, 1, ctx_seq, KV_RANK))
        new_kr.append(kr_c.reshape(ctx_batch, 1, ctx_seq, QK_ROPE))

        lat_index = lambda b: b
        pk_s, gb_s = _front_call(
            xs, mod, lat_index, *shared, _rope_tables(xs.shape[1], True), False)
        cache_k = jnp.concatenate(
            [cache_ckv[:, l], jnp.zeros(cache_krope[:, l].shape, _F32), cache_krope[:, l]],
            axis=-1).astype(_BF16)
        cache_v = jnp.concatenate(
            [cache_ckv[:, l], jnp.ones(cache_ckv[:, l].shape, _F32)], axis=-1).astype(_BF16)
        xs = _back_call(pk_s, cache_k, cache_v, gb_s, xs, mod, lat_index, wuv2, wo_bf, fg)
    return (xp, xs, jnp.concatenate(new_ckv, axis=1), jnp.concatenate(new_kr, axis=1))
```

```python
import functools
import math

import jax
import jax.numpy as jnp
from jax import lax
from jax.experimental import pallas as pl
from jax.experimental.pallas import tpu as pltpu

D_MODEL = 1024
GRID_W = 64
EPS = 1e-6
A_HEADS = 4
A_HEAD_DIM = 128
A_WIDTH = A_HEADS * A_HEAD_DIM
CHUNK = 128
MLA_HEADS = 4
QK_NOPE = 128
QK_ROPE = 64
V_HEAD = 128
B_WIDTH = MLA_HEADS * V_HEAD
Q_RANK = 256
KV_RANK = 128
AXIS_PAIRS = QK_ROPE // 4
ROPE_THETA = 10000.0
ATTN_SCALE = 1.0 / math.sqrt(QK_NOPE + QK_ROPE)
LOG2_E = 1.4426950408889634

QK_WIDTH = KV_RANK + 2 * QK_ROPE
OFF_U = 0
OFF_V = OFF_U + A_WIDTH
OFF_GA = OFF_V + A_WIDTH
OFF_CQ = OFF_GA + A_WIDTH
OFF_CKV = OFF_CQ + Q_RANK
OFF_KR = OFF_CKV + KV_RANK
OFF_GB = OFF_KR + 2 * QK_ROPE
IN_WIDTH = OFF_GB - QK_ROPE + B_WIDTH
IN_WIDTH2 = OFF_GB + B_WIDTH

PK_A = 0
PK_K = PK_A + A_WIDTH
PK_V = PK_K + QK_WIDTH
PK_Q = PK_V + 2 * KV_RANK
PK_WIDTH = PK_Q + MLA_HEADS * QK_WIDTH

LANES = 128
SUB_TILE = 256
TOKEN_TILE = 512
PROJ_BLOCK = 256
VMEM_LIMIT_BYTES = 56 * 1024 * 1024

_F32 = jnp.float32
_BF16 = jnp.bfloat16
_NT_DIMS = (((1,), (1,)), ((), ()))


def _silu(x):
    hx = 0.5 * x
    return hx + hx * jnp.tanh(hx)


def _gelu_tanh(x):
    return x * (0.5 * (1.0 + jnp.tanh(0.7978845608028654 * (x + 0.044715 * (x * x * x)))))


def _rmsnorm(x, g):
    ms = jnp.mean(x * x, axis=-1, keepdims=True)
    return (x * lax.rsqrt(ms + EPS)) * g


def _mod_body(cond_ref, w_ref, b_ref, o_ref):
    s = _silu(cond_ref[...])
    o_ref[...] = jnp.dot(s.astype(_BF16), w_ref[...].astype(_BF16),
                         preferred_element_type=_F32) + b_ref[...]


def _mod_call(cond, w_ada, b_ada):
    rows = cond.shape[0]
    n_out = w_ada.shape[1]
    col_block = 512
    return pl.pallas_call(
        _mod_body,
        grid=(n_out // col_block,),
        in_specs=[
            pl.BlockSpec((rows, D_MODEL), lambda j: (0, 0)),
            pl.BlockSpec((D_MODEL, col_block), lambda j: (0, j)),
            pl.BlockSpec((1, col_block), lambda j: (0, j)),
        ],
        out_specs=pl.BlockSpec((rows, col_block), lambda j: (0, j)),
        out_shape=jax.ShapeDtypeStruct((rows, n_out), _F32),
        name="adaln_mod",
    )(cond, w_ada, b_ada.reshape(1, n_out))


def _fold_body(wq_ref, wk_ref, o_ref):
    o_ref[0] = lax.dot_general(wq_ref[0], wk_ref[0], _NT_DIMS,
                               precision=lax.Precision.HIGHEST,
                               preferred_element_type=_F32)


def _fold_call(wq_nope, w_uk):
    return pl.pallas_call(
        _fold_body,
        grid=(MLA_HEADS,),
        in_specs=[
            pl.BlockSpec((1, Q_RANK, QK_NOPE), lambda h: (h, 0, 0)),
            pl.BlockSpec((1, KV_RANK, QK_NOPE), lambda h: (h, 0, 0)),
        ],
        out_specs=pl.BlockSpec((1, Q_RANK, KV_RANK), lambda h: (h, 0, 0)),
        out_shape=jax.ShapeDtypeStruct((MLA_HEADS, Q_RANK, KV_RANK), _F32),
        name="fold_q_uk",
    )(wq_nope, w_uk)


_KR_BLOCK = OFF_KR // LANES


_PREP_SPLIT = IN_WIDTH2 // 2


def _prep_body(w_ref, o_ref):
    half = QK_ROPE // 2
    blocks_per_step = _PREP_SPLIT // LANES

    def emit(step):
        base = step * _PREP_SPLIT
        for j in range(step * blocks_per_step, (step + 1) * blocks_per_step):
            if j < _KR_BLOCK:
                blk = w_ref[j * LANES - base:(j + 1) * LANES - base, :]
            elif j == _KR_BLOCK:
                kr = w_ref[OFF_KR - base:OFF_KR - base + QK_ROPE, :]
                partner = jnp.concatenate([-kr[half:, :], kr[:half, :]], axis=0)
                blk = jnp.concatenate([kr, partner], axis=0)
            else:
                lo = j * LANES - QK_ROPE - base
                blk = w_ref[lo:lo + LANES, :]
            o_ref[:, j * LANES - base:(j + 1) * LANES - base] = blk.T.astype(_BF16)

    for step in range(IN_WIDTH2 // _PREP_SPLIT):
        pl.when(pl.program_id(0) == step)(functools.partial(emit, step))


def _prep_call(w_in_t):
    assert _KR_BLOCK * LANES == OFF_KR and OFF_KR >= _PREP_SPLIT
    return pl.pallas_call(
        _prep_body,
        grid=(IN_WIDTH2 // _PREP_SPLIT,),
        in_specs=[pl.BlockSpec((_PREP_SPLIT, D_MODEL), lambda s: (s, 0))],
        out_specs=pl.BlockSpec((D_MODEL, _PREP_SPLIT), lambda s: (0, s)),
        out_shape=jax.ShapeDtypeStruct((D_MODEL, IN_WIDTH2), _BF16),
        compiler_params=pltpu.CompilerParams(vmem_limit_bytes=VMEM_LIMIT_BYTES),
        name="prep_w_in",
    )(w_in_t)


def _front_substep(x_ref, mod_ref, ng_ref, win_ref, ws_ref, bs_ref, gv_ref, qg_ref, wq_ref,
                   kvg_ref, tables, out_refs, row0, h_write, h_read, z_write, z_read, emit_cache):
    if emit_cache:
        pk_ref, gb_ref, ckv_ref, kr_ref = out_refs
    else:
        pk_ref, gb_ref = out_refs
    t1, t2k, t2q = tables
    rows = slice(row0, row0 + SUB_TILE)
    half = SUB_TILE // 2
    assert IN_WIDTH2 == 10 * PROJ_BLOCK

    def project(j):
        lo = j * PROJ_BLOCK
        z_write[:, lo:lo + PROJ_BLOCK] = jnp.dot(h_read[...], win_ref[:, lo:lo + PROJ_BLOCK],
                                                 preferred_element_type=_F32)

    shift = mod_ref[0, :, 0:D_MODEL]
    gain = ng_ref[...] * (1.0 + mod_ref[0, :, D_MODEL:2 * D_MODEL])

    def norm_rows(lo):
        x = x_ref[0, row0 + lo:row0 + lo + half, :]
        ms = jnp.mean(x * x, axis=-1, keepdims=True)
        h_write[lo:lo + half, :] = ((x * lax.rsqrt(ms + EPS)) * gain + shift).astype(_BF16)

    def vn_head(hd):
        lo = hd * A_HEAD_DIM
        v = _gelu_tanh(z_read[:, OFF_V + lo:OFF_V + lo + A_HEAD_DIM])
        return _rmsnorm(v, gv_ref[:, lo:lo + A_HEAD_DIM]).astype(_BF16)

    def mix_head(hd, vn):
        bias = bs_ref[:, hd:hd + 1]
        parts = [jnp.dot(ws_ref[hd], vn[c * CHUNK:(c + 1) * CHUNK, :],
                         preferred_element_type=_F32) + bias for c in range(SUB_TILE // CHUNK)]
        return jnp.concatenate(parts, axis=0)

    def gate_a(hd, mixed):
        lo = hd * A_HEAD_DIM
        u = _gelu_tanh(z_read[:, OFF_U + lo:OFF_U + lo + A_HEAD_DIM])
        g = _silu(z_read[:, OFF_GA + lo:OFF_GA + lo + A_HEAD_DIM])
        pk_ref[0, rows, PK_A + lo:PK_A + lo + A_HEAD_DIM] = (u * mixed * g).astype(_BF16)

    def gate_b(lo, width):
        gb_ref[0, rows, lo:lo + width] = _silu(z_read[:, OFF_GB + lo:OFF_GB + lo + width])

    def rope_q(qa, hd):
        base = hd * QK_WIDTH
        pair = qa[:, base + KV_RANK:base + QK_WIDTH]
        rope = pair * t1 + pltpu.roll(pair, QK_ROPE, 1) * t2q
        pk_ref[0, rows, PK_Q + base:PK_Q + base + QK_WIDTH] = jnp.concatenate(
            [qa[:, base:base + KV_RANK], rope], axis=1).astype(_BF16)

    project(0)
    cq = _rmsnorm(z_read[:, OFF_CQ:OFF_CQ + Q_RANK],
                  qg_ref[...] * (ATTN_SCALE * LOG2_E)).astype(_BF16)
    vn0 = vn_head(0)
    project(1)
    vn1 = vn_head(1)
    gate_b(0, B_WIDTH // 2)
    project(2)
    vn2 = vn_head(2)
    gate_b(B_WIDTH // 2, B_WIDTH // 2)
    project(3)
    vn3 = vn_head(3)
    ckv = _rmsnorm(z_read[:, OFF_CKV:OFF_CKV + KV_RANK], kvg_ref[...])
    kpair = z_read[:, OFF_KR:OFF_KR + 2 * QK_ROPE]
    krot = kpair * t1 + pltpu.roll(kpair, QK_ROPE, 1) * t2k
    pk_ref[0, rows, PK_K:PK_K + QK_WIDTH] = jnp.concatenate([ckv, krot], axis=1).astype(_BF16)
    pk_ref[0, rows, PK_V:PK_V + 2 * KV_RANK] = jnp.concatenate(
        [ckv, jnp.ones_like(ckv)], axis=1).astype(_BF16)
    if emit_cache:
        ckv_ref[0, 0, rows, :] = ckv
        kr_ref[0, 0, rows, :] = kpair[:, 0:QK_ROPE]
    qa = jnp.dot(cq, wq_ref[...], preferred_element_type=_F32)
    norm_rows(0)
    project(4)
    rope_q(qa, 0)
    rope_q(qa, 1)
    mixed0 = mix_head(0, vn0)
    mixed1 = mix_head(1, vn1)
    rope_q(qa, 2)
    rope_q(qa, 3)
    project(5)
    gate_a(0, mixed0)
    mixed2 = mix_head(2, vn2)
    mixed3 = mix_head(3, vn3)
    project(6)
    gate_a(1, mixed1)
    norm_rows(half)
    project(7)
    gate_a(2, mixed2)
    project(8)
    gate_a(3, mixed3)
    project(9)


def _front_body(x_ref, mod_ref, ng_ref, win_ref, ws_ref, bs_ref, gv_ref, qg_ref, wq_ref,
                kvg_ref, t1_ref, t2k_ref, t2q_ref, *rest, emit_cache, tiles_per_row):
    out_refs, (h_a, h_b, z_a, z_b) = rest[:-4], rest[-4:]
    t = pl.program_id(0)
    tile = x_ref.shape[1]
    assert tile == 2 * SUB_TILE
    n_pos = t1_ref.shape[0]

    @pl.when(t == 0)
    def _():
        h_b[...] = jnp.zeros(h_b.shape, _BF16)
        z_b[...] = jnp.zeros(z_b.shape, _F32)

    pos_base = (jnp.maximum(t - 1, 0) % tiles_per_row) * tile
    for sub, (h_write, h_read, z_write, z_read) in enumerate(
            ((h_a, h_b, z_a, z_b), (h_b, h_a, z_b, z_a))):
        row0 = sub * SUB_TILE
        pos = pl.multiple_of((pos_base + row0) % n_pos, SUB_TILE)
        tables = tuple(r[pl.ds(pos, SUB_TILE), :] for r in (t1_ref, t2k_ref, t2q_ref))
        _front_substep(x_ref, mod_ref, ng_ref, win_ref, ws_ref, bs_ref, gv_ref, qg_ref, wq_ref,
                       kvg_ref, tables, out_refs, row0, h_write, h_read, z_write, z_read,
                       emit_cache)


def _front_call(x, mod, mod_index, norm_g, win2, ws_bf, bs_t, g_v, q_norm_g, wq2, kv_norm_g,
                tables, emit_cache):
    batch, seq, _ = x.shape
    tile = 2 * SUB_TILE
    tiles_per_row = seq // tile
    n_tiles = batch * tiles_per_row
    t1, t2k, t2q = tables

    def in_tile(t):
        tt = jnp.minimum(t, n_tiles - 1)
        return tt // tiles_per_row, tt % tiles_per_row

    def out_tile(t):
        tt = jnp.maximum(t - 1, 0)
        return tt // tiles_per_row, tt % tiles_per_row

    full = lambda shape: pl.BlockSpec(shape, lambda t: (0,) * len(shape))
    in_specs = [
        pl.BlockSpec((1, tile, D_MODEL), lambda t: (*in_tile(t), 0)),
        pl.BlockSpec((1, 1, 3 * D_MODEL), lambda t: (mod_index(in_tile(t)[0]), 0, 0)),
        full((1, D_MODEL)),
        full((D_MODEL, IN_WIDTH2)),
        full((A_HEADS, CHUNK, CHUNK)),
        full((CHUNK, A_HEADS)),
        full((1, A_WIDTH)),
        full((1, Q_RANK)),
        full((Q_RANK, MLA_HEADS * QK_WIDTH)),
        full((1, KV_RANK)),
        full(t1.shape), full(t2k.shape), full(t2q.shape),
    ]
    out_shape = [
        jax.ShapeDtypeStruct((batch, seq, PK_WIDTH), _BF16),
        jax.ShapeDtypeStruct((batch, seq, B_WIDTH), _F32),
    ]
    out_specs = [
        pl.BlockSpec((1, tile, PK_WIDTH), lambda t: (*out_tile(t), 0)),
        pl.BlockSpec((1, tile, B_WIDTH), lambda t: (*out_tile(t), 0)),
    ]
    if emit_cache:
        out_shape += [
            jax.ShapeDtypeStruct((batch, 1, seq, KV_RANK), _F32),
            jax.ShapeDtypeStruct((batch, 1, seq, QK_ROPE), _F32),
        ]
        out_specs += [
            pl.BlockSpec((1, 1, tile, KV_RANK), lambda t: (out_tile(t)[0], 0, out_tile(t)[1], 0)),
            pl.BlockSpec((1, 1, tile, QK_ROPE), lambda t: (out_tile(t)[0], 0, out_tile(t)[1], 0)),
        ]
    return pl.pallas_call(
        functools.partial(_front_body, emit_cache=emit_cache, tiles_per_row=tiles_per_row),
        grid=(n_tiles + 1,),
        in_specs=in_specs,
        out_specs=out_specs,
        out_shape=out_shape,
        scratch_shapes=[pltpu.VMEM((SUB_TILE, D_MODEL), _BF16), pltpu.VMEM((SUB_TILE, D_MODEL), _BF16),
                        pltpu.VMEM((SUB_TILE, IN_WIDTH2), _F32),
                        pltpu.VMEM((SUB_TILE, IN_WIDTH2), _F32)],
        compiler_params=pltpu.CompilerParams(
            dimension_semantics=("arbitrary",),
            vmem_limit_bytes=VMEM_LIMIT_BYTES),
        name="front_ctx" if emit_cache else "front_lat",
    )(x, mod, norm_g, win2, ws_bf, bs_t, g_v, q_norm_g, wq2, kv_norm_g, t1, t2k, t2q)


def _scores(q, k_refs):
    return [lax.dot_general(q, k_ref[0], _NT_DIMS, preferred_element_type=_F32)
            for k_ref in k_refs]


def _softmax_values(scores, v_refs):
    m = functools.reduce(jnp.maximum, [jnp.max(s, axis=1, keepdims=True) for s in scores])
    acc = None
    for s, v_ref in zip(scores, v_refs):
        p = jnp.exp2(s - m).astype(_BF16)
        part = jnp.dot(p, v_ref[0], preferred_element_type=_F32)
        acc = part if acc is None else acc + part
    return acc[:, 0:KV_RANK] / acc[:, KV_RANK:2 * KV_RANK]


def _back_body(*refs, has_ctx):
    if has_ctx:
        (q_ref, klat_ref, vlat_ref, kctx_ref, vctx_ref, a_ref, gb_ref, x_ref, mod_ref, wuv_ref,
         wo_ref, fg_ref, y_ref) = refs
        k_refs, v_refs = (kctx_ref, klat_ref), (vctx_ref, vlat_ref)
    else:
        (q_ref, klat_ref, vlat_ref, a_ref, gb_ref, x_ref, mod_ref, wuv_ref, wo_ref, fg_ref,
         y_ref) = refs
        k_refs, v_refs = (klat_ref,), (vlat_ref,)
    tq = q_ref.shape[1]

    groups = [
        jnp.concatenate([q_ref[0, :, hd * QK_WIDTH:(hd + 1) * QK_WIDTH] for hd in (2 * g, 2 * g + 1)],
                        axis=0) for g in range(MLA_HEADS // 2)]
    scores = [_scores(qg, k_refs) for qg in groups]
    outs = [_softmax_values(s, v_refs) for s in scores]

    attn_cols = []
    for pair, o in enumerate(outs):
        o2 = jnp.concatenate([o[0:tq, :], o[tq:2 * tq, :]], axis=1)
        attn_cols.append(jnp.dot(o2.astype(_BF16), wuv_ref[pair], preferred_element_type=_F32))
    attn = jnp.concatenate(attn_cols, axis=1) * gb_ref[0]

    y = jnp.dot(a_ref[0], wo_ref[0:A_WIDTH, :], preferred_element_type=_F32)
    y = y + jnp.dot(attn.astype(_BF16), wo_ref[A_WIDTH:A_WIDTH + B_WIDTH, :],
                    preferred_element_type=_F32)
    gate = mod_ref[0, :, 2 * D_MODEL:3 * D_MODEL]
    out = x_ref[0] + gate * y
    y_ref[0] = _rmsnorm(out, fg_ref[...])


def _back_call(packed, kctx, vctx, gb, x, mod, mod_index, wuv2, wo_bf, final_g):
    batch, seq, _ = x.shape
    tq = min(TOKEN_TILE, seq)
    has_ctx = kctx is not None
    full = lambda shape: pl.BlockSpec(shape, lambda b, i: (0,) * len(shape))
    keys = lambda arr: pl.BlockSpec((1, arr.shape[1], arr.shape[2]), lambda b, i: (b, 0, 0))
    in_specs = [
        pl.BlockSpec((1, tq, MLA_HEADS * QK_WIDTH), lambda b, i: (b, i, PK_Q // (MLA_HEADS * QK_WIDTH))),
        pl.BlockSpec((1, seq, QK_WIDTH), lambda b, i: (b, 0, PK_K // QK_WIDTH)),
        pl.BlockSpec((1, seq, 2 * KV_RANK), lambda b, i: (b, 0, PK_V // (2 * KV_RANK))),
    ]
    args = [packed, packed, packed]
    if has_ctx:
        in_specs += [keys(kctx), keys(vctx)]
        args += [kctx, vctx]
    in_specs += [
        pl.BlockSpec((1, tq, A_WIDTH), lambda b, i: (b, i, PK_A // A_WIDTH)),
        pl.BlockSpec((1, tq, B_WIDTH), lambda b, i: (b, i, 0)),
        pl.BlockSpec((1, tq, D_MODEL), lambda b, i: (b, i, 0)),
        pl.BlockSpec((1, 1, 3 * D_MODEL), lambda b, i: (mod_index(b), 0, 0)),
        full((MLA_HEADS // 2, 2 * KV_RANK, 2 * V_HEAD)),
        full((D_MODEL, D_MODEL)),
        full((1, D_MODEL)),
    ]
    args += [packed, gb, x, mod, wuv2, wo_bf, final_g]
    return pl.pallas_call(
        functools.partial(_back_body, has_ctx=has_ctx),
        grid=(batch, seq // tq),
        in_specs=in_specs,
        out_specs=pl.BlockSpec((1, tq, D_MODEL), lambda b, i: (b, i, 0)),
        out_shape=jax.ShapeDtypeStruct((batch, seq, D_MODEL), _F32),
        compiler_params=pltpu.CompilerParams(
            dimension_semantics=("arbitrary", "arbitrary"),
            vmem_limit_bytes=VMEM_LIMIT_BYTES),
        name="back_lat" if has_ctx else "back_ctx",
    )(*args)


def _swap_halves(w):
    half = QK_ROPE // 2
    return jnp.concatenate([-w[..., half:], w[..., :half]], axis=-1)


def _rope_tables(n_tokens, rotate):
    ones = jnp.ones((n_tokens, QK_ROPE), _F32)
    zeros = jnp.zeros((n_tokens, QK_ROPE), _F32)
    if rotate:
        rows = n_tokens // GRID_W
        inv = ROPE_THETA ** (-jnp.arange(AXIS_PAIRS, dtype=_F32) / AXIS_PAIRS)
        ang_r = jnp.arange(rows, dtype=_F32)[:, None] * inv
        ang_c = jnp.arange(GRID_W, dtype=_F32)[:, None] * inv

        def grid_table(fn):
            tr = jnp.broadcast_to(fn(ang_r)[:, None, :], (rows, GRID_W, AXIS_PAIRS))
            tc = jnp.broadcast_to(fn(ang_c)[None, :, :], (rows, GRID_W, AXIS_PAIRS))
            half = jnp.concatenate([tr, tc], axis=-1).reshape(n_tokens, 2 * AXIS_PAIRS)
            return jnp.concatenate([half, half], axis=-1)

        cos, sin = grid_table(jnp.cos), grid_table(jnp.sin)
    else:
        cos, sin = ones, zeros
    t1 = jnp.concatenate([cos, zeros], axis=-1)
    t2k = jnp.concatenate([sin, zeros], axis=-1)
    t2q = jnp.concatenate([sin, ones], axis=-1)
    return t1, t2k, t2q


def kernel(x_prompt, x_sample, cache_ckv, cache_krope, c, c_ctx, norm_g, w_ada, b_ada, w_in, w_s,
           b_s, g_v, q_norm_g, w_uq, kv_norm_g, w_ukv, w_o, final_g):
    depth = norm_g.shape[0]
    assert depth == 1 and w_in.shape[2] == IN_WIDTH
    dec_batch = x_sample.shape[0]
    xp, xs = x_prompt, x_sample
    new_ckv, new_kr = [], []
    for l in range(depth):
        cond = jnp.concatenate(
            [c, c_ctx[None, :], jnp.zeros((16 - dec_batch - 1, D_MODEL), _F32)], axis=0)
        mod = _mod_call(cond, w_ada[l], b_ada[l]).reshape(16, 1, 3 * D_MODEL)

        win2 = _prep_call(jnp.transpose(w_in[l]))
        wq3 = w_uq[l].reshape(Q_RANK, MLA_HEADS, QK_NOPE + QK_ROPE)
        wkv3 = w_ukv[l].reshape(KV_RANK, MLA_HEADS, QK_NOPE + V_HEAD)
        wq_nope = jnp.transpose(wq3[:, :, :QK_NOPE], (1, 0, 2))
        w_uk = jnp.transpose(wkv3[:, :, :QK_NOPE], (1, 0, 2))
        w_uv = jnp.transpose(wkv3[:, :, QK_NOPE:], (1, 0, 2))
        wq_abs = _fold_call(wq_nope, w_uk)
        wq_rope = jnp.transpose(wq3[:, :, QK_NOPE:], (1, 0, 2))
        wq2 = jnp.concatenate([wq_abs, wq_rope, _swap_halves(wq_rope)], axis=-1)
        wq2 = jnp.transpose(wq2, (1, 0, 2)).reshape(Q_RANK, MLA_HEADS * QK_WIDTH).astype(_BF16)
        zero_blk = jnp.zeros((KV_RANK, V_HEAD), _F32)
        wuv2 = jnp.stack([
            jnp.block([[w_uv[2 * p], zero_blk], [zero_blk, w_uv[2 * p + 1]]])
            for p in range(MLA_HEADS // 2)]).astype(_BF16)
        wo_bf = w_o[l].astype(_BF16)
        ws_bf = w_s[l].astype(_BF16)
        bs_t = b_s[l].T
        gv_row = g_v[l].reshape(1, A_WIDTH)
        shared = (norm_g[l].reshape(1, D_MODEL), win2, ws_bf, bs_t, gv_row,
                  q_norm_g[l].reshape(1, Q_RANK), wq2, kv_norm_g[l].reshape(1, KV_RANK))
        fg = final_g.reshape(1, D_MODEL)

        ctx_index = lambda b: dec_batch
        ctx_batch, ctx_seq, _ = xp.shape
        pack = 2 * SUB_TILE // ctx_seq
        pk_c, gb_c, ckv_c, kr_c = _front_call(
            xp.reshape(ctx_batch // pack, pack * ctx_seq, D_MODEL), mod, ctx_index, *shared,
            _rope_tables(ctx_seq, False), True)
        unpack = lambda arr: arr.reshape(ctx_batch, ctx_seq, arr.shape[-1])
        xp = _back_call(unpack(pk_c), None, None, unpack(gb_c), xp, mod, ctx_index, wuv2, wo_bf,
                        fg)
        new_ckv.append(ckv_c.reshape(ctx_batch, 1, ctx_seq, KV_RANK))
        new_kr.append(kr_c.reshape(ctx_batch, 1, ctx_seq, QK_ROPE))

        lat_index = lambda b: b
        pk_s, gb_s = _front_call(
            xs, mod, lat_index, *shared, _rope_tables(xs.shape[1], True), False)
        cache_k = jnp.concatenate(
            [cache_ckv[:, l], jnp.zeros(cache_krope[:, l].shape, _F32), cache_krope[:, l]],
            axis=-1).astype(_BF16)
        cache_v = jnp.concatenate(
            [cache_ckv[:, l], jnp.ones(cache_ckv[:, l].shape, _F32)], axis=-1).astype(_BF16)
        xs = _back_call(pk_s, cache_k, cache_v, gb_s, xs, mod, lat_index, wuv2, wo_bf, fg)
    return (xp, xs, jnp.concatenate(new_ckv, axis=1), jnp.concatenate(new_kr, axis=1))
```

```python
import functools
import math

import jax
import jax.numpy as jnp
from jax import lax
from jax.experimental import pallas as pl
from jax.experimental.pallas import tpu as pltpu

D_MODEL = 1024
GRID_W = 64
EPS = 1e-6
A_HEADS = 4
A_HEAD_DIM = 128
A_WIDTH = A_HEADS * A_HEAD_DIM
CHUNK = 128
MLA_HEADS = 4
QK_NOPE = 128
QK_ROPE = 64
V_HEAD = 128
B_WIDTH = MLA_HEADS * V_HEAD
Q_RANK = 256
KV_RANK = 128
AXIS_PAIRS = QK_ROPE // 4
ROPE_THETA = 10000.0
ATTN_SCALE = 1.0 / math.sqrt(QK_NOPE + QK_ROPE)
LOG2_E = 1.4426950408889634

QK_WIDTH = KV_RANK + 2 * QK_ROPE
OFF_U = 0
OFF_V = OFF_U + A_WIDTH
OFF_GA = OFF_V + A_WIDTH
OFF_CQ = OFF_GA + A_WIDTH
OFF_CKV = OFF_CQ + Q_RANK
OFF_KR = OFF_CKV + KV_RANK
OFF_GB = OFF_KR + 2 * QK_ROPE
IN_WIDTH = OFF_GB - QK_ROPE + B_WIDTH
IN_WIDTH2 = OFF_GB + B_WIDTH

KV_WIDTH = QK_WIDTH + 2 * KV_RANK
Q_WIDTH = MLA_HEADS * QK_WIDTH
QA_WIDTH = Q_WIDTH + A_WIDTH

LANES = 128
SUB_TILE = 256
TOKEN_TILE = 512
PROJ_BLOCK = 256
VMEM_LIMIT_BYTES = 56 * 1024 * 1024

_F32 = jnp.float32
_BF16 = jnp.bfloat16
_NT_DIMS = (((1,), (1,)), ((), ()))


def _silu(x):
    hx = 0.5 * x
    return hx + hx * jnp.tanh(hx)


def _gelu_tanh(x):
    return x * (0.5 * (1.0 + jnp.tanh(0.7978845608028654 * (x + 0.044715 * (x * x * x)))))


def _rmsnorm(x, g):
    ms = jnp.mean(x * x, axis=-1, keepdims=True)
    return (x * lax.rsqrt(ms + EPS)) * g


def _mod_body(cond_ref, w_ref, b_ref, o_ref):
    s = _silu(cond_ref[...])
    o_ref[...] = jnp.dot(s.astype(_BF16), w_ref[...].astype(_BF16),
                         preferred_element_type=_F32) + b_ref[...]


def _mod_call(cond, w_ada, b_ada):
    rows = cond.shape[0]
    n_out = w_ada.shape[1]
    col_block = 512
    return pl.pallas_call(
        _mod_body,
        grid=(n_out // col_block,),
        in_specs=[
            pl.BlockSpec((rows, D_MODEL), lambda j: (0, 0)),
            pl.BlockSpec((D_MODEL, col_block), lambda j: (0, j)),
            pl.BlockSpec((1, col_block), lambda j: (0, j)),
        ],
        out_specs=pl.BlockSpec((rows, col_block), lambda j: (0, j)),
        out_shape=jax.ShapeDtypeStruct((rows, n_out), _F32),
        name="adaln_mod",
    )(cond, w_ada, b_ada.reshape(1, n_out))


def _fold_body(wq_ref, wk_ref, o_ref):
    o_ref[0] = lax.dot_general(wq_ref[0], wk_ref[0], _NT_DIMS,
                               precision=lax.Precision.HIGHEST,
                               preferred_element_type=_F32)


def _fold_call(wq_nope, w_uk):
    return pl.pallas_call(
        _fold_body,
        grid=(MLA_HEADS,),
        in_specs=[
            pl.BlockSpec((1, Q_RANK, QK_NOPE), lambda h: (h, 0, 0)),
            pl.BlockSpec((1, KV_RANK, QK_NOPE), lambda h: (h, 0, 0)),
        ],
        out_specs=pl.BlockSpec((1, Q_RANK, KV_RANK), lambda h: (h, 0, 0)),
        out_shape=jax.ShapeDtypeStruct((MLA_HEADS, Q_RANK, KV_RANK), _F32),
        name="fold_q_uk",
    )(wq_nope, w_uk)


_KR_BLOCK = OFF_KR // LANES


_PREP_SPLIT = IN_WIDTH2 // 2


def _prep_body(w_ref, o_ref):
    half = QK_ROPE // 2
    blocks_per_step = _PREP_SPLIT // LANES

    def emit(step):
        base = step * _PREP_SPLIT
        for j in range(step * blocks_per_step, (step + 1) * blocks_per_step):
            if j < _KR_BLOCK:
                blk = w_ref[j * LANES - base:(j + 1) * LANES - base, :]
            elif j == _KR_BLOCK:
                kr = w_ref[OFF_KR - base:OFF_KR - base + QK_ROPE, :]
                partner = jnp.concatenate([-kr[half:, :], kr[:half, :]], axis=0)
                blk = jnp.concatenate([kr, partner], axis=0)
            else:
                lo = j * LANES - QK_ROPE - base
                blk = w_ref[lo:lo + LANES, :]
            o_ref[:, j * LANES - base:(j + 1) * LANES - base] = blk.T.astype(_BF16)

    for step in range(IN_WIDTH2 // _PREP_SPLIT):
        pl.when(pl.program_id(0) == step)(functools.partial(emit, step))


def _prep_call(w_in_t):
    assert _KR_BLOCK * LANES == OFF_KR and OFF_KR >= _PREP_SPLIT
    return pl.pallas_call(
        _prep_body,
        grid=(IN_WIDTH2 // _PREP_SPLIT,),
        in_specs=[pl.BlockSpec((_PREP_SPLIT, D_MODEL), lambda s: (s, 0))],
        out_specs=pl.BlockSpec((D_MODEL, _PREP_SPLIT), lambda s: (0, s)),
        out_shape=jax.ShapeDtypeStruct((D_MODEL, IN_WIDTH2), _BF16),
        compiler_params=pltpu.CompilerParams(vmem_limit_bytes=VMEM_LIMIT_BYTES),
        name="prep_w_in",
    )(w_in_t)


def _front_substep(x_ref, mod_ref, ng_ref, win_ref, ws_ref, bs_ref, gv_ref, qg_ref, wq_ref,
                   kvg_ref, k_tables, q_tables, out_refs, row0, write, read, emit_cache):
    if emit_cache:
        kv_ref, gb_ref, qa_ref, ckv_ref, kr_ref = out_refs
    else:
        kv_ref, gb_ref, qa_ref = out_refs
    h_w, z_w, cq_w, vn_w, ug_w, qa_w = write
    h_r, z_r, cq_r, vn_r, ug_r, qa_r = read
    rows = slice(row0, row0 + SUB_TILE)
    half = SUB_TILE // 2
    assert IN_WIDTH2 == 10 * PROJ_BLOCK

    q_rows = jnp.dot(cq_r[...], wq_ref[...], preferred_element_type=_F32)
    t1, _, t2q = q_tables
    for hd in range(MLA_HEADS):
        base = hd * QK_WIDTH
        pair = q_rows[:, base + KV_RANK:base + QK_WIDTH]
        rope = pair * t1 + pltpu.roll(pair, QK_ROPE, 1) * t2q
        qa_w[:, base:base + QK_WIDTH] = jnp.concatenate(
            [q_rows[:, base:base + KV_RANK], rope], axis=1).astype(_BF16)
    for hd in range(A_HEADS):
        lo = hd * A_HEAD_DIM
        bias = bs_ref[:, hd:hd + 1]
        for c in range(SUB_TILE // CHUNK):
            rs = slice(c * CHUNK, (c + 1) * CHUNK)
            mixed = jnp.dot(ws_ref[hd], vn_r[rs, lo:lo + A_HEAD_DIM],
                            preferred_element_type=_F32) + bias
            qa_w[rs, Q_WIDTH + lo:Q_WIDTH + lo + A_HEAD_DIM] = (
                ug_r[rs, lo:lo + A_HEAD_DIM] * mixed).astype(_BF16)

    qa_ref[0, rows, :] = qa_r[...]

    def project(j):
        lo = j * PROJ_BLOCK
        z_w[:, lo:lo + PROJ_BLOCK] = jnp.dot(h_r[...], win_ref[:, lo:lo + PROJ_BLOCK],
                                             preferred_element_type=_F32)

    shift = mod_ref[0, :, 0:D_MODEL]
    gain = ng_ref[...] * (1.0 + mod_ref[0, :, D_MODEL:2 * D_MODEL])

    def norm_rows(lo):
        x = x_ref[0, row0 + lo:row0 + lo + half, :]
        ms = jnp.mean(x * x, axis=-1, keepdims=True)
        h_w[lo:lo + half, :] = ((x * lax.rsqrt(ms + EPS)) * gain + shift).astype(_BF16)

    def vn_head(hd):
        lo = hd * A_HEAD_DIM
        v = _gelu_tanh(z_r[:, OFF_V + lo:OFF_V + lo + A_HEAD_DIM])
        vn_w[:, lo:lo + A_HEAD_DIM] = _rmsnorm(v, gv_ref[:, lo:lo + A_HEAD_DIM]).astype(_BF16)

    def ug_head(hd):
        lo = hd * A_HEAD_DIM
        u = _gelu_tanh(z_r[:, OFF_U + lo:OFF_U + lo + A_HEAD_DIM])
        ug_w[:, lo:lo + A_HEAD_DIM] = u * _silu(z_r[:, OFF_GA + lo:OFF_GA + lo + A_HEAD_DIM])

    def gate_b(lo, width):
        gb_ref[0, rows, lo:lo + width] = _silu(z_r[:, OFF_GB + lo:OFF_GB + lo + width])

    def keys_values():
        cq_w[...] = _rmsnorm(z_r[:, OFF_CQ:OFF_CQ + Q_RANK],
                             qg_ref[...] * (ATTN_SCALE * LOG2_E)).astype(_BF16)
        kt1, kt2, _ = k_tables
        ckv = _rmsnorm(z_r[:, OFF_CKV:OFF_CKV + KV_RANK], kvg_ref[...])
        kpair = z_r[:, OFF_KR:OFF_KR + 2 * QK_ROPE]
        krot = kpair * kt1 + pltpu.roll(kpair, QK_ROPE, 1) * kt2
        kv_ref[0, rows, :] = jnp.concatenate(
            [ckv, krot, ckv, jnp.ones_like(ckv)], axis=1).astype(_BF16)
        if emit_cache:
            ckv_ref[0, 0, rows, :] = ckv
            kr_ref[0, 0, rows, :] = kpair[:, 0:QK_ROPE]

    project(0)
    norm_rows(0)
    project(1)
    norm_rows(half)
    project(2)
    keys_values()
    project(3)
    vn_head(0)
    vn_head(1)
    project(4)
    vn_head(2)
    vn_head(3)
    project(5)
    ug_head(0)
    project(6)
    ug_head(1)
    project(7)
    ug_head(2)
    project(8)
    ug_head(3)
    gate_b(0, B_WIDTH // 2)
    project(9)
    gate_b(B_WIDTH // 2, B_WIDTH // 2)


_FRONT_LAG = 4
_VECTOR_LAG = 2


def _front_body(x_ref, mod_ref, ng_ref, win_ref, ws_ref, bs_ref, gv_ref, qg_ref, wq_ref,
                kvg_ref, t1_ref, t2k_ref, t2q_ref, *rest, emit_cache, tiles_per_row, n_tiles):
    n_buf = 6
    out_refs = rest[:-2 * n_buf]
    bufs_a, bufs_b = rest[-2 * n_buf:-n_buf], rest[-n_buf:]
    t = pl.program_id(0)
    assert x_ref.shape[1] == 2 * SUB_TILE
    n_pos = t1_ref.shape[0]

    @pl.when(t == 0)
    def _():
        for ref in bufs_b:
            ref[...] = jnp.zeros(ref.shape, ref.dtype)

    def tables_at(sub, lag):
        s = 2 * t + sub - lag + 2 * _FRONT_LAG
        tile_i = jnp.clip(s // 2 - _FRONT_LAG, 0, n_tiles - 1)
        pos = ((tile_i % tiles_per_row) * 2 + s % 2) * SUB_TILE
        pos = pl.multiple_of(pos % n_pos, SUB_TILE)
        return tuple(r[pl.ds(pos, SUB_TILE), :] for r in (t1_ref, t2k_ref, t2q_ref))

    for sub, (write, read) in enumerate(((bufs_a, bufs_b), (bufs_b, bufs_a))):
        _front_substep(x_ref, mod_ref, ng_ref, win_ref, ws_ref, bs_ref, gv_ref, qg_ref, wq_ref,
                       kvg_ref, tables_at(sub, _VECTOR_LAG), tables_at(sub, _FRONT_LAG - 1),
                       out_refs, sub * SUB_TILE, write, read, emit_cache)


def _front_call(x, mod, mod_index, norm_g, win2, ws_bf, bs_t, g_v, q_norm_g, wq2, kv_norm_g,
                tables, emit_cache):
    batch, seq, _ = x.shape
    tile = 2 * SUB_TILE
    tiles_per_row = seq // tile
    n_tiles = batch * tiles_per_row
    t1, t2k, t2q = tables

    def tile_at(t, lag):
        tt = jnp.clip(t - lag, 0, n_tiles - 1)
        return tt // tiles_per_row, tt % tiles_per_row

    def rows_spec(width, lag):
        return pl.BlockSpec((1, tile, width), lambda t: (*tile_at(t, lag), 0))

    def cache_spec(width):
        return pl.BlockSpec((1, 1, tile, width),
                            lambda t: (tile_at(t, 1)[0], 0, tile_at(t, 1)[1], 0))

    full = lambda shape: pl.BlockSpec(shape, lambda t: (0,) * len(shape))
    in_specs = [
        rows_spec(D_MODEL, 0),
        pl.BlockSpec((1, 1, 3 * D_MODEL), lambda t: (mod_index(tile_at(t, 0)[0]), 0, 0)),
        full((1, D_MODEL)),
        full((D_MODEL, IN_WIDTH2)),
        full((A_HEADS, CHUNK, CHUNK)),
        full((CHUNK, A_HEADS)),
        full((1, A_WIDTH)),
        full((1, Q_RANK)),
        full((Q_RANK, MLA_HEADS * QK_WIDTH)),
        full((1, KV_RANK)),
        full(t1.shape), full(t2k.shape), full(t2q.shape),
    ]
    out_shape = [
        jax.ShapeDtypeStruct((batch, seq, KV_WIDTH), _BF16),
        jax.ShapeDtypeStruct((batch, seq, B_WIDTH), _F32),
        jax.ShapeDtypeStruct((batch, seq, QA_WIDTH), _BF16),
    ]
    out_specs = [
        rows_spec(KV_WIDTH, _VECTOR_LAG // 2),
        rows_spec(B_WIDTH, _VECTOR_LAG // 2),
        rows_spec(QA_WIDTH, _FRONT_LAG // 2),
    ]
    if emit_cache:
        out_shape += [
            jax.ShapeDtypeStruct((batch, 1, seq, KV_RANK), _F32),
            jax.ShapeDtypeStruct((batch, 1, seq, QK_ROPE), _F32),
        ]
        out_specs += [cache_spec(KV_RANK), cache_spec(QK_ROPE)]
    hand_off = [
        pltpu.VMEM((SUB_TILE, D_MODEL), _BF16),
        pltpu.VMEM((SUB_TILE, IN_WIDTH2), _F32),
        pltpu.VMEM((SUB_TILE, Q_RANK), _BF16),
        pltpu.VMEM((SUB_TILE, A_WIDTH), _BF16),
        pltpu.VMEM((SUB_TILE, A_WIDTH), _F32),
        pltpu.VMEM((SUB_TILE, QA_WIDTH), _BF16),
    ]
    return pl.pallas_call(
        functools.partial(_front_body, emit_cache=emit_cache, tiles_per_row=tiles_per_row,
                          n_tiles=n_tiles),
        grid=(n_tiles + _FRONT_LAG // 2,),
        in_specs=in_specs,
        out_specs=out_specs,
        out_shape=out_shape,
        scratch_shapes=hand_off + hand_off,
        compiler_params=pltpu.CompilerParams(
            dimension_semantics=("arbitrary",),
            vmem_limit_bytes=VMEM_LIMIT_BYTES),
        name="front_ctx" if emit_cache else "front_lat",
    )(x, mod, norm_g, win2, ws_bf, bs_t, g_v, q_norm_g, wq2, kv_norm_g, t1, t2k, t2q)


def _scores(q, k_refs):
    return [lax.dot_general(q, k_ref[0], _NT_DIMS, preferred_element_type=_F32)
            for k_ref in k_refs]


def _softmax_values(scores, v_refs):
    m = functools.reduce(jnp.maximum, [jnp.max(s, axis=1, keepdims=True) for s in scores])
    acc = None
    for s, v_ref in zip(scores, v_refs):
        p = jnp.exp2(s - m).astype(_BF16)
        part = jnp.dot(p, v_ref[0], preferred_element_type=_F32)
        acc = part if acc is None else acc + part
    return acc[:, 0:KV_RANK] / acc[:, KV_RANK:2 * KV_RANK]


def _back_body(*refs, has_ctx):
    if has_ctx:
        (q_ref, klat_ref, vlat_ref, kctx_ref, vctx_ref, a_ref, gb_ref, x_ref, mod_ref, wuv_ref,
         wo_ref, fg_ref, y_ref) = refs
        k_refs, v_refs = (kctx_ref, klat_ref), (vctx_ref, vlat_ref)
    else:
        (q_ref, klat_ref, vlat_ref, a_ref, gb_ref, x_ref, mod_ref, wuv_ref, wo_ref, fg_ref,
         y_ref) = refs
        k_refs, v_refs = (klat_ref,), (vlat_ref,)
    tq = q_ref.shape[1]

    groups = [
        jnp.concatenate([q_ref[0, :, hd * QK_WIDTH:(hd + 1) * QK_WIDTH] for hd in (2 * g, 2 * g + 1)],
                        axis=0) for g in range(MLA_HEADS // 2)]
    scores = [_scores(qg, k_refs) for qg in groups]
    outs = [_softmax_values(s, v_refs) for s in scores]

    attn_cols = []
    for pair, o in enumerate(outs):
        o2 = jnp.concatenate([o[0:tq, :], o[tq:2 * tq, :]], axis=1)
        attn_cols.append(jnp.dot(o2.astype(_BF16), wuv_ref[pair], preferred_element_type=_F32))
    attn = jnp.concatenate(attn_cols, axis=1) * gb_ref[0]

    y = jnp.dot(a_ref[0], wo_ref[0:A_WIDTH, :], preferred_element_type=_F32)
    y = y + jnp.dot(attn.astype(_BF16), wo_ref[A_WIDTH:A_WIDTH + B_WIDTH, :],
                    preferred_element_type=_F32)
    gate = mod_ref[0, :, 2 * D_MODEL:3 * D_MODEL]
    out = x_ref[0] + gate * y
    y_ref[0] = _rmsnorm(out, fg_ref[...])


def _back_call(kv, qa, kctx, vctx, gb, x, mod, mod_index, wuv2, wo_bf, final_g):
    batch, seq, _ = x.shape
    tq = min(TOKEN_TILE, seq)
    has_ctx = kctx is not None
    full = lambda shape: pl.BlockSpec(shape, lambda b, i: (0,) * len(shape))
    keys = lambda arr: pl.BlockSpec((1, arr.shape[1], arr.shape[2]), lambda b, i: (b, 0, 0))
    in_specs = [
        pl.BlockSpec((1, tq, Q_WIDTH), lambda b, i: (b, i, 0)),
        pl.BlockSpec((1, seq, QK_WIDTH), lambda b, i: (b, 0, 0)),
        pl.BlockSpec((1, seq, 2 * KV_RANK), lambda b, i: (b, 0, QK_WIDTH // (2 * KV_RANK))),
    ]
    args = [qa, kv, kv]
    if has_ctx:
        in_specs += [keys(kctx), keys(vctx)]
        args += [kctx, vctx]
    in_specs += [
        pl.BlockSpec((1, tq, A_WIDTH), lambda b, i: (b, i, Q_WIDTH // A_WIDTH)),
        pl.BlockSpec((1, tq, B_WIDTH), lambda b, i: (b, i, 0)),
        pl.BlockSpec((1, tq, D_MODEL), lambda b, i: (b, i, 0)),
        pl.BlockSpec((1, 1, 3 * D_MODEL), lambda b, i: (mod_index(b), 0, 0)),
        full((MLA_HEADS // 2, 2 * KV_RANK, 2 * V_HEAD)),
        full((D_MODEL, D_MODEL)),
        full((1, D_MODEL)),
    ]
    args += [qa, gb, x, mod, wuv2, wo_bf, final_g]
    return pl.pallas_call(
        functools.partial(_back_body, has_ctx=has_ctx),
        grid=(batch, seq // tq),
        in_specs=in_specs,
        out_specs=pl.BlockSpec((1, tq, D_MODEL), lambda b, i: (b, i, 0)),
        out_shape=jax.ShapeDtypeStruct((batch, seq, D_MODEL), _F32),
        compiler_params=pltpu.CompilerParams(
            dimension_semantics=("arbitrary", "arbitrary"),
            vmem_limit_bytes=VMEM_LIMIT_BYTES),
        name="back_lat" if has_ctx else "back_ctx",
    )(*args)


def _swap_halves(w):
    half = QK_ROPE // 2
    return jnp.concatenate([-w[..., half:], w[..., :half]], axis=-1)


def _rope_tables(n_tokens, rotate):
    ones = jnp.ones((n_tokens, QK_ROPE), _F32)
    zeros = jnp.zeros((n_tokens, QK_ROPE), _F32)
    if rotate:
        rows = n_tokens // GRID_W
        inv = ROPE_THETA ** (-jnp.arange(AXIS_PAIRS, dtype=_F32) / AXIS_PAIRS)
        ang_r = jnp.arange(rows, dtype=_F32)[:, None] * inv
        ang_c = jnp.arange(GRID_W, dtype=_F32)[:, None] * inv

        def grid_table(fn):
            tr = jnp.broadcast_to(fn(ang_r)[:, None, :], (rows, GRID_W, AXIS_PAIRS))
            tc = jnp.broadcast_to(fn(ang_c)[None, :, :], (rows, GRID_W, AXIS_PAIRS))
            half = jnp.concatenate([tr, tc], axis=-1).reshape(n_tokens, 2 * AXIS_PAIRS)
            return jnp.concatenate([half, half], axis=-1)

        cos, sin = grid_table(jnp.cos), grid_table(jnp.sin)
    else:
        cos, sin = ones, zeros
    t1 = jnp.concatenate([cos, zeros], axis=-1)
    t2k = jnp.concatenate([sin, zeros], axis=-1)
    t2q = jnp.concatenate([sin, ones], axis=-1)
    return t1, t2k, t2q


def kernel(x_prompt, x_sample, cache_ckv, cache_krope, c, c_ctx, norm_g, w_ada, b_ada, w_in, w_s,
           b_s, g_v, q_norm_g, w_uq, kv_norm_g, w_ukv, w_o, final_g):
    depth = norm_g.shape[0]
    assert depth == 1 and w_in.shape[2] == IN_WIDTH
    dec_batch = x_sample.shape[0]
    xp, xs = x_prompt, x_sample
    new_ckv, new_kr = [], []
    for l in range(depth):
        cond = jnp.concatenate(
            [c, c_ctx[None, :], jnp.zeros((16 - dec_batch - 1, D_MODEL), _F32)], axis=0)
        mod = _mod_call(cond, w_ada[l], b_ada[l]).reshape(16, 1, 3 * D_MODEL)

        win2 = _prep_call(jnp.transpose(w_in[l]))
        wq3 = w_uq[l].reshape(Q_RANK, MLA_HEADS, QK_NOPE + QK_ROPE)
        wkv3 = w_ukv[l].reshape(KV_RANK, MLA_HEADS, QK_NOPE + V_HEAD)
        wq_nope = jnp.transpose(wq3[:, :, :QK_NOPE], (1, 0, 2))
        w_uk = jnp.transpose(wkv3[:, :, :QK_NOPE], (1, 0, 2))
        w_uv = jnp.transpose(wkv3[:, :, QK_NOPE:], (1, 0, 2))
        wq_abs = _fold_call(wq_nope, w_uk)
        wq_rope = jnp.transpose(wq3[:, :, QK_NOPE:], (1, 0, 2))
        wq2 = jnp.concatenate([wq_abs, wq_rope, _swap_halves(wq_rope)], axis=-1)
        wq2 = jnp.transpose(wq2, (1, 0, 2)).reshape(Q_RANK, MLA_HEADS * QK_WIDTH).astype(_BF16)
        zero_blk = jnp.zeros((KV_RANK, V_HEAD), _F32)
        wuv2 = jnp.stack([
            jnp.block([[w_uv[2 * p], zero_blk], [zero_blk, w_uv[2 * p + 1]]])
            for p in range(MLA_HEADS // 2)]).astype(_BF16)
        wo_bf = w_o[l].astype(_BF16)
        ws_bf = w_s[l].astype(_BF16)
        bs_t = b_s[l].T
        gv_row = g_v[l].reshape(1, A_WIDTH)
        shared = (norm_g[l].reshape(1, D_MODEL), win2, ws_bf, bs_t, gv_row,
                  q_norm_g[l].reshape(1, Q_RANK), wq2, kv_norm_g[l].reshape(1, KV_RANK))
        fg = final_g.reshape(1, D_MODEL)

        ctx_index = lambda b: dec_batch
        ctx_batch, ctx_seq, _ = xp.shape
        pack = 2 * SUB_TILE // ctx_seq
        kv_c, gb_c, qa_c, ckv_c, kr_c = _front_call(
            xp.reshape(ctx_batch // pack, pack * ctx_seq, D_MODEL), mod, ctx_index, *shared,
            _rope_tables(ctx_seq, False), True)
        unpack = lambda arr: arr.reshape(ctx_batch, ctx_seq, arr.shape[-1])
        xp = _back_call(unpack(kv_c), unpack(qa_c), None, None, unpack(gb_c), xp, mod, ctx_index,
                        wuv2, wo_bf, fg)
        new_ckv.append(ckv_c.reshape(ctx_batch, 1, ctx_seq, KV_RANK))
        new_kr.append(kr_c.reshape(ctx_batch, 1, ctx_seq, QK_ROPE))

        lat_index = lambda b: b
        kv_s, gb_s, qa_s = _front_call(
            xs, mod, lat_index, *shared, _rope_tables(xs.shape[1], True), False)
        cache_k = jnp.concatenate(
            [cache_ckv[:, l], jnp.zeros(cache_krope[:, l].shape, _F32), cache_krope[:, l]],
            axis=-1).astype(_BF16)
        cache_v = jnp.concatenate(
            [cache_ckv[:, l], jnp.ones(cache_ckv[:, l].shape, _F32)], axis=-1).astype(_BF16)
        xs = _back_call(kv_s, qa_s, cache_k, cache_v, gb_s, xs, mod, lat_index, wuv2, wo_bf, fg)
    return (xp, xs, jnp.concatenate(new_ckv, axis=1), jnp.concatenate(new_kr, axis=1))
```

```python
import functools
import math

import jax
import jax.numpy as jnp
from jax import lax
from jax.experimental import pallas as pl
from jax.experimental.pallas import tpu as pltpu

D_MODEL = 1024
GRID_W = 64
EPS = 1e-6
A_HEADS = 4
A_HEAD_DIM = 128
A_WIDTH = A_HEADS * A_HEAD_DIM
CHUNK = 128
MLA_HEADS = 4
QK_NOPE = 128
QK_ROPE = 64
V_HEAD = 128
B_WIDTH = MLA_HEADS * V_HEAD
Q_RANK = 256
KV_RANK = 128
AXIS_PAIRS = QK_ROPE // 4
ROPE_THETA = 10000.0
ATTN_SCALE = 1.0 / math.sqrt(QK_NOPE + QK_ROPE)
LOG2_E = 1.4426950408889634

QK_WIDTH = KV_RANK + 2 * QK_ROPE
OFF_U = 0
OFF_V = OFF_U + A_WIDTH
OFF_GA = OFF_V + A_WIDTH
OFF_CQ = OFF_GA + A_WIDTH
OFF_CKV = OFF_CQ + Q_RANK
OFF_KR = OFF_CKV + KV_RANK
OFF_GB = OFF_KR + 2 * QK_ROPE
IN_WIDTH = OFF_GB - QK_ROPE + B_WIDTH
IN_WIDTH2 = OFF_GB + B_WIDTH

PK_A = 0
PK_K = PK_A + A_WIDTH
PK_V = PK_K + QK_WIDTH
PK_Q = PK_V + 2 * KV_RANK
PK_WIDTH = PK_Q + MLA_HEADS * QK_WIDTH

LANES = 128
SUB_TILE = 256
TOKEN_TILE = 512
PROJ_BLOCK = 512
VMEM_LIMIT_BYTES = 56 * 1024 * 1024

_F32 = jnp.float32
_BF16 = jnp.bfloat16
_NT_DIMS = (((1,), (1,)), ((), ()))


def _silu(x):
    hx = 0.5 * x
    return hx + hx * jnp.tanh(hx)


def _gelu_tanh(x):
    return x * (0.5 * (1.0 + jnp.tanh(0.7978845608028654 * (x + 0.044715 * (x * x * x)))))


def _rmsnorm(x, g):
    ms = jnp.mean(x * x, axis=-1, keepdims=True)
    return (x * lax.rsqrt(ms + EPS)) * g


def _mod_body(cond_ref, w_ref, b_ref, o_ref):
    s = _silu(cond_ref[...])
    o_ref[...] = jnp.dot(s.astype(_BF16), w_ref[...].astype(_BF16),
                         preferred_element_type=_F32) + b_ref[...]


def _mod_call(cond, w_ada, b_ada):
    rows = cond.shape[0]
    n_out = w_ada.shape[1]
    col_block = 512
    return pl.pallas_call(
        _mod_body,
        grid=(n_out // col_block,),
        in_specs=[
            pl.BlockSpec((rows, D_MODEL), lambda j: (0, 0)),
            pl.BlockSpec((D_MODEL, col_block), lambda j: (0, j)),
            pl.BlockSpec((1, col_block), lambda j: (0, j)),
        ],
        out_specs=pl.BlockSpec((rows, col_block), lambda j: (0, j)),
        out_shape=jax.ShapeDtypeStruct((rows, n_out), _F32),
        name="adaln_mod",
    )(cond, w_ada, b_ada.reshape(1, n_out))


def _fold_body(wq_ref, wk_ref, o_ref):
    o_ref[0] = lax.dot_general(wq_ref[0], wk_ref[0], _NT_DIMS,
                               precision=lax.Precision.HIGHEST,
                               preferred_element_type=_F32)


def _fold_call(wq_nope, w_uk):
    return pl.pallas_call(
        _fold_body,
        grid=(MLA_HEADS,),
        in_specs=[
            pl.BlockSpec((1, Q_RANK, QK_NOPE), lambda h: (h, 0, 0)),
            pl.BlockSpec((1, KV_RANK, QK_NOPE), lambda h: (h, 0, 0)),
        ],
        out_specs=pl.BlockSpec((1, Q_RANK, KV_RANK), lambda h: (h, 0, 0)),
        out_shape=jax.ShapeDtypeStruct((MLA_HEADS, Q_RANK, KV_RANK), _F32),
        name="fold_q_uk",
    )(wq_nope, w_uk)


_KR_BLOCK = OFF_KR // LANES


_PREP_SPLIT = IN_WIDTH2 // 2


def _prep_body(w_ref, o_ref):
    half = QK_ROPE // 2
    blocks_per_step = _PREP_SPLIT // LANES

    def emit(step):
        base = step * _PREP_SPLIT
        for j in range(step * blocks_per_step, (step + 1) * blocks_per_step):
            if j < _KR_BLOCK:
                blk = w_ref[j * LANES - base:(j + 1) * LANES - base, :]
            elif j == _KR_BLOCK:
                kr = w_ref[OFF_KR - base:OFF_KR - base + QK_ROPE, :]
                partner = jnp.concatenate([-kr[half:, :], kr[:half, :]], axis=0)
                blk = jnp.concatenate([kr, partner], axis=0)
            else:
                lo = j * LANES - QK_ROPE - base
                blk = w_ref[lo:lo + LANES, :]
            o_ref[:, j * LANES - base:(j + 1) * LANES - base] = blk.T.astype(_BF16)

    for step in range(IN_WIDTH2 // _PREP_SPLIT):
        pl.when(pl.program_id(0) == step)(functools.partial(emit, step))


def _prep_call(w_in_t):
    assert _KR_BLOCK * LANES == OFF_KR and OFF_KR >= _PREP_SPLIT
    return pl.pallas_call(
        _prep_body,
        grid=(IN_WIDTH2 // _PREP_SPLIT,),
        in_specs=[pl.BlockSpec((_PREP_SPLIT, D_MODEL), lambda s: (s, 0))],
        out_specs=pl.BlockSpec((D_MODEL, _PREP_SPLIT), lambda s: (0, s)),
        out_shape=jax.ShapeDtypeStruct((D_MODEL, IN_WIDTH2), _BF16),
        compiler_params=pltpu.CompilerParams(vmem_limit_bytes=VMEM_LIMIT_BYTES),
        name="prep_w_in",
    )(w_in_t)


def _front_substep(x_ref, mod_ref, ng_ref, win_ref, ws_ref, bs_ref, gv_ref, qg_ref, wq_ref,
                   kvg_ref, tables, out_refs, row0, h_write, h_read, z_write, z_read, emit_cache):
    if emit_cache:
        pk_ref, gb_ref, ckv_ref, kr_ref = out_refs
    else:
        pk_ref, gb_ref = out_refs
    t1, t2k, t2q = tables
    rows = slice(row0, row0 + SUB_TILE)
    half = SUB_TILE // 2
    assert IN_WIDTH2 == 5 * PROJ_BLOCK

    def project(j):
        lo = j * PROJ_BLOCK
        z_write[:, lo:lo + PROJ_BLOCK] = jnp.dot(h_read[...], win_ref[:, lo:lo + PROJ_BLOCK],
                                                 preferred_element_type=_F32)

    shift = mod_ref[0, :, 0:D_MODEL]
    gain = ng_ref[...] * (1.0 + mod_ref[0, :, D_MODEL:2 * D_MODEL])

    def norm_rows(lo):
        x = x_ref[0, row0 + lo:row0 + lo + half, :]
        ms = jnp.mean(x * x, axis=-1, keepdims=True)
        h_write[lo:lo + half, :] = ((x * lax.rsqrt(ms + EPS)) * gain + shift).astype(_BF16)

    def vn_head(hd):
        lo = hd * A_HEAD_DIM
        v = _gelu_tanh(z_read[:, OFF_V + lo:OFF_V + lo + A_HEAD_DIM])
        return _rmsnorm(v, gv_ref[:, lo:lo + A_HEAD_DIM]).astype(_BF16)

    def mix_head(hd, vn):
        bias = bs_ref[:, hd:hd + 1]
        parts = [jnp.dot(ws_ref[hd], vn[c * CHUNK:(c + 1) * CHUNK, :],
                         preferred_element_type=_F32) + bias for c in range(SUB_TILE // CHUNK)]
        return jnp.concatenate(parts, axis=0)

    def gate_a(hd, mixed):
        lo = hd * A_HEAD_DIM
        u = _gelu_tanh(z_read[:, OFF_U + lo:OFF_U + lo + A_HEAD_DIM])
        g = _silu(z_read[:, OFF_GA + lo:OFF_GA + lo + A_HEAD_DIM])
        pk_ref[0, rows, PK_A + lo:PK_A + lo + A_HEAD_DIM] = (u * mixed * g).astype(_BF16)

    def gate_b(lo, width):
        gb_ref[0, rows, lo:lo + width] = _silu(z_read[:, OFF_GB + lo:OFF_GB + lo + width])

    def rope_q(qa, hd):
        base = hd * QK_WIDTH
        pair = qa[:, base + KV_RANK:base + QK_WIDTH]
        rope = pair * t1 + pltpu.roll(pair, QK_ROPE, 1) * t2q
        pk_ref[0, rows, PK_Q + base:PK_Q + base + QK_WIDTH] = jnp.concatenate(
            [qa[:, base:base + KV_RANK], rope], axis=1).astype(_BF16)

    project(0)
    cq = _rmsnorm(z_read[:, OFF_CQ:OFF_CQ + Q_RANK],
                  qg_ref[...] * (ATTN_SCALE * LOG2_E)).astype(_BF16)
    vn0 = vn_head(0)
    vn1 = vn_head(1)
    gate_b(0, B_WIDTH // 2)
    project(1)
    vn2 = vn_head(2)
    gate_b(B_WIDTH // 2, B_WIDTH // 2)
    vn3 = vn_head(3)
    ckv = _rmsnorm(z_read[:, OFF_CKV:OFF_CKV + KV_RANK], kvg_ref[...])
    kpair = z_read[:, OFF_KR:OFF_KR + 2 * QK_ROPE]
    krot = kpair * t1 + pltpu.roll(kpair, QK_ROPE, 1) * t2k
    pk_ref[0, rows, PK_K:PK_K + QK_WIDTH] = jnp.concatenate([ckv, krot], axis=1).astype(_BF16)
    pk_ref[0, rows, PK_V:PK_V + 2 * KV_RANK] = jnp.concatenate(
        [ckv, jnp.ones_like(ckv)], axis=1).astype(_BF16)
    if emit_cache:
        ckv_ref[0, 0, rows, :] = ckv
        kr_ref[0, 0, rows, :] = kpair[:, 0:QK_ROPE]
    qa = jnp.dot(cq, wq_ref[...], preferred_element_type=_F32)
    norm_rows(0)
    project(2)
    rope_q(qa, 0)
    rope_q(qa, 1)
    mixed0 = mix_head(0, vn0)
    mixed1 = mix_head(1, vn1)
    rope_q(qa, 2)
    rope_q(qa, 3)
    gate_a(0, mixed0)
    mixed2 = mix_head(2, vn2)
    mixed3 = mix_head(3, vn3)
    project(3)
    gate_a(1, mixed1)
    norm_rows(half)
    gate_a(2, mixed2)
    project(4)
    gate_a(3, mixed3)


def _front_body(x_ref, mod_ref, ng_ref, win_ref, ws_ref, bs_ref, gv_ref, qg_ref, wq_ref,
                kvg_ref, t1_ref, t2k_ref, t2q_ref, *rest, emit_cache, tiles_per_row):
    out_refs, (h_a, h_b, z_a, z_b) = rest[:-4], rest[-4:]
    t = pl.program_id(0)
    tile = x_ref.shape[1]
    assert tile == 2 * SUB_TILE
    n_pos = t1_ref.shape[0]

    @pl.when(t == 0)
    def _():
        h_b[...] = jnp.zeros(h_b.shape, _BF16)
        z_b[...] = jnp.zeros(z_b.shape, _F32)

    pos_base = (jnp.maximum(t - 1, 0) % tiles_per_row) * tile
    for sub, (h_write, h_read, z_write, z_read) in enumerate(
            ((h_a, h_b, z_a, z_b), (h_b, h_a, z_b, z_a))):
        row0 = sub * SUB_TILE
        pos = pl.multiple_of((pos_base + row0) % n_pos, SUB_TILE)
        tables = tuple(r[pl.ds(pos, SUB_TILE), :] for r in (t1_ref, t2k_ref, t2q_ref))
        _front_substep(x_ref, mod_ref, ng_ref, win_ref, ws_ref, bs_ref, gv_ref, qg_ref, wq_ref,
                       kvg_ref, tables, out_refs, row0, h_write, h_read, z_write, z_read,
                       emit_cache)


def _front_call(x, mod, mod_index, norm_g, win2, ws_bf, bs_t, g_v, q_norm_g, wq2, kv_norm_g,
                tables, emit_cache):
    batch, seq, _ = x.shape
    tile = 2 * SUB_TILE
    tiles_per_row = seq // tile
    n_tiles = batch * tiles_per_row
    t1, t2k, t2q = tables

    def in_tile(t):
        tt = jnp.minimum(t, n_tiles - 1)
        return tt // tiles_per_row, tt % tiles_per_row

    def out_tile(t):
        tt = jnp.maximum(t - 1, 0)
        return tt // tiles_per_row, tt % tiles_per_row

    full = lambda shape: pl.BlockSpec(shape, lambda t: (0,) * len(shape))
    in_specs = [
        pl.BlockSpec((1, tile, D_MODEL), lambda t: (*in_tile(t), 0)),
        pl.BlockSpec((1, 1, 3 * D_MODEL), lambda t: (mod_index(in_tile(t)[0]), 0, 0)),
        full((1, D_MODEL)),
        full((D_MODEL, IN_WIDTH2)),
        full((A_HEADS, CHUNK, CHUNK)),
        full((CHUNK, A_HEADS)),
        full((1, A_WIDTH)),
        full((1, Q_RANK)),
        full((Q_RANK, MLA_HEADS * QK_WIDTH)),
        full((1, KV_RANK)),
        full(t1.shape), full(t2k.shape), full(t2q.shape),
    ]
    out_shape = [
        jax.ShapeDtypeStruct((batch, seq, PK_WIDTH), _BF16),
        jax.ShapeDtypeStruct((batch, seq, B_WIDTH), _F32),
    ]
    out_specs = [
        pl.BlockSpec((1, tile, PK_WIDTH), lambda t: (*out_tile(t), 0)),
        pl.BlockSpec((1, tile, B_WIDTH), lambda t: (*out_tile(t), 0)),
    ]
    if emit_cache:
        out_shape += [
            jax.ShapeDtypeStruct((batch, 1, seq, KV_RANK), _F32),
            jax.ShapeDtypeStruct((batch, 1, seq, QK_ROPE), _F32),
        ]
        out_specs += [
            pl.BlockSpec((1, 1, tile, KV_RANK), lambda t: (out_tile(t)[0], 0, out_tile(t)[1], 0)),
            pl.BlockSpec((1, 1, tile, QK_ROPE), lambda t: (out_tile(t)[0], 0, out_tile(t)[1], 0)),
        ]
    return pl.pallas_call(
        functools.partial(_front_body, emit_cache=emit_cache, tiles_per_row=tiles_per_row),
        grid=(n_tiles + 1,),
        in_specs=in_specs,
        out_specs=out_specs,
        out_shape=out_shape,
        scratch_shapes=[pltpu.VMEM((SUB_TILE, D_MODEL), _BF16), pltpu.VMEM((SUB_TILE, D_MODEL), _BF16),
                        pltpu.VMEM((SUB_TILE, IN_WIDTH2), _F32),
                        pltpu.VMEM((SUB_TILE, IN_WIDTH2), _F32)],
        compiler_params=pltpu.CompilerParams(
            dimension_semantics=("arbitrary",),
            vmem_limit_bytes=VMEM_LIMIT_BYTES),
        name="front_ctx" if emit_cache else "front_lat",
    )(x, mod, norm_g, win2, ws_bf, bs_t, g_v, q_norm_g, wq2, kv_norm_g, t1, t2k, t2q)


def _scores(q, k_refs):
    return [lax.dot_general(q, k_ref[0], _NT_DIMS, preferred_element_type=_F32)
            for k_ref in k_refs]


def _softmax_values(scores, v_refs):
    m = functools.reduce(jnp.maximum, [jnp.max(s, axis=1, keepdims=True) for s in scores])
    acc = None
    for s, v_ref in zip(scores, v_refs):
        p = jnp.exp2(s - m).astype(_BF16)
        part = jnp.dot(p, v_ref[0], preferred_element_type=_F32)
        acc = part if acc is None else acc + part
    return acc[:, 0:KV_RANK] / acc[:, KV_RANK:2 * KV_RANK]


def _back_body(*refs, has_ctx):
    if has_ctx:
        (q_ref, klat_ref, vlat_ref, kctx_ref, vctx_ref, a_ref, gb_ref, x_ref, mod_ref, wuv_ref,
         wo_ref, fg_ref, y_ref) = refs
        k_refs, v_refs = (kctx_ref, klat_ref), (vctx_ref, vlat_ref)
    else:
        (q_ref, klat_ref, vlat_ref, a_ref, gb_ref, x_ref, mod_ref, wuv_ref, wo_ref, fg_ref,
         y_ref) = refs
        k_refs, v_refs = (klat_ref,), (vlat_ref,)
    tq = q_ref.shape[1]

    groups = [
        jnp.concatenate([q_ref[0, :, hd * QK_WIDTH:(hd + 1) * QK_WIDTH] for hd in (2 * g, 2 * g + 1)],
                        axis=0) for g in range(MLA_HEADS // 2)]
    scores = [_scores(qg, k_refs) for qg in groups]
    outs = [_softmax_values(s, v_refs) for s in scores]

    attn_cols = []
    for pair, o in enumerate(outs):
        o2 = jnp.concatenate([o[0:tq, :], o[tq:2 * tq, :]], axis=1)
        attn_cols.append(jnp.dot(o2.astype(_BF16), wuv_ref[pair], preferred_element_type=_F32))
    attn = jnp.concatenate(attn_cols, axis=1) * gb_ref[0]

    y = jnp.dot(a_ref[0], wo_ref[0:A_WIDTH, :], preferred_element_type=_F32)
    y = y + jnp.dot(attn.astype(_BF16), wo_ref[A_WIDTH:A_WIDTH + B_WIDTH, :],
                    preferred_element_type=_F32)
    gate = mod_ref[0, :, 2 * D_MODEL:3 * D_MODEL]
    out = x_ref[0] + gate * y
    y_ref[0] = _rmsnorm(out, fg_ref[...])


def _back_call(packed, kctx, vctx, gb, x, mod, mod_index, wuv2, wo_bf, final_g):
    batch, seq, _ = x.shape
    tq = min(TOKEN_TILE, seq)
    has_ctx = kctx is not None
    full = lambda shape: pl.BlockSpec(shape, lambda b, i: (0,) * len(shape))
    keys = lambda arr: pl.BlockSpec((1, arr.shape[1], arr.shape[2]), lambda b, i: (b, 0, 0))
    in_specs = [
        pl.BlockSpec((1, tq, MLA_HEADS * QK_WIDTH), lambda b, i: (b, i, PK_Q // (MLA_HEADS * QK_WIDTH))),
        pl.BlockSpec((1, seq, QK_WIDTH), lambda b, i: (b, 0, PK_K // QK_WIDTH)),
        pl.BlockSpec((1, seq, 2 * KV_RANK), lambda b, i: (b, 0, PK_V // (2 * KV_RANK))),
    ]
    args = [packed, packed, packed]
    if has_ctx:
        in_specs += [keys(kctx), keys(vctx)]
        args += [kctx, vctx]
    in_specs += [
        pl.BlockSpec((1, tq, A_WIDTH), lambda b, i: (b, i, PK_A // A_WIDTH)),
        pl.BlockSpec((1, tq, B_WIDTH), lambda b, i: (b, i, 0)),
        pl.BlockSpec((1, tq, D_MODEL), lambda b, i: (b, i, 0)),
        pl.BlockSpec((1, 1, 3 * D_MODEL), lambda b, i: (mod_index(b), 0, 0)),
        full((MLA_HEADS // 2, 2 * KV_RANK, 2 * V_HEAD)),
        full((D_MODEL, D_MODEL)),
        full((1, D_MODEL)),
    ]
    args += [packed, gb, x, mod, wuv2, wo_bf, final_g]
    return pl.pallas_call(
        functools.partial(_back_body, has_ctx=has_ctx),
        grid=(batch, seq // tq),
        in_specs=in_specs,
        out_specs=pl.BlockSpec((1, tq, D_MODEL), lambda b, i: (b, i, 0)),
        out_shape=jax.ShapeDtypeStruct((batch, seq, D_MODEL), _F32),
        compiler_params=pltpu.CompilerParams(
            dimension_semantics=("arbitrary", "arbitrary"),
            vmem_limit_bytes=VMEM_LIMIT_BYTES),
        name="back_lat" if has_ctx else "back_ctx",
    )(*args)


def _swap_halves(w):
    half = QK_ROPE // 2
    return jnp.concatenate([-w[..., half:], w[..., :half]], axis=-1)


def _rope_tables(n_tokens, rotate):
    ones = jnp.ones((n_tokens, QK_ROPE), _F32)
    zeros = jnp.zeros((n_tokens, QK_ROPE), _F32)
    if rotate:
        rows = n_tokens // GRID_W
        inv = ROPE_THETA ** (-jnp.arange(AXIS_PAIRS, dtype=_F32) / AXIS_PAIRS)
        ang_r = jnp.arange(rows, dtype=_F32)[:, None] * inv
        ang_c = jnp.arange(GRID_W, dtype=_F32)[:, None] * inv

        def grid_table(fn):
            tr = jnp.broadcast_to(fn(ang_r)[:, None, :], (rows, GRID_W, AXIS_PAIRS))
            tc = jnp.broadcast_to(fn(ang_c)[None, :, :], (rows, GRID_W, AXIS_PAIRS))
            half = jnp.concatenate([tr, tc], axis=-1).reshape(n_tokens, 2 * AXIS_PAIRS)
            return jnp.concatenate([half, half], axis=-1)

        cos, sin = grid_table(jnp.cos), grid_table(jnp.sin)
    else:
        cos, sin = ones, zeros
    t1 = jnp.concatenate([cos, zeros], axis=-1)
    t2k = jnp.concatenate([sin, zeros], axis=-1)
    t2q = jnp.concatenate([sin, ones], axis=-1)
    return t1, t2k, t2q


def kernel(x_prompt, x_sample, cache_ckv, cache_krope, c, c_ctx, norm_g, w_ada, b_ada, w_in, w_s,
           b_s, g_v, q_norm_g, w_uq, kv_norm_g, w_ukv, w_o, final_g):
    depth = norm_g.shape[0]
    assert depth == 1 and w_in.shape[2] == IN_WIDTH
    dec_batch = x_sample.shape[0]
    xp, xs = x_prompt, x_sample
    new_ckv, new_kr = [], []
    for l in range(depth):
        cond = jnp.concatenate(
            [c, c_ctx[None, :], jnp.zeros((16 - dec_batch - 1, D_MODEL), _F32)], axis=0)
        mod = _mod_call(cond, w_ada[l], b_ada[l]).reshape(16, 1, 3 * D_MODEL)

        win2 = _prep_call(jnp.transpose(w_in[l]))
        wq3 = w_uq[l].reshape(Q_RANK, MLA_HEADS, QK_NOPE + QK_ROPE)
        wkv3 = w_ukv[l].reshape(KV_RANK, MLA_HEADS, QK_NOPE + V_HEAD)
        wq_nope = jnp.transpose(wq3[:, :, :QK_NOPE], (1, 0, 2))
        w_uk = jnp.transpose(wkv3[:, :, :QK_NOPE], (1, 0, 2))
        w_uv = jnp.transpose(wkv3[:, :, QK_NOPE:], (1, 0, 2))
        wq_abs = _fold_call(wq_nope, w_uk)
        wq_rope = jnp.transpose(wq3[:, :, QK_NOPE:], (1, 0, 2))
        wq2 = jnp.concatenate([wq_abs, wq_rope, _swap_halves(wq_rope)], axis=-1)
        wq2 = jnp.transpose(wq2, (1, 0, 2)).reshape(Q_RANK, MLA_HEADS * QK_WIDTH).astype(_BF16)
        zero_blk = jnp.zeros((KV_RANK, V_HEAD), _F32)
        wuv2 = jnp.stack([
            jnp.block([[w_uv[2 * p], zero_blk], [zero_blk, w_uv[2 * p + 1]]])
            for p in range(MLA_HEADS // 2)]).astype(_BF16)
        wo_bf = w_o[l].astype(_BF16)
        ws_bf = w_s[l].astype(_BF16)
        bs_t = b_s[l].T
        gv_row = g_v[l].reshape(1, A_WIDTH)
        shared = (norm_g[l].reshape(1, D_MODEL), win2, ws_bf, bs_t, gv_row,
                  q_norm_g[l].reshape(1, Q_RANK), wq2, kv_norm_g[l].reshape(1, KV_RANK))
        fg = final_g.reshape(1, D_MODEL)

        ctx_index = lambda b: dec_batch
        ctx_batch, ctx_seq, _ = xp.shape
        pack = 2 * SUB_TILE // ctx_seq
        pk_c, gb_c, ckv_c, kr_c = _front_call(
            xp.reshape(ctx_batch // pack, pack * ctx_seq, D_MODEL), mod, ctx_index, *shared,
            _rope_tables(ctx_seq, False), True)
        unpack = lambda arr: arr.reshape(ctx_batch, ctx_seq, arr.shape[-1])
        xp = _back_call(unpack(pk_c), None, None, unpack(gb_c), xp, mod, ctx_index, wuv2, wo_bf,
                        fg)
        new_ckv.append(ckv_c.reshape(ctx_batch, 1, ctx_seq, KV_RANK))
        new_kr.append(kr_c.reshape(ctx_batch, 1, ctx_seq, QK_ROPE))

        lat_index = lambda b: b
        pk_s, gb_s = _front_call(
            xs, mod, lat_index, *shared, _rope_tables(xs.shape[1], True), False)
        cache_k = jnp.concatenate(
            [cache_ckv[:, l], jnp.zeros(cache_krope[:, l].shape, _F32), cache_krope[:, l]],
            axis=-1).astype(_BF16)
        cache_v = jnp.concatenate(
            [cache_ckv[:, l], jnp.ones(cache_ckv[:, l].shape, _F32)], axis=-1).astype(_BF16)
        xs = _back_call(pk_s, cache_k, cache_v, gb_s, xs, mod, lat_index, wuv2, wo_bf, fg)
    return (xp, xs, jnp.concatenate(new_ckv, axis=1), jnp.concatenate(new_kr, axis=1))
```

```python
import functools
import math

import jax
import jax.numpy as jnp
from jax import lax
from jax.experimental import pallas as pl
from jax.experimental.pallas import tpu as pltpu

D_MODEL = 1024
GRID_W = 64
EPS = 1e-6
A_HEADS = 4
A_HEAD_DIM = 128
A_WIDTH = A_HEADS * A_HEAD_DIM
CHUNK = 128
MLA_HEADS = 4
QK_NOPE = 128
QK_ROPE = 64
V_HEAD = 128
B_WIDTH = MLA_HEADS * V_HEAD
Q_RANK = 256
KV_RANK = 128
AXIS_PAIRS = QK_ROPE // 4
ROPE_THETA = 10000.0
ATTN_SCALE = 1.0 / math.sqrt(QK_NOPE + QK_ROPE)
LOG2_E = 1.4426950408889634

QK_WIDTH = KV_RANK + 2 * QK_ROPE
OFF_U = 0
OFF_V = OFF_U + A_WIDTH
OFF_GA = OFF_V + A_WIDTH
OFF_CQ = OFF_GA + A_WIDTH
OFF_CKV = OFF_CQ + Q_RANK
OFF_KR = OFF_CKV + KV_RANK
OFF_GB = OFF_KR + 2 * QK_ROPE
IN_WIDTH = OFF_GB - QK_ROPE + B_WIDTH
IN_WIDTH2 = OFF_GB + B_WIDTH

PK_A = 0
PK_K = PK_A + A_WIDTH
PK_V = PK_K + QK_WIDTH
PK_Q = PK_V + 2 * KV_RANK
PK_WIDTH = PK_Q + MLA_HEADS * QK_WIDTH

LANES = 128
SUB_TILE = 256
TOKEN_TILE = 512
PROJ_BLOCK = 256
VMEM_LIMIT_BYTES = 56 * 1024 * 1024

_F32 = jnp.float32
_BF16 = jnp.bfloat16
_NT_DIMS = (((1,), (1,)), ((), ()))


def _silu(x):
    hx = 0.5 * x
    return hx + hx * jnp.tanh(hx)


def _gelu_tanh(x):
    return x * (0.5 * (1.0 + jnp.tanh(0.7978845608028654 * (x + 0.044715 * (x * x * x)))))


def _rmsnorm(x, g):
    ms = jnp.mean(x * x, axis=-1, keepdims=True)
    return (x * lax.rsqrt(ms + EPS)) * g


def _mod_body(cond_ref, w_ref, b_ref, o_ref):
    s = _silu(cond_ref[...])
    o_ref[...] = jnp.dot(s.astype(_BF16), w_ref[...].astype(_BF16),
                         preferred_element_type=_F32) + b_ref[...]


def _mod_call(cond, w_ada, b_ada):
    rows = cond.shape[0]
    n_out = w_ada.shape[1]
    col_block = 512
    return pl.pallas_call(
        _mod_body,
        grid=(n_out // col_block,),
        in_specs=[
            pl.BlockSpec((rows, D_MODEL), lambda j: (0, 0)),
            pl.BlockSpec((D_MODEL, col_block), lambda j: (0, j)),
            pl.BlockSpec((1, col_block), lambda j: (0, j)),
        ],
        out_specs=pl.BlockSpec((rows, col_block), lambda j: (0, j)),
        out_shape=jax.ShapeDtypeStruct((rows, n_out), _F32),
        name="adaln_mod",
    )(cond, w_ada, b_ada.reshape(1, n_out))


def _fold_body(wq_ref, wk_ref, o_ref):
    o_ref[0] = lax.dot_general(wq_ref[0], wk_ref[0], _NT_DIMS,
                               precision=lax.Precision.HIGHEST,
                               preferred_element_type=_F32)


def _fold_call(wq_nope, w_uk):
    return pl.pallas_call(
        _fold_body,
        grid=(MLA_HEADS,),
        in_specs=[
            pl.BlockSpec((1, Q_RANK, QK_NOPE), lambda h: (h, 0, 0)),
            pl.BlockSpec((1, KV_RANK, QK_NOPE), lambda h: (h, 0, 0)),
        ],
        out_specs=pl.BlockSpec((1, Q_RANK, KV_RANK), lambda h: (h, 0, 0)),
        out_shape=jax.ShapeDtypeStruct((MLA_HEADS, Q_RANK, KV_RANK), _F32),
        name="fold_q_uk",
    )(wq_nope, w_uk)


_KR_BLOCK = OFF_KR // LANES


_PREP_SPLIT = IN_WIDTH2 // 2


def _prep_body(w_ref, o_ref):
    half = QK_ROPE // 2
    blocks_per_step = _PREP_SPLIT // LANES

    def emit(step):
        base = step * _PREP_SPLIT
        for j in range(step * blocks_per_step, (step + 1) * blocks_per_step):
            if j < _KR_BLOCK:
                blk = w_ref[j * LANES - base:(j + 1) * LANES - base, :]
            elif j == _KR_BLOCK:
                kr = w_ref[OFF_KR - base:OFF_KR - base + QK_ROPE, :]
                partner = jnp.concatenate([-kr[half:, :], kr[:half, :]], axis=0)
                blk = jnp.concatenate([kr, partner], axis=0)
            else:
                lo = j * LANES - QK_ROPE - base
                blk = w_ref[lo:lo + LANES, :]
            o_ref[:, j * LANES - base:(j + 1) * LANES - base] = blk.T.astype(_BF16)

    for step in range(IN_WIDTH2 // _PREP_SPLIT):
        pl.when(pl.program_id(0) == step)(functools.partial(emit, step))


def _prep_call(w_in_t):
    assert _KR_BLOCK * LANES == OFF_KR and OFF_KR >= _PREP_SPLIT
    return pl.pallas_call(
        _prep_body,
        grid=(IN_WIDTH2 // _PREP_SPLIT,),
        in_specs=[pl.BlockSpec((_PREP_SPLIT, D_MODEL), lambda s: (s, 0))],
        out_specs=pl.BlockSpec((D_MODEL, _PREP_SPLIT), lambda s: (0, s)),
        out_shape=jax.ShapeDtypeStruct((D_MODEL, IN_WIDTH2), _BF16),
        compiler_params=pltpu.CompilerParams(vmem_limit_bytes=VMEM_LIMIT_BYTES),
        name="prep_w_in",
    )(w_in_t)


def _front_substep(x_ref, mod_ref, ng_ref, win_ref, ws_ref, bs_ref, gv_ref, qg_ref, wq_ref,
                   kvg_ref, tables, out_refs, row0, h_write, h_read, z_write, z_read, emit_cache):
    if emit_cache:
        pk_ref, gb_ref, ckv_ref, kr_ref = out_refs
    else:
        pk_ref, gb_ref = out_refs
    t1, t2k, t2q = tables
    rows = slice(row0, row0 + SUB_TILE)
    half = SUB_TILE // 2
    assert IN_WIDTH2 == 10 * PROJ_BLOCK

    def project(j):
        lo = j * PROJ_BLOCK
        z_write[:, lo:lo + PROJ_BLOCK] = jnp.dot(h_read[...], win_ref[:, lo:lo + PROJ_BLOCK],
                                                 preferred_element_type=_F32)

    shift = mod_ref[0, :, 0:D_MODEL]
    gain = ng_ref[...] * (1.0 + mod_ref[0, :, D_MODEL:2 * D_MODEL])

    def norm_rows(lo):
        x = x_ref[0, row0 + lo:row0 + lo + half, :]
        ms = jnp.mean(x * x, axis=-1, keepdims=True)
        h_write[lo:lo + half, :] = ((x * lax.rsqrt(ms + EPS)) * gain + shift).astype(_BF16)

    def vn_head(hd):
        lo = hd * A_HEAD_DIM
        v = _gelu_tanh(z_read[:, OFF_V + lo:OFF_V + lo + A_HEAD_DIM])
        return _rmsnorm(v, gv_ref[:, lo:lo + A_HEAD_DIM]).astype(_BF16)

    def mix_head(hd, vn):
        bias = bs_ref[:, hd:hd + 1]
        parts = [jnp.dot(ws_ref[hd], vn[c * CHUNK:(c + 1) * CHUNK, :],
                         preferred_element_type=_F32) + bias for c in range(SUB_TILE // CHUNK)]
        return jnp.concatenate(parts, axis=0)

    def gate_a(hd, mixed):
        lo = hd * A_HEAD_DIM
        u = _gelu_tanh(z_read[:, OFF_U + lo:OFF_U + lo + A_HEAD_DIM])
        g = _silu(z_read[:, OFF_GA + lo:OFF_GA + lo + A_HEAD_DIM])
        pk_ref[0, rows, PK_A + lo:PK_A + lo + A_HEAD_DIM] = (u * mixed * g).astype(_BF16)

    def gate_b(lo, width):
        gb_ref[0, rows, lo:lo + width] = _silu(z_read[:, OFF_GB + lo:OFF_GB + lo + width])

    def rope_q(qa, hd):
        base = hd * QK_WIDTH
        pair = qa[:, base + KV_RANK:base + QK_WIDTH]
        rope = pair * t1 + pltpu.roll(pair, QK_ROPE, 1) * t2q
        pk_ref[0, rows, PK_Q + base:PK_Q + base + QK_WIDTH] = jnp.concatenate(
            [qa[:, base:base + KV_RANK], rope], axis=1).astype(_BF16)

    project(0)
    cq = _rmsnorm(z_read[:, OFF_CQ:OFF_CQ + Q_RANK],
                  qg_ref[...] * (ATTN_SCALE * LOG2_E)).astype(_BF16)
    vn0 = vn_head(0)
    project(1)
    vn1 = vn_head(1)
    gate_b(0, B_WIDTH // 2)
    project(2)
    vn2 = vn_head(2)
    gate_b(B_WIDTH // 2, B_WIDTH // 2)
    project(3)
    vn3 = vn_head(3)
    ckv = _rmsnorm(z_read[:, OFF_CKV:OFF_CKV + KV_RANK], kvg_ref[...])
    kpair = z_read[:, OFF_KR:OFF_KR + 2 * QK_ROPE]
    krot = kpair * t1 + pltpu.roll(kpair, QK_ROPE, 1) * t2k
    pk_ref[0, rows, PK_K:PK_K + QK_WIDTH] = jnp.concatenate([ckv, krot], axis=1).astype(_BF16)
    pk_ref[0, rows, PK_V:PK_V + 2 * KV_RANK] = jnp.concatenate(
        [ckv, jnp.ones_like(ckv)], axis=1).astype(_BF16)
    if emit_cache:
        ckv_ref[0, 0, rows, :] = ckv
        kr_ref[0, 0, rows, :] = kpair[:, 0:QK_ROPE]
    qa = jnp.dot(cq, wq_ref[...], preferred_element_type=_F32)
    norm_rows(0)
    project(4)
    rope_q(qa, 0)
    rope_q(qa, 1)
    mixed0 = mix_head(0, vn0)
    mixed1 = mix_head(1, vn1)
    rope_q(qa, 2)
    rope_q(qa, 3)
    project(5)
    gate_a(0, mixed0)
    mixed2 = mix_head(2, vn2)
    mixed3 = mix_head(3, vn3)
    project(6)
    gate_a(1, mixed1)
    norm_rows(half)
    project(7)
    gate_a(2, mixed2)
    project(8)
    gate_a(3, mixed3)
    project(9)


def _front_body(x_ref, mod_ref, ng_ref, win_ref, ws_ref, bs_ref, gv_ref, qg_ref, wq_ref,
                kvg_ref, t1_ref, t2k_ref, t2q_ref, *rest, emit_cache, tiles_per_row):
    out_refs, (h_a, h_b, z_a, z_b) = rest[:-4], rest[-4:]
    t = pl.program_id(0)
    tile = x_ref.shape[1]
    assert tile == 2 * SUB_TILE
    n_pos = t1_ref.shape[0]

    @pl.when(t == 0)
    def _():
        h_b[...] = jnp.zeros(h_b.shape, _BF16)
        z_b[...] = jnp.zeros(z_b.shape, _F32)

    pos_base = (jnp.maximum(t - 1, 0) % tiles_per_row) * tile
    for sub, (h_write, h_read, z_write, z_read) in enumerate(
            ((h_a, h_b, z_a, z_b), (h_b, h_a, z_b, z_a))):
        row0 = sub * SUB_TILE
        pos = pl.multiple_of((pos_base + row0) % n_pos, SUB_TILE)
        tables = tuple(r[pl.ds(pos, SUB_TILE), :] for r in (t1_ref, t2k_ref, t2q_ref))
        _front_substep(x_ref, mod_ref, ng_ref, win_ref, ws_ref, bs_ref, gv_ref, qg_ref, wq_ref,
                       kvg_ref, tables, out_refs, row0, h_write, h_read, z_write, z_read,
                       emit_cache)


def _front_call(x, mod, mod_index, norm_g, win2, ws_bf, bs_t, g_v, q_norm_g, wq2, kv_norm_g,
                tables, emit_cache):
    batch, seq, _ = x.shape
    tile = 2 * SUB_TILE
    tiles_per_row = seq // tile
    n_tiles = batch * tiles_per_row
    t1, t2k, t2q = tables

    def in_tile(t):
        tt = jnp.minimum(t, n_tiles - 1)
        return tt // tiles_per_row, tt % tiles_per_row

    def out_tile(t):
        tt = jnp.maximum(t - 1, 0)
        return tt // tiles_per_row, tt % tiles_per_row

    full = lambda shape: pl.BlockSpec(shape, lambda t: (0,) * len(shape))
    in_specs = [
        pl.BlockSpec((1, tile, D_MODEL), lambda t: (*in_tile(t), 0)),
        pl.BlockSpec((1, 1, 3 * D_MODEL), lambda t: (mod_index(in_tile(t)[0]), 0, 0)),
        full((1, D_MODEL)),
        full((D_MODEL, IN_WIDTH2)),
        full((A_HEADS, CHUNK, CHUNK)),
        full((CHUNK, A_HEADS)),
        full((1, A_WIDTH)),
        full((1, Q_RANK)),
        full((Q_RANK, MLA_HEADS * QK_WIDTH)),
        full((1, KV_RANK)),
        full(t1.shape), full(t2k.shape), full(t2q.shape),
    ]
    out_shape = [
        jax.ShapeDtypeStruct((batch, seq, PK_WIDTH), _BF16),
        jax.ShapeDtypeStruct((batch, seq, B_WIDTH), _F32),
    ]
    out_specs = [
        pl.BlockSpec((1, tile, PK_WIDTH), lambda t: (*out_tile(t), 0)),
        pl.BlockSpec((1, tile, B_WIDTH), lambda t: (*out_tile(t), 0)),
    ]
    if emit_cache:
        out_shape += [
            jax.ShapeDtypeStruct((batch, 1, seq, KV_RANK), _F32),
            jax.ShapeDtypeStruct((batch, 1, seq, QK_ROPE), _F32),
        ]
        out_specs += [
            pl.BlockSpec((1, 1, tile, KV_RANK), lambda t: (out_tile(t)[0], 0, out_tile(t)[1], 0)),
            pl.BlockSpec((1, 1, tile, QK_ROPE), lambda t: (out_tile(t)[0], 0, out_tile(t)[1], 0)),
        ]
    return pl.pallas_call(
        functools.partial(_front_body, emit_cache=emit_cache, tiles_per_row=tiles_per_row),
        grid=(n_tiles + 1,),
        in_specs=in_specs,
        out_specs=out_specs,
        out_shape=out_shape,
        scratch_shapes=[pltpu.VMEM((SUB_TILE, D_MODEL), _BF16), pltpu.VMEM((SUB_TILE, D_MODEL), _BF16),
                        pltpu.VMEM((SUB_TILE, IN_WIDTH2), _F32),
                        pltpu.VMEM((SUB_TILE, IN_WIDTH2), _F32)],
        compiler_params=pltpu.CompilerParams(
            dimension_semantics=("arbitrary",),
            vmem_limit_bytes=VMEM_LIMIT_BYTES),
        name="front_ctx" if emit_cache else "front_lat",
    )(x, mod, norm_g, win2, ws_bf, bs_t, g_v, q_norm_g, wq2, kv_norm_g, t1, t2k, t2q)


def _scores(q, k_refs):
    return [lax.dot_general(q, k_ref[0], _NT_DIMS, preferred_element_type=_F32)
            for k_ref in k_refs]


def _softmax_values(scores, v_refs):
    m = functools.reduce(jnp.maximum, [jnp.max(s, axis=1, keepdims=True) for s in scores])
    acc = None
    for s, v_ref in zip(scores, v_refs):
        p = jnp.exp2(s - m).astype(_BF16)
        part = jnp.dot(p, v_ref[0], preferred_element_type=_F32)
        acc = part if acc is None else acc + part
    return acc[:, 0:KV_RANK] / acc[:, KV_RANK:2 * KV_RANK]


def _back_body(*refs, has_ctx):
    if has_ctx:
        (q_ref, klat_ref, vlat_ref, kctx_ref, vctx_ref, a_ref, gb_ref, x_ref, mod_ref, wuv_ref,
         wo_ref, fg_ref, y_ref, o_ref) = refs
        k_refs, v_refs = (kctx_ref, klat_ref), (vctx_ref, vlat_ref)
    else:
        (q_ref, klat_ref, vlat_ref, a_ref, gb_ref, x_ref, mod_ref, wuv_ref, wo_ref, fg_ref,
         y_ref, o_ref) = refs
        k_refs, v_refs = (klat_ref,), (vlat_ref,)
    tq = q_ref.shape[1]
    pair_width = 2 * KV_RANK

    @pl.when(pl.program_id(0) == 0)
    def _():
        o_ref[...] = jnp.zeros(o_ref.shape, o_ref.dtype)

    attn_cols = [jnp.dot(o_ref[:, pair * pair_width:(pair + 1) * pair_width], wuv_ref[pair],
                         preferred_element_type=_F32) for pair in range(MLA_HEADS // 2)]
    attn = jnp.concatenate(attn_cols, axis=1) * gb_ref[0]
    y = jnp.dot(a_ref[0], wo_ref[0:A_WIDTH, :], preferred_element_type=_F32)
    y = y + jnp.dot(attn.astype(_BF16), wo_ref[A_WIDTH:A_WIDTH + B_WIDTH, :],
                    preferred_element_type=_F32)
    gate = mod_ref[0, :, 2 * D_MODEL:3 * D_MODEL]
    out = x_ref[0] + gate * y
    y_ref[0] = _rmsnorm(out, fg_ref[...])

    groups = [
        jnp.concatenate([q_ref[0, :, hd * QK_WIDTH:(hd + 1) * QK_WIDTH] for hd in (2 * g, 2 * g + 1)],
                        axis=0) for g in range(MLA_HEADS // 2)]
    scores = [_scores(qg, k_refs) for qg in groups]
    for pair, s in enumerate(scores):
        o = _softmax_values(s, v_refs)
        o_ref[:, pair * pair_width:(pair + 1) * pair_width] = jnp.concatenate(
            [o[0:tq, :], o[tq:2 * tq, :]], axis=1).astype(_BF16)


def _back_call(packed, kctx, vctx, gb, x, mod, mod_index, wuv2, wo_bf, final_g):
    batch, seq, _ = x.shape
    tq = min(TOKEN_TILE, seq)
    tiles_per_row = seq // tq
    n_tiles = batch * tiles_per_row
    has_ctx = kctx is not None

    def attn_tile(t):
        tt = jnp.minimum(t, n_tiles - 1)
        return tt // tiles_per_row, tt % tiles_per_row

    def out_tile(t):
        tt = jnp.maximum(t - 1, 0)
        return tt // tiles_per_row, tt % tiles_per_row

    full = lambda shape: pl.BlockSpec(shape, lambda t: (0,) * len(shape))
    keys = lambda arr: pl.BlockSpec((1, arr.shape[1], arr.shape[2]),
                                    lambda t: (attn_tile(t)[0], 0, 0))
    in_specs = [
        pl.BlockSpec((1, tq, MLA_HEADS * QK_WIDTH),
                     lambda t: (*attn_tile(t), PK_Q // (MLA_HEADS * QK_WIDTH))),
        pl.BlockSpec((1, seq, QK_WIDTH), lambda t: (attn_tile(t)[0], 0, PK_K // QK_WIDTH)),
        pl.BlockSpec((1, seq, 2 * KV_RANK), lambda t: (attn_tile(t)[0], 0, PK_V // (2 * KV_RANK))),
    ]
    args = [packed, packed, packed]
    if has_ctx:
        in_specs += [keys(kctx), keys(vctx)]
        args += [kctx, vctx]
    in_specs += [
        pl.BlockSpec((1, tq, A_WIDTH), lambda t: (*out_tile(t), PK_A // A_WIDTH)),
        pl.BlockSpec((1, tq, B_WIDTH), lambda t: (*out_tile(t), 0)),
        pl.BlockSpec((1, tq, D_MODEL), lambda t: (*out_tile(t), 0)),
        pl.BlockSpec((1, 1, 3 * D_MODEL), lambda t: (mod_index(out_tile(t)[0]), 0, 0)),
        full((MLA_HEADS // 2, 2 * KV_RANK, 2 * V_HEAD)),
        full((D_MODEL, D_MODEL)),
        full((1, D_MODEL)),
    ]
    args += [packed, gb, x, mod, wuv2, wo_bf, final_g]
    return pl.pallas_call(
        functools.partial(_back_body, has_ctx=has_ctx),
        grid=(n_tiles + 1,),
        in_specs=in_specs,
        out_specs=pl.BlockSpec((1, tq, D_MODEL), lambda t: (*out_tile(t), 0)),
        out_shape=jax.ShapeDtypeStruct((batch, seq, D_MODEL), _F32),
        scratch_shapes=[pltpu.VMEM((tq, MLA_HEADS * KV_RANK), _BF16)],
        compiler_params=pltpu.CompilerParams(
            dimension_semantics=("arbitrary",),
            vmem_limit_bytes=VMEM_LIMIT_BYTES),
        name="back_lat" if has_ctx else "back_ctx",
    )(*args)


def _swap_halves(w):
    half = QK_ROPE // 2
    return jnp.concatenate([-w[..., half:], w[..., :half]], axis=-1)


def _rope_tables(n_tokens, rotate):
    ones = jnp.ones((n_tokens, QK_ROPE), _F32)
    zeros = jnp.zeros((n_tokens, QK_ROPE), _F32)
    if rotate:
        rows = n_tokens // GRID_W
        inv = ROPE_THETA ** (-jnp.arange(AXIS_PAIRS, dtype=_F32) / AXIS_PAIRS)
        ang_r = jnp.arange(rows, dtype=_F32)[:, None] * inv
        ang_c = jnp.arange(GRID_W, dtype=_F32)[:, None] * inv

        def grid_table(fn):
            tr = jnp.broadcast_to(fn(ang_r)[:, None, :], (rows, GRID_W, AXIS_PAIRS))
            tc = jnp.broadcast_to(fn(ang_c)[None, :, :], (rows, GRID_W, AXIS_PAIRS))
            half = jnp.concatenate([tr, tc], axis=-1).reshape(n_tokens, 2 * AXIS_PAIRS)
            return jnp.concatenate([half, half], axis=-1)

        cos, sin = grid_table(jnp.cos), grid_table(jnp.sin)
    else:
        cos, sin = ones, zeros
    t1 = jnp.concatenate([cos, zeros], axis=-1)
    t2k = jnp.concatenate([sin, zeros], axis=-1)
    t2q = jnp.concatenate([sin, ones], axis=-1)
    return t1, t2k, t2q


def kernel(x_prompt, x_sample, cache_ckv, cache_krope, c, c_ctx, norm_g, w_ada, b_ada, w_in, w_s,
           b_s, g_v, q_norm_g, w_uq, kv_norm_g, w_ukv, w_o, final_g):
    depth = norm_g.shape[0]
    assert depth == 1 and w_in.shape[2] == IN_WIDTH
    dec_batch = x_sample.shape[0]
    xp, xs = x_prompt, x_sample
    new_ckv, new_kr = [], []
    for l in range(depth):
        cond = jnp.concatenate(
            [c, c_ctx[None, :], jnp.zeros((16 - dec_batch - 1, D_MODEL), _F32)], axis=0)
        mod = _mod_call(cond, w_ada[l], b_ada[l]).reshape(16, 1, 3 * D_MODEL)

        win2 = _prep_call(jnp.transpose(w_in[l]))
        wq3 = w_uq[l].reshape(Q_RANK, MLA_HEADS, QK_NOPE + QK_ROPE)
        wkv3 = w_ukv[l].reshape(KV_RANK, MLA_HEADS, QK_NOPE + V_HEAD)
        wq_nope = jnp.transpose(wq3[:, :, :QK_NOPE], (1, 0, 2))
        w_uk = jnp.transpose(wkv3[:, :, :QK_NOPE], (1, 0, 2))
        w_uv = jnp.transpose(wkv3[:, :, QK_NOPE:], (1, 0, 2))
        wq_abs = _fold_call(wq_nope, w_uk)
        wq_rope = jnp.transpose(wq3[:, :, QK_NOPE:], (1, 0, 2))
        wq2 = jnp.concatenate([wq_abs, wq_rope, _swap_halves(wq_rope)], axis=-1)
        wq2 = jnp.transpose(wq2, (1, 0, 2)).reshape(Q_RANK, MLA_HEADS * QK_WIDTH).astype(_BF16)
        zero_blk = jnp.zeros((KV_RANK, V_HEAD), _F32)
        wuv2 = jnp.stack([
            jnp.block([[w_uv[2 * p], zero_blk], [zero_blk, w_uv[2 * p + 1]]])
            for p in range(MLA_HEADS // 2)]).astype(_BF16)
        wo_bf = w_o[l].astype(_BF16)
        ws_bf = w_s[l].astype(_BF16)
        bs_t = b_s[l].T
        gv_row = g_v[l].reshape(1, A_WIDTH)
        shared = (norm_g[l].reshape(1, D_MODEL), win2, ws_bf, bs_t, gv_row,
                  q_norm_g[l].reshape(1, Q_RANK), wq2, kv_norm_g[l].reshape(1, KV_RANK))
        fg = final_g.reshape(1, D_MODEL)

        ctx_index = lambda b: dec_batch
        ctx_batch, ctx_seq, _ = xp.shape
        pack = 2 * SUB_TILE // ctx_seq
        pk_c, gb_c, ckv_c, kr_c = _front_call(
            xp.reshape(ctx_batch // pack, pack * ctx_seq, D_MODEL), mod, ctx_index, *shared,
            _rope_tables(ctx_seq, False), True)
        unpack = lambda arr: arr.reshape(ctx_batch, ctx_seq, arr.shape[-1])
        xp = _back_call(unpack(pk_c), None, None, unpack(gb_c), xp, mod, ctx_index, wuv2, wo_bf,
                        fg)
        new_ckv.append(ckv_c.reshape(ctx_batch, 1, ctx_seq, KV_RANK))
        new_kr.append(kr_c.reshape(ctx_batch, 1, ctx_seq, QK_ROPE))

        lat_index = lambda b: b
        pk_s, gb_s = _front_call(
            xs, mod, lat_index, *shared, _rope_tables(xs.shape[1], True), False)
        cache_k = jnp.concatenate(
            [cache_ckv[:, l], jnp.zeros(cache_krope[:, l].shape, _F32), cache_krope[:, l]],
            axis=-1).astype(_BF16)
        cache_v = jnp.concatenate(
            [cache_ckv[:, l], jnp.ones(cache_ckv[:, l].shape, _F32)], axis=-1).astype(_BF16)
        xs = _back_call(pk_s, cache_k, cache_v, gb_s, xs, mod, lat_index, wuv2, wo_bf, fg)
    return (xp, xs, jnp.concatenate(new_ckv, axis=1), jnp.concatenate(new_kr, axis=1))
```

```python
import functools
import math

import jax
import jax.numpy as jnp
from jax import lax
from jax.experimental import pallas as pl
from jax.experimental.pallas import tpu as pltpu

D_MODEL = 1024
GRID_W = 64
EPS = 1e-6
A_HEADS = 4
A_HEAD_DIM = 128
A_WIDTH = A_HEADS * A_HEAD_DIM
CHUNK = 128
MLA_HEADS = 4
QK_NOPE = 128
QK_ROPE = 64
V_HEAD = 128
B_WIDTH = MLA_HEADS * V_HEAD
Q_RANK = 256
KV_RANK = 128
AXIS_PAIRS = QK_ROPE // 4
ROPE_THETA = 10000.0
ATTN_SCALE = 1.0 / math.sqrt(QK_NOPE + QK_ROPE)
LOG2_E = 1.4426950408889634

QK_WIDTH = KV_RANK + 2 * QK_ROPE
OFF_U = 0
OFF_V = OFF_U + A_WIDTH
OFF_GA = OFF_V + A_WIDTH
OFF_CQ = OFF_GA + A_WIDTH
OFF_CKV = OFF_CQ + Q_RANK
OFF_KR = OFF_CKV + KV_RANK
OFF_GB = OFF_KR + 2 * QK_ROPE
IN_WIDTH = OFF_GB - QK_ROPE + B_WIDTH
IN_WIDTH2 = OFF_GB + B_WIDTH

PK_A = 0
PK_K = PK_A + A_WIDTH
PK_V = PK_K + QK_WIDTH
PK_Q = PK_V + 2 * KV_RANK
PK_WIDTH = PK_Q + MLA_HEADS * QK_WIDTH

LANES = 128
SUB_TILE = 256
TOKEN_TILE = 512
PROJ_BLOCK = 256
VMEM_LIMIT_BYTES = 56 * 1024 * 1024

_F32 = jnp.float32
_BF16 = jnp.bfloat16
_NT_DIMS = (((1,), (1,)), ((), ()))


def _silu(x):
    hx = 0.5 * x
    return hx + hx * jnp.tanh(hx)


def _gelu_tanh(x):
    return x * (0.5 * (1.0 + jnp.tanh(0.7978845608028654 * (x + 0.044715 * (x * x * x)))))


def _rmsnorm(x, g):
    ms = jnp.mean(x * x, axis=-1, keepdims=True)
    return (x * lax.rsqrt(ms + EPS)) * g


def _mod_body(cond_ref, w_ref, b_ref, o_ref):
    s = _silu(cond_ref[...])
    o_ref[...] = jnp.dot(s.astype(_BF16), w_ref[...].astype(_BF16),
                         preferred_element_type=_F32) + b_ref[...]


def _mod_call(cond, w_ada, b_ada):
    rows = cond.shape[0]
    n_out = w_ada.shape[1]
    col_block = 512
    return pl.pallas_call(
        _mod_body,
        grid=(n_out // col_block,),
        in_specs=[
            pl.BlockSpec((rows, D_MODEL), lambda j: (0, 0)),
            pl.BlockSpec((D_MODEL, col_block), lambda j: (0, j)),
            pl.BlockSpec((1, col_block), lambda j: (0, j)),
        ],
        out_specs=pl.BlockSpec((rows, col_block), lambda j: (0, j)),
        out_shape=jax.ShapeDtypeStruct((rows, n_out), _F32),
        name="adaln_mod",
    )(cond, w_ada, b_ada.reshape(1, n_out))


def _fold_body(wq_ref, wk_ref, o_ref):
    o_ref[0] = lax.dot_general(wq_ref[0], wk_ref[0], _NT_DIMS,
                               precision=lax.Precision.HIGHEST,
                               preferred_element_type=_F32)


def _fold_call(wq_nope, w_uk):
    return pl.pallas_call(
        _fold_body,
        grid=(MLA_HEADS,),
        in_specs=[
            pl.BlockSpec((1, Q_RANK, QK_NOPE), lambda h: (h, 0, 0)),
            pl.BlockSpec((1, KV_RANK, QK_NOPE), lambda h: (h, 0, 0)),
        ],
        out_specs=pl.BlockSpec((1, Q_RANK, KV_RANK), lambda h: (h, 0, 0)),
        out_shape=jax.ShapeDtypeStruct((MLA_HEADS, Q_RANK, KV_RANK), _F32),
        name="fold_q_uk",
    )(wq_nope, w_uk)


_KR_BLOCK = OFF_KR // LANES


_PREP_SPLIT = IN_WIDTH2 // 2


def _prep_body(w_ref, o_ref):
    half = QK_ROPE // 2
    blocks_per_step = _PREP_SPLIT // LANES

    def emit(step):
        base = step * _PREP_SPLIT
        for j in range(step * blocks_per_step, (step + 1) * blocks_per_step):
            if j < _KR_BLOCK:
                blk = w_ref[j * LANES - base:(j + 1) * LANES - base, :]
            elif j == _KR_BLOCK:
                kr = w_ref[OFF_KR - base:OFF_KR - base + QK_ROPE, :]
                partner = jnp.concatenate([-kr[half:, :], kr[:half, :]], axis=0)
                blk = jnp.concatenate([kr, partner], axis=0)
            else:
                lo = j * LANES - QK_ROPE - base
                blk = w_ref[lo:lo + LANES, :]
            o_ref[:, j * LANES - base:(j + 1) * LANES - base] = blk.T.astype(_BF16)

    for step in range(IN_WIDTH2 // _PREP_SPLIT):
        pl.when(pl.program_id(0) == step)(functools.partial(emit, step))


def _prep_call(w_in_t):
    assert _KR_BLOCK * LANES == OFF_KR and OFF_KR >= _PREP_SPLIT
    return pl.pallas_call(
        _prep_body,
        grid=(IN_WIDTH2 // _PREP_SPLIT,),
        in_specs=[pl.BlockSpec((_PREP_SPLIT, D_MODEL), lambda s: (s, 0))],
        out_specs=pl.BlockSpec((D_MODEL, _PREP_SPLIT), lambda s: (0, s)),
        out_shape=jax.ShapeDtypeStruct((D_MODEL, IN_WIDTH2), _BF16),
        compiler_params=pltpu.CompilerParams(vmem_limit_bytes=VMEM_LIMIT_BYTES),
        name="prep_w_in",
    )(w_in_t)


def _front_substep(x_ref, mod_ref, ng_ref, win_ref, ws_ref, bs_ref, gv_ref, qg_ref, wq_ref,
                   kvg_ref, tables, out_refs, row0, h_write, h_read, z_write, z_read, emit_cache):
    if emit_cache:
        pk_ref, gb_ref, ckv_ref, kr_ref = out_refs
    else:
        pk_ref, gb_ref = out_refs
    t1, t2k, t2q = tables
    rows = slice(row0, row0 + SUB_TILE)
    half = SUB_TILE // 2
    assert IN_WIDTH2 == 10 * PROJ_BLOCK

    def project(j):
        lo = j * PROJ_BLOCK
        z_write[:, lo:lo + PROJ_BLOCK] = jnp.dot(h_read[...], win_ref[:, lo:lo + PROJ_BLOCK],
                                                 preferred_element_type=_F32)

    shift = mod_ref[0, :, 0:D_MODEL]
    gain = ng_ref[...] * (1.0 + mod_ref[0, :, D_MODEL:2 * D_MODEL])

    def norm_rows(lo):
        x = x_ref[0, row0 + lo:row0 + lo + half, :]
        ms = jnp.mean(x * x, axis=-1, keepdims=True)
        h_write[lo:lo + half, :] = ((x * lax.rsqrt(ms + EPS)) * gain + shift).astype(_BF16)

    def vn_head(hd):
        lo = hd * A_HEAD_DIM
        v = _gelu_tanh(z_read[:, OFF_V + lo:OFF_V + lo + A_HEAD_DIM])
        return _rmsnorm(v, gv_ref[:, lo:lo + A_HEAD_DIM]).astype(_BF16)

    def mix_head(hd, vn):
        bias = bs_ref[:, hd:hd + 1]
        parts = [jnp.dot(ws_ref[hd], vn[c * CHUNK:(c + 1) * CHUNK, :],
                         preferred_element_type=_F32) + bias for c in range(SUB_TILE // CHUNK)]
        return jnp.concatenate(parts, axis=0)

    def gate_a(hd, mixed):
        lo = hd * A_HEAD_DIM
        u = _gelu_tanh(z_read[:, OFF_U + lo:OFF_U + lo + A_HEAD_DIM])
        g = _silu(z_read[:, OFF_GA + lo:OFF_GA + lo + A_HEAD_DIM])
        pk_ref[0, rows, PK_A + lo:PK_A + lo + A_HEAD_DIM] = (u * mixed * g).astype(_BF16)

    def gate_b(lo, width):
        gb_ref[0, rows, lo:lo + width] = _silu(z_read[:, OFF_GB + lo:OFF_GB + lo + width])

    def rope_q(qa, hd):
        base = hd * QK_WIDTH
        pair = qa[:, base + KV_RANK:base + QK_WIDTH]
        rope = pair * t1 + pltpu.roll(pair, QK_ROPE, 1) * t2q
        pk_ref[0, rows, PK_Q + base:PK_Q + base + QK_WIDTH] = jnp.concatenate(
            [qa[:, base:base + KV_RANK], rope], axis=1).astype(_BF16)

    project(0)
    cq = _rmsnorm(z_read[:, OFF_CQ:OFF_CQ + Q_RANK],
                  qg_ref[...] * (ATTN_SCALE * LOG2_E)).astype(_BF16)
    vn0 = vn_head(0)
    project(1)
    vn1 = vn_head(1)
    gate_b(0, B_WIDTH // 2)
    project(2)
    vn2 = vn_head(2)
    gate_b(B_WIDTH // 2, B_WIDTH // 2)
    project(3)
    vn3 = vn_head(3)
    ckv = _rmsnorm(z_read[:, OFF_CKV:OFF_CKV + KV_RANK], kvg_ref[...])
    kpair = z_read[:, OFF_KR:OFF_KR + 2 * QK_ROPE]
    krot = kpair * t1 + pltpu.roll(kpair, QK_ROPE, 1) * t2k
    pk_ref[0, rows, PK_K:PK_K + QK_WIDTH] = jnp.concatenate([ckv, krot], axis=1).astype(_BF16)
    pk_ref[0, rows, PK_V:PK_V + 2 * KV_RANK] = jnp.concatenate(
        [ckv, jnp.ones_like(ckv)], axis=1).astype(_BF16)
    if emit_cache:
        ckv_ref[0, 0, rows, :] = ckv
        kr_ref[0, 0, rows, :] = kpair[:, 0:QK_ROPE]
    qa = jnp.dot(cq, wq_ref[...], preferred_element_type=_F32)
    norm_rows(0)
    project(4)
    rope_q(qa, 0)
    rope_q(qa, 1)
    mixed0 = mix_head(0, vn0)
    mixed1 = mix_head(1, vn1)
    rope_q(qa, 2)
    rope_q(qa, 3)
    project(5)
    gate_a(0, mixed0)
    mixed2 = mix_head(2, vn2)
    mixed3 = mix_head(3, vn3)
    project(6)
    gate_a(1, mixed1)
    norm_rows(half)
    project(7)
    gate_a(2, mixed2)
    project(8)
    gate_a(3, mixed3)
    project(9)


def _front_body(x_ref, mod_ref, ng_ref, win_ref, ws_ref, bs_ref, gv_ref, qg_ref, wq_ref,
                kvg_ref, t1_ref, t2k_ref, t2q_ref, *rest, emit_cache, tiles_per_row):
    out_refs, (h_a, h_b, z_a, z_b) = rest[:-4], rest[-4:]
    t = pl.program_id(0)
    tile = x_ref.shape[1]
    assert tile == 2 * SUB_TILE
    n_pos = t1_ref.shape[0]

    @pl.when(t == 0)
    def _():
        h_b[...] = jnp.zeros(h_b.shape, _BF16)
        z_b[...] = jnp.zeros(z_b.shape, _F32)

    pos_base = (jnp.maximum(t - 1, 0) % tiles_per_row) * tile
    for sub, (h_write, h_read, z_write, z_read) in enumerate(
            ((h_a, h_b, z_a, z_b), (h_b, h_a, z_b, z_a))):
        row0 = sub * SUB_TILE
        pos = pl.multiple_of((pos_base + row0) % n_pos, SUB_TILE)
        tables = tuple(r[pl.ds(pos, SUB_TILE), :] for r in (t1_ref, t2k_ref, t2q_ref))
        _front_substep(x_ref, mod_ref, ng_ref, win_ref, ws_ref, bs_ref, gv_ref, qg_ref, wq_ref,
                       kvg_ref, tables, out_refs, row0, h_write, h_read, z_write, z_read,
                       emit_cache)


def _front_call(x, mod, mod_index, norm_g, win2, ws_bf, bs_t, g_v, q_norm_g, wq2, kv_norm_g,
                tables, emit_cache):
    batch, seq, _ = x.shape
    tile = 2 * SUB_TILE
    tiles_per_row = seq // tile
    n_tiles = batch * tiles_per_row
    t1, t2k, t2q = tables

    def in_tile(t):
        tt = jnp.minimum(t, n_tiles - 1)
        return tt // tiles_per_row, tt % tiles_per_row

    def out_tile(t):
        tt = jnp.maximum(t - 1, 0)
        return tt // tiles_per_row, tt % tiles_per_row

    full = lambda shape: pl.BlockSpec(shape, lambda t: (0,) * len(shape))
    in_specs = [
        pl.BlockSpec((1, tile, D_MODEL), lambda t: (*in_tile(t), 0)),
        pl.BlockSpec((1, 1, 3 * D_MODEL), lambda t: (mod_index(in_tile(t)[0]), 0, 0)),
        full((1, D_MODEL)),
        full((D_MODEL, IN_WIDTH2)),
        full((A_HEADS, CHUNK, CHUNK)),
        full((CHUNK, A_HEADS)),
        full((1, A_WIDTH)),
        full((1, Q_RANK)),
        full((Q_RANK, MLA_HEADS * QK_WIDTH)),
        full((1, KV_RANK)),
        full(t1.shape), full(t2k.shape), full(t2q.shape),
    ]
    out_shape = [
        jax.ShapeDtypeStruct((batch, seq, PK_WIDTH), _BF16),
        jax.ShapeDtypeStruct((batch, seq, B_WIDTH), _F32),
    ]
    out_specs = [
        pl.BlockSpec((1, tile, PK_WIDTH), lambda t: (*out_tile(t), 0)),
        pl.BlockSpec((1, tile, B_WIDTH), lambda t: (*out_tile(t), 0)),
    ]
    if emit_cache:
        out_shape += [
            jax.ShapeDtypeStruct((batch, 1, seq, KV_RANK), _F32),
            jax.ShapeDtypeStruct((batch, 1, seq, QK_ROPE), _F32),
        ]
        out_specs += [
            pl.BlockSpec((1, 1, tile, KV_RANK), lambda t: (out_tile(t)[0], 0, out_tile(t)[1], 0)),
            pl.BlockSpec((1, 1, tile, QK_ROPE), lambda t: (out_tile(t)[0], 0, out_tile(t)[1], 0)),
        ]
    return pl.pallas_call(
        functools.partial(_front_body, emit_cache=emit_cache, tiles_per_row=tiles_per_row),
        grid=(n_tiles + 1,),
        in_specs=in_specs,
        out_specs=out_specs,
        out_shape=out_shape,
        scratch_shapes=[pltpu.VMEM((SUB_TILE, D_MODEL), _BF16), pltpu.VMEM((SUB_TILE, D_MODEL), _BF16),
                        pltpu.VMEM((SUB_TILE, IN_WIDTH2), _F32),
                        pltpu.VMEM((SUB_TILE, IN_WIDTH2), _F32)],
        compiler_params=pltpu.CompilerParams(
            dimension_semantics=("arbitrary",),
            vmem_limit_bytes=VMEM_LIMIT_BYTES),
        name="front_ctx" if emit_cache else "front_lat",
    )(x, mod, norm_g, win2, ws_bf, bs_t, g_v, q_norm_g, wq2, kv_norm_g, t1, t2k, t2q)


def _scores(q, k_refs, row):
    return [lax.dot_general(q, k_ref[row], _NT_DIMS, preferred_element_type=_F32)
            for k_ref in k_refs]


def _softmax_values(scores, v_refs, row):
    m = functools.reduce(jnp.maximum, [jnp.max(s, axis=1, keepdims=True) for s in scores])
    acc = None
    for s, v_ref in zip(scores, v_refs):
        p = jnp.exp2(s - m).astype(_BF16)
        part = jnp.dot(p, v_ref[row], preferred_element_type=_F32)
        acc = part if acc is None else acc + part
    return acc[:, 0:KV_RANK] / acc[:, KV_RANK:2 * KV_RANK]


def _back_body(*refs, has_ctx):
    if has_ctx:
        (q_ref, klat_ref, vlat_ref, kctx_ref, vctx_ref, a_ref, gb_ref, x_ref, mod_ref, wuv_ref,
         wo_ref, fg_ref, y_ref) = refs
        k_refs, v_refs = (kctx_ref, klat_ref), (vctx_ref, vlat_ref)
    else:
        (q_ref, klat_ref, vlat_ref, a_ref, gb_ref, x_ref, mod_ref, wuv_ref, wo_ref, fg_ref,
         y_ref) = refs
        k_refs, v_refs = (klat_ref,), (vlat_ref,)
    n_rows, tq = q_ref.shape[0], q_ref.shape[1]

    groups = [
        (row, jnp.concatenate([q_ref[row, :, hd * QK_WIDTH:(hd + 1) * QK_WIDTH]
                               for hd in (2 * g, 2 * g + 1)], axis=0))
        for row in range(n_rows) for g in range(MLA_HEADS // 2)]
    scores = [(row, _scores(qg, k_refs, row)) for row, qg in groups]
    outs = [_softmax_values(s, v_refs, row) for row, s in scores]

    n_pairs = MLA_HEADS // 2
    attn_cols = []
    for pair in range(n_pairs):
        o2 = jnp.concatenate(
            [jnp.concatenate([outs[row * n_pairs + pair][0:tq, :],
                              outs[row * n_pairs + pair][tq:2 * tq, :]], axis=1)
             for row in range(n_rows)], axis=0)
        attn_cols.append(jnp.dot(o2.astype(_BF16), wuv_ref[pair], preferred_element_type=_F32))
    flat = lambda ref: ref[...].reshape(n_rows * tq, ref.shape[-1])
    attn = jnp.concatenate(attn_cols, axis=1) * flat(gb_ref)

    y = jnp.dot(flat(a_ref), wo_ref[0:A_WIDTH, :], preferred_element_type=_F32)
    y = y + jnp.dot(attn.astype(_BF16), wo_ref[A_WIDTH:A_WIDTH + B_WIDTH, :],
                    preferred_element_type=_F32)
    gate = mod_ref[0, :, 2 * D_MODEL:3 * D_MODEL]
    out = flat(x_ref) + gate * y
    y_ref[...] = _rmsnorm(out, fg_ref[...]).reshape(y_ref.shape)


def _back_call(packed, kctx, vctx, gb, x, mod, mod_index, wuv2, wo_bf, final_g, rows_per_step):
    batch, seq, _ = x.shape
    tq = min(TOKEN_TILE, seq)
    has_ctx = kctx is not None
    rows = rows_per_step
    assert batch % rows == 0 and (rows == 1 or seq == tq)
    full = lambda shape: pl.BlockSpec(shape, lambda b, i: (0,) * len(shape))
    keys = lambda arr: pl.BlockSpec((rows, arr.shape[1], arr.shape[2]), lambda b, i: (b, 0, 0))
    in_specs = [
        pl.BlockSpec((rows, tq, MLA_HEADS * QK_WIDTH),
                     lambda b, i: (b, i, PK_Q // (MLA_HEADS * QK_WIDTH))),
        pl.BlockSpec((rows, seq, QK_WIDTH), lambda b, i: (b, 0, PK_K // QK_WIDTH)),
        pl.BlockSpec((rows, seq, 2 * KV_RANK), lambda b, i: (b, 0, PK_V // (2 * KV_RANK))),
    ]
    args = [packed, packed, packed]
    if has_ctx:
        in_specs += [keys(kctx), keys(vctx)]
        args += [kctx, vctx]
    in_specs += [
        pl.BlockSpec((rows, tq, A_WIDTH), lambda b, i: (b, i, PK_A // A_WIDTH)),
        pl.BlockSpec((rows, tq, B_WIDTH), lambda b, i: (b, i, 0)),
        pl.BlockSpec((rows, tq, D_MODEL), lambda b, i: (b, i, 0)),
        pl.BlockSpec((1, 1, 3 * D_MODEL), lambda b, i: (mod_index(b * rows), 0, 0)),
        full((MLA_HEADS // 2, 2 * KV_RANK, 2 * V_HEAD)),
        full((D_MODEL, D_MODEL)),
        full((1, D_MODEL)),
    ]
    args += [packed, gb, x, mod, wuv2, wo_bf, final_g]
    return pl.pallas_call(
        functools.partial(_back_body, has_ctx=has_ctx),
        grid=(batch // rows, seq // tq),
        in_specs=in_specs,
        out_specs=pl.BlockSpec((rows, tq, D_MODEL), lambda b, i: (b, i, 0)),
        out_shape=jax.ShapeDtypeStruct((batch, seq, D_MODEL), _F32),
        compiler_params=pltpu.CompilerParams(
            dimension_semantics=("arbitrary", "arbitrary"),
            vmem_limit_bytes=VMEM_LIMIT_BYTES),
        name="back_lat" if has_ctx else "back_ctx",
    )(*args)


def _swap_halves(w):
    half = QK_ROPE // 2
    return jnp.concatenate([-w[..., half:], w[..., :half]], axis=-1)


def _rope_tables(n_tokens, rotate):
    ones = jnp.ones((n_tokens, QK_ROPE), _F32)
    zeros = jnp.zeros((n_tokens, QK_ROPE), _F32)
    if rotate:
        rows = n_tokens // GRID_W
        inv = ROPE_THETA ** (-jnp.arange(AXIS_PAIRS, dtype=_F32) / AXIS_PAIRS)
        ang_r = jnp.arange(rows, dtype=_F32)[:, None] * inv
        ang_c = jnp.arange(GRID_W, dtype=_F32)[:, None] * inv

        def grid_table(fn):
            tr = jnp.broadcast_to(fn(ang_r)[:, None, :], (rows, GRID_W, AXIS_PAIRS))
            tc = jnp.broadcast_to(fn(ang_c)[None, :, :], (rows, GRID_W, AXIS_PAIRS))
            half = jnp.concatenate([tr, tc], axis=-1).reshape(n_tokens, 2 * AXIS_PAIRS)
            return jnp.concatenate([half, half], axis=-1)

        cos, sin = grid_table(jnp.cos), grid_table(jnp.sin)
    else:
        cos, sin = ones, zeros
    t1 = jnp.concatenate([cos, zeros], axis=-1)
    t2k = jnp.concatenate([sin, zeros], axis=-1)
    t2q = jnp.concatenate([sin, ones], axis=-1)
    return t1, t2k, t2q


def kernel(x_prompt, x_sample, cache_ckv, cache_krope, c, c_ctx, norm_g, w_ada, b_ada, w_in, w_s,
           b_s, g_v, q_norm_g, w_uq, kv_norm_g, w_ukv, w_o, final_g):
    depth = norm_g.shape[0]
    assert depth == 1 and w_in.shape[2] == IN_WIDTH
    dec_batch = x_sample.shape[0]
    xp, xs = x_prompt, x_sample
    new_ckv, new_kr = [], []
    for l in range(depth):
        cond = jnp.concatenate(
            [c, c_ctx[None, :], jnp.zeros((16 - dec_batch - 1, D_MODEL), _F32)], axis=0)
        mod = _mod_call(cond, w_ada[l], b_ada[l]).reshape(16, 1, 3 * D_MODEL)

        win2 = _prep_call(jnp.transpose(w_in[l]))
        wq3 = w_uq[l].reshape(Q_RANK, MLA_HEADS, QK_NOPE + QK_ROPE)
        wkv3 = w_ukv[l].reshape(KV_RANK, MLA_HEADS, QK_NOPE + V_HEAD)
        wq_nope = jnp.transpose(wq3[:, :, :QK_NOPE], (1, 0, 2))
        w_uk = jnp.transpose(wkv3[:, :, :QK_NOPE], (1, 0, 2))
        w_uv = jnp.transpose(wkv3[:, :, QK_NOPE:], (1, 0, 2))
        wq_abs = _fold_call(wq_nope, w_uk)
        wq_rope = jnp.transpose(wq3[:, :, QK_NOPE:], (1, 0, 2))
        wq2 = jnp.concatenate([wq_abs, wq_rope, _swap_halves(wq_rope)], axis=-1)
        wq2 = jnp.transpose(wq2, (1, 0, 2)).reshape(Q_RANK, MLA_HEADS * QK_WIDTH).astype(_BF16)
        zero_blk = jnp.zeros((KV_RANK, V_HEAD), _F32)
        wuv2 = jnp.stack([
            jnp.block([[w_uv[2 * p], zero_blk], [zero_blk, w_uv[2 * p + 1]]])
            for p in range(MLA_HEADS // 2)]).astype(_BF16)
        wo_bf = w_o[l].astype(_BF16)
        ws_bf = w_s[l].astype(_BF16)
        bs_t = b_s[l].T
        gv_row = g_v[l].reshape(1, A_WIDTH)
        shared = (norm_g[l].reshape(1, D_MODEL), win2, ws_bf, bs_t, gv_row,
                  q_norm_g[l].reshape(1, Q_RANK), wq2, kv_norm_g[l].reshape(1, KV_RANK))
        fg = final_g.reshape(1, D_MODEL)

        ctx_index = lambda b: dec_batch
        ctx_batch, ctx_seq, _ = xp.shape
        pack = 2 * SUB_TILE // ctx_seq
        pk_c, gb_c, ckv_c, kr_c = _front_call(
            xp.reshape(ctx_batch // pack, pack * ctx_seq, D_MODEL), mod, ctx_index, *shared,
            _rope_tables(ctx_seq, False), True)
        unpack = lambda arr: arr.reshape(ctx_batch, ctx_seq, arr.shape[-1])
        xp = _back_call(unpack(pk_c), None, None, unpack(gb_c), xp, mod, ctx_index, wuv2, wo_bf,
                        fg, rows_per_step=pack)
        new_ckv.append(ckv_c.reshape(ctx_batch, 1, ctx_seq, KV_RANK))
        new_kr.append(kr_c.reshape(ctx_batch, 1, ctx_seq, QK_ROPE))

        lat_index = lambda b: b
        pk_s, gb_s = _front_call(
            xs, mod, lat_index, *shared, _rope_tables(xs.shape[1], True), False)
        cache_k = jnp.concatenate(
            [cache_ckv[:, l], jnp.zeros(cache_krope[:, l].shape, _F32), cache_krope[:, l]],
            axis=-1).astype(_BF16)
        cache_v = jnp.concatenate(
            [cache_ckv[:, l], jnp.ones(cache_ckv[:, l].shape, _F32)], axis=-1).astype(_BF16)
        xs = _back_call(pk_s, cache_k, cache_v, gb_s, xs, mod, lat_index, wuv2, wo_bf, fg,
                        rows_per_step=1)
    return (xp, xs, jnp.concatenate(new_ckv, axis=1), jnp.concatenate(new_kr, axis=1))
```

```python
import functools
import math

import jax
import jax.numpy as jnp
from jax import lax
from jax.experimental import pallas as pl
from jax.experimental.pallas import tpu as pltpu

D_MODEL = 1024
GRID_W = 64
EPS = 1e-6
A_HEADS = 4
A_HEAD_DIM = 128
A_WIDTH = A_HEADS * A_HEAD_DIM
CHUNK = 128
MLA_HEADS = 4
QK_NOPE = 128
QK_ROPE = 64
V_HEAD = 128
B_WIDTH = MLA_HEADS * V_HEAD
Q_RANK = 256
KV_RANK = 128
AXIS_PAIRS = QK_ROPE // 4
ROPE_THETA = 10000.0
ATTN_SCALE = 1.0 / math.sqrt(QK_NOPE + QK_ROPE)
LOG2_E = 1.4426950408889634

QK_WIDTH = KV_RANK + 2 * QK_ROPE
OFF_U = 0
OFF_V = OFF_U + A_WIDTH
OFF_GA = OFF_V + A_WIDTH
OFF_CQ = OFF_GA + A_WIDTH
OFF_CKV = OFF_CQ + Q_RANK
OFF_KR = OFF_CKV + KV_RANK
OFF_GB = OFF_KR + 2 * QK_ROPE
IN_WIDTH = OFF_GB - QK_ROPE + B_WIDTH
IN_WIDTH2 = OFF_GB + B_WIDTH

PK_A = 0
PK_K = PK_A + A_WIDTH
PK_V = PK_K + QK_WIDTH
PK_Q = PK_V + 2 * KV_RANK
PK_WIDTH = PK_Q + MLA_HEADS * QK_WIDTH

LANES = 128
SUBLANES = 8
SUB_TILE = 256
TOKEN_TILE = 512
PROJ_BLOCK = 256
VMEM_LIMIT_BYTES = 56 * 1024 * 1024

_F32 = jnp.float32
_BF16 = jnp.bfloat16
_NT_DIMS = (((1,), (1,)), ((), ()))


def _silu(x):
    hx = 0.5 * x
    return hx + hx * jnp.tanh(hx)


def _gelu_tanh(x):
    return x * (0.5 * (1.0 + jnp.tanh(0.7978845608028654 * (x + 0.044715 * (x * x * x)))))


def _rmsnorm(x, g):
    ms = jnp.mean(x * x, axis=-1, keepdims=True)
    return (x * lax.rsqrt(ms + EPS)) * g


def _mod_body(cond_ref, w_ref, b_ref, o_ref):
    s = _silu(cond_ref[...])
    mod = jnp.dot(s.astype(_BF16), w_ref[...].astype(_BF16),
                  preferred_element_type=_F32) + b_ref[...]
    rows, cols = mod.shape
    o_ref[...] = jnp.broadcast_to(mod[:, None, :], (rows, SUBLANES, cols)).reshape(
        rows * SUBLANES, cols)


def _mod_call(cond, w_ada, b_ada):
    rows = cond.shape[0]
    n_out = w_ada.shape[1]
    col_block = D_MODEL
    return pl.pallas_call(
        _mod_body,
        grid=(n_out // col_block,),
        in_specs=[
            pl.BlockSpec((rows, D_MODEL), lambda j: (0, 0)),
            pl.BlockSpec((D_MODEL, col_block), lambda j: (0, j)),
            pl.BlockSpec((1, col_block), lambda j: (0, j)),
        ],
        out_specs=pl.BlockSpec((rows * SUBLANES, col_block), lambda j: (0, j)),
        out_shape=jax.ShapeDtypeStruct((rows * SUBLANES, n_out), _F32),
        name="adaln_mod",
    )(cond, w_ada, b_ada.reshape(1, n_out))


def _fold_body(wq_ref, wk_ref, o_ref):
    o_ref[0] = lax.dot_general(wq_ref[0], wk_ref[0], _NT_DIMS,
                               precision=lax.Precision.HIGHEST,
                               preferred_element_type=_F32)


def _fold_call(wq_nope, w_uk):
    return pl.pallas_call(
        _fold_body,
        grid=(MLA_HEADS,),
        in_specs=[
            pl.BlockSpec((1, Q_RANK, QK_NOPE), lambda h: (h, 0, 0)),
            pl.BlockSpec((1, KV_RANK, QK_NOPE), lambda h: (h, 0, 0)),
        ],
        out_specs=pl.BlockSpec((1, Q_RANK, KV_RANK), lambda h: (h, 0, 0)),
        out_shape=jax.ShapeDtypeStruct((MLA_HEADS, Q_RANK, KV_RANK), _F32),
        name="fold_q_uk",
    )(wq_nope, w_uk)


_KR_BLOCK = OFF_KR // LANES


_PREP_SPLIT = IN_WIDTH2 // 2


def _prep_body(w_ref, o_ref):
    half = QK_ROPE // 2
    blocks_per_step = _PREP_SPLIT // LANES

    def emit(step):
        base = step * _PREP_SPLIT
        for j in range(step * blocks_per_step, (step + 1) * blocks_per_step):
            if j < _KR_BLOCK:
                blk = w_ref[j * LANES - base:(j + 1) * LANES - base, :]
            elif j == _KR_BLOCK:
                kr = w_ref[OFF_KR - base:OFF_KR - base + QK_ROPE, :]
                partner = jnp.concatenate([-kr[half:, :], kr[:half, :]], axis=0)
                blk = jnp.concatenate([kr, partner], axis=0)
            else:
                lo = j * LANES - QK_ROPE - base
                blk = w_ref[lo:lo + LANES, :]
            o_ref[:, j * LANES - base:(j + 1) * LANES - base] = blk.T.astype(_BF16)

    for step in range(IN_WIDTH2 // _PREP_SPLIT):
        pl.when(pl.program_id(0) == step)(functools.partial(emit, step))


def _prep_call(w_in_t):
    assert _KR_BLOCK * LANES == OFF_KR and OFF_KR >= _PREP_SPLIT
    return pl.pallas_call(
        _prep_body,
        grid=(IN_WIDTH2 // _PREP_SPLIT,),
        in_specs=[pl.BlockSpec((_PREP_SPLIT, D_MODEL), lambda s: (s, 0))],
        out_specs=pl.BlockSpec((D_MODEL, _PREP_SPLIT), lambda s: (0, s)),
        out_shape=jax.ShapeDtypeStruct((D_MODEL, IN_WIDTH2), _BF16),
        compiler_params=pltpu.CompilerParams(vmem_limit_bytes=VMEM_LIMIT_BYTES),
        name="prep_w_in",
    )(w_in_t)


def _front_substep(x_ref, mod_ref, ng_ref, win_ref, ws_ref, bs_ref, gv_ref, qg_ref, wq_ref,
                   kvg_ref, tables, out_refs, row0, h_write, h_read, z_write, z_read, emit_cache):
    if emit_cache:
        pk_ref, gb_ref, ckv_ref, kr_ref = out_refs
    else:
        pk_ref, gb_ref = out_refs
    first_half = lax.broadcasted_iota(jnp.int32, tables.shape, 1) < QK_ROPE
    t1 = jnp.where(first_half, tables, 0.0)
    t2k = jnp.where(first_half, pltpu.roll(tables, QK_ROPE, 1), 0.0)
    t2q = jnp.where(first_half, t2k, 1.0)
    rows = slice(row0, row0 + SUB_TILE)
    half = SUB_TILE // 2
    assert IN_WIDTH2 == 10 * PROJ_BLOCK

    def project(j):
        lo = j * PROJ_BLOCK
        z_write[:, lo:lo + PROJ_BLOCK] = jnp.dot(h_read[...], win_ref[:, lo:lo + PROJ_BLOCK],
                                                 preferred_element_type=_F32)

    shift = mod_ref[0:1, 0:D_MODEL]
    gain = ng_ref[...] * (1.0 + mod_ref[0:1, D_MODEL:2 * D_MODEL])

    def norm_rows(lo):
        x = x_ref[0, row0 + lo:row0 + lo + half, :]
        ms = jnp.mean(x * x, axis=-1, keepdims=True)
        h_write[lo:lo + half, :] = ((x * lax.rsqrt(ms + EPS)) * gain + shift).astype(_BF16)

    def vn_head(hd):
        lo = hd * A_HEAD_DIM
        v = _gelu_tanh(z_read[:, OFF_V + lo:OFF_V + lo + A_HEAD_DIM])
        return _rmsnorm(v, gv_ref[:, lo:lo + A_HEAD_DIM]).astype(_BF16)

    def mix_head(hd, vn):
        bias = bs_ref[:, hd:hd + 1]
        parts = [jnp.dot(ws_ref[hd], vn[c * CHUNK:(c + 1) * CHUNK, :],
                         preferred_element_type=_F32) + bias for c in range(SUB_TILE // CHUNK)]
        return jnp.concatenate(parts, axis=0)

    def gate_a(hd, mixed):
        lo = hd * A_HEAD_DIM
        u = _gelu_tanh(z_read[:, OFF_U + lo:OFF_U + lo + A_HEAD_DIM])
        g = _silu(z_read[:, OFF_GA + lo:OFF_GA + lo + A_HEAD_DIM])
        pk_ref[0, rows, PK_A + lo:PK_A + lo + A_HEAD_DIM] = (u * mixed * g).astype(_BF16)

    def gate_b(lo, width):
        gb_ref[0, rows, lo:lo + width] = _silu(z_read[:, OFF_GB + lo:OFF_GB + lo + width])

    def rope_q(qa, hd):
        base = hd * QK_WIDTH
        pair = qa[:, base + KV_RANK:base + QK_WIDTH]
        rope = pair * t1 + pltpu.roll(pair, QK_ROPE, 1) * t2q
        pk_ref[0, rows, PK_Q + base:PK_Q + base + QK_WIDTH] = jnp.concatenate(
            [qa[:, base:base + KV_RANK], rope], axis=1).astype(_BF16)

    project(0)
    cq = _rmsnorm(z_read[:, OFF_CQ:OFF_CQ + Q_RANK],
                  qg_ref[...] * (ATTN_SCALE * LOG2_E)).astype(_BF16)
    vn0 = vn_head(0)
    project(1)
    vn1 = vn_head(1)
    gate_b(0, B_WIDTH // 2)
    project(2)
    vn2 = vn_head(2)
    gate_b(B_WIDTH // 2, B_WIDTH // 2)
    project(3)
    vn3 = vn_head(3)
    ckv = _rmsnorm(z_read[:, OFF_CKV:OFF_CKV + KV_RANK], kvg_ref[...])
    kpair = z_read[:, OFF_KR:OFF_KR + 2 * QK_ROPE]
    krot = kpair * t1 + pltpu.roll(kpair, QK_ROPE, 1) * t2k
    pk_ref[0, rows, PK_K:PK_K + QK_WIDTH] = jnp.concatenate([ckv, krot], axis=1).astype(_BF16)
    pk_ref[0, rows, PK_V:PK_V + 2 * KV_RANK] = jnp.concatenate(
        [ckv, jnp.ones_like(ckv)], axis=1).astype(_BF16)
    if emit_cache:
        ckv_ref[0, 0, rows, :] = ckv
        kr_ref[0, 0, rows, :] = kpair[:, 0:QK_ROPE]
    qa = jnp.dot(cq, wq_ref[...], preferred_element_type=_F32)
    norm_rows(0)
    project(4)
    rope_q(qa, 0)
    rope_q(qa, 1)
    mixed0 = mix_head(0, vn0)
    mixed1 = mix_head(1, vn1)
    rope_q(qa, 2)
    rope_q(qa, 3)
    project(5)
    gate_a(0, mixed0)
    mixed2 = mix_head(2, vn2)
    mixed3 = mix_head(3, vn3)
    project(6)
    gate_a(1, mixed1)
    norm_rows(half)
    project(7)
    gate_a(2, mixed2)
    project(8)
    gate_a(3, mixed3)
    project(9)


def _front_body(x_ref, mod_ref, ng_ref, win_ref, ws_ref, bs_ref, gv_ref, qg_ref, wq_ref,
                kvg_ref, tab_ref, *rest, emit_cache, tiles_per_row):
    out_refs, (h_a, h_b, z_a, z_b) = rest[:-4], rest[-4:]
    t = pl.program_id(0)
    tile = x_ref.shape[1]
    assert tile == 2 * SUB_TILE
    n_pos = tab_ref.shape[0]

    @pl.when(t == 0)
    def _():
        h_b[...] = jnp.zeros(h_b.shape, _BF16)
        z_b[...] = jnp.zeros(z_b.shape, _F32)

    pos_base = (jnp.maximum(t - 1, 0) % tiles_per_row) * tile
    for sub, (h_write, h_read, z_write, z_read) in enumerate(
            ((h_a, h_b, z_a, z_b), (h_b, h_a, z_b, z_a))):
        row0 = sub * SUB_TILE
        pos = pl.multiple_of((pos_base + row0) % n_pos, SUB_TILE)
        tables = tab_ref[pl.ds(pos, SUB_TILE), :]
        _front_substep(x_ref, mod_ref, ng_ref, win_ref, ws_ref, bs_ref, gv_ref, qg_ref, wq_ref,
                       kvg_ref, tables, out_refs, row0, h_write, h_read, z_write, z_read,
                       emit_cache)


def _front_call(x, mod, mod_index, norm_g, win2, ws_bf, bs_t, g_v, q_norm_g, wq2, kv_norm_g,
                tables, emit_cache):
    batch, seq, _ = x.shape
    tile = 2 * SUB_TILE
    tiles_per_row = seq // tile
    n_tiles = batch * tiles_per_row

    def in_tile(t):
        tt = jnp.minimum(t, n_tiles - 1)
        return tt // tiles_per_row, tt % tiles_per_row

    def out_tile(t):
        tt = jnp.maximum(t - 1, 0)
        return tt // tiles_per_row, tt % tiles_per_row

    full = lambda shape: pl.BlockSpec(shape, lambda t: (0,) * len(shape))
    in_specs = [
        pl.BlockSpec((1, tile, D_MODEL), lambda t: (*in_tile(t), 0)),
        pl.BlockSpec((SUBLANES, 3 * D_MODEL), lambda t: (mod_index(in_tile(t)[0]), 0)),
        full((1, D_MODEL)),
        full((D_MODEL, IN_WIDTH2)),
        full((A_HEADS, CHUNK, CHUNK)),
        full((CHUNK, A_HEADS)),
        full((1, A_WIDTH)),
        full((1, Q_RANK)),
        full((Q_RANK, MLA_HEADS * QK_WIDTH)),
        full((1, KV_RANK)),
        full(tables.shape),
    ]
    out_shape = [
        jax.ShapeDtypeStruct((batch, seq, PK_WIDTH), _BF16),
        jax.ShapeDtypeStruct((batch, seq, B_WIDTH), _F32),
    ]
    out_specs = [
        pl.BlockSpec((1, tile, PK_WIDTH), lambda t: (*out_tile(t), 0)),
        pl.BlockSpec((1, tile, B_WIDTH), lambda t: (*out_tile(t), 0)),
    ]
    if emit_cache:
        out_shape += [
            jax.ShapeDtypeStruct((batch, 1, seq, KV_RANK), _F32),
            jax.ShapeDtypeStruct((batch, 1, seq, QK_ROPE), _F32),
        ]
        out_specs += [
            pl.BlockSpec((1, 1, tile, KV_RANK), lambda t: (out_tile(t)[0], 0, out_tile(t)[1], 0)),
            pl.BlockSpec((1, 1, tile, QK_ROPE), lambda t: (out_tile(t)[0], 0, out_tile(t)[1], 0)),
        ]
    return pl.pallas_call(
        functools.partial(_front_body, emit_cache=emit_cache, tiles_per_row=tiles_per_row),
        grid=(n_tiles + 1,),
        in_specs=in_specs,
        out_specs=out_specs,
        out_shape=out_shape,
        scratch_shapes=[pltpu.VMEM((SUB_TILE, D_MODEL), _BF16), pltpu.VMEM((SUB_TILE, D_MODEL), _BF16),
                        pltpu.VMEM((SUB_TILE, IN_WIDTH2), _F32),
                        pltpu.VMEM((SUB_TILE, IN_WIDTH2), _F32)],
        compiler_params=pltpu.CompilerParams(
            dimension_semantics=("arbitrary",),
            vmem_limit_bytes=VMEM_LIMIT_BYTES),
        name="front_ctx" if emit_cache else "front_lat",
    )(x, mod, norm_g, win2, ws_bf, bs_t, g_v, q_norm_g, wq2, kv_norm_g, tables)


def _scores(q, k_refs, row):
    return [lax.dot_general(q, k_ref[row], _NT_DIMS, preferred_element_type=_F32)
            for k_ref in k_refs]


def _softmax_values(scores, v_refs, row):
    m = functools.reduce(jnp.maximum, [jnp.max(s, axis=1, keepdims=True) for s in scores])
    acc = None
    for s, values in zip(scores, v_refs):
        p = jnp.exp2(s - m).astype(_BF16)
        part = jnp.dot(p, values[row], preferred_element_type=_F32)
        acc = part if acc is None else acc + part
    return acc[:, 0:KV_RANK] / acc[:, KV_RANK:2 * KV_RANK]


def _back_body(*refs, has_ctx):
    if has_ctx:
        (q_ref, klat_ref, vlat_ref, kctx_ref, a_ref, gb_ref, x_ref, mod_ref, wuv_ref,
         wo_ref, fg_ref, y_ref) = refs
        k_refs, v_refs = (kctx_ref, klat_ref), (None, vlat_ref)
    else:
        (q_ref, klat_ref, vlat_ref, a_ref, gb_ref, x_ref, mod_ref, wuv_ref, wo_ref, fg_ref,
         y_ref) = refs
        k_refs, v_refs = (klat_ref,), (vlat_ref,)
    n_rows, tq = q_ref.shape[0], q_ref.shape[1]

    groups = [
        (row, jnp.concatenate([q_ref[row, :, hd * QK_WIDTH:(hd + 1) * QK_WIDTH]
                               for hd in (2 * g, 2 * g + 1)], axis=0))
        for row in range(n_rows) for g in range(MLA_HEADS // 2)]
    scores = [(row, _scores(qg, k_refs, row)) for row, qg in groups]
    if has_ctx:
        ckv = kctx_ref[:, :, 0:KV_RANK]
        v_refs = (jnp.concatenate([ckv, jnp.ones_like(ckv)], axis=-1), vlat_ref)
    outs = [_softmax_values(s, v_refs, row) for row, s in scores]

    n_pairs = MLA_HEADS // 2
    attn_cols = []
    for pair in range(n_pairs):
        o2 = jnp.concatenate(
            [jnp.concatenate([outs[row * n_pairs + pair][0:tq, :],
                              outs[row * n_pairs + pair][tq:2 * tq, :]], axis=1)
             for row in range(n_rows)], axis=0)
        attn_cols.append(jnp.dot(o2.astype(_BF16), wuv_ref[pair], preferred_element_type=_F32))
    flat = lambda ref: ref[...].reshape(n_rows * tq, ref.shape[-1])
    attn = jnp.concatenate(attn_cols, axis=1) * flat(gb_ref)

    y = jnp.dot(flat(a_ref), wo_ref[0:A_WIDTH, :], preferred_element_type=_F32)
    y = y + jnp.dot(attn.astype(_BF16), wo_ref[A_WIDTH:A_WIDTH + B_WIDTH, :],
                    preferred_element_type=_F32)
    gate = mod_ref[0:1, 2 * D_MODEL:3 * D_MODEL]
    out = flat(x_ref) + gate * y
    y_ref[...] = _rmsnorm(out, fg_ref[...]).reshape(y_ref.shape)


def _back_call(packed, kctx, gb, x, mod, mod_index, wuv2, wo_bf, final_g, rows_per_step):
    batch, seq, _ = x.shape
    tq = min(TOKEN_TILE, seq)
    has_ctx = kctx is not None
    rows = rows_per_step
    assert batch % rows == 0 and (rows == 1 or seq == tq)
    full = lambda shape: pl.BlockSpec(shape, lambda b, i: (0,) * len(shape))
    keys = lambda arr: pl.BlockSpec((rows, arr.shape[1], arr.shape[2]), lambda b, i: (b, 0, 0))
    in_specs = [
        pl.BlockSpec((rows, tq, MLA_HEADS * QK_WIDTH),
                     lambda b, i: (b, i, PK_Q // (MLA_HEADS * QK_WIDTH))),
        pl.BlockSpec((rows, seq, QK_WIDTH), lambda b, i: (b, 0, PK_K // QK_WIDTH)),
        pl.BlockSpec((rows, seq, 2 * KV_RANK), lambda b, i: (b, 0, PK_V // (2 * KV_RANK))),
    ]
    args = [packed, packed, packed]
    if has_ctx:
        in_specs += [keys(kctx)]
        args += [kctx]
    in_specs += [
        pl.BlockSpec((rows, tq, A_WIDTH), lambda b, i: (b, i, PK_A // A_WIDTH)),
        pl.BlockSpec((rows, tq, B_WIDTH), lambda b, i: (b, i, 0)),
        pl.BlockSpec((rows, tq, D_MODEL), lambda b, i: (b, i, 0)),
        pl.BlockSpec((SUBLANES, 3 * D_MODEL), lambda b, i: (mod_index(b * rows), 0)),
        full((MLA_HEADS // 2, 2 * KV_RANK, 2 * V_HEAD)),
        full((D_MODEL, D_MODEL)),
        full((1, D_MODEL)),
    ]
    args += [packed, gb, x, mod, wuv2, wo_bf, final_g]
    return pl.pallas_call(
        functools.partial(_back_body, has_ctx=has_ctx),
        grid=(batch // rows, seq // tq),
        in_specs=in_specs,
        out_specs=pl.BlockSpec((rows, tq, D_MODEL), lambda b, i: (b, i, 0)),
        out_shape=jax.ShapeDtypeStruct((batch, seq, D_MODEL), _F32),
        compiler_params=pltpu.CompilerParams(
            dimension_semantics=("arbitrary", "arbitrary"),
            vmem_limit_bytes=VMEM_LIMIT_BYTES),
        name="back_lat" if has_ctx else "back_ctx",
    )(*args)


def _swap_halves(w):
    half = QK_ROPE // 2
    return jnp.concatenate([-w[..., half:], w[..., :half]], axis=-1)


def _rope_tables(n_tokens, rotate):
    if rotate:
        rows = n_tokens // GRID_W
        inv = ROPE_THETA ** (-jnp.arange(AXIS_PAIRS, dtype=_F32) / AXIS_PAIRS)
        ang_r = jnp.arange(rows, dtype=_F32)[:, None] * inv
        ang_c = jnp.arange(GRID_W, dtype=_F32)[:, None] * inv

        def grid_table(fn):
            tr = jnp.broadcast_to(fn(ang_r)[:, None, :], (rows, GRID_W, AXIS_PAIRS))
            tc = jnp.broadcast_to(fn(ang_c)[None, :, :], (rows, GRID_W, AXIS_PAIRS))
            half = jnp.concatenate([tr, tc], axis=-1).reshape(n_tokens, 2 * AXIS_PAIRS)
            return jnp.concatenate([half, half], axis=-1)

        cos, sin = grid_table(jnp.cos), grid_table(jnp.sin)
    else:
        cos, sin = jnp.ones((n_tokens, QK_ROPE), _F32), jnp.zeros((n_tokens, QK_ROPE), _F32)
    return jnp.concatenate([cos, sin], axis=-1)


def kernel(x_prompt, x_sample, cache_ckv, cache_krope, c, c_ctx, norm_g, w_ada, b_ada, w_in, w_s,
           b_s, g_v, q_norm_g, w_uq, kv_norm_g, w_ukv, w_o, final_g):
    depth = norm_g.shape[0]
    assert depth == 1 and w_in.shape[2] == IN_WIDTH
    dec_batch = x_sample.shape[0]
    xp, xs = x_prompt, x_sample
    new_ckv, new_kr = [], []
    for l in range(depth):
        cond = jnp.concatenate(
            [c, c_ctx[None, :], jnp.zeros((16 - dec_batch - 1, D_MODEL), _F32)], axis=0)
        mod = _mod_call(cond, w_ada[l], b_ada[l])

        win2 = _prep_call(jnp.transpose(w_in[l]))
        wq3 = w_uq[l].reshape(Q_RANK, MLA_HEADS, QK_NOPE + QK_ROPE)
        wkv3 = w_ukv[l].reshape(KV_RANK, MLA_HEADS, QK_NOPE + V_HEAD)
        wq_nope = jnp.transpose(wq3[:, :, :QK_NOPE], (1, 0, 2))
        w_uk = jnp.transpose(wkv3[:, :, :QK_NOPE], (1, 0, 2))
        w_uv = jnp.transpose(wkv3[:, :, QK_NOPE:], (1, 0, 2))
        wq_abs = _fold_call(wq_nope, w_uk)
        wq_rope = jnp.transpose(wq3[:, :, QK_NOPE:], (1, 0, 2))
        wq2 = jnp.concatenate([wq_abs, wq_rope, _swap_halves(wq_rope)], axis=-1)
        wq2 = jnp.transpose(wq2, (1, 0, 2)).reshape(Q_RANK, MLA_HEADS * QK_WIDTH).astype(_BF16)
        zero_blk = jnp.zeros((KV_RANK, V_HEAD), _F32)
        wuv2 = jnp.stack([
            jnp.block([[w_uv[2 * p], zero_blk], [zero_blk, w_uv[2 * p + 1]]])
            for p in range(MLA_HEADS // 2)]).astype(_BF16)
        wo_bf = w_o[l].astype(_BF16)
        ws_bf = w_s[l].astype(_BF16)
        bs_t = b_s[l].T
        gv_row = g_v[l].reshape(1, A_WIDTH)
        shared = (norm_g[l].reshape(1, D_MODEL), win2, ws_bf, bs_t, gv_row,
                  q_norm_g[l].reshape(1, Q_RANK), wq2, kv_norm_g[l].reshape(1, KV_RANK))
        fg = final_g.reshape(1, D_MODEL)

        ctx_index = lambda b: dec_batch
        ctx_batch, ctx_seq, _ = xp.shape
        pack = 2 * SUB_TILE // ctx_seq
        pk_c, gb_c, ckv_c, kr_c = _front_call(
            xp.reshape(ctx_batch // pack, pack * ctx_seq, D_MODEL), mod, ctx_index, *shared,
            _rope_tables(ctx_seq, False), True)
        unpack = lambda arr: arr.reshape(ctx_batch, ctx_seq, arr.shape[-1])
        xp = _back_call(unpack(pk_c), None, unpack(gb_c), xp, mod, ctx_index, wuv2, wo_bf, fg,
                        rows_per_step=pack)
        new_ckv.append(ckv_c.reshape(ctx_batch, 1, ctx_seq, KV_RANK))
        new_kr.append(kr_c.reshape(ctx_batch, 1, ctx_seq, QK_ROPE))

        lat_index = lambda b: b
        pk_s, gb_s = _front_call(
            xs, mod, lat_index, *shared, _rope_tables(xs.shape[1], True), False)
        cache_k = jnp.concatenate(
            [cache_ckv[:, l], jnp.zeros(cache_krope[:, l].shape, _F32), cache_krope[:, l]],
            axis=-1).astype(_BF16)
        xs = _back_call(pk_s, cache_k, gb_s, xs, mod, lat_index, wuv2, wo_bf, fg, rows_per_step=1)
    return (xp, xs, jnp.concatenate(new_ckv, axis=1), jnp.concatenate(new_kr, axis=1))
```

```python
import functools
import math

import jax
import jax.numpy as jnp
from jax import lax
from jax.experimental import pallas as pl
from jax.experimental.pallas import tpu as pltpu

D_MODEL = 1024
GRID_W = 64
EPS = 1e-6
A_HEADS = 4
A_HEAD_DIM = 128
A_WIDTH = A_HEADS * A_HEAD_DIM
CHUNK = 128
MLA_HEADS = 4
QK_NOPE = 128
QK_ROPE = 64
V_HEAD = 128
B_WIDTH = MLA_HEADS * V_HEAD
Q_RANK = 256
KV_RANK = 128
AXIS_PAIRS = QK_ROPE // 4
ROPE_THETA = 10000.0
ATTN_SCALE = 1.0 / math.sqrt(QK_NOPE + QK_ROPE)
LOG2_E = 1.4426950408889634

QK_WIDTH = KV_RANK + 2 * QK_ROPE
OFF_U = 0
OFF_V = OFF_U + A_WIDTH
OFF_GA = OFF_V + A_WIDTH
OFF_CQ = OFF_GA + A_WIDTH
OFF_CKV = OFF_CQ + Q_RANK
OFF_KR = OFF_CKV + KV_RANK
OFF_GB = OFF_KR + 2 * QK_ROPE
IN_WIDTH = OFF_GB - QK_ROPE + B_WIDTH
IN_WIDTH2 = OFF_GB + B_WIDTH

PK_A = 0
PK_K = PK_A + A_WIDTH
PK_V = PK_K + QK_WIDTH
PK_Q = PK_V + 2 * KV_RANK
PK_WIDTH = PK_Q + MLA_HEADS * QK_WIDTH

LANES = 128
SUBLANES = 8
SUB_TILE = 256
TOKEN_TILE = 512
PROJ_BLOCK = 256
VMEM_LIMIT_BYTES = 56 * 1024 * 1024

_F32 = jnp.float32
_BF16 = jnp.bfloat16
_NT_DIMS = (((1,), (1,)), ((), ()))


def _silu(x):
    hx = 0.5 * x
    return hx + hx * jnp.tanh(hx)


def _gelu_tanh(x):
    return x * (0.5 * (1.0 + jnp.tanh(0.7978845608028654 * (x + 0.044715 * (x * x * x)))))


def _rmsnorm(x, g):
    ms = jnp.mean(x * x, axis=-1, keepdims=True)
    return (x * lax.rsqrt(ms + EPS)) * g


def _mod_body(cond_ref, w_ref, b_ref, o_ref):
    s = _silu(cond_ref[...])
    mod = jnp.dot(s.astype(_BF16), w_ref[...].astype(_BF16),
                  preferred_element_type=_F32) + b_ref[...]
    rows, cols = mod.shape
    o_ref[...] = jnp.broadcast_to(mod[:, None, :], (rows, SUBLANES, cols)).reshape(
        rows * SUBLANES, cols)


def _mod_call(cond, w_ada, b_ada):
    rows = cond.shape[0]
    n_out = w_ada.shape[1]
    col_block = D_MODEL
    return pl.pallas_call(
        _mod_body,
        grid=(n_out // col_block,),
        in_specs=[
            pl.BlockSpec((rows, D_MODEL), lambda j: (0, 0)),
            pl.BlockSpec((D_MODEL, col_block), lambda j: (0, j)),
            pl.BlockSpec((1, col_block), lambda j: (0, j)),
        ],
        out_specs=pl.BlockSpec((rows * SUBLANES, col_block), lambda j: (0, j)),
        out_shape=jax.ShapeDtypeStruct((rows * SUBLANES, n_out), _F32),
        name="adaln_mod",
    )(cond, w_ada, b_ada.reshape(1, n_out))


def _fold_body(wq_ref, wkv_ref, q_ref, uv_ref):
    half, quarter = LANES // 2, LANES // 4
    assert QK_NOPE == LANES and QK_ROPE == half and V_HEAD == LANES
    lane = lax.broadcasted_iota(jnp.int32, (Q_RANK, LANES), 1)
    low = lane < half

    def cols(block):
        return wq_ref[:, block * LANES:(block + 1) * LANES]

    for hd in range(MLA_HEADS):
        start = hd * (QK_NOPE + QK_ROPE)
        b0, odd = divmod(start, LANES)
        if odd == 0:
            nope = cols(b0)
            rope_blk = cols(b0 + 1)
            rope_lo = rope_blk
            rope_hi = pltpu.roll(rope_blk, half, 1)
        else:
            assert odd == half
            nope = jnp.where(low, pltpu.roll(cols(b0), half, 1), pltpu.roll(cols(b0 + 1), half, 1))
            rope_blk = cols(b0 + 1)
            rope_lo = pltpu.roll(rope_blk, half, 1)
            rope_hi = rope_blk
        partner = jnp.where(lane < half + quarter,
                            -pltpu.roll(rope_hi, LANES - quarter, 1),
                            pltpu.roll(rope_hi, quarter, 1))
        w_uk = wkv_ref[:, hd * (QK_NOPE + V_HEAD):hd * (QK_NOPE + V_HEAD) + QK_NOPE]
        absorbed = lax.dot_general(nope, w_uk, _NT_DIMS, precision=lax.Precision.HIGHEST,
                                   preferred_element_type=_F32)
        base = hd * QK_WIDTH
        q_ref[:, base:base + KV_RANK] = absorbed.astype(_BF16)
        q_ref[:, base + KV_RANK:base + QK_WIDTH] = jnp.where(low, rope_lo, partner).astype(_BF16)

    uv_ref[...] = jnp.zeros(uv_ref.shape, _BF16)
    for hd in range(MLA_HEADS):
        w_uv = wkv_ref[:, hd * (QK_NOPE + V_HEAD) + QK_NOPE:(hd + 1) * (QK_NOPE + V_HEAD)]
        pair, pos = divmod(hd, 2)
        uv_ref[pair, pos * KV_RANK:(pos + 1) * KV_RANK, pos * V_HEAD:(pos + 1) * V_HEAD] = (
            w_uv.astype(_BF16))


def _fold_call(w_uq, w_ukv):
    return pl.pallas_call(
        _fold_body,
        out_shape=[jax.ShapeDtypeStruct((Q_RANK, MLA_HEADS * QK_WIDTH), _BF16),
                   jax.ShapeDtypeStruct((MLA_HEADS // 2, 2 * KV_RANK, 2 * V_HEAD), _BF16)],
        name="fold_q_uk",
    )(w_uq, w_ukv)


_KR_BLOCK = OFF_KR // LANES


_PREP_SPLIT = IN_WIDTH2 // 2


def _prep_body(w_ref, o_ref):
    half = QK_ROPE // 2
    blocks_per_step = _PREP_SPLIT // LANES

    def emit(step):
        base = step * _PREP_SPLIT
        for j in range(step * blocks_per_step, (step + 1) * blocks_per_step):
            if j < _KR_BLOCK:
                blk = w_ref[j * LANES - base:(j + 1) * LANES - base, :]
            elif j == _KR_BLOCK:
                kr = w_ref[OFF_KR - base:OFF_KR - base + QK_ROPE, :]
                partner = jnp.concatenate([-kr[half:, :], kr[:half, :]], axis=0)
                blk = jnp.concatenate([kr, partner], axis=0)
            else:
                lo = j * LANES - QK_ROPE - base
                blk = w_ref[lo:lo + LANES, :]
            o_ref[:, j * LANES - base:(j + 1) * LANES - base] = blk.T.astype(_BF16)

    for step in range(IN_WIDTH2 // _PREP_SPLIT):
        pl.when(pl.program_id(0) == step)(functools.partial(emit, step))


def _prep_call(w_in_t):
    assert _KR_BLOCK * LANES == OFF_KR and OFF_KR >= _PREP_SPLIT
    return pl.pallas_call(
        _prep_body,
        grid=(IN_WIDTH2 // _PREP_SPLIT,),
        in_specs=[pl.BlockSpec((_PREP_SPLIT, D_MODEL), lambda s: (s, 0))],
        out_specs=pl.BlockSpec((D_MODEL, _PREP_SPLIT), lambda s: (0, s)),
        out_shape=jax.ShapeDtypeStruct((D_MODEL, IN_WIDTH2), _BF16),
        compiler_params=pltpu.CompilerParams(vmem_limit_bytes=VMEM_LIMIT_BYTES),
        name="prep_w_in",
    )(w_in_t)


def _front_substep(x_ref, mod_ref, ng_ref, win_ref, ws_ref, bs_ref, gv_ref, qg_ref, wq_ref,
                   kvg_ref, tables, out_refs, row0, h_write, h_read, z_write, z_read, emit_cache):
    if emit_cache:
        pk_ref, gb_ref, ckv_ref, kr_ref = out_refs
    else:
        pk_ref, gb_ref = out_refs
    first_half = lax.broadcasted_iota(jnp.int32, tables.shape, 1) < QK_ROPE
    t1 = jnp.where(first_half, tables, 0.0)
    t2k = jnp.where(first_half, pltpu.roll(tables, QK_ROPE, 1), 0.0)
    t2q = jnp.where(first_half, t2k, 1.0)
    rows = slice(row0, row0 + SUB_TILE)
    half = SUB_TILE // 2
    assert IN_WIDTH2 == 10 * PROJ_BLOCK

    def project(j):
        lo = j * PROJ_BLOCK
        z_write[:, lo:lo + PROJ_BLOCK] = jnp.dot(h_read[...], win_ref[:, lo:lo + PROJ_BLOCK],
                                                 preferred_element_type=_F32)

    shift = mod_ref[0:1, 0:D_MODEL]
    gain = ng_ref[...] * (1.0 + mod_ref[0:1, D_MODEL:2 * D_MODEL])

    def norm_rows(lo):
        x = x_ref[0, row0 + lo:row0 + lo + half, :]
        ms = jnp.mean(x * x, axis=-1, keepdims=True)
        h_write[lo:lo + half, :] = ((x * lax.rsqrt(ms + EPS)) * gain + shift).astype(_BF16)

    def vn_head(hd):
        lo = hd * A_HEAD_DIM
        v = _gelu_tanh(z_read[:, OFF_V + lo:OFF_V + lo + A_HEAD_DIM])
        return _rmsnorm(v, gv_ref[:, lo:lo + A_HEAD_DIM]).astype(_BF16)

    def mix_head(hd, vn):
        bias = bs_ref[:, hd:hd + 1]
        parts = [jnp.dot(ws_ref[hd], vn[c * CHUNK:(c + 1) * CHUNK, :],
                         preferred_element_type=_F32) + bias for c in range(SUB_TILE // CHUNK)]
        return jnp.concatenate(parts, axis=0)

    def gate_a(hd, mixed):
        lo = hd * A_HEAD_DIM
        u = _gelu_tanh(z_read[:, OFF_U + lo:OFF_U + lo + A_HEAD_DIM])
        g = _silu(z_read[:, OFF_GA + lo:OFF_GA + lo + A_HEAD_DIM])
        pk_ref[0, rows, PK_A + lo:PK_A + lo + A_HEAD_DIM] = (u * mixed * g).astype(_BF16)

    def gate_b(lo, width):
        gb_ref[0, rows, lo:lo + width] = _silu(z_read[:, OFF_GB + lo:OFF_GB + lo + width])

    def rope_q(qa, hd):
        base = hd * QK_WIDTH
        pair = qa[:, base + KV_RANK:base + QK_WIDTH]
        rope = pair * t1 + pltpu.roll(pair, QK_ROPE, 1) * t2q
        pk_ref[0, rows, PK_Q + base:PK_Q + base + QK_WIDTH] = jnp.concatenate(
            [qa[:, base:base + KV_RANK], rope], axis=1).astype(_BF16)

    project(0)
    cq = _rmsnorm(z_read[:, OFF_CQ:OFF_CQ + Q_RANK],
                  qg_ref[...] * (ATTN_SCALE * LOG2_E)).astype(_BF16)
    vn0 = vn_head(0)
    project(1)
    vn1 = vn_head(1)
    gate_b(0, B_WIDTH // 2)
    project(2)
    vn2 = vn_head(2)
    gate_b(B_WIDTH // 2, B_WIDTH // 2)
    project(3)
    vn3 = vn_head(3)
    ckv = _rmsnorm(z_read[:, OFF_CKV:OFF_CKV + KV_RANK], kvg_ref[...])
    kpair = z_read[:, OFF_KR:OFF_KR + 2 * QK_ROPE]
    krot = kpair * t1 + pltpu.roll(kpair, QK_ROPE, 1) * t2k
    pk_ref[0, rows, PK_K:PK_K + QK_WIDTH] = jnp.concatenate([ckv, krot], axis=1).astype(_BF16)
    pk_ref[0, rows, PK_V:PK_V + 2 * KV_RANK] = jnp.concatenate(
        [ckv, jnp.ones_like(ckv)], axis=1).astype(_BF16)
    if emit_cache:
        ckv_ref[0, 0, rows, :] = ckv
        kr_ref[0, 0, rows, :] = kpair[:, 0:QK_ROPE]
    qa = jnp.dot(cq, wq_ref[...], preferred_element_type=_F32)
    norm_rows(0)
    project(4)
    rope_q(qa, 0)
    rope_q(qa, 1)
    mixed0 = mix_head(0, vn0)
    mixed1 = mix_head(1, vn1)
    rope_q(qa, 2)
    rope_q(qa, 3)
    project(5)
    gate_a(0, mixed0)
    mixed2 = mix_head(2, vn2)
    mixed3 = mix_head(3, vn3)
    project(6)
    gate_a(1, mixed1)
    norm_rows(half)
    project(7)
    gate_a(2, mixed2)
    project(8)
    gate_a(3, mixed3)
    project(9)


def _front_body(x_ref, mod_ref, ng_ref, win_ref, ws_ref, bs_ref, gv_ref, qg_ref, wq_ref,
                kvg_ref, tab_ref, *rest, emit_cache, tiles_per_row):
    out_refs, (h_a, h_b, z_a, z_b) = rest[:-4], rest[-4:]
    t = pl.program_id(0)
    tile = x_ref.shape[1]
    assert tile == 2 * SUB_TILE
    n_pos = tab_ref.shape[0]

    @pl.when(t == 0)
    def _():
        h_b[...] = jnp.zeros(h_b.shape, _BF16)
        z_b[...] = jnp.zeros(z_b.shape, _F32)

    pos_base = (jnp.maximum(t - 1, 0) % tiles_per_row) * tile
    for sub, (h_write, h_read, z_write, z_read) in enumerate(
            ((h_a, h_b, z_a, z_b), (h_b, h_a, z_b, z_a))):
        row0 = sub * SUB_TILE
        pos = pl.multiple_of((pos_base + row0) % n_pos, SUB_TILE)
        tables = tab_ref[pl.ds(pos, SUB_TILE), :]
        _front_substep(x_ref, mod_ref, ng_ref, win_ref, ws_ref, bs_ref, gv_ref, qg_ref, wq_ref,
                       kvg_ref, tables, out_refs, row0, h_write, h_read, z_write, z_read,
                       emit_cache)


def _front_call(x, mod, mod_index, norm_g, win2, ws_bf, bs_t, g_v, q_norm_g, wq2, kv_norm_g,
                tables, emit_cache):
    batch, seq, _ = x.shape
    tile = 2 * SUB_TILE
    tiles_per_row = seq // tile
    n_tiles = batch * tiles_per_row
    assert seq % tile == 0 and tables.shape[0] % SUB_TILE == 0

    def in_tile(t):
        tt = jnp.minimum(t, n_tiles - 1)
        return tt // tiles_per_row, tt % tiles_per_row

    def out_tile(t):
        tt = jnp.maximum(t - 1, 0)
        return tt // tiles_per_row, tt % tiles_per_row

    full = lambda shape: pl.BlockSpec(shape, lambda t: (0,) * len(shape))
    in_specs = [
        pl.BlockSpec((1, tile, D_MODEL), lambda t: (*in_tile(t), 0)),
        pl.BlockSpec((SUBLANES, 3 * D_MODEL), lambda t: (mod_index(in_tile(t)[0]), 0)),
        full((1, D_MODEL)),
        full((D_MODEL, IN_WIDTH2)),
        full((A_HEADS, CHUNK, CHUNK)),
        full((CHUNK, A_HEADS)),
        full((1, A_WIDTH)),
        full((1, Q_RANK)),
        full((Q_RANK, MLA_HEADS * QK_WIDTH)),
        full((1, KV_RANK)),
        full(tables.shape),
    ]
    out_shape = [
        jax.ShapeDtypeStruct((batch, seq, PK_WIDTH), _BF16),
        jax.ShapeDtypeStruct((batch, seq, B_WIDTH), _F32),
    ]
    out_specs = [
        pl.BlockSpec((1, tile, PK_WIDTH), lambda t: (*out_tile(t), 0)),
        pl.BlockSpec((1, tile, B_WIDTH), lambda t: (*out_tile(t), 0)),
    ]
    if emit_cache:
        out_shape += [
            jax.ShapeDtypeStruct((batch, 1, seq, KV_RANK), _F32),
            jax.ShapeDtypeStruct((batch, 1, seq, QK_ROPE), _F32),
        ]
        out_specs += [
            pl.BlockSpec((1, 1, tile, KV_RANK), lambda t: (out_tile(t)[0], 0, out_tile(t)[1], 0)),
            pl.BlockSpec((1, 1, tile, QK_ROPE), lambda t: (out_tile(t)[0], 0, out_tile(t)[1], 0)),
        ]
    return pl.pallas_call(
        functools.partial(_front_body, emit_cache=emit_cache, tiles_per_row=tiles_per_row),
        grid=(n_tiles + 1,),
        in_specs=in_specs,
        out_specs=out_specs,
        out_shape=out_shape,
        scratch_shapes=[pltpu.VMEM((SUB_TILE, D_MODEL), _BF16), pltpu.VMEM((SUB_TILE, D_MODEL), _BF16),
                        pltpu.VMEM((SUB_TILE, IN_WIDTH2), _F32),
                        pltpu.VMEM((SUB_TILE, IN_WIDTH2), _F32)],
        compiler_params=pltpu.CompilerParams(
            dimension_semantics=("arbitrary",),
            vmem_limit_bytes=VMEM_LIMIT_BYTES),
        name="front_ctx" if emit_cache else "front_lat",
    )(x, mod, norm_g, win2, ws_bf, bs_t, g_v, q_norm_g, wq2, kv_norm_g, tables)


def _scores(q, k_refs, row):
    return [lax.dot_general(q, k_ref[row], _NT_DIMS, preferred_element_type=_F32)
            for k_ref in k_refs]


def _softmax_values(scores, v_refs, row):
    m = functools.reduce(jnp.maximum, [jnp.max(s, axis=1, keepdims=True) for s in scores])
    acc = None
    for s, values in zip(scores, v_refs):
        p = jnp.exp2(s - m).astype(_BF16)
        part = jnp.dot(p, values[row], preferred_element_type=_F32)
        acc = part if acc is None else acc + part
    return acc[:, 0:KV_RANK] / acc[:, KV_RANK:2 * KV_RANK]


def _back_body(*refs, has_ctx):
    if has_ctx:
        (q_ref, klat_ref, vlat_ref, kctx_ref, a_ref, gb_ref, x_ref, mod_ref, wuv_ref,
         wo_ref, fg_ref, y_ref) = refs
        k_refs, v_refs = (kctx_ref, klat_ref), (None, vlat_ref)
    else:
        (q_ref, klat_ref, vlat_ref, a_ref, gb_ref, x_ref, mod_ref, wuv_ref, wo_ref, fg_ref,
         y_ref) = refs
        k_refs, v_refs = (klat_ref,), (vlat_ref,)
    n_rows, tq = q_ref.shape[0], q_ref.shape[1]

    groups = [
        (row, jnp.concatenate([q_ref[row, :, hd * QK_WIDTH:(hd + 1) * QK_WIDTH]
                               for hd in (2 * g, 2 * g + 1)], axis=0))
        for row in range(n_rows) for g in range(MLA_HEADS // 2)]
    scores = [(row, _scores(qg, k_refs, row)) for row, qg in groups]
    if has_ctx:
        ckv = kctx_ref[:, :, 0:KV_RANK]
        v_refs = (jnp.concatenate([ckv, jnp.ones_like(ckv)], axis=-1), vlat_ref)
    outs = [_softmax_values(s, v_refs, row) for row, s in scores]

    n_pairs = MLA_HEADS // 2
    attn_cols = []
    for pair in range(n_pairs):
        o2 = jnp.concatenate(
            [jnp.concatenate([outs[row * n_pairs + pair][0:tq, :],
                              outs[row * n_pairs + pair][tq:2 * tq, :]], axis=1)
             for row in range(n_rows)], axis=0)
        attn_cols.append(jnp.dot(o2.astype(_BF16), wuv_ref[pair], preferred_element_type=_F32))
    flat = lambda ref: ref[...].reshape(n_rows * tq, ref.shape[-1])
    attn = jnp.concatenate(attn_cols, axis=1) * flat(gb_ref)

    y = jnp.dot(flat(a_ref), wo_ref[0:A_WIDTH, :], preferred_element_type=_F32)
    y = y + jnp.dot(attn.astype(_BF16), wo_ref[A_WIDTH:A_WIDTH + B_WIDTH, :],
                    preferred_element_type=_F32)
    gate = mod_ref[0:1, 2 * D_MODEL:3 * D_MODEL]
    out = flat(x_ref) + gate * y
    y_ref[...] = _rmsnorm(out, fg_ref[...]).reshape(y_ref.shape)


def _back_call(packed, kctx, gb, x, mod, mod_index, wuv2, wo_bf, final_g, rows_per_step):
    batch, seq, _ = x.shape
    tq = min(TOKEN_TILE, seq)
    has_ctx = kctx is not None
    rows = rows_per_step
    assert batch % rows == 0 and (rows == 1 or seq == tq)
    full = lambda shape: pl.BlockSpec(shape, lambda b, i: (0,) * len(shape))
    keys = lambda arr: pl.BlockSpec((rows, arr.shape[1], arr.shape[2]), lambda b, i: (b, 0, 0))
    in_specs = [
        pl.BlockSpec((rows, tq, MLA_HEADS * QK_WIDTH),
                     lambda b, i: (b, i, PK_Q // (MLA_HEADS * QK_WIDTH))),
        pl.BlockSpec((rows, seq, QK_WIDTH), lambda b, i: (b, 0, PK_K // QK_WIDTH)),
        pl.BlockSpec((rows, seq, 2 * KV_RANK), lambda b, i: (b, 0, PK_V // (2 * KV_RANK))),
    ]
    args = [packed, packed, packed]
    if has_ctx:
        in_specs += [keys(kctx)]
        args += [kctx]
    in_specs += [
        pl.BlockSpec((rows, tq, A_WIDTH), lambda b, i: (b, i, PK_A // A_WIDTH)),
        pl.BlockSpec((rows, tq, B_WIDTH), lambda b, i: (b, i, 0)),
        pl.BlockSpec((rows, tq, D_MODEL), lambda b, i: (b, i, 0)),
        pl.BlockSpec((SUBLANES, 3 * D_MODEL), lambda b, i: (mod_index(b * rows), 0)),
        full((MLA_HEADS // 2, 2 * KV_RANK, 2 * V_HEAD)),
        full((D_MODEL, D_MODEL)),
        full((1, D_MODEL)),
    ]
    args += [packed, gb, x, mod, wuv2, wo_bf, final_g]
    return pl.pallas_call(
        functools.partial(_back_body, has_ctx=has_ctx),
        grid=(batch // rows, seq // tq),
        in_specs=in_specs,
        out_specs=pl.BlockSpec((rows, tq, D_MODEL), lambda b, i: (b, i, 0)),
        out_shape=jax.ShapeDtypeStruct((batch, seq, D_MODEL), _F32),
        compiler_params=pltpu.CompilerParams(
            dimension_semantics=("arbitrary", "arbitrary"),
            vmem_limit_bytes=VMEM_LIMIT_BYTES),
        name="back_lat" if has_ctx else "back_ctx",
    )(*args)


def _rope_tables(n_tokens, rotate):
    if rotate:
        rows = n_tokens // GRID_W
        inv = ROPE_THETA ** (-jnp.arange(AXIS_PAIRS, dtype=_F32) / AXIS_PAIRS)
        ang_r = jnp.arange(rows, dtype=_F32)[:, None] * inv
        ang_c = jnp.arange(GRID_W, dtype=_F32)[:, None] * inv

        def grid_table(fn):
            tr = jnp.broadcast_to(fn(ang_r)[:, None, :], (rows, GRID_W, AXIS_PAIRS))
            tc = jnp.broadcast_to(fn(ang_c)[None, :, :], (rows, GRID_W, AXIS_PAIRS))
            half = jnp.concatenate([tr, tc], axis=-1).reshape(n_tokens, 2 * AXIS_PAIRS)
            return jnp.concatenate([half, half], axis=-1)

        cos, sin = grid_table(jnp.cos), grid_table(jnp.sin)
    else:
        cos, sin = jnp.ones((n_tokens, QK_ROPE), _F32), jnp.zeros((n_tokens, QK_ROPE), _F32)
    return jnp.concatenate([cos, sin], axis=-1)


def kernel(x_prompt, x_sample, cache_ckv, cache_krope, c, c_ctx, norm_g, w_ada, b_ada, w_in, w_s,
           b_s, g_v, q_norm_g, w_uq, kv_norm_g, w_ukv, w_o, final_g):
    depth = norm_g.shape[0]
    assert depth == 1 and w_in.shape[2] == IN_WIDTH
    dec_batch = x_sample.shape[0]
    xp, xs = x_prompt, x_sample
    new_ckv, new_kr = [], []
    for l in range(depth):
        cond = jnp.concatenate(
            [c, c_ctx[None, :], jnp.zeros((16 - dec_batch - 1, D_MODEL), _F32)], axis=0)
        mod = _mod_call(cond, w_ada[l], b_ada[l])

        win2 = _prep_call(jnp.transpose(w_in[l]))
        wq2, wuv2 = _fold_call(w_uq[l], w_ukv[l])
        wo_bf = w_o[l].astype(_BF16)
        ws_bf = w_s[l].astype(_BF16)
        bs_t = b_s[l].T
        gv_row = g_v[l].reshape(1, A_WIDTH)
        shared = (norm_g[l].reshape(1, D_MODEL), win2, ws_bf, bs_t, gv_row,
                  q_norm_g[l].reshape(1, Q_RANK), wq2, kv_norm_g[l].reshape(1, KV_RANK))
        fg = final_g.reshape(1, D_MODEL)

        ctx_index = lambda b: dec_batch
        ctx_batch, ctx_seq, _ = xp.shape
        pack = 2 * SUB_TILE // ctx_seq
        pk_c, gb_c, ckv_c, kr_c = _front_call(
            xp.reshape(ctx_batch // pack, pack * ctx_seq, D_MODEL), mod, ctx_index, *shared,
            _rope_tables(ctx_seq, False), True)
        unpack = lambda arr: arr.reshape(ctx_batch, ctx_seq, arr.shape[-1])
        xp = _back_call(unpack(pk_c), None, unpack(gb_c), xp, mod, ctx_index, wuv2, wo_bf, fg,
                        rows_per_step=pack)
        new_ckv.append(ckv_c.reshape(ctx_batch, 1, ctx_seq, KV_RANK))
        new_kr.append(kr_c.reshape(ctx_batch, 1, ctx_seq, QK_ROPE))

        lat_index = lambda b: b
        pk_s, gb_s = _front_call(
            xs, mod, lat_index, *shared, _rope_tables(xs.shape[1], True), False)
        cache_k = jnp.concatenate(
            [cache_ckv[:, l], jnp.zeros(cache_krope[:, l].shape, _F32), cache_krope[:, l]],
            axis=-1).astype(_BF16)
        xs = _back_call(pk_s, cache_k, gb_s, xs, mod, lat_index, wuv2, wo_bf, fg, rows_per_step=1)
    return (xp, xs, jnp.concatenate(new_ckv, axis=1), jnp.concatenate(new_kr, axis=1))
```

```python
import functools
import math

import jax
import jax.numpy as jnp
from jax import lax
from jax.experimental import pallas as pl
from jax.experimental.pallas import tpu as pltpu

D_MODEL = 1024
GRID_W = 64
EPS = 1e-6
A_HEADS = 4
A_HEAD_DIM = 128
A_WIDTH = A_HEADS * A_HEAD_DIM
CHUNK = 128
MLA_HEADS = 4
QK_NOPE = 128
QK_ROPE = 64
V_HEAD = 128
B_WIDTH = MLA_HEADS * V_HEAD
Q_RANK = 256
KV_RANK = 128
AXIS_PAIRS = QK_ROPE // 4
ROPE_THETA = 10000.0
ATTN_SCALE = 1.0 / math.sqrt(QK_NOPE + QK_ROPE)
LOG2_E = 1.4426950408889634

QK_WIDTH = KV_RANK + 2 * QK_ROPE
OFF_U = 0
OFF_V = OFF_U + A_WIDTH
OFF_GA = OFF_V + A_WIDTH
OFF_CQ = OFF_GA + A_WIDTH
OFF_CKV = OFF_CQ + Q_RANK
OFF_KR = OFF_CKV + KV_RANK
OFF_GB = OFF_KR + 2 * QK_ROPE
IN_WIDTH = OFF_GB - QK_ROPE + B_WIDTH
IN_WIDTH2 = OFF_GB + B_WIDTH

PK_A = 0
PK_K = PK_A + A_WIDTH
PK_V = PK_K + QK_WIDTH
PK_Q = PK_V + 2 * KV_RANK
PK_WIDTH = PK_Q + MLA_HEADS * QK_WIDTH

LANES = 128
SUBLANES = 8
SUB_TILE = 256
TOKEN_TILE = 512
PROJ_BLOCK = 256
VMEM_LIMIT_BYTES = 56 * 1024 * 1024

_F32 = jnp.float32
_BF16 = jnp.bfloat16
_NT_DIMS = (((1,), (1,)), ((), ()))


def _silu(x):
    hx = 0.5 * x
    return hx + hx * jnp.tanh(hx)


def _gelu_tanh(x):
    return x * (0.5 * (1.0 + jnp.tanh(0.7978845608028654 * (x + 0.044715 * (x * x * x)))))


def _rmsnorm(x, g):
    ms = jnp.mean(x * x, axis=-1, keepdims=True)
    return (x * lax.rsqrt(ms + EPS)) * g


def _mod_body(cond_ref, wa_ref, wb_ref, b_ref, o_ref):
    s = _silu(cond_ref[...]).astype(_BF16)
    halves = [jnp.dot(s, w_ref[...].astype(_BF16), preferred_element_type=_F32)
              for w_ref in (wa_ref, wb_ref)]
    mod = jnp.concatenate(halves, axis=1) + b_ref[...]
    rows, cols = mod.shape
    o_ref[...] = jnp.broadcast_to(mod[:, None, :], (rows, SUBLANES, cols)).reshape(
        rows * SUBLANES, cols)


def _mod_call(cond, w_ada, b_ada):
    rows = cond.shape[0]
    n_out = w_ada.shape[1]
    col_block = D_MODEL
    return pl.pallas_call(
        _mod_body,
        grid=(n_out // col_block,),
        in_specs=[
            pl.BlockSpec((rows, D_MODEL), lambda j: (0, 0)),
            pl.BlockSpec((D_MODEL, col_block // 2), lambda j: (0, 2 * j)),
            pl.BlockSpec((D_MODEL, col_block // 2), lambda j: (0, 2 * j + 1)),
            pl.BlockSpec((1, col_block), lambda j: (0, j)),
        ],
        out_specs=pl.BlockSpec((rows * SUBLANES, col_block), lambda j: (0, j)),
        out_shape=jax.ShapeDtypeStruct((rows * SUBLANES, n_out), _F32),
        name="adaln_mod",
    )(cond, w_ada, w_ada, b_ada.reshape(1, n_out))


def _fold_body(wq_ref, wkv_ref, q_ref, uv_ref):
    half, quarter = LANES // 2, LANES // 4
    assert QK_NOPE == LANES and QK_ROPE == half and V_HEAD == LANES
    lane = lax.broadcasted_iota(jnp.int32, (Q_RANK, LANES), 1)
    low = lane < half

    def cols(block):
        return wq_ref[:, block * LANES:(block + 1) * LANES]

    for hd in range(MLA_HEADS):
        start = hd * (QK_NOPE + QK_ROPE)
        b0, odd = divmod(start, LANES)
        if odd == 0:
            nope = cols(b0)
            rope_blk = cols(b0 + 1)
            rope_lo = rope_blk
            rope_hi = pltpu.roll(rope_blk, half, 1)
        else:
            assert odd == half
            nope = jnp.where(low, pltpu.roll(cols(b0), half, 1), pltpu.roll(cols(b0 + 1), half, 1))
            rope_blk = cols(b0 + 1)
            rope_lo = pltpu.roll(rope_blk, half, 1)
            rope_hi = rope_blk
        partner = jnp.where(lane < half + quarter,
                            -pltpu.roll(rope_hi, LANES - quarter, 1),
                            pltpu.roll(rope_hi, quarter, 1))
        w_uk = wkv_ref[:, hd * (QK_NOPE + V_HEAD):hd * (QK_NOPE + V_HEAD) + QK_NOPE]
        absorbed = lax.dot_general(nope, w_uk, _NT_DIMS, precision=lax.Precision.HIGHEST,
                                   preferred_element_type=_F32)
        base = hd * QK_WIDTH
        q_ref[:, base:base + KV_RANK] = absorbed.astype(_BF16)
        q_ref[:, base + KV_RANK:base + QK_WIDTH] = jnp.where(low, rope_lo, partner).astype(_BF16)

    uv_ref[...] = jnp.zeros(uv_ref.shape, _BF16)
    for hd in range(MLA_HEADS):
        w_uv = wkv_ref[:, hd * (QK_NOPE + V_HEAD) + QK_NOPE:(hd + 1) * (QK_NOPE + V_HEAD)]
        pair, pos = divmod(hd, 2)
        uv_ref[pair, pos * KV_RANK:(pos + 1) * KV_RANK, pos * V_HEAD:(pos + 1) * V_HEAD] = (
            w_uv.astype(_BF16))


def _fold_call(w_uq, w_ukv):
    return pl.pallas_call(
        _fold_body,
        out_shape=[jax.ShapeDtypeStruct((Q_RANK, MLA_HEADS * QK_WIDTH), _BF16),
                   jax.ShapeDtypeStruct((MLA_HEADS // 2, 2 * KV_RANK, 2 * V_HEAD), _BF16)],
        name="fold_q_uk",
    )(w_uq, w_ukv)


_KR_BLOCK = OFF_KR // LANES


_PREP_SPLIT = IN_WIDTH2 // 2


def _prep_body(w_ref, o_ref):
    half = QK_ROPE // 2
    blocks_per_step = _PREP_SPLIT // LANES

    def emit(step):
        base = step * _PREP_SPLIT
        for j in range(step * blocks_per_step, (step + 1) * blocks_per_step):
            if j < _KR_BLOCK:
                blk = w_ref[j * LANES - base:(j + 1) * LANES - base, :]
            elif j == _KR_BLOCK:
                kr = w_ref[OFF_KR - base:OFF_KR - base + QK_ROPE, :]
                partner = jnp.concatenate([-kr[half:, :], kr[:half, :]], axis=0)
                blk = jnp.concatenate([kr, partner], axis=0)
            else:
                lo = j * LANES - QK_ROPE - base
                blk = w_ref[lo:lo + LANES, :]
            o_ref[:, j * LANES - base:(j + 1) * LANES - base] = blk.T.astype(_BF16)

    for step in range(IN_WIDTH2 // _PREP_SPLIT):
        pl.when(pl.program_id(0) == step)(functools.partial(emit, step))


def _prep_call(w_in_t):
    assert _KR_BLOCK * LANES == OFF_KR and OFF_KR >= _PREP_SPLIT
    return pl.pallas_call(
        _prep_body,
        grid=(IN_WIDTH2 // _PREP_SPLIT,),
        in_specs=[pl.BlockSpec((_PREP_SPLIT, D_MODEL), lambda s: (s, 0))],
        out_specs=pl.BlockSpec((D_MODEL, _PREP_SPLIT), lambda s: (0, s)),
        out_shape=jax.ShapeDtypeStruct((D_MODEL, IN_WIDTH2), _BF16),
        compiler_params=pltpu.CompilerParams(vmem_limit_bytes=VMEM_LIMIT_BYTES),
        name="prep_w_in",
    )(w_in_t)


def _front_substep(x_ref, mod_ref, ng_ref, win_ref, ws_ref, bs_ref, gv_ref, qg_ref, wq_ref,
                   kvg_ref, tables, out_refs, row0, h_write, h_read, z_write, z_read, emit_cache):
    if emit_cache:
        pk_ref, gb_ref, ckv_ref, kr_ref = out_refs
    else:
        pk_ref, gb_ref = out_refs
    first_half = lax.broadcasted_iota(jnp.int32, tables.shape, 1) < QK_ROPE
    t1 = jnp.where(first_half, tables, 0.0)
    t2k = jnp.where(first_half, pltpu.roll(tables, QK_ROPE, 1), 0.0)
    t2q = jnp.where(first_half, t2k, 1.0)
    rows = slice(row0, row0 + SUB_TILE)
    half = SUB_TILE // 2
    assert IN_WIDTH2 == 10 * PROJ_BLOCK

    def project(j):
        lo = j * PROJ_BLOCK
        z_write[:, lo:lo + PROJ_BLOCK] = jnp.dot(h_read[...], win_ref[:, lo:lo + PROJ_BLOCK],
                                                 preferred_element_type=_F32)

    shift = mod_ref[0:1, 0:D_MODEL]
    gain = ng_ref[...] * (1.0 + mod_ref[0:1, D_MODEL:2 * D_MODEL])

    def norm_rows(lo):
        x = x_ref[0, row0 + lo:row0 + lo + half, :]
        ms = jnp.mean(x * x, axis=-1, keepdims=True)
        h_write[lo:lo + half, :] = ((x * lax.rsqrt(ms + EPS)) * gain + shift).astype(_BF16)

    def vn_head(hd):
        lo = hd * A_HEAD_DIM
        v = _gelu_tanh(z_read[:, OFF_V + lo:OFF_V + lo + A_HEAD_DIM])
        return _rmsnorm(v, gv_ref[:, lo:lo + A_HEAD_DIM]).astype(_BF16)

    def mix_head(hd, vn):
        bias = bs_ref[:, hd:hd + 1]
        parts = [jnp.dot(ws_ref[hd], vn[c * CHUNK:(c + 1) * CHUNK, :],
                         preferred_element_type=_F32) + bias for c in range(SUB_TILE // CHUNK)]
        return jnp.concatenate(parts, axis=0)

    def gate_a(hd, mixed):
        lo = hd * A_HEAD_DIM
        u = _gelu_tanh(z_read[:, OFF_U + lo:OFF_U + lo + A_HEAD_DIM])
        g = _silu(z_read[:, OFF_GA + lo:OFF_GA + lo + A_HEAD_DIM])
        pk_ref[0, rows, PK_A + lo:PK_A + lo + A_HEAD_DIM] = (u * mixed * g).astype(_BF16)

    def gate_b(lo, width):
        gb_ref[0, rows, lo:lo + width] = _silu(z_read[:, OFF_GB + lo:OFF_GB + lo + width])

    def rope_q(qa, hd):
        base = hd * QK_WIDTH
        pair = qa[:, base + KV_RANK:base + QK_WIDTH]
        rope = pair * t1 + pltpu.roll(pair, QK_ROPE, 1) * t2q
        pk_ref[0, rows, PK_Q + base:PK_Q + base + QK_WIDTH] = jnp.concatenate(
            [qa[:, base:base + KV_RANK], rope], axis=1).astype(_BF16)

    project(0)
    cq = _rmsnorm(z_read[:, OFF_CQ:OFF_CQ + Q_RANK],
                  qg_ref[...] * (ATTN_SCALE * LOG2_E)).astype(_BF16)
    vn0 = vn_head(0)
    project(1)
    vn1 = vn_head(1)
    gate_b(0, B_WIDTH // 2)
    project(2)
    vn2 = vn_head(2)
    gate_b(B_WIDTH // 2, B_WIDTH // 2)
    project(3)
    vn3 = vn_head(3)
    ckv = _rmsnorm(z_read[:, OFF_CKV:OFF_CKV + KV_RANK], kvg_ref[...])
    kpair = z_read[:, OFF_KR:OFF_KR + 2 * QK_ROPE]
    krot = kpair * t1 + pltpu.roll(kpair, QK_ROPE, 1) * t2k
    pk_ref[0, rows, PK_K:PK_K + QK_WIDTH] = jnp.concatenate([ckv, krot], axis=1).astype(_BF16)
    pk_ref[0, rows, PK_V:PK_V + 2 * KV_RANK] = jnp.concatenate(
        [ckv, jnp.ones_like(ckv)], axis=1).astype(_BF16)
    if emit_cache:
        ckv_ref[0, 0, rows, :] = ckv
        kr_ref[row0 // SUB_TILE, 0] = kpair.T[0:QK_ROPE, :]
    qa = jnp.dot(cq, wq_ref[...], preferred_element_type=_F32)
    norm_rows(0)
    project(4)
    rope_q(qa, 0)
    rope_q(qa, 1)
    mixed0 = mix_head(0, vn0)
    mixed1 = mix_head(1, vn1)
    rope_q(qa, 2)
    rope_q(qa, 3)
    project(5)
    gate_a(0, mixed0)
    mixed2 = mix_head(2, vn2)
    mixed3 = mix_head(3, vn3)
    project(6)
    gate_a(1, mixed1)
    norm_rows(half)
    project(7)
    gate_a(2, mixed2)
    project(8)
    gate_a(3, mixed3)
    project(9)


def _front_body(x_ref, mod_ref, ng_ref, win_ref, ws_ref, bs_ref, gv_ref, qg_ref, wq_ref,
                kvg_ref, tab_ref, *rest, emit_cache, tiles_per_row):
    out_refs, (h_a, h_b, z_a, z_b) = rest[:-4], rest[-4:]
    t = pl.program_id(0)
    tile = x_ref.shape[1]
    assert tile == 2 * SUB_TILE
    n_pos = tab_ref.shape[0]

    @pl.when(t == 0)
    def _():
        h_b[...] = jnp.zeros(h_b.shape, _BF16)
        z_b[...] = jnp.zeros(z_b.shape, _F32)

    pos_base = (jnp.maximum(t - 1, 0) % tiles_per_row) * tile
    for sub, (h_write, h_read, z_write, z_read) in enumerate(
            ((h_a, h_b, z_a, z_b), (h_b, h_a, z_b, z_a))):
        row0 = sub * SUB_TILE
        pos = pl.multiple_of((pos_base + row0) % n_pos, SUB_TILE)
        tables = tab_ref[pl.ds(pos, SUB_TILE), :]
        _front_substep(x_ref, mod_ref, ng_ref, win_ref, ws_ref, bs_ref, gv_ref, qg_ref, wq_ref,
                       kvg_ref, tables, out_refs, row0, h_write, h_read, z_write, z_read,
                       emit_cache)


def _front_call(x, mod, mod_index, norm_g, win2, ws_bf, bs_t, g_v, q_norm_g, wq2, kv_norm_g,
                tables, emit_cache):
    batch, seq, _ = x.shape
    tile = 2 * SUB_TILE
    tiles_per_row = seq // tile
    n_tiles = batch * tiles_per_row
    assert seq % tile == 0 and tables.shape[0] % SUB_TILE == 0

    def in_tile(t):
        tt = jnp.minimum(t, n_tiles - 1)
        return tt // tiles_per_row, tt % tiles_per_row

    def out_tile(t):
        tt = jnp.maximum(t - 1, 0)
        return tt // tiles_per_row, tt % tiles_per_row

    full = lambda shape: pl.BlockSpec(shape, lambda t: (0,) * len(shape))
    in_specs = [
        pl.BlockSpec((1, tile, D_MODEL), lambda t: (*in_tile(t), 0)),
        pl.BlockSpec((SUBLANES, 3 * D_MODEL), lambda t: (mod_index(in_tile(t)[0]), 0)),
        full((1, D_MODEL)),
        full((D_MODEL, IN_WIDTH2)),
        full((A_HEADS, CHUNK, CHUNK)),
        full((CHUNK, A_HEADS)),
        full((1, A_WIDTH)),
        full((1, Q_RANK)),
        full((Q_RANK, MLA_HEADS * QK_WIDTH)),
        full((1, KV_RANK)),
        full(tables.shape),
    ]
    out_shape = [
        jax.ShapeDtypeStruct((batch, seq, PK_WIDTH), _BF16),
        jax.ShapeDtypeStruct((batch, seq, B_WIDTH), _F32),
    ]
    out_specs = [
        pl.BlockSpec((1, tile, PK_WIDTH), lambda t: (*out_tile(t), 0)),
        pl.BlockSpec((1, tile, B_WIDTH), lambda t: (*out_tile(t), 0)),
    ]
    if emit_cache:
        out_shape += [
            jax.ShapeDtypeStruct((batch, 1, seq, KV_RANK), _F32),
            jax.ShapeDtypeStruct((n_tiles * 2, 1, QK_ROPE, SUB_TILE), _F32),
        ]
        out_specs += [
            pl.BlockSpec((1, 1, tile, KV_RANK), lambda t: (out_tile(t)[0], 0, out_tile(t)[1], 0)),
            pl.BlockSpec((2, 1, QK_ROPE, SUB_TILE), lambda t: (jnp.maximum(t - 1, 0), 0, 0, 0)),
        ]
    return pl.pallas_call(
        functools.partial(_front_body, emit_cache=emit_cache, tiles_per_row=tiles_per_row),
        grid=(n_tiles + 1,),
        in_specs=in_specs,
        out_specs=out_specs,
        out_shape=out_shape,
        scratch_shapes=[pltpu.VMEM((SUB_TILE, D_MODEL), _BF16), pltpu.VMEM((SUB_TILE, D_MODEL), _BF16),
                        pltpu.VMEM((SUB_TILE, IN_WIDTH2), _F32),
                        pltpu.VMEM((SUB_TILE, IN_WIDTH2), _F32)],
        compiler_params=pltpu.CompilerParams(
            dimension_semantics=("arbitrary",),
            vmem_limit_bytes=VMEM_LIMIT_BYTES),
        name="front_ctx" if emit_cache else "front_lat",
    )(x, mod, norm_g, win2, ws_bf, bs_t, g_v, q_norm_g, wq2, kv_norm_g, tables)


def _scores(q, k_refs, row):
    return [lax.dot_general(q, k_ref[row], _NT_DIMS, preferred_element_type=_F32)
            for k_ref in k_refs]


def _softmax_values(scores, v_refs, row):
    m = functools.reduce(jnp.maximum, [jnp.max(s, axis=1, keepdims=True) for s in scores])
    acc = None
    for s, values in zip(scores, v_refs):
        p = jnp.exp2(s - m).astype(_BF16)
        part = jnp.dot(p, values[row], preferred_element_type=_F32)
        acc = part if acc is None else acc + part
    return acc[:, 0:KV_RANK] / acc[:, KV_RANK:2 * KV_RANK]


def _back_body(*refs, has_ctx):
    if has_ctx:
        (q_ref, klat_ref, vlat_ref, kctx_ref, a_ref, gb_ref, x_ref, mod_ref, wuv_ref,
         wo_ref, fg_ref, y_ref) = refs
        k_refs, v_refs = (kctx_ref, klat_ref), (None, vlat_ref)
    else:
        (q_ref, klat_ref, vlat_ref, a_ref, gb_ref, x_ref, mod_ref, wuv_ref, wo_ref, fg_ref,
         y_ref) = refs
        k_refs, v_refs = (klat_ref,), (vlat_ref,)
    n_rows, tq = q_ref.shape[0], q_ref.shape[1]

    groups = [
        (row, jnp.concatenate([q_ref[row, :, hd * QK_WIDTH:(hd + 1) * QK_WIDTH]
                               for hd in (2 * g, 2 * g + 1)], axis=0))
        for row in range(n_rows) for g in range(MLA_HEADS // 2)]
    scores = [(row, _scores(qg, k_refs, row)) for row, qg in groups]
    if has_ctx:
        ckv = kctx_ref[:, :, 0:KV_RANK]
        v_refs = (jnp.concatenate([ckv, jnp.ones_like(ckv)], axis=-1), vlat_ref)
    outs = [_softmax_values(s, v_refs, row) for row, s in scores]

    n_pairs = MLA_HEADS // 2
    attn_cols = []
    for pair in range(n_pairs):
        o2 = jnp.concatenate(
            [jnp.concatenate([outs[row * n_pairs + pair][0:tq, :],
                              outs[row * n_pairs + pair][tq:2 * tq, :]], axis=1)
             for row in range(n_rows)], axis=0)
        attn_cols.append(jnp.dot(o2.astype(_BF16), wuv_ref[pair], preferred_element_type=_F32))
    flat = lambda ref: ref[...].reshape(n_rows * tq, ref.shape[-1])
    attn = jnp.concatenate(attn_cols, axis=1) * flat(gb_ref)

    y = jnp.dot(flat(a_ref), wo_ref[0:A_WIDTH, :], preferred_element_type=_F32)
    y = y + jnp.dot(attn.astype(_BF16), wo_ref[A_WIDTH:A_WIDTH + B_WIDTH, :],
                    preferred_element_type=_F32)
    gate = mod_ref[0:1, 2 * D_MODEL:3 * D_MODEL]
    out = flat(x_ref) + gate * y
    y_ref[...] = _rmsnorm(out, fg_ref[...]).reshape(y_ref.shape)


def _back_call(packed, kctx, gb, x, mod, mod_index, wuv2, wo_bf, final_g, rows_per_step):
    batch, seq, _ = x.shape
    tq = min(TOKEN_TILE, seq)
    has_ctx = kctx is not None
    rows = rows_per_step
    assert batch % rows == 0 and (rows == 1 or seq == tq)
    full = lambda shape: pl.BlockSpec(shape, lambda b, i: (0,) * len(shape))
    keys = lambda arr: pl.BlockSpec((rows, arr.shape[1], arr.shape[2]), lambda b, i: (b, 0, 0))
    in_specs = [
        pl.BlockSpec((rows, tq, MLA_HEADS * QK_WIDTH),
                     lambda b, i: (b, i, PK_Q // (MLA_HEADS * QK_WIDTH))),
        pl.BlockSpec((rows, seq, QK_WIDTH), lambda b, i: (b, 0, PK_K // QK_WIDTH)),
        pl.BlockSpec((rows, seq, 2 * KV_RANK), lambda b, i: (b, 0, PK_V // (2 * KV_RANK))),
    ]
    args = [packed, packed, packed]
    if has_ctx:
        in_specs += [keys(kctx)]
        args += [kctx]
    in_specs += [
        pl.BlockSpec((rows, tq, A_WIDTH), lambda b, i: (b, i, PK_A // A_WIDTH)),
        pl.BlockSpec((rows, tq, B_WIDTH), lambda b, i: (b, i, 0)),
        pl.BlockSpec((rows, tq, D_MODEL), lambda b, i: (b, i, 0)),
        pl.BlockSpec((SUBLANES, 3 * D_MODEL), lambda b, i: (mod_index(b * rows), 0)),
        full((MLA_HEADS // 2, 2 * KV_RANK, 2 * V_HEAD)),
        full((D_MODEL, D_MODEL)),
        full((1, D_MODEL)),
    ]
    args += [packed, gb, x, mod, wuv2, wo_bf, final_g]
    return pl.pallas_call(
        functools.partial(_back_body, has_ctx=has_ctx),
        grid=(batch // rows, seq // tq),
        in_specs=in_specs,
        out_specs=pl.BlockSpec((rows, tq, D_MODEL), lambda b, i: (b, i, 0)),
        out_shape=jax.ShapeDtypeStruct((batch, seq, D_MODEL), _F32),
        compiler_params=pltpu.CompilerParams(
            dimension_semantics=("arbitrary", "arbitrary"),
            vmem_limit_bytes=VMEM_LIMIT_BYTES),
        name="back_lat" if has_ctx else "back_ctx",
    )(*args)


def _rope_tables(n_tokens, rotate):
    if rotate:
        rows = n_tokens // GRID_W
        inv = ROPE_THETA ** (-jnp.arange(AXIS_PAIRS, dtype=_F32) / AXIS_PAIRS)
        ang_r = jnp.arange(rows, dtype=_F32)[:, None] * inv
        ang_c = jnp.arange(GRID_W, dtype=_F32)[:, None] * inv

        def grid_table(fn):
            tr = jnp.broadcast_to(fn(ang_r)[:, None, :], (rows, GRID_W, AXIS_PAIRS))
            tc = jnp.broadcast_to(fn(ang_c)[None, :, :], (rows, GRID_W, AXIS_PAIRS))
            half = jnp.concatenate([tr, tc], axis=-1).reshape(n_tokens, 2 * AXIS_PAIRS)
            return jnp.concatenate([half, half], axis=-1)

        cos, sin = grid_table(jnp.cos), grid_table(jnp.sin)
    else:
        cos, sin = jnp.ones((n_tokens, QK_ROPE), _F32), jnp.zeros((n_tokens, QK_ROPE), _F32)
    return jnp.concatenate([cos, sin], axis=-1)


def kernel(x_prompt, x_sample, cache_ckv, cache_krope, c, c_ctx, norm_g, w_ada, b_ada, w_in, w_s,
           b_s, g_v, q_norm_g, w_uq, kv_norm_g, w_ukv, w_o, final_g):
    depth = norm_g.shape[0]
    assert depth == 1 and w_in.shape[2] == IN_WIDTH
    dec_batch = x_sample.shape[0]
    xp, xs = x_prompt, x_sample
    new_ckv, new_kr = [], []
    for l in range(depth):
        cond = jnp.concatenate(
            [c, c_ctx[None, :], jnp.zeros((16 - dec_batch - 1, D_MODEL), _F32)], axis=0)
        mod = _mod_call(cond, w_ada[l], b_ada[l])

        win2 = _prep_call(jnp.transpose(w_in[l]))
        wq2, wuv2 = _fold_call(w_uq[l], w_ukv[l])
        wo_bf = w_o[l].astype(_BF16)
        ws_bf = w_s[l].astype(_BF16)
        bs_t = b_s[l].T
        gv_row = g_v[l].reshape(1, A_WIDTH)
        shared = (norm_g[l].reshape(1, D_MODEL), win2, ws_bf, bs_t, gv_row,
                  q_norm_g[l].reshape(1, Q_RANK), wq2, kv_norm_g[l].reshape(1, KV_RANK))
        fg = final_g.reshape(1, D_MODEL)

        ctx_index = lambda b: dec_batch
        ctx_batch, ctx_seq, _ = xp.shape
        pack = 2 * SUB_TILE // ctx_seq
        pk_c, gb_c, ckv_c, kr_c = _front_call(
            xp.reshape(ctx_batch // pack, pack * ctx_seq, D_MODEL), mod, ctx_index, *shared,
            _rope_tables(ctx_seq, False), True)
        unpack = lambda arr: arr.reshape(ctx_batch, ctx_seq, arr.shape[-1])
        xp = _back_call(unpack(pk_c), None, unpack(gb_c), xp, mod, ctx_index, wuv2, wo_bf, fg,
                        rows_per_step=pack)
        new_ckv.append(ckv_c.reshape(ctx_batch, 1, ctx_seq, KV_RANK))
        assert ctx_seq == SUB_TILE
        new_kr.append(jnp.swapaxes(kr_c, 2, 3))

        lat_index = lambda b: b
        pk_s, gb_s = _front_call(
            xs, mod, lat_index, *shared, _rope_tables(xs.shape[1], True), False)
        cache_k = jnp.concatenate(
            [cache_ckv[:, l], jnp.zeros(cache_krope[:, l].shape, _F32), cache_krope[:, l]],
            axis=-1).astype(_BF16)
        xs = _back_call(pk_s, cache_k, gb_s, xs, mod, lat_index, wuv2, wo_bf, fg, rows_per_step=1)
    return (xp, xs, jnp.concatenate(new_ckv, axis=1), jnp.concatenate(new_kr, axis=1))
```

```python
import functools
import math

import jax
import jax.numpy as jnp
from jax import lax
from jax.experimental import pallas as pl
from jax.experimental.pallas import tpu as pltpu

D_MODEL = 1024
GRID_W = 64
EPS = 1e-6
A_HEADS = 4
A_HEAD_DIM = 128
A_WIDTH = A_HEADS * A_HEAD_DIM
CHUNK = 128
MLA_HEADS = 4
QK_NOPE = 128
QK_ROPE = 64
V_HEAD = 128
B_WIDTH = MLA_HEADS * V_HEAD
Q_RANK = 256
KV_RANK = 128
AXIS_PAIRS = QK_ROPE // 4
ROPE_THETA = 10000.0
ATTN_SCALE = 1.0 / math.sqrt(QK_NOPE + QK_ROPE)
LOG2_E = 1.4426950408889634

QK_WIDTH = KV_RANK + 2 * QK_ROPE
OFF_U = 0
OFF_V = OFF_U + A_WIDTH
OFF_GA = OFF_V + A_WIDTH
OFF_CQ = OFF_GA + A_WIDTH
OFF_CKV = OFF_CQ + Q_RANK
OFF_KR = OFF_CKV + KV_RANK
OFF_GB = OFF_KR + 2 * QK_ROPE
IN_WIDTH = OFF_GB - QK_ROPE + B_WIDTH
IN_WIDTH2 = OFF_GB + B_WIDTH

PK_A = 0
PK_K = PK_A + A_WIDTH
PK_V = PK_K + QK_WIDTH
PK_Q = PK_V + 2 * KV_RANK
PK_WIDTH = PK_Q + MLA_HEADS * QK_WIDTH

LANES = 128
SUBLANES = 8
SUB_TILE = 256
TOKEN_TILE = 512
PROJ_BLOCK = 256
VMEM_LIMIT_BYTES = 56 * 1024 * 1024

_F32 = jnp.float32
_BF16 = jnp.bfloat16
_NT_DIMS = (((1,), (1,)), ((), ()))


def _silu(x):
    hx = 0.5 * x
    return hx + hx * jnp.tanh(hx)


def _gelu_tanh(x):
    return x * (0.5 * (1.0 + jnp.tanh(0.7978845608028654 * (x + 0.044715 * (x * x * x)))))


def _rmsnorm(x, g):
    ms = jnp.mean(x * x, axis=-1, keepdims=True)
    return (x * lax.rsqrt(ms + EPS)) * g


def _mod_body(cond_ref, w_ref, b_ref, o_ref):
    k = pl.program_id(0)
    s = _silu(cond_ref[...]).astype(_BF16)
    part = jnp.dot(s, w_ref[...].astype(_BF16), preferred_element_type=_F32)
    rows, cols = part.shape
    spread = lambda v: jnp.broadcast_to(v[:, None, :], (rows, SUBLANES, cols)).reshape(
        rows * SUBLANES, cols)

    @pl.when(k == 0)
    def _():
        o_ref[...] = spread(part + b_ref[...])

    @pl.when(k > 0)
    def _():
        o_ref[...] += spread(part)


def _mod_call(cond, w_ada, b_ada):
    rows = cond.shape[0]
    n_in, n_out = w_ada.shape
    k_block = 256
    return pl.pallas_call(
        _mod_body,
        grid=(n_in // k_block,),
        in_specs=[
            pl.BlockSpec((rows, k_block), lambda k: (0, k)),
            pl.BlockSpec((k_block, n_out), lambda k: (k, 0)),
            pl.BlockSpec((1, n_out), lambda k: (0, 0)),
        ],
        out_specs=pl.BlockSpec((rows * SUBLANES, n_out), lambda k: (0, 0)),
        out_shape=jax.ShapeDtypeStruct((rows * SUBLANES, n_out), _F32),
        compiler_params=pltpu.CompilerParams(dimension_semantics=("arbitrary",)),
        name="adaln_mod",
    )(cond, w_ada, b_ada.reshape(1, n_out))


def _fold_body(wq_ref, wkv_ref, q_ref, uv_ref):
    half, quarter = LANES // 2, LANES // 4
    assert QK_NOPE == LANES and QK_ROPE == half and V_HEAD == LANES
    lane = lax.broadcasted_iota(jnp.int32, (Q_RANK, LANES), 1)
    low = lane < half

    def cols(block):
        return wq_ref[:, block * LANES:(block + 1) * LANES]

    for hd in range(MLA_HEADS):
        start = hd * (QK_NOPE + QK_ROPE)
        b0, odd = divmod(start, LANES)
        if odd == 0:
            nope = cols(b0)
            rope_blk = cols(b0 + 1)
            rope_lo = rope_blk
            rope_hi = pltpu.roll(rope_blk, half, 1)
        else:
            assert odd == half
            nope = jnp.where(low, pltpu.roll(cols(b0), half, 1), pltpu.roll(cols(b0 + 1), half, 1))
            rope_blk = cols(b0 + 1)
            rope_lo = pltpu.roll(rope_blk, half, 1)
            rope_hi = rope_blk
        partner = jnp.where(lane < half + quarter,
                            -pltpu.roll(rope_hi, LANES - quarter, 1),
                            pltpu.roll(rope_hi, quarter, 1))
        w_uk = wkv_ref[:, hd * (QK_NOPE + V_HEAD):hd * (QK_NOPE + V_HEAD) + QK_NOPE]
        absorbed = lax.dot_general(nope, w_uk, _NT_DIMS, precision=lax.Precision.HIGHEST,
                                   preferred_element_type=_F32)
        base = hd * QK_WIDTH
        q_ref[:, base:base + KV_RANK] = absorbed.astype(_BF16)
        q_ref[:, base + KV_RANK:base + QK_WIDTH] = jnp.where(low, rope_lo, partner).astype(_BF16)

    uv_ref[...] = jnp.zeros(uv_ref.shape, _BF16)
    for hd in range(MLA_HEADS):
        w_uv = wkv_ref[:, hd * (QK_NOPE + V_HEAD) + QK_NOPE:(hd + 1) * (QK_NOPE + V_HEAD)]
        pair, pos = divmod(hd, 2)
        uv_ref[pair, pos * KV_RANK:(pos + 1) * KV_RANK, pos * V_HEAD:(pos + 1) * V_HEAD] = (
            w_uv.astype(_BF16))


def _fold_call(w_uq, w_ukv):
    return pl.pallas_call(
        _fold_body,
        out_shape=[jax.ShapeDtypeStruct((Q_RANK, MLA_HEADS * QK_WIDTH), _BF16),
                   jax.ShapeDtypeStruct((MLA_HEADS // 2, 2 * KV_RANK, 2 * V_HEAD), _BF16)],
        name="fold_q_uk",
    )(w_uq, w_ukv)


_KR_BLOCK = OFF_KR // LANES


_PREP_SPLIT = IN_WIDTH2 // 2


def _prep_body(w_ref, o_ref):
    half = QK_ROPE // 2
    blocks_per_step = _PREP_SPLIT // LANES

    def emit(step):
        base = step * _PREP_SPLIT
        for j in range(step * blocks_per_step, (step + 1) * blocks_per_step):
            if j < _KR_BLOCK:
                blk = w_ref[j * LANES - base:(j + 1) * LANES - base, :]
            elif j == _KR_BLOCK:
                kr = w_ref[OFF_KR - base:OFF_KR - base + QK_ROPE, :]
                partner = jnp.concatenate([-kr[half:, :], kr[:half, :]], axis=0)
                blk = jnp.concatenate([kr, partner], axis=0)
            else:
                lo = j * LANES - QK_ROPE - base
                blk = w_ref[lo:lo + LANES, :]
            o_ref[:, j * LANES - base:(j + 1) * LANES - base] = blk.T.astype(_BF16)

    for step in range(IN_WIDTH2 // _PREP_SPLIT):
        pl.when(pl.program_id(0) == step)(functools.partial(emit, step))


def _prep_call(w_in_t):
    assert _KR_BLOCK * LANES == OFF_KR and OFF_KR >= _PREP_SPLIT
    return pl.pallas_call(
        _prep_body,
        grid=(IN_WIDTH2 // _PREP_SPLIT,),
        in_specs=[pl.BlockSpec((_PREP_SPLIT, D_MODEL), lambda s: (s, 0))],
        out_specs=pl.BlockSpec((D_MODEL, _PREP_SPLIT), lambda s: (0, s)),
        out_shape=jax.ShapeDtypeStruct((D_MODEL, IN_WIDTH2), _BF16),
        compiler_params=pltpu.CompilerParams(vmem_limit_bytes=VMEM_LIMIT_BYTES),
        name="prep_w_in",
    )(w_in_t)


def _front_substep(x_ref, mod_ref, ng_ref, win_ref, ws_ref, bs_ref, gv_ref, qg_ref, wq_ref,
                   kvg_ref, tables, out_refs, row0, h_write, h_read, z_write, z_read, emit_cache):
    if emit_cache:
        pk_ref, gb_ref, ckv_ref, kr_ref = out_refs
    else:
        pk_ref, gb_ref = out_refs
    first_half = lax.broadcasted_iota(jnp.int32, tables.shape, 1) < QK_ROPE
    t1 = jnp.where(first_half, tables, 0.0)
    t2k = jnp.where(first_half, pltpu.roll(tables, QK_ROPE, 1), 0.0)
    t2q = jnp.where(first_half, t2k, 1.0)
    rows = slice(row0, row0 + SUB_TILE)
    half = SUB_TILE // 2
    assert IN_WIDTH2 == 10 * PROJ_BLOCK

    def project(j):
        lo = j * PROJ_BLOCK
        z_write[:, lo:lo + PROJ_BLOCK] = jnp.dot(h_read[...], win_ref[:, lo:lo + PROJ_BLOCK],
                                                 preferred_element_type=_F32)

    shift = mod_ref[0:1, 0:D_MODEL]
    gain = ng_ref[...] * (1.0 + mod_ref[0:1, D_MODEL:2 * D_MODEL])

    def norm_rows(lo):
        x = x_ref[0, row0 + lo:row0 + lo + half, :]
        ms = jnp.mean(x * x, axis=-1, keepdims=True)
        h_write[lo:lo + half, :] = ((x * lax.rsqrt(ms + EPS)) * gain + shift).astype(_BF16)

    def vn_head(hd):
        lo = hd * A_HEAD_DIM
        v = _gelu_tanh(z_read[:, OFF_V + lo:OFF_V + lo + A_HEAD_DIM])
        return _rmsnorm(v, gv_ref[:, lo:lo + A_HEAD_DIM]).astype(_BF16)

    def mix_head(hd, vn):
        bias = bs_ref[:, hd:hd + 1]
        parts = [jnp.dot(ws_ref[hd], vn[c * CHUNK:(c + 1) * CHUNK, :],
                         preferred_element_type=_F32) + bias for c in range(SUB_TILE // CHUNK)]
        return jnp.concatenate(parts, axis=0)

    def gate_a(hd, mixed):
        lo = hd * A_HEAD_DIM
        u = _gelu_tanh(z_read[:, OFF_U + lo:OFF_U + lo + A_HEAD_DIM])
        g = _silu(z_read[:, OFF_GA + lo:OFF_GA + lo + A_HEAD_DIM])
        pk_ref[0, rows, PK_A + lo:PK_A + lo + A_HEAD_DIM] = (u * mixed * g).astype(_BF16)

    def gate_b(lo, width):
        gb_ref[0, rows, lo:lo + width] = _silu(z_read[:, OFF_GB + lo:OFF_GB + lo + width])

    def rope_q(qa, hd):
        base = hd * QK_WIDTH
        pair = qa[:, base + KV_RANK:base + QK_WIDTH]
        rope = pair * t1 + pltpu.roll(pair, QK_ROPE, 1) * t2q
        pk_ref[0, rows, PK_Q + base:PK_Q + base + QK_WIDTH] = jnp.concatenate(
            [qa[:, base:base + KV_RANK], rope], axis=1).astype(_BF16)

    project(0)
    cq = _rmsnorm(z_read[:, OFF_CQ:OFF_CQ + Q_RANK],
                  qg_ref[...] * (ATTN_SCALE * LOG2_E)).astype(_BF16)
    vn0 = vn_head(0)
    project(1)
    vn1 = vn_head(1)
    gate_b(0, B_WIDTH // 2)
    project(2)
    vn2 = vn_head(2)
    gate_b(B_WIDTH // 2, B_WIDTH // 2)
    project(3)
    vn3 = vn_head(3)
    ckv = _rmsnorm(z_read[:, OFF_CKV:OFF_CKV + KV_RANK], kvg_ref[...])
    kpair = z_read[:, OFF_KR:OFF_KR + 2 * QK_ROPE]
    krot = kpair * t1 + pltpu.roll(kpair, QK_ROPE, 1) * t2k
    pk_ref[0, rows, PK_K:PK_K + QK_WIDTH] = jnp.concatenate([ckv, krot], axis=1).astype(_BF16)
    pk_ref[0, rows, PK_V:PK_V + 2 * KV_RANK] = jnp.concatenate(
        [ckv, jnp.ones_like(ckv)], axis=1).astype(_BF16)
    if emit_cache:
        ckv_ref[0, 0, rows, :] = ckv
        kr_ref[row0 // SUB_TILE, 0] = kpair.T[0:QK_ROPE, :]
    qa = jnp.dot(cq, wq_ref[...], preferred_element_type=_F32)
    norm_rows(0)
    project(4)
    rope_q(qa, 0)
    rope_q(qa, 1)
    mixed0 = mix_head(0, vn0)
    mixed1 = mix_head(1, vn1)
    rope_q(qa, 2)
    rope_q(qa, 3)
    project(5)
    gate_a(0, mixed0)
    mixed2 = mix_head(2, vn2)
    mixed3 = mix_head(3, vn3)
    project(6)
    gate_a(1, mixed1)
    norm_rows(half)
    project(7)
    gate_a(2, mixed2)
    project(8)
    gate_a(3, mixed3)
    project(9)


def _front_body(x_ref, mod_ref, ng_ref, win_ref, ws_ref, bs_ref, gv_ref, qg_ref, wq_ref,
                kvg_ref, tab_ref, *rest, emit_cache, tiles_per_row):
    out_refs, (h_a, h_b, z_a, z_b) = rest[:-4], rest[-4:]
    t = pl.program_id(0)
    tile = x_ref.shape[1]
    assert tile == 2 * SUB_TILE
    n_pos = tab_ref.shape[0]

    @pl.when(t == 0)
    def _():
        h_b[...] = jnp.zeros(h_b.shape, _BF16)
        z_b[...] = jnp.zeros(z_b.shape, _F32)

    pos_base = (jnp.maximum(t - 1, 0) % tiles_per_row) * tile
    for sub, (h_write, h_read, z_write, z_read) in enumerate(
            ((h_a, h_b, z_a, z_b), (h_b, h_a, z_b, z_a))):
        row0 = sub * SUB_TILE
        pos = pl.multiple_of((pos_base + row0) % n_pos, SUB_TILE)
        tables = tab_ref[pl.ds(pos, SUB_TILE), :]
        _front_substep(x_ref, mod_ref, ng_ref, win_ref, ws_ref, bs_ref, gv_ref, qg_ref, wq_ref,
                       kvg_ref, tables, out_refs, row0, h_write, h_read, z_write, z_read,
                       emit_cache)


def _front_call(x, mod, mod_index, norm_g, win2, ws_bf, bs_t, g_v, q_norm_g, wq2, kv_norm_g,
                tables, emit_cache):
    batch, seq, _ = x.shape
    tile = 2 * SUB_TILE
    tiles_per_row = seq // tile
    n_tiles = batch * tiles_per_row
    assert seq % tile == 0 and tables.shape[0] % SUB_TILE == 0

    def in_tile(t):
        tt = jnp.minimum(t, n_tiles - 1)
        return tt // tiles_per_row, tt % tiles_per_row

    def out_tile(t):
        tt = jnp.maximum(t - 1, 0)
        return tt // tiles_per_row, tt % tiles_per_row

    full = lambda shape: pl.BlockSpec(shape, lambda t: (0,) * len(shape))
    in_specs = [
        pl.BlockSpec((1, tile, D_MODEL), lambda t: (*in_tile(t), 0)),
        pl.BlockSpec((SUBLANES, 3 * D_MODEL), lambda t: (mod_index(in_tile(t)[0]), 0)),
        full((1, D_MODEL)),
        full((D_MODEL, IN_WIDTH2)),
        full((A_HEADS, CHUNK, CHUNK)),
        full((CHUNK, A_HEADS)),
        full((1, A_WIDTH)),
        full((1, Q_RANK)),
        full((Q_RANK, MLA_HEADS * QK_WIDTH)),
        full((1, KV_RANK)),
        full(tables.shape),
    ]
    out_shape = [
        jax.ShapeDtypeStruct((batch, seq, PK_WIDTH), _BF16),
        jax.ShapeDtypeStruct((batch, seq, B_WIDTH), _F32),
    ]
    out_specs = [
        pl.BlockSpec((1, tile, PK_WIDTH), lambda t: (*out_tile(t), 0)),
        pl.BlockSpec((1, tile, B_WIDTH), lambda t: (*out_tile(t), 0)),
    ]
    if emit_cache:
        out_shape += [
            jax.ShapeDtypeStruct((batch, 1, seq, KV_RANK), _F32),
            jax.ShapeDtypeStruct((n_tiles * 2, 1, QK_ROPE, SUB_TILE), _F32),
        ]
        out_specs += [
            pl.BlockSpec((1, 1, tile, KV_RANK), lambda t: (out_tile(t)[0], 0, out_tile(t)[1], 0)),
            pl.BlockSpec((2, 1, QK_ROPE, SUB_TILE), lambda t: (jnp.maximum(t - 1, 0), 0, 0, 0)),
        ]
    return pl.pallas_call(
        functools.partial(_front_body, emit_cache=emit_cache, tiles_per_row=tiles_per_row),
        grid=(n_tiles + 1,),
        in_specs=in_specs,
        out_specs=out_specs,
        out_shape=out_shape,
        scratch_shapes=[pltpu.VMEM((SUB_TILE, D_MODEL), _BF16), pltpu.VMEM((SUB_TILE, D_MODEL), _BF16),
                        pltpu.VMEM((SUB_TILE, IN_WIDTH2), _F32),
                        pltpu.VMEM((SUB_TILE, IN_WIDTH2), _F32)],
        compiler_params=pltpu.CompilerParams(
            dimension_semantics=("arbitrary",),
            vmem_limit_bytes=VMEM_LIMIT_BYTES),
        name="front_ctx" if emit_cache else "front_lat",
    )(x, mod, norm_g, win2, ws_bf, bs_t, g_v, q_norm_g, wq2, kv_norm_g, tables)


def _scores(q, k_refs, row):
    return [lax.dot_general(q, k_ref[row], _NT_DIMS, preferred_element_type=_F32)
            for k_ref in k_refs]


def _softmax_values(scores, v_refs, row):
    m = functools.reduce(jnp.maximum, [jnp.max(s, axis=1, keepdims=True) for s in scores])
    acc = None
    for s, values in zip(scores, v_refs):
        p = jnp.exp2(s - m).astype(_BF16)
        part = jnp.dot(p, values[row], preferred_element_type=_F32)
        acc = part if acc is None else acc + part
    return acc[:, 0:KV_RANK] / acc[:, KV_RANK:2 * KV_RANK]


def _back_body(*refs, has_ctx):
    if has_ctx:
        (q_ref, klat_ref, vlat_ref, kctx_ref, a_ref, gb_ref, x_ref, mod_ref, wuv_ref,
         wo_ref, fg_ref, y_ref) = refs
        k_refs, v_refs = (kctx_ref, klat_ref), (None, vlat_ref)
    else:
        (q_ref, klat_ref, vlat_ref, a_ref, gb_ref, x_ref, mod_ref, wuv_ref, wo_ref, fg_ref,
         y_ref) = refs
        k_refs, v_refs = (klat_ref,), (vlat_ref,)
    n_rows, tq = q_ref.shape[0], q_ref.shape[1]

    groups = [
        (row, jnp.concatenate([q_ref[row, :, hd * QK_WIDTH:(hd + 1) * QK_WIDTH]
                               for hd in (2 * g, 2 * g + 1)], axis=0))
        for row in range(n_rows) for g in range(MLA_HEADS // 2)]
    scores = [(row, _scores(qg, k_refs, row)) for row, qg in groups]
    if has_ctx:
        ckv = kctx_ref[:, :, 0:KV_RANK]
        v_refs = (jnp.concatenate([ckv, jnp.ones_like(ckv)], axis=-1), vlat_ref)
    outs = [_softmax_values(s, v_refs, row) for row, s in scores]

    n_pairs = MLA_HEADS // 2
    attn_cols = []
    for pair in range(n_pairs):
        o2 = jnp.concatenate(
            [jnp.concatenate([outs[row * n_pairs + pair][0:tq, :],
                              outs[row * n_pairs + pair][tq:2 * tq, :]], axis=1)
             for row in range(n_rows)], axis=0)
        attn_cols.append(jnp.dot(o2.astype(_BF16), wuv_ref[pair], preferred_element_type=_F32))
    flat = lambda ref: ref[...].reshape(n_rows * tq, ref.shape[-1])
    attn = jnp.concatenate(attn_cols, axis=1) * flat(gb_ref)

    y = jnp.dot(flat(a_ref), wo_ref[0:A_WIDTH, :], preferred_element_type=_F32)
    y = y + jnp.dot(attn.astype(_BF16), wo_ref[A_WIDTH:A_WIDTH + B_WIDTH, :],
                    preferred_element_type=_F32)
    gate = mod_ref[0:1, 2 * D_MODEL:3 * D_MODEL]
    out = flat(x_ref) + gate * y
    y_ref[...] = _rmsnorm(out, fg_ref[...]).reshape(y_ref.shape)


def _back_call(packed, kctx, gb, x, mod, mod_index, wuv2, wo_bf, final_g, rows_per_step):
    batch, seq, _ = x.shape
    tq = min(TOKEN_TILE, seq)
    has_ctx = kctx is not None
    rows = rows_per_step
    assert batch % rows == 0 and (rows == 1 or seq == tq)
    full = lambda shape: pl.BlockSpec(shape, lambda b, i: (0,) * len(shape))
    keys = lambda arr: pl.BlockSpec((rows, arr.shape[1], arr.shape[2]), lambda b, i: (b, 0, 0))
    in_specs = [
        pl.BlockSpec((rows, tq, MLA_HEADS * QK_WIDTH),
                     lambda b, i: (b, i, PK_Q // (MLA_HEADS * QK_WIDTH))),
        pl.BlockSpec((rows, seq, QK_WIDTH), lambda b, i: (b, 0, PK_K // QK_WIDTH)),
        pl.BlockSpec((rows, seq, 2 * KV_RANK), lambda b, i: (b, 0, PK_V // (2 * KV_RANK))),
    ]
    args = [packed, packed, packed]
    if has_ctx:
        in_specs += [keys(kctx)]
        args += [kctx]
    in_specs += [
        pl.BlockSpec((rows, tq, A_WIDTH), lambda b, i: (b, i, PK_A // A_WIDTH)),
        pl.BlockSpec((rows, tq, B_WIDTH), lambda b, i: (b, i, 0)),
        pl.BlockSpec((rows, tq, D_MODEL), lambda b, i: (b, i, 0)),
        pl.BlockSpec((SUBLANES, 3 * D_MODEL), lambda b, i: (mod_index(b * rows), 0)),
        full((MLA_HEADS // 2, 2 * KV_RANK, 2 * V_HEAD)),
        full((D_MODEL, D_MODEL)),
        full((1, D_MODEL)),
    ]
    args += [packed, gb, x, mod, wuv2, wo_bf, final_g]
    return pl.pallas_call(
        functools.partial(_back_body, has_ctx=has_ctx),
        grid=(batch // rows, seq // tq),
        in_specs=in_specs,
        out_specs=pl.BlockSpec((rows, tq, D_MODEL), lambda b, i: (b, i, 0)),
        out_shape=jax.ShapeDtypeStruct((batch, seq, D_MODEL), _F32),
        compiler_params=pltpu.CompilerParams(
            dimension_semantics=("arbitrary", "arbitrary"),
            vmem_limit_bytes=VMEM_LIMIT_BYTES),
        name="back_lat" if has_ctx else "back_ctx",
    )(*args)


def _rope_tables(n_tokens, rotate):
    if rotate:
        rows = n_tokens // GRID_W
        inv = ROPE_THETA ** (-jnp.arange(AXIS_PAIRS, dtype=_F32) / AXIS_PAIRS)
        ang_r = jnp.arange(rows, dtype=_F32)[:, None] * inv
        ang_c = jnp.arange(GRID_W, dtype=_F32)[:, None] * inv

        def grid_table(fn):
            tr = jnp.broadcast_to(fn(ang_r)[:, None, :], (rows, GRID_W, AXIS_PAIRS))
            tc = jnp.broadcast_to(fn(ang_c)[None, :, :], (rows, GRID_W, AXIS_PAIRS))
            half = jnp.concatenate([tr, tc], axis=-1).reshape(n_tokens, 2 * AXIS_PAIRS)
            return jnp.concatenate([half, half], axis=-1)

        cos, sin = grid_table(jnp.cos), grid_table(jnp.sin)
    else:
        cos, sin = jnp.ones((n_tokens, QK_ROPE), _F32), jnp.zeros((n_tokens, QK_ROPE), _F32)
    return jnp.concatenate([cos, sin], axis=-1)


def kernel(x_prompt, x_sample, cache_ckv, cache_krope, c, c_ctx, norm_g, w_ada, b_ada, w_in, w_s,
           b_s, g_v, q_norm_g, w_uq, kv_norm_g, w_ukv, w_o, final_g):
    depth = norm_g.shape[0]
    assert depth == 1 and w_in.shape[2] == IN_WIDTH
    dec_batch = x_sample.shape[0]
    xp, xs = x_prompt, x_sample
    new_ckv, new_kr = [], []
    for l in range(depth):
        cond = jnp.concatenate(
            [c, c_ctx[None, :], jnp.zeros((16 - dec_batch - 1, D_MODEL), _F32)], axis=0)
        mod = _mod_call(cond, w_ada[l], b_ada[l])

        win2 = _prep_call(jnp.transpose(w_in[l]))
        wq2, wuv2 = _fold_call(w_uq[l], w_ukv[l])
        wo_bf = w_o[l].astype(_BF16)
        ws_bf = w_s[l].astype(_BF16)
        bs_t = b_s[l].T
        gv_row = g_v[l].reshape(1, A_WIDTH)
        shared = (norm_g[l].reshape(1, D_MODEL), win2, ws_bf, bs_t, gv_row,
                  q_norm_g[l].reshape(1, Q_RANK), wq2, kv_norm_g[l].reshape(1, KV_RANK))
        fg = final_g.reshape(1, D_MODEL)

        ctx_index = lambda b: dec_batch
        ctx_batch, ctx_seq, _ = xp.shape
        pack = 2 * SUB_TILE // ctx_seq
        pk_c, gb_c, ckv_c, kr_c = _front_call(
            xp.reshape(ctx_batch // pack, pack * ctx_seq, D_MODEL), mod, ctx_index, *shared,
            _rope_tables(ctx_seq, False), True)
        unpack = lambda arr: arr.reshape(ctx_batch, ctx_seq, arr.shape[-1])
        xp = _back_call(unpack(pk_c), None, unpack(gb_c), xp, mod, ctx_index, wuv2, wo_bf, fg,
                        rows_per_step=pack)
        new_ckv.append(ckv_c.reshape(ctx_batch, 1, ctx_seq, KV_RANK))
        assert ctx_seq == SUB_TILE
        new_kr.append(jnp.swapaxes(kr_c, 2, 3))

        lat_index = lambda b: b
        pk_s, gb_s = _front_call(
            xs, mod, lat_index, *shared, _rope_tables(xs.shape[1], True), False)
        cache_k = jnp.concatenate(
            [cache_ckv[:, l], jnp.zeros(cache_krope[:, l].shape, _F32), cache_krope[:, l]],
            axis=-1).astype(_BF16)
        xs = _back_call(pk_s, cache_k, gb_s, xs, mod, lat_index, wuv2, wo_bf, fg, rows_per_step=1)
    return (xp, xs, jnp.concatenate(new_ckv, axis=1), jnp.concatenate(new_kr, axis=1))
```

```python
import functools
import math

import jax
import jax.numpy as jnp
from jax import lax
from jax.experimental import pallas as pl
from jax.experimental.pallas import tpu as pltpu

D_MODEL = 1024
GRID_W = 64
EPS = 1e-6
A_HEADS = 4
A_HEAD_DIM = 128
A_WIDTH = A_HEADS * A_HEAD_DIM
CHUNK = 128
MLA_HEADS = 4
QK_NOPE = 128
QK_ROPE = 64
V_HEAD = 128
B_WIDTH = MLA_HEADS * V_HEAD
Q_RANK = 256
KV_RANK = 128
AXIS_PAIRS = QK_ROPE // 4
ROPE_THETA = 10000.0
ATTN_SCALE = 1.0 / math.sqrt(QK_NOPE + QK_ROPE)
LOG2_E = 1.4426950408889634

QK_WIDTH = KV_RANK + 2 * QK_ROPE
OFF_U = 0
OFF_V = OFF_U + A_WIDTH
OFF_GA = OFF_V + A_WIDTH
OFF_CQ = OFF_GA + A_WIDTH
OFF_CKV = OFF_CQ + Q_RANK
OFF_KR = OFF_CKV + KV_RANK
OFF_GB = OFF_KR + 2 * QK_ROPE
IN_WIDTH = OFF_GB - QK_ROPE + B_WIDTH
IN_WIDTH2 = OFF_GB + B_WIDTH

PK_A = 0
PK_K = PK_A + A_WIDTH
PK_V = PK_K + QK_WIDTH
PK_Q = PK_V + 2 * KV_RANK
PK_WIDTH = PK_Q + MLA_HEADS * QK_WIDTH

LANES = 128
SUBLANES = 8
SUB_TILE = 256
TOKEN_TILE = 512
PROJ_BLOCK = 256
VMEM_LIMIT_BYTES = 56 * 1024 * 1024

_F32 = jnp.float32
_BF16 = jnp.bfloat16
_NT_DIMS = (((1,), (1,)), ((), ()))


def _silu(x):
    hx = 0.5 * x
    return hx + hx * jnp.tanh(hx)


def _gelu_tanh(x):
    return x * (0.5 * (1.0 + jnp.tanh(0.7978845608028654 * (x + 0.044715 * (x * x * x)))))


def _rmsnorm(x, g):
    ms = jnp.mean(x * x, axis=-1, keepdims=True)
    return (x * lax.rsqrt(ms + EPS)) * g


def _mod_body(cond_ref, wa_ref, wb_ref, b_ref, o_ref):
    s = _silu(cond_ref[...]).astype(_BF16)
    halves = [jnp.dot(s, w_ref[...].astype(_BF16), preferred_element_type=_F32)
              for w_ref in (wa_ref, wb_ref)]
    mod = jnp.concatenate(halves, axis=1) + b_ref[...]
    rows, cols = mod.shape
    o_ref[...] = jnp.broadcast_to(mod[:, None, :], (rows, SUBLANES, cols)).reshape(
        rows * SUBLANES, cols)


def _fold_body(wq_ref, wkv_ref, q_ref, uv_ref):
    half, quarter = LANES // 2, LANES // 4
    assert QK_NOPE == LANES and QK_ROPE == half and V_HEAD == LANES
    lane = lax.broadcasted_iota(jnp.int32, (Q_RANK, LANES), 1)
    low = lane < half

    def cols(block):
        return wq_ref[:, block * LANES:(block + 1) * LANES]

    for hd in range(MLA_HEADS):
        start = hd * (QK_NOPE + QK_ROPE)
        b0, odd = divmod(start, LANES)
        if odd == 0:
            nope = cols(b0)
            rope_blk = cols(b0 + 1)
            rope_lo = rope_blk
            rope_hi = pltpu.roll(rope_blk, half, 1)
        else:
            assert odd == half
            nope = jnp.where(low, pltpu.roll(cols(b0), half, 1), pltpu.roll(cols(b0 + 1), half, 1))
            rope_blk = cols(b0 + 1)
            rope_lo = pltpu.roll(rope_blk, half, 1)
            rope_hi = rope_blk
        partner = jnp.where(lane < half + quarter,
                            -pltpu.roll(rope_hi, LANES - quarter, 1),
                            pltpu.roll(rope_hi, quarter, 1))
        w_uk = wkv_ref[:, hd * (QK_NOPE + V_HEAD):hd * (QK_NOPE + V_HEAD) + QK_NOPE]
        absorbed = lax.dot_general(nope, w_uk, _NT_DIMS, precision=lax.Precision.HIGHEST,
                                   preferred_element_type=_F32)
        base = hd * QK_WIDTH
        q_ref[:, base:base + KV_RANK] = absorbed.astype(_BF16)
        q_ref[:, base + KV_RANK:base + QK_WIDTH] = jnp.where(low, rope_lo, partner).astype(_BF16)

    uv_ref[...] = jnp.zeros(uv_ref.shape, _BF16)
    for hd in range(MLA_HEADS):
        w_uv = wkv_ref[:, hd * (QK_NOPE + V_HEAD) + QK_NOPE:(hd + 1) * (QK_NOPE + V_HEAD)]
        pair, pos = divmod(hd, 2)
        uv_ref[pair, pos * KV_RANK:(pos + 1) * KV_RANK, pos * V_HEAD:(pos + 1) * V_HEAD] = (
            w_uv.astype(_BF16))


_KR_BLOCK = OFF_KR // LANES


_PREP_SPLIT = IN_WIDTH2 // 2


def _prep_body(step, w_ref, o_ref):
    half = QK_ROPE // 2
    blocks_per_step = _PREP_SPLIT // LANES
    base = step * _PREP_SPLIT
    for j in range(step * blocks_per_step, (step + 1) * blocks_per_step):
        if j < _KR_BLOCK:
            blk = w_ref[j * LANES - base:(j + 1) * LANES - base, :]
        elif j == _KR_BLOCK:
            kr = w_ref[OFF_KR - base:OFF_KR - base + QK_ROPE, :]
            partner = jnp.concatenate([-kr[half:, :], kr[:half, :]], axis=0)
            blk = jnp.concatenate([kr, partner], axis=0)
        else:
            lo = j * LANES - QK_ROPE - base
            blk = w_ref[lo:lo + LANES, :]
        o_ref[:, j * LANES - base:(j + 1) * LANES - base] = blk.T.astype(_BF16)


def _weights_body(cond_ref, wa_ref, wb_ref, b_ref, win_ref, wq_ref, wkv_ref,
                  mod_ref, win2_ref, q_ref, uv_ref):
    j = pl.program_id(0)
    _mod_body(cond_ref, wa_ref, wb_ref, b_ref, mod_ref)
    n_prep = IN_WIDTH2 // _PREP_SPLIT
    for step in range(n_prep):
        pl.when(j == step)(functools.partial(_prep_body, step, win_ref, win2_ref))
    pl.when(j == n_prep)(functools.partial(_fold_body, wq_ref, wkv_ref, q_ref, uv_ref))


def _weights_call(cond, w_ada, b_ada, w_in_t, w_uq, w_ukv):
    rows = cond.shape[0]
    n_out = w_ada.shape[1]
    col_block = D_MODEL
    n_steps = n_out // col_block
    n_prep = IN_WIDTH2 // _PREP_SPLIT
    assert _KR_BLOCK * LANES == OFF_KR and OFF_KR >= _PREP_SPLIT and n_steps == n_prep + 1
    prep_step = lambda j: jnp.minimum(j, n_prep - 1)
    full = lambda shape: pl.BlockSpec(shape, lambda j: (0,) * len(shape))
    uv_shape = (MLA_HEADS // 2, 2 * KV_RANK, 2 * V_HEAD)
    return pl.pallas_call(
        _weights_body,
        grid=(n_steps,),
        in_specs=[
            full((rows, D_MODEL)),
            pl.BlockSpec((D_MODEL, col_block // 2), lambda j: (0, 2 * j)),
            pl.BlockSpec((D_MODEL, col_block // 2), lambda j: (0, 2 * j + 1)),
            pl.BlockSpec((1, col_block), lambda j: (0, j)),
            pl.BlockSpec((_PREP_SPLIT, D_MODEL), lambda j: (prep_step(j), 0)),
            full(w_uq.shape),
            full(w_ukv.shape),
        ],
        out_specs=[
            pl.BlockSpec((rows * SUBLANES, col_block), lambda j: (0, j)),
            pl.BlockSpec((D_MODEL, _PREP_SPLIT), lambda j: (0, prep_step(j))),
            full((Q_RANK, MLA_HEADS * QK_WIDTH)),
            full(uv_shape),
        ],
        out_shape=[
            jax.ShapeDtypeStruct((rows * SUBLANES, n_out), _F32),
            jax.ShapeDtypeStruct((D_MODEL, IN_WIDTH2), _BF16),
            jax.ShapeDtypeStruct((Q_RANK, MLA_HEADS * QK_WIDTH), _BF16),
            jax.ShapeDtypeStruct(uv_shape, _BF16),
        ],
        compiler_params=pltpu.CompilerParams(
            dimension_semantics=("arbitrary",), vmem_limit_bytes=VMEM_LIMIT_BYTES),
        name="weights",
    )(cond, w_ada, w_ada, b_ada.reshape(1, n_out), w_in_t, w_uq, w_ukv)


def _front_substep(x_ref, mod_ref, ng_ref, win_ref, ws_ref, bs_ref, gv_ref, qg_ref, wq_ref,
                   kvg_ref, tables, out_refs, row0, h_write, h_read, z_write, z_read, emit_cache):
    if emit_cache:
        pk_ref, gb_ref, ckv_ref, kr_ref = out_refs
    else:
        pk_ref, gb_ref = out_refs
    first_half = lax.broadcasted_iota(jnp.int32, tables.shape, 1) < QK_ROPE
    t1 = jnp.where(first_half, tables, 0.0)
    t2k = jnp.where(first_half, pltpu.roll(tables, QK_ROPE, 1), 0.0)
    t2q = jnp.where(first_half, t2k, 1.0)
    rows = slice(row0, row0 + SUB_TILE)
    half = SUB_TILE // 2
    assert IN_WIDTH2 == 10 * PROJ_BLOCK

    def project(j):
        lo = j * PROJ_BLOCK
        z_write[:, lo:lo + PROJ_BLOCK] = jnp.dot(h_read[...], win_ref[:, lo:lo + PROJ_BLOCK],
                                                 preferred_element_type=_F32)

    shift = mod_ref[0:1, 0:D_MODEL]
    gain = ng_ref[...] * (1.0 + mod_ref[0:1, D_MODEL:2 * D_MODEL])

    def norm_rows(lo):
        x = x_ref[0, row0 + lo:row0 + lo + half, :]
        ms = jnp.mean(x * x, axis=-1, keepdims=True)
        h_write[lo:lo + half, :] = ((x * lax.rsqrt(ms + EPS)) * gain + shift).astype(_BF16)

    def vn_head(hd):
        lo = hd * A_HEAD_DIM
        v = _gelu_tanh(z_read[:, OFF_V + lo:OFF_V + lo + A_HEAD_DIM])
        return _rmsnorm(v, gv_ref[:, lo:lo + A_HEAD_DIM]).astype(_BF16)

    def mix_head(hd, vn):
        bias = bs_ref[:, hd:hd + 1]
        parts = [jnp.dot(ws_ref[hd], vn[c * CHUNK:(c + 1) * CHUNK, :],
                         preferred_element_type=_F32) + bias for c in range(SUB_TILE // CHUNK)]
        return jnp.concatenate(parts, axis=0)

    def gate_a(hd, mixed):
        lo = hd * A_HEAD_DIM
        u = _gelu_tanh(z_read[:, OFF_U + lo:OFF_U + lo + A_HEAD_DIM])
        g = _silu(z_read[:, OFF_GA + lo:OFF_GA + lo + A_HEAD_DIM])
        pk_ref[0, rows, PK_A + lo:PK_A + lo + A_HEAD_DIM] = (u * mixed * g).astype(_BF16)

    def gate_b(lo, width):
        gb_ref[0, rows, lo:lo + width] = _silu(z_read[:, OFF_GB + lo:OFF_GB + lo + width])

    def rope_q(qa, hd):
        base = hd * QK_WIDTH
        pair = qa[:, base + KV_RANK:base + QK_WIDTH]
        rope = pair * t1 + pltpu.roll(pair, QK_ROPE, 1) * t2q
        pk_ref[0, rows, PK_Q + base:PK_Q + base + QK_WIDTH] = jnp.concatenate(
            [qa[:, base:base + KV_RANK], rope], axis=1).astype(_BF16)

    project(0)
    cq = _rmsnorm(z_read[:, OFF_CQ:OFF_CQ + Q_RANK],
                  qg_ref[...] * (ATTN_SCALE * LOG2_E)).astype(_BF16)
    vn0 = vn_head(0)
    project(1)
    vn1 = vn_head(1)
    gate_b(0, B_WIDTH // 2)
    project(2)
    vn2 = vn_head(2)
    gate_b(B_WIDTH // 2, B_WIDTH // 2)
    project(3)
    vn3 = vn_head(3)
    ckv = _rmsnorm(z_read[:, OFF_CKV:OFF_CKV + KV_RANK], kvg_ref[...])
    kpair = z_read[:, OFF_KR:OFF_KR + 2 * QK_ROPE]
    krot = kpair * t1 + pltpu.roll(kpair, QK_ROPE, 1) * t2k
    pk_ref[0, rows, PK_K:PK_K + QK_WIDTH] = jnp.concatenate([ckv, krot], axis=1).astype(_BF16)
    pk_ref[0, rows, PK_V:PK_V + 2 * KV_RANK] = jnp.concatenate(
        [ckv, jnp.ones_like(ckv)], axis=1).astype(_BF16)
    if emit_cache:
        ckv_ref[0, 0, rows, :] = ckv
        kr_ref[row0 // SUB_TILE, 0] = kpair.T[0:QK_ROPE, :]
    qa = jnp.dot(cq, wq_ref[...], preferred_element_type=_F32)
    norm_rows(0)
    project(4)
    rope_q(qa, 0)
    rope_q(qa, 1)
    mixed0 = mix_head(0, vn0)
    mixed1 = mix_head(1, vn1)
    rope_q(qa, 2)
    rope_q(qa, 3)
    project(5)
    gate_a(0, mixed0)
    mixed2 = mix_head(2, vn2)
    mixed3 = mix_head(3, vn3)
    project(6)
    gate_a(1, mixed1)
    norm_rows(half)
    project(7)
    gate_a(2, mixed2)
    project(8)
    gate_a(3, mixed3)
    project(9)


def _front_body(x_ref, mod_ref, ng_ref, win_ref, ws_ref, bs_ref, gv_ref, qg_ref, wq_ref,
                kvg_ref, tab_ref, *rest, emit_cache, tiles_per_row):
    out_refs, (h_a, h_b, z_a, z_b) = rest[:-4], rest[-4:]
    t = pl.program_id(0)
    tile = x_ref.shape[1]
    assert tile == 2 * SUB_TILE
    n_pos = tab_ref.shape[0]

    @pl.when(t == 0)
    def _():
        h_b[...] = jnp.zeros(h_b.shape, _BF16)
        z_b[...] = jnp.zeros(z_b.shape, _F32)

    pos_base = (jnp.maximum(t - 1, 0) % tiles_per_row) * tile
    for sub, (h_write, h_read, z_write, z_read) in enumerate(
            ((h_a, h_b, z_a, z_b), (h_b, h_a, z_b, z_a))):
        row0 = sub * SUB_TILE
        pos = pl.multiple_of((pos_base + row0) % n_pos, SUB_TILE)
        tables = tab_ref[pl.ds(pos, SUB_TILE), :]
        _front_substep(x_ref, mod_ref, ng_ref, win_ref, ws_ref, bs_ref, gv_ref, qg_ref, wq_ref,
                       kvg_ref, tables, out_refs, row0, h_write, h_read, z_write, z_read,
                       emit_cache)


def _front_call(x, mod, mod_index, norm_g, win2, ws_bf, bs_t, g_v, q_norm_g, wq2, kv_norm_g,
                tables, emit_cache):
    batch, seq, _ = x.shape
    tile = 2 * SUB_TILE
    tiles_per_row = seq // tile
    n_tiles = batch * tiles_per_row
    assert seq % tile == 0 and tables.shape[0] % SUB_TILE == 0

    def in_tile(t):
        tt = jnp.minimum(t, n_tiles - 1)
        return tt // tiles_per_row, tt % tiles_per_row

    def out_tile(t):
        tt = jnp.maximum(t - 1, 0)
        return tt // tiles_per_row, tt % tiles_per_row

    full = lambda shape: pl.BlockSpec(shape, lambda t: (0,) * len(shape))
    in_specs = [
        pl.BlockSpec((1, tile, D_MODEL), lambda t: (*in_tile(t), 0)),
        pl.BlockSpec((SUBLANES, 3 * D_MODEL), lambda t: (mod_index(in_tile(t)[0]), 0)),
        full((1, D_MODEL)),
        full((D_MODEL, IN_WIDTH2)),
        full((A_HEADS, CHUNK, CHUNK)),
        full((CHUNK, A_HEADS)),
        full((1, A_WIDTH)),
        full((1, Q_RANK)),
        full((Q_RANK, MLA_HEADS * QK_WIDTH)),
        full((1, KV_RANK)),
        full(tables.shape),
    ]
    out_shape = [
        jax.ShapeDtypeStruct((batch, seq, PK_WIDTH), _BF16),
        jax.ShapeDtypeStruct((batch, seq, B_WIDTH), _F32),
    ]
    out_specs = [
        pl.BlockSpec((1, tile, PK_WIDTH), lambda t: (*out_tile(t), 0)),
        pl.BlockSpec((1, tile, B_WIDTH), lambda t: (*out_tile(t), 0)),
    ]
    if emit_cache:
        out_shape += [
            jax.ShapeDtypeStruct((batch, 1, seq, KV_RANK), _F32),
            jax.ShapeDtypeStruct((n_tiles * 2, 1, QK_ROPE, SUB_TILE), _F32),
        ]
        out_specs += [
            pl.BlockSpec((1, 1, tile, KV_RANK), lambda t: (out_tile(t)[0], 0, out_tile(t)[1], 0)),
            pl.BlockSpec((2, 1, QK_ROPE, SUB_TILE), lambda t: (jnp.maximum(t - 1, 0), 0, 0, 0)),
        ]
    return pl.pallas_call(
        functools.partial(_front_body, emit_cache=emit_cache, tiles_per_row=tiles_per_row),
        grid=(n_tiles + 1,),
        in_specs=in_specs,
        out_specs=out_specs,
        out_shape=out_shape,
        scratch_shapes=[pltpu.VMEM((SUB_TILE, D_MODEL), _BF16), pltpu.VMEM((SUB_TILE, D_MODEL), _BF16),
                        pltpu.VMEM((SUB_TILE, IN_WIDTH2), _F32),
                        pltpu.VMEM((SUB_TILE, IN_WIDTH2), _F32)],
        compiler_params=pltpu.CompilerParams(
            dimension_semantics=("arbitrary",),
            vmem_limit_bytes=VMEM_LIMIT_BYTES),
        name="front_ctx" if emit_cache else "front_lat",
    )(x, mod, norm_g, win2, ws_bf, bs_t, g_v, q_norm_g, wq2, kv_norm_g, tables)


def _scores(q, k_refs, row):
    return [lax.dot_general(q, k_ref[row], _NT_DIMS, preferred_element_type=_F32)
            for k_ref in k_refs]


def _softmax_values(scores, v_refs, row):
    m = functools.reduce(jnp.maximum, [jnp.max(s, axis=1, keepdims=True) for s in scores])
    acc = None
    for s, values in zip(scores, v_refs):
        p = jnp.exp2(s - m).astype(_BF16)
        part = jnp.dot(p, values[row], preferred_element_type=_F32)
        acc = part if acc is None else acc + part
    return acc[:, 0:KV_RANK] / acc[:, KV_RANK:2 * KV_RANK]


def _back_body(*refs, has_ctx):
    if has_ctx:
        (q_ref, klat_ref, vlat_ref, kctx_ref, a_ref, gb_ref, x_ref, mod_ref, wuv_ref,
         wo_ref, fg_ref, y_ref) = refs
        k_refs, v_refs = (kctx_ref, klat_ref), (None, vlat_ref)
    else:
        (q_ref, klat_ref, vlat_ref, a_ref, gb_ref, x_ref, mod_ref, wuv_ref, wo_ref, fg_ref,
         y_ref) = refs
        k_refs, v_refs = (klat_ref,), (vlat_ref,)
    n_rows, tq = q_ref.shape[0], q_ref.shape[1]

    groups = [
        (row, jnp.concatenate([q_ref[row, :, hd * QK_WIDTH:(hd + 1) * QK_WIDTH]
                               for hd in (2 * g, 2 * g + 1)], axis=0))
        for row in range(n_rows) for g in range(MLA_HEADS // 2)]
    scores = [(row, _scores(qg, k_refs, row)) for row, qg in groups]
    if has_ctx:
        ckv = kctx_ref[:, :, 0:KV_RANK]
        v_refs = (jnp.concatenate([ckv, jnp.ones_like(ckv)], axis=-1), vlat_ref)
    outs = [_softmax_values(s, v_refs, row) for row, s in scores]

    n_pairs = MLA_HEADS // 2
    attn_cols = []
    for pair in range(n_pairs):
        o2 = jnp.concatenate(
            [jnp.concatenate([outs[row * n_pairs + pair][0:tq, :],
                              outs[row * n_pairs + pair][tq:2 * tq, :]], axis=1)
             for row in range(n_rows)], axis=0)
        attn_cols.append(jnp.dot(o2.astype(_BF16), wuv_ref[pair], preferred_element_type=_F32))
    flat = lambda ref: ref[...].reshape(n_rows * tq, ref.shape[-1])
    attn = jnp.concatenate(attn_cols, axis=1) * flat(gb_ref)

    y = jnp.dot(flat(a_ref), wo_ref[0:A_WIDTH, :], preferred_element_type=_F32)
    y = y + jnp.dot(attn.astype(_BF16), wo_ref[A_WIDTH:A_WIDTH + B_WIDTH, :],
                    preferred_element_type=_F32)
    gate = mod_ref[0:1, 2 * D_MODEL:3 * D_MODEL]
    out = flat(x_ref) + gate * y
    y_ref[...] = _rmsnorm(out, fg_ref[...]).reshape(y_ref.shape)


def _back_call(packed, kctx, gb, x, mod, mod_index, wuv2, wo_bf, final_g, rows_per_step):
    batch, seq, _ = x.shape
    tq = min(TOKEN_TILE, seq)
    has_ctx = kctx is not None
    rows = rows_per_step
    assert batch % rows == 0 and (rows == 1 or seq == tq)
    full = lambda shape: pl.BlockSpec(shape, lambda b, i: (0,) * len(shape))
    keys = lambda arr: pl.BlockSpec((rows, arr.shape[1], arr.shape[2]), lambda b, i: (b, 0, 0))
    in_specs = [
        pl.BlockSpec((rows, tq, MLA_HEADS * QK_WIDTH),
                     lambda b, i: (b, i, PK_Q // (MLA_HEADS * QK_WIDTH))),
        pl.BlockSpec((rows, seq, QK_WIDTH), lambda b, i: (b, 0, PK_K // QK_WIDTH)),
        pl.BlockSpec((rows, seq, 2 * KV_RANK), lambda b, i: (b, 0, PK_V // (2 * KV_RANK))),
    ]
    args = [packed, packed, packed]
    if has_ctx:
        in_specs += [keys(kctx)]
        args += [kctx]
    in_specs += [
        pl.BlockSpec((rows, tq, A_WIDTH), lambda b, i: (b, i, PK_A // A_WIDTH)),
        pl.BlockSpec((rows, tq, B_WIDTH), lambda b, i: (b, i, 0)),
        pl.BlockSpec((rows, tq, D_MODEL), lambda b, i: (b, i, 0)),
        pl.BlockSpec((SUBLANES, 3 * D_MODEL), lambda b, i: (mod_index(b * rows), 0)),
        full((MLA_HEADS // 2, 2 * KV_RANK, 2 * V_HEAD)),
        full((D_MODEL, D_MODEL)),
        full((1, D_MODEL)),
    ]
    args += [packed, gb, x, mod, wuv2, wo_bf, final_g]
    return pl.pallas_call(
        functools.partial(_back_body, has_ctx=has_ctx),
        grid=(batch // rows, seq // tq),
        in_specs=in_specs,
        out_specs=pl.BlockSpec((rows, tq, D_MODEL), lambda b, i: (b, i, 0)),
        out_shape=jax.ShapeDtypeStruct((batch, seq, D_MODEL), _F32),
        compiler_params=pltpu.CompilerParams(
            dimension_semantics=("arbitrary", "arbitrary"),
            vmem_limit_bytes=VMEM_LIMIT_BYTES),
        name="back_lat" if has_ctx else "back_ctx",
    )(*args)


def _rope_tables(n_tokens, rotate):
    if rotate:
        rows = n_tokens // GRID_W
        inv = ROPE_THETA ** (-jnp.arange(AXIS_PAIRS, dtype=_F32) / AXIS_PAIRS)
        ang_r = jnp.arange(rows, dtype=_F32)[:, None] * inv
        ang_c = jnp.arange(GRID_W, dtype=_F32)[:, None] * inv

        def grid_table(fn):
            tr = jnp.broadcast_to(fn(ang_r)[:, None, :], (rows, GRID_W, AXIS_PAIRS))
            tc = jnp.broadcast_to(fn(ang_c)[None, :, :], (rows, GRID_W, AXIS_PAIRS))
            half = jnp.concatenate([tr, tc], axis=-1).reshape(n_tokens, 2 * AXIS_PAIRS)
            return jnp.concatenate([half, half], axis=-1)

        cos, sin = grid_table(jnp.cos), grid_table(jnp.sin)
    else:
        cos, sin = jnp.ones((n_tokens, QK_ROPE), _F32), jnp.zeros((n_tokens, QK_ROPE), _F32)
    return jnp.concatenate([cos, sin], axis=-1)


def kernel(x_prompt, x_sample, cache_ckv, cache_krope, c, c_ctx, norm_g, w_ada, b_ada, w_in, w_s,
           b_s, g_v, q_norm_g, w_uq, kv_norm_g, w_ukv, w_o, final_g):
    depth = norm_g.shape[0]
    assert depth == 1 and w_in.shape[2] == IN_WIDTH
    dec_batch = x_sample.shape[0]
    xp, xs = x_prompt, x_sample
    new_ckv, new_kr = [], []
    for l in range(depth):
        cond = jnp.concatenate(
            [c, c_ctx[None, :], jnp.zeros((16 - dec_batch - 1, D_MODEL), _F32)], axis=0)
        mod, win2, wq2, wuv2 = _weights_call(cond, w_ada[l], b_ada[l], jnp.transpose(w_in[l]),
                                             w_uq[l], w_ukv[l])
        wo_bf = w_o[l].astype(_BF16)
        ws_bf = w_s[l].astype(_BF16)
        bs_t = b_s[l].T
        gv_row = g_v[l].reshape(1, A_WIDTH)
        shared = (norm_g[l].reshape(1, D_MODEL), win2, ws_bf, bs_t, gv_row,
                  q_norm_g[l].reshape(1, Q_RANK), wq2, kv_norm_g[l].reshape(1, KV_RANK))
        fg = final_g.reshape(1, D_MODEL)

        ctx_index = lambda b: dec_batch
        ctx_batch, ctx_seq, _ = xp.shape
        pack = 2 * SUB_TILE // ctx_seq
        pk_c, gb_c, ckv_c, kr_c = _front_call(
            xp.reshape(ctx_batch // pack, pack * ctx_seq, D_MODEL), mod, ctx_index, *shared,
            _rope_tables(ctx_seq, False), True)
        unpack = lambda arr: arr.reshape(ctx_batch, ctx_seq, arr.shape[-1])
        xp = _back_call(unpack(pk_c), None, unpack(gb_c), xp, mod, ctx_index, wuv2, wo_bf, fg,
                        rows_per_step=pack)
        new_ckv.append(ckv_c.reshape(ctx_batch, 1, ctx_seq, KV_RANK))
        assert ctx_seq == SUB_TILE
        new_kr.append(jnp.swapaxes(kr_c, 2, 3))

        lat_index = lambda b: b
        pk_s, gb_s = _front_call(
            xs, mod, lat_index, *shared, _rope_tables(xs.shape[1], True), False)
        cache_k = jnp.concatenate(
            [cache_ckv[:, l], jnp.zeros(cache_krope[:, l].shape, _F32), cache_krope[:, l]],
            axis=-1).astype(_BF16)
        xs = _back_call(pk_s, cache_k, gb_s, xs, mod, lat_index, wuv2, wo_bf, fg, rows_per_step=1)
    return (xp, xs, jnp.concatenate(new_ckv, axis=1), jnp.concatenate(new_kr, axis=1))
```

```python
import functools
import math

import jax
import jax.numpy as jnp
from jax import lax
from jax.experimental import pallas as pl
from jax.experimental.pallas import tpu as pltpu

D_MODEL = 1024
GRID_W = 64
EPS = 1e-6
A_HEADS = 4
A_HEAD_DIM = 128
A_WIDTH = A_HEADS * A_HEAD_DIM
CHUNK = 128
MLA_HEADS = 4
QK_NOPE = 128
QK_ROPE = 64
V_HEAD = 128
B_WIDTH = MLA_HEADS * V_HEAD
Q_RANK = 256
KV_RANK = 128
AXIS_PAIRS = QK_ROPE // 4
ROPE_THETA = 10000.0
ATTN_SCALE = 1.0 / math.sqrt(QK_NOPE + QK_ROPE)
LOG2_E = 1.4426950408889634

QK_WIDTH = KV_RANK + 2 * QK_ROPE
OFF_U = 0
OFF_V = OFF_U + A_WIDTH
OFF_GA = OFF_V + A_WIDTH
OFF_CQ = OFF_GA + A_WIDTH
OFF_CKV = OFF_CQ + Q_RANK
OFF_KR = OFF_CKV + KV_RANK
OFF_GB = OFF_KR + 2 * QK_ROPE
IN_WIDTH = OFF_GB - QK_ROPE + B_WIDTH
IN_WIDTH2 = OFF_GB + B_WIDTH

PK_A = 0
PK_K = PK_A + A_WIDTH
PK_V = PK_K + QK_WIDTH
PK_Q = PK_V + 2 * KV_RANK
PK_WIDTH = PK_Q + MLA_HEADS * QK_WIDTH

LANES = 128
SUBLANES = 8
SUB_TILE = 256
TOKEN_TILE = 512
PROJ_BLOCK = 256
VMEM_LIMIT_BYTES = 56 * 1024 * 1024

_F32 = jnp.float32
_BF16 = jnp.bfloat16
_NT_DIMS = (((1,), (1,)), ((), ()))


def _silu(x):
    hx = 0.5 * x
    return hx + hx * jnp.tanh(hx)


def _gelu_tanh(x):
    return x * (0.5 * (1.0 + jnp.tanh(0.7978845608028654 * (x + 0.044715 * (x * x * x)))))


def _rmsnorm(x, g):
    ms = jnp.mean(x * x, axis=-1, keepdims=True)
    return (x * lax.rsqrt(ms + EPS)) * g


def _mod_body(c_ref, cctx_ref, wa_ref, wb_ref, b_ref, o_ref):
    ctx_rows = jnp.broadcast_to(cctx_ref[...], (SUBLANES, cctx_ref.shape[1]))
    s = _silu(jnp.concatenate([c_ref[...], ctx_rows], axis=0)).astype(_BF16)
    halves = [jnp.dot(s, w_ref[...].astype(_BF16), preferred_element_type=_F32)
              for w_ref in (wa_ref, wb_ref)]
    mod = jnp.concatenate(halves, axis=1) + b_ref[...]
    rows, cols = mod.shape
    o_ref[...] = jnp.broadcast_to(mod[:, None, :], (rows, SUBLANES, cols)).reshape(
        rows * SUBLANES, cols)


def _fold_body(wq_ref, wkv_ref, q_ref, uv_ref):
    half, quarter = LANES // 2, LANES // 4
    assert QK_NOPE == LANES and QK_ROPE == half and V_HEAD == LANES
    lane = lax.broadcasted_iota(jnp.int32, (Q_RANK, LANES), 1)
    low = lane < half

    def cols(block):
        return wq_ref[:, block * LANES:(block + 1) * LANES]

    for hd in range(MLA_HEADS):
        start = hd * (QK_NOPE + QK_ROPE)
        b0, odd = divmod(start, LANES)
        if odd == 0:
            nope = cols(b0)
            rope_blk = cols(b0 + 1)
            rope_lo = rope_blk
            rope_hi = pltpu.roll(rope_blk, half, 1)
        else:
            assert odd == half
            nope = jnp.where(low, pltpu.roll(cols(b0), half, 1), pltpu.roll(cols(b0 + 1), half, 1))
            rope_blk = cols(b0 + 1)
            rope_lo = pltpu.roll(rope_blk, half, 1)
            rope_hi = rope_blk
        partner = jnp.where(lane < half + quarter,
                            -pltpu.roll(rope_hi, LANES - quarter, 1),
                            pltpu.roll(rope_hi, quarter, 1))
        w_uk = wkv_ref[:, hd * (QK_NOPE + V_HEAD):hd * (QK_NOPE + V_HEAD) + QK_NOPE]
        absorbed = lax.dot_general(nope, w_uk, _NT_DIMS, precision=lax.Precision.HIGHEST,
                                   preferred_element_type=_F32)
        base = hd * QK_WIDTH
        q_ref[:, base:base + KV_RANK] = absorbed.astype(_BF16)
        q_ref[:, base + KV_RANK:base + QK_WIDTH] = jnp.where(low, rope_lo, partner).astype(_BF16)

    uv_ref[...] = jnp.zeros(uv_ref.shape, _BF16)
    for hd in range(MLA_HEADS):
        w_uv = wkv_ref[:, hd * (QK_NOPE + V_HEAD) + QK_NOPE:(hd + 1) * (QK_NOPE + V_HEAD)]
        pair, pos = divmod(hd, 2)
        uv_ref[pair, pos * KV_RANK:(pos + 1) * KV_RANK, pos * V_HEAD:(pos + 1) * V_HEAD] = (
            w_uv.astype(_BF16))


_KR_BLOCK = OFF_KR // LANES


_PREP_SPLIT = IN_WIDTH2 // 2


def _prep_body(step, w_ref, o_ref):
    half = QK_ROPE // 2
    blocks_per_step = _PREP_SPLIT // LANES
    base = step * _PREP_SPLIT
    for j in range(step * blocks_per_step, (step + 1) * blocks_per_step):
        if j < _KR_BLOCK:
            blk = w_ref[j * LANES - base:(j + 1) * LANES - base, :]
        elif j == _KR_BLOCK:
            kr = w_ref[OFF_KR - base:OFF_KR - base + QK_ROPE, :]
            partner = jnp.concatenate([-kr[half:, :], kr[:half, :]], axis=0)
            blk = jnp.concatenate([kr, partner], axis=0)
        else:
            lo = j * LANES - QK_ROPE - base
            blk = w_ref[lo:lo + LANES, :]
        o_ref[:, j * LANES - base:(j + 1) * LANES - base] = blk.T.astype(_BF16)


def _weights_body(c_ref, cctx_ref, wa_ref, wb_ref, b_ref, win_ref, wq_ref, wkv_ref,
                  mod_ref, win2_ref, q_ref, uv_ref):
    j = pl.program_id(0)
    _mod_body(c_ref, cctx_ref, wa_ref, wb_ref, b_ref, mod_ref)
    n_prep = IN_WIDTH2 // _PREP_SPLIT
    for step in range(n_prep):
        pl.when(j == step)(functools.partial(_prep_body, step, win_ref, win2_ref))
    pl.when(j == n_prep)(functools.partial(_fold_body, wq_ref, wkv_ref, q_ref, uv_ref))


def _weights_call(c, c_ctx, w_ada, b_ada, w_in_t, w_uq, w_ukv):
    assert c.shape[0] % SUBLANES == 0
    rows = c.shape[0] + SUBLANES
    n_out = w_ada.shape[1]
    col_block = D_MODEL
    n_steps = n_out // col_block
    n_prep = IN_WIDTH2 // _PREP_SPLIT
    assert _KR_BLOCK * LANES == OFF_KR and OFF_KR >= _PREP_SPLIT and n_steps == n_prep + 1
    prep_step = lambda j: jnp.minimum(j, n_prep - 1)
    full = lambda shape: pl.BlockSpec(shape, lambda j: (0,) * len(shape))
    uv_shape = (MLA_HEADS // 2, 2 * KV_RANK, 2 * V_HEAD)
    return pl.pallas_call(
        _weights_body,
        grid=(n_steps,),
        in_specs=[
            full(c.shape),
            full((1, D_MODEL)),
            pl.BlockSpec((D_MODEL, col_block // 2), lambda j: (0, 2 * j)),
            pl.BlockSpec((D_MODEL, col_block // 2), lambda j: (0, 2 * j + 1)),
            pl.BlockSpec((1, col_block), lambda j: (0, j)),
            pl.BlockSpec((_PREP_SPLIT, D_MODEL), lambda j: (prep_step(j), 0)),
            full(w_uq.shape),
            full(w_ukv.shape),
        ],
        out_specs=[
            pl.BlockSpec((rows * SUBLANES, col_block), lambda j: (0, j)),
            pl.BlockSpec((D_MODEL, _PREP_SPLIT), lambda j: (0, prep_step(j))),
            full((Q_RANK, MLA_HEADS * QK_WIDTH)),
            full(uv_shape),
        ],
        out_shape=[
            jax.ShapeDtypeStruct((rows * SUBLANES, n_out), _F32),
            jax.ShapeDtypeStruct((D_MODEL, IN_WIDTH2), _BF16),
            jax.ShapeDtypeStruct((Q_RANK, MLA_HEADS * QK_WIDTH), _BF16),
            jax.ShapeDtypeStruct(uv_shape, _BF16),
        ],
        compiler_params=pltpu.CompilerParams(
            dimension_semantics=("arbitrary",), vmem_limit_bytes=VMEM_LIMIT_BYTES),
        name="weights",
    )(c, c_ctx.reshape(1, D_MODEL), w_ada, w_ada, b_ada.reshape(1, n_out), w_in_t, w_uq, w_ukv)


def _front_substep(x_ref, mod_ref, ng_ref, win_ref, ws_ref, bs_ref, gv_ref, qg_ref, wq_ref,
                   kvg_ref, tables, out_refs, row0, h_write, h_read, z_write, z_read, emit_cache):
    if emit_cache:
        pk_ref, gb_ref, ckv_ref, kr_ref = out_refs
    else:
        pk_ref, gb_ref = out_refs
    first_half = lax.broadcasted_iota(jnp.int32, tables.shape, 1) < QK_ROPE
    t1 = jnp.where(first_half, tables, 0.0)
    t2k = jnp.where(first_half, pltpu.roll(tables, QK_ROPE, 1), 0.0)
    t2q = jnp.where(first_half, t2k, 1.0)
    rows = slice(row0, row0 + SUB_TILE)
    half = SUB_TILE // 2
    assert IN_WIDTH2 == 10 * PROJ_BLOCK

    def project(j):
        lo = j * PROJ_BLOCK
        z_write[:, lo:lo + PROJ_BLOCK] = jnp.dot(h_read[...], win_ref[:, lo:lo + PROJ_BLOCK],
                                                 preferred_element_type=_F32)

    shift = mod_ref[0:1, 0:D_MODEL]
    gain = ng_ref[...] * (1.0 + mod_ref[0:1, D_MODEL:2 * D_MODEL])

    def norm_rows(lo):
        x = x_ref[0, row0 + lo:row0 + lo + half, :]
        ms = jnp.mean(x * x, axis=-1, keepdims=True)
        h_write[lo:lo + half, :] = ((x * lax.rsqrt(ms + EPS)) * gain + shift).astype(_BF16)

    def vn_head(hd):
        lo = hd * A_HEAD_DIM
        v = _gelu_tanh(z_read[:, OFF_V + lo:OFF_V + lo + A_HEAD_DIM])
        return _rmsnorm(v, gv_ref[:, lo:lo + A_HEAD_DIM]).astype(_BF16)

    def mix_head(hd, vn):
        bias = bs_ref[:, hd:hd + 1]
        parts = [jnp.dot(ws_ref[hd].astype(_BF16), vn[c * CHUNK:(c + 1) * CHUNK, :],
                         preferred_element_type=_F32) + bias for c in range(SUB_TILE // CHUNK)]
        return jnp.concatenate(parts, axis=0)

    def gate_a(hd, mixed):
        lo = hd * A_HEAD_DIM
        u = _gelu_tanh(z_read[:, OFF_U + lo:OFF_U + lo + A_HEAD_DIM])
        g = _silu(z_read[:, OFF_GA + lo:OFF_GA + lo + A_HEAD_DIM])
        pk_ref[0, rows, PK_A + lo:PK_A + lo + A_HEAD_DIM] = (u * mixed * g).astype(_BF16)

    def gate_b(lo, width):
        gb_ref[0, rows, lo:lo + width] = _silu(z_read[:, OFF_GB + lo:OFF_GB + lo + width])

    def rope_q(qa, hd):
        base = hd * QK_WIDTH
        pair = qa[:, base + KV_RANK:base + QK_WIDTH]
        rope = pair * t1 + pltpu.roll(pair, QK_ROPE, 1) * t2q
        pk_ref[0, rows, PK_Q + base:PK_Q + base + QK_WIDTH] = jnp.concatenate(
            [qa[:, base:base + KV_RANK], rope], axis=1).astype(_BF16)

    project(0)
    cq = _rmsnorm(z_read[:, OFF_CQ:OFF_CQ + Q_RANK],
                  qg_ref[...] * (ATTN_SCALE * LOG2_E)).astype(_BF16)
    vn0 = vn_head(0)
    project(1)
    vn1 = vn_head(1)
    gate_b(0, B_WIDTH // 2)
    project(2)
    vn2 = vn_head(2)
    gate_b(B_WIDTH // 2, B_WIDTH // 2)
    project(3)
    vn3 = vn_head(3)
    ckv = _rmsnorm(z_read[:, OFF_CKV:OFF_CKV + KV_RANK], kvg_ref[...])
    kpair = z_read[:, OFF_KR:OFF_KR + 2 * QK_ROPE]
    krot = kpair * t1 + pltpu.roll(kpair, QK_ROPE, 1) * t2k
    pk_ref[0, rows, PK_K:PK_K + QK_WIDTH] = jnp.concatenate([ckv, krot], axis=1).astype(_BF16)
    pk_ref[0, rows, PK_V:PK_V + 2 * KV_RANK] = jnp.concatenate(
        [ckv, jnp.ones_like(ckv)], axis=1).astype(_BF16)
    if emit_cache:
        ckv_ref[0, 0, rows, :] = ckv
        kr_ref[row0 // SUB_TILE, 0] = kpair.T[0:QK_ROPE, :]
    qa = jnp.dot(cq, wq_ref[...], preferred_element_type=_F32)
    norm_rows(0)
    project(4)
    rope_q(qa, 0)
    rope_q(qa, 1)
    mixed0 = mix_head(0, vn0)
    mixed1 = mix_head(1, vn1)
    rope_q(qa, 2)
    rope_q(qa, 3)
    project(5)
    gate_a(0, mixed0)
    mixed2 = mix_head(2, vn2)
    mixed3 = mix_head(3, vn3)
    project(6)
    gate_a(1, mixed1)
    norm_rows(half)
    project(7)
    gate_a(2, mixed2)
    project(8)
    gate_a(3, mixed3)
    project(9)


def _front_body(x_ref, mod_ref, ng_ref, win_ref, ws_ref, bs_ref, gv_ref, qg_ref, wq_ref,
                kvg_ref, tab_ref, *rest, emit_cache, tiles_per_row):
    out_refs, (h_a, h_b, z_a, z_b) = rest[:-4], rest[-4:]
    t = pl.program_id(0)
    tile = x_ref.shape[1]
    assert tile == 2 * SUB_TILE
    n_pos = tab_ref.shape[0]

    @pl.when(t == 0)
    def _():
        h_b[...] = jnp.zeros(h_b.shape, _BF16)
        z_b[...] = jnp.zeros(z_b.shape, _F32)

    pos_base = (jnp.maximum(t - 1, 0) % tiles_per_row) * tile
    for sub, (h_write, h_read, z_write, z_read) in enumerate(
            ((h_a, h_b, z_a, z_b), (h_b, h_a, z_b, z_a))):
        row0 = sub * SUB_TILE
        pos = pl.multiple_of((pos_base + row0) % n_pos, SUB_TILE)
        tables = tab_ref[pl.ds(pos, SUB_TILE), :]
        _front_substep(x_ref, mod_ref, ng_ref, win_ref, ws_ref, bs_ref, gv_ref, qg_ref, wq_ref,
                       kvg_ref, tables, out_refs, row0, h_write, h_read, z_write, z_read,
                       emit_cache)


def _front_call(x, mod, mod_index, norm_g, win2, w_s, bs_t, g_v, q_norm_g, wq2, kv_norm_g,
                tables, emit_cache):
    batch, seq, _ = x.shape
    tile = 2 * SUB_TILE
    tiles_per_row = seq // tile
    n_tiles = batch * tiles_per_row
    assert seq % tile == 0 and tables.shape[0] % SUB_TILE == 0

    def in_tile(t):
        tt = jnp.minimum(t, n_tiles - 1)
        return tt // tiles_per_row, tt % tiles_per_row

    def out_tile(t):
        tt = jnp.maximum(t - 1, 0)
        return tt // tiles_per_row, tt % tiles_per_row

    full = lambda shape: pl.BlockSpec(shape, lambda t: (0,) * len(shape))
    in_specs = [
        pl.BlockSpec((1, tile, D_MODEL), lambda t: (*in_tile(t), 0)),
        pl.BlockSpec((SUBLANES, 3 * D_MODEL), lambda t: (mod_index(in_tile(t)[0]), 0)),
        full((1, D_MODEL)),
        full((D_MODEL, IN_WIDTH2)),
        full((A_HEADS, CHUNK, CHUNK)),
        full((CHUNK, A_HEADS)),
        full((1, A_WIDTH)),
        full((1, Q_RANK)),
        full((Q_RANK, MLA_HEADS * QK_WIDTH)),
        full((1, KV_RANK)),
        full(tables.shape),
    ]
    out_shape = [
        jax.ShapeDtypeStruct((batch, seq, PK_WIDTH), _BF16),
        jax.ShapeDtypeStruct((batch, seq, B_WIDTH), _F32),
    ]
    out_specs = [
        pl.BlockSpec((1, tile, PK_WIDTH), lambda t: (*out_tile(t), 0)),
        pl.BlockSpec((1, tile, B_WIDTH), lambda t: (*out_tile(t), 0)),
    ]
    if emit_cache:
        out_shape += [
            jax.ShapeDtypeStruct((batch, 1, seq, KV_RANK), _F32),
            jax.ShapeDtypeStruct((n_tiles * 2, 1, QK_ROPE, SUB_TILE), _F32),
        ]
        out_specs += [
            pl.BlockSpec((1, 1, tile, KV_RANK), lambda t: (out_tile(t)[0], 0, out_tile(t)[1], 0)),
            pl.BlockSpec((2, 1, QK_ROPE, SUB_TILE), lambda t: (jnp.maximum(t - 1, 0), 0, 0, 0)),
        ]
    return pl.pallas_call(
        functools.partial(_front_body, emit_cache=emit_cache, tiles_per_row=tiles_per_row),
        grid=(n_tiles + 1,),
        in_specs=in_specs,
        out_specs=out_specs,
        out_shape=out_shape,
        scratch_shapes=[pltpu.VMEM((SUB_TILE, D_MODEL), _BF16), pltpu.VMEM((SUB_TILE, D_MODEL), _BF16),
                        pltpu.VMEM((SUB_TILE, IN_WIDTH2), _F32),
                        pltpu.VMEM((SUB_TILE, IN_WIDTH2), _F32)],
        compiler_params=pltpu.CompilerParams(
            dimension_semantics=("arbitrary",),
            vmem_limit_bytes=VMEM_LIMIT_BYTES),
        name="front_ctx" if emit_cache else "front_lat",
    )(x, mod, norm_g, win2, w_s, bs_t, g_v, q_norm_g, wq2, kv_norm_g, tables)


def _scores(q, k_refs, row):
    return [lax.dot_general(q, k_ref[row], _NT_DIMS, preferred_element_type=_F32)
            for k_ref in k_refs]


def _softmax_values(scores, v_refs, row):
    m = functools.reduce(jnp.maximum, [jnp.max(s, axis=1, keepdims=True) for s in scores])
    acc = None
    for s, values in zip(scores, v_refs):
        p = jnp.exp2(s - m).astype(_BF16)
        part = jnp.dot(p, values[row], preferred_element_type=_F32)
        acc = part if acc is None else acc + part
    return acc[:, 0:KV_RANK] / acc[:, KV_RANK:2 * KV_RANK]


def _back_body(*refs, has_ctx):
    if has_ctx:
        (q_ref, klat_ref, vlat_ref, kctx_ref, a_ref, gb_ref, x_ref, mod_ref, wuv_ref,
         wo_ref, fg_ref, y_ref) = refs
        k_refs, v_refs = (kctx_ref, klat_ref), (None, vlat_ref)
    else:
        (q_ref, klat_ref, vlat_ref, a_ref, gb_ref, x_ref, mod_ref, wuv_ref, wo_ref, fg_ref,
         y_ref) = refs
        k_refs, v_refs = (klat_ref,), (vlat_ref,)
    n_rows, tq = q_ref.shape[0], q_ref.shape[1]

    groups = [
        (row, jnp.concatenate([q_ref[row, :, hd * QK_WIDTH:(hd + 1) * QK_WIDTH]
                               for hd in (2 * g, 2 * g + 1)], axis=0))
        for row in range(n_rows) for g in range(MLA_HEADS // 2)]
    scores = [(row, _scores(qg, k_refs, row)) for row, qg in groups]
    if has_ctx:
        ckv = kctx_ref[:, :, 0:KV_RANK]
        v_refs = (jnp.concatenate([ckv, jnp.ones_like(ckv)], axis=-1), vlat_ref)
    outs = [_softmax_values(s, v_refs, row) for row, s in scores]

    n_pairs = MLA_HEADS // 2
    attn_cols = []
    for pair in range(n_pairs):
        o2 = jnp.concatenate(
            [jnp.concatenate([outs[row * n_pairs + pair][0:tq, :],
                              outs[row * n_pairs + pair][tq:2 * tq, :]], axis=1)
             for row in range(n_rows)], axis=0)
        attn_cols.append(jnp.dot(o2.astype(_BF16), wuv_ref[pair], preferred_element_type=_F32))
    flat = lambda ref: ref[...].reshape(n_rows * tq, ref.shape[-1])
    attn = jnp.concatenate(attn_cols, axis=1) * flat(gb_ref)

    y = jnp.dot(flat(a_ref), wo_ref[0:A_WIDTH, :], preferred_element_type=_F32)
    y = y + jnp.dot(attn.astype(_BF16), wo_ref[A_WIDTH:A_WIDTH + B_WIDTH, :],
                    preferred_element_type=_F32)
    gate = mod_ref[0:1, 2 * D_MODEL:3 * D_MODEL]
    out = flat(x_ref) + gate * y
    y_ref[...] = _rmsnorm(out, fg_ref[...]).reshape(y_ref.shape)


def _back_call(packed, kctx, gb, x, mod, mod_index, wuv2, wo_bf, final_g, rows_per_step):
    batch, seq, _ = x.shape
    tq = min(TOKEN_TILE, seq)
    has_ctx = kctx is not None
    rows = rows_per_step
    assert batch % rows == 0 and (rows == 1 or seq == tq)
    full = lambda shape: pl.BlockSpec(shape, lambda b, i: (0,) * len(shape))
    keys = lambda arr: pl.BlockSpec((rows, arr.shape[1], arr.shape[2]), lambda b, i: (b, 0, 0))
    in_specs = [
        pl.BlockSpec((rows, tq, MLA_HEADS * QK_WIDTH),
                     lambda b, i: (b, i, PK_Q // (MLA_HEADS * QK_WIDTH))),
        pl.BlockSpec((rows, seq, QK_WIDTH), lambda b, i: (b, 0, PK_K // QK_WIDTH)),
        pl.BlockSpec((rows, seq, 2 * KV_RANK), lambda b, i: (b, 0, PK_V // (2 * KV_RANK))),
    ]
    args = [packed, packed, packed]
    if has_ctx:
        in_specs += [keys(kctx)]
        args += [kctx]
    in_specs += [
        pl.BlockSpec((rows, tq, A_WIDTH), lambda b, i: (b, i, PK_A // A_WIDTH)),
        pl.BlockSpec((rows, tq, B_WIDTH), lambda b, i: (b, i, 0)),
        pl.BlockSpec((rows, tq, D_MODEL), lambda b, i: (b, i, 0)),
        pl.BlockSpec((SUBLANES, 3 * D_MODEL), lambda b, i: (mod_index(b * rows), 0)),
        full((MLA_HEADS // 2, 2 * KV_RANK, 2 * V_HEAD)),
        full((D_MODEL, D_MODEL)),
        full((1, D_MODEL)),
    ]
    args += [packed, gb, x, mod, wuv2, wo_bf, final_g]
    return pl.pallas_call(
        functools.partial(_back_body, has_ctx=has_ctx),
        grid=(batch // rows, seq // tq),
        in_specs=in_specs,
        out_specs=pl.BlockSpec((rows, tq, D_MODEL), lambda b, i: (b, i, 0)),
        out_shape=jax.ShapeDtypeStruct((batch, seq, D_MODEL), _F32),
        compiler_params=pltpu.CompilerParams(
            dimension_semantics=("arbitrary", "arbitrary"),
            vmem_limit_bytes=VMEM_LIMIT_BYTES),
        name="back_lat" if has_ctx else "back_ctx",
    )(*args)


def _rope_tables(n_tokens, rotate):
    if rotate:
        rows = n_tokens // GRID_W
        inv = ROPE_THETA ** (-jnp.arange(AXIS_PAIRS, dtype=_F32) / AXIS_PAIRS)
        ang_r = jnp.arange(rows, dtype=_F32)[:, None] * inv
        ang_c = jnp.arange(GRID_W, dtype=_F32)[:, None] * inv

        def grid_table(fn):
            tr = jnp.broadcast_to(fn(ang_r)[:, None, :], (rows, GRID_W, AXIS_PAIRS))
            tc = jnp.broadcast_to(fn(ang_c)[None, :, :], (rows, GRID_W, AXIS_PAIRS))
            half = jnp.concatenate([tr, tc], axis=-1).reshape(n_tokens, 2 * AXIS_PAIRS)
            return jnp.concatenate([half, half], axis=-1)

        cos, sin = grid_table(jnp.cos), grid_table(jnp.sin)
    else:
        cos, sin = jnp.ones((n_tokens, QK_ROPE), _F32), jnp.zeros((n_tokens, QK_ROPE), _F32)
    return jnp.concatenate([cos, sin], axis=-1)


def kernel(x_prompt, x_sample, cache_ckv, cache_krope, c, c_ctx, norm_g, w_ada, b_ada, w_in, w_s,
           b_s, g_v, q_norm_g, w_uq, kv_norm_g, w_ukv, w_o, final_g):
    depth = norm_g.shape[0]
    assert depth == 1 and w_in.shape[2] == IN_WIDTH
    dec_batch = x_sample.shape[0]
    xp, xs = x_prompt, x_sample
    new_ckv, new_kr = [], []
    for l in range(depth):
        mod, win2, wq2, wuv2 = _weights_call(c, c_ctx, w_ada[l], b_ada[l], jnp.transpose(w_in[l]),
                                             w_uq[l], w_ukv[l])
        wo_bf = w_o[l].astype(_BF16)
        bs_t = b_s[l].T
        gv_row = g_v[l].reshape(1, A_WIDTH)
        shared = (norm_g[l].reshape(1, D_MODEL), win2, w_s[l], bs_t, gv_row,
                  q_norm_g[l].reshape(1, Q_RANK), wq2, kv_norm_g[l].reshape(1, KV_RANK))
        fg = final_g.reshape(1, D_MODEL)

        ctx_index = lambda b: dec_batch
        ctx_batch, ctx_seq, _ = xp.shape
        pack = 2 * SUB_TILE // ctx_seq
        pk_c, gb_c, ckv_c, kr_c = _front_call(
            xp.reshape(ctx_batch // pack, pack * ctx_seq, D_MODEL), mod, ctx_index, *shared,
            _rope_tables(ctx_seq, False), True)
        unpack = lambda arr: arr.reshape(ctx_batch, ctx_seq, arr.shape[-1])
        xp = _back_call(unpack(pk_c), None, unpack(gb_c), xp, mod, ctx_index, wuv2, wo_bf, fg,
                        rows_per_step=pack)
        new_ckv.append(ckv_c.reshape(ctx_batch, 1, ctx_seq, KV_RANK))
        assert ctx_seq == SUB_TILE
        new_kr.append(jnp.swapaxes(kr_c, 2, 3))

        lat_index = lambda b: b
        pk_s, gb_s = _front_call(
            xs, mod, lat_index, *shared, _rope_tables(xs.shape[1], True), False)
        cache_k = jnp.concatenate(
            [cache_ckv[:, l], jnp.zeros(cache_krope[:, l].shape, _F32), cache_krope[:, l]],
            axis=-1).astype(_BF16)
        xs = _back_call(pk_s, cache_k, gb_s, xs, mod, lat_index, wuv2, wo_bf, fg, rows_per_step=1)
    return (xp, xs, jnp.concatenate(new_ckv, axis=1), jnp.concatenate(new_kr, axis=1))
```

```python
import functools
import math

import jax
import jax.numpy as jnp
from jax import lax
from jax.experimental import pallas as pl
from jax.experimental.pallas import tpu as pltpu

D_MODEL = 1024
GRID_W = 64
EPS = 1e-6
A_HEADS = 4
A_HEAD_DIM = 128
A_WIDTH = A_HEADS * A_HEAD_DIM
CHUNK = 128
MLA_HEADS = 4
QK_NOPE = 128
QK_ROPE = 64
V_HEAD = 128
B_WIDTH = MLA_HEADS * V_HEAD
Q_RANK = 256
KV_RANK = 128
AXIS_PAIRS = QK_ROPE // 4
ROPE_THETA = 10000.0
ATTN_SCALE = 1.0 / math.sqrt(QK_NOPE + QK_ROPE)
LOG2_E = 1.4426950408889634

QK_WIDTH = KV_RANK + 2 * QK_ROPE
OFF_U = 0
OFF_V = OFF_U + A_WIDTH
OFF_GA = OFF_V + A_WIDTH
OFF_CQ = OFF_GA + A_WIDTH
OFF_CKV = OFF_CQ + Q_RANK
OFF_KR = OFF_CKV + KV_RANK
OFF_GB = OFF_KR + 2 * QK_ROPE
IN_WIDTH = OFF_GB - QK_ROPE + B_WIDTH
IN_WIDTH2 = OFF_GB + B_WIDTH

PK_A = 0
PK_K = PK_A + A_WIDTH
PK_V = PK_K + QK_WIDTH
PK_Q = PK_V + 2 * KV_RANK
PK_WIDTH = PK_Q + MLA_HEADS * QK_WIDTH

LANES = 128
SUBLANES = 8
SUB_TILE = 256
TOKEN_TILE = 512
PROJ_BLOCK = 256
VMEM_LIMIT_BYTES = 56 * 1024 * 1024

_F32 = jnp.float32
_BF16 = jnp.bfloat16
_NT_DIMS = (((1,), (1,)), ((), ()))


def _silu(x):
    hx = 0.5 * x
    return hx + hx * jnp.tanh(hx)


def _gelu_tanh(x):
    return x * (0.5 * (1.0 + jnp.tanh(0.7978845608028654 * (x + 0.044715 * (x * x * x)))))


def _rmsnorm(x, g):
    ms = jnp.mean(x * x, axis=-1, keepdims=True)
    return (x * lax.rsqrt(ms + EPS)) * g


def _mod_body(c_ref, cctx_ref, wa_ref, wb_ref, b_ref, o_ref):
    ctx_rows = jnp.broadcast_to(cctx_ref[...], (SUBLANES, cctx_ref.shape[1]))
    s = _silu(jnp.concatenate([c_ref[...], ctx_rows], axis=0)).astype(_BF16)
    halves = [jnp.dot(s, w_ref[...].astype(_BF16), preferred_element_type=_F32)
              for w_ref in (wa_ref, wb_ref)]
    mod = jnp.concatenate(halves, axis=1) + b_ref[...]
    rows, cols = mod.shape
    o_ref[...] = jnp.broadcast_to(mod[:, None, :], (rows, SUBLANES, cols)).reshape(
        rows * SUBLANES, cols)


def _fold_body(wq_ref, wkv_ref, q_ref, uv_ref):
    half, quarter = LANES // 2, LANES // 4
    assert QK_NOPE == LANES and QK_ROPE == half and V_HEAD == LANES
    lane = lax.broadcasted_iota(jnp.int32, (Q_RANK, LANES), 1)
    low = lane < half

    def cols(block):
        return wq_ref[:, block * LANES:(block + 1) * LANES]

    for hd in range(MLA_HEADS):
        start = hd * (QK_NOPE + QK_ROPE)
        b0, odd = divmod(start, LANES)
        if odd == 0:
            nope = cols(b0)
            rope_blk = cols(b0 + 1)
            rope_lo = rope_blk
            rope_hi = pltpu.roll(rope_blk, half, 1)
        else:
            assert odd == half
            nope = jnp.where(low, pltpu.roll(cols(b0), half, 1), pltpu.roll(cols(b0 + 1), half, 1))
            rope_blk = cols(b0 + 1)
            rope_lo = pltpu.roll(rope_blk, half, 1)
            rope_hi = rope_blk
        partner = jnp.where(lane < half + quarter,
                            -pltpu.roll(rope_hi, LANES - quarter, 1),
                            pltpu.roll(rope_hi, quarter, 1))
        w_uk = wkv_ref[:, hd * (QK_NOPE + V_HEAD):hd * (QK_NOPE + V_HEAD) + QK_NOPE]
        absorbed = lax.dot_general(nope, w_uk, _NT_DIMS, precision=lax.Precision.HIGHEST,
                                   preferred_element_type=_F32)
        base = hd * QK_WIDTH
        q_ref[:, base:base + KV_RANK] = absorbed.astype(_BF16)
        q_ref[:, base + KV_RANK:base + QK_WIDTH] = jnp.where(low, rope_lo, partner).astype(_BF16)

    uv_ref[...] = jnp.zeros(uv_ref.shape, _BF16)
    for hd in range(MLA_HEADS):
        w_uv = wkv_ref[:, hd * (QK_NOPE + V_HEAD) + QK_NOPE:(hd + 1) * (QK_NOPE + V_HEAD)]
        pair, pos = divmod(hd, 2)
        uv_ref[pair, pos * KV_RANK:(pos + 1) * KV_RANK, pos * V_HEAD:(pos + 1) * V_HEAD] = (
            w_uv.astype(_BF16))


_KR_BLOCK = OFF_KR // LANES


_PREP_SPLIT = IN_WIDTH2 // 2


def _prep_body(step, w_ref, o_ref):
    half = QK_ROPE // 2
    blocks_per_step = _PREP_SPLIT // LANES
    base = step * _PREP_SPLIT
    for j in range(step * blocks_per_step, (step + 1) * blocks_per_step):
        if j < _KR_BLOCK:
            blk = w_ref[j * LANES - base:(j + 1) * LANES - base, :]
        elif j == _KR_BLOCK:
            kr = w_ref[OFF_KR - base:OFF_KR - base + QK_ROPE, :]
            partner = jnp.concatenate([-kr[half:, :], kr[:half, :]], axis=0)
            blk = jnp.concatenate([kr, partner], axis=0)
        else:
            lo = j * LANES - QK_ROPE - base
            blk = w_ref[lo:lo + LANES, :]
        o_ref[:, j * LANES - base:(j + 1) * LANES - base] = blk.T.astype(_BF16)


def _weights_body(c_ref, cctx_ref, wa_ref, wb_ref, b_ref, win_ref, wq_ref, wkv_ref,
                  mod_ref, win2_ref, q_ref, uv_ref):
    j = pl.program_id(0)
    _mod_body(c_ref, cctx_ref, wa_ref, wb_ref, b_ref, mod_ref)
    n_prep = IN_WIDTH2 // _PREP_SPLIT
    for step in range(n_prep):
        pl.when(j == step)(functools.partial(_prep_body, step, win_ref, win2_ref))
    pl.when(j == n_prep)(functools.partial(_fold_body, wq_ref, wkv_ref, q_ref, uv_ref))


def _weights_call(c, c_ctx, w_ada, b_ada, w_in_t, w_uq, w_ukv):
    assert c.shape[0] % SUBLANES == 0
    rows = c.shape[0] + SUBLANES
    n_out = w_ada.shape[1]
    col_block = D_MODEL
    n_steps = n_out // col_block
    n_prep = IN_WIDTH2 // _PREP_SPLIT
    assert _KR_BLOCK * LANES == OFF_KR and OFF_KR >= _PREP_SPLIT and n_steps == n_prep + 1
    prep_step = lambda j: jnp.minimum(j, n_prep - 1)
    full = lambda shape: pl.BlockSpec(shape, lambda j: (0,) * len(shape))
    uv_shape = (MLA_HEADS // 2, 2 * KV_RANK, 2 * V_HEAD)
    return pl.pallas_call(
        _weights_body,
        grid=(n_steps,),
        in_specs=[
            full(c.shape),
            full((1, D_MODEL)),
            pl.BlockSpec((D_MODEL, col_block // 2), lambda j: (0, 2 * j)),
            pl.BlockSpec((D_MODEL, col_block // 2), lambda j: (0, 2 * j + 1)),
            pl.BlockSpec((1, col_block), lambda j: (0, j)),
            pl.BlockSpec((_PREP_SPLIT, D_MODEL), lambda j: (prep_step(j), 0)),
            full(w_uq.shape),
            full(w_ukv.shape),
        ],
        out_specs=[
            pl.BlockSpec((rows * SUBLANES, col_block), lambda j: (0, j)),
            pl.BlockSpec((D_MODEL, _PREP_SPLIT), lambda j: (0, prep_step(j))),
            full((Q_RANK, MLA_HEADS * QK_WIDTH)),
            full(uv_shape),
        ],
        out_shape=[
            jax.ShapeDtypeStruct((rows * SUBLANES, n_out), _F32),
            jax.ShapeDtypeStruct((D_MODEL, IN_WIDTH2), _BF16),
            jax.ShapeDtypeStruct((Q_RANK, MLA_HEADS * QK_WIDTH), _BF16),
            jax.ShapeDtypeStruct(uv_shape, _BF16),
        ],
        compiler_params=pltpu.CompilerParams(
            dimension_semantics=("arbitrary",), vmem_limit_bytes=VMEM_LIMIT_BYTES),
        name="weights",
    )(c, c_ctx.reshape(1, D_MODEL), w_ada, w_ada, b_ada.reshape(1, n_out), w_in_t, w_uq, w_ukv)


def _front_substep(x_ref, mod_ref, ng_ref, win_ref, ws_ref, bs_ref, gv_ref, qg_ref, wq_ref,
                   kvg_ref, tables, out_refs, row0, h_write, h_read, z_write, z_read, emit_cache):
    if emit_cache:
        pk_ref, gb_ref, ckv_ref, kr_ref = out_refs
    else:
        pk_ref, gb_ref = out_refs
    first_half = lax.broadcasted_iota(jnp.int32, tables.shape, 1) < QK_ROPE
    t1 = jnp.where(first_half, tables, 0.0)
    t2k = jnp.where(first_half, pltpu.roll(tables, QK_ROPE, 1), 0.0)
    t2q = jnp.where(first_half, t2k, 1.0)
    rows = slice(row0, row0 + SUB_TILE)
    half = SUB_TILE // 2
    assert IN_WIDTH2 == 10 * PROJ_BLOCK

    def project(j):
        lo = j * PROJ_BLOCK
        z_write[:, lo:lo + PROJ_BLOCK] = jnp.dot(h_read[...], win_ref[:, lo:lo + PROJ_BLOCK],
                                                 preferred_element_type=_F32)

    shift = mod_ref[0:1, 0:D_MODEL]
    gain = ng_ref[...] * (1.0 + mod_ref[0:1, D_MODEL:2 * D_MODEL])

    def norm_rows(lo):
        x = x_ref[0, row0 + lo:row0 + lo + half, :]
        ms = jnp.mean(x * x, axis=-1, keepdims=True)
        h_write[lo:lo + half, :] = ((x * lax.rsqrt(ms + EPS)) * gain + shift).astype(_BF16)

    def vn_head(hd):
        lo = hd * A_HEAD_DIM
        v = _gelu_tanh(z_read[:, OFF_V + lo:OFF_V + lo + A_HEAD_DIM])
        return _rmsnorm(v, gv_ref[:, lo:lo + A_HEAD_DIM]).astype(_BF16)

    def mix_head(hd, vn):
        bias = bs_ref[:, hd:hd + 1]
        parts = [jnp.dot(ws_ref[hd], vn[c * CHUNK:(c + 1) * CHUNK, :],
                         preferred_element_type=_F32) + bias for c in range(SUB_TILE // CHUNK)]
        return jnp.concatenate(parts, axis=0)

    def gate_a(hd, mixed):
        lo = hd * A_HEAD_DIM
        u = _gelu_tanh(z_read[:, OFF_U + lo:OFF_U + lo + A_HEAD_DIM])
        g = _silu(z_read[:, OFF_GA + lo:OFF_GA + lo + A_HEAD_DIM])
        pk_ref[0, rows, PK_A + lo:PK_A + lo + A_HEAD_DIM] = (u * mixed * g).astype(_BF16)

    def gate_b(lo, width):
        gb_ref[0, rows, lo:lo + width] = _silu(z_read[:, OFF_GB + lo:OFF_GB + lo + width])

    def rope_q(qa, hd):
        base = hd * QK_WIDTH
        pair = qa[:, base + KV_RANK:base + QK_WIDTH]
        rope = pair * t1 + pltpu.roll(pair, QK_ROPE, 1) * t2q
        pk_ref[0, rows, PK_Q + base:PK_Q + base + QK_WIDTH] = jnp.concatenate(
            [qa[:, base:base + KV_RANK], rope], axis=1).astype(_BF16)

    project(0)
    cq = _rmsnorm(z_read[:, OFF_CQ:OFF_CQ + Q_RANK],
                  qg_ref[...] * (ATTN_SCALE * LOG2_E)).astype(_BF16)
    vn0 = vn_head(0)
    project(1)
    vn1 = vn_head(1)
    gate_b(0, B_WIDTH // 2)
    project(2)
    vn2 = vn_head(2)
    gate_b(B_WIDTH // 2, B_WIDTH // 2)
    project(3)
    vn3 = vn_head(3)
    ckv = _rmsnorm(z_read[:, OFF_CKV:OFF_CKV + KV_RANK], kvg_ref[...])
    kpair = z_read[:, OFF_KR:OFF_KR + 2 * QK_ROPE]
    krot = kpair * t1 + pltpu.roll(kpair, QK_ROPE, 1) * t2k
    pk_ref[0, rows, PK_K:PK_K + QK_WIDTH] = jnp.concatenate([ckv, krot], axis=1).astype(_BF16)
    pk_ref[0, rows, PK_V:PK_V + 2 * KV_RANK] = jnp.concatenate(
        [ckv, jnp.ones_like(ckv)], axis=1).astype(_BF16)
    if emit_cache:
        ckv_ref[0, 0, rows, :] = ckv
        kr_ref[row0 // SUB_TILE, 0] = kpair.T[0:QK_ROPE, :]
    qa = jnp.dot(cq, wq_ref[...], preferred_element_type=_F32)
    norm_rows(0)
    project(4)
    rope_q(qa, 0)
    rope_q(qa, 1)
    mixed0 = mix_head(0, vn0)
    mixed1 = mix_head(1, vn1)
    rope_q(qa, 2)
    rope_q(qa, 3)
    project(5)
    gate_a(0, mixed0)
    mixed2 = mix_head(2, vn2)
    mixed3 = mix_head(3, vn3)
    project(6)
    gate_a(1, mixed1)
    norm_rows(half)
    project(7)
    gate_a(2, mixed2)
    project(8)
    gate_a(3, mixed3)
    project(9)


def _front_body(x_ref, mod_ref, ng_ref, win_ref, ws_ref, bs_ref, gv_ref, qg_ref, wq_ref,
                kvg_ref, tab_ref, *rest, emit_cache, tiles_per_row):
    out_refs, (h_a, h_b, z_a, z_b) = rest[:-4], rest[-4:]
    t = pl.program_id(0)
    tile = x_ref.shape[1]
    assert tile == 2 * SUB_TILE
    n_pos = tab_ref.shape[0]

    @pl.when(t == 0)
    def _():
        h_b[...] = jnp.zeros(h_b.shape, _BF16)
        z_b[...] = jnp.zeros(z_b.shape, _F32)

    pos_base = (jnp.maximum(t - 1, 0) % tiles_per_row) * tile
    for sub, (h_write, h_read, z_write, z_read) in enumerate(
            ((h_a, h_b, z_a, z_b), (h_b, h_a, z_b, z_a))):
        row0 = sub * SUB_TILE
        pos = pl.multiple_of((pos_base + row0) % n_pos, SUB_TILE)
        tables = tab_ref[pl.ds(pos, SUB_TILE), :]
        _front_substep(x_ref, mod_ref, ng_ref, win_ref, ws_ref, bs_ref, gv_ref, qg_ref, wq_ref,
                       kvg_ref, tables, out_refs, row0, h_write, h_read, z_write, z_read,
                       emit_cache)


def _front_call(x, mod, mod_index, norm_g, win2, ws_bf, bs_t, g_v, q_norm_g, wq2, kv_norm_g,
                tables, emit_cache):
    batch, seq, _ = x.shape
    tile = 2 * SUB_TILE
    tiles_per_row = seq // tile
    n_tiles = batch * tiles_per_row
    assert seq % tile == 0 and tables.shape[0] % SUB_TILE == 0

    def in_tile(t):
        tt = jnp.minimum(t, n_tiles - 1)
        return tt // tiles_per_row, tt % tiles_per_row

    def out_tile(t):
        tt = jnp.maximum(t - 1, 0)
        return tt // tiles_per_row, tt % tiles_per_row

    full = lambda shape: pl.BlockSpec(shape, lambda t: (0,) * len(shape))
    in_specs = [
        pl.BlockSpec((1, tile, D_MODEL), lambda t: (*in_tile(t), 0)),
        pl.BlockSpec((SUBLANES, 3 * D_MODEL), lambda t: (mod_index(in_tile(t)[0]), 0)),
        full((1, D_MODEL)),
        full((D_MODEL, IN_WIDTH2)),
        full((A_HEADS, CHUNK, CHUNK)),
        full((CHUNK, A_HEADS)),
        full((1, A_WIDTH)),
        full((1, Q_RANK)),
        full((Q_RANK, MLA_HEADS * QK_WIDTH)),
        full((1, KV_RANK)),
        full(tables.shape),
    ]
    out_shape = [
        jax.ShapeDtypeStruct((batch, seq, PK_WIDTH), _BF16),
        jax.ShapeDtypeStruct((batch, seq, B_WIDTH), _F32),
    ]
    out_specs = [
        pl.BlockSpec((1, tile, PK_WIDTH), lambda t: (*out_tile(t), 0)),
        pl.BlockSpec((1, tile, B_WIDTH), lambda t: (*out_tile(t), 0)),
    ]
    if emit_cache:
        out_shape += [
            jax.ShapeDtypeStruct((batch, 1, seq, KV_RANK), _F32),
            jax.ShapeDtypeStruct((n_tiles * 2, 1, QK_ROPE, SUB_TILE), _F32),
        ]
        out_specs += [
            pl.BlockSpec((1, 1, tile, KV_RANK), lambda t: (out_tile(t)[0], 0, out_tile(t)[1], 0)),
            pl.BlockSpec((2, 1, QK_ROPE, SUB_TILE), lambda t: (jnp.maximum(t - 1, 0), 0, 0, 0)),
        ]
    return pl.pallas_call(
        functools.partial(_front_body, emit_cache=emit_cache, tiles_per_row=tiles_per_row),
        grid=(n_tiles + 1,),
        in_specs=in_specs,
        out_specs=out_specs,
        out_shape=out_shape,
        scratch_shapes=[pltpu.VMEM((SUB_TILE, D_MODEL), _BF16), pltpu.VMEM((SUB_TILE, D_MODEL), _BF16),
                        pltpu.VMEM((SUB_TILE, IN_WIDTH2), _F32),
                        pltpu.VMEM((SUB_TILE, IN_WIDTH2), _F32)],
        compiler_params=pltpu.CompilerParams(
            dimension_semantics=("arbitrary",),
            vmem_limit_bytes=VMEM_LIMIT_BYTES),
        name="front_ctx" if emit_cache else "front_lat",
    )(x, mod, norm_g, win2, ws_bf, bs_t, g_v, q_norm_g, wq2, kv_norm_g, tables)


def _scores(q, k_refs, row):
    return [lax.dot_general(q, k_ref[row], _NT_DIMS, preferred_element_type=_F32)
            for k_ref in k_refs]


def _softmax_values(scores, v_refs, row):
    m = functools.reduce(jnp.maximum, [jnp.max(s, axis=1, keepdims=True) for s in scores])
    acc = None
    for s, values in zip(scores, v_refs):
        p = jnp.exp2(s - m).astype(_BF16)
        part = jnp.dot(p, values[row], preferred_element_type=_F32)
        acc = part if acc is None else acc + part
    return acc[:, 0:KV_RANK] / acc[:, KV_RANK:2 * KV_RANK]


def _back_body(*refs, has_ctx):
    if has_ctx:
        (q_ref, klat_ref, vlat_ref, kctx_ref, a_ref, gb_ref, x_ref, mod_ref, wuv_ref,
         wo_ref, fg_ref, y_ref) = refs
        k_refs, v_refs = (kctx_ref, klat_ref), (None, vlat_ref)
    else:
        (q_ref, klat_ref, vlat_ref, a_ref, gb_ref, x_ref, mod_ref, wuv_ref, wo_ref, fg_ref,
         y_ref) = refs
        k_refs, v_refs = (klat_ref,), (vlat_ref,)
    n_rows, tq = q_ref.shape[0], q_ref.shape[1]

    groups = [
        (row, jnp.concatenate([q_ref[row, :, hd * QK_WIDTH:(hd + 1) * QK_WIDTH]
                               for hd in (2 * g, 2 * g + 1)], axis=0))
        for row in range(n_rows) for g in range(MLA_HEADS // 2)]
    scores = [(row, _scores(qg, k_refs, row)) for row, qg in groups]
    if has_ctx:
        ckv = kctx_ref[:, :, 0:KV_RANK]
        v_refs = (jnp.concatenate([ckv, jnp.ones_like(ckv)], axis=-1), vlat_ref)
    outs = [_softmax_values(s, v_refs, row) for row, s in scores]

    n_pairs = MLA_HEADS // 2
    attn_cols = []
    for pair in range(n_pairs):
        o2 = jnp.concatenate(
            [jnp.concatenate([outs[row * n_pairs + pair][0:tq, :],
                              outs[row * n_pairs + pair][tq:2 * tq, :]], axis=1)
             for row in range(n_rows)], axis=0)
        attn_cols.append(jnp.dot(o2.astype(_BF16), wuv_ref[pair], preferred_element_type=_F32))
    flat = lambda ref: ref[...].reshape(n_rows * tq, ref.shape[-1])
    attn = jnp.concatenate(attn_cols, axis=1) * flat(gb_ref)

    y = jnp.dot(flat(a_ref), wo_ref[0:A_WIDTH, :], preferred_element_type=_F32)
    y = y + jnp.dot(attn.astype(_BF16), wo_ref[A_WIDTH:A_WIDTH + B_WIDTH, :],
                    preferred_element_type=_F32)
    gate = mod_ref[0:1, 2 * D_MODEL:3 * D_MODEL]
    out = flat(x_ref) + gate * y
    y_ref[...] = _rmsnorm(out, fg_ref[...]).reshape(y_ref.shape)


def _back_call(packed, kctx, gb, x, mod, mod_index, wuv2, wo_bf, final_g, rows_per_step):
    batch, seq, _ = x.shape
    tq = min(TOKEN_TILE, seq)
    has_ctx = kctx is not None
    rows = rows_per_step
    assert batch % rows == 0 and (rows == 1 or seq == tq)
    full = lambda shape: pl.BlockSpec(shape, lambda b, i: (0,) * len(shape))
    keys = lambda arr: pl.BlockSpec((rows, arr.shape[1], arr.shape[2]), lambda b, i: (b, 0, 0))
    in_specs = [
        pl.BlockSpec((rows, tq, MLA_HEADS * QK_WIDTH),
                     lambda b, i: (b, i, PK_Q // (MLA_HEADS * QK_WIDTH))),
        pl.BlockSpec((rows, seq, QK_WIDTH), lambda b, i: (b, 0, PK_K // QK_WIDTH)),
        pl.BlockSpec((rows, seq, 2 * KV_RANK), lambda b, i: (b, 0, PK_V // (2 * KV_RANK))),
    ]
    args = [packed, packed, packed]
    if has_ctx:
        in_specs += [keys(kctx)]
        args += [kctx]
    in_specs += [
        pl.BlockSpec((rows, tq, A_WIDTH), lambda b, i: (b, i, PK_A // A_WIDTH)),
        pl.BlockSpec((rows, tq, B_WIDTH), lambda b, i: (b, i, 0)),
        pl.BlockSpec((rows, tq, D_MODEL), lambda b, i: (b, i, 0)),
        pl.BlockSpec((SUBLANES, 3 * D_MODEL), lambda b, i: (mod_index(b * rows), 0)),
        full((MLA_HEADS // 2, 2 * KV_RANK, 2 * V_HEAD)),
        full((D_MODEL, D_MODEL)),
        full((1, D_MODEL)),
    ]
    args += [packed, gb, x, mod, wuv2, wo_bf, final_g]
    return pl.pallas_call(
        functools.partial(_back_body, has_ctx=has_ctx),
        grid=(batch // rows, seq // tq),
        in_specs=in_specs,
        out_specs=pl.BlockSpec((rows, tq, D_MODEL), lambda b, i: (b, i, 0)),
        out_shape=jax.ShapeDtypeStruct((batch, seq, D_MODEL), _F32),
        compiler_params=pltpu.CompilerParams(
            dimension_semantics=("arbitrary", "arbitrary"),
            vmem_limit_bytes=VMEM_LIMIT_BYTES),
        name="back_lat" if has_ctx else "back_ctx",
    )(*args)


def _rope_tables(n_tokens, rotate):
    if rotate:
        rows = n_tokens // GRID_W
        inv = ROPE_THETA ** (-jnp.arange(AXIS_PAIRS, dtype=_F32) / AXIS_PAIRS)
        ang_r = jnp.arange(rows, dtype=_F32)[:, None] * inv
        ang_c = jnp.arange(GRID_W, dtype=_F32)[:, None] * inv

        def grid_table(fn):
            tr = jnp.broadcast_to(fn(ang_r)[:, None, :], (rows, GRID_W, AXIS_PAIRS))
            tc = jnp.broadcast_to(fn(ang_c)[None, :, :], (rows, GRID_W, AXIS_PAIRS))
            half = jnp.concatenate([tr, tc], axis=-1).reshape(n_tokens, 2 * AXIS_PAIRS)
            return jnp.concatenate([half, half], axis=-1)

        cos, sin = grid_table(jnp.cos), grid_table(jnp.sin)
    else:
        cos, sin = jnp.ones((n_tokens, QK_ROPE), _F32), jnp.zeros((n_tokens, QK_ROPE), _F32)
    return jnp.concatenate([cos, sin], axis=-1)


def kernel(x_prompt, x_sample, cache_ckv, cache_krope, c, c_ctx, norm_g, w_ada, b_ada, w_in, w_s,
           b_s, g_v, q_norm_g, w_uq, kv_norm_g, w_ukv, w_o, final_g):
    depth = norm_g.shape[0]
    assert depth == 1 and w_in.shape[2] == IN_WIDTH
    dec_batch = x_sample.shape[0]
    xp, xs = x_prompt, x_sample
    new_ckv, new_kr = [], []
    for l in range(depth):
        mod, win2, wq2, wuv2 = _weights_call(c, c_ctx, w_ada[l], b_ada[l], jnp.transpose(w_in[l]),
                                             w_uq[l], w_ukv[l])
        wo_bf = w_o[l].astype(_BF16)
        bs_t = b_s[l].T
        gv_row = g_v[l].reshape(1, A_WIDTH)
        ws_bf = w_s[l].astype(_BF16)
        shared = (norm_g[l].reshape(1, D_MODEL), win2, ws_bf, bs_t, gv_row,
                  q_norm_g[l].reshape(1, Q_RANK), wq2, kv_norm_g[l].reshape(1, KV_RANK))
        fg = final_g.reshape(1, D_MODEL)

        ctx_index = lambda b: dec_batch
        ctx_batch, ctx_seq, _ = xp.shape
        pack = 2 * SUB_TILE // ctx_seq
        pk_c, gb_c, ckv_c, kr_c = _front_call(
            xp.reshape(ctx_batch // pack, pack * ctx_seq, D_MODEL), mod, ctx_index, *shared,
            _rope_tables(ctx_seq, False), True)
        unpack = lambda arr: arr.reshape(ctx_batch, ctx_seq, arr.shape[-1])
        xp = _back_call(unpack(pk_c), None, unpack(gb_c), xp, mod, ctx_index, wuv2, wo_bf, fg,
                        rows_per_step=pack)
        new_ckv.append(ckv_c.reshape(ctx_batch, 1, ctx_seq, KV_RANK))
        assert ctx_seq == SUB_TILE
        new_kr.append(jnp.swapaxes(kr_c, 2, 3))

        lat_index = lambda b: b
        pk_s, gb_s = _front_call(
            xs, mod, lat_index, *shared, _rope_tables(xs.shape[1], True), False)
        cache_k = jnp.concatenate(
            [cache_ckv[:, l], jnp.zeros(cache_krope[:, l].shape, _F32), cache_krope[:, l]],
            axis=-1).astype(_BF16)
        xs = _back_call(pk_s, cache_k, gb_s, xs, mod, lat_index, wuv2, wo_bf, fg, rows_per_step=1)
    return (xp, xs, jnp.concatenate(new_ckv, axis=1), jnp.concatenate(new_kr, axis=1))
```

```python
import functools
import math

import jax
import jax.numpy as jnp
from jax import lax
from jax.experimental import pallas as pl
from jax.experimental.pallas import tpu as pltpu

D_MODEL = 1024
GRID_W = 64
EPS = 1e-6
A_HEADS = 4
A_HEAD_DIM = 128
A_WIDTH = A_HEADS * A_HEAD_DIM
CHUNK = 128
MLA_HEADS = 4
QK_NOPE = 128
QK_ROPE = 64
V_HEAD = 128
B_WIDTH = MLA_HEADS * V_HEAD
Q_RANK = 256
KV_RANK = 128
AXIS_PAIRS = QK_ROPE // 4
ROPE_THETA = 10000.0
ATTN_SCALE = 1.0 / math.sqrt(QK_NOPE + QK_ROPE)
LOG2_E = 1.4426950408889634

QK_WIDTH = KV_RANK + 2 * QK_ROPE
OFF_U = 0
OFF_V = OFF_U + A_WIDTH
OFF_GA = OFF_V + A_WIDTH
OFF_CQ = OFF_GA + A_WIDTH
OFF_CKV = OFF_CQ + Q_RANK
OFF_KR = OFF_CKV + KV_RANK
OFF_GB = OFF_KR + 2 * QK_ROPE
IN_WIDTH = OFF_GB - QK_ROPE + B_WIDTH
IN_WIDTH2 = OFF_GB + B_WIDTH

PK_A = 0
PK_K = PK_A + A_WIDTH
PK_V = PK_K + QK_WIDTH
PK_Q = PK_V + 2 * KV_RANK
PK_WIDTH = PK_Q + MLA_HEADS * QK_WIDTH

LANES = 128
SUBLANES = 8
SUB_TILE = 256
TOKEN_TILE = 512
CTX_ROWS_PER_STEP = 4
PROJ_BLOCK = 256
VMEM_LIMIT_BYTES = 56 * 1024 * 1024

_F32 = jnp.float32
_BF16 = jnp.bfloat16
_NT_DIMS = (((1,), (1,)), ((), ()))


def _silu(x):
    hx = 0.5 * x
    return hx + hx * jnp.tanh(hx)


def _gelu_tanh(x):
    return x * (0.5 * (1.0 + jnp.tanh(0.7978845608028654 * (x + 0.044715 * (x * x * x)))))


def _rmsnorm(x, g):
    ms = jnp.mean(x * x, axis=-1, keepdims=True)
    return (x * lax.rsqrt(ms + EPS)) * g


def _mod_body(cond_ref, wa_ref, wb_ref, b_ref, o_ref):
    s = _silu(cond_ref[...]).astype(_BF16)
    halves = [jnp.dot(s, w_ref[...].astype(_BF16), preferred_element_type=_F32)
              for w_ref in (wa_ref, wb_ref)]
    mod = jnp.concatenate(halves, axis=1) + b_ref[...]
    rows, cols = mod.shape
    o_ref[...] = jnp.broadcast_to(mod[:, None, :], (rows, SUBLANES, cols)).reshape(
        rows * SUBLANES, cols)


def _fold_body(wq_ref, wkv_ref, q_ref, uv_ref):
    half, quarter = LANES // 2, LANES // 4
    assert QK_NOPE == LANES and QK_ROPE == half and V_HEAD == LANES
    lane = lax.broadcasted_iota(jnp.int32, (Q_RANK, LANES), 1)
    low = lane < half

    def cols(block):
        return wq_ref[:, block * LANES:(block + 1) * LANES]

    for hd in range(MLA_HEADS):
        start = hd * (QK_NOPE + QK_ROPE)
        b0, odd = divmod(start, LANES)
        if odd == 0:
            nope = cols(b0)
            rope_blk = cols(b0 + 1)
            rope_lo = rope_blk
            rope_hi = pltpu.roll(rope_blk, half, 1)
        else:
            assert odd == half
            nope = jnp.where(low, pltpu.roll(cols(b0), half, 1), pltpu.roll(cols(b0 + 1), half, 1))
            rope_blk = cols(b0 + 1)
            rope_lo = pltpu.roll(rope_blk, half, 1)
            rope_hi = rope_blk
        partner = jnp.where(lane < half + quarter,
                            -pltpu.roll(rope_hi, LANES - quarter, 1),
                            pltpu.roll(rope_hi, quarter, 1))
        w_uk = wkv_ref[:, hd * (QK_NOPE + V_HEAD):hd * (QK_NOPE + V_HEAD) + QK_NOPE]
        absorbed = lax.dot_general(nope, w_uk, _NT_DIMS, precision=lax.Precision.HIGHEST,
                                   preferred_element_type=_F32)
        base = hd * QK_WIDTH
        q_ref[:, base:base + KV_RANK] = absorbed.astype(_BF16)
        q_ref[:, base + KV_RANK:base + QK_WIDTH] = jnp.where(low, rope_lo, partner).astype(_BF16)

    uv_ref[...] = jnp.zeros(uv_ref.shape, _BF16)
    for hd in range(MLA_HEADS):
        w_uv = wkv_ref[:, hd * (QK_NOPE + V_HEAD) + QK_NOPE:(hd + 1) * (QK_NOPE + V_HEAD)]
        pair, pos = divmod(hd, 2)
        uv_ref[pair, pos * KV_RANK:(pos + 1) * KV_RANK, pos * V_HEAD:(pos + 1) * V_HEAD] = (
            w_uv.astype(_BF16))


_KR_BLOCK = OFF_KR // LANES


_PREP_SPLIT = IN_WIDTH2 // 2


def _prep_body(step, w_ref, o_ref):
    half = QK_ROPE // 2
    blocks_per_step = _PREP_SPLIT // LANES
    base = step * _PREP_SPLIT
    for j in range(step * blocks_per_step, (step + 1) * blocks_per_step):
        if j < _KR_BLOCK:
            blk = w_ref[j * LANES - base:(j + 1) * LANES - base, :]
        elif j == _KR_BLOCK:
            kr = w_ref[OFF_KR - base:OFF_KR - base + QK_ROPE, :]
            partner = jnp.concatenate([-kr[half:, :], kr[:half, :]], axis=0)
            blk = jnp.concatenate([kr, partner], axis=0)
        else:
            lo = j * LANES - QK_ROPE - base
            blk = w_ref[lo:lo + LANES, :]
        o_ref[:, j * LANES - base:(j + 1) * LANES - base] = blk.T.astype(_BF16)


def _weights_body(cond_ref, wa_ref, wb_ref, b_ref, win_ref, wq_ref, wkv_ref,
                  mod_ref, win2_ref, q_ref, uv_ref):
    j = pl.program_id(0)
    _mod_body(cond_ref, wa_ref, wb_ref, b_ref, mod_ref)
    n_prep = IN_WIDTH2 // _PREP_SPLIT
    for step in range(n_prep):
        pl.when(j == step)(functools.partial(_prep_body, step, win_ref, win2_ref))
    pl.when(j == n_prep)(functools.partial(_fold_body, wq_ref, wkv_ref, q_ref, uv_ref))


def _weights_call(cond, w_ada, b_ada, w_in_t, w_uq, w_ukv):
    rows = cond.shape[0]
    n_out = w_ada.shape[1]
    col_block = D_MODEL
    n_steps = n_out // col_block
    n_prep = IN_WIDTH2 // _PREP_SPLIT
    assert _KR_BLOCK * LANES == OFF_KR and OFF_KR >= _PREP_SPLIT and n_steps == n_prep + 1
    prep_step = lambda j: jnp.minimum(j, n_prep - 1)
    full = lambda shape: pl.BlockSpec(shape, lambda j: (0,) * len(shape))
    uv_shape = (MLA_HEADS // 2, 2 * KV_RANK, 2 * V_HEAD)
    return pl.pallas_call(
        _weights_body,
        grid=(n_steps,),
        in_specs=[
            full((rows, D_MODEL)),
            pl.BlockSpec((D_MODEL, col_block // 2), lambda j: (0, 2 * j)),
            pl.BlockSpec((D_MODEL, col_block // 2), lambda j: (0, 2 * j + 1)),
            pl.BlockSpec((1, col_block), lambda j: (0, j)),
            pl.BlockSpec((_PREP_SPLIT, D_MODEL), lambda j: (prep_step(j), 0)),
            full(w_uq.shape),
            full(w_ukv.shape),
        ],
        out_specs=[
            pl.BlockSpec((rows * SUBLANES, col_block), lambda j: (0, j)),
            pl.BlockSpec((D_MODEL, _PREP_SPLIT), lambda j: (0, prep_step(j))),
            full((Q_RANK, MLA_HEADS * QK_WIDTH)),
            full(uv_shape),
        ],
        out_shape=[
            jax.ShapeDtypeStruct((rows * SUBLANES, n_out), _F32),
            jax.ShapeDtypeStruct((D_MODEL, IN_WIDTH2), _BF16),
            jax.ShapeDtypeStruct((Q_RANK, MLA_HEADS * QK_WIDTH), _BF16),
            jax.ShapeDtypeStruct(uv_shape, _BF16),
        ],
        compiler_params=pltpu.CompilerParams(
            dimension_semantics=("arbitrary",), vmem_limit_bytes=VMEM_LIMIT_BYTES),
        name="weights",
    )(cond, w_ada, w_ada, b_ada.reshape(1, n_out), w_in_t, w_uq, w_ukv)


def _front_substep(x_ref, mod_ref, ng_ref, win_ref, ws_ref, bs_ref, gv_ref, qg_ref, wq_ref,
                   kvg_ref, tables, out_refs, row0, h_write, h_read, z_write, z_read, emit_cache):
    if emit_cache:
        pk_ref, gb_ref, ckv_ref, kr_ref = out_refs
    else:
        pk_ref, gb_ref = out_refs
    first_half = lax.broadcasted_iota(jnp.int32, tables.shape, 1) < QK_ROPE
    t1 = jnp.where(first_half, tables, 0.0)
    t2k = jnp.where(first_half, pltpu.roll(tables, QK_ROPE, 1), 0.0)
    t2q = jnp.where(first_half, t2k, 1.0)
    rows = slice(row0, row0 + SUB_TILE)
    half = SUB_TILE // 2
    assert IN_WIDTH2 == 10 * PROJ_BLOCK

    def project(j):
        lo = j * PROJ_BLOCK
        z_write[:, lo:lo + PROJ_BLOCK] = jnp.dot(h_read[...], win_ref[:, lo:lo + PROJ_BLOCK],
                                                 preferred_element_type=_F32)

    shift = mod_ref[0:1, 0:D_MODEL]
    gain = ng_ref[...] * (1.0 + mod_ref[0:1, D_MODEL:2 * D_MODEL])

    def norm_rows(lo):
        x = x_ref[0, row0 + lo:row0 + lo + half, :]
        ms = jnp.mean(x * x, axis=-1, keepdims=True)
        h_write[lo:lo + half, :] = ((x * lax.rsqrt(ms + EPS)) * gain + shift).astype(_BF16)

    def vn_head(hd):
        lo = hd * A_HEAD_DIM
        v = _gelu_tanh(z_read[:, OFF_V + lo:OFF_V + lo + A_HEAD_DIM])
        return _rmsnorm(v, gv_ref[:, lo:lo + A_HEAD_DIM]).astype(_BF16)

    def mix_head(hd, vn):
        bias = bs_ref[:, hd:hd + 1]
        parts = [jnp.dot(ws_ref[hd], vn[c * CHUNK:(c + 1) * CHUNK, :],
                         preferred_element_type=_F32) + bias for c in range(SUB_TILE // CHUNK)]
        return jnp.concatenate(parts, axis=0)

    def gate_a(hd, mixed):
        lo = hd * A_HEAD_DIM
        u = _gelu_tanh(z_read[:, OFF_U + lo:OFF_U + lo + A_HEAD_DIM])
        g = _silu(z_read[:, OFF_GA + lo:OFF_GA + lo + A_HEAD_DIM])
        pk_ref[0, rows, PK_A + lo:PK_A + lo + A_HEAD_DIM] = (u * mixed * g).astype(_BF16)

    def gate_b(lo, width):
        gb_ref[0, rows, lo:lo + width] = _silu(z_read[:, OFF_GB + lo:OFF_GB + lo + width])

    def rope_q(qa, hd):
        base = hd * QK_WIDTH
        pair = qa[:, base + KV_RANK:base + QK_WIDTH]
        rope = pair * t1 + pltpu.roll(pair, QK_ROPE, 1) * t2q
        pk_ref[0, rows, PK_Q + base:PK_Q + base + QK_WIDTH] = jnp.concatenate(
            [qa[:, base:base + KV_RANK], rope], axis=1).astype(_BF16)

    project(0)
    cq = _rmsnorm(z_read[:, OFF_CQ:OFF_CQ + Q_RANK],
                  qg_ref[...] * (ATTN_SCALE * LOG2_E)).astype(_BF16)
    vn0 = vn_head(0)
    project(1)
    vn1 = vn_head(1)
    gate_b(0, B_WIDTH // 2)
    project(2)
    vn2 = vn_head(2)
    gate_b(B_WIDTH // 2, B_WIDTH // 2)
    project(3)
    vn3 = vn_head(3)
    ckv = _rmsnorm(z_read[:, OFF_CKV:OFF_CKV + KV_RANK], kvg_ref[...])
    kpair = z_read[:, OFF_KR:OFF_KR + 2 * QK_ROPE]
    krot = kpair * t1 + pltpu.roll(kpair, QK_ROPE, 1) * t2k
    pk_ref[0, rows, PK_K:PK_K + QK_WIDTH] = jnp.concatenate([ckv, krot], axis=1).astype(_BF16)
    pk_ref[0, rows, PK_V:PK_V + 2 * KV_RANK] = jnp.concatenate(
        [ckv, jnp.ones_like(ckv)], axis=1).astype(_BF16)
    if emit_cache:
        ckv_ref[0, 0, rows, :] = ckv
        kr_ref[row0 // SUB_TILE, 0] = kpair.T[0:QK_ROPE, :]
    qa = jnp.dot(cq, wq_ref[...], preferred_element_type=_F32)
    norm_rows(0)
    project(4)
    rope_q(qa, 0)
    rope_q(qa, 1)
    mixed0 = mix_head(0, vn0)
    mixed1 = mix_head(1, vn1)
    rope_q(qa, 2)
    rope_q(qa, 3)
    project(5)
    gate_a(0, mixed0)
    mixed2 = mix_head(2, vn2)
    mixed3 = mix_head(3, vn3)
    project(6)
    gate_a(1, mixed1)
    norm_rows(half)
    project(7)
    gate_a(2, mixed2)
    project(8)
    gate_a(3, mixed3)
    project(9)


def _front_body(x_ref, mod_ref, ng_ref, win_ref, ws_ref, bs_ref, gv_ref, qg_ref, wq_ref,
                kvg_ref, tab_ref, *rest, emit_cache, tiles_per_row):
    out_refs, (h_a, h_b, z_a, z_b) = rest[:-4], rest[-4:]
    t = pl.program_id(0)
    tile = x_ref.shape[1]
    assert tile == 2 * SUB_TILE
    n_pos = tab_ref.shape[0]

    @pl.when(t == 0)
    def _():
        h_b[...] = jnp.zeros(h_b.shape, _BF16)
        z_b[...] = jnp.zeros(z_b.shape, _F32)

    pos_base = (jnp.maximum(t - 1, 0) % tiles_per_row) * tile
    for sub, (h_write, h_read, z_write, z_read) in enumerate(
            ((h_a, h_b, z_a, z_b), (h_b, h_a, z_b, z_a))):
        row0 = sub * SUB_TILE
        pos = pl.multiple_of((pos_base + row0) % n_pos, SUB_TILE)
        tables = tab_ref[pl.ds(pos, SUB_TILE), :]
        _front_substep(x_ref, mod_ref, ng_ref, win_ref, ws_ref, bs_ref, gv_ref, qg_ref, wq_ref,
                       kvg_ref, tables, out_refs, row0, h_write, h_read, z_write, z_read,
                       emit_cache)


def _front_call(x, mod, mod_index, norm_g, win2, ws_bf, bs_t, g_v, q_norm_g, wq2, kv_norm_g,
                tables, emit_cache):
    batch, seq, _ = x.shape
    tile = 2 * SUB_TILE
    tiles_per_row = seq // tile
    n_tiles = batch * tiles_per_row
    assert seq % tile == 0 and tables.shape[0] % SUB_TILE == 0

    def in_tile(t):
        tt = jnp.minimum(t, n_tiles - 1)
        return tt // tiles_per_row, tt % tiles_per_row

    def out_tile(t):
        tt = jnp.maximum(t - 1, 0)
        return tt // tiles_per_row, tt % tiles_per_row

    full = lambda shape: pl.BlockSpec(shape, lambda t: (0,) * len(shape))
    in_specs = [
        pl.BlockSpec((1, tile, D_MODEL), lambda t: (*in_tile(t), 0)),
        pl.BlockSpec((SUBLANES, 3 * D_MODEL), lambda t: (mod_index(in_tile(t)[0]), 0)),
        full((1, D_MODEL)),
        full((D_MODEL, IN_WIDTH2)),
        full((A_HEADS, CHUNK, CHUNK)),
        full((CHUNK, A_HEADS)),
        full((1, A_WIDTH)),
        full((1, Q_RANK)),
        full((Q_RANK, MLA_HEADS * QK_WIDTH)),
        full((1, KV_RANK)),
        full(tables.shape),
    ]
    out_shape = [
        jax.ShapeDtypeStruct((batch, seq, PK_WIDTH), _BF16),
        jax.ShapeDtypeStruct((batch, seq, B_WIDTH), _F32),
    ]
    out_specs = [
        pl.BlockSpec((1, tile, PK_WIDTH), lambda t: (*out_tile(t), 0)),
        pl.BlockSpec((1, tile, B_WIDTH), lambda t: (*out_tile(t), 0)),
    ]
    if emit_cache:
        out_shape += [
            jax.ShapeDtypeStruct((batch, 1, seq, KV_RANK), _F32),
            jax.ShapeDtypeStruct((n_tiles * 2, 1, QK_ROPE, SUB_TILE), _F32),
        ]
        out_specs += [
            pl.BlockSpec((1, 1, tile, KV_RANK), lambda t: (out_tile(t)[0], 0, out_tile(t)[1], 0)),
            pl.BlockSpec((2, 1, QK_ROPE, SUB_TILE), lambda t: (jnp.maximum(t - 1, 0), 0, 0, 0)),
        ]
    return pl.pallas_call(
        functools.partial(_front_body, emit_cache=emit_cache, tiles_per_row=tiles_per_row),
        grid=(n_tiles + 1,),
        in_specs=in_specs,
        out_specs=out_specs,
        out_shape=out_shape,
        scratch_shapes=[pltpu.VMEM((SUB_TILE, D_MODEL), _BF16), pltpu.VMEM((SUB_TILE, D_MODEL), _BF16),
                        pltpu.VMEM((SUB_TILE, IN_WIDTH2), _F32),
                        pltpu.VMEM((SUB_TILE, IN_WIDTH2), _F32)],
        compiler_params=pltpu.CompilerParams(
            dimension_semantics=("arbitrary",),
            vmem_limit_bytes=VMEM_LIMIT_BYTES),
        name="front_ctx" if emit_cache else "front_lat",
    )(x, mod, norm_g, win2, ws_bf, bs_t, g_v, q_norm_g, wq2, kv_norm_g, tables)


def _scores(q, k_refs, row):
    return [lax.dot_general(q, k_ref[row], _NT_DIMS, preferred_element_type=_F32)
            for k_ref in k_refs]


def _softmax_values(scores, v_refs, row):
    m = functools.reduce(jnp.maximum, [jnp.max(s, axis=1, keepdims=True) for s in scores])
    acc = None
    for s, values in zip(scores, v_refs):
        p = jnp.exp2(s - m).astype(_BF16)
        part = jnp.dot(p, values[row], preferred_element_type=_F32)
        acc = part if acc is None else acc + part
    return acc[:, 0:KV_RANK] / acc[:, KV_RANK:2 * KV_RANK]


def _back_body(*refs, has_ctx):
    if has_ctx:
        (q_ref, klat_ref, vlat_ref, kctx_ref, a_ref, gb_ref, x_ref, mod_ref, wuv_ref,
         wo_ref, fg_ref, y_ref) = refs
        k_refs, v_refs = (kctx_ref, klat_ref), (None, vlat_ref)
    else:
        (q_ref, klat_ref, vlat_ref, a_ref, gb_ref, x_ref, mod_ref, wuv_ref, wo_ref, fg_ref,
         y_ref) = refs
        k_refs, v_refs = (klat_ref,), (vlat_ref,)
    n_rows, tq = q_ref.shape[0], q_ref.shape[1]

    groups = [
        (row, jnp.concatenate([q_ref[row, :, hd * QK_WIDTH:(hd + 1) * QK_WIDTH]
                               for hd in (2 * g, 2 * g + 1)], axis=0))
        for row in range(n_rows) for g in range(MLA_HEADS // 2)]
    scores = [(row, _scores(qg, k_refs, row)) for row, qg in groups]
    if has_ctx:
        ckv = kctx_ref[:, :, 0:KV_RANK]
        v_refs = (jnp.concatenate([ckv, jnp.ones_like(ckv)], axis=-1), vlat_ref)
    outs = [_softmax_values(s, v_refs, row) for row, s in scores]

    n_pairs = MLA_HEADS // 2
    attn_cols = []
    for pair in range(n_pairs):
        o2 = jnp.concatenate(
            [jnp.concatenate([outs[row * n_pairs + pair][0:tq, :],
                              outs[row * n_pairs + pair][tq:2 * tq, :]], axis=1)
             for row in range(n_rows)], axis=0)
        attn_cols.append(jnp.dot(o2.astype(_BF16), wuv_ref[pair], preferred_element_type=_F32))
    flat = lambda ref: ref[...].reshape(n_rows * tq, ref.shape[-1])
    attn = jnp.concatenate(attn_cols, axis=1) * flat(gb_ref)

    y = jnp.dot(flat(a_ref), wo_ref[0:A_WIDTH, :], preferred_element_type=_F32)
    y = y + jnp.dot(attn.astype(_BF16), wo_ref[A_WIDTH:A_WIDTH + B_WIDTH, :],
                    preferred_element_type=_F32)
    gate = mod_ref[0:1, 2 * D_MODEL:3 * D_MODEL]
    out = flat(x_ref) + gate * y
    y_ref[...] = _rmsnorm(out, fg_ref[...]).reshape(y_ref.shape)


def _back_call(packed, kctx, gb, x, mod, mod_index, wuv2, wo_bf, final_g, rows_per_step):
    batch, seq, _ = x.shape
    tq = min(TOKEN_TILE, seq)
    has_ctx = kctx is not None
    rows = rows_per_step
    assert batch % rows == 0 and (rows == 1 or seq == tq)
    full = lambda shape: pl.BlockSpec(shape, lambda b, i: (0,) * len(shape))
    keys = lambda arr: pl.BlockSpec((rows, arr.shape[1], arr.shape[2]), lambda b, i: (b, 0, 0))
    in_specs = [
        pl.BlockSpec((rows, tq, MLA_HEADS * QK_WIDTH),
                     lambda b, i: (b, i, PK_Q // (MLA_HEADS * QK_WIDTH))),
        pl.BlockSpec((rows, seq, QK_WIDTH), lambda b, i: (b, 0, PK_K // QK_WIDTH)),
        pl.BlockSpec((rows, seq, 2 * KV_RANK), lambda b, i: (b, 0, PK_V // (2 * KV_RANK))),
    ]
    args = [packed, packed, packed]
    if has_ctx:
        in_specs += [keys(kctx)]
        args += [kctx]
    in_specs += [
        pl.BlockSpec((rows, tq, A_WIDTH), lambda b, i: (b, i, PK_A // A_WIDTH)),
        pl.BlockSpec((rows, tq, B_WIDTH), lambda b, i: (b, i, 0)),
        pl.BlockSpec((rows, tq, D_MODEL), lambda b, i: (b, i, 0)),
        pl.BlockSpec((SUBLANES, 3 * D_MODEL), lambda b, i: (mod_index(b * rows), 0)),
        full((MLA_HEADS // 2, 2 * KV_RANK, 2 * V_HEAD)),
        full((D_MODEL, D_MODEL)),
        full((1, D_MODEL)),
    ]
    args += [packed, gb, x, mod, wuv2, wo_bf, final_g]
    return pl.pallas_call(
        functools.partial(_back_body, has_ctx=has_ctx),
        grid=(batch // rows, seq // tq),
        in_specs=in_specs,
        out_specs=pl.BlockSpec((rows, tq, D_MODEL), lambda b, i: (b, i, 0)),
        out_shape=jax.ShapeDtypeStruct((batch, seq, D_MODEL), _F32),
        compiler_params=pltpu.CompilerParams(
            dimension_semantics=("arbitrary", "arbitrary"),
            vmem_limit_bytes=VMEM_LIMIT_BYTES),
        name="back_lat" if has_ctx else "back_ctx",
    )(*args)


def _rope_tables(n_tokens, rotate):
    if rotate:
        rows = n_tokens // GRID_W
        inv = ROPE_THETA ** (-jnp.arange(AXIS_PAIRS, dtype=_F32) / AXIS_PAIRS)
        ang_r = jnp.arange(rows, dtype=_F32)[:, None] * inv
        ang_c = jnp.arange(GRID_W, dtype=_F32)[:, None] * inv

        def grid_table(fn):
            tr = jnp.broadcast_to(fn(ang_r)[:, None, :], (rows, GRID_W, AXIS_PAIRS))
            tc = jnp.broadcast_to(fn(ang_c)[None, :, :], (rows, GRID_W, AXIS_PAIRS))
            half = jnp.concatenate([tr, tc], axis=-1).reshape(n_tokens, 2 * AXIS_PAIRS)
            return jnp.concatenate([half, half], axis=-1)

        cos, sin = grid_table(jnp.cos), grid_table(jnp.sin)
    else:
        cos, sin = jnp.ones((n_tokens, QK_ROPE), _F32), jnp.zeros((n_tokens, QK_ROPE), _F32)
    return jnp.concatenate([cos, sin], axis=-1)


def kernel(x_prompt, x_sample, cache_ckv, cache_krope, c, c_ctx, norm_g, w_ada, b_ada, w_in, w_s,
           b_s, g_v, q_norm_g, w_uq, kv_norm_g, w_ukv, w_o, final_g):
    depth = norm_g.shape[0]
    assert depth == 1 and w_in.shape[2] == IN_WIDTH
    dec_batch = x_sample.shape[0]
    xp, xs = x_prompt, x_sample
    new_ckv, new_kr = [], []
    for l in range(depth):
        cond = jnp.concatenate(
            [c, c_ctx[None, :], jnp.zeros((16 - dec_batch - 1, D_MODEL), _F32)], axis=0)
        mod, win2, wq2, wuv2 = _weights_call(cond, w_ada[l], b_ada[l], jnp.transpose(w_in[l]),
                                             w_uq[l], w_ukv[l])
        wo_bf = w_o[l].astype(_BF16)
        ws_bf = w_s[l].astype(_BF16)
        bs_t = b_s[l].T
        gv_row = g_v[l].reshape(1, A_WIDTH)
        shared = (norm_g[l].reshape(1, D_MODEL), win2, ws_bf, bs_t, gv_row,
                  q_norm_g[l].reshape(1, Q_RANK), wq2, kv_norm_g[l].reshape(1, KV_RANK))
        fg = final_g.reshape(1, D_MODEL)

        ctx_index = lambda b: dec_batch
        ctx_batch, ctx_seq, _ = xp.shape
        pack = 2 * SUB_TILE // ctx_seq
        pk_c, gb_c, ckv_c, kr_c = _front_call(
            xp.reshape(ctx_batch // pack, pack * ctx_seq, D_MODEL), mod, ctx_index, *shared,
            _rope_tables(ctx_seq, False), True)
        unpack = lambda arr: arr.reshape(ctx_batch, ctx_seq, arr.shape[-1])
        xp = _back_call(unpack(pk_c), None, unpack(gb_c), xp, mod, ctx_index, wuv2, wo_bf, fg,
                        rows_per_step=CTX_ROWS_PER_STEP)
        new_ckv.append(ckv_c.reshape(ctx_batch, 1, ctx_seq, KV_RANK))
        assert ctx_seq == SUB_TILE
        new_kr.append(jnp.swapaxes(kr_c, 2, 3))

        lat_index = lambda b: b
        pk_s, gb_s = _front_call(
            xs, mod, lat_index, *shared, _rope_tables(xs.shape[1], True), False)
        cache_k = jnp.concatenate(
            [cache_ckv[:, l], jnp.zeros(cache_krope[:, l].shape, _F32), cache_krope[:, l]],
            axis=-1).astype(_BF16)
        xs = _back_call(pk_s, cache_k, gb_s, xs, mod, lat_index, wuv2, wo_bf, fg, rows_per_step=1)
    return (xp, xs, jnp.concatenate(new_ckv, axis=1), jnp.concatenate(new_kr, axis=1))
```

```python
import functools
import math

import jax
import jax.numpy as jnp
from jax import lax
from jax.experimental import pallas as pl
from jax.experimental.pallas import tpu as pltpu

D_MODEL = 1024
GRID_W = 64
EPS = 1e-6
A_HEADS = 4
A_HEAD_DIM = 128
A_WIDTH = A_HEADS * A_HEAD_DIM
CHUNK = 128
MLA_HEADS = 4
QK_NOPE = 128
QK_ROPE = 64
V_HEAD = 128
B_WIDTH = MLA_HEADS * V_HEAD
Q_RANK = 256
KV_RANK = 128
AXIS_PAIRS = QK_ROPE // 4
ROPE_THETA = 10000.0
ATTN_SCALE = 1.0 / math.sqrt(QK_NOPE + QK_ROPE)
LOG2_E = 1.4426950408889634

QK_WIDTH = KV_RANK + 2 * QK_ROPE
OFF_U = 0
OFF_V = OFF_U + A_WIDTH
OFF_GA = OFF_V + A_WIDTH
OFF_CQ = OFF_GA + A_WIDTH
OFF_CKV = OFF_CQ + Q_RANK
OFF_KR = OFF_CKV + KV_RANK
OFF_GB = OFF_KR + 2 * QK_ROPE
IN_WIDTH = OFF_GB - QK_ROPE + B_WIDTH
IN_WIDTH2 = OFF_GB + B_WIDTH

PK_A = 0
PK_K = PK_A + A_WIDTH
PK_V = PK_K + QK_WIDTH
PK_Q = PK_V + 2 * KV_RANK
PK_WIDTH = PK_Q + MLA_HEADS * QK_WIDTH

LANES = 128
SUBLANES = 8
ONES_ROWS = 16
SUB_TILE = 256
TOKEN_TILE = 512
PROJ_BLOCK = 256
KEY_CHUNK = 512
SCORE_LOOKAHEAD = 2
VMEM_LIMIT_BYTES = 56 * 1024 * 1024

_F32 = jnp.float32
_BF16 = jnp.bfloat16
_NT_DIMS = (((1,), (1,)), ((), ()))


def _silu(x):
    hx = 0.5 * x
    return hx + hx * jnp.tanh(hx)


def _gelu_tanh(x):
    return x * (0.5 * (1.0 + jnp.tanh(0.7978845608028654 * (x + 0.044715 * (x * x * x)))))


def _rmsnorm(x, g):
    ms = jnp.mean(x * x, axis=-1, keepdims=True)
    return (x * lax.rsqrt(ms + EPS)) * g


def _mod_body(cond_ref, wa_ref, wb_ref, b_ref, o_ref):
    s = _silu(cond_ref[...]).astype(_BF16)
    halves = [jnp.dot(s, w_ref[...].astype(_BF16), preferred_element_type=_F32)
              for w_ref in (wa_ref, wb_ref)]
    mod = jnp.concatenate(halves, axis=1) + b_ref[...]
    rows, cols = mod.shape
    o_ref[...] = jnp.broadcast_to(mod[:, None, :], (rows, SUBLANES, cols)).reshape(
        rows * SUBLANES, cols)


def _fold_body(wq_ref, wkv_ref, q_ref, uv_ref):
    half, quarter = LANES // 2, LANES // 4
    assert QK_NOPE == LANES and QK_ROPE == half and V_HEAD == LANES
    lane = lax.broadcasted_iota(jnp.int32, (Q_RANK, LANES), 1)
    low = lane < half

    def cols(block):
        return wq_ref[:, block * LANES:(block + 1) * LANES]

    for hd in range(MLA_HEADS):
        start = hd * (QK_NOPE + QK_ROPE)
        b0, odd = divmod(start, LANES)
        if odd == 0:
            nope = cols(b0)
            rope_blk = cols(b0 + 1)
            rope_lo = rope_blk
            rope_hi = pltpu.roll(rope_blk, half, 1)
        else:
            assert odd == half
            nope = jnp.where(low, pltpu.roll(cols(b0), half, 1), pltpu.roll(cols(b0 + 1), half, 1))
            rope_blk = cols(b0 + 1)
            rope_lo = pltpu.roll(rope_blk, half, 1)
            rope_hi = rope_blk
        partner = jnp.where(lane < half + quarter,
                            -pltpu.roll(rope_hi, LANES - quarter, 1),
                            pltpu.roll(rope_hi, quarter, 1))
        w_uk = wkv_ref[:, hd * (QK_NOPE + V_HEAD):hd * (QK_NOPE + V_HEAD) + QK_NOPE]
        absorbed = lax.dot_general(nope, w_uk, _NT_DIMS, precision=lax.Precision.HIGHEST,
                                   preferred_element_type=_F32)
        base = hd * QK_WIDTH
        q_ref[:, base:base + KV_RANK] = absorbed.astype(_BF16)
        q_ref[:, base + KV_RANK:base + QK_WIDTH] = jnp.where(low, rope_lo, partner).astype(_BF16)

    uv_ref[...] = jnp.zeros(uv_ref.shape, _BF16)
    for hd in range(MLA_HEADS):
        w_uv = wkv_ref[:, hd * (QK_NOPE + V_HEAD) + QK_NOPE:(hd + 1) * (QK_NOPE + V_HEAD)]
        pair, pos = divmod(hd, 2)
        uv_ref[pair, pos * KV_RANK:(pos + 1) * KV_RANK, pos * V_HEAD:(pos + 1) * V_HEAD] = (
            w_uv.astype(_BF16))


_KR_BLOCK = OFF_KR // LANES


_PREP_SPLIT = IN_WIDTH2 // 2


def _prep_body(step, w_ref, o_ref):
    half = QK_ROPE // 2
    blocks_per_step = _PREP_SPLIT // LANES
    base = step * _PREP_SPLIT
    for j in range(step * blocks_per_step, (step + 1) * blocks_per_step):
        if j < _KR_BLOCK:
            blk = w_ref[j * LANES - base:(j + 1) * LANES - base, :]
        elif j == _KR_BLOCK:
            kr = w_ref[OFF_KR - base:OFF_KR - base + QK_ROPE, :]
            partner = jnp.concatenate([-kr[half:, :], kr[:half, :]], axis=0)
            blk = jnp.concatenate([kr, partner], axis=0)
        else:
            lo = j * LANES - QK_ROPE - base
            blk = w_ref[lo:lo + LANES, :]
        o_ref[:, j * LANES - base:(j + 1) * LANES - base] = blk.T.astype(_BF16)


def _weights_body(cond_ref, wa_ref, wb_ref, b_ref, win_ref, wq_ref, wkv_ref,
                  mod_ref, win2_ref, q_ref, uv_ref):
    j = pl.program_id(0)
    _mod_body(cond_ref, wa_ref, wb_ref, b_ref, mod_ref)
    n_prep = IN_WIDTH2 // _PREP_SPLIT
    for step in range(n_prep):
        pl.when(j == step)(functools.partial(_prep_body, step, win_ref, win2_ref))
    pl.when(j == n_prep)(functools.partial(_fold_body, wq_ref, wkv_ref, q_ref, uv_ref))


def _weights_call(cond, w_ada, b_ada, w_in_t, w_uq, w_ukv):
    rows = cond.shape[0]
    n_out = w_ada.shape[1]
    col_block = D_MODEL
    n_steps = n_out // col_block
    n_prep = IN_WIDTH2 // _PREP_SPLIT
    assert _KR_BLOCK * LANES == OFF_KR and OFF_KR >= _PREP_SPLIT and n_steps == n_prep + 1
    prep_step = lambda j: jnp.minimum(j, n_prep - 1)
    full = lambda shape: pl.BlockSpec(shape, lambda j: (0,) * len(shape))
    uv_shape = (MLA_HEADS // 2, 2 * KV_RANK, 2 * V_HEAD)
    return pl.pallas_call(
        _weights_body,
        grid=(n_steps,),
        in_specs=[
            full((rows, D_MODEL)),
            pl.BlockSpec((D_MODEL, col_block // 2), lambda j: (0, 2 * j)),
            pl.BlockSpec((D_MODEL, col_block // 2), lambda j: (0, 2 * j + 1)),
            pl.BlockSpec((1, col_block), lambda j: (0, j)),
            pl.BlockSpec((_PREP_SPLIT, D_MODEL), lambda j: (prep_step(j), 0)),
            full(w_uq.shape),
            full(w_ukv.shape),
        ],
        out_specs=[
            pl.BlockSpec((rows * SUBLANES, col_block), lambda j: (0, j)),
            pl.BlockSpec((D_MODEL, _PREP_SPLIT), lambda j: (0, prep_step(j))),
            full((Q_RANK, MLA_HEADS * QK_WIDTH)),
            full(uv_shape),
        ],
        out_shape=[
            jax.ShapeDtypeStruct((rows * SUBLANES, n_out), _F32),
            jax.ShapeDtypeStruct((D_MODEL, IN_WIDTH2), _BF16),
            jax.ShapeDtypeStruct((Q_RANK, MLA_HEADS * QK_WIDTH), _BF16),
            jax.ShapeDtypeStruct(uv_shape, _BF16),
        ],
        compiler_params=pltpu.CompilerParams(
            dimension_semantics=("arbitrary",), vmem_limit_bytes=VMEM_LIMIT_BYTES),
        name="weights",
    )(cond, w_ada, w_ada, b_ada.reshape(1, n_out), w_in_t, w_uq, w_ukv)


def _front_substep(x_ref, mod_ref, ng_ref, win_ref, ws_ref, bs_ref, gv_ref, qg_ref, wq_ref,
                   kvg_ref, tables, out_refs, row0, h_write, h_read, z_write, z_read, emit_cache):
    if emit_cache:
        pk_ref, gb_ref, ckv_ref, kr_ref = out_refs
    else:
        pk_ref, gb_ref = out_refs
    first_half = lax.broadcasted_iota(jnp.int32, tables.shape, 1) < QK_ROPE
    t1 = jnp.where(first_half, tables, 0.0)
    t2k = jnp.where(first_half, pltpu.roll(tables, QK_ROPE, 1), 0.0)
    t2q = jnp.where(first_half, t2k, 1.0)
    rows = slice(row0, row0 + SUB_TILE)
    half = SUB_TILE // 2
    assert IN_WIDTH2 == 10 * PROJ_BLOCK

    def project(j):
        lo = j * PROJ_BLOCK
        z_write[:, lo:lo + PROJ_BLOCK] = jnp.dot(h_read[...], win_ref[:, lo:lo + PROJ_BLOCK],
                                                 preferred_element_type=_F32)

    shift = mod_ref[0:1, 0:D_MODEL]
    gain = ng_ref[...] * (1.0 + mod_ref[0:1, D_MODEL:2 * D_MODEL])

    def norm_rows(lo):
        x = x_ref[0, row0 + lo:row0 + lo + half, :]
        ms = jnp.mean(x * x, axis=-1, keepdims=True)
        h_write[lo:lo + half, :] = ((x * lax.rsqrt(ms + EPS)) * gain + shift).astype(_BF16)

    def vn_head(hd):
        lo = hd * A_HEAD_DIM
        v = _gelu_tanh(z_read[:, OFF_V + lo:OFF_V + lo + A_HEAD_DIM])
        return _rmsnorm(v, gv_ref[:, lo:lo + A_HEAD_DIM]).astype(_BF16)

    def mix_head(hd, vn):
        bias = bs_ref[:, hd:hd + 1]
        parts = [jnp.dot(ws_ref[hd], vn[c * CHUNK:(c + 1) * CHUNK, :],
                         preferred_element_type=_F32) + bias for c in range(SUB_TILE // CHUNK)]
        return jnp.concatenate(parts, axis=0)

    def gate_a(hd, mixed):
        lo = hd * A_HEAD_DIM
        u = _gelu_tanh(z_read[:, OFF_U + lo:OFF_U + lo + A_HEAD_DIM])
        g = _silu(z_read[:, OFF_GA + lo:OFF_GA + lo + A_HEAD_DIM])
        pk_ref[0, rows, PK_A + lo:PK_A + lo + A_HEAD_DIM] = (u * mixed * g).astype(_BF16)

    def gate_b(lo, width):
        gb_ref[0, rows, lo:lo + width] = _silu(z_read[:, OFF_GB + lo:OFF_GB + lo + width])

    def rope_q(qa, hd):
        base = hd * QK_WIDTH
        pair = qa[:, base + KV_RANK:base + QK_WIDTH]
        rope = pair * t1 + pltpu.roll(pair, QK_ROPE, 1) * t2q
        pk_ref[0, rows, PK_Q + base:PK_Q + base + QK_WIDTH] = jnp.concatenate(
            [qa[:, base:base + KV_RANK], rope], axis=1).astype(_BF16)

    project(0)
    cq = _rmsnorm(z_read[:, OFF_CQ:OFF_CQ + Q_RANK],
                  qg_ref[...] * (ATTN_SCALE * LOG2_E)).astype(_BF16)
    vn0 = vn_head(0)
    project(1)
    vn1 = vn_head(1)
    gate_b(0, B_WIDTH // 2)
    project(2)
    vn2 = vn_head(2)
    gate_b(B_WIDTH // 2, B_WIDTH // 2)
    project(3)
    vn3 = vn_head(3)
    ckv = _rmsnorm(z_read[:, OFF_CKV:OFF_CKV + KV_RANK], kvg_ref[...])
    kpair = z_read[:, OFF_KR:OFF_KR + 2 * QK_ROPE]
    krot = kpair * t1 + pltpu.roll(kpair, QK_ROPE, 1) * t2k
    pk_ref[0, rows, PK_K:PK_K + QK_WIDTH] = jnp.concatenate([ckv, krot], axis=1).astype(_BF16)
    pk_ref[0, rows, PK_V:PK_V + 2 * KV_RANK] = jnp.concatenate(
        [ckv, jnp.ones_like(ckv)], axis=1).astype(_BF16)
    if emit_cache:
        ckv_ref[0, 0, rows, :] = ckv
        kr_ref[row0 // SUB_TILE, 0] = kpair.T[0:QK_ROPE, :]
    qa = jnp.dot(cq, wq_ref[...], preferred_element_type=_F32)
    norm_rows(0)
    project(4)
    rope_q(qa, 0)
    rope_q(qa, 1)
    mixed0 = mix_head(0, vn0)
    mixed1 = mix_head(1, vn1)
    rope_q(qa, 2)
    rope_q(qa, 3)
    project(5)
    gate_a(0, mixed0)
    mixed2 = mix_head(2, vn2)
    mixed3 = mix_head(3, vn3)
    project(6)
    gate_a(1, mixed1)
    norm_rows(half)
    project(7)
    gate_a(2, mixed2)
    project(8)
    gate_a(3, mixed3)
    project(9)


def _front_body(x_ref, mod_ref, ng_ref, win_ref, ws_ref, bs_ref, gv_ref, qg_ref, wq_ref,
                kvg_ref, tab_ref, *rest, emit_cache, tiles_per_row):
    out_refs, (h_a, h_b, z_a, z_b) = rest[:-4], rest[-4:]
    t = pl.program_id(0)
    tile = x_ref.shape[1]
    assert tile == 2 * SUB_TILE
    n_pos = tab_ref.shape[0]

    @pl.when(t == 0)
    def _():
        h_b[...] = jnp.zeros(h_b.shape, _BF16)
        z_b[...] = jnp.zeros(z_b.shape, _F32)

    pos_base = (jnp.maximum(t - 1, 0) % tiles_per_row) * tile
    for sub, (h_write, h_read, z_write, z_read) in enumerate(
            ((h_a, h_b, z_a, z_b), (h_b, h_a, z_b, z_a))):
        row0 = sub * SUB_TILE
        pos = pl.multiple_of((pos_base + row0) % n_pos, SUB_TILE)
        tables = tab_ref[pl.ds(pos, SUB_TILE), :]
        _front_substep(x_ref, mod_ref, ng_ref, win_ref, ws_ref, bs_ref, gv_ref, qg_ref, wq_ref,
                       kvg_ref, tables, out_refs, row0, h_write, h_read, z_write, z_read,
                       emit_cache)


def _front_call(x, mod, mod_index, norm_g, win2, ws_bf, bs_t, g_v, q_norm_g, wq2, kv_norm_g,
                tables, emit_cache):
    batch, seq, _ = x.shape
    tile = 2 * SUB_TILE
    tiles_per_row = seq // tile
    n_tiles = batch * tiles_per_row
    assert seq % tile == 0 and tables.shape[0] % SUB_TILE == 0

    def in_tile(t):
        tt = jnp.minimum(t, n_tiles - 1)
        return tt // tiles_per_row, tt % tiles_per_row

    def out_tile(t):
        tt = jnp.maximum(t - 1, 0)
        return tt // tiles_per_row, tt % tiles_per_row

    full = lambda shape: pl.BlockSpec(shape, lambda t: (0,) * len(shape))
    in_specs = [
        pl.BlockSpec((1, tile, D_MODEL), lambda t: (*in_tile(t), 0)),
        pl.BlockSpec((SUBLANES, 3 * D_MODEL), lambda t: (mod_index(in_tile(t)[0]), 0)),
        full((1, D_MODEL)),
        full((D_MODEL, IN_WIDTH2)),
        full((A_HEADS, CHUNK, CHUNK)),
        full((CHUNK, A_HEADS)),
        full((1, A_WIDTH)),
        full((1, Q_RANK)),
        full((Q_RANK, MLA_HEADS * QK_WIDTH)),
        full((1, KV_RANK)),
        full(tables.shape),
    ]
    out_shape = [
        jax.ShapeDtypeStruct((batch, seq, PK_WIDTH), _BF16),
        jax.ShapeDtypeStruct((batch, seq, B_WIDTH), _F32),
    ]
    out_specs = [
        pl.BlockSpec((1, tile, PK_WIDTH), lambda t: (*out_tile(t), 0)),
        pl.BlockSpec((1, tile, B_WIDTH), lambda t: (*out_tile(t), 0)),
    ]
    if emit_cache:
        out_shape += [
            jax.ShapeDtypeStruct((batch, 1, seq, KV_RANK), _F32),
            jax.ShapeDtypeStruct((n_tiles * 2, 1, QK_ROPE, SUB_TILE), _F32),
        ]
        out_specs += [
            pl.BlockSpec((1, 1, tile, KV_RANK), lambda t: (out_tile(t)[0], 0, out_tile(t)[1], 0)),
            pl.BlockSpec((2, 1, QK_ROPE, SUB_TILE), lambda t: (jnp.maximum(t - 1, 0), 0, 0, 0)),
        ]
    return pl.pallas_call(
        functools.partial(_front_body, emit_cache=emit_cache, tiles_per_row=tiles_per_row),
        grid=(n_tiles + 1,),
        in_specs=in_specs,
        out_specs=out_specs,
        out_shape=out_shape,
        scratch_shapes=[pltpu.VMEM((SUB_TILE, D_MODEL), _BF16), pltpu.VMEM((SUB_TILE, D_MODEL), _BF16),
                        pltpu.VMEM((SUB_TILE, IN_WIDTH2), _F32),
                        pltpu.VMEM((SUB_TILE, IN_WIDTH2), _F32)],
        compiler_params=pltpu.CompilerParams(
            dimension_semantics=("arbitrary",),
            vmem_limit_bytes=VMEM_LIMIT_BYTES),
        name="front_ctx" if emit_cache else "front_lat",
    )(x, mod, norm_g, win2, ws_bf, bs_t, g_v, q_norm_g, wq2, kv_norm_g, tables)


def _key_chunks(k_refs):
    chunks = []
    for k_ref in k_refs:
        size = min(KEY_CHUNK, k_ref.shape[1])
        assert k_ref.shape[1] % size == 0
        chunks += [(k_ref, lo, size) for lo in range(0, k_ref.shape[1], size)]
    return chunks


def _values_t(k_ref, row):
    ckv_t = k_ref[row, :, 0:KV_RANK].T
    return jnp.concatenate([ckv_t, jnp.ones((ONES_ROWS, ckv_t.shape[1]), _BF16)], axis=0)


def _attention(q_ref, k_refs, qt_ref):
    n_rows = q_ref.shape[0]
    chunks = _key_chunks(k_refs)
    steps = [(row, hd, c) for row in range(n_rows) for hd in range(MLA_HEADS)
             for c in range(len(chunks))]

    for row in range(n_rows):
        for hd in range(MLA_HEADS):
            qt_ref[row * MLA_HEADS + hd] = q_ref[row, :, hd * QK_WIDTH:(hd + 1) * QK_WIDTH].T

    def logits(row, hd, c):
        k_ref, lo, size = chunks[c]
        return jnp.dot(k_ref[row, lo:lo + size, :], qt_ref[row * MLA_HEADS + hd],
                       preferred_element_type=_F32)

    pending = [logits(*step) for step in steps[:SCORE_LOOKAHEAD]]
    values_t = [{id(k_ref): _values_t(k_ref, row) for k_ref in k_refs} for row in range(n_rows)]
    outs = {}
    m = acc = None
    for j, (row, hd, c) in enumerate(steps):
        if j + SCORE_LOOKAHEAD < len(steps):
            pending.append(logits(*steps[j + SCORE_LOOKAHEAD]))
        s, pending[j] = pending[j], None
        k_ref, lo, size = chunks[c]
        chunk_max = jnp.max(s, axis=0, keepdims=True)
        if c > 0:
            m_new = jnp.maximum(m, chunk_max)
            acc = acc * jnp.exp2(m - m_new)
            m = m_new
        else:
            m = chunk_max
        p = jnp.exp2(s - m).astype(_BF16)
        part = jnp.dot(values_t[row][id(k_ref)][:, lo:lo + size], p, preferred_element_type=_F32)
        acc = acc + part if c > 0 else part
        if c == len(chunks) - 1:
            outs[row, hd] = (acc[0:KV_RANK, :] / acc[KV_RANK:KV_RANK + 1, :]).T
    return outs


def _back_body(*refs, has_ctx):
    if has_ctx:
        (q_ref, klat_ref, kctx_ref, a_ref, gb_ref, x_ref, mod_ref, wuv_ref, wo_ref, fg_ref,
         y_ref, qt_ref) = refs
        k_refs = (kctx_ref, klat_ref)
    else:
        (q_ref, klat_ref, a_ref, gb_ref, x_ref, mod_ref, wuv_ref, wo_ref, fg_ref, y_ref,
         qt_ref) = refs
        k_refs = (klat_ref,)
    n_rows, tq = q_ref.shape[0], q_ref.shape[1]

    outs = _attention(q_ref, k_refs, qt_ref)

    attn_cols = []
    for pair in range(MLA_HEADS // 2):
        o2 = jnp.concatenate(
            [jnp.concatenate([outs[row, 2 * pair], outs[row, 2 * pair + 1]], axis=1)
             for row in range(n_rows)], axis=0)
        attn_cols.append(jnp.dot(o2.astype(_BF16), wuv_ref[pair], preferred_element_type=_F32))
    flat = lambda ref: ref[...].reshape(n_rows * tq, ref.shape[-1])
    attn = jnp.concatenate(attn_cols, axis=1) * flat(gb_ref)

    y = jnp.dot(flat(a_ref), wo_ref[0:A_WIDTH, :], preferred_element_type=_F32)
    y = y + jnp.dot(attn.astype(_BF16), wo_ref[A_WIDTH:A_WIDTH + B_WIDTH, :],
                    preferred_element_type=_F32)
    gate = mod_ref[0:1, 2 * D_MODEL:3 * D_MODEL]
    out = flat(x_ref) + gate * y
    y_ref[...] = _rmsnorm(out, fg_ref[...]).reshape(y_ref.shape)


def _back_call(packed, kctx, gb, x, mod, mod_index, wuv2, wo_bf, final_g, rows_per_step):
    batch, seq, _ = x.shape
    tq = min(TOKEN_TILE, seq)
    has_ctx = kctx is not None
    rows = rows_per_step
    assert batch % rows == 0 and (rows == 1 or seq == tq)
    full = lambda shape: pl.BlockSpec(shape, lambda b, i: (0,) * len(shape))
    keys = lambda arr: pl.BlockSpec((rows, arr.shape[1], arr.shape[2]), lambda b, i: (b, 0, 0))
    in_specs = [
        pl.BlockSpec((rows, tq, MLA_HEADS * QK_WIDTH),
                     lambda b, i: (b, i, PK_Q // (MLA_HEADS * QK_WIDTH))),
        pl.BlockSpec((rows, seq, QK_WIDTH), lambda b, i: (b, 0, PK_K // QK_WIDTH)),
    ]
    args = [packed, packed]
    if has_ctx:
        in_specs += [keys(kctx)]
        args += [kctx]
    in_specs += [
        pl.BlockSpec((rows, tq, A_WIDTH), lambda b, i: (b, i, PK_A // A_WIDTH)),
        pl.BlockSpec((rows, tq, B_WIDTH), lambda b, i: (b, i, 0)),
        pl.BlockSpec((rows, tq, D_MODEL), lambda b, i: (b, i, 0)),
        pl.BlockSpec((SUBLANES, 3 * D_MODEL), lambda b, i: (mod_index(b * rows), 0)),
        full((MLA_HEADS // 2, 2 * KV_RANK, 2 * V_HEAD)),
        full((D_MODEL, D_MODEL)),
        full((1, D_MODEL)),
    ]
    args += [packed, gb, x, mod, wuv2, wo_bf, final_g]
    return pl.pallas_call(
        functools.partial(_back_body, has_ctx=has_ctx),
        grid=(batch // rows, seq // tq),
        in_specs=in_specs,
        out_specs=pl.BlockSpec((rows, tq, D_MODEL), lambda b, i: (b, i, 0)),
        out_shape=jax.ShapeDtypeStruct((batch, seq, D_MODEL), _F32),
        scratch_shapes=[pltpu.VMEM((rows * MLA_HEADS, QK_WIDTH, tq), _BF16)],
        compiler_params=pltpu.CompilerParams(
            dimension_semantics=("arbitrary", "arbitrary"),
            vmem_limit_bytes=VMEM_LIMIT_BYTES),
        name="back_lat" if has_ctx else "back_ctx",
    )(*args)


def _rope_tables(n_tokens, rotate):
    if rotate:
        rows = n_tokens // GRID_W
        inv = ROPE_THETA ** (-jnp.arange(AXIS_PAIRS, dtype=_F32) / AXIS_PAIRS)
        ang_r = jnp.arange(rows, dtype=_F32)[:, None] * inv
        ang_c = jnp.arange(GRID_W, dtype=_F32)[:, None] * inv

        def grid_table(fn):
            tr = jnp.broadcast_to(fn(ang_r)[:, None, :], (rows, GRID_W, AXIS_PAIRS))
            tc = jnp.broadcast_to(fn(ang_c)[None, :, :], (rows, GRID_W, AXIS_PAIRS))
            half = jnp.concatenate([tr, tc], axis=-1).reshape(n_tokens, 2 * AXIS_PAIRS)
            return jnp.concatenate([half, half], axis=-1)

        cos, sin = grid_table(jnp.cos), grid_table(jnp.sin)
    else:
        cos, sin = jnp.ones((n_tokens, QK_ROPE), _F32), jnp.zeros((n_tokens, QK_ROPE), _F32)
    return jnp.concatenate([cos, sin], axis=-1)


def kernel(x_prompt, x_sample, cache_ckv, cache_krope, c, c_ctx, norm_g, w_ada, b_ada, w_in, w_s,
           b_s, g_v, q_norm_g, w_uq, kv_norm_g, w_ukv, w_o, final_g):
    depth = norm_g.shape[0]
    assert depth == 1 and w_in.shape[2] == IN_WIDTH
    dec_batch = x_sample.shape[0]
    xp, xs = x_prompt, x_sample
    new_ckv, new_kr = [], []
    for l in range(depth):
        cond = jnp.concatenate(
            [c, c_ctx[None, :], jnp.zeros((16 - dec_batch - 1, D_MODEL), _F32)], axis=0)
        mod, win2, wq2, wuv2 = _weights_call(cond, w_ada[l], b_ada[l], jnp.transpose(w_in[l]),
                                             w_uq[l], w_ukv[l])
        wo_bf = w_o[l].astype(_BF16)
        ws_bf = w_s[l].astype(_BF16)
        bs_t = b_s[l].T
        gv_row = g_v[l].reshape(1, A_WIDTH)
        shared = (norm_g[l].reshape(1, D_MODEL), win2, ws_bf, bs_t, gv_row,
                  q_norm_g[l].reshape(1, Q_RANK), wq2, kv_norm_g[l].reshape(1, KV_RANK))
        fg = final_g.reshape(1, D_MODEL)

        ctx_index = lambda b: dec_batch
        ctx_batch, ctx_seq, _ = xp.shape
        pack = 2 * SUB_TILE // ctx_seq
        pk_c, gb_c, ckv_c, kr_c = _front_call(
            xp.reshape(ctx_batch // pack, pack * ctx_seq, D_MODEL), mod, ctx_index, *shared,
            _rope_tables(ctx_seq, False), True)
        unpack = lambda arr: arr.reshape(ctx_batch, ctx_seq, arr.shape[-1])
        xp = _back_call(unpack(pk_c), None, unpack(gb_c), xp, mod, ctx_index, wuv2, wo_bf, fg,
                        rows_per_step=pack)
        new_ckv.append(ckv_c.reshape(ctx_batch, 1, ctx_seq, KV_RANK))
        assert ctx_seq == SUB_TILE
        new_kr.append(jnp.swapaxes(kr_c, 2, 3))

        lat_index = lambda b: b
        pk_s, gb_s = _front_call(
            xs, mod, lat_index, *shared, _rope_tables(xs.shape[1], True), False)
        cache_k = jnp.concatenate(
            [cache_ckv[:, l], jnp.zeros(cache_krope[:, l].shape, _F32), cache_krope[:, l]],
            axis=-1).astype(_BF16)
        xs = _back_call(pk_s, cache_k, gb_s, xs, mod, lat_index, wuv2, wo_bf, fg, rows_per_step=1)
    return (xp, xs, jnp.concatenate(new_ckv, axis=1), jnp.concatenate(new_kr, axis=1))
```

```python
import functools
import math

import jax
import jax.numpy as jnp
from jax import lax
from jax.experimental import pallas as pl
from jax.experimental.pallas import tpu as pltpu

D_MODEL = 1024
GRID_W = 64
EPS = 1e-6
A_HEADS = 4
A_HEAD_DIM = 128
A_WIDTH = A_HEADS * A_HEAD_DIM
CHUNK = 128
MLA_HEADS = 4
QK_NOPE = 128
QK_ROPE = 64
V_HEAD = 128
B_WIDTH = MLA_HEADS * V_HEAD
Q_RANK = 256
KV_RANK = 128
AXIS_PAIRS = QK_ROPE // 4
ROPE_THETA = 10000.0
ATTN_SCALE = 1.0 / math.sqrt(QK_NOPE + QK_ROPE)
LOG2_E = 1.4426950408889634

QK_WIDTH = KV_RANK + 2 * QK_ROPE
OFF_U = 0
OFF_V = OFF_U + A_WIDTH
OFF_GA = OFF_V + A_WIDTH
OFF_CQ = OFF_GA + A_WIDTH
OFF_CKV = OFF_CQ + Q_RANK
OFF_KR = OFF_CKV + KV_RANK
OFF_GB = OFF_KR + 2 * QK_ROPE
IN_WIDTH = OFF_GB - QK_ROPE + B_WIDTH
IN_WIDTH2 = OFF_GB + B_WIDTH

PK_A = 0
PK_K = PK_A + A_WIDTH
PK_V = PK_K + QK_WIDTH
PK_Q = PK_V + 2 * KV_RANK
PK_WIDTH = PK_Q + MLA_HEADS * QK_WIDTH

LANES = 128
SUBLANES = 8
ONES_ROWS = 16
SUB_TILE = 256
TOKEN_TILE = 512
PROJ_BLOCK = 256
OUT_ROWS = 256
KEY_CHUNK = 256
SCORE_LOOKAHEAD = 3
VMEM_LIMIT_BYTES = 56 * 1024 * 1024

_F32 = jnp.float32
_BF16 = jnp.bfloat16
_NT_DIMS = (((1,), (1,)), ((), ()))


def _silu(x):
    hx = 0.5 * x
    return hx + hx * jnp.tanh(hx)


def _gelu_tanh(x):
    return x * (0.5 * (1.0 + jnp.tanh(0.7978845608028654 * (x + 0.044715 * (x * x * x)))))


def _rmsnorm(x, g):
    ms = jnp.mean(x * x, axis=-1, keepdims=True)
    return (x * lax.rsqrt(ms + EPS)) * g


def _mod_body(cond_ref, wa_ref, wb_ref, b_ref, o_ref):
    s = _silu(cond_ref[...]).astype(_BF16)
    halves = [jnp.dot(s, w_ref[...].astype(_BF16), preferred_element_type=_F32)
              for w_ref in (wa_ref, wb_ref)]
    mod = jnp.concatenate(halves, axis=1) + b_ref[...]
    rows, cols = mod.shape
    o_ref[...] = jnp.broadcast_to(mod[:, None, :], (rows, SUBLANES, cols)).reshape(
        rows * SUBLANES, cols)


def _fold_body(wq_ref, wkv_ref, q_ref, uv_ref):
    half, quarter = LANES // 2, LANES // 4
    assert QK_NOPE == LANES and QK_ROPE == half and V_HEAD == LANES
    lane = lax.broadcasted_iota(jnp.int32, (Q_RANK, LANES), 1)
    low = lane < half

    def cols(block):
        return wq_ref[:, block * LANES:(block + 1) * LANES]

    for hd in range(MLA_HEADS):
        start = hd * (QK_NOPE + QK_ROPE)
        b0, odd = divmod(start, LANES)
        if odd == 0:
            nope = cols(b0)
            rope_blk = cols(b0 + 1)
            rope_lo = rope_blk
            rope_hi = pltpu.roll(rope_blk, half, 1)
        else:
            assert odd == half
            nope = jnp.where(low, pltpu.roll(cols(b0), half, 1), pltpu.roll(cols(b0 + 1), half, 1))
            rope_blk = cols(b0 + 1)
            rope_lo = pltpu.roll(rope_blk, half, 1)
            rope_hi = rope_blk
        partner = jnp.where(lane < half + quarter,
                            -pltpu.roll(rope_hi, LANES - quarter, 1),
                            pltpu.roll(rope_hi, quarter, 1))
        w_uk = wkv_ref[:, hd * (QK_NOPE + V_HEAD):hd * (QK_NOPE + V_HEAD) + QK_NOPE]
        absorbed = lax.dot_general(nope, w_uk, _NT_DIMS, precision=lax.Precision.HIGHEST,
                                   preferred_element_type=_F32)
        base = hd * QK_WIDTH
        q_ref[:, base:base + KV_RANK] = absorbed.astype(_BF16)
        q_ref[:, base + KV_RANK:base + QK_WIDTH] = jnp.where(low, rope_lo, partner).astype(_BF16)

    uv_ref[...] = jnp.zeros(uv_ref.shape, _BF16)
    for hd in range(MLA_HEADS):
        w_uv = wkv_ref[:, hd * (QK_NOPE + V_HEAD) + QK_NOPE:(hd + 1) * (QK_NOPE + V_HEAD)]
        pair, pos = divmod(hd, 2)
        uv_ref[pair, pos * KV_RANK:(pos + 1) * KV_RANK, pos * V_HEAD:(pos + 1) * V_HEAD] = (
            w_uv.astype(_BF16))


_KR_BLOCK = OFF_KR // LANES


_PREP_SPLIT = IN_WIDTH2 // 2


def _prep_body(step, w_ref, o_ref):
    half = QK_ROPE // 2
    blocks_per_step = _PREP_SPLIT // LANES
    base = step * _PREP_SPLIT
    for j in range(step * blocks_per_step, (step + 1) * blocks_per_step):
        if j < _KR_BLOCK:
            blk = w_ref[j * LANES - base:(j + 1) * LANES - base, :]
        elif j == _KR_BLOCK:
            kr = w_ref[OFF_KR - base:OFF_KR - base + QK_ROPE, :]
            partner = jnp.concatenate([-kr[half:, :], kr[:half, :]], axis=0)
            blk = jnp.concatenate([kr, partner], axis=0)
        else:
            lo = j * LANES - QK_ROPE - base
            blk = w_ref[lo:lo + LANES, :]
        o_ref[:, j * LANES - base:(j + 1) * LANES - base] = blk.T.astype(_BF16)


def _weights_body(cond_ref, wa_ref, wb_ref, b_ref, win_ref, wq_ref, wkv_ref,
                  mod_ref, win2_ref, q_ref, uv_ref):
    j = pl.program_id(0)
    _mod_body(cond_ref, wa_ref, wb_ref, b_ref, mod_ref)
    n_prep = IN_WIDTH2 // _PREP_SPLIT
    for step in range(n_prep):
        pl.when(j == step)(functools.partial(_prep_body, step, win_ref, win2_ref))
    pl.when(j == n_prep)(functools.partial(_fold_body, wq_ref, wkv_ref, q_ref, uv_ref))


def _weights_call(cond, w_ada, b_ada, w_in_t, w_uq, w_ukv):
    rows = cond.shape[0]
    n_out = w_ada.shape[1]
    col_block = D_MODEL
    n_steps = n_out // col_block
    n_prep = IN_WIDTH2 // _PREP_SPLIT
    assert _KR_BLOCK * LANES == OFF_KR and OFF_KR >= _PREP_SPLIT and n_steps == n_prep + 1
    prep_step = lambda j: jnp.minimum(j, n_prep - 1)
    full = lambda shape: pl.BlockSpec(shape, lambda j: (0,) * len(shape))
    uv_shape = (MLA_HEADS // 2, 2 * KV_RANK, 2 * V_HEAD)
    return pl.pallas_call(
        _weights_body,
        grid=(n_steps,),
        in_specs=[
            full((rows, D_MODEL)),
            pl.BlockSpec((D_MODEL, col_block // 2), lambda j: (0, 2 * j)),
            pl.BlockSpec((D_MODEL, col_block // 2), lambda j: (0, 2 * j + 1)),
            pl.BlockSpec((1, col_block), lambda j: (0, j)),
            pl.BlockSpec((_PREP_SPLIT, D_MODEL), lambda j: (prep_step(j), 0)),
            full(w_uq.shape),
            full(w_ukv.shape),
        ],
        out_specs=[
            pl.BlockSpec((rows * SUBLANES, col_block), lambda j: (0, j)),
            pl.BlockSpec((D_MODEL, _PREP_SPLIT), lambda j: (0, prep_step(j))),
            full((Q_RANK, MLA_HEADS * QK_WIDTH)),
            full(uv_shape),
        ],
        out_shape=[
            jax.ShapeDtypeStruct((rows * SUBLANES, n_out), _F32),
            jax.ShapeDtypeStruct((D_MODEL, IN_WIDTH2), _BF16),
            jax.ShapeDtypeStruct((Q_RANK, MLA_HEADS * QK_WIDTH), _BF16),
            jax.ShapeDtypeStruct(uv_shape, _BF16),
        ],
        compiler_params=pltpu.CompilerParams(
            dimension_semantics=("arbitrary",), vmem_limit_bytes=VMEM_LIMIT_BYTES),
        name="weights",
    )(cond, w_ada, w_ada, b_ada.reshape(1, n_out), w_in_t, w_uq, w_ukv)


def _front_substep(x_ref, mod_ref, ng_ref, win_ref, ws_ref, bs_ref, gv_ref, qg_ref, wq_ref,
                   kvg_ref, tables, out_refs, row0, h_write, h_read, z_write, z_read, emit_cache):
    if emit_cache:
        pk_ref, gb_ref, ckv_ref, kr_ref = out_refs
    else:
        pk_ref, gb_ref = out_refs
    first_half = lax.broadcasted_iota(jnp.int32, tables.shape, 1) < QK_ROPE
    t1 = jnp.where(first_half, tables, 0.0)
    t2k = jnp.where(first_half, pltpu.roll(tables, QK_ROPE, 1), 0.0)
    t2q = jnp.where(first_half, t2k, 1.0)
    rows = slice(row0, row0 + SUB_TILE)
    half = SUB_TILE // 2
    assert IN_WIDTH2 == 10 * PROJ_BLOCK

    def project(j):
        lo = j * PROJ_BLOCK
        z_write[:, lo:lo + PROJ_BLOCK] = jnp.dot(h_read[...], win_ref[:, lo:lo + PROJ_BLOCK],
                                                 preferred_element_type=_F32)

    shift = mod_ref[0:1, 0:D_MODEL]
    gain = ng_ref[...] * (1.0 + mod_ref[0:1, D_MODEL:2 * D_MODEL])

    def norm_rows(lo):
        x = x_ref[0, row0 + lo:row0 + lo + half, :]
        ms = jnp.mean(x * x, axis=-1, keepdims=True)
        h_write[lo:lo + half, :] = ((x * lax.rsqrt(ms + EPS)) * gain + shift).astype(_BF16)

    def vn_head(hd):
        lo = hd * A_HEAD_DIM
        v = _gelu_tanh(z_read[:, OFF_V + lo:OFF_V + lo + A_HEAD_DIM])
        return _rmsnorm(v, gv_ref[:, lo:lo + A_HEAD_DIM]).astype(_BF16)

    def mix_head(hd, vn):
        bias = bs_ref[:, hd:hd + 1]
        parts = [jnp.dot(ws_ref[hd], vn[c * CHUNK:(c + 1) * CHUNK, :],
                         preferred_element_type=_F32) + bias for c in range(SUB_TILE // CHUNK)]
        return jnp.concatenate(parts, axis=0)

    def gate_a(hd, mixed):
        lo = hd * A_HEAD_DIM
        u = _gelu_tanh(z_read[:, OFF_U + lo:OFF_U + lo + A_HEAD_DIM])
        g = _silu(z_read[:, OFF_GA + lo:OFF_GA + lo + A_HEAD_DIM])
        pk_ref[0, rows, PK_A + lo:PK_A + lo + A_HEAD_DIM] = (u * mixed * g).astype(_BF16)

    def gate_b(lo, width):
        gb_ref[0, rows, lo:lo + width] = _silu(z_read[:, OFF_GB + lo:OFF_GB + lo + width])

    def rope_q(qa, hd):
        base = hd * QK_WIDTH
        pair = qa[:, base + KV_RANK:base + QK_WIDTH]
        rope = pair * t1 + pltpu.roll(pair, QK_ROPE, 1) * t2q
        pk_ref[0, rows, PK_Q + base:PK_Q + base + QK_WIDTH] = jnp.concatenate(
            [qa[:, base:base + KV_RANK], rope], axis=1).astype(_BF16)

    project(0)
    cq = _rmsnorm(z_read[:, OFF_CQ:OFF_CQ + Q_RANK],
                  qg_ref[...] * (ATTN_SCALE * LOG2_E)).astype(_BF16)
    vn0 = vn_head(0)
    project(1)
    vn1 = vn_head(1)
    gate_b(0, B_WIDTH // 2)
    project(2)
    vn2 = vn_head(2)
    gate_b(B_WIDTH // 2, B_WIDTH // 2)
    project(3)
    vn3 = vn_head(3)
    ckv = _rmsnorm(z_read[:, OFF_CKV:OFF_CKV + KV_RANK], kvg_ref[...])
    kpair = z_read[:, OFF_KR:OFF_KR + 2 * QK_ROPE]
    krot = kpair * t1 + pltpu.roll(kpair, QK_ROPE, 1) * t2k
    pk_ref[0, rows, PK_K:PK_K + QK_WIDTH] = jnp.concatenate([ckv, krot], axis=1).astype(_BF16)
    pk_ref[0, rows, PK_V:PK_V + 2 * KV_RANK] = jnp.concatenate(
        [ckv, jnp.ones_like(ckv)], axis=1).astype(_BF16)
    if emit_cache:
        ckv_ref[0, 0, rows, :] = ckv
        kr_ref[row0 // SUB_TILE, 0] = kpair.T[0:QK_ROPE, :]
    qa = jnp.dot(cq, wq_ref[...], preferred_element_type=_F32)
    norm_rows(0)
    project(4)
    rope_q(qa, 0)
    rope_q(qa, 1)
    mixed0 = mix_head(0, vn0)
    mixed1 = mix_head(1, vn1)
    rope_q(qa, 2)
    rope_q(qa, 3)
    project(5)
    gate_a(0, mixed0)
    mixed2 = mix_head(2, vn2)
    mixed3 = mix_head(3, vn3)
    project(6)
    gate_a(1, mixed1)
    norm_rows(half)
    project(7)
    gate_a(2, mixed2)
    project(8)
    gate_a(3, mixed3)
    project(9)


def _front_body(x_ref, mod_ref, ng_ref, win_ref, ws_ref, bs_ref, gv_ref, qg_ref, wq_ref,
                kvg_ref, tab_ref, *rest, emit_cache, tiles_per_row):
    out_refs, (h_a, h_b, z_a, z_b) = rest[:-4], rest[-4:]
    t = pl.program_id(0)
    tile = x_ref.shape[1]
    assert tile == 2 * SUB_TILE
    n_pos = tab_ref.shape[0]

    @pl.when(t == 0)
    def _():
        h_b[...] = jnp.zeros(h_b.shape, _BF16)
        z_b[...] = jnp.zeros(z_b.shape, _F32)

    pos_base = (jnp.maximum(t - 1, 0) % tiles_per_row) * tile
    for sub, (h_write, h_read, z_write, z_read) in enumerate(
            ((h_a, h_b, z_a, z_b), (h_b, h_a, z_b, z_a))):
        row0 = sub * SUB_TILE
        pos = pl.multiple_of((pos_base + row0) % n_pos, SUB_TILE)
        tables = tab_ref[pl.ds(pos, SUB_TILE), :]
        _front_substep(x_ref, mod_ref, ng_ref, win_ref, ws_ref, bs_ref, gv_ref, qg_ref, wq_ref,
                       kvg_ref, tables, out_refs, row0, h_write, h_read, z_write, z_read,
                       emit_cache)


def _front_call(x, mod, mod_index, norm_g, win2, ws_bf, bs_t, g_v, q_norm_g, wq2, kv_norm_g,
                tables, emit_cache):
    batch, seq, _ = x.shape
    tile = 2 * SUB_TILE
    tiles_per_row = seq // tile
    n_tiles = batch * tiles_per_row
    assert seq % tile == 0 and tables.shape[0] % SUB_TILE == 0

    def in_tile(t):
        tt = jnp.minimum(t, n_tiles - 1)
        return tt // tiles_per_row, tt % tiles_per_row

    def out_tile(t):
        tt = jnp.maximum(t - 1, 0)
        return tt // tiles_per_row, tt % tiles_per_row

    full = lambda shape: pl.BlockSpec(shape, lambda t: (0,) * len(shape))
    in_specs = [
        pl.BlockSpec((1, tile, D_MODEL), lambda t: (*in_tile(t), 0)),
        pl.BlockSpec((SUBLANES, 3 * D_MODEL), lambda t: (mod_index(in_tile(t)[0]), 0)),
        full((1, D_MODEL)),
        full((D_MODEL, IN_WIDTH2)),
        full((A_HEADS, CHUNK, CHUNK)),
        full((CHUNK, A_HEADS)),
        full((1, A_WIDTH)),
        full((1, Q_RANK)),
        full((Q_RANK, MLA_HEADS * QK_WIDTH)),
        full((1, KV_RANK)),
        full(tables.shape),
    ]
    out_shape = [
        jax.ShapeDtypeStruct((batch, seq, PK_WIDTH), _BF16),
        jax.ShapeDtypeStruct((batch, seq, B_WIDTH), _F32),
    ]
    out_specs = [
        pl.BlockSpec((1, tile, PK_WIDTH), lambda t: (*out_tile(t), 0)),
        pl.BlockSpec((1, tile, B_WIDTH), lambda t: (*out_tile(t), 0)),
    ]
    if emit_cache:
        out_shape += [
            jax.ShapeDtypeStruct((batch, 1, seq, KV_RANK), _F32),
            jax.ShapeDtypeStruct((n_tiles * 2, 1, QK_ROPE, SUB_TILE), _F32),
        ]
        out_specs += [
            pl.BlockSpec((1, 1, tile, KV_RANK), lambda t: (out_tile(t)[0], 0, out_tile(t)[1], 0)),
            pl.BlockSpec((2, 1, QK_ROPE, SUB_TILE), lambda t: (jnp.maximum(t - 1, 0), 0, 0, 0)),
        ]
    return pl.pallas_call(
        functools.partial(_front_body, emit_cache=emit_cache, tiles_per_row=tiles_per_row),
        grid=(n_tiles + 1,),
        in_specs=in_specs,
        out_specs=out_specs,
        out_shape=out_shape,
        scratch_shapes=[pltpu.VMEM((SUB_TILE, D_MODEL), _BF16), pltpu.VMEM((SUB_TILE, D_MODEL), _BF16),
                        pltpu.VMEM((SUB_TILE, IN_WIDTH2), _F32),
                        pltpu.VMEM((SUB_TILE, IN_WIDTH2), _F32)],
        compiler_params=pltpu.CompilerParams(
            dimension_semantics=("arbitrary",),
            vmem_limit_bytes=VMEM_LIMIT_BYTES),
        name="front_ctx" if emit_cache else "front_lat",
    )(x, mod, norm_g, win2, ws_bf, bs_t, g_v, q_norm_g, wq2, kv_norm_g, tables)


def _key_chunks(k_refs):
    chunks = []
    for k_ref in k_refs:
        size = min(KEY_CHUNK, k_ref.shape[1])
        assert k_ref.shape[1] % size == 0
        chunks += [(k_ref, lo, size) for lo in range(0, k_ref.shape[1], size)]
    return chunks


def _values_t(k_ref, row):
    ckv_t = k_ref[row, :, 0:KV_RANK].T
    return jnp.concatenate([ckv_t, jnp.ones((ONES_ROWS, ckv_t.shape[1]), _BF16)], axis=0)


def _attention(q_ref, k_refs, qt_ref, fillers):
    n_rows = q_ref.shape[0]
    chunks = _key_chunks(k_refs)
    steps = [(row, hd, c) for row in range(n_rows) for hd in range(MLA_HEADS)
             for c in range(len(chunks))]

    for row in range(n_rows):
        for hd in range(MLA_HEADS):
            qt_ref[row * MLA_HEADS + hd] = q_ref[row, :, hd * QK_WIDTH:(hd + 1) * QK_WIDTH].T

    def logits(row, hd, c):
        k_ref, lo, size = chunks[c]
        return jnp.dot(k_ref[row, lo:lo + size, :], qt_ref[row * MLA_HEADS + hd],
                       preferred_element_type=_F32)

    pending = [logits(*step) for step in steps[:SCORE_LOOKAHEAD]]
    values_t = [{id(k_ref): _values_t(k_ref, row) for k_ref in k_refs} for row in range(n_rows)]
    outs = {}
    m = acc = None
    for j, (row, hd, c) in enumerate(steps):
        if j + SCORE_LOOKAHEAD < len(steps):
            pending.append(logits(*steps[j + SCORE_LOOKAHEAD]))
        s, pending[j] = pending[j], None
        k_ref, lo, size = chunks[c]
        chunk_max = jnp.max(s, axis=0, keepdims=True)
        if c > 0:
            m_new = jnp.maximum(m, chunk_max)
            acc = acc * jnp.exp2(m - m_new)
            m = m_new
        else:
            m = chunk_max
        p = jnp.exp2(s - m).astype(_BF16)
        part = jnp.dot(values_t[row][id(k_ref)][:, lo:lo + size], p, preferred_element_type=_F32)
        acc = acc + part if c > 0 else part
        if j >= len(steps) - len(fillers):
            fillers[j - len(steps) + len(fillers)]()
        if c == len(chunks) - 1:
            outs[row, hd] = (acc[0:KV_RANK, :] / acc[KV_RANK:KV_RANK + 1, :]).T
    return outs


def _back_body(*refs, has_ctx):
    if has_ctx:
        (q_ref, klat_ref, kctx_ref, a_ref, gb_ref, x_ref, mod_ref, wuv_ref, wo_ref, fg_ref,
         y_ref, qt_ref, ya_ref) = refs
        k_refs = (kctx_ref, klat_ref)
    else:
        (q_ref, klat_ref, a_ref, gb_ref, x_ref, mod_ref, wuv_ref, wo_ref, fg_ref, y_ref,
         qt_ref, ya_ref) = refs
        k_refs = (klat_ref,)
    n_rows, tq = q_ref.shape[0], q_ref.shape[1]

    block = min(OUT_ROWS, tq)
    blocks = [(row, lo) for row in range(n_rows) for lo in range(0, tq, block)]

    def a_part(row, lo):
        def run():
            ya_ref[row, lo:lo + block, :] = jnp.dot(
                a_ref[row, lo:lo + block, :], wo_ref[0:A_WIDTH, :], preferred_element_type=_F32)
        return run

    outs = _attention(q_ref, k_refs, qt_ref, [a_part(*b) for b in blocks])

    gate = mod_ref[0:1, 2 * D_MODEL:3 * D_MODEL]
    for row, lo in blocks:
        rows = slice(lo, lo + block)
        attn_cols = []
        for pair in range(MLA_HEADS // 2):
            o2 = jnp.concatenate([outs[row, 2 * pair][rows, :],
                                  outs[row, 2 * pair + 1][rows, :]], axis=1)
            attn_cols.append(jnp.dot(o2.astype(_BF16), wuv_ref[pair],
                                     preferred_element_type=_F32))
        attn = jnp.concatenate(attn_cols, axis=1) * gb_ref[row, rows, :]
        y = ya_ref[row, rows, :] + jnp.dot(
            attn.astype(_BF16), wo_ref[A_WIDTH:A_WIDTH + B_WIDTH, :], preferred_element_type=_F32)
        out = x_ref[row, rows, :] + gate * y
        y_ref[row, rows, :] = _rmsnorm(out, fg_ref[...])


def _back_call(packed, kctx, gb, x, mod, mod_index, wuv2, wo_bf, final_g, rows_per_step):
    batch, seq, _ = x.shape
    tq = min(TOKEN_TILE, seq)
    has_ctx = kctx is not None
    rows = rows_per_step
    assert batch % rows == 0 and (rows == 1 or seq == tq)
    full = lambda shape: pl.BlockSpec(shape, lambda b, i: (0,) * len(shape))
    keys = lambda arr: pl.BlockSpec((rows, arr.shape[1], arr.shape[2]), lambda b, i: (b, 0, 0))
    in_specs = [
        pl.BlockSpec((rows, tq, MLA_HEADS * QK_WIDTH),
                     lambda b, i: (b, i, PK_Q // (MLA_HEADS * QK_WIDTH))),
        pl.BlockSpec((rows, seq, QK_WIDTH), lambda b, i: (b, 0, PK_K // QK_WIDTH)),
    ]
    args = [packed, packed]
    if has_ctx:
        in_specs += [keys(kctx)]
        args += [kctx]
    in_specs += [
        pl.BlockSpec((rows, tq, A_WIDTH), lambda b, i: (b, i, PK_A // A_WIDTH)),
        pl.BlockSpec((rows, tq, B_WIDTH), lambda b, i: (b, i, 0)),
        pl.BlockSpec((rows, tq, D_MODEL), lambda b, i: (b, i, 0)),
        pl.BlockSpec((SUBLANES, 3 * D_MODEL), lambda b, i: (mod_index(b * rows), 0)),
        full((MLA_HEADS // 2, 2 * KV_RANK, 2 * V_HEAD)),
        full((D_MODEL, D_MODEL)),
        full((1, D_MODEL)),
    ]
    args += [packed, gb, x, mod, wuv2, wo_bf, final_g]
    return pl.pallas_call(
        functools.partial(_back_body, has_ctx=has_ctx),
        grid=(batch // rows, seq // tq),
        in_specs=in_specs,
        out_specs=pl.BlockSpec((rows, tq, D_MODEL), lambda b, i: (b, i, 0)),
        out_shape=jax.ShapeDtypeStruct((batch, seq, D_MODEL), _F32),
        scratch_shapes=[
            pltpu.VMEM((rows * MLA_HEADS, QK_WIDTH, tq), _BF16),
            pltpu.VMEM((rows, tq, D_MODEL), _F32),
        ],
        compiler_params=pltpu.CompilerParams(
            dimension_semantics=("arbitrary", "arbitrary"),
            vmem_limit_bytes=VMEM_LIMIT_BYTES),
        name="back_lat" if has_ctx else "back_ctx",
    )(*args)


def _rope_tables(n_tokens, rotate):
    if rotate:
        rows = n_tokens // GRID_W
        inv = ROPE_THETA ** (-jnp.arange(AXIS_PAIRS, dtype=_F32) / AXIS_PAIRS)
        ang_r = jnp.arange(rows, dtype=_F32)[:, None] * inv
        ang_c = jnp.arange(GRID_W, dtype=_F32)[:, None] * inv

        def grid_table(fn):
            tr = jnp.broadcast_to(fn(ang_r)[:, None, :], (rows, GRID_W, AXIS_PAIRS))
            tc = jnp.broadcast_to(fn(ang_c)[None, :, :], (rows, GRID_W, AXIS_PAIRS))
            half = jnp.concatenate([tr, tc], axis=-1).reshape(n_tokens, 2 * AXIS_PAIRS)
            return jnp.concatenate([half, half], axis=-1)

        cos, sin = grid_table(jnp.cos), grid_table(jnp.sin)
    else:
        cos, sin = jnp.ones((n_tokens, QK_ROPE), _F32), jnp.zeros((n_tokens, QK_ROPE), _F32)
    return jnp.concatenate([cos, sin], axis=-1)


def kernel(x_prompt, x_sample, cache_ckv, cache_krope, c, c_ctx, norm_g, w_ada, b_ada, w_in, w_s,
           b_s, g_v, q_norm_g, w_uq, kv_norm_g, w_ukv, w_o, final_g):
    depth = norm_g.shape[0]
    assert depth == 1 and w_in.shape[2] == IN_WIDTH
    dec_batch = x_sample.shape[0]
    xp, xs = x_prompt, x_sample
    new_ckv, new_kr = [], []
    for l in range(depth):
        cond = jnp.concatenate(
            [c, c_ctx[None, :], jnp.zeros((16 - dec_batch - 1, D_MODEL), _F32)], axis=0)
        mod, win2, wq2, wuv2 = _weights_call(cond, w_ada[l], b_ada[l], jnp.transpose(w_in[l]),
                                             w_uq[l], w_ukv[l])
        wo_bf = w_o[l].astype(_BF16)
        ws_bf = w_s[l].astype(_BF16)
        bs_t = b_s[l].T
        gv_row = g_v[l].reshape(1, A_WIDTH)
        shared = (norm_g[l].reshape(1, D_MODEL), win2, ws_bf, bs_t, gv_row,
                  q_norm_g[l].reshape(1, Q_RANK), wq2, kv_norm_g[l].reshape(1, KV_RANK))
        fg = final_g.reshape(1, D_MODEL)

        ctx_index = lambda b: dec_batch
        ctx_batch, ctx_seq, _ = xp.shape
        pack = 2 * SUB_TILE // ctx_seq
        pk_c, gb_c, ckv_c, kr_c = _front_call(
            xp.reshape(ctx_batch // pack, pack * ctx_seq, D_MODEL), mod, ctx_index, *shared,
            _rope_tables(ctx_seq, False), True)
        unpack = lambda arr: arr.reshape(ctx_batch, ctx_seq, arr.shape[-1])
        xp = _back_call(unpack(pk_c), None, unpack(gb_c), xp, mod, ctx_index, wuv2, wo_bf, fg,
                        rows_per_step=pack)
        new_ckv.append(ckv_c.reshape(ctx_batch, 1, ctx_seq, KV_RANK))
        assert ctx_seq == SUB_TILE
        new_kr.append(jnp.swapaxes(kr_c, 2, 3))

        lat_index = lambda b: b
        pk_s, gb_s = _front_call(
            xs, mod, lat_index, *shared, _rope_tables(xs.shape[1], True), False)
        cache_k = jnp.concatenate(
            [cache_ckv[:, l], jnp.zeros(cache_krope[:, l].shape, _F32), cache_krope[:, l]],
            axis=-1).astype(_BF16)
        xs = _back_call(pk_s, cache_k, gb_s, xs, mod, lat_index, wuv2, wo_bf, fg, rows_per_step=1)
    return (xp, xs, jnp.concatenate(new_ckv, axis=1), jnp.concatenate(new_kr, axis=1))
```

```python
import functools
import math

import jax
import jax.numpy as jnp
from jax import lax
from jax.experimental import pallas as pl
from jax.experimental.pallas import tpu as pltpu

D_MODEL = 1024
GRID_W = 64
EPS = 1e-6
A_HEADS = 4
A_HEAD_DIM = 128
A_WIDTH = A_HEADS * A_HEAD_DIM
CHUNK = 128
MLA_HEADS = 4
QK_NOPE = 128
QK_ROPE = 64
V_HEAD = 128
B_WIDTH = MLA_HEADS * V_HEAD
Q_RANK = 256
KV_RANK = 128
AXIS_PAIRS = QK_ROPE // 4
ROPE_THETA = 10000.0
ATTN_SCALE = 1.0 / math.sqrt(QK_NOPE + QK_ROPE)
LOG2_E = 1.4426950408889634

QK_WIDTH = KV_RANK + 2 * QK_ROPE
OFF_U = 0
OFF_V = OFF_U + A_WIDTH
OFF_GA = OFF_V + A_WIDTH
OFF_CQ = OFF_GA + A_WIDTH
OFF_CKV = OFF_CQ + Q_RANK
OFF_KR = OFF_CKV + KV_RANK
OFF_GB = OFF_KR + 2 * QK_ROPE
IN_WIDTH = OFF_GB - QK_ROPE + B_WIDTH
IN_WIDTH2 = OFF_GB + B_WIDTH

PK_A = 0
PK_K = PK_A + A_WIDTH
PK_V = PK_K + QK_WIDTH
PK_Q = PK_V + 2 * KV_RANK
PK_WIDTH = PK_Q + MLA_HEADS * QK_WIDTH

LANES = 128
SUBLANES = 8
ONES_ROWS = 16
SUB_TILE = 256
TOKEN_TILE = 512
PROJ_BLOCK = 256
OUT_ROWS = 256
KEY_CHUNK = 256
SCORE_LOOKAHEAD = 3
VMEM_LIMIT_BYTES = 56 * 1024 * 1024

_F32 = jnp.float32
_BF16 = jnp.bfloat16
_NT_DIMS = (((1,), (1,)), ((), ()))


def _silu(x):
    hx = 0.5 * x
    return hx + hx * jnp.tanh(hx)


def _gelu_tanh(x):
    return x * (0.5 * (1.0 + jnp.tanh(0.7978845608028654 * (x + 0.044715 * (x * x * x)))))


def _rmsnorm(x, g):
    ms = jnp.mean(x * x, axis=-1, keepdims=True)
    return (x * lax.rsqrt(ms + EPS)) * g


def _mod_body(cond_ref, wa_ref, wb_ref, b_ref, o_ref):
    s = _silu(cond_ref[...]).astype(_BF16)
    halves = [jnp.dot(s, w_ref[...].astype(_BF16), preferred_element_type=_F32)
              for w_ref in (wa_ref, wb_ref)]
    mod = jnp.concatenate(halves, axis=1) + b_ref[...]
    rows, cols = mod.shape
    o_ref[...] = jnp.broadcast_to(mod[:, None, :], (rows, SUBLANES, cols)).reshape(
        rows * SUBLANES, cols)


def _fold_body(wq_ref, wkv_ref, q_ref, uv_ref):
    half, quarter = LANES // 2, LANES // 4
    assert QK_NOPE == LANES and QK_ROPE == half and V_HEAD == LANES
    lane = lax.broadcasted_iota(jnp.int32, (Q_RANK, LANES), 1)
    low = lane < half

    def cols(block):
        return wq_ref[:, block * LANES:(block + 1) * LANES]

    for hd in range(MLA_HEADS):
        start = hd * (QK_NOPE + QK_ROPE)
        b0, odd = divmod(start, LANES)
        if odd == 0:
            nope = cols(b0)
            rope_blk = cols(b0 + 1)
            rope_lo = rope_blk
            rope_hi = pltpu.roll(rope_blk, half, 1)
        else:
            assert odd == half
            nope = jnp.where(low, pltpu.roll(cols(b0), half, 1), pltpu.roll(cols(b0 + 1), half, 1))
            rope_blk = cols(b0 + 1)
            rope_lo = pltpu.roll(rope_blk, half, 1)
            rope_hi = rope_blk
        partner = jnp.where(lane < half + quarter,
                            -pltpu.roll(rope_hi, LANES - quarter, 1),
                            pltpu.roll(rope_hi, quarter, 1))
        w_uk = wkv_ref[:, hd * (QK_NOPE + V_HEAD):hd * (QK_NOPE + V_HEAD) + QK_NOPE]
        absorbed = lax.dot_general(nope, w_uk, _NT_DIMS, precision=lax.Precision.HIGHEST,
                                   preferred_element_type=_F32)
        base = hd * QK_WIDTH
        q_ref[:, base:base + KV_RANK] = absorbed.astype(_BF16)
        q_ref[:, base + KV_RANK:base + QK_WIDTH] = jnp.where(low, rope_lo, partner).astype(_BF16)

    uv_ref[...] = jnp.zeros(uv_ref.shape, _BF16)
    for hd in range(MLA_HEADS):
        w_uv = wkv_ref[:, hd * (QK_NOPE + V_HEAD) + QK_NOPE:(hd + 1) * (QK_NOPE + V_HEAD)]
        pair, pos = divmod(hd, 2)
        uv_ref[pair, pos * KV_RANK:(pos + 1) * KV_RANK, pos * V_HEAD:(pos + 1) * V_HEAD] = (
            w_uv.astype(_BF16))


_KR_BLOCK = OFF_KR // LANES


_PREP_SPLIT = IN_WIDTH2 // 2


def _prep_body(step, w_ref, o_ref):
    half = QK_ROPE // 2
    blocks_per_step = _PREP_SPLIT // LANES
    base = step * _PREP_SPLIT
    for j in range(step * blocks_per_step, (step + 1) * blocks_per_step):
        if j < _KR_BLOCK:
            blk = w_ref[j * LANES - base:(j + 1) * LANES - base, :]
        elif j == _KR_BLOCK:
            kr = w_ref[OFF_KR - base:OFF_KR - base + QK_ROPE, :]
            partner = jnp.concatenate([-kr[half:, :], kr[:half, :]], axis=0)
            blk = jnp.concatenate([kr, partner], axis=0)
        else:
            lo = j * LANES - QK_ROPE - base
            blk = w_ref[lo:lo + LANES, :]
        o_ref[:, j * LANES - base:(j + 1) * LANES - base] = blk.T.astype(_BF16)


def _weights_body(cond_ref, wa_ref, wb_ref, b_ref, win_ref, wq_ref, wkv_ref,
                  mod_ref, win2_ref, q_ref, uv_ref):
    j = pl.program_id(0)
    _mod_body(cond_ref, wa_ref, wb_ref, b_ref, mod_ref)
    n_prep = IN_WIDTH2 // _PREP_SPLIT
    for step in range(n_prep):
        pl.when(j == step)(functools.partial(_prep_body, step, win_ref, win2_ref))
    pl.when(j == n_prep)(functools.partial(_fold_body, wq_ref, wkv_ref, q_ref, uv_ref))


def _weights_call(cond, w_ada, b_ada, w_in_t, w_uq, w_ukv):
    rows = cond.shape[0]
    n_out = w_ada.shape[1]
    col_block = D_MODEL
    n_steps = n_out // col_block
    n_prep = IN_WIDTH2 // _PREP_SPLIT
    assert _KR_BLOCK * LANES == OFF_KR and OFF_KR >= _PREP_SPLIT and n_steps == n_prep + 1
    prep_step = lambda j: jnp.minimum(j, n_prep - 1)
    full = lambda shape: pl.BlockSpec(shape, lambda j: (0,) * len(shape))
    uv_shape = (MLA_HEADS // 2, 2 * KV_RANK, 2 * V_HEAD)
    return pl.pallas_call(
        _weights_body,
        grid=(n_steps,),
        in_specs=[
            full((rows, D_MODEL)),
            pl.BlockSpec((D_MODEL, col_block // 2), lambda j: (0, 2 * j)),
            pl.BlockSpec((D_MODEL, col_block // 2), lambda j: (0, 2 * j + 1)),
            pl.BlockSpec((1, col_block), lambda j: (0, j)),
            pl.BlockSpec((_PREP_SPLIT, D_MODEL), lambda j: (prep_step(j), 0)),
            full(w_uq.shape),
            full(w_ukv.shape),
        ],
        out_specs=[
            pl.BlockSpec((rows * SUBLANES, col_block), lambda j: (0, j)),
            pl.BlockSpec((D_MODEL, _PREP_SPLIT), lambda j: (0, prep_step(j))),
            full((Q_RANK, MLA_HEADS * QK_WIDTH)),
            full(uv_shape),
        ],
        out_shape=[
            jax.ShapeDtypeStruct((rows * SUBLANES, n_out), _F32),
            jax.ShapeDtypeStruct((D_MODEL, IN_WIDTH2), _BF16),
            jax.ShapeDtypeStruct((Q_RANK, MLA_HEADS * QK_WIDTH), _BF16),
            jax.ShapeDtypeStruct(uv_shape, _BF16),
        ],
        compiler_params=pltpu.CompilerParams(
            dimension_semantics=("arbitrary",), vmem_limit_bytes=VMEM_LIMIT_BYTES),
        name="weights",
    )(cond, w_ada, w_ada, b_ada.reshape(1, n_out), w_in_t, w_uq, w_ukv)


def _front_substep(x_ref, mod_ref, ng_ref, win_ref, ws_ref, bs_ref, gv_ref, qg_ref, wq_ref,
                   kvg_ref, tables, out_refs, row0, h_write, h_read, z_write, z_read, emit_cache):
    if emit_cache:
        pk_ref, gb_ref, ckv_ref, kr_ref = out_refs
    else:
        pk_ref, gb_ref = out_refs
    first_half = lax.broadcasted_iota(jnp.int32, tables.shape, 1) < QK_ROPE
    t1 = jnp.where(first_half, tables, 0.0)
    t2k = jnp.where(first_half, pltpu.roll(tables, QK_ROPE, 1), 0.0)
    t2q = jnp.where(first_half, t2k, 1.0)
    rows = slice(row0, row0 + SUB_TILE)
    half = SUB_TILE // 2
    assert IN_WIDTH2 == 10 * PROJ_BLOCK

    def project(j):
        lo = j * PROJ_BLOCK
        z_write[:, lo:lo + PROJ_BLOCK] = jnp.dot(h_read[...], win_ref[:, lo:lo + PROJ_BLOCK],
                                                 preferred_element_type=_F32)

    shift = mod_ref[0:1, 0:D_MODEL]
    gain = ng_ref[...] * (1.0 + mod_ref[0:1, D_MODEL:2 * D_MODEL])

    def norm_rows(lo):
        x = x_ref[0, row0 + lo:row0 + lo + half, :]
        ms = jnp.mean(x * x, axis=-1, keepdims=True)
        h_write[lo:lo + half, :] = ((x * lax.rsqrt(ms + EPS)) * gain + shift).astype(_BF16)

    def vn_head(hd):
        lo = hd * A_HEAD_DIM
        v = _gelu_tanh(z_read[:, OFF_V + lo:OFF_V + lo + A_HEAD_DIM])
        return _rmsnorm(v, gv_ref[:, lo:lo + A_HEAD_DIM]).astype(_BF16)

    def mix_head(hd, vn):
        bias = bs_ref[:, hd:hd + 1]
        parts = [jnp.dot(ws_ref[hd], vn[c * CHUNK:(c + 1) * CHUNK, :],
                         preferred_element_type=_F32) + bias for c in range(SUB_TILE // CHUNK)]
        return jnp.concatenate(parts, axis=0)

    def gate_a(hd, mixed):
        lo = hd * A_HEAD_DIM
        u = _gelu_tanh(z_read[:, OFF_U + lo:OFF_U + lo + A_HEAD_DIM])
        g = _silu(z_read[:, OFF_GA + lo:OFF_GA + lo + A_HEAD_DIM])
        pk_ref[0, rows, PK_A + lo:PK_A + lo + A_HEAD_DIM] = (u * mixed * g).astype(_BF16)

    def gate_b(lo, width):
        gb_ref[0, rows, lo:lo + width] = _silu(z_read[:, OFF_GB + lo:OFF_GB + lo + width])

    def rope_q(qa, hd):
        base = hd * QK_WIDTH
        pair = qa[:, base + KV_RANK:base + QK_WIDTH]
        rope = pair * t1 + pltpu.roll(pair, QK_ROPE, 1) * t2q
        pk_ref[0, rows, PK_Q + base:PK_Q + base + QK_WIDTH] = jnp.concatenate(
            [qa[:, base:base + KV_RANK], rope], axis=1).astype(_BF16)

    project(0)
    cq = _rmsnorm(z_read[:, OFF_CQ:OFF_CQ + Q_RANK],
                  qg_ref[...] * (ATTN_SCALE * LOG2_E)).astype(_BF16)
    vn0 = vn_head(0)
    project(1)
    vn1 = vn_head(1)
    gate_b(0, B_WIDTH // 2)
    project(2)
    vn2 = vn_head(2)
    gate_b(B_WIDTH // 2, B_WIDTH // 2)
    project(3)
    vn3 = vn_head(3)
    ckv = _rmsnorm(z_read[:, OFF_CKV:OFF_CKV + KV_RANK], kvg_ref[...])
    kpair = z_read[:, OFF_KR:OFF_KR + 2 * QK_ROPE]
    krot = kpair * t1 + pltpu.roll(kpair, QK_ROPE, 1) * t2k
    pk_ref[0, rows, PK_K:PK_K + QK_WIDTH] = jnp.concatenate([ckv, krot], axis=1).astype(_BF16)
    pk_ref[0, rows, PK_V:PK_V + 2 * KV_RANK] = jnp.concatenate(
        [ckv, jnp.ones_like(ckv)], axis=1).astype(_BF16)
    if emit_cache:
        ckv_ref[0, 0, rows, :] = ckv
        kr_ref[row0 // SUB_TILE, 0] = kpair.T[0:QK_ROPE, :]
    qa = jnp.dot(cq, wq_ref[...], preferred_element_type=_F32)
    norm_rows(0)
    project(4)
    rope_q(qa, 0)
    rope_q(qa, 1)
    mixed0 = mix_head(0, vn0)
    mixed1 = mix_head(1, vn1)
    rope_q(qa, 2)
    rope_q(qa, 3)
    project(5)
    gate_a(0, mixed0)
    mixed2 = mix_head(2, vn2)
    mixed3 = mix_head(3, vn3)
    project(6)
    gate_a(1, mixed1)
    norm_rows(half)
    project(7)
    gate_a(2, mixed2)
    project(8)
    gate_a(3, mixed3)
    project(9)


def _front_body(x_ref, mod_ref, ng_ref, win_ref, ws_ref, bs_ref, gv_ref, qg_ref, wq_ref,
                kvg_ref, tab_ref, *rest, emit_cache, tiles_per_row):
    out_refs, (h_a, h_b, z_a, z_b) = rest[:-4], rest[-4:]
    t = pl.program_id(0)
    tile = x_ref.shape[1]
    assert tile == 2 * SUB_TILE
    n_pos = tab_ref.shape[0]

    @pl.when(t == 0)
    def _():
        h_b[...] = jnp.zeros(h_b.shape, _BF16)
        z_b[...] = jnp.zeros(z_b.shape, _F32)

    pos_base = (jnp.maximum(t - 1, 0) % tiles_per_row) * tile
    for sub, (h_write, h_read, z_write, z_read) in enumerate(
            ((h_a, h_b, z_a, z_b), (h_b, h_a, z_b, z_a))):
        row0 = sub * SUB_TILE
        pos = pl.multiple_of((pos_base + row0) % n_pos, SUB_TILE)
        tables = tab_ref[pl.ds(pos, SUB_TILE), :]
        _front_substep(x_ref, mod_ref, ng_ref, win_ref, ws_ref, bs_ref, gv_ref, qg_ref, wq_ref,
                       kvg_ref, tables, out_refs, row0, h_write, h_read, z_write, z_read,
                       emit_cache)


def _front_call(x, mod, mod_index, norm_g, win2, ws_bf, bs_t, g_v, q_norm_g, wq2, kv_norm_g,
                tables, emit_cache):
    batch, seq, _ = x.shape
    tile = 2 * SUB_TILE
    tiles_per_row = seq // tile
    n_tiles = batch * tiles_per_row
    assert seq % tile == 0 and tables.shape[0] % SUB_TILE == 0

    def in_tile(t):
        tt = jnp.minimum(t, n_tiles - 1)
        return tt // tiles_per_row, tt % tiles_per_row

    def out_tile(t):
        tt = jnp.maximum(t - 1, 0)
        return tt // tiles_per_row, tt % tiles_per_row

    full = lambda shape: pl.BlockSpec(shape, lambda t: (0,) * len(shape))
    in_specs = [
        pl.BlockSpec((1, tile, D_MODEL), lambda t: (*in_tile(t), 0)),
        pl.BlockSpec((SUBLANES, 3 * D_MODEL), lambda t: (mod_index(in_tile(t)[0]), 0)),
        full((1, D_MODEL)),
        full((D_MODEL, IN_WIDTH2)),
        full((A_HEADS, CHUNK, CHUNK)),
        full((CHUNK, A_HEADS)),
        full((1, A_WIDTH)),
        full((1, Q_RANK)),
        full((Q_RANK, MLA_HEADS * QK_WIDTH)),
        full((1, KV_RANK)),
        full(tables.shape),
    ]
    out_shape = [
        jax.ShapeDtypeStruct((batch, seq, PK_WIDTH), _BF16),
        jax.ShapeDtypeStruct((batch, seq, B_WIDTH), _F32),
    ]
    out_specs = [
        pl.BlockSpec((1, tile, PK_WIDTH), lambda t: (*out_tile(t), 0)),
        pl.BlockSpec((1, tile, B_WIDTH), lambda t: (*out_tile(t), 0)),
    ]
    if emit_cache:
        out_shape += [
            jax.ShapeDtypeStruct((batch, 1, seq, KV_RANK), _F32),
            jax.ShapeDtypeStruct((n_tiles * 2, 1, QK_ROPE, SUB_TILE), _F32),
        ]
        out_specs += [
            pl.BlockSpec((1, 1, tile, KV_RANK), lambda t: (out_tile(t)[0], 0, out_tile(t)[1], 0)),
            pl.BlockSpec((2, 1, QK_ROPE, SUB_TILE), lambda t: (jnp.maximum(t - 1, 0), 0, 0, 0)),
        ]
    return pl.pallas_call(
        functools.partial(_front_body, emit_cache=emit_cache, tiles_per_row=tiles_per_row),
        grid=(n_tiles + 1,),
        in_specs=in_specs,
        out_specs=out_specs,
        out_shape=out_shape,
        scratch_shapes=[pltpu.VMEM((SUB_TILE, D_MODEL), _BF16), pltpu.VMEM((SUB_TILE, D_MODEL), _BF16),
                        pltpu.VMEM((SUB_TILE, IN_WIDTH2), _F32),
                        pltpu.VMEM((SUB_TILE, IN_WIDTH2), _F32)],
        compiler_params=pltpu.CompilerParams(
            dimension_semantics=("arbitrary",),
            vmem_limit_bytes=VMEM_LIMIT_BYTES),
        name="front_ctx" if emit_cache else "front_lat",
    )(x, mod, norm_g, win2, ws_bf, bs_t, g_v, q_norm_g, wq2, kv_norm_g, tables)


def _key_chunks(k_refs):
    chunks = []
    for k_ref in k_refs:
        size = min(KEY_CHUNK, k_ref.shape[1])
        assert k_ref.shape[1] % size == 0
        chunks += [(k_ref, lo, size) for lo in range(0, k_ref.shape[1], size)]
    return chunks


def _values_t(k_ref, row):
    ckv_t = k_ref[row, :, 0:KV_RANK].T
    return jnp.concatenate([ckv_t, jnp.ones((ONES_ROWS, ckv_t.shape[1]), _BF16)], axis=0)


def _attention(q_ref, k_refs, qt_ref, fillers):
    n_rows = q_ref.shape[0]
    chunks = _key_chunks(k_refs)
    steps = [(row, hd, c) for row in range(n_rows) for hd in range(MLA_HEADS)
             for c in range(len(chunks))]

    for row in range(n_rows):
        for hd in range(MLA_HEADS):
            qt_ref[row * MLA_HEADS + hd] = q_ref[row, :, hd * QK_WIDTH:(hd + 1) * QK_WIDTH].T

    def logits(row, hd, c):
        k_ref, lo, size = chunks[c]
        return jnp.dot(k_ref[row, lo:lo + size, :], qt_ref[row * MLA_HEADS + hd],
                       preferred_element_type=_F32)

    pending = [logits(*step) for step in steps[:SCORE_LOOKAHEAD]]
    values_t = [{id(k_ref): _values_t(k_ref, row) for k_ref in k_refs} for row in range(n_rows)]
    outs = {}
    m = acc = None
    for j, (row, hd, c) in enumerate(steps):
        if j + SCORE_LOOKAHEAD < len(steps):
            pending.append(logits(*steps[j + SCORE_LOOKAHEAD]))
        s, pending[j] = pending[j], None
        k_ref, lo, size = chunks[c]
        chunk_max = jnp.max(s, axis=0, keepdims=True)
        if c > 0:
            m_new = jnp.maximum(m, chunk_max)
            acc = acc * jnp.exp2(m - m_new)
            m = m_new
        else:
            m = chunk_max
        p = jnp.exp2(s - m).astype(_BF16)
        part = jnp.dot(values_t[row][id(k_ref)][:, lo:lo + size], p, preferred_element_type=_F32)
        acc = acc + part if c > 0 else part
        if j >= len(steps) - len(fillers):
            fillers[j - len(steps) + len(fillers)]()
        if c == len(chunks) - 1:
            outs[row, hd] = (acc[0:KV_RANK, :] / acc[KV_RANK:KV_RANK + 1, :]).T
    return outs


def _attention_single(q_ref, k_ref, fillers):
    n_rows, tq = q_ref.shape[0], q_ref.shape[1]
    groups = [(row, pair) for row in range(n_rows) for pair in range(MLA_HEADS // 2)]
    scores = {}
    for row, pair in groups:
        q = jnp.concatenate([q_ref[row, :, hd * QK_WIDTH:(hd + 1) * QK_WIDTH]
                             for hd in (2 * pair, 2 * pair + 1)], axis=0)
        scores[row, pair] = lax.dot_general(q, k_ref[row], _NT_DIMS, preferred_element_type=_F32)
    for fill in fillers:
        fill()
    outs = {}
    for row, pair in groups:
        s = scores[row, pair]
        ckv = k_ref[row, :, 0:KV_RANK]
        values = jnp.concatenate([ckv, jnp.ones_like(ckv)], axis=-1)
        p = jnp.exp2(s - jnp.max(s, axis=1, keepdims=True)).astype(_BF16)
        acc = jnp.dot(p, values, preferred_element_type=_F32)
        o = acc[:, 0:KV_RANK] / acc[:, KV_RANK:2 * KV_RANK]
        outs[row, 2 * pair], outs[row, 2 * pair + 1] = o[0:tq, :], o[tq:2 * tq, :]
    return outs


def _back_body(*refs, has_ctx):
    if has_ctx:
        (q_ref, klat_ref, kctx_ref, a_ref, gb_ref, x_ref, mod_ref, wuv_ref, wo_ref, fg_ref,
         y_ref, qt_ref, ya_ref) = refs
        k_refs = (kctx_ref, klat_ref)
    else:
        (q_ref, klat_ref, a_ref, gb_ref, x_ref, mod_ref, wuv_ref, wo_ref, fg_ref, y_ref,
         qt_ref, ya_ref) = refs
        k_refs = (klat_ref,)
    n_rows, tq = q_ref.shape[0], q_ref.shape[1]

    block = min(OUT_ROWS, tq)
    blocks = [(row, lo) for row in range(n_rows) for lo in range(0, tq, block)]

    def a_part(row, lo):
        def run():
            ya_ref[row, lo:lo + block, :] = jnp.dot(
                a_ref[row, lo:lo + block, :], wo_ref[0:A_WIDTH, :], preferred_element_type=_F32)
        return run

    fillers = [a_part(*b) for b in blocks]
    if len(_key_chunks(k_refs)) == 1:
        outs = _attention_single(q_ref, k_refs[0], fillers)
    else:
        outs = _attention(q_ref, k_refs, qt_ref, fillers)

    gate = mod_ref[0:1, 2 * D_MODEL:3 * D_MODEL]
    for row, lo in blocks:
        rows = slice(lo, lo + block)
        attn_cols = []
        for pair in range(MLA_HEADS // 2):
            o2 = jnp.concatenate([outs[row, 2 * pair][rows, :],
                                  outs[row, 2 * pair + 1][rows, :]], axis=1)
            attn_cols.append(jnp.dot(o2.astype(_BF16), wuv_ref[pair],
                                     preferred_element_type=_F32))
        attn = jnp.concatenate(attn_cols, axis=1) * gb_ref[row, rows, :]
        y = ya_ref[row, rows, :] + jnp.dot(
            attn.astype(_BF16), wo_ref[A_WIDTH:A_WIDTH + B_WIDTH, :], preferred_element_type=_F32)
        out = x_ref[row, rows, :] + gate * y
        y_ref[row, rows, :] = _rmsnorm(out, fg_ref[...])


def _back_call(packed, kctx, gb, x, mod, mod_index, wuv2, wo_bf, final_g, rows_per_step):
    batch, seq, _ = x.shape
    tq = min(TOKEN_TILE, seq)
    has_ctx = kctx is not None
    rows = rows_per_step
    assert batch % rows == 0 and (rows == 1 or seq == tq)
    full = lambda shape: pl.BlockSpec(shape, lambda b, i: (0,) * len(shape))
    keys = lambda arr: pl.BlockSpec((rows, arr.shape[1], arr.shape[2]), lambda b, i: (b, 0, 0))
    in_specs = [
        pl.BlockSpec((rows, tq, MLA_HEADS * QK_WIDTH),
                     lambda b, i: (b, i, PK_Q // (MLA_HEADS * QK_WIDTH))),
        pl.BlockSpec((rows, seq, QK_WIDTH), lambda b, i: (b, 0, PK_K // QK_WIDTH)),
    ]
    args = [packed, packed]
    if has_ctx:
        in_specs += [keys(kctx)]
        args += [kctx]
    in_specs += [
        pl.BlockSpec((rows, tq, A_WIDTH), lambda b, i: (b, i, PK_A // A_WIDTH)),
        pl.BlockSpec((rows, tq, B_WIDTH), lambda b, i: (b, i, 0)),
        pl.BlockSpec((rows, tq, D_MODEL), lambda b, i: (b, i, 0)),
        pl.BlockSpec((SUBLANES, 3 * D_MODEL), lambda b, i: (mod_index(b * rows), 0)),
        full((MLA_HEADS // 2, 2 * KV_RANK, 2 * V_HEAD)),
        full((D_MODEL, D_MODEL)),
        full((1, D_MODEL)),
    ]
    args += [packed, gb, x, mod, wuv2, wo_bf, final_g]
    return pl.pallas_call(
        functools.partial(_back_body, has_ctx=has_ctx),
        grid=(batch // rows, seq // tq),
        in_specs=in_specs,
        out_specs=pl.BlockSpec((rows, tq, D_MODEL), lambda b, i: (b, i, 0)),
        out_shape=jax.ShapeDtypeStruct((batch, seq, D_MODEL), _F32),
        scratch_shapes=[
            pltpu.VMEM((rows * MLA_HEADS, QK_WIDTH, tq), _BF16),
            pltpu.VMEM((rows, tq, D_MODEL), _F32),
        ],
        compiler_params=pltpu.CompilerParams(
            dimension_semantics=("arbitrary", "arbitrary"),
            vmem_limit_bytes=VMEM_LIMIT_BYTES),
        name="back_lat" if has_ctx else "back_ctx",
    )(*args)


def _rope_tables(n_tokens, rotate):
    if rotate:
        rows = n_tokens // GRID_W
        inv = ROPE_THETA ** (-jnp.arange(AXIS_PAIRS, dtype=_F32) / AXIS_PAIRS)
        ang_r = jnp.arange(rows, dtype=_F32)[:, None] * inv
        ang_c = jnp.arange(GRID_W, dtype=_F32)[:, None] * inv

        def grid_table(fn):
            tr = jnp.broadcast_to(fn(ang_r)[:, None, :], (rows, GRID_W, AXIS_PAIRS))
            tc = jnp.broadcast_to(fn(ang_c)[None, :, :], (rows, GRID_W, AXIS_PAIRS))
            half = jnp.concatenate([tr, tc], axis=-1).reshape(n_tokens, 2 * AXIS_PAIRS)
            return jnp.concatenate([half, half], axis=-1)

        cos, sin = grid_table(jnp.cos), grid_table(jnp.sin)
    else:
        cos, sin = jnp.ones((n_tokens, QK_ROPE), _F32), jnp.zeros((n_tokens, QK_ROPE), _F32)
    return jnp.concatenate([cos, sin], axis=-1)


def kernel(x_prompt, x_sample, cache_ckv, cache_krope, c, c_ctx, norm_g, w_ada, b_ada, w_in, w_s,
           b_s, g_v, q_norm_g, w_uq, kv_norm_g, w_ukv, w_o, final_g):
    depth = norm_g.shape[0]
    assert depth == 1 and w_in.shape[2] == IN_WIDTH
    dec_batch = x_sample.shape[0]
    xp, xs = x_prompt, x_sample
    new_ckv, new_kr = [], []
    for l in range(depth):
        cond = jnp.concatenate(
            [c, c_ctx[None, :], jnp.zeros((16 - dec_batch - 1, D_MODEL), _F32)], axis=0)
        mod, win2, wq2, wuv2 = _weights_call(cond, w_ada[l], b_ada[l], jnp.transpose(w_in[l]),
                                             w_uq[l], w_ukv[l])
        wo_bf = w_o[l].astype(_BF16)
        ws_bf = w_s[l].astype(_BF16)
        bs_t = b_s[l].T
        gv_row = g_v[l].reshape(1, A_WIDTH)
        shared = (norm_g[l].reshape(1, D_MODEL), win2, ws_bf, bs_t, gv_row,
                  q_norm_g[l].reshape(1, Q_RANK), wq2, kv_norm_g[l].reshape(1, KV_RANK))
        fg = final_g.reshape(1, D_MODEL)

        ctx_index = lambda b: dec_batch
        ctx_batch, ctx_seq, _ = xp.shape
        pack = 2 * SUB_TILE // ctx_seq
        pk_c, gb_c, ckv_c, kr_c = _front_call(
            xp.reshape(ctx_batch // pack, pack * ctx_seq, D_MODEL), mod, ctx_index, *shared,
            _rope_tables(ctx_seq, False), True)
        unpack = lambda arr: arr.reshape(ctx_batch, ctx_seq, arr.shape[-1])
        xp = _back_call(unpack(pk_c), None, unpack(gb_c), xp, mod, ctx_index, wuv2, wo_bf, fg,
                        rows_per_step=pack)
        new_ckv.append(ckv_c.reshape(ctx_batch, 1, ctx_seq, KV_RANK))
        assert ctx_seq == SUB_TILE
        new_kr.append(jnp.swapaxes(kr_c, 2, 3))

        lat_index = lambda b: b
        pk_s, gb_s = _front_call(
            xs, mod, lat_index, *shared, _rope_tables(xs.shape[1], True), False)
        cache_k = jnp.concatenate(
            [cache_ckv[:, l], jnp.zeros(cache_krope[:, l].shape, _F32), cache_krope[:, l]],
            axis=-1).astype(_BF16)
        xs = _back_call(pk_s, cache_k, gb_s, xs, mod, lat_index, wuv2, wo_bf, fg, rows_per_step=1)
    return (xp, xs, jnp.concatenate(new_ckv, axis=1), jnp.concatenate(new_kr, axis=1))
```

```python
import functools
import math

import jax
import jax.numpy as jnp
from jax import lax
from jax.experimental import pallas as pl
from jax.experimental.pallas import tpu as pltpu

D_MODEL = 1024
GRID_W = 64
EPS = 1e-6
A_HEADS = 4
A_HEAD_DIM = 128
A_WIDTH = A_HEADS * A_HEAD_DIM
CHUNK = 128
MLA_HEADS = 4
QK_NOPE = 128
QK_ROPE = 64
V_HEAD = 128
B_WIDTH = MLA_HEADS * V_HEAD
Q_RANK = 256
KV_RANK = 128
AXIS_PAIRS = QK_ROPE // 4
ROPE_THETA = 10000.0
ATTN_SCALE = 1.0 / math.sqrt(QK_NOPE + QK_ROPE)
LOG2_E = 1.4426950408889634

QK_WIDTH = KV_RANK + 2 * QK_ROPE
OFF_U = 0
OFF_V = OFF_U + A_WIDTH
OFF_GA = OFF_V + A_WIDTH
OFF_CQ = OFF_GA + A_WIDTH
OFF_CKV = OFF_CQ + Q_RANK
OFF_KR = OFF_CKV + KV_RANK
OFF_GB = OFF_KR + 2 * QK_ROPE
IN_WIDTH = OFF_GB - QK_ROPE + B_WIDTH
IN_WIDTH2 = OFF_GB + B_WIDTH

PK_Q = 0
PK_A = PK_Q + MLA_HEADS * QK_WIDTH
PK_K = PK_A + A_WIDTH
PK_WIDTH = PK_K + QK_WIDTH

LANES = 128
SUBLANES = 8
ONES_ROWS = 16
SUB_TILE = 256
TOKEN_TILE = 512
PROJ_BLOCK = 256
OUT_ROWS = 256
KEY_CHUNK = 256
SCORE_LOOKAHEAD = 3
VMEM_LIMIT_BYTES = 56 * 1024 * 1024

_F32 = jnp.float32
_BF16 = jnp.bfloat16
_NT_DIMS = (((1,), (1,)), ((), ()))


def _silu(x):
    hx = 0.5 * x
    return hx + hx * jnp.tanh(hx)


def _gelu_tanh(x):
    return x * (0.5 * (1.0 + jnp.tanh(0.7978845608028654 * (x + 0.044715 * (x * x * x)))))


def _rmsnorm(x, g):
    ms = jnp.mean(x * x, axis=-1, keepdims=True)
    return (x * lax.rsqrt(ms + EPS)) * g


def _mod_body(cond_ref, wa_ref, wb_ref, b_ref, o_ref):
    s = _silu(cond_ref[...]).astype(_BF16)
    halves = [jnp.dot(s, w_ref[...].astype(_BF16), preferred_element_type=_F32)
              for w_ref in (wa_ref, wb_ref)]
    mod = jnp.concatenate(halves, axis=1) + b_ref[...]
    rows, cols = mod.shape
    o_ref[...] = jnp.broadcast_to(mod[:, None, :], (rows, SUBLANES, cols)).reshape(
        rows * SUBLANES, cols)


def _fold_body(wq_ref, wkv_ref, q_ref, uv_ref):
    half, quarter = LANES // 2, LANES // 4
    assert QK_NOPE == LANES and QK_ROPE == half and V_HEAD == LANES
    lane = lax.broadcasted_iota(jnp.int32, (Q_RANK, LANES), 1)
    low = lane < half

    def cols(block):
        return wq_ref[:, block * LANES:(block + 1) * LANES]

    for hd in range(MLA_HEADS):
        start = hd * (QK_NOPE + QK_ROPE)
        b0, odd = divmod(start, LANES)
        if odd == 0:
            nope = cols(b0)
            rope_blk = cols(b0 + 1)
            rope_lo = rope_blk
            rope_hi = pltpu.roll(rope_blk, half, 1)
        else:
            assert odd == half
            nope = jnp.where(low, pltpu.roll(cols(b0), half, 1), pltpu.roll(cols(b0 + 1), half, 1))
            rope_blk = cols(b0 + 1)
            rope_lo = pltpu.roll(rope_blk, half, 1)
            rope_hi = rope_blk
        partner = jnp.where(lane < half + quarter,
                            -pltpu.roll(rope_hi, LANES - quarter, 1),
                            pltpu.roll(rope_hi, quarter, 1))
        w_uk = wkv_ref[:, hd * (QK_NOPE + V_HEAD):hd * (QK_NOPE + V_HEAD) + QK_NOPE]
        absorbed = lax.dot_general(nope, w_uk, _NT_DIMS, precision=lax.Precision.HIGHEST,
                                   preferred_element_type=_F32)
        base = hd * QK_WIDTH
        q_ref[:, base:base + KV_RANK] = absorbed.astype(_BF16)
        q_ref[:, base + KV_RANK:base + QK_WIDTH] = jnp.where(low, rope_lo, partner).astype(_BF16)

    uv_ref[...] = jnp.zeros(uv_ref.shape, _BF16)
    for hd in range(MLA_HEADS):
        w_uv = wkv_ref[:, hd * (QK_NOPE + V_HEAD) + QK_NOPE:(hd + 1) * (QK_NOPE + V_HEAD)]
        pair, pos = divmod(hd, 2)
        uv_ref[pair, pos * KV_RANK:(pos + 1) * KV_RANK, pos * V_HEAD:(pos + 1) * V_HEAD] = (
            w_uv.astype(_BF16))


_KR_BLOCK = OFF_KR // LANES


_PREP_SPLIT = IN_WIDTH2 // 2


def _prep_body(step, w_ref, o_ref):
    half = QK_ROPE // 2
    blocks_per_step = _PREP_SPLIT // LANES
    base = step * _PREP_SPLIT
    for j in range(step * blocks_per_step, (step + 1) * blocks_per_step):
        if j < _KR_BLOCK:
            blk = w_ref[j * LANES - base:(j + 1) * LANES - base, :]
        elif j == _KR_BLOCK:
            kr = w_ref[OFF_KR - base:OFF_KR - base + QK_ROPE, :]
            partner = jnp.concatenate([-kr[half:, :], kr[:half, :]], axis=0)
            blk = jnp.concatenate([kr, partner], axis=0)
        else:
            lo = j * LANES - QK_ROPE - base
            blk = w_ref[lo:lo + LANES, :]
        o_ref[:, j * LANES - base:(j + 1) * LANES - base] = blk.T.astype(_BF16)


def _weights_body(cond_ref, wa_ref, wb_ref, b_ref, win_ref, wq_ref, wkv_ref,
                  mod_ref, win2_ref, q_ref, uv_ref):
    j = pl.program_id(0)
    _mod_body(cond_ref, wa_ref, wb_ref, b_ref, mod_ref)
    n_prep = IN_WIDTH2 // _PREP_SPLIT
    for step in range(n_prep):
        pl.when(j == step)(functools.partial(_prep_body, step, win_ref, win2_ref))
    pl.when(j == n_prep)(functools.partial(_fold_body, wq_ref, wkv_ref, q_ref, uv_ref))


def _weights_call(cond, w_ada, b_ada, w_in_t, w_uq, w_ukv):
    rows = cond.shape[0]
    n_out = w_ada.shape[1]
    col_block = D_MODEL
    n_steps = n_out // col_block
    n_prep = IN_WIDTH2 // _PREP_SPLIT
    assert _KR_BLOCK * LANES == OFF_KR and OFF_KR >= _PREP_SPLIT and n_steps == n_prep + 1
    prep_step = lambda j: jnp.minimum(j, n_prep - 1)
    full = lambda shape: pl.BlockSpec(shape, lambda j: (0,) * len(shape))
    uv_shape = (MLA_HEADS // 2, 2 * KV_RANK, 2 * V_HEAD)
    return pl.pallas_call(
        _weights_body,
        grid=(n_steps,),
        in_specs=[
            full((rows, D_MODEL)),
            pl.BlockSpec((D_MODEL, col_block // 2), lambda j: (0, 2 * j)),
            pl.BlockSpec((D_MODEL, col_block // 2), lambda j: (0, 2 * j + 1)),
            pl.BlockSpec((1, col_block), lambda j: (0, j)),
            pl.BlockSpec((_PREP_SPLIT, D_MODEL), lambda j: (prep_step(j), 0)),
            full(w_uq.shape),
            full(w_ukv.shape),
        ],
        out_specs=[
            pl.BlockSpec((rows * SUBLANES, col_block), lambda j: (0, j)),
            pl.BlockSpec((D_MODEL, _PREP_SPLIT), lambda j: (0, prep_step(j))),
            full((Q_RANK, MLA_HEADS * QK_WIDTH)),
            full(uv_shape),
        ],
        out_shape=[
            jax.ShapeDtypeStruct((rows * SUBLANES, n_out), _F32),
            jax.ShapeDtypeStruct((D_MODEL, IN_WIDTH2), _BF16),
            jax.ShapeDtypeStruct((Q_RANK, MLA_HEADS * QK_WIDTH), _BF16),
            jax.ShapeDtypeStruct(uv_shape, _BF16),
        ],
        compiler_params=pltpu.CompilerParams(
            dimension_semantics=("arbitrary",), vmem_limit_bytes=VMEM_LIMIT_BYTES),
        name="weights",
    )(cond, w_ada, w_ada, b_ada.reshape(1, n_out), w_in_t, w_uq, w_ukv)


def _front_substep(x_ref, mod_ref, ng_ref, win_ref, ws_ref, bs_ref, gv_ref, qg_ref, wq_ref,
                   kvg_ref, tables, out_refs, row0, h_write, h_read, z_write, z_read, emit_cache):
    if emit_cache:
        pk_ref, gb_ref, ckv_ref, kr_ref = out_refs
    else:
        pk_ref, gb_ref = out_refs
    first_half = lax.broadcasted_iota(jnp.int32, tables.shape, 1) < QK_ROPE
    t1 = jnp.where(first_half, tables, 0.0)
    t2k = jnp.where(first_half, pltpu.roll(tables, QK_ROPE, 1), 0.0)
    t2q = jnp.where(first_half, t2k, 1.0)
    rows = slice(row0, row0 + SUB_TILE)
    half = SUB_TILE // 2
    assert IN_WIDTH2 == 10 * PROJ_BLOCK

    def project(j):
        lo = j * PROJ_BLOCK
        z_write[:, lo:lo + PROJ_BLOCK] = jnp.dot(h_read[...], win_ref[:, lo:lo + PROJ_BLOCK],
                                                 preferred_element_type=_F32)

    shift = mod_ref[0:1, 0:D_MODEL]
    gain = ng_ref[...] * (1.0 + mod_ref[0:1, D_MODEL:2 * D_MODEL])

    def norm_rows(lo):
        x = x_ref[0, row0 + lo:row0 + lo + half, :]
        ms = jnp.mean(x * x, axis=-1, keepdims=True)
        h_write[lo:lo + half, :] = ((x * lax.rsqrt(ms + EPS)) * gain + shift).astype(_BF16)

    def vn_head(hd):
        lo = hd * A_HEAD_DIM
        v = _gelu_tanh(z_read[:, OFF_V + lo:OFF_V + lo + A_HEAD_DIM])
        return _rmsnorm(v, gv_ref[:, lo:lo + A_HEAD_DIM]).astype(_BF16)

    def mix_head(hd, vn):
        bias = bs_ref[:, hd:hd + 1]
        parts = [jnp.dot(ws_ref[hd], vn[c * CHUNK:(c + 1) * CHUNK, :],
                         preferred_element_type=_F32) + bias for c in range(SUB_TILE // CHUNK)]
        return jnp.concatenate(parts, axis=0)

    def gate_a(hd, mixed):
        lo = hd * A_HEAD_DIM
        u = _gelu_tanh(z_read[:, OFF_U + lo:OFF_U + lo + A_HEAD_DIM])
        g = _silu(z_read[:, OFF_GA + lo:OFF_GA + lo + A_HEAD_DIM])
        pk_ref[0, rows, PK_A + lo:PK_A + lo + A_HEAD_DIM] = (u * mixed * g).astype(_BF16)

    def gate_b(lo, width):
        gb_ref[0, rows, lo:lo + width] = _silu(z_read[:, OFF_GB + lo:OFF_GB + lo + width])

    def rope_q(qa, hd):
        base = hd * QK_WIDTH
        pair = qa[:, base + KV_RANK:base + QK_WIDTH]
        rope = pair * t1 + pltpu.roll(pair, QK_ROPE, 1) * t2q
        pk_ref[0, rows, PK_Q + base:PK_Q + base + QK_WIDTH] = jnp.concatenate(
            [qa[:, base:base + KV_RANK], rope], axis=1).astype(_BF16)

    project(0)
    cq = _rmsnorm(z_read[:, OFF_CQ:OFF_CQ + Q_RANK],
                  qg_ref[...] * (ATTN_SCALE * LOG2_E)).astype(_BF16)
    vn0 = vn_head(0)
    project(1)
    vn1 = vn_head(1)
    gate_b(0, B_WIDTH // 2)
    project(2)
    vn2 = vn_head(2)
    gate_b(B_WIDTH // 2, B_WIDTH // 2)
    project(3)
    vn3 = vn_head(3)
    ckv = _rmsnorm(z_read[:, OFF_CKV:OFF_CKV + KV_RANK], kvg_ref[...])
    kpair = z_read[:, OFF_KR:OFF_KR + 2 * QK_ROPE]
    krot = kpair * t1 + pltpu.roll(kpair, QK_ROPE, 1) * t2k
    pk_ref[0, rows, PK_K:PK_K + QK_WIDTH] = jnp.concatenate([ckv, krot], axis=1).astype(_BF16)
    if emit_cache:
        ckv_ref[0, 0, rows, :] = ckv
        kr_ref[row0 // SUB_TILE, 0] = kpair.T[0:QK_ROPE, :]
    qa = jnp.dot(cq, wq_ref[...], preferred_element_type=_F32)
    norm_rows(0)
    project(4)
    rope_q(qa, 0)
    rope_q(qa, 1)
    mixed0 = mix_head(0, vn0)
    mixed1 = mix_head(1, vn1)
    rope_q(qa, 2)
    rope_q(qa, 3)
    project(5)
    gate_a(0, mixed0)
    mixed2 = mix_head(2, vn2)
    mixed3 = mix_head(3, vn3)
    project(6)
    gate_a(1, mixed1)
    norm_rows(half)
    project(7)
    gate_a(2, mixed2)
    project(8)
    gate_a(3, mixed3)
    project(9)


def _front_body(x_ref, mod_ref, ng_ref, win_ref, ws_ref, bs_ref, gv_ref, qg_ref, wq_ref,
                kvg_ref, tab_ref, *rest, emit_cache, tiles_per_row):
    out_refs, (h_a, h_b, z_a, z_b) = rest[:-4], rest[-4:]
    t = pl.program_id(0)
    tile = x_ref.shape[1]
    assert tile == 2 * SUB_TILE
    n_pos = tab_ref.shape[0]

    @pl.when(t == 0)
    def _():
        h_b[...] = jnp.zeros(h_b.shape, _BF16)
        z_b[...] = jnp.zeros(z_b.shape, _F32)

    pos_base = (jnp.maximum(t - 1, 0) % tiles_per_row) * tile
    for sub, (h_write, h_read, z_write, z_read) in enumerate(
            ((h_a, h_b, z_a, z_b), (h_b, h_a, z_b, z_a))):
        row0 = sub * SUB_TILE
        pos = pl.multiple_of((pos_base + row0) % n_pos, SUB_TILE)
        tables = tab_ref[pl.ds(pos, SUB_TILE), :]
        _front_substep(x_ref, mod_ref, ng_ref, win_ref, ws_ref, bs_ref, gv_ref, qg_ref, wq_ref,
                       kvg_ref, tables, out_refs, row0, h_write, h_read, z_write, z_read,
                       emit_cache)


def _front_call(x, mod, mod_index, norm_g, win2, ws_bf, bs_t, g_v, q_norm_g, wq2, kv_norm_g,
                tables, emit_cache):
    batch, seq, _ = x.shape
    tile = 2 * SUB_TILE
    tiles_per_row = seq // tile
    n_tiles = batch * tiles_per_row
    assert seq % tile == 0 and tables.shape[0] % SUB_TILE == 0

    def in_tile(t):
        tt = jnp.minimum(t, n_tiles - 1)
        return tt // tiles_per_row, tt % tiles_per_row

    def out_tile(t):
        tt = jnp.maximum(t - 1, 0)
        return tt // tiles_per_row, tt % tiles_per_row

    full = lambda shape: pl.BlockSpec(shape, lambda t: (0,) * len(shape))
    in_specs = [
        pl.BlockSpec((1, tile, D_MODEL), lambda t: (*in_tile(t), 0)),
        pl.BlockSpec((SUBLANES, 3 * D_MODEL), lambda t: (mod_index(in_tile(t)[0]), 0)),
        full((1, D_MODEL)),
        full((D_MODEL, IN_WIDTH2)),
        full((A_HEADS, CHUNK, CHUNK)),
        full((CHUNK, A_HEADS)),
        full((1, A_WIDTH)),
        full((1, Q_RANK)),
        full((Q_RANK, MLA_HEADS * QK_WIDTH)),
        full((1, KV_RANK)),
        full(tables.shape),
    ]
    out_shape = [
        jax.ShapeDtypeStruct((batch, seq, PK_WIDTH), _BF16),
        jax.ShapeDtypeStruct((batch, seq, B_WIDTH), _F32),
    ]
    out_specs = [
        pl.BlockSpec((1, tile, PK_WIDTH), lambda t: (*out_tile(t), 0)),
        pl.BlockSpec((1, tile, B_WIDTH), lambda t: (*out_tile(t), 0)),
    ]
    if emit_cache:
        out_shape += [
            jax.ShapeDtypeStruct((batch, 1, seq, KV_RANK), _F32),
            jax.ShapeDtypeStruct((n_tiles * 2, 1, QK_ROPE, SUB_TILE), _F32),
        ]
        out_specs += [
            pl.BlockSpec((1, 1, tile, KV_RANK), lambda t: (out_tile(t)[0], 0, out_tile(t)[1], 0)),
            pl.BlockSpec((2, 1, QK_ROPE, SUB_TILE), lambda t: (jnp.maximum(t - 1, 0), 0, 0, 0)),
        ]
    return pl.pallas_call(
        functools.partial(_front_body, emit_cache=emit_cache, tiles_per_row=tiles_per_row),
        grid=(n_tiles + 1,),
        in_specs=in_specs,
        out_specs=out_specs,
        out_shape=out_shape,
        scratch_shapes=[pltpu.VMEM((SUB_TILE, D_MODEL), _BF16), pltpu.VMEM((SUB_TILE, D_MODEL), _BF16),
                        pltpu.VMEM((SUB_TILE, IN_WIDTH2), _F32),
                        pltpu.VMEM((SUB_TILE, IN_WIDTH2), _F32)],
        compiler_params=pltpu.CompilerParams(
            dimension_semantics=("arbitrary",),
            vmem_limit_bytes=VMEM_LIMIT_BYTES),
        name="front_ctx" if emit_cache else "front_lat",
    )(x, mod, norm_g, win2, ws_bf, bs_t, g_v, q_norm_g, wq2, kv_norm_g, tables)


def _key_chunks(k_refs):
    chunks = []
    for k_ref in k_refs:
        size = min(KEY_CHUNK, k_ref.shape[1])
        assert k_ref.shape[1] % size == 0
        chunks += [(k_ref, lo, size) for lo in range(0, k_ref.shape[1], size)]
    return chunks


def _values_t(k_ref, row):
    ckv_t = k_ref[row, :, 0:KV_RANK].T
    return jnp.concatenate([ckv_t, jnp.ones((ONES_ROWS, ckv_t.shape[1]), _BF16)], axis=0)


def _attention(q_ref, k_refs, qt_ref, fillers):
    n_rows = q_ref.shape[0]
    chunks = _key_chunks(k_refs)
    steps = [(row, hd, c) for row in range(n_rows) for hd in range(MLA_HEADS)
             for c in range(len(chunks))]

    for row in range(n_rows):
        for hd in range(MLA_HEADS):
            qt_ref[row * MLA_HEADS + hd] = q_ref[row, :, hd * QK_WIDTH:(hd + 1) * QK_WIDTH].T

    def logits(row, hd, c):
        k_ref, lo, size = chunks[c]
        return jnp.dot(k_ref[row, lo:lo + size, :], qt_ref[row * MLA_HEADS + hd],
                       preferred_element_type=_F32)

    pending = [logits(*step) for step in steps[:SCORE_LOOKAHEAD]]
    values_t = [{id(k_ref): _values_t(k_ref, row) for k_ref in k_refs} for row in range(n_rows)]
    outs = {}
    m = acc = None
    for j, (row, hd, c) in enumerate(steps):
        if j + SCORE_LOOKAHEAD < len(steps):
            pending.append(logits(*steps[j + SCORE_LOOKAHEAD]))
        s, pending[j] = pending[j], None
        k_ref, lo, size = chunks[c]
        chunk_max = jnp.max(s, axis=0, keepdims=True)
        if c > 0:
            m_new = jnp.maximum(m, chunk_max)
            acc = acc * jnp.exp2(m - m_new)
            m = m_new
        else:
            m = chunk_max
        p = jnp.exp2(s - m).astype(_BF16)
        part = jnp.dot(values_t[row][id(k_ref)][:, lo:lo + size], p, preferred_element_type=_F32)
        acc = acc + part if c > 0 else part
        if j >= len(steps) - len(fillers):
            fillers[j - len(steps) + len(fillers)]()
        if c == len(chunks) - 1:
            outs[row, hd] = (acc[0:KV_RANK, :] / acc[KV_RANK:KV_RANK + 1, :]).T
    return outs


def _attention_single(q_ref, k_ref, fillers):
    n_rows, tq = q_ref.shape[0], q_ref.shape[1]
    groups = [(row, pair) for row in range(n_rows) for pair in range(MLA_HEADS // 2)]
    scores = {}
    for row, pair in groups:
        q = jnp.concatenate([q_ref[row, :, hd * QK_WIDTH:(hd + 1) * QK_WIDTH]
                             for hd in (2 * pair, 2 * pair + 1)], axis=0)
        scores[row, pair] = lax.dot_general(q, k_ref[row], _NT_DIMS, preferred_element_type=_F32)
    for fill in fillers:
        fill()
    outs = {}
    for row, pair in groups:
        s = scores[row, pair]
        ckv = k_ref[row, :, 0:KV_RANK]
        values = jnp.concatenate([ckv, jnp.ones_like(ckv)], axis=-1)
        p = jnp.exp2(s - jnp.max(s, axis=1, keepdims=True)).astype(_BF16)
        acc = jnp.dot(p, values, preferred_element_type=_F32)
        o = acc[:, 0:KV_RANK] / acc[:, KV_RANK:2 * KV_RANK]
        outs[row, 2 * pair], outs[row, 2 * pair + 1] = o[0:tq, :], o[tq:2 * tq, :]
    return outs


def _back_body(*refs, has_ctx):
    if has_ctx:
        (q_ref, klat_ref, kctx_ref, a_ref, gb_ref, x_ref, mod_ref, wuv_ref, wo_ref, fg_ref,
         y_ref, qt_ref, ya_ref) = refs
        k_refs = (kctx_ref, klat_ref)
    else:
        (q_ref, klat_ref, a_ref, gb_ref, x_ref, mod_ref, wuv_ref, wo_ref, fg_ref, y_ref,
         qt_ref, ya_ref) = refs
        k_refs = (klat_ref,)
    n_rows, tq = q_ref.shape[0], q_ref.shape[1]

    block = min(OUT_ROWS, tq)
    blocks = [(row, lo) for row in range(n_rows) for lo in range(0, tq, block)]

    def a_part(row, lo):
        def run():
            ya_ref[row, lo:lo + block, :] = jnp.dot(
                a_ref[row, lo:lo + block, :], wo_ref[0:A_WIDTH, :], preferred_element_type=_F32)
        return run

    fillers = [a_part(*b) for b in blocks]
    if len(_key_chunks(k_refs)) == 1:
        outs = _attention_single(q_ref, k_refs[0], fillers)
    else:
        outs = _attention(q_ref, k_refs, qt_ref, fillers)

    gate = mod_ref[0:1, 2 * D_MODEL:3 * D_MODEL]
    for row, lo in blocks:
        rows = slice(lo, lo + block)
        attn_cols = []
        for pair in range(MLA_HEADS // 2):
            o2 = jnp.concatenate([outs[row, 2 * pair][rows, :],
                                  outs[row, 2 * pair + 1][rows, :]], axis=1)
            attn_cols.append(jnp.dot(o2.astype(_BF16), wuv_ref[pair],
                                     preferred_element_type=_F32))
        attn = jnp.concatenate(attn_cols, axis=1) * gb_ref[row, rows, :]
        y = ya_ref[row, rows, :] + jnp.dot(
            attn.astype(_BF16), wo_ref[A_WIDTH:A_WIDTH + B_WIDTH, :], preferred_element_type=_F32)
        out = x_ref[row, rows, :] + gate * y
        y_ref[row, rows, :] = _rmsnorm(out, fg_ref[...])


def _back_call(packed, kctx, gb, x, mod, mod_index, wuv2, wo_bf, final_g, rows_per_step):
    batch, seq, _ = x.shape
    tq = min(TOKEN_TILE, seq)
    has_ctx = kctx is not None
    rows = rows_per_step
    assert batch % rows == 0 and (rows == 1 or seq == tq)
    full = lambda shape: pl.BlockSpec(shape, lambda b, i: (0,) * len(shape))
    keys = lambda arr: pl.BlockSpec((rows, arr.shape[1], arr.shape[2]), lambda b, i: (b, 0, 0))
    in_specs = [
        pl.BlockSpec((rows, tq, MLA_HEADS * QK_WIDTH),
                     lambda b, i: (b, i, PK_Q // (MLA_HEADS * QK_WIDTH))),
        pl.BlockSpec((rows, seq, QK_WIDTH), lambda b, i: (b, 0, PK_K // QK_WIDTH)),
    ]
    args = [packed, packed]
    if has_ctx:
        in_specs += [keys(kctx)]
        args += [kctx]
    in_specs += [
        pl.BlockSpec((rows, tq, A_WIDTH), lambda b, i: (b, i, PK_A // A_WIDTH)),
        pl.BlockSpec((rows, tq, B_WIDTH), lambda b, i: (b, i, 0)),
        pl.BlockSpec((rows, tq, D_MODEL), lambda b, i: (b, i, 0)),
        pl.BlockSpec((SUBLANES, 3 * D_MODEL), lambda b, i: (mod_index(b * rows), 0)),
        full((MLA_HEADS // 2, 2 * KV_RANK, 2 * V_HEAD)),
        full((D_MODEL, D_MODEL)),
        full((1, D_MODEL)),
    ]
    args += [packed, gb, x, mod, wuv2, wo_bf, final_g]
    return pl.pallas_call(
        functools.partial(_back_body, has_ctx=has_ctx),
        grid=(batch // rows, seq // tq),
        in_specs=in_specs,
        out_specs=pl.BlockSpec((rows, tq, D_MODEL), lambda b, i: (b, i, 0)),
        out_shape=jax.ShapeDtypeStruct((batch, seq, D_MODEL), _F32),
        scratch_shapes=[
            pltpu.VMEM((rows * MLA_HEADS, QK_WIDTH, tq), _BF16),
            pltpu.VMEM((rows, tq, D_MODEL), _F32),
        ],
        compiler_params=pltpu.CompilerParams(
            dimension_semantics=("arbitrary", "arbitrary"),
            vmem_limit_bytes=VMEM_LIMIT_BYTES),
        name="back_lat" if has_ctx else "back_ctx",
    )(*args)


def _rope_tables(n_tokens, rotate):
    if rotate:
        rows = n_tokens // GRID_W
        inv = ROPE_THETA ** (-jnp.arange(AXIS_PAIRS, dtype=_F32) / AXIS_PAIRS)
        ang_r = jnp.arange(rows, dtype=_F32)[:, None] * inv
        ang_c = jnp.arange(GRID_W, dtype=_F32)[:, None] * inv

        def grid_table(fn):
            tr = jnp.broadcast_to(fn(ang_r)[:, None, :], (rows, GRID_W, AXIS_PAIRS))
            tc = jnp.broadcast_to(fn(ang_c)[None, :, :], (rows, GRID_W, AXIS_PAIRS))
            half = jnp.concatenate([tr, tc], axis=-1).reshape(n_tokens, 2 * AXIS_PAIRS)
            return jnp.concatenate([half, half], axis=-1)

        cos, sin = grid_table(jnp.cos), grid_table(jnp.sin)
    else:
        cos, sin = jnp.ones((n_tokens, QK_ROPE), _F32), jnp.zeros((n_tokens, QK_ROPE), _F32)
    return jnp.concatenate([cos, sin], axis=-1)


def kernel(x_prompt, x_sample, cache_ckv, cache_krope, c, c_ctx, norm_g, w_ada, b_ada, w_in, w_s,
           b_s, g_v, q_norm_g, w_uq, kv_norm_g, w_ukv, w_o, final_g):
    depth = norm_g.shape[0]
    assert depth == 1 and w_in.shape[2] == IN_WIDTH
    dec_batch = x_sample.shape[0]
    xp, xs = x_prompt, x_sample
    new_ckv, new_kr = [], []
    for l in range(depth):
        cond = jnp.concatenate(
            [c, c_ctx[None, :], jnp.zeros((16 - dec_batch - 1, D_MODEL), _F32)], axis=0)
        mod, win2, wq2, wuv2 = _weights_call(cond, w_ada[l], b_ada[l], jnp.transpose(w_in[l]),
                                             w_uq[l], w_ukv[l])
        wo_bf = w_o[l].astype(_BF16)
        ws_bf = w_s[l].astype(_BF16)
        bs_t = b_s[l].T
        gv_row = g_v[l].reshape(1, A_WIDTH)
        shared = (norm_g[l].reshape(1, D_MODEL), win2, ws_bf, bs_t, gv_row,
                  q_norm_g[l].reshape(1, Q_RANK), wq2, kv_norm_g[l].reshape(1, KV_RANK))
        fg = final_g.reshape(1, D_MODEL)

        ctx_index = lambda b: dec_batch
        ctx_batch, ctx_seq, _ = xp.shape
        pack = 2 * SUB_TILE // ctx_seq
        pk_c, gb_c, ckv_c, kr_c = _front_call(
            xp.reshape(ctx_batch // pack, pack * ctx_seq, D_MODEL), mod, ctx_index, *shared,
            _rope_tables(ctx_seq, False), True)
        unpack = lambda arr: arr.reshape(ctx_batch, ctx_seq, arr.shape[-1])
        xp = _back_call(unpack(pk_c), None, unpack(gb_c), xp, mod, ctx_index, wuv2, wo_bf, fg,
                        rows_per_step=pack)
        new_ckv.append(ckv_c.reshape(ctx_batch, 1, ctx_seq, KV_RANK))
        assert ctx_seq == SUB_TILE
        new_kr.append(jnp.swapaxes(kr_c, 2, 3))

        lat_index = lambda b: b
        pk_s, gb_s = _front_call(
            xs, mod, lat_index, *shared, _rope_tables(xs.shape[1], True), False)
        cache_k = jnp.concatenate(
            [cache_ckv[:, l], jnp.zeros(cache_krope[:, l].shape, _F32), cache_krope[:, l]],
            axis=-1).astype(_BF16)
        xs = _back_call(pk_s, cache_k, gb_s, xs, mod, lat_index, wuv2, wo_bf, fg, rows_per_step=1)
    return (xp, xs, jnp.concatenate(new_ckv, axis=1), jnp.concatenate(new_kr, axis=1))
```

```python
import functools
import math

import jax
import jax.numpy as jnp
from jax import lax
from jax.experimental import pallas as pl
from jax.experimental.pallas import tpu as pltpu

D_MODEL = 1024
GRID_W = 64
EPS = 1e-6
A_HEADS = 4
A_HEAD_DIM = 128
A_WIDTH = A_HEADS * A_HEAD_DIM
CHUNK = 128
MLA_HEADS = 4
QK_NOPE = 128
QK_ROPE = 64
V_HEAD = 128
B_WIDTH = MLA_HEADS * V_HEAD
Q_RANK = 256
KV_RANK = 128
AXIS_PAIRS = QK_ROPE // 4
ROPE_THETA = 10000.0
ATTN_SCALE = 1.0 / math.sqrt(QK_NOPE + QK_ROPE)
LOG2_E = 1.4426950408889634

QK_WIDTH = KV_RANK + 2 * QK_ROPE
OFF_U = 0
OFF_V = OFF_U + A_WIDTH
OFF_GA = OFF_V + A_WIDTH
OFF_CQ = OFF_GA + A_WIDTH
OFF_CKV = OFF_CQ + Q_RANK
OFF_KR = OFF_CKV + KV_RANK
OFF_GB = OFF_KR + 2 * QK_ROPE
IN_WIDTH = OFF_GB - QK_ROPE + B_WIDTH
IN_WIDTH2 = OFF_GB + B_WIDTH

PK_Q = 0
PK_A = PK_Q + MLA_HEADS * QK_WIDTH
PK_K = PK_A + A_WIDTH
PK_WIDTH = PK_K + QK_WIDTH

LANES = 128
SUBLANES = 8
ONES_ROWS = 16
SUB_TILE = 512
TOKEN_TILE = 512
CTX_ROWS_PER_STEP = 2
PROJ_BLOCK = 256
OUT_ROWS = 256
KEY_CHUNK = 256
SCORE_LOOKAHEAD = 3
VMEM_LIMIT_BYTES = 56 * 1024 * 1024

_F32 = jnp.float32
_BF16 = jnp.bfloat16
_NT_DIMS = (((1,), (1,)), ((), ()))


def _silu(x):
    hx = 0.5 * x
    return hx + hx * jnp.tanh(hx)


def _gelu_tanh(x):
    return x * (0.5 * (1.0 + jnp.tanh(0.7978845608028654 * (x + 0.044715 * (x * x * x)))))


def _rmsnorm(x, g):
    ms = jnp.mean(x * x, axis=-1, keepdims=True)
    return (x * lax.rsqrt(ms + EPS)) * g


def _mod_body(cond_ref, wa_ref, wb_ref, b_ref, o_ref):
    s = _silu(cond_ref[...]).astype(_BF16)
    halves = [jnp.dot(s, w_ref[...].astype(_BF16), preferred_element_type=_F32)
              for w_ref in (wa_ref, wb_ref)]
    mod = jnp.concatenate(halves, axis=1) + b_ref[...]
    rows, cols = mod.shape
    o_ref[...] = jnp.broadcast_to(mod[:, None, :], (rows, SUBLANES, cols)).reshape(
        rows * SUBLANES, cols)


def _fold_body(wq_ref, wkv_ref, q_ref, uv_ref):
    half, quarter = LANES // 2, LANES // 4
    assert QK_NOPE == LANES and QK_ROPE == half and V_HEAD == LANES
    lane = lax.broadcasted_iota(jnp.int32, (Q_RANK, LANES), 1)
    low = lane < half

    def cols(block):
        return wq_ref[:, block * LANES:(block + 1) * LANES]

    for hd in range(MLA_HEADS):
        start = hd * (QK_NOPE + QK_ROPE)
        b0, odd = divmod(start, LANES)
        if odd == 0:
            nope = cols(b0)
            rope_blk = cols(b0 + 1)
            rope_lo = rope_blk
            rope_hi = pltpu.roll(rope_blk, half, 1)
        else:
            assert odd == half
            nope = jnp.where(low, pltpu.roll(cols(b0), half, 1), pltpu.roll(cols(b0 + 1), half, 1))
            rope_blk = cols(b0 + 1)
            rope_lo = pltpu.roll(rope_blk, half, 1)
            rope_hi = rope_blk
        partner = jnp.where(lane < half + quarter,
                            -pltpu.roll(rope_hi, LANES - quarter, 1),
                            pltpu.roll(rope_hi, quarter, 1))
        w_uk = wkv_ref[:, hd * (QK_NOPE + V_HEAD):hd * (QK_NOPE + V_HEAD) + QK_NOPE]
        absorbed = lax.dot_general(nope, w_uk, _NT_DIMS, precision=lax.Precision.HIGHEST,
                                   preferred_element_type=_F32)
        base = hd * QK_WIDTH
        q_ref[:, base:base + KV_RANK] = absorbed.astype(_BF16)
        q_ref[:, base + KV_RANK:base + QK_WIDTH] = jnp.where(low, rope_lo, partner).astype(_BF16)

    uv_ref[...] = jnp.zeros(uv_ref.shape, _BF16)
    for hd in range(MLA_HEADS):
        w_uv = wkv_ref[:, hd * (QK_NOPE + V_HEAD) + QK_NOPE:(hd + 1) * (QK_NOPE + V_HEAD)]
        pair, pos = divmod(hd, 2)
        uv_ref[pair, pos * KV_RANK:(pos + 1) * KV_RANK, pos * V_HEAD:(pos + 1) * V_HEAD] = (
            w_uv.astype(_BF16))


_KR_BLOCK = OFF_KR // LANES


_PREP_SPLIT = IN_WIDTH2 // 2


def _prep_body(step, w_ref, o_ref):
    half = QK_ROPE // 2
    blocks_per_step = _PREP_SPLIT // LANES
    base = step * _PREP_SPLIT
    for j in range(step * blocks_per_step, (step + 1) * blocks_per_step):
        if j < _KR_BLOCK:
            blk = w_ref[j * LANES - base:(j + 1) * LANES - base, :]
        elif j == _KR_BLOCK:
            kr = w_ref[OFF_KR - base:OFF_KR - base + QK_ROPE, :]
            partner = jnp.concatenate([-kr[half:, :], kr[:half, :]], axis=0)
            blk = jnp.concatenate([kr, partner], axis=0)
        else:
            lo = j * LANES - QK_ROPE - base
            blk = w_ref[lo:lo + LANES, :]
        o_ref[:, j * LANES - base:(j + 1) * LANES - base] = blk.T.astype(_BF16)


def _weights_body(cond_ref, wa_ref, wb_ref, b_ref, win_ref, wq_ref, wkv_ref,
                  mod_ref, win2_ref, q_ref, uv_ref):
    j = pl.program_id(0)
    _mod_body(cond_ref, wa_ref, wb_ref, b_ref, mod_ref)
    n_prep = IN_WIDTH2 // _PREP_SPLIT
    for step in range(n_prep):
        pl.when(j == step)(functools.partial(_prep_body, step, win_ref, win2_ref))
    pl.when(j == n_prep)(functools.partial(_fold_body, wq_ref, wkv_ref, q_ref, uv_ref))


def _weights_call(cond, w_ada, b_ada, w_in_t, w_uq, w_ukv):
    rows = cond.shape[0]
    n_out = w_ada.shape[1]
    col_block = D_MODEL
    n_steps = n_out // col_block
    n_prep = IN_WIDTH2 // _PREP_SPLIT
    assert _KR_BLOCK * LANES == OFF_KR and OFF_KR >= _PREP_SPLIT and n_steps == n_prep + 1
    prep_step = lambda j: jnp.minimum(j, n_prep - 1)
    full = lambda shape: pl.BlockSpec(shape, lambda j: (0,) * len(shape))
    uv_shape = (MLA_HEADS // 2, 2 * KV_RANK, 2 * V_HEAD)
    return pl.pallas_call(
        _weights_body,
        grid=(n_steps,),
        in_specs=[
            full((rows, D_MODEL)),
            pl.BlockSpec((D_MODEL, col_block // 2), lambda j: (0, 2 * j)),
            pl.BlockSpec((D_MODEL, col_block // 2), lambda j: (0, 2 * j + 1)),
            pl.BlockSpec((1, col_block), lambda j: (0, j)),
            pl.BlockSpec((_PREP_SPLIT, D_MODEL), lambda j: (prep_step(j), 0)),
            full(w_uq.shape),
            full(w_ukv.shape),
        ],
        out_specs=[
            pl.BlockSpec((rows * SUBLANES, col_block), lambda j: (0, j)),
            pl.BlockSpec((D_MODEL, _PREP_SPLIT), lambda j: (0, prep_step(j))),
            full((Q_RANK, MLA_HEADS * QK_WIDTH)),
            full(uv_shape),
        ],
        out_shape=[
            jax.ShapeDtypeStruct((rows * SUBLANES, n_out), _F32),
            jax.ShapeDtypeStruct((D_MODEL, IN_WIDTH2), _BF16),
            jax.ShapeDtypeStruct((Q_RANK, MLA_HEADS * QK_WIDTH), _BF16),
            jax.ShapeDtypeStruct(uv_shape, _BF16),
        ],
        compiler_params=pltpu.CompilerParams(
            dimension_semantics=("arbitrary",), vmem_limit_bytes=VMEM_LIMIT_BYTES),
        name="weights",
    )(cond, w_ada, w_ada, b_ada.reshape(1, n_out), w_in_t, w_uq, w_ukv)


def _front_substep(x_ref, mod_ref, ng_ref, win_ref, ws_ref, bs_ref, gv_ref, qg_ref, wq_ref,
                   kvg_ref, tables, out_refs, row0, h_write, h_read, z_write, z_read, cache_seq):
    if cache_seq:
        pk_ref, gb_ref, ckv_ref, kr_ref = out_refs
    else:
        pk_ref, gb_ref = out_refs
    first_half = lax.broadcasted_iota(jnp.int32, tables.shape, 1) < QK_ROPE
    t1 = jnp.where(first_half, tables, 0.0)
    t2k = jnp.where(first_half, pltpu.roll(tables, QK_ROPE, 1), 0.0)
    t2q = jnp.where(first_half, t2k, 1.0)
    rows = slice(row0, row0 + SUB_TILE)
    half = SUB_TILE // 2
    assert IN_WIDTH2 == 10 * PROJ_BLOCK

    def project(j):
        lo = j * PROJ_BLOCK
        z_write[:, lo:lo + PROJ_BLOCK] = jnp.dot(h_read[...], win_ref[:, lo:lo + PROJ_BLOCK],
                                                 preferred_element_type=_F32)

    shift = mod_ref[0:1, 0:D_MODEL]
    gain = ng_ref[...] * (1.0 + mod_ref[0:1, D_MODEL:2 * D_MODEL])

    def norm_rows(lo):
        x = x_ref[0, row0 + lo:row0 + lo + half, :]
        ms = jnp.mean(x * x, axis=-1, keepdims=True)
        h_write[lo:lo + half, :] = ((x * lax.rsqrt(ms + EPS)) * gain + shift).astype(_BF16)

    def vn_head(hd):
        lo = hd * A_HEAD_DIM
        v = _gelu_tanh(z_read[:, OFF_V + lo:OFF_V + lo + A_HEAD_DIM])
        return _rmsnorm(v, gv_ref[:, lo:lo + A_HEAD_DIM]).astype(_BF16)

    def mix_head(hd, vn):
        bias = bs_ref[:, hd:hd + 1]
        parts = [jnp.dot(ws_ref[hd], vn[c * CHUNK:(c + 1) * CHUNK, :],
                         preferred_element_type=_F32) + bias for c in range(SUB_TILE // CHUNK)]
        return jnp.concatenate(parts, axis=0)

    def gate_a(hd, mixed):
        lo = hd * A_HEAD_DIM
        u = _gelu_tanh(z_read[:, OFF_U + lo:OFF_U + lo + A_HEAD_DIM])
        g = _silu(z_read[:, OFF_GA + lo:OFF_GA + lo + A_HEAD_DIM])
        pk_ref[0, rows, PK_A + lo:PK_A + lo + A_HEAD_DIM] = (u * mixed * g).astype(_BF16)

    def gate_b(lo, width):
        gb_ref[0, rows, lo:lo + width] = _silu(z_read[:, OFF_GB + lo:OFF_GB + lo + width])

    def rope_q(qa, hd):
        base = hd * QK_WIDTH
        pair = qa[:, base + KV_RANK:base + QK_WIDTH]
        rope = pair * t1 + pltpu.roll(pair, QK_ROPE, 1) * t2q
        pk_ref[0, rows, PK_Q + base:PK_Q + base + QK_WIDTH] = jnp.concatenate(
            [qa[:, base:base + KV_RANK], rope], axis=1).astype(_BF16)

    project(0)
    cq = _rmsnorm(z_read[:, OFF_CQ:OFF_CQ + Q_RANK],
                  qg_ref[...] * (ATTN_SCALE * LOG2_E)).astype(_BF16)
    vn0 = vn_head(0)
    project(1)
    vn1 = vn_head(1)
    gate_b(0, B_WIDTH // 2)
    project(2)
    vn2 = vn_head(2)
    gate_b(B_WIDTH // 2, B_WIDTH // 2)
    project(3)
    vn3 = vn_head(3)
    ckv = _rmsnorm(z_read[:, OFF_CKV:OFF_CKV + KV_RANK], kvg_ref[...])
    kpair = z_read[:, OFF_KR:OFF_KR + 2 * QK_ROPE]
    krot = kpair * t1 + pltpu.roll(kpair, QK_ROPE, 1) * t2k
    pk_ref[0, rows, PK_K:PK_K + QK_WIDTH] = jnp.concatenate([ckv, krot], axis=1).astype(_BF16)
    if cache_seq:
        ckv_ref[0, 0, rows, :] = ckv
        kpair_t = kpair.T
        for seq_lo in range(0, SUB_TILE, cache_seq):
            kr_ref[(row0 + seq_lo) // cache_seq, 0] = kpair_t[0:QK_ROPE, seq_lo:seq_lo + cache_seq]
    qa = jnp.dot(cq, wq_ref[...], preferred_element_type=_F32)
    norm_rows(0)
    project(4)
    rope_q(qa, 0)
    rope_q(qa, 1)
    mixed0 = mix_head(0, vn0)
    mixed1 = mix_head(1, vn1)
    rope_q(qa, 2)
    rope_q(qa, 3)
    project(5)
    gate_a(0, mixed0)
    mixed2 = mix_head(2, vn2)
    mixed3 = mix_head(3, vn3)
    project(6)
    gate_a(1, mixed1)
    norm_rows(half)
    project(7)
    gate_a(2, mixed2)
    project(8)
    gate_a(3, mixed3)
    project(9)


def _front_body(x_ref, mod_ref, ng_ref, win_ref, ws_ref, bs_ref, gv_ref, qg_ref, wq_ref,
                kvg_ref, tab_ref, *rest, cache_seq, tiles_per_row):
    out_refs, (h_a, h_b, z_a, z_b) = rest[:-4], rest[-4:]
    t = pl.program_id(0)
    tile = x_ref.shape[1]
    assert tile == 2 * SUB_TILE
    n_pos = tab_ref.shape[0]

    @pl.when(t == 0)
    def _():
        h_b[...] = jnp.zeros(h_b.shape, _BF16)
        z_b[...] = jnp.zeros(z_b.shape, _F32)

    pos_base = (jnp.maximum(t - 1, 0) % tiles_per_row) * tile
    for sub, (h_write, h_read, z_write, z_read) in enumerate(
            ((h_a, h_b, z_a, z_b), (h_b, h_a, z_b, z_a))):
        row0 = sub * SUB_TILE
        pos = pl.multiple_of((pos_base + row0) % n_pos, SUB_TILE)
        tables = tab_ref[pl.ds(pos, SUB_TILE), :]
        _front_substep(x_ref, mod_ref, ng_ref, win_ref, ws_ref, bs_ref, gv_ref, qg_ref, wq_ref,
                       kvg_ref, tables, out_refs, row0, h_write, h_read, z_write, z_read,
                       cache_seq)


def _front_call(x, mod, mod_index, norm_g, win2, ws_bf, bs_t, g_v, q_norm_g, wq2, kv_norm_g,
                tables, cache_seq):
    batch, seq, _ = x.shape
    tile = 2 * SUB_TILE
    tiles_per_row = seq // tile
    n_tiles = batch * tiles_per_row
    assert seq % tile == 0 and tables.shape[0] % SUB_TILE == 0

    def in_tile(t):
        tt = jnp.minimum(t, n_tiles - 1)
        return tt // tiles_per_row, tt % tiles_per_row

    def out_tile(t):
        tt = jnp.maximum(t - 1, 0)
        return tt // tiles_per_row, tt % tiles_per_row

    full = lambda shape: pl.BlockSpec(shape, lambda t: (0,) * len(shape))
    in_specs = [
        pl.BlockSpec((1, tile, D_MODEL), lambda t: (*in_tile(t), 0)),
        pl.BlockSpec((SUBLANES, 3 * D_MODEL), lambda t: (mod_index(in_tile(t)[0]), 0)),
        full((1, D_MODEL)),
        full((D_MODEL, IN_WIDTH2)),
        full((A_HEADS, CHUNK, CHUNK)),
        full((CHUNK, A_HEADS)),
        full((1, A_WIDTH)),
        full((1, Q_RANK)),
        full((Q_RANK, MLA_HEADS * QK_WIDTH)),
        full((1, KV_RANK)),
        full(tables.shape),
    ]
    out_shape = [
        jax.ShapeDtypeStruct((batch, seq, PK_WIDTH), _BF16),
        jax.ShapeDtypeStruct((batch, seq, B_WIDTH), _F32),
    ]
    out_specs = [
        pl.BlockSpec((1, tile, PK_WIDTH), lambda t: (*out_tile(t), 0)),
        pl.BlockSpec((1, tile, B_WIDTH), lambda t: (*out_tile(t), 0)),
    ]
    if cache_seq:
        assert SUB_TILE % cache_seq == 0
        out_shape += [
            jax.ShapeDtypeStruct((batch, 1, seq, KV_RANK), _F32),
            jax.ShapeDtypeStruct((batch * seq // cache_seq, 1, QK_ROPE, cache_seq), _F32),
        ]
        out_specs += [
            pl.BlockSpec((1, 1, tile, KV_RANK), lambda t: (out_tile(t)[0], 0, out_tile(t)[1], 0)),
            pl.BlockSpec((tile // cache_seq, 1, QK_ROPE, cache_seq),
                         lambda t: (jnp.maximum(t - 1, 0), 0, 0, 0)),
        ]
    return pl.pallas_call(
        functools.partial(_front_body, cache_seq=cache_seq, tiles_per_row=tiles_per_row),
        grid=(n_tiles + 1,),
        in_specs=in_specs,
        out_specs=out_specs,
        out_shape=out_shape,
        scratch_shapes=[pltpu.VMEM((SUB_TILE, D_MODEL), _BF16), pltpu.VMEM((SUB_TILE, D_MODEL), _BF16),
                        pltpu.VMEM((SUB_TILE, IN_WIDTH2), _F32),
                        pltpu.VMEM((SUB_TILE, IN_WIDTH2), _F32)],
        compiler_params=pltpu.CompilerParams(
            dimension_semantics=("arbitrary",),
            vmem_limit_bytes=VMEM_LIMIT_BYTES),
        name="front_ctx" if cache_seq else "front_lat",
    )(x, mod, norm_g, win2, ws_bf, bs_t, g_v, q_norm_g, wq2, kv_norm_g, tables)


def _key_chunks(k_refs):
    chunks = []
    for k_ref in k_refs:
        size = min(KEY_CHUNK, k_ref.shape[1])
        assert k_ref.shape[1] % size == 0
        chunks += [(k_ref, lo, size) for lo in range(0, k_ref.shape[1], size)]
    return chunks


def _values_t(k_ref, row):
    ckv_t = k_ref[row, :, 0:KV_RANK].T
    return jnp.concatenate([ckv_t, jnp.ones((ONES_ROWS, ckv_t.shape[1]), _BF16)], axis=0)


def _attention(q_ref, k_refs, qt_ref, fillers):
    n_rows = q_ref.shape[0]
    chunks = _key_chunks(k_refs)
    steps = [(row, hd, c) for row in range(n_rows) for hd in range(MLA_HEADS)
             for c in range(len(chunks))]

    for row in range(n_rows):
        for hd in range(MLA_HEADS):
            qt_ref[row * MLA_HEADS + hd] = q_ref[row, :, hd * QK_WIDTH:(hd + 1) * QK_WIDTH].T

    def logits(row, hd, c):
        k_ref, lo, size = chunks[c]
        return jnp.dot(k_ref[row, lo:lo + size, :], qt_ref[row * MLA_HEADS + hd],
                       preferred_element_type=_F32)

    pending = [logits(*step) for step in steps[:SCORE_LOOKAHEAD]]
    values_t = [{id(k_ref): _values_t(k_ref, row) for k_ref in k_refs} for row in range(n_rows)]
    outs = {}
    m = acc = None
    for j, (row, hd, c) in enumerate(steps):
        if j + SCORE_LOOKAHEAD < len(steps):
            pending.append(logits(*steps[j + SCORE_LOOKAHEAD]))
        s, pending[j] = pending[j], None
        k_ref, lo, size = chunks[c]
        chunk_max = jnp.max(s, axis=0, keepdims=True)
        if c > 0:
            m_new = jnp.maximum(m, chunk_max)
            acc = acc * jnp.exp2(m - m_new)
            m = m_new
        else:
            m = chunk_max
        p = jnp.exp2(s - m).astype(_BF16)
        part = jnp.dot(values_t[row][id(k_ref)][:, lo:lo + size], p, preferred_element_type=_F32)
        acc = acc + part if c > 0 else part
        if j >= len(steps) - len(fillers):
            fillers[j - len(steps) + len(fillers)]()
        if c == len(chunks) - 1:
            outs[row, hd] = (acc[0:KV_RANK, :] / acc[KV_RANK:KV_RANK + 1, :]).T
    return outs


def _attention_single(q_ref, k_ref, fillers):
    n_rows, tq = q_ref.shape[0], q_ref.shape[1]
    groups = [(row, pair) for row in range(n_rows) for pair in range(MLA_HEADS // 2)]
    scores = {}
    for row, pair in groups:
        q = jnp.concatenate([q_ref[row, :, hd * QK_WIDTH:(hd + 1) * QK_WIDTH]
                             for hd in (2 * pair, 2 * pair + 1)], axis=0)
        scores[row, pair] = lax.dot_general(q, k_ref[row], _NT_DIMS, preferred_element_type=_F32)
    for fill in fillers:
        fill()
    outs = {}
    for row, pair in groups:
        s = scores[row, pair]
        ckv = k_ref[row, :, 0:KV_RANK]
        values = jnp.concatenate([ckv, jnp.ones_like(ckv)], axis=-1)
        p = jnp.exp2(s - jnp.max(s, axis=1, keepdims=True)).astype(_BF16)
        acc = jnp.dot(p, values, preferred_element_type=_F32)
        o = acc[:, 0:KV_RANK] / acc[:, KV_RANK:2 * KV_RANK]
        outs[row, 2 * pair], outs[row, 2 * pair + 1] = o[0:tq, :], o[tq:2 * tq, :]
    return outs


def _back_body(*refs, has_ctx):
    if has_ctx:
        (q_ref, klat_ref, kctx_ref, a_ref, gb_ref, x_ref, mod_ref, wuv_ref, wo_ref, fg_ref,
         y_ref, qt_ref, ya_ref) = refs
        k_refs = (kctx_ref, klat_ref)
    else:
        (q_ref, klat_ref, a_ref, gb_ref, x_ref, mod_ref, wuv_ref, wo_ref, fg_ref, y_ref,
         qt_ref, ya_ref) = refs
        k_refs = (klat_ref,)
    n_rows, tq = q_ref.shape[0], q_ref.shape[1]

    block = min(OUT_ROWS, tq)
    blocks = [(row, lo) for row in range(n_rows) for lo in range(0, tq, block)]

    def a_part(row, lo):
        def run():
            ya_ref[row, lo:lo + block, :] = jnp.dot(
                a_ref[row, lo:lo + block, :], wo_ref[0:A_WIDTH, :], preferred_element_type=_F32)
        return run

    fillers = [a_part(*b) for b in blocks]
    if len(_key_chunks(k_refs)) == 1:
        outs = _attention_single(q_ref, k_refs[0], fillers)
    else:
        outs = _attention(q_ref, k_refs, qt_ref, fillers)

    gate = mod_ref[0:1, 2 * D_MODEL:3 * D_MODEL]
    for row, lo in blocks:
        rows = slice(lo, lo + block)
        attn_cols = []
        for pair in range(MLA_HEADS // 2):
            o2 = jnp.concatenate([outs[row, 2 * pair][rows, :],
                                  outs[row, 2 * pair + 1][rows, :]], axis=1)
            attn_cols.append(jnp.dot(o2.astype(_BF16), wuv_ref[pair],
                                     preferred_element_type=_F32))
        attn = jnp.concatenate(attn_cols, axis=1) * gb_ref[row, rows, :]
        y = ya_ref[row, rows, :] + jnp.dot(
            attn.astype(_BF16), wo_ref[A_WIDTH:A_WIDTH + B_WIDTH, :], preferred_element_type=_F32)
        out = x_ref[row, rows, :] + gate * y
        y_ref[row, rows, :] = _rmsnorm(out, fg_ref[...])


def _back_call(packed, kctx, gb, x, mod, mod_index, wuv2, wo_bf, final_g, rows_per_step):
    batch, seq, _ = x.shape
    tq = min(TOKEN_TILE, seq)
    has_ctx = kctx is not None
    rows = rows_per_step
    assert batch % rows == 0 and (rows == 1 or seq == tq)
    full = lambda shape: pl.BlockSpec(shape, lambda b, i: (0,) * len(shape))
    keys = lambda arr: pl.BlockSpec((rows, arr.shape[1], arr.shape[2]), lambda b, i: (b, 0, 0))
    in_specs = [
        pl.BlockSpec((rows, tq, MLA_HEADS * QK_WIDTH),
                     lambda b, i: (b, i, PK_Q // (MLA_HEADS * QK_WIDTH))),
        pl.BlockSpec((rows, seq, QK_WIDTH), lambda b, i: (b, 0, PK_K // QK_WIDTH)),
    ]
    args = [packed, packed]
    if has_ctx:
        in_specs += [keys(kctx)]
        args += [kctx]
    in_specs += [
        pl.BlockSpec((rows, tq, A_WIDTH), lambda b, i: (b, i, PK_A // A_WIDTH)),
        pl.BlockSpec((rows, tq, B_WIDTH), lambda b, i: (b, i, 0)),
        pl.BlockSpec((rows, tq, D_MODEL), lambda b, i: (b, i, 0)),
        pl.BlockSpec((SUBLANES, 3 * D_MODEL), lambda b, i: (mod_index(b * rows), 0)),
        full((MLA_HEADS // 2, 2 * KV_RANK, 2 * V_HEAD)),
        full((D_MODEL, D_MODEL)),
        full((1, D_MODEL)),
    ]
    args += [packed, gb, x, mod, wuv2, wo_bf, final_g]
    return pl.pallas_call(
        functools.partial(_back_body, has_ctx=has_ctx),
        grid=(batch // rows, seq // tq),
        in_specs=in_specs,
        out_specs=pl.BlockSpec((rows, tq, D_MODEL), lambda b, i: (b, i, 0)),
        out_shape=jax.ShapeDtypeStruct((batch, seq, D_MODEL), _F32),
        scratch_shapes=[
            pltpu.VMEM((rows * MLA_HEADS, QK_WIDTH, tq), _BF16),
            pltpu.VMEM((rows, tq, D_MODEL), _F32),
        ],
        compiler_params=pltpu.CompilerParams(
            dimension_semantics=("arbitrary", "arbitrary"),
            vmem_limit_bytes=VMEM_LIMIT_BYTES),
        name="back_lat" if has_ctx else "back_ctx",
    )(*args)


def _rope_tables(n_tokens, rotate):
    if rotate:
        rows = n_tokens // GRID_W
        inv = ROPE_THETA ** (-jnp.arange(AXIS_PAIRS, dtype=_F32) / AXIS_PAIRS)
        ang_r = jnp.arange(rows, dtype=_F32)[:, None] * inv
        ang_c = jnp.arange(GRID_W, dtype=_F32)[:, None] * inv

        def grid_table(fn):
            tr = jnp.broadcast_to(fn(ang_r)[:, None, :], (rows, GRID_W, AXIS_PAIRS))
            tc = jnp.broadcast_to(fn(ang_c)[None, :, :], (rows, GRID_W, AXIS_PAIRS))
            half = jnp.concatenate([tr, tc], axis=-1).reshape(n_tokens, 2 * AXIS_PAIRS)
            return jnp.concatenate([half, half], axis=-1)

        cos, sin = grid_table(jnp.cos), grid_table(jnp.sin)
    else:
        cos, sin = jnp.ones((n_tokens, QK_ROPE), _F32), jnp.zeros((n_tokens, QK_ROPE), _F32)
    return jnp.concatenate([cos, sin], axis=-1)


def kernel(x_prompt, x_sample, cache_ckv, cache_krope, c, c_ctx, norm_g, w_ada, b_ada, w_in, w_s,
           b_s, g_v, q_norm_g, w_uq, kv_norm_g, w_ukv, w_o, final_g):
    depth = norm_g.shape[0]
    assert depth == 1 and w_in.shape[2] == IN_WIDTH
    dec_batch = x_sample.shape[0]
    xp, xs = x_prompt, x_sample
    new_ckv, new_kr = [], []
    for l in range(depth):
        cond = jnp.concatenate(
            [c, c_ctx[None, :], jnp.zeros((16 - dec_batch - 1, D_MODEL), _F32)], axis=0)
        mod, win2, wq2, wuv2 = _weights_call(cond, w_ada[l], b_ada[l], jnp.transpose(w_in[l]),
                                             w_uq[l], w_ukv[l])
        wo_bf = w_o[l].astype(_BF16)
        ws_bf = w_s[l].astype(_BF16)
        bs_t = b_s[l].T
        gv_row = g_v[l].reshape(1, A_WIDTH)
        shared = (norm_g[l].reshape(1, D_MODEL), win2, ws_bf, bs_t, gv_row,
                  q_norm_g[l].reshape(1, Q_RANK), wq2, kv_norm_g[l].reshape(1, KV_RANK))
        fg = final_g.reshape(1, D_MODEL)

        ctx_index = lambda b: dec_batch
        ctx_batch, ctx_seq, _ = xp.shape
        pack = 2 * SUB_TILE // ctx_seq
        pk_c, gb_c, ckv_c, kr_c = _front_call(
            xp.reshape(ctx_batch // pack, pack * ctx_seq, D_MODEL), mod, ctx_index, *shared,
            _rope_tables(SUB_TILE, False), ctx_seq)
        unpack = lambda arr: arr.reshape(ctx_batch, ctx_seq, arr.shape[-1])
        xp = _back_call(unpack(pk_c), None, unpack(gb_c), xp, mod, ctx_index, wuv2, wo_bf, fg,
                        rows_per_step=CTX_ROWS_PER_STEP)
        new_ckv.append(ckv_c.reshape(ctx_batch, 1, ctx_seq, KV_RANK))
        new_kr.append(jnp.swapaxes(kr_c, 2, 3))

        lat_index = lambda b: b
        pk_s, gb_s = _front_call(
            xs, mod, lat_index, *shared, _rope_tables(xs.shape[1], True), None)
        cache_k = jnp.concatenate(
            [cache_ckv[:, l], jnp.zeros(cache_krope[:, l].shape, _F32), cache_krope[:, l]],
            axis=-1).astype(_BF16)
        xs = _back_call(pk_s, cache_k, gb_s, xs, mod, lat_index, wuv2, wo_bf, fg, rows_per_step=1)
    return (xp, xs, jnp.concatenate(new_ckv, axis=1), jnp.concatenate(new_kr, axis=1))
```

```python
import functools
import math

import jax
import jax.numpy as jnp
import numpy as np
from jax import lax
from jax.experimental import pallas as pl
from jax.experimental.pallas import tpu as pltpu

D_MODEL = 1024
GRID_W = 64
EPS = 1e-6
A_HEADS = 4
A_HEAD_DIM = 128
A_WIDTH = A_HEADS * A_HEAD_DIM
CHUNK = 128
MLA_HEADS = 4
QK_NOPE = 128
QK_ROPE = 64
V_HEAD = 128
B_WIDTH = MLA_HEADS * V_HEAD
Q_RANK = 256
KV_RANK = 128
AXIS_PAIRS = QK_ROPE // 4
ROPE_THETA = 10000.0
ATTN_SCALE = 1.0 / math.sqrt(QK_NOPE + QK_ROPE)
LOG2_E = 1.4426950408889634

QK_WIDTH = KV_RANK + 2 * QK_ROPE
OFF_U = 0
OFF_V = OFF_U + A_WIDTH
OFF_GA = OFF_V + A_WIDTH
OFF_CQ = OFF_GA + A_WIDTH
OFF_CKV = OFF_CQ + Q_RANK
OFF_KR = OFF_CKV + KV_RANK
OFF_GB = OFF_KR + 2 * QK_ROPE
IN_WIDTH = OFF_GB - QK_ROPE + B_WIDTH
IN_WIDTH2 = OFF_GB + B_WIDTH

PK_Q = 0
PK_A = PK_Q + MLA_HEADS * QK_WIDTH
PK_K = PK_A + A_WIDTH
PK_WIDTH = PK_K + QK_WIDTH

LANES = 128
SUBLANES = 8
ONES_ROWS = 16
SUB_TILE = 256
TOKEN_TILE = 512
PROJ_BLOCK = 256
OUT_ROWS = 256
KEY_CHUNK = 256
SCORE_LOOKAHEAD = 3
VMEM_LIMIT_BYTES = 56 * 1024 * 1024

_F32 = jnp.float32
_BF16 = jnp.bfloat16
_NT_DIMS = (((1,), (1,)), ((), ()))


def _silu(x):
    hx = 0.5 * x
    return hx + hx * jnp.tanh(hx)


def _gelu_tanh(x):
    return x * (0.5 * (1.0 + jnp.tanh(0.7978845608028654 * (x + 0.044715 * (x * x * x)))))


def _rmsnorm(x, g):
    ms = jnp.mean(x * x, axis=-1, keepdims=True)
    return (x * lax.rsqrt(ms + EPS)) * g


def _mod_body(cond_ref, wa_ref, wb_ref, b_ref, o_ref):
    s = _silu(cond_ref[...]).astype(_BF16)
    halves = [jnp.dot(s, w_ref[...].astype(_BF16), preferred_element_type=_F32)
              for w_ref in (wa_ref, wb_ref)]
    mod = jnp.concatenate(halves, axis=1) + b_ref[...]
    rows, cols = mod.shape
    o_ref[...] = jnp.broadcast_to(mod[:, None, :], (rows, SUBLANES, cols)).reshape(
        rows * SUBLANES, cols)


def _fold_body(wq_ref, wkv_ref, q_ref, uv_ref):
    half, quarter = LANES // 2, LANES // 4
    assert QK_NOPE == LANES and QK_ROPE == half and V_HEAD == LANES
    lane = lax.broadcasted_iota(jnp.int32, (Q_RANK, LANES), 1)
    low = lane < half

    def cols(block):
        return wq_ref[:, block * LANES:(block + 1) * LANES]

    for hd in range(MLA_HEADS):
        start = hd * (QK_NOPE + QK_ROPE)
        b0, odd = divmod(start, LANES)
        if odd == 0:
            nope = cols(b0)
            rope_blk = cols(b0 + 1)
            rope_lo = rope_blk
            rope_hi = pltpu.roll(rope_blk, half, 1)
        else:
            assert odd == half
            nope = jnp.where(low, pltpu.roll(cols(b0), half, 1), pltpu.roll(cols(b0 + 1), half, 1))
            rope_blk = cols(b0 + 1)
            rope_lo = pltpu.roll(rope_blk, half, 1)
            rope_hi = rope_blk
        partner = jnp.where(lane < half + quarter,
                            -pltpu.roll(rope_hi, LANES - quarter, 1),
                            pltpu.roll(rope_hi, quarter, 1))
        w_uk = wkv_ref[:, hd * (QK_NOPE + V_HEAD):hd * (QK_NOPE + V_HEAD) + QK_NOPE]
        absorbed = lax.dot_general(nope, w_uk, _NT_DIMS, precision=lax.Precision.HIGHEST,
                                   preferred_element_type=_F32)
        base = hd * QK_WIDTH
        q_ref[:, base:base + KV_RANK] = absorbed.astype(_BF16)
        q_ref[:, base + KV_RANK:base + QK_WIDTH] = jnp.where(low, rope_lo, partner).astype(_BF16)

    uv_ref[...] = jnp.zeros(uv_ref.shape, _BF16)
    for hd in range(MLA_HEADS):
        w_uv = wkv_ref[:, hd * (QK_NOPE + V_HEAD) + QK_NOPE:(hd + 1) * (QK_NOPE + V_HEAD)]
        pair, pos = divmod(hd, 2)
        uv_ref[pair, pos * KV_RANK:(pos + 1) * KV_RANK, pos * V_HEAD:(pos + 1) * V_HEAD] = (
            w_uv.astype(_BF16))


_KR_BLOCK = OFF_KR // LANES


_PREP_SPLIT = IN_WIDTH2 // 2


def _prep_body(step, w_ref, o_ref):
    half = QK_ROPE // 2
    blocks_per_step = _PREP_SPLIT // LANES
    base = step * _PREP_SPLIT
    for j in range(step * blocks_per_step, (step + 1) * blocks_per_step):
        if j < _KR_BLOCK:
            blk = w_ref[j * LANES - base:(j + 1) * LANES - base, :]
        elif j == _KR_BLOCK:
            kr = w_ref[OFF_KR - base:OFF_KR - base + QK_ROPE, :]
            partner = jnp.concatenate([-kr[half:, :], kr[:half, :]], axis=0)
            blk = jnp.concatenate([kr, partner], axis=0)
        else:
            lo = j * LANES - QK_ROPE - base
            blk = w_ref[lo:lo + LANES, :]
        o_ref[:, j * LANES - base:(j + 1) * LANES - base] = blk.T.astype(_BF16)


def _weights_body(cond_ref, wa_ref, wb_ref, b_ref, win_ref, wq_ref, wkv_ref,
                  mod_ref, win2_ref, q_ref, uv_ref):
    j = pl.program_id(0)
    _mod_body(cond_ref, wa_ref, wb_ref, b_ref, mod_ref)
    n_prep = IN_WIDTH2 // _PREP_SPLIT
    for step in range(n_prep):
        pl.when(j == step)(functools.partial(_prep_body, step, win_ref, win2_ref))
    pl.when(j == n_prep)(functools.partial(_fold_body, wq_ref, wkv_ref, q_ref, uv_ref))


def _weights_call(cond, w_ada, b_ada, w_in_t, w_uq, w_ukv):
    rows = cond.shape[0]
    n_out = w_ada.shape[1]
    col_block = D_MODEL
    n_steps = n_out // col_block
    n_prep = IN_WIDTH2 // _PREP_SPLIT
    assert _KR_BLOCK * LANES == OFF_KR and OFF_KR >= _PREP_SPLIT and n_steps == n_prep + 1
    prep_step = lambda j: jnp.minimum(j, n_prep - 1)
    full = lambda shape: pl.BlockSpec(shape, lambda j: (0,) * len(shape))
    uv_shape = (MLA_HEADS // 2, 2 * KV_RANK, 2 * V_HEAD)
    return pl.pallas_call(
        _weights_body,
        grid=(n_steps,),
        in_specs=[
            full((rows, D_MODEL)),
            pl.BlockSpec((D_MODEL, col_block // 2), lambda j: (0, 2 * j)),
            pl.BlockSpec((D_MODEL, col_block // 2), lambda j: (0, 2 * j + 1)),
            pl.BlockSpec((1, col_block), lambda j: (0, j)),
            pl.BlockSpec((_PREP_SPLIT, D_MODEL), lambda j: (prep_step(j), 0)),
            full(w_uq.shape),
            full(w_ukv.shape),
        ],
        out_specs=[
            pl.BlockSpec((rows * SUBLANES, col_block), lambda j: (0, j)),
            pl.BlockSpec((D_MODEL, _PREP_SPLIT), lambda j: (0, prep_step(j))),
            full((Q_RANK, MLA_HEADS * QK_WIDTH)),
            full(uv_shape),
        ],
        out_shape=[
            jax.ShapeDtypeStruct((rows * SUBLANES, n_out), _F32),
            jax.ShapeDtypeStruct((D_MODEL, IN_WIDTH2), _BF16),
            jax.ShapeDtypeStruct((Q_RANK, MLA_HEADS * QK_WIDTH), _BF16),
            jax.ShapeDtypeStruct(uv_shape, _BF16),
        ],
        compiler_params=pltpu.CompilerParams(
            dimension_semantics=("arbitrary",), vmem_limit_bytes=VMEM_LIMIT_BYTES),
        name="weights",
    )(cond, w_ada, w_ada, b_ada.reshape(1, n_out), w_in_t, w_uq, w_ukv)


def _front_substep(x_ref, mod_ref, ng_ref, win_ref, ws_ref, bs_ref, gv_ref, qg_ref, wq_ref,
                   kvg_ref, tables, out_refs, row0, h_write, h_read, z_write, z_read, emit_cache):
    if emit_cache:
        pk_ref, gb_ref, ckv_ref, kr_ref = out_refs
    else:
        pk_ref, gb_ref = out_refs
    first_half = lax.broadcasted_iota(jnp.int32, tables.shape, 1) < QK_ROPE
    t1 = jnp.where(first_half, tables, 0.0)
    t2k = jnp.where(first_half, pltpu.roll(tables, QK_ROPE, 1), 0.0)
    t2q = jnp.where(first_half, t2k, 1.0)
    rows = slice(row0, row0 + SUB_TILE)
    half = SUB_TILE // 2
    assert IN_WIDTH2 == 10 * PROJ_BLOCK

    def project(j):
        lo = j * PROJ_BLOCK
        z_write[:, lo:lo + PROJ_BLOCK] = jnp.dot(h_read[...], win_ref[:, lo:lo + PROJ_BLOCK],
                                                 preferred_element_type=_F32)

    shift = mod_ref[0:1, 0:D_MODEL]
    gain = ng_ref[...] * (1.0 + mod_ref[0:1, D_MODEL:2 * D_MODEL])

    def norm_rows(lo):
        x = x_ref[0, row0 + lo:row0 + lo + half, :]
        ms = jnp.mean(x * x, axis=-1, keepdims=True)
        h_write[lo:lo + half, :] = ((x * lax.rsqrt(ms + EPS)) * gain + shift).astype(_BF16)

    def vn_head(hd):
        lo = hd * A_HEAD_DIM
        v = _gelu_tanh(z_read[:, OFF_V + lo:OFF_V + lo + A_HEAD_DIM])
        return _rmsnorm(v, gv_ref[:, lo:lo + A_HEAD_DIM]).astype(_BF16)

    def mix_head(hd, vn):
        bias = bs_ref[:, hd:hd + 1]
        parts = [jnp.dot(ws_ref[hd], vn[c * CHUNK:(c + 1) * CHUNK, :],
                         preferred_element_type=_F32) + bias for c in range(SUB_TILE // CHUNK)]
        return jnp.concatenate(parts, axis=0)

    def gate_a(hd, mixed):
        lo = hd * A_HEAD_DIM
        u = _gelu_tanh(z_read[:, OFF_U + lo:OFF_U + lo + A_HEAD_DIM])
        g = _silu(z_read[:, OFF_GA + lo:OFF_GA + lo + A_HEAD_DIM])
        pk_ref[0, rows, PK_A + lo:PK_A + lo + A_HEAD_DIM] = (u * mixed * g).astype(_BF16)

    def gate_b(lo, width):
        gb_ref[0, rows, lo:lo + width] = _silu(z_read[:, OFF_GB + lo:OFF_GB + lo + width])

    def rope_q(qa, hd):
        base = hd * QK_WIDTH
        pair = qa[:, base + KV_RANK:base + QK_WIDTH]
        rope = pair * t1 + pltpu.roll(pair, QK_ROPE, 1) * t2q
        pk_ref[0, rows, PK_Q + base:PK_Q + base + QK_WIDTH] = jnp.concatenate(
            [qa[:, base:base + KV_RANK], rope], axis=1).astype(_BF16)

    project(0)
    cq = _rmsnorm(z_read[:, OFF_CQ:OFF_CQ + Q_RANK],
                  qg_ref[...] * (ATTN_SCALE * LOG2_E)).astype(_BF16)
    vn0 = vn_head(0)
    project(1)
    vn1 = vn_head(1)
    gate_b(0, B_WIDTH // 2)
    project(2)
    vn2 = vn_head(2)
    gate_b(B_WIDTH // 2, B_WIDTH // 2)
    project(3)
    vn3 = vn_head(3)
    ckv = _rmsnorm(z_read[:, OFF_CKV:OFF_CKV + KV_RANK], kvg_ref[...])
    kpair = z_read[:, OFF_KR:OFF_KR + 2 * QK_ROPE]
    krot = kpair * t1 + pltpu.roll(kpair, QK_ROPE, 1) * t2k
    pk_ref[0, rows, PK_K:PK_K + QK_WIDTH] = jnp.concatenate([ckv, krot], axis=1).astype(_BF16)
    if emit_cache:
        ckv_ref[0, 0, rows, :] = ckv
        kr_ref[row0 // SUB_TILE, 0] = kpair.T[0:QK_ROPE, :]
    qa = jnp.dot(cq, wq_ref[...], preferred_element_type=_F32)
    norm_rows(0)
    project(4)
    rope_q(qa, 0)
    rope_q(qa, 1)
    mixed0 = mix_head(0, vn0)
    mixed1 = mix_head(1, vn1)
    rope_q(qa, 2)
    rope_q(qa, 3)
    project(5)
    gate_a(0, mixed0)
    mixed2 = mix_head(2, vn2)
    mixed3 = mix_head(3, vn3)
    project(6)
    gate_a(1, mixed1)
    norm_rows(half)
    project(7)
    gate_a(2, mixed2)
    project(8)
    gate_a(3, mixed3)
    project(9)


def _front_body(x_ref, mod_ref, ng_ref, win_ref, ws_ref, bs_ref, gv_ref, qg_ref, wq_ref,
                kvg_ref, tab_ref, *rest, emit_cache, tiles_per_row):
    out_refs, (h_a, h_b, z_a, z_b) = rest[:-4], rest[-4:]
    t = pl.program_id(0)
    tile = x_ref.shape[1]
    assert tile == 2 * SUB_TILE
    n_pos = tab_ref.shape[0]

    @pl.when(t == 0)
    def _():
        h_b[...] = jnp.zeros(h_b.shape, _BF16)
        z_b[...] = jnp.zeros(z_b.shape, _F32)

    pos_base = (jnp.maximum(t - 1, 0) % tiles_per_row) * tile
    for sub, (h_write, h_read, z_write, z_read) in enumerate(
            ((h_a, h_b, z_a, z_b), (h_b, h_a, z_b, z_a))):
        row0 = sub * SUB_TILE
        pos = pl.multiple_of((pos_base + row0) % n_pos, SUB_TILE)
        tables = tab_ref[pl.ds(pos, SUB_TILE), :]
        _front_substep(x_ref, mod_ref, ng_ref, win_ref, ws_ref, bs_ref, gv_ref, qg_ref, wq_ref,
                       kvg_ref, tables, out_refs, row0, h_write, h_read, z_write, z_read,
                       emit_cache)


def _front_call(x, mod, mod_index, norm_g, win2, ws_bf, bs_t, g_v, q_norm_g, wq2, kv_norm_g,
                tables, emit_cache):
    batch, seq, _ = x.shape
    tile = 2 * SUB_TILE
    tiles_per_row = seq // tile
    n_tiles = batch * tiles_per_row
    assert seq % tile == 0 and tables.shape[0] % SUB_TILE == 0

    def in_tile(t):
        tt = jnp.minimum(t, n_tiles - 1)
        return tt // tiles_per_row, tt % tiles_per_row

    def out_tile(t):
        tt = jnp.maximum(t - 1, 0)
        return tt // tiles_per_row, tt % tiles_per_row

    full = lambda shape: pl.BlockSpec(shape, lambda t: (0,) * len(shape))
    in_specs = [
        pl.BlockSpec((1, tile, D_MODEL), lambda t: (*in_tile(t), 0)),
        pl.BlockSpec((SUBLANES, 3 * D_MODEL), lambda t: (mod_index(in_tile(t)[0]), 0)),
        full((1, D_MODEL)),
        full((D_MODEL, IN_WIDTH2)),
        full((A_HEADS, CHUNK, CHUNK)),
        full((CHUNK, A_HEADS)),
        full((1, A_WIDTH)),
        full((1, Q_RANK)),
        full((Q_RANK, MLA_HEADS * QK_WIDTH)),
        full((1, KV_RANK)),
        full(tables.shape),
    ]
    out_shape = [
        jax.ShapeDtypeStruct((batch, seq, PK_WIDTH), _BF16),
        jax.ShapeDtypeStruct((batch, seq, B_WIDTH), _F32),
    ]
    out_specs = [
        pl.BlockSpec((1, tile, PK_WIDTH), lambda t: (*out_tile(t), 0)),
        pl.BlockSpec((1, tile, B_WIDTH), lambda t: (*out_tile(t), 0)),
    ]
    if emit_cache:
        out_shape += [
            jax.ShapeDtypeStruct((batch, 1, seq, KV_RANK), _F32),
            jax.ShapeDtypeStruct((n_tiles * 2, 1, QK_ROPE, SUB_TILE), _F32),
        ]
        out_specs += [
            pl.BlockSpec((1, 1, tile, KV_RANK), lambda t: (out_tile(t)[0], 0, out_tile(t)[1], 0)),
            pl.BlockSpec((2, 1, QK_ROPE, SUB_TILE), lambda t: (jnp.maximum(t - 1, 0), 0, 0, 0)),
        ]
    return pl.pallas_call(
        functools.partial(_front_body, emit_cache=emit_cache, tiles_per_row=tiles_per_row),
        grid=(n_tiles + 1,),
        in_specs=in_specs,
        out_specs=out_specs,
        out_shape=out_shape,
        scratch_shapes=[pltpu.VMEM((SUB_TILE, D_MODEL), _BF16), pltpu.VMEM((SUB_TILE, D_MODEL), _BF16),
                        pltpu.VMEM((SUB_TILE, IN_WIDTH2), _F32),
                        pltpu.VMEM((SUB_TILE, IN_WIDTH2), _F32)],
        compiler_params=pltpu.CompilerParams(
            dimension_semantics=("arbitrary",),
            vmem_limit_bytes=VMEM_LIMIT_BYTES),
        name="front_ctx" if emit_cache else "front_lat",
    )(x, mod, norm_g, win2, ws_bf, bs_t, g_v, q_norm_g, wq2, kv_norm_g, tables)


def _key_chunks(k_refs):
    chunks = []
    for k_ref in k_refs:
        size = min(KEY_CHUNK, k_ref.shape[1])
        assert k_ref.shape[1] % size == 0
        chunks += [(k_ref, lo, size) for lo in range(0, k_ref.shape[1], size)]
    return chunks


def _values_t(k_ref, row):
    ckv_t = k_ref[row, :, 0:KV_RANK].T
    return jnp.concatenate([ckv_t, jnp.ones((ONES_ROWS, ckv_t.shape[1]), _BF16)], axis=0)


def _attention(q_ref, k_refs, qt_ref, fillers):
    n_rows = q_ref.shape[0]
    chunks = _key_chunks(k_refs)
    steps = [(row, hd, c) for row in range(n_rows) for hd in range(MLA_HEADS)
             for c in range(len(chunks))]

    for row in range(n_rows):
        for hd in range(MLA_HEADS):
            qt_ref[row * MLA_HEADS + hd] = q_ref[row, :, hd * QK_WIDTH:(hd + 1) * QK_WIDTH].T

    def logits(row, hd, c):
        k_ref, lo, size = chunks[c]
        return jnp.dot(k_ref[row, lo:lo + size, :], qt_ref[row * MLA_HEADS + hd],
                       preferred_element_type=_F32)

    pending = [logits(*step) for step in steps[:SCORE_LOOKAHEAD]]
    values_t = [{id(k_ref): _values_t(k_ref, row) for k_ref in k_refs} for row in range(n_rows)]
    outs = {}
    m = acc = None
    for j, (row, hd, c) in enumerate(steps):
        if j + SCORE_LOOKAHEAD < len(steps):
            pending.append(logits(*steps[j + SCORE_LOOKAHEAD]))
        s, pending[j] = pending[j], None
        k_ref, lo, size = chunks[c]
        chunk_max = jnp.max(s, axis=0, keepdims=True)
        if c > 0:
            m_new = jnp.maximum(m, chunk_max)
            acc = acc * jnp.exp2(m - m_new)
            m = m_new
        else:
            m = chunk_max
        p = jnp.exp2(s - m).astype(_BF16)
        part = jnp.dot(values_t[row][id(k_ref)][:, lo:lo + size], p, preferred_element_type=_F32)
        acc = acc + part if c > 0 else part
        if j >= len(steps) - len(fillers):
            fillers[j - len(steps) + len(fillers)]()
        if c == len(chunks) - 1:
            outs[row, hd] = (acc[0:KV_RANK, :] / acc[KV_RANK:KV_RANK + 1, :]).T
    return outs


def _attention_single(q_ref, k_ref, fillers):
    n_rows, tq = q_ref.shape[0], q_ref.shape[1]
    groups = [(row, pair) for row in range(n_rows) for pair in range(MLA_HEADS // 2)]
    scores = {}
    for row, pair in groups:
        q = jnp.concatenate([q_ref[row, :, hd * QK_WIDTH:(hd + 1) * QK_WIDTH]
                             for hd in (2 * pair, 2 * pair + 1)], axis=0)
        scores[row, pair] = lax.dot_general(q, k_ref[row], _NT_DIMS, preferred_element_type=_F32)
    for fill in fillers:
        fill()
    outs = {}
    for row, pair in groups:
        s = scores[row, pair]
        ckv = k_ref[row, :, 0:KV_RANK]
        values = jnp.concatenate([ckv, jnp.ones_like(ckv)], axis=-1)
        p = jnp.exp2(s - jnp.max(s, axis=1, keepdims=True)).astype(_BF16)
        acc = jnp.dot(p, values, preferred_element_type=_F32)
        o = acc[:, 0:KV_RANK] / acc[:, KV_RANK:2 * KV_RANK]
        outs[row, 2 * pair], outs[row, 2 * pair + 1] = o[0:tq, :], o[tq:2 * tq, :]
    return outs


def _back_body(*refs, has_ctx):
    if has_ctx:
        (q_ref, klat_ref, kctx_ref, a_ref, gb_ref, x_ref, mod_ref, wuv_ref, wo_ref, fg_ref,
         y_ref, qt_ref, ya_ref) = refs
        k_refs = (kctx_ref, klat_ref)
    else:
        (q_ref, klat_ref, a_ref, gb_ref, x_ref, mod_ref, wuv_ref, wo_ref, fg_ref, y_ref,
         qt_ref, ya_ref) = refs
        k_refs = (klat_ref,)
    n_rows, tq = q_ref.shape[0], q_ref.shape[1]

    block = min(OUT_ROWS, tq)
    blocks = [(row, lo) for row in range(n_rows) for lo in range(0, tq, block)]

    def a_part(row, lo):
        def run():
            ya_ref[row, lo:lo + block, :] = jnp.dot(
                a_ref[row, lo:lo + block, :], wo_ref[0:A_WIDTH, :], preferred_element_type=_F32)
        return run

    fillers = [a_part(*b) for b in blocks]
    if len(_key_chunks(k_refs)) == 1:
        outs = _attention_single(q_ref, k_refs[0], fillers)
    else:
        outs = _attention(q_ref, k_refs, qt_ref, fillers)

    gate = mod_ref[0:1, 2 * D_MODEL:3 * D_MODEL]
    for row, lo in blocks:
        rows = slice(lo, lo + block)
        attn_cols = []
        for pair in range(MLA_HEADS // 2):
            o2 = jnp.concatenate([outs[row, 2 * pair][rows, :],
                                  outs[row, 2 * pair + 1][rows, :]], axis=1)
            attn_cols.append(jnp.dot(o2.astype(_BF16), wuv_ref[pair],
                                     preferred_element_type=_F32))
        attn = jnp.concatenate(attn_cols, axis=1) * gb_ref[row, rows, :]
        y = ya_ref[row, rows, :] + jnp.dot(
            attn.astype(_BF16), wo_ref[A_WIDTH:A_WIDTH + B_WIDTH, :], preferred_element_type=_F32)
        out = x_ref[row, rows, :] + gate * y
        y_ref[row, rows, :] = _rmsnorm(out, fg_ref[...])


def _back_call(packed, kctx, gb, x, mod, mod_index, wuv2, wo_bf, final_g, rows_per_step):
    batch, seq, _ = x.shape
    tq = min(TOKEN_TILE, seq)
    has_ctx = kctx is not None
    rows = rows_per_step
    assert batch % rows == 0 and (rows == 1 or seq == tq)
    full = lambda shape: pl.BlockSpec(shape, lambda b, i: (0,) * len(shape))
    keys = lambda arr: pl.BlockSpec((rows, arr.shape[1], arr.shape[2]), lambda b, i: (b, 0, 0))
    in_specs = [
        pl.BlockSpec((rows, tq, MLA_HEADS * QK_WIDTH),
                     lambda b, i: (b, i, PK_Q // (MLA_HEADS * QK_WIDTH))),
        pl.BlockSpec((rows, seq, QK_WIDTH), lambda b, i: (b, 0, PK_K // QK_WIDTH)),
    ]
    args = [packed, packed]
    if has_ctx:
        in_specs += [keys(kctx)]
        args += [kctx]
    in_specs += [
        pl.BlockSpec((rows, tq, A_WIDTH), lambda b, i: (b, i, PK_A // A_WIDTH)),
        pl.BlockSpec((rows, tq, B_WIDTH), lambda b, i: (b, i, 0)),
        pl.BlockSpec((rows, tq, D_MODEL), lambda b, i: (b, i, 0)),
        pl.BlockSpec((SUBLANES, 3 * D_MODEL), lambda b, i: (mod_index(b * rows), 0)),
        full((MLA_HEADS // 2, 2 * KV_RANK, 2 * V_HEAD)),
        full((D_MODEL, D_MODEL)),
        full((1, D_MODEL)),
    ]
    args += [packed, gb, x, mod, wuv2, wo_bf, final_g]
    return pl.pallas_call(
        functools.partial(_back_body, has_ctx=has_ctx),
        grid=(batch // rows, seq // tq),
        in_specs=in_specs,
        out_specs=pl.BlockSpec((rows, tq, D_MODEL), lambda b, i: (b, i, 0)),
        out_shape=jax.ShapeDtypeStruct((batch, seq, D_MODEL), _F32),
        scratch_shapes=[
            pltpu.VMEM((rows * MLA_HEADS, QK_WIDTH, tq), _BF16),
            pltpu.VMEM((rows, tq, D_MODEL), _F32),
        ],
        compiler_params=pltpu.CompilerParams(
            dimension_semantics=("arbitrary", "arbitrary"),
            vmem_limit_bytes=VMEM_LIMIT_BYTES),
        name="back_lat" if has_ctx else "back_ctx",
    )(*args)


def _rope_tables(n_tokens, rotate):
    if rotate:
        rows = n_tokens // GRID_W
        inv = ROPE_THETA ** (-np.arange(AXIS_PAIRS, dtype=np.float64) / AXIS_PAIRS)
        ang_r = np.arange(rows, dtype=np.float64)[:, None] * inv
        ang_c = np.arange(GRID_W, dtype=np.float64)[:, None] * inv

        def grid_table(fn):
            tr = np.broadcast_to(fn(ang_r)[:, None, :], (rows, GRID_W, AXIS_PAIRS))
            tc = np.broadcast_to(fn(ang_c)[None, :, :], (rows, GRID_W, AXIS_PAIRS))
            half = np.concatenate([tr, tc], axis=-1).reshape(n_tokens, 2 * AXIS_PAIRS)
            return np.concatenate([half, half], axis=-1)

        cos, sin = grid_table(np.cos), grid_table(np.sin)
    else:
        cos, sin = np.ones((n_tokens, QK_ROPE)), np.zeros((n_tokens, QK_ROPE))
    return jnp.asarray(np.concatenate([cos, sin], axis=-1), dtype=_F32)


def kernel(x_prompt, x_sample, cache_ckv, cache_krope, c, c_ctx, norm_g, w_ada, b_ada, w_in, w_s,
           b_s, g_v, q_norm_g, w_uq, kv_norm_g, w_ukv, w_o, final_g):
    depth = norm_g.shape[0]
    assert depth == 1 and w_in.shape[2] == IN_WIDTH
    dec_batch = x_sample.shape[0]
    xp, xs = x_prompt, x_sample
    new_ckv, new_kr = [], []
    for l in range(depth):
        cond = jnp.concatenate(
            [c, c_ctx[None, :], jnp.zeros((16 - dec_batch - 1, D_MODEL), _F32)], axis=0)
        mod, win2, wq2, wuv2 = _weights_call(cond, w_ada[l], b_ada[l], jnp.transpose(w_in[l]),
                                             w_uq[l], w_ukv[l])
        wo_bf = w_o[l].astype(_BF16)
        ws_bf = w_s[l].astype(_BF16)
        bs_t = b_s[l].T
        gv_row = g_v[l].reshape(1, A_WIDTH)
        shared = (norm_g[l].reshape(1, D_MODEL), win2, ws_bf, bs_t, gv_row,
                  q_norm_g[l].reshape(1, Q_RANK), wq2, kv_norm_g[l].reshape(1, KV_RANK))
        fg = final_g.reshape(1, D_MODEL)

        ctx_index = lambda b: dec_batch
        ctx_batch, ctx_seq, _ = xp.shape
        pack = 2 * SUB_TILE // ctx_seq
        pk_c, gb_c, ckv_c, kr_c = _front_call(
            xp.reshape(ctx_batch // pack, pack * ctx_seq, D_MODEL), mod, ctx_index, *shared,
            _rope_tables(ctx_seq, False), True)
        unpack = lambda arr: arr.reshape(ctx_batch, ctx_seq, arr.shape[-1])
        xp = _back_call(unpack(pk_c), None, unpack(gb_c), xp, mod, ctx_index, wuv2, wo_bf, fg,
                        rows_per_step=pack)
        new_ckv.append(ckv_c.reshape(ctx_batch, 1, ctx_seq, KV_RANK))
        assert ctx_seq == SUB_TILE
        new_kr.append(jnp.swapaxes(kr_c, 2, 3))

        lat_index = lambda b: b
        pk_s, gb_s = _front_call(
            xs, mod, lat_index, *shared, _rope_tables(xs.shape[1], True), False)
        cache_k = jnp.concatenate(
            [cache_ckv[:, l], jnp.zeros(cache_krope[:, l].shape, _F32), cache_krope[:, l]],
            axis=-1).astype(_BF16)
        xs = _back_call(pk_s, cache_k, gb_s, xs, mod, lat_index, wuv2, wo_bf, fg, rows_per_step=1)
    return (xp, xs, jnp.concatenate(new_ckv, axis=1), jnp.concatenate(new_kr, axis=1))
```

```python
import functools
import math

import jax
import jax.numpy as jnp
from jax import lax
from jax.experimental import pallas as pl
from jax.experimental.pallas import tpu as pltpu

D_MODEL = 1024
GRID_W = 64
EPS = 1e-6
A_HEADS = 4
A_HEAD_DIM = 128
A_WIDTH = A_HEADS * A_HEAD_DIM
CHUNK = 128
MLA_HEADS = 4
QK_NOPE = 128
QK_ROPE = 64
V_HEAD = 128
B_WIDTH = MLA_HEADS * V_HEAD
Q_RANK = 256
KV_RANK = 128
AXIS_PAIRS = QK_ROPE // 4
ROPE_THETA = 10000.0
ATTN_SCALE = 1.0 / math.sqrt(QK_NOPE + QK_ROPE)
LOG2_E = 1.4426950408889634

QK_WIDTH = KV_RANK + 2 * QK_ROPE
OFF_U = 0
OFF_V = OFF_U + A_WIDTH
OFF_GA = OFF_V + A_WIDTH
OFF_CQ = OFF_GA + A_WIDTH
OFF_CKV = OFF_CQ + Q_RANK
OFF_KR = OFF_CKV + KV_RANK
OFF_GB = OFF_KR + 2 * QK_ROPE
IN_WIDTH = OFF_GB - QK_ROPE + B_WIDTH
IN_WIDTH2 = OFF_GB + B_WIDTH

PK_Q = 0
PK_A = PK_Q + MLA_HEADS * QK_WIDTH
PK_K = PK_A + A_WIDTH
PK_WIDTH = PK_K + QK_WIDTH

LANES = 128
SUBLANES = 8
ONES_ROWS = 16
SUB_TILE = 256
TOKEN_TILE = 512
PROJ_BLOCK = 256
OUT_ROWS = 256
KEY_CHUNK = 256
SCORE_LOOKAHEAD = 3
VMEM_LIMIT_BYTES = 56 * 1024 * 1024

_F32 = jnp.float32
_BF16 = jnp.bfloat16
_NT_DIMS = (((1,), (1,)), ((), ()))


def _silu(x):
    hx = 0.5 * x
    return hx + hx * jnp.tanh(hx)


def _gelu_tanh(x):
    return x * (0.5 * (1.0 + jnp.tanh(0.7978845608028654 * (x + 0.044715 * (x * x * x)))))


def _rmsnorm(x, g):
    ms = jnp.mean(x * x, axis=-1, keepdims=True)
    return (x * lax.rsqrt(ms + EPS)) * g


def _mod_body(cond_ref, wa_ref, wb_ref, b_ref, o_ref):
    s = _silu(cond_ref[...]).astype(_BF16)
    halves = [jnp.dot(s, w_ref[...].astype(_BF16), preferred_element_type=_F32)
              for w_ref in (wa_ref, wb_ref)]
    mod = jnp.concatenate(halves, axis=1) + b_ref[...]
    rows, cols = mod.shape
    o_ref[...] = jnp.broadcast_to(mod[:, None, :], (rows, SUBLANES, cols)).reshape(
        rows * SUBLANES, cols)


def _fold_body(wq_ref, wkv_ref, q_ref, uv_ref):
    half, quarter = LANES // 2, LANES // 4
    assert QK_NOPE == LANES and QK_ROPE == half and V_HEAD == LANES
    lane = lax.broadcasted_iota(jnp.int32, (Q_RANK, LANES), 1)
    low = lane < half

    def cols(block):
        return wq_ref[:, block * LANES:(block + 1) * LANES]

    for hd in range(MLA_HEADS):
        start = hd * (QK_NOPE + QK_ROPE)
        b0, odd = divmod(start, LANES)
        if odd == 0:
            nope = cols(b0)
            rope_blk = cols(b0 + 1)
            rope_lo = rope_blk
            rope_hi = pltpu.roll(rope_blk, half, 1)
        else:
            assert odd == half
            nope = jnp.where(low, pltpu.roll(cols(b0), half, 1), pltpu.roll(cols(b0 + 1), half, 1))
            rope_blk = cols(b0 + 1)
            rope_lo = pltpu.roll(rope_blk, half, 1)
            rope_hi = rope_blk
        partner = jnp.where(lane < half + quarter,
                            -pltpu.roll(rope_hi, LANES - quarter, 1),
                            pltpu.roll(rope_hi, quarter, 1))
        w_uk = wkv_ref[:, hd * (QK_NOPE + V_HEAD):hd * (QK_NOPE + V_HEAD) + QK_NOPE]
        absorbed = lax.dot_general(nope, w_uk, _NT_DIMS, precision=lax.Precision.HIGHEST,
                                   preferred_element_type=_F32)
        base = hd * QK_WIDTH
        q_ref[:, base:base + KV_RANK] = absorbed.astype(_BF16)
        q_ref[:, base + KV_RANK:base + QK_WIDTH] = jnp.where(low, rope_lo, partner).astype(_BF16)

    uv_ref[...] = jnp.zeros(uv_ref.shape, _BF16)
    for hd in range(MLA_HEADS):
        w_uv = wkv_ref[:, hd * (QK_NOPE + V_HEAD) + QK_NOPE:(hd + 1) * (QK_NOPE + V_HEAD)]
        pair, pos = divmod(hd, 2)
        uv_ref[pair, pos * KV_RANK:(pos + 1) * KV_RANK, pos * V_HEAD:(pos + 1) * V_HEAD] = (
            w_uv.astype(_BF16))


_KR_BLOCK = OFF_KR // LANES


_PREP_SPLIT = IN_WIDTH2 // 2


def _prep_body(step, w_ref, o_ref):
    half = QK_ROPE // 2
    blocks_per_step = _PREP_SPLIT // LANES
    base = step * _PREP_SPLIT
    for j in range(step * blocks_per_step, (step + 1) * blocks_per_step):
        if j < _KR_BLOCK:
            blk = w_ref[j * LANES - base:(j + 1) * LANES - base, :]
        elif j == _KR_BLOCK:
            kr = w_ref[OFF_KR - base:OFF_KR - base + QK_ROPE, :]
            partner = jnp.concatenate([-kr[half:, :], kr[:half, :]], axis=0)
            blk = jnp.concatenate([kr, partner], axis=0)
        else:
            lo = j * LANES - QK_ROPE - base
            blk = w_ref[lo:lo + LANES, :]
        o_ref[:, j * LANES - base:(j + 1) * LANES - base] = blk.T.astype(_BF16)


def _weights_body(cond_ref, wa_ref, wb_ref, b_ref, win_ref, wq_ref, wkv_ref,
                  mod_ref, win2_ref, q_ref, uv_ref):
    j = pl.program_id(0)
    _mod_body(cond_ref, wa_ref, wb_ref, b_ref, mod_ref)
    n_prep = IN_WIDTH2 // _PREP_SPLIT
    for step in range(n_prep):
        pl.when(j == step)(functools.partial(_prep_body, step, win_ref, win2_ref))
    pl.when(j == n_prep)(functools.partial(_fold_body, wq_ref, wkv_ref, q_ref, uv_ref))


def _weights_call(cond, w_ada, b_ada, w_in_t, w_uq, w_ukv):
    rows = cond.shape[0]
    n_out = w_ada.shape[1]
    col_block = D_MODEL
    n_steps = n_out // col_block
    n_prep = IN_WIDTH2 // _PREP_SPLIT
    assert _KR_BLOCK * LANES == OFF_KR and OFF_KR >= _PREP_SPLIT and n_steps == n_prep + 1
    prep_step = lambda j: jnp.minimum(j, n_prep - 1)
    full = lambda shape: pl.BlockSpec(shape, lambda j: (0,) * len(shape))
    uv_shape = (MLA_HEADS // 2, 2 * KV_RANK, 2 * V_HEAD)
    return pl.pallas_call(
        _weights_body,
        grid=(n_steps,),
        in_specs=[
            full((rows, D_MODEL)),
            pl.BlockSpec((D_MODEL, col_block // 2), lambda j: (0, 2 * j)),
            pl.BlockSpec((D_MODEL, col_block // 2), lambda j: (0, 2 * j + 1)),
            pl.BlockSpec((1, col_block), lambda j: (0, j)),
            pl.BlockSpec((_PREP_SPLIT, D_MODEL), lambda j: (prep_step(j), 0)),
            full(w_uq.shape),
            full(w_ukv.shape),
        ],
        out_specs=[
            pl.BlockSpec((rows * SUBLANES, col_block), lambda j: (0, j)),
            pl.BlockSpec((D_MODEL, _PREP_SPLIT), lambda j: (0, prep_step(j))),
            full((Q_RANK, MLA_HEADS * QK_WIDTH)),
            full(uv_shape),
        ],
        out_shape=[
            jax.ShapeDtypeStruct((rows * SUBLANES, n_out), _F32),
            jax.ShapeDtypeStruct((D_MODEL, IN_WIDTH2), _BF16),
            jax.ShapeDtypeStruct((Q_RANK, MLA_HEADS * QK_WIDTH), _BF16),
            jax.ShapeDtypeStruct(uv_shape, _BF16),
        ],
        compiler_params=pltpu.CompilerParams(
            dimension_semantics=("arbitrary",), vmem_limit_bytes=VMEM_LIMIT_BYTES),
        name="weights",
    )(cond, w_ada, w_ada, b_ada.reshape(1, n_out), w_in_t, w_uq, w_ukv)


def _front_substep(x_ref, mod_ref, ng_ref, win_ref, ws_ref, bs_ref, gv_ref, qg_ref, wq_ref,
                   kvg_ref, tables, out_refs, row0, h_write, h_read, z_write, z_read, emit_cache):
    if emit_cache:
        pk_ref, gb_ref, ckv_ref, kr_ref = out_refs
    else:
        pk_ref, gb_ref = out_refs
    first_half = lax.broadcasted_iota(jnp.int32, tables.shape, 1) < QK_ROPE
    t1 = jnp.where(first_half, tables, 0.0)
    t2k = jnp.where(first_half, pltpu.roll(tables, QK_ROPE, 1), 0.0)
    t2q = jnp.where(first_half, t2k, 1.0)
    rows = slice(row0, row0 + SUB_TILE)
    half = SUB_TILE // 2
    assert IN_WIDTH2 == 10 * PROJ_BLOCK

    def project(j):
        lo = j * PROJ_BLOCK
        z_write[:, lo:lo + PROJ_BLOCK] = jnp.dot(h_read[...], win_ref[:, lo:lo + PROJ_BLOCK],
                                                 preferred_element_type=_F32)

    shift = mod_ref[0:1, 0:D_MODEL]
    gain = ng_ref[...] * (1.0 + mod_ref[0:1, D_MODEL:2 * D_MODEL])

    def norm_rows(lo):
        x = x_ref[0, row0 + lo:row0 + lo + half, :]
        ms = jnp.mean(x * x, axis=-1, keepdims=True)
        h_write[lo:lo + half, :] = ((x * lax.rsqrt(ms + EPS)) * gain + shift).astype(_BF16)

    def vn_head(hd):
        lo = hd * A_HEAD_DIM
        v = _gelu_tanh(z_read[:, OFF_V + lo:OFF_V + lo + A_HEAD_DIM])
        return _rmsnorm(v, gv_ref[:, lo:lo + A_HEAD_DIM]).astype(_BF16)

    def mix_head(hd, vn):
        bias = bs_ref[:, hd:hd + 1]
        parts = [jnp.dot(ws_ref[hd], vn[c * CHUNK:(c + 1) * CHUNK, :],
                         preferred_element_type=_F32) + bias for c in range(SUB_TILE // CHUNK)]
        return jnp.concatenate(parts, axis=0)

    def gate_a(hd, mixed):
        lo = hd * A_HEAD_DIM
        u = _gelu_tanh(z_read[:, OFF_U + lo:OFF_U + lo + A_HEAD_DIM])
        g = _silu(z_read[:, OFF_GA + lo:OFF_GA + lo + A_HEAD_DIM])
        pk_ref[0, rows, PK_A + lo:PK_A + lo + A_HEAD_DIM] = (u * mixed * g).astype(_BF16)

    def gate_b(lo, width):
        gb_ref[0, rows, lo:lo + width] = _silu(z_read[:, OFF_GB + lo:OFF_GB + lo + width])

    def rope_q(qa, hd):
        base = hd * QK_WIDTH
        pair = qa[:, base + KV_RANK:base + QK_WIDTH]
        rope = pair * t1 + pltpu.roll(pair, QK_ROPE, 1) * t2q
        pk_ref[0, rows, PK_Q + base:PK_Q + base + QK_WIDTH] = jnp.concatenate(
            [qa[:, base:base + KV_RANK], rope], axis=1).astype(_BF16)

    project(0)
    cq = _rmsnorm(z_read[:, OFF_CQ:OFF_CQ + Q_RANK],
                  qg_ref[...] * (ATTN_SCALE * LOG2_E)).astype(_BF16)
    vn0 = vn_head(0)
    project(1)
    vn1 = vn_head(1)
    gate_b(0, B_WIDTH // 2)
    project(2)
    vn2 = vn_head(2)
    gate_b(B_WIDTH // 2, B_WIDTH // 2)
    project(3)
    vn3 = vn_head(3)
    ckv = _rmsnorm(z_read[:, OFF_CKV:OFF_CKV + KV_RANK], kvg_ref[...])
    kpair = z_read[:, OFF_KR:OFF_KR + 2 * QK_ROPE]
    krot = kpair * t1 + pltpu.roll(kpair, QK_ROPE, 1) * t2k
    pk_ref[0, rows, PK_K:PK_K + QK_WIDTH] = jnp.concatenate([ckv, krot], axis=1).astype(_BF16)
    if emit_cache:
        ckv_ref[0, 0, rows, :] = ckv
        kr_ref[row0 // SUB_TILE, 0] = kpair.T[0:QK_ROPE, :]
    qa = jnp.dot(cq, wq_ref[...], preferred_element_type=_F32)
    norm_rows(0)
    project(4)
    rope_q(qa, 0)
    rope_q(qa, 1)
    mixed0 = mix_head(0, vn0)
    mixed1 = mix_head(1, vn1)
    rope_q(qa, 2)
    rope_q(qa, 3)
    project(5)
    gate_a(0, mixed0)
    mixed2 = mix_head(2, vn2)
    mixed3 = mix_head(3, vn3)
    project(6)
    gate_a(1, mixed1)
    norm_rows(half)
    project(7)
    gate_a(2, mixed2)
    project(8)
    gate_a(3, mixed3)
    project(9)


def _front_body(x_ref, mod_ref, ng_ref, win_ref, ws_ref, bs_ref, gv_ref, qg_ref, wq_ref,
                kvg_ref, tab_ref, *rest, emit_cache, tiles_per_row):
    out_refs, (h_a, h_b, z_a, z_b) = rest[:-4], rest[-4:]
    t = pl.program_id(0)
    tile = x_ref.shape[1]
    assert tile == 2 * SUB_TILE
    n_pos = tab_ref.shape[0]

    @pl.when(t == 0)
    def _():
        h_b[...] = jnp.zeros(h_b.shape, _BF16)
        z_b[...] = jnp.zeros(z_b.shape, _F32)

    pos_base = (jnp.maximum(t - 1, 0) % tiles_per_row) * tile
    for sub, (h_write, h_read, z_write, z_read) in enumerate(
            ((h_a, h_b, z_a, z_b), (h_b, h_a, z_b, z_a))):
        row0 = sub * SUB_TILE
        pos = pl.multiple_of((pos_base + row0) % n_pos, SUB_TILE)
        tables = tab_ref[pl.ds(pos, SUB_TILE), :]
        _front_substep(x_ref, mod_ref, ng_ref, win_ref, ws_ref, bs_ref, gv_ref, qg_ref, wq_ref,
                       kvg_ref, tables, out_refs, row0, h_write, h_read, z_write, z_read,
                       emit_cache)


def _front_call(x, mod, mod_index, norm_g, win2, ws_bf, bs_t, g_v, q_norm_g, wq2, kv_norm_g,
                tables, emit_cache):
    batch, seq, _ = x.shape
    tile = 2 * SUB_TILE
    tiles_per_row = seq // tile
    n_tiles = batch * tiles_per_row
    assert seq % tile == 0 and tables.shape[0] % SUB_TILE == 0

    def in_tile(t):
        tt = jnp.minimum(t, n_tiles - 1)
        return tt // tiles_per_row, tt % tiles_per_row

    def out_tile(t):
        tt = jnp.maximum(t - 1, 0)
        return tt // tiles_per_row, tt % tiles_per_row

    full = lambda shape: pl.BlockSpec(shape, lambda t: (0,) * len(shape))
    in_specs = [
        pl.BlockSpec((1, tile, D_MODEL), lambda t: (*in_tile(t), 0)),
        pl.BlockSpec((SUBLANES, 3 * D_MODEL), lambda t: (mod_index(in_tile(t)[0]), 0)),
        full((1, D_MODEL)),
        full((D_MODEL, IN_WIDTH2)),
        full((A_HEADS, CHUNK, CHUNK)),
        full((CHUNK, A_HEADS)),
        full((1, A_WIDTH)),
        full((1, Q_RANK)),
        full((Q_RANK, MLA_HEADS * QK_WIDTH)),
        full((1, KV_RANK)),
        full(tables.shape),
    ]
    out_shape = [
        jax.ShapeDtypeStruct((batch, seq, PK_WIDTH), _BF16),
        jax.ShapeDtypeStruct((batch, seq, B_WIDTH), _F32),
    ]
    out_specs = [
        pl.BlockSpec((1, tile, PK_WIDTH), lambda t: (*out_tile(t), 0)),
        pl.BlockSpec((1, tile, B_WIDTH), lambda t: (*out_tile(t), 0)),
    ]
    if emit_cache:
        out_shape += [
            jax.ShapeDtypeStruct((batch, 1, seq, KV_RANK), _F32),
            jax.ShapeDtypeStruct((n_tiles * 2, 1, QK_ROPE, SUB_TILE), _F32),
        ]
        out_specs += [
            pl.BlockSpec((1, 1, tile, KV_RANK), lambda t: (out_tile(t)[0], 0, out_tile(t)[1], 0)),
            pl.BlockSpec((2, 1, QK_ROPE, SUB_TILE), lambda t: (jnp.maximum(t - 1, 0), 0, 0, 0)),
        ]
    return pl.pallas_call(
        functools.partial(_front_body, emit_cache=emit_cache, tiles_per_row=tiles_per_row),
        grid=(n_tiles + 1,),
        in_specs=in_specs,
        out_specs=out_specs,
        out_shape=out_shape,
        scratch_shapes=[pltpu.VMEM((SUB_TILE, D_MODEL), _BF16), pltpu.VMEM((SUB_TILE, D_MODEL), _BF16),
                        pltpu.VMEM((SUB_TILE, IN_WIDTH2), _F32),
                        pltpu.VMEM((SUB_TILE, IN_WIDTH2), _F32)],
        compiler_params=pltpu.CompilerParams(
            dimension_semantics=("arbitrary",),
            vmem_limit_bytes=VMEM_LIMIT_BYTES),
        name="front_ctx" if emit_cache else "front_lat",
    )(x, mod, norm_g, win2, ws_bf, bs_t, g_v, q_norm_g, wq2, kv_norm_g, tables)


def _key_chunks(k_refs):
    chunks = []
    for k_ref in k_refs:
        size = min(KEY_CHUNK, k_ref.shape[1])
        assert k_ref.shape[1] % size == 0
        chunks += [(k_ref, lo, size) for lo in range(0, k_ref.shape[1], size)]
    return chunks


def _values_t(k_ref, row):
    ckv_t = k_ref[row, :, 0:KV_RANK].T
    return jnp.concatenate([ckv_t, jnp.ones((ONES_ROWS, ckv_t.shape[1]), _BF16)], axis=0)


def _attention(q_ref, k_refs, qt_ref, fillers):
    n_rows = q_ref.shape[0]
    chunks = _key_chunks(k_refs)
    steps = [(row, hd, c) for row in range(n_rows) for pair in range(MLA_HEADS // 2)
             for c in range(len(chunks)) for hd in (2 * pair, 2 * pair + 1)]

    for row in range(n_rows):
        for hd in range(MLA_HEADS):
            qt_ref[row * MLA_HEADS + hd] = q_ref[row, :, hd * QK_WIDTH:(hd + 1) * QK_WIDTH].T

    def logits(row, hd, c):
        k_ref, lo, size = chunks[c]
        return jnp.dot(k_ref[row, lo:lo + size, :], qt_ref[row * MLA_HEADS + hd],
                       preferred_element_type=_F32)

    pending = [logits(*step) for step in steps[:SCORE_LOOKAHEAD]]
    values_t = [{id(k_ref): _values_t(k_ref, row) for k_ref in k_refs} for row in range(n_rows)]
    outs = {}
    state = {}
    for j, (row, hd, c) in enumerate(steps):
        m, acc = state.get((row, hd), (None, None))
        if j + SCORE_LOOKAHEAD < len(steps):
            pending.append(logits(*steps[j + SCORE_LOOKAHEAD]))
        s, pending[j] = pending[j], None
        k_ref, lo, size = chunks[c]
        chunk_max = jnp.max(s, axis=0, keepdims=True)
        if c > 0:
            m_new = jnp.maximum(m, chunk_max)
            acc = acc * jnp.exp2(m - m_new)
            m = m_new
        else:
            m = chunk_max
        p = jnp.exp2(s - m).astype(_BF16)
        part = jnp.dot(values_t[row][id(k_ref)][:, lo:lo + size], p, preferred_element_type=_F32)
        acc = acc + part if c > 0 else part
        state[row, hd] = (m, acc)
        if j >= len(steps) - len(fillers):
            fillers[j - len(steps) + len(fillers)]()
        if c == len(chunks) - 1:
            outs[row, hd] = (acc[0:KV_RANK, :] / acc[KV_RANK:KV_RANK + 1, :]).T
    return outs


def _attention_single(q_ref, k_ref, fillers):
    n_rows, tq = q_ref.shape[0], q_ref.shape[1]
    groups = [(row, pair) for row in range(n_rows) for pair in range(MLA_HEADS // 2)]
    scores = {}
    for row, pair in groups:
        q = jnp.concatenate([q_ref[row, :, hd * QK_WIDTH:(hd + 1) * QK_WIDTH]
                             for hd in (2 * pair, 2 * pair + 1)], axis=0)
        scores[row, pair] = lax.dot_general(q, k_ref[row], _NT_DIMS, preferred_element_type=_F32)
    for fill in fillers:
        fill()
    outs = {}
    for row, pair in groups:
        s = scores[row, pair]
        ckv = k_ref[row, :, 0:KV_RANK]
        values = jnp.concatenate([ckv, jnp.ones_like(ckv)], axis=-1)
        p = jnp.exp2(s - jnp.max(s, axis=1, keepdims=True)).astype(_BF16)
        acc = jnp.dot(p, values, preferred_element_type=_F32)
        o = acc[:, 0:KV_RANK] / acc[:, KV_RANK:2 * KV_RANK]
        outs[row, 2 * pair], outs[row, 2 * pair + 1] = o[0:tq, :], o[tq:2 * tq, :]
    return outs


def _back_body(*refs, has_ctx):
    if has_ctx:
        (q_ref, klat_ref, kctx_ref, a_ref, gb_ref, x_ref, mod_ref, wuv_ref, wo_ref, fg_ref,
         y_ref, qt_ref, ya_ref) = refs
        k_refs = (kctx_ref, klat_ref)
    else:
        (q_ref, klat_ref, a_ref, gb_ref, x_ref, mod_ref, wuv_ref, wo_ref, fg_ref, y_ref,
         qt_ref, ya_ref) = refs
        k_refs = (klat_ref,)
    n_rows, tq = q_ref.shape[0], q_ref.shape[1]

    block = min(OUT_ROWS, tq)
    blocks = [(row, lo) for row in range(n_rows) for lo in range(0, tq, block)]

    def a_part(row, lo):
        def run():
            ya_ref[row, lo:lo + block, :] = jnp.dot(
                a_ref[row, lo:lo + block, :], wo_ref[0:A_WIDTH, :], preferred_element_type=_F32)
        return run

    fillers = [a_part(*b) for b in blocks]
    if len(_key_chunks(k_refs)) == 1:
        outs = _attention_single(q_ref, k_refs[0], fillers)
    else:
        outs = _attention(q_ref, k_refs, qt_ref, fillers)

    gate = mod_ref[0:1, 2 * D_MODEL:3 * D_MODEL]
    for row, lo in blocks:
        rows = slice(lo, lo + block)
        attn_cols = []
        for pair in range(MLA_HEADS // 2):
            o2 = jnp.concatenate([outs[row, 2 * pair][rows, :],
                                  outs[row, 2 * pair + 1][rows, :]], axis=1)
            attn_cols.append(jnp.dot(o2.astype(_BF16), wuv_ref[pair],
                                     preferred_element_type=_F32))
        attn = jnp.concatenate(attn_cols, axis=1) * gb_ref[row, rows, :]
        y = ya_ref[row, rows, :] + jnp.dot(
            attn.astype(_BF16), wo_ref[A_WIDTH:A_WIDTH + B_WIDTH, :], preferred_element_type=_F32)
        out = x_ref[row, rows, :] + gate * y
        y_ref[row, rows, :] = _rmsnorm(out, fg_ref[...])


def _back_call(packed, kctx, gb, x, mod, mod_index, wuv2, wo_bf, final_g, rows_per_step):
    batch, seq, _ = x.shape
    tq = min(TOKEN_TILE, seq)
    has_ctx = kctx is not None
    rows = rows_per_step
    assert batch % rows == 0 and (rows == 1 or seq == tq)
    full = lambda shape: pl.BlockSpec(shape, lambda b, i: (0,) * len(shape))
    keys = lambda arr: pl.BlockSpec((rows, arr.shape[1], arr.shape[2]), lambda b, i: (b, 0, 0))
    in_specs = [
        pl.BlockSpec((rows, tq, MLA_HEADS * QK_WIDTH),
                     lambda b, i: (b, i, PK_Q // (MLA_HEADS * QK_WIDTH))),
        pl.BlockSpec((rows, seq, QK_WIDTH), lambda b, i: (b, 0, PK_K // QK_WIDTH)),
    ]
    args = [packed, packed]
    if has_ctx:
        in_specs += [keys(kctx)]
        args += [kctx]
    in_specs += [
        pl.BlockSpec((rows, tq, A_WIDTH), lambda b, i: (b, i, PK_A // A_WIDTH)),
        pl.BlockSpec((rows, tq, B_WIDTH), lambda b, i: (b, i, 0)),
        pl.BlockSpec((rows, tq, D_MODEL), lambda b, i: (b, i, 0)),
        pl.BlockSpec((SUBLANES, 3 * D_MODEL), lambda b, i: (mod_index(b * rows), 0)),
        full((MLA_HEADS // 2, 2 * KV_RANK, 2 * V_HEAD)),
        full((D_MODEL, D_MODEL)),
        full((1, D_MODEL)),
    ]
    args += [packed, gb, x, mod, wuv2, wo_bf, final_g]
    return pl.pallas_call(
        functools.partial(_back_body, has_ctx=has_ctx),
        grid=(batch // rows, seq // tq),
        in_specs=in_specs,
        out_specs=pl.BlockSpec((rows, tq, D_MODEL), lambda b, i: (b, i, 0)),
        out_shape=jax.ShapeDtypeStruct((batch, seq, D_MODEL), _F32),
        scratch_shapes=[
            pltpu.VMEM((rows * MLA_HEADS, QK_WIDTH, tq), _BF16),
            pltpu.VMEM((rows, tq, D_MODEL), _F32),
        ],
        compiler_params=pltpu.CompilerParams(
            dimension_semantics=("arbitrary", "arbitrary"),
            vmem_limit_bytes=VMEM_LIMIT_BYTES),
        name="back_lat" if has_ctx else "back_ctx",
    )(*args)


def _rope_tables(n_tokens, rotate):
    if rotate:
        rows = n_tokens // GRID_W
        inv = ROPE_THETA ** (-jnp.arange(AXIS_PAIRS, dtype=_F32) / AXIS_PAIRS)
        ang_r = jnp.arange(rows, dtype=_F32)[:, None] * inv
        ang_c = jnp.arange(GRID_W, dtype=_F32)[:, None] * inv

        def grid_table(fn):
            tr = jnp.broadcast_to(fn(ang_r)[:, None, :], (rows, GRID_W, AXIS_PAIRS))
            tc = jnp.broadcast_to(fn(ang_c)[None, :, :], (rows, GRID_W, AXIS_PAIRS))
            half = jnp.concatenate([tr, tc], axis=-1).reshape(n_tokens, 2 * AXIS_PAIRS)
            return jnp.concatenate([half, half], axis=-1)

        cos, sin = grid_table(jnp.cos), grid_table(jnp.sin)
    else:
        cos, sin = jnp.ones((n_tokens, QK_ROPE), _F32), jnp.zeros((n_tokens, QK_ROPE), _F32)
    return jnp.concatenate([cos, sin], axis=-1)


def kernel(x_prompt, x_sample, cache_ckv, cache_krope, c, c_ctx, norm_g, w_ada, b_ada, w_in, w_s,
           b_s, g_v, q_norm_g, w_uq, kv_norm_g, w_ukv, w_o, final_g):
    depth = norm_g.shape[0]
    assert depth == 1 and w_in.shape[2] == IN_WIDTH
    dec_batch = x_sample.shape[0]
    xp, xs = x_prompt, x_sample
    new_ckv, new_kr = [], []
    for l in range(depth):
        cond = jnp.concatenate(
            [c, c_ctx[None, :], jnp.zeros((16 - dec_batch - 1, D_MODEL), _F32)], axis=0)
        mod, win2, wq2, wuv2 = _weights_call(cond, w_ada[l], b_ada[l], jnp.transpose(w_in[l]),
                                             w_uq[l], w_ukv[l])
        wo_bf = w_o[l].astype(_BF16)
        ws_bf = w_s[l].astype(_BF16)
        bs_t = b_s[l].T
        gv_row = g_v[l].reshape(1, A_WIDTH)
        shared = (norm_g[l].reshape(1, D_MODEL), win2, ws_bf, bs_t, gv_row,
                  q_norm_g[l].reshape(1, Q_RANK), wq2, kv_norm_g[l].reshape(1, KV_RANK))
        fg = final_g.reshape(1, D_MODEL)

        ctx_index = lambda b: dec_batch
        ctx_batch, ctx_seq, _ = xp.shape
        pack = 2 * SUB_TILE // ctx_seq
        pk_c, gb_c, ckv_c, kr_c = _front_call(
            xp.reshape(ctx_batch // pack, pack * ctx_seq, D_MODEL), mod, ctx_index, *shared,
            _rope_tables(ctx_seq, False), True)
        unpack = lambda arr: arr.reshape(ctx_batch, ctx_seq, arr.shape[-1])
        xp = _back_call(unpack(pk_c), None, unpack(gb_c), xp, mod, ctx_index, wuv2, wo_bf, fg,
                        rows_per_step=pack)
        new_ckv.append(ckv_c.reshape(ctx_batch, 1, ctx_seq, KV_RANK))
        assert ctx_seq == SUB_TILE
        new_kr.append(jnp.swapaxes(kr_c, 2, 3))

        lat_index = lambda b: b
        pk_s, gb_s = _front_call(
            xs, mod, lat_index, *shared, _rope_tables(xs.shape[1], True), False)
        cache_k = jnp.concatenate(
            [cache_ckv[:, l], jnp.zeros(cache_krope[:, l].shape, _F32), cache_krope[:, l]],
            axis=-1).astype(_BF16)
        xs = _back_call(pk_s, cache_k, gb_s, xs, mod, lat_index, wuv2, wo_bf, fg, rows_per_step=1)
    return (xp, xs, jnp.concatenate(new_ckv, axis=1), jnp.concatenate(new_kr, axis=1))
```

```python
import functools
import math

import jax
import jax.numpy as jnp
from jax import lax
from jax.experimental import pallas as pl
from jax.experimental.pallas import tpu as pltpu

D_MODEL = 1024
GRID_W = 64
EPS = 1e-6
A_HEADS = 4
A_HEAD_DIM = 128
A_WIDTH = A_HEADS * A_HEAD_DIM
CHUNK = 128
MLA_HEADS = 4
QK_NOPE = 128
QK_ROPE = 64
V_HEAD = 128
B_WIDTH = MLA_HEADS * V_HEAD
Q_RANK = 256
KV_RANK = 128
AXIS_PAIRS = QK_ROPE // 4
ROPE_THETA = 10000.0
ATTN_SCALE = 1.0 / math.sqrt(QK_NOPE + QK_ROPE)
LOG2_E = 1.4426950408889634

QK_WIDTH = KV_RANK + 2 * QK_ROPE
OFF_U = 0
OFF_V = OFF_U + A_WIDTH
OFF_GA = OFF_V + A_WIDTH
OFF_CQ = OFF_GA + A_WIDTH
OFF_CKV = OFF_CQ + Q_RANK
OFF_KR = OFF_CKV + KV_RANK
OFF_GB = OFF_KR + 2 * QK_ROPE
IN_WIDTH = OFF_GB - QK_ROPE + B_WIDTH
IN_WIDTH2 = OFF_GB + B_WIDTH

PK_Q = 0
PK_A = PK_Q + MLA_HEADS * QK_WIDTH
PK_K = PK_A + A_WIDTH
PK_WIDTH = PK_K + QK_WIDTH

LANES = 128
SUBLANES = 8
ONES_ROWS = 16
SUB_TILE = 256
TOKEN_TILE = 512
PROJ_BLOCK = 256
OUT_ROWS = 256
KEY_CHUNK = 256
SCORE_LOOKAHEAD = 3
VMEM_LIMIT_BYTES = 56 * 1024 * 1024

_F32 = jnp.float32
_BF16 = jnp.bfloat16
_NT_DIMS = (((1,), (1,)), ((), ()))


def _silu(x):
    hx = 0.5 * x
    return hx + hx * jnp.tanh(hx)


def _gelu_tanh(x):
    return x * (0.5 * (1.0 + jnp.tanh(0.7978845608028654 * (x + 0.044715 * (x * x * x)))))


def _rmsnorm(x, g):
    ms = jnp.mean(x * x, axis=-1, keepdims=True)
    return (x * lax.rsqrt(ms + EPS)) * g


def _mod_body(cond_ref, wa_ref, wb_ref, b_ref, o_ref):
    s = _silu(cond_ref[...]).astype(_BF16)
    halves = [jnp.dot(s, w_ref[...].astype(_BF16), preferred_element_type=_F32)
              for w_ref in (wa_ref, wb_ref)]
    mod = jnp.concatenate(halves, axis=1) + b_ref[...]
    rows, cols = mod.shape
    o_ref[...] = jnp.broadcast_to(mod[:, None, :], (rows, SUBLANES, cols)).reshape(
        rows * SUBLANES, cols)


def _fold_body(wq_ref, wkv_ref, q_ref, uv_ref):
    half, quarter = LANES // 2, LANES // 4
    assert QK_NOPE == LANES and QK_ROPE == half and V_HEAD == LANES
    lane = lax.broadcasted_iota(jnp.int32, (Q_RANK, LANES), 1)
    low = lane < half

    def cols(block):
        return wq_ref[:, block * LANES:(block + 1) * LANES]

    for hd in range(MLA_HEADS):
        start = hd * (QK_NOPE + QK_ROPE)
        b0, odd = divmod(start, LANES)
        if odd == 0:
            nope = cols(b0)
            rope_blk = cols(b0 + 1)
            rope_lo = rope_blk
            rope_hi = pltpu.roll(rope_blk, half, 1)
        else:
            assert odd == half
            nope = jnp.where(low, pltpu.roll(cols(b0), half, 1), pltpu.roll(cols(b0 + 1), half, 1))
            rope_blk = cols(b0 + 1)
            rope_lo = pltpu.roll(rope_blk, half, 1)
            rope_hi = rope_blk
        partner = jnp.where(lane < half + quarter,
                            -pltpu.roll(rope_hi, LANES - quarter, 1),
                            pltpu.roll(rope_hi, quarter, 1))
        w_uk = wkv_ref[:, hd * (QK_NOPE + V_HEAD):hd * (QK_NOPE + V_HEAD) + QK_NOPE]
        absorbed = lax.dot_general(nope, w_uk, _NT_DIMS, precision=lax.Precision.HIGHEST,
                                   preferred_element_type=_F32)
        base = hd * QK_WIDTH
        q_ref[:, base:base + KV_RANK] = absorbed.astype(_BF16)
        q_ref[:, base + KV_RANK:base + QK_WIDTH] = jnp.where(low, rope_lo, partner).astype(_BF16)

    uv_ref[...] = jnp.zeros(uv_ref.shape, _BF16)
    for hd in range(MLA_HEADS):
        w_uv = wkv_ref[:, hd * (QK_NOPE + V_HEAD) + QK_NOPE:(hd + 1) * (QK_NOPE + V_HEAD)]
        pair, pos = divmod(hd, 2)
        uv_ref[pair, pos * KV_RANK:(pos + 1) * KV_RANK, pos * V_HEAD:(pos + 1) * V_HEAD] = (
            w_uv.astype(_BF16))


_KR_BLOCK = OFF_KR // LANES


_PREP_SPLIT = IN_WIDTH2 // 2


def _prep_body(step, w_ref, o_ref):
    half = QK_ROPE // 2
    blocks_per_step = _PREP_SPLIT // LANES
    base = step * _PREP_SPLIT
    for j in range(step * blocks_per_step, (step + 1) * blocks_per_step):
        if j < _KR_BLOCK:
            blk = w_ref[j * LANES - base:(j + 1) * LANES - base, :]
        elif j == _KR_BLOCK:
            kr = w_ref[OFF_KR - base:OFF_KR - base + QK_ROPE, :]
            partner = jnp.concatenate([-kr[half:, :], kr[:half, :]], axis=0)
            blk = jnp.concatenate([kr, partner], axis=0)
        else:
            lo = j * LANES - QK_ROPE - base
            blk = w_ref[lo:lo + LANES, :]
        o_ref[:, j * LANES - base:(j + 1) * LANES - base] = blk.T.astype(_BF16)


def _weights_body(cond_ref, wa_ref, wb_ref, b_ref, win_ref, wq_ref, wkv_ref,
                  mod_ref, win2_ref, q_ref, uv_ref):
    j = pl.program_id(0)
    _mod_body(cond_ref, wa_ref, wb_ref, b_ref, mod_ref)
    n_prep = IN_WIDTH2 // _PREP_SPLIT
    for step in range(n_prep):
        pl.when(j == step)(functools.partial(_prep_body, step, win_ref, win2_ref))
    pl.when(j == n_prep)(functools.partial(_fold_body, wq_ref, wkv_ref, q_ref, uv_ref))


def _weights_call(cond, w_ada, b_ada, w_in_t, w_uq, w_ukv):
    rows = cond.shape[0]
    n_out = w_ada.shape[1]
    col_block = D_MODEL
    n_steps = n_out // col_block
    n_prep = IN_WIDTH2 // _PREP_SPLIT
    assert _KR_BLOCK * LANES == OFF_KR and OFF_KR >= _PREP_SPLIT and n_steps == n_prep + 1
    prep_step = lambda j: jnp.minimum(j, n_prep - 1)
    full = lambda shape: pl.BlockSpec(shape, lambda j: (0,) * len(shape))
    uv_shape = (MLA_HEADS // 2, 2 * KV_RANK, 2 * V_HEAD)
    return pl.pallas_call(
        _weights_body,
        grid=(n_steps,),
        in_specs=[
            full((rows, D_MODEL)),
            pl.BlockSpec((D_MODEL, col_block // 2), lambda j: (0, 2 * j)),
            pl.BlockSpec((D_MODEL, col_block // 2), lambda j: (0, 2 * j + 1)),
            pl.BlockSpec((1, col_block), lambda j: (0, j)),
            pl.BlockSpec((_PREP_SPLIT, D_MODEL), lambda j: (prep_step(j), 0)),
            full(w_uq.shape),
            full(w_ukv.shape),
        ],
        out_specs=[
            pl.BlockSpec((rows * SUBLANES, col_block), lambda j: (0, j)),
            pl.BlockSpec((D_MODEL, _PREP_SPLIT), lambda j: (0, prep_step(j))),
            full((Q_RANK, MLA_HEADS * QK_WIDTH)),
            full(uv_shape),
        ],
        out_shape=[
            jax.ShapeDtypeStruct((rows * SUBLANES, n_out), _F32),
            jax.ShapeDtypeStruct((D_MODEL, IN_WIDTH2), _BF16),
            jax.ShapeDtypeStruct((Q_RANK, MLA_HEADS * QK_WIDTH), _BF16),
            jax.ShapeDtypeStruct(uv_shape, _BF16),
        ],
        compiler_params=pltpu.CompilerParams(
            dimension_semantics=("arbitrary",), vmem_limit_bytes=VMEM_LIMIT_BYTES),
        name="weights",
    )(cond, w_ada, w_ada, b_ada.reshape(1, n_out), w_in_t, w_uq, w_ukv)


def _front_substep(x_ref, mod_ref, ng_ref, win_ref, ws_ref, bs_ref, gv_ref, qg_ref, wq_ref,
                   kvg_ref, tables, out_refs, row0, h_write, h_read, z_write, z_read, emit_cache):
    if emit_cache:
        pk_ref, gb_ref, ckv_ref, kr_ref = out_refs
    else:
        pk_ref, gb_ref = out_refs
    first_half = lax.broadcasted_iota(jnp.int32, tables.shape, 1) < QK_ROPE
    t1 = jnp.where(first_half, tables, 0.0)
    t2k = jnp.where(first_half, pltpu.roll(tables, QK_ROPE, 1), 0.0)
    t2q = jnp.where(first_half, t2k, 1.0)
    rows = slice(row0, row0 + SUB_TILE)
    half = SUB_TILE // 2
    assert IN_WIDTH2 == 10 * PROJ_BLOCK

    def project(j):
        lo = j * PROJ_BLOCK
        z_write[:, lo:lo + PROJ_BLOCK] = jnp.dot(h_read[...], win_ref[:, lo:lo + PROJ_BLOCK],
                                                 preferred_element_type=_F32)

    shift = mod_ref[0:1, 0:D_MODEL]
    gain = ng_ref[...] * (1.0 + mod_ref[0:1, D_MODEL:2 * D_MODEL])

    def norm_rows(lo):
        x = x_ref[0, row0 + lo:row0 + lo + half, :]
        ms = jnp.mean(x * x, axis=-1, keepdims=True)
        h_write[lo:lo + half, :] = ((x * lax.rsqrt(ms + EPS)) * gain + shift).astype(_BF16)

    def vn_head(hd):
        lo = hd * A_HEAD_DIM
        v = _gelu_tanh(z_read[:, OFF_V + lo:OFF_V + lo + A_HEAD_DIM])
        return _rmsnorm(v, gv_ref[:, lo:lo + A_HEAD_DIM]).astype(_BF16)

    def mix_head(hd, vn):
        bias = bs_ref[:, hd:hd + 1]
        parts = [jnp.dot(ws_ref[hd], vn[c * CHUNK:(c + 1) * CHUNK, :],
                         preferred_element_type=_F32) + bias for c in range(SUB_TILE // CHUNK)]
        return jnp.concatenate(parts, axis=0)

    def gate_a(hd, mixed):
        lo = hd * A_HEAD_DIM
        u = _gelu_tanh(z_read[:, OFF_U + lo:OFF_U + lo + A_HEAD_DIM])
        g = _silu(z_read[:, OFF_GA + lo:OFF_GA + lo + A_HEAD_DIM])
        pk_ref[0, rows, PK_A + lo:PK_A + lo + A_HEAD_DIM] = (u * mixed * g).astype(_BF16)

    def gate_b(lo, width):
        gb_ref[0, rows, lo:lo + width] = _silu(z_read[:, OFF_GB + lo:OFF_GB + lo + width])

    def rope_q(qa, hd):
        base = hd * QK_WIDTH
        pair = qa[:, base + KV_RANK:base + QK_WIDTH]
        rope = pair * t1 + pltpu.roll(pair, QK_ROPE, 1) * t2q
        pk_ref[0, rows, PK_Q + base:PK_Q + base + QK_WIDTH] = jnp.concatenate(
            [qa[:, base:base + KV_RANK], rope], axis=1).astype(_BF16)

    project(0)
    cq = _rmsnorm(z_read[:, OFF_CQ:OFF_CQ + Q_RANK],
                  qg_ref[...] * (ATTN_SCALE * LOG2_E)).astype(_BF16)
    vn0 = vn_head(0)
    project(1)
    vn1 = vn_head(1)
    gate_b(0, B_WIDTH // 2)
    project(2)
    vn2 = vn_head(2)
    gate_b(B_WIDTH // 2, B_WIDTH // 2)
    project(3)
    vn3 = vn_head(3)
    ckv = _rmsnorm(z_read[:, OFF_CKV:OFF_CKV + KV_RANK], kvg_ref[...])
    kpair = z_read[:, OFF_KR:OFF_KR + 2 * QK_ROPE]
    krot = kpair * t1 + pltpu.roll(kpair, QK_ROPE, 1) * t2k
    pk_ref[0, rows, PK_K:PK_K + QK_WIDTH] = jnp.concatenate([ckv, krot], axis=1).astype(_BF16)
    if emit_cache:
        ckv_ref[0, 0, rows, :] = ckv
        kr_ref[row0 // SUB_TILE, 0] = kpair.T[0:QK_ROPE, :]
    qa = jnp.dot(cq, wq_ref[...], preferred_element_type=_F32)
    norm_rows(0)
    project(4)
    rope_q(qa, 0)
    rope_q(qa, 1)
    mixed0 = mix_head(0, vn0)
    mixed1 = mix_head(1, vn1)
    rope_q(qa, 2)
    rope_q(qa, 3)
    project(5)
    gate_a(0, mixed0)
    mixed2 = mix_head(2, vn2)
    mixed3 = mix_head(3, vn3)
    project(6)
    gate_a(1, mixed1)
    norm_rows(half)
    project(7)
    gate_a(2, mixed2)
    project(8)
    gate_a(3, mixed3)
    project(9)


def _front_body(x_ref, mod_ref, ng_ref, win_ref, ws_ref, bs_ref, gv_ref, qg_ref, wq_ref,
                kvg_ref, tab_ref, *rest, emit_cache, tiles_per_row):
    out_refs, (h_a, h_b, z_a, z_b) = rest[:-4], rest[-4:]
    t = pl.program_id(0)
    tile = x_ref.shape[1]
    assert tile == 2 * SUB_TILE
    n_pos = tab_ref.shape[0]

    @pl.when(t == 0)
    def _():
        h_b[...] = jnp.zeros(h_b.shape, _BF16)
        z_b[...] = jnp.zeros(z_b.shape, _F32)

    pos_base = (jnp.maximum(t - 1, 0) % tiles_per_row) * tile
    for sub, (h_write, h_read, z_write, z_read) in enumerate(
            ((h_a, h_b, z_a, z_b), (h_b, h_a, z_b, z_a))):
        row0 = sub * SUB_TILE
        pos = pl.multiple_of((pos_base + row0) % n_pos, SUB_TILE)
        tables = tab_ref[pl.ds(pos, SUB_TILE), :]
        _front_substep(x_ref, mod_ref, ng_ref, win_ref, ws_ref, bs_ref, gv_ref, qg_ref, wq_ref,
                       kvg_ref, tables, out_refs, row0, h_write, h_read, z_write, z_read,
                       emit_cache)


def _front_call(x, mod, mod_index, norm_g, win2, ws_bf, bs_t, g_v, q_norm_g, wq2, kv_norm_g,
                tables, emit_cache):
    batch, seq, _ = x.shape
    tile = 2 * SUB_TILE
    tiles_per_row = seq // tile
    n_tiles = batch * tiles_per_row
    assert seq % tile == 0 and tables.shape[0] % SUB_TILE == 0

    def in_tile(t):
        tt = jnp.minimum(t, n_tiles - 1)
        return tt // tiles_per_row, tt % tiles_per_row

    def out_tile(t):
        tt = jnp.maximum(t - 1, 0)
        return tt // tiles_per_row, tt % tiles_per_row

    full = lambda shape: pl.BlockSpec(shape, lambda t: (0,) * len(shape))
    in_specs = [
        pl.BlockSpec((1, tile, D_MODEL), lambda t: (*in_tile(t), 0)),
        pl.BlockSpec((SUBLANES, 3 * D_MODEL), lambda t: (mod_index(in_tile(t)[0]), 0)),
        full((1, D_MODEL)),
        full((D_MODEL, IN_WIDTH2)),
        full((A_HEADS, CHUNK, CHUNK)),
        full((CHUNK, A_HEADS)),
        full((1, A_WIDTH)),
        full((1, Q_RANK)),
        full((Q_RANK, MLA_HEADS * QK_WIDTH)),
        full((1, KV_RANK)),
        full(tables.shape),
    ]
    out_shape = [
        jax.ShapeDtypeStruct((batch, seq, PK_WIDTH), _BF16),
        jax.ShapeDtypeStruct((batch, seq, B_WIDTH), _F32),
    ]
    out_specs = [
        pl.BlockSpec((1, tile, PK_WIDTH), lambda t: (*out_tile(t), 0)),
        pl.BlockSpec((1, tile, B_WIDTH), lambda t: (*out_tile(t), 0)),
    ]
    if emit_cache:
        out_shape += [
            jax.ShapeDtypeStruct((batch, 1, seq, KV_RANK), _F32),
            jax.ShapeDtypeStruct((n_tiles * 2, 1, QK_ROPE, SUB_TILE), _F32),
        ]
        out_specs += [
            pl.BlockSpec((1, 1, tile, KV_RANK), lambda t: (out_tile(t)[0], 0, out_tile(t)[1], 0)),
            pl.BlockSpec((2, 1, QK_ROPE, SUB_TILE), lambda t: (jnp.maximum(t - 1, 0), 0, 0, 0)),
        ]
    return pl.pallas_call(
        functools.partial(_front_body, emit_cache=emit_cache, tiles_per_row=tiles_per_row),
        grid=(n_tiles + 1,),
        in_specs=in_specs,
        out_specs=out_specs,
        out_shape=out_shape,
        scratch_shapes=[pltpu.VMEM((SUB_TILE, D_MODEL), _BF16), pltpu.VMEM((SUB_TILE, D_MODEL), _BF16),
                        pltpu.VMEM((SUB_TILE, IN_WIDTH2), _F32),
                        pltpu.VMEM((SUB_TILE, IN_WIDTH2), _F32)],
        compiler_params=pltpu.CompilerParams(
            dimension_semantics=("arbitrary",),
            vmem_limit_bytes=VMEM_LIMIT_BYTES),
        name="front_ctx" if emit_cache else "front_lat",
    )(x, mod, norm_g, win2, ws_bf, bs_t, g_v, q_norm_g, wq2, kv_norm_g, tables)


def _key_chunks(k_refs):
    chunks = []
    for k_ref in k_refs:
        size = min(KEY_CHUNK, k_ref.shape[1])
        assert k_ref.shape[1] % size == 0
        chunks += [(k_ref, lo, size) for lo in range(0, k_ref.shape[1], size)]
    return chunks


def _values_t(k_ref, row):
    ckv_t = k_ref[row, :, 0:KV_RANK].T
    return jnp.concatenate([ckv_t, jnp.ones((ONES_ROWS, ckv_t.shape[1]), _BF16)], axis=0)


def _attention(q_ref, k_refs, qt_ref, fillers):
    n_rows = q_ref.shape[0]
    chunks = _key_chunks(k_refs)
    steps = [(row, hd, c) for row in range(n_rows) for c in range(len(chunks))
             for hd in range(MLA_HEADS)]

    for row in range(n_rows):
        for hd in range(MLA_HEADS):
            qt_ref[row * MLA_HEADS + hd] = q_ref[row, :, hd * QK_WIDTH:(hd + 1) * QK_WIDTH].T

    def logits(row, hd, c):
        k_ref, lo, size = chunks[c]
        return jnp.dot(k_ref[row, lo:lo + size, :], qt_ref[row * MLA_HEADS + hd],
                       preferred_element_type=_F32)

    pending = [logits(*step) for step in steps[:SCORE_LOOKAHEAD]]
    values_t = [{id(k_ref): _values_t(k_ref, row) for k_ref in k_refs} for row in range(n_rows)]
    outs = {}
    state = {}
    for j, (row, hd, c) in enumerate(steps):
        m, acc = state.get((row, hd), (None, None))
        if j + SCORE_LOOKAHEAD < len(steps):
            pending.append(logits(*steps[j + SCORE_LOOKAHEAD]))
        s, pending[j] = pending[j], None
        k_ref, lo, size = chunks[c]
        chunk_max = jnp.max(s, axis=0, keepdims=True)
        if c > 0:
            m_new = jnp.maximum(m, chunk_max)
            acc = acc * jnp.exp2(m - m_new)
            m = m_new
        else:
            m = chunk_max
        p = jnp.exp2(s - m).astype(_BF16)
        part = jnp.dot(values_t[row][id(k_ref)][:, lo:lo + size], p, preferred_element_type=_F32)
        acc = acc + part if c > 0 else part
        state[row, hd] = (m, acc)
        if j >= len(steps) - len(fillers):
            fillers[j - len(steps) + len(fillers)]()
        if c == len(chunks) - 1:
            outs[row, hd] = (acc[0:KV_RANK, :] / acc[KV_RANK:KV_RANK + 1, :]).T
    return outs


def _attention_single(q_ref, k_ref, fillers):
    n_rows, tq = q_ref.shape[0], q_ref.shape[1]
    groups = [(row, pair) for row in range(n_rows) for pair in range(MLA_HEADS // 2)]
    scores = {}
    for row, pair in groups:
        q = jnp.concatenate([q_ref[row, :, hd * QK_WIDTH:(hd + 1) * QK_WIDTH]
                             for hd in (2 * pair, 2 * pair + 1)], axis=0)
        scores[row, pair] = lax.dot_general(q, k_ref[row], _NT_DIMS, preferred_element_type=_F32)
    for fill in fillers:
        fill()
    outs = {}
    for row, pair in groups:
        s = scores[row, pair]
        ckv = k_ref[row, :, 0:KV_RANK]
        values = jnp.concatenate([ckv, jnp.ones_like(ckv)], axis=-1)
        p = jnp.exp2(s - jnp.max(s, axis=1, keepdims=True)).astype(_BF16)
        acc = jnp.dot(p, values, preferred_element_type=_F32)
        o = acc[:, 0:KV_RANK] / acc[:, KV_RANK:2 * KV_RANK]
        outs[row, 2 * pair], outs[row, 2 * pair + 1] = o[0:tq, :], o[tq:2 * tq, :]
    return outs


def _back_body(*refs, has_ctx):
    if has_ctx:
        (q_ref, klat_ref, kctx_ref, a_ref, gb_ref, x_ref, mod_ref, wuv_ref, wo_ref, fg_ref,
         y_ref, qt_ref, ya_ref) = refs
        k_refs = (kctx_ref, klat_ref)
    else:
        (q_ref, klat_ref, a_ref, gb_ref, x_ref, mod_ref, wuv_ref, wo_ref, fg_ref, y_ref,
         qt_ref, ya_ref) = refs
        k_refs = (klat_ref,)
    n_rows, tq = q_ref.shape[0], q_ref.shape[1]

    block = min(OUT_ROWS, tq)
    blocks = [(row, lo) for row in range(n_rows) for lo in range(0, tq, block)]

    def a_part(row, lo):
        def run():
            ya_ref[row, lo:lo + block, :] = jnp.dot(
                a_ref[row, lo:lo + block, :], wo_ref[0:A_WIDTH, :], preferred_element_type=_F32)
        return run

    fillers = [a_part(*b) for b in blocks]
    if len(_key_chunks(k_refs)) == 1:
        outs = _attention_single(q_ref, k_refs[0], fillers)
    else:
        outs = _attention(q_ref, k_refs, qt_ref, fillers)

    gate = mod_ref[0:1, 2 * D_MODEL:3 * D_MODEL]
    for row, lo in blocks:
        rows = slice(lo, lo + block)
        attn_cols = []
        for pair in range(MLA_HEADS // 2):
            o2 = jnp.concatenate([outs[row, 2 * pair][rows, :],
                                  outs[row, 2 * pair + 1][rows, :]], axis=1)
            attn_cols.append(jnp.dot(o2.astype(_BF16), wuv_ref[pair],
                                     preferred_element_type=_F32))
        attn = jnp.concatenate(attn_cols, axis=1) * gb_ref[row, rows, :]
        y = ya_ref[row, rows, :] + jnp.dot(
            attn.astype(_BF16), wo_ref[A_WIDTH:A_WIDTH + B_WIDTH, :], preferred_element_type=_F32)
        out = x_ref[row, rows, :] + gate * y
        y_ref[row, rows, :] = _rmsnorm(out, fg_ref[...])


def _back_call(packed, kctx, gb, x, mod, mod_index, wuv2, wo_bf, final_g, rows_per_step):
    batch, seq, _ = x.shape
    tq = min(TOKEN_TILE, seq)
    has_ctx = kctx is not None
    rows = rows_per_step
    assert batch % rows == 0 and (rows == 1 or seq == tq)
    full = lambda shape: pl.BlockSpec(shape, lambda b, i: (0,) * len(shape))
    keys = lambda arr: pl.BlockSpec((rows, arr.shape[1], arr.shape[2]), lambda b, i: (b, 0, 0))
    in_specs = [
        pl.BlockSpec((rows, tq, MLA_HEADS * QK_WIDTH),
                     lambda b, i: (b, i, PK_Q // (MLA_HEADS * QK_WIDTH))),
        pl.BlockSpec((rows, seq, QK_WIDTH), lambda b, i: (b, 0, PK_K // QK_WIDTH)),
    ]
    args = [packed, packed]
    if has_ctx:
        in_specs += [keys(kctx)]
        args += [kctx]
    in_specs += [
        pl.BlockSpec((rows, tq, A_WIDTH), lambda b, i: (b, i, PK_A // A_WIDTH)),
        pl.BlockSpec((rows, tq, B_WIDTH), lambda b, i: (b, i, 0)),
        pl.BlockSpec((rows, tq, D_MODEL), lambda b, i: (b, i, 0)),
        pl.BlockSpec((SUBLANES, 3 * D_MODEL), lambda b, i: (mod_index(b * rows), 0)),
        full((MLA_HEADS // 2, 2 * KV_RANK, 2 * V_HEAD)),
        full((D_MODEL, D_MODEL)),
        full((1, D_MODEL)),
    ]
    args += [packed, gb, x, mod, wuv2, wo_bf, final_g]
    return pl.pallas_call(
        functools.partial(_back_body, has_ctx=has_ctx),
        grid=(batch // rows, seq // tq),
        in_specs=in_specs,
        out_specs=pl.BlockSpec((rows, tq, D_MODEL), lambda b, i: (b, i, 0)),
        out_shape=jax.ShapeDtypeStruct((batch, seq, D_MODEL), _F32),
        scratch_shapes=[
            pltpu.VMEM((rows * MLA_HEADS, QK_WIDTH, tq), _BF16),
            pltpu.VMEM((rows, tq, D_MODEL), _F32),
        ],
        compiler_params=pltpu.CompilerParams(
            dimension_semantics=("arbitrary", "arbitrary"),
            vmem_limit_bytes=VMEM_LIMIT_BYTES),
        name="back_lat" if has_ctx else "back_ctx",
    )(*args)


def _rope_tables(n_tokens, rotate):
    if rotate:
        rows = n_tokens // GRID_W
        inv = ROPE_THETA ** (-jnp.arange(AXIS_PAIRS, dtype=_F32) / AXIS_PAIRS)
        ang_r = jnp.arange(rows, dtype=_F32)[:, None] * inv
        ang_c = jnp.arange(GRID_W, dtype=_F32)[:, None] * inv

        def grid_table(fn):
            tr = jnp.broadcast_to(fn(ang_r)[:, None, :], (rows, GRID_W, AXIS_PAIRS))
            tc = jnp.broadcast_to(fn(ang_c)[None, :, :], (rows, GRID_W, AXIS_PAIRS))
            half = jnp.concatenate([tr, tc], axis=-1).reshape(n_tokens, 2 * AXIS_PAIRS)
            return jnp.concatenate([half, half], axis=-1)

        cos, sin = grid_table(jnp.cos), grid_table(jnp.sin)
    else:
        cos, sin = jnp.ones((n_tokens, QK_ROPE), _F32), jnp.zeros((n_tokens, QK_ROPE), _F32)
    return jnp.concatenate([cos, sin], axis=-1)


def kernel(x_prompt, x_sample, cache_ckv, cache_krope, c, c_ctx, norm_g, w_ada, b_ada, w_in, w_s,
           b_s, g_v, q_norm_g, w_uq, kv_norm_g, w_ukv, w_o, final_g):
    depth = norm_g.shape[0]
    assert depth == 1 and w_in.shape[2] == IN_WIDTH
    dec_batch = x_sample.shape[0]
    xp, xs = x_prompt, x_sample
    new_ckv, new_kr = [], []
    for l in range(depth):
        cond = jnp.concatenate(
            [c, c_ctx[None, :], jnp.zeros((16 - dec_batch - 1, D_MODEL), _F32)], axis=0)
        mod, win2, wq2, wuv2 = _weights_call(cond, w_ada[l], b_ada[l], jnp.transpose(w_in[l]),
                                             w_uq[l], w_ukv[l])
        wo_bf = w_o[l].astype(_BF16)
        ws_bf = w_s[l].astype(_BF16)
        bs_t = b_s[l].T
        gv_row = g_v[l].reshape(1, A_WIDTH)
        shared = (norm_g[l].reshape(1, D_MODEL), win2, ws_bf, bs_t, gv_row,
                  q_norm_g[l].reshape(1, Q_RANK), wq2, kv_norm_g[l].reshape(1, KV_RANK))
        fg = final_g.reshape(1, D_MODEL)

        ctx_index = lambda b: dec_batch
        ctx_batch, ctx_seq, _ = xp.shape
        pack = 2 * SUB_TILE // ctx_seq
        pk_c, gb_c, ckv_c, kr_c = _front_call(
            xp.reshape(ctx_batch // pack, pack * ctx_seq, D_MODEL), mod, ctx_index, *shared,
            _rope_tables(ctx_seq, False), True)
        unpack = lambda arr: arr.reshape(ctx_batch, ctx_seq, arr.shape[-1])
        xp = _back_call(unpack(pk_c), None, unpack(gb_c), xp, mod, ctx_index, wuv2, wo_bf, fg,
                        rows_per_step=pack)
        new_ckv.append(ckv_c.reshape(ctx_batch, 1, ctx_seq, KV_RANK))
        assert ctx_seq == SUB_TILE
        new_kr.append(jnp.swapaxes(kr_c, 2, 3))

        lat_index = lambda b: b
        pk_s, gb_s = _front_call(
            xs, mod, lat_index, *shared, _rope_tables(xs.shape[1], True), False)
        cache_k = jnp.concatenate(
            [cache_ckv[:, l], jnp.zeros(cache_krope[:, l].shape, _F32), cache_krope[:, l]],
            axis=-1).astype(_BF16)
        xs = _back_call(pk_s, cache_k, gb_s, xs, mod, lat_index, wuv2, wo_bf, fg, rows_per_step=1)
    return (xp, xs, jnp.concatenate(new_ckv, axis=1), jnp.concatenate(new_kr, axis=1))
```

```python
import functools
import math

import jax
import jax.numpy as jnp
from jax import lax
from jax.experimental import pallas as pl
from jax.experimental.pallas import tpu as pltpu

D_MODEL = 1024
GRID_W = 64
EPS = 1e-6
A_HEADS = 4
A_HEAD_DIM = 128
A_WIDTH = A_HEADS * A_HEAD_DIM
CHUNK = 128
MLA_HEADS = 4
QK_NOPE = 128
QK_ROPE = 64
V_HEAD = 128
B_WIDTH = MLA_HEADS * V_HEAD
Q_RANK = 256
KV_RANK = 128
AXIS_PAIRS = QK_ROPE // 4
ROPE_THETA = 10000.0
ATTN_SCALE = 1.0 / math.sqrt(QK_NOPE + QK_ROPE)
LOG2_E = 1.4426950408889634

QK_WIDTH = KV_RANK + 2 * QK_ROPE
OFF_U = 0
OFF_V = OFF_U + A_WIDTH
OFF_GA = OFF_V + A_WIDTH
OFF_CQ = OFF_GA + A_WIDTH
OFF_CKV = OFF_CQ + Q_RANK
OFF_KR = OFF_CKV + KV_RANK
OFF_GB = OFF_KR + 2 * QK_ROPE
IN_WIDTH = OFF_GB - QK_ROPE + B_WIDTH
IN_WIDTH2 = OFF_GB + B_WIDTH

PK_Q = 0
PK_A = PK_Q + MLA_HEADS * QK_WIDTH
PK_K = PK_A + A_WIDTH
PK_WIDTH = PK_K + QK_WIDTH

LANES = 128
SUBLANES = 8
ONES_ROWS = 16
SUB_TILE = 256
TOKEN_TILE = 512
PROJ_BLOCK = 256
OUT_ROWS = 256
KEY_CHUNK = 256
SCORE_LOOKAHEAD = 2
VMEM_LIMIT_BYTES = 56 * 1024 * 1024

_F32 = jnp.float32
_BF16 = jnp.bfloat16
_NT_DIMS = (((1,), (1,)), ((), ()))


def _silu(x):
    hx = 0.5 * x
    return hx + hx * jnp.tanh(hx)


def _gelu_tanh(x):
    return x * (0.5 * (1.0 + jnp.tanh(0.7978845608028654 * (x + 0.044715 * (x * x * x)))))


def _rmsnorm(x, g):
    ms = jnp.mean(x * x, axis=-1, keepdims=True)
    return (x * lax.rsqrt(ms + EPS)) * g


def _mod_body(cond_ref, wa_ref, wb_ref, b_ref, o_ref):
    s = _silu(cond_ref[...]).astype(_BF16)
    halves = [jnp.dot(s, w_ref[...].astype(_BF16), preferred_element_type=_F32)
              for w_ref in (wa_ref, wb_ref)]
    mod = jnp.concatenate(halves, axis=1) + b_ref[...]
    rows, cols = mod.shape
    o_ref[...] = jnp.broadcast_to(mod[:, None, :], (rows, SUBLANES, cols)).reshape(
        rows * SUBLANES, cols)


def _fold_body(wq_ref, wkv_ref, q_ref, uv_ref):
    half, quarter = LANES // 2, LANES // 4
    assert QK_NOPE == LANES and QK_ROPE == half and V_HEAD == LANES
    lane = lax.broadcasted_iota(jnp.int32, (Q_RANK, LANES), 1)
    low = lane < half

    def cols(block):
        return wq_ref[:, block * LANES:(block + 1) * LANES]

    for hd in range(MLA_HEADS):
        start = hd * (QK_NOPE + QK_ROPE)
        b0, odd = divmod(start, LANES)
        if odd == 0:
            nope = cols(b0)
            rope_blk = cols(b0 + 1)
            rope_lo = rope_blk
            rope_hi = pltpu.roll(rope_blk, half, 1)
        else:
            assert odd == half
            nope = jnp.where(low, pltpu.roll(cols(b0), half, 1), pltpu.roll(cols(b0 + 1), half, 1))
            rope_blk = cols(b0 + 1)
            rope_lo = pltpu.roll(rope_blk, half, 1)
            rope_hi = rope_blk
        partner = jnp.where(lane < half + quarter,
                            -pltpu.roll(rope_hi, LANES - quarter, 1),
                            pltpu.roll(rope_hi, quarter, 1))
        w_uk = wkv_ref[:, hd * (QK_NOPE + V_HEAD):hd * (QK_NOPE + V_HEAD) + QK_NOPE]
        absorbed = lax.dot_general(nope, w_uk, _NT_DIMS, precision=lax.Precision.HIGHEST,
                                   preferred_element_type=_F32)
        base = hd * QK_WIDTH
        q_ref[:, base:base + KV_RANK] = absorbed.astype(_BF16)
        q_ref[:, base + KV_RANK:base + QK_WIDTH] = jnp.where(low, rope_lo, partner).astype(_BF16)

    uv_ref[...] = jnp.zeros(uv_ref.shape, _BF16)
    for hd in range(MLA_HEADS):
        w_uv = wkv_ref[:, hd * (QK_NOPE + V_HEAD) + QK_NOPE:(hd + 1) * (QK_NOPE + V_HEAD)]
        pair, pos = divmod(hd, 2)
        uv_ref[pair, pos * KV_RANK:(pos + 1) * KV_RANK, pos * V_HEAD:(pos + 1) * V_HEAD] = (
            w_uv.astype(_BF16))


_KR_BLOCK = OFF_KR // LANES


_PREP_SPLIT = IN_WIDTH2 // 2


def _prep_body(step, w_ref, o_ref):
    half = QK_ROPE // 2
    blocks_per_step = _PREP_SPLIT // LANES
    base = step * _PREP_SPLIT
    for j in range(step * blocks_per_step, (step + 1) * blocks_per_step):
        if j < _KR_BLOCK:
            blk = w_ref[j * LANES - base:(j + 1) * LANES - base, :]
        elif j == _KR_BLOCK:
            kr = w_ref[OFF_KR - base:OFF_KR - base + QK_ROPE, :]
            partner = jnp.concatenate([-kr[half:, :], kr[:half, :]], axis=0)
            blk = jnp.concatenate([kr, partner], axis=0)
        else:
            lo = j * LANES - QK_ROPE - base
            blk = w_ref[lo:lo + LANES, :]
        o_ref[:, j * LANES - base:(j + 1) * LANES - base] = blk.T.astype(_BF16)


def _weights_body(cond_ref, wa_ref, wb_ref, b_ref, win_ref, wq_ref, wkv_ref,
                  mod_ref, win2_ref, q_ref, uv_ref):
    j = pl.program_id(0)
    _mod_body(cond_ref, wa_ref, wb_ref, b_ref, mod_ref)
    n_prep = IN_WIDTH2 // _PREP_SPLIT
    for step in range(n_prep):
        pl.when(j == step)(functools.partial(_prep_body, step, win_ref, win2_ref))
    pl.when(j == n_prep)(functools.partial(_fold_body, wq_ref, wkv_ref, q_ref, uv_ref))


def _weights_call(cond, w_ada, b_ada, w_in_t, w_uq, w_ukv):
    rows = cond.shape[0]
    n_out = w_ada.shape[1]
    col_block = D_MODEL
    n_steps = n_out // col_block
    n_prep = IN_WIDTH2 // _PREP_SPLIT
    assert _KR_BLOCK * LANES == OFF_KR and OFF_KR >= _PREP_SPLIT and n_steps == n_prep + 1
    prep_step = lambda j: jnp.minimum(j, n_prep - 1)
    full = lambda shape: pl.BlockSpec(shape, lambda j: (0,) * len(shape))
    uv_shape = (MLA_HEADS // 2, 2 * KV_RANK, 2 * V_HEAD)
    return pl.pallas_call(
        _weights_body,
        grid=(n_steps,),
        in_specs=[
            full((rows, D_MODEL)),
            pl.BlockSpec((D_MODEL, col_block // 2), lambda j: (0, 2 * j)),
            pl.BlockSpec((D_MODEL, col_block // 2), lambda j: (0, 2 * j + 1)),
            pl.BlockSpec((1, col_block), lambda j: (0, j)),
            pl.BlockSpec((_PREP_SPLIT, D_MODEL), lambda j: (prep_step(j), 0)),
            full(w_uq.shape),
            full(w_ukv.shape),
        ],
        out_specs=[
            pl.BlockSpec((rows * SUBLANES, col_block), lambda j: (0, j)),
            pl.BlockSpec((D_MODEL, _PREP_SPLIT), lambda j: (0, prep_step(j))),
            full((Q_RANK, MLA_HEADS * QK_WIDTH)),
            full(uv_shape),
        ],
        out_shape=[
            jax.ShapeDtypeStruct((rows * SUBLANES, n_out), _F32),
            jax.ShapeDtypeStruct((D_MODEL, IN_WIDTH2), _BF16),
            jax.ShapeDtypeStruct((Q_RANK, MLA_HEADS * QK_WIDTH), _BF16),
            jax.ShapeDtypeStruct(uv_shape, _BF16),
        ],
        compiler_params=pltpu.CompilerParams(
            dimension_semantics=("arbitrary",), vmem_limit_bytes=VMEM_LIMIT_BYTES),
        name="weights",
    )(cond, w_ada, w_ada, b_ada.reshape(1, n_out), w_in_t, w_uq, w_ukv)


def _front_substep(x_ref, mod_ref, ng_ref, win_ref, ws_ref, bs_ref, gv_ref, qg_ref, wq_ref,
                   kvg_ref, tables, out_refs, row0, h_write, h_read, z_write, z_read, emit_cache):
    if emit_cache:
        pk_ref, gb_ref, ckv_ref, kr_ref = out_refs
    else:
        pk_ref, gb_ref = out_refs
    first_half = lax.broadcasted_iota(jnp.int32, tables.shape, 1) < QK_ROPE
    t1 = jnp.where(first_half, tables, 0.0)
    t2k = jnp.where(first_half, pltpu.roll(tables, QK_ROPE, 1), 0.0)
    t2q = jnp.where(first_half, t2k, 1.0)
    rows = slice(row0, row0 + SUB_TILE)
    half = SUB_TILE // 2
    assert IN_WIDTH2 == 10 * PROJ_BLOCK

    def project(j):
        lo = j * PROJ_BLOCK
        z_write[:, lo:lo + PROJ_BLOCK] = jnp.dot(h_read[...], win_ref[:, lo:lo + PROJ_BLOCK],
                                                 preferred_element_type=_F32)

    shift = mod_ref[0:1, 0:D_MODEL]
    gain = ng_ref[...] * (1.0 + mod_ref[0:1, D_MODEL:2 * D_MODEL])

    def norm_rows(lo):
        x = x_ref[0, row0 + lo:row0 + lo + half, :]
        ms = jnp.mean(x * x, axis=-1, keepdims=True)
        h_write[lo:lo + half, :] = ((x * lax.rsqrt(ms + EPS)) * gain + shift).astype(_BF16)

    def vn_head(hd):
        lo = hd * A_HEAD_DIM
        v = _gelu_tanh(z_read[:, OFF_V + lo:OFF_V + lo + A_HEAD_DIM])
        return _rmsnorm(v, gv_ref[:, lo:lo + A_HEAD_DIM]).astype(_BF16)

    def mix_head(hd, vn):
        bias = bs_ref[:, hd:hd + 1]
        parts = [jnp.dot(ws_ref[hd], vn[c * CHUNK:(c + 1) * CHUNK, :],
                         preferred_element_type=_F32) + bias for c in range(SUB_TILE // CHUNK)]
        return jnp.concatenate(parts, axis=0)

    def gate_a(hd, mixed):
        lo = hd * A_HEAD_DIM
        u = _gelu_tanh(z_read[:, OFF_U + lo:OFF_U + lo + A_HEAD_DIM])
        g = _silu(z_read[:, OFF_GA + lo:OFF_GA + lo + A_HEAD_DIM])
        pk_ref[0, rows, PK_A + lo:PK_A + lo + A_HEAD_DIM] = (u * mixed * g).astype(_BF16)

    def gate_b(lo, width):
        gb_ref[0, rows, lo:lo + width] = _silu(z_read[:, OFF_GB + lo:OFF_GB + lo + width])

    def rope_q(qa, hd):
        base = hd * QK_WIDTH
        pair = qa[:, base + KV_RANK:base + QK_WIDTH]
        rope = pair * t1 + pltpu.roll(pair, QK_ROPE, 1) * t2q
        pk_ref[0, rows, PK_Q + base:PK_Q + base + QK_WIDTH] = jnp.concatenate(
            [qa[:, base:base + KV_RANK], rope], axis=1).astype(_BF16)

    project(0)
    cq = _rmsnorm(z_read[:, OFF_CQ:OFF_CQ + Q_RANK],
                  qg_ref[...] * (ATTN_SCALE * LOG2_E)).astype(_BF16)
    vn0 = vn_head(0)
    project(1)
    vn1 = vn_head(1)
    gate_b(0, B_WIDTH // 2)
    project(2)
    vn2 = vn_head(2)
    gate_b(B_WIDTH // 2, B_WIDTH // 2)
    project(3)
    vn3 = vn_head(3)
    ckv = _rmsnorm(z_read[:, OFF_CKV:OFF_CKV + KV_RANK], kvg_ref[...])
    kpair = z_read[:, OFF_KR:OFF_KR + 2 * QK_ROPE]
    krot = kpair * t1 + pltpu.roll(kpair, QK_ROPE, 1) * t2k
    pk_ref[0, rows, PK_K:PK_K + QK_WIDTH] = jnp.concatenate([ckv, krot], axis=1).astype(_BF16)
    if emit_cache:
        ckv_ref[0, 0, rows, :] = ckv
        kr_ref[row0 // SUB_TILE, 0] = kpair.T[0:QK_ROPE, :]
    qa = jnp.dot(cq, wq_ref[...], preferred_element_type=_F32)
    norm_rows(0)
    project(4)
    rope_q(qa, 0)
    rope_q(qa, 1)
    mixed0 = mix_head(0, vn0)
    mixed1 = mix_head(1, vn1)
    rope_q(qa, 2)
    rope_q(qa, 3)
    project(5)
    gate_a(0, mixed0)
    mixed2 = mix_head(2, vn2)
    mixed3 = mix_head(3, vn3)
    project(6)
    gate_a(1, mixed1)
    norm_rows(half)
    project(7)
    gate_a(2, mixed2)
    project(8)
    gate_a(3, mixed3)
    project(9)


def _front_body(x_ref, mod_ref, ng_ref, win_ref, ws_ref, bs_ref, gv_ref, qg_ref, wq_ref,
                kvg_ref, tab_ref, *rest, emit_cache, tiles_per_row):
    out_refs, (h_a, h_b, z_a, z_b) = rest[:-4], rest[-4:]
    t = pl.program_id(0)
    tile = x_ref.shape[1]
    assert tile == 2 * SUB_TILE
    n_pos = tab_ref.shape[0]

    @pl.when(t == 0)
    def _():
        h_b[...] = jnp.zeros(h_b.shape, _BF16)
        z_b[...] = jnp.zeros(z_b.shape, _F32)

    pos_base = (jnp.maximum(t - 1, 0) % tiles_per_row) * tile
    for sub, (h_write, h_read, z_write, z_read) in enumerate(
            ((h_a, h_b, z_a, z_b), (h_b, h_a, z_b, z_a))):
        row0 = sub * SUB_TILE
        pos = pl.multiple_of((pos_base + row0) % n_pos, SUB_TILE)
        tables = tab_ref[pl.ds(pos, SUB_TILE), :]
        _front_substep(x_ref, mod_ref, ng_ref, win_ref, ws_ref, bs_ref, gv_ref, qg_ref, wq_ref,
                       kvg_ref, tables, out_refs, row0, h_write, h_read, z_write, z_read,
                       emit_cache)


def _front_call(x, mod, mod_index, norm_g, win2, ws_bf, bs_t, g_v, q_norm_g, wq2, kv_norm_g,
                tables, emit_cache):
    batch, seq, _ = x.shape
    tile = 2 * SUB_TILE
    tiles_per_row = seq // tile
    n_tiles = batch * tiles_per_row
    assert seq % tile == 0 and tables.shape[0] % SUB_TILE == 0

    def in_tile(t):
        tt = jnp.minimum(t, n_tiles - 1)
        return tt // tiles_per_row, tt % tiles_per_row

    def out_tile(t):
        tt = jnp.maximum(t - 1, 0)
        return tt // tiles_per_row, tt % tiles_per_row

    full = lambda shape: pl.BlockSpec(shape, lambda t: (0,) * len(shape))
    in_specs = [
        pl.BlockSpec((1, tile, D_MODEL), lambda t: (*in_tile(t), 0)),
        pl.BlockSpec((SUBLANES, 3 * D_MODEL), lambda t: (mod_index(in_tile(t)[0]), 0)),
        full((1, D_MODEL)),
        full((D_MODEL, IN_WIDTH2)),
        full((A_HEADS, CHUNK, CHUNK)),
        full((CHUNK, A_HEADS)),
        full((1, A_WIDTH)),
        full((1, Q_RANK)),
        full((Q_RANK, MLA_HEADS * QK_WIDTH)),
        full((1, KV_RANK)),
        full(tables.shape),
    ]
    out_shape = [
        jax.ShapeDtypeStruct((batch, seq, PK_WIDTH), _BF16),
        jax.ShapeDtypeStruct((batch, seq, B_WIDTH), _F32),
    ]
    out_specs = [
        pl.BlockSpec((1, tile, PK_WIDTH), lambda t: (*out_tile(t), 0)),
        pl.BlockSpec((1, tile, B_WIDTH), lambda t: (*out_tile(t), 0)),
    ]
    if emit_cache:
        out_shape += [
            jax.ShapeDtypeStruct((batch, 1, seq, KV_RANK), _F32),
            jax.ShapeDtypeStruct((n_tiles * 2, 1, QK_ROPE, SUB_TILE), _F32),
        ]
        out_specs += [
            pl.BlockSpec((1, 1, tile, KV_RANK), lambda t: (out_tile(t)[0], 0, out_tile(t)[1], 0)),
            pl.BlockSpec((2, 1, QK_ROPE, SUB_TILE), lambda t: (jnp.maximum(t - 1, 0), 0, 0, 0)),
        ]
    return pl.pallas_call(
        functools.partial(_front_body, emit_cache=emit_cache, tiles_per_row=tiles_per_row),
        grid=(n_tiles + 1,),
        in_specs=in_specs,
        out_specs=out_specs,
        out_shape=out_shape,
        scratch_shapes=[pltpu.VMEM((SUB_TILE, D_MODEL), _BF16), pltpu.VMEM((SUB_TILE, D_MODEL), _BF16),
                        pltpu.VMEM((SUB_TILE, IN_WIDTH2), _F32),
                        pltpu.VMEM((SUB_TILE, IN_WIDTH2), _F32)],
        compiler_params=pltpu.CompilerParams(
            dimension_semantics=("arbitrary",),
            vmem_limit_bytes=VMEM_LIMIT_BYTES),
        name="front_ctx" if emit_cache else "front_lat",
    )(x, mod, norm_g, win2, ws_bf, bs_t, g_v, q_norm_g, wq2, kv_norm_g, tables)


def _key_chunks(k_refs):
    chunks = []
    for k_ref in k_refs:
        size = min(KEY_CHUNK, k_ref.shape[1])
        assert k_ref.shape[1] % size == 0
        chunks += [(k_ref, lo, size) for lo in range(0, k_ref.shape[1], size)]
    return chunks


def _values_t(k_ref, row):
    ckv_t = k_ref[row, :, 0:KV_RANK].T
    return jnp.concatenate([ckv_t, jnp.ones((ONES_ROWS, ckv_t.shape[1]), _BF16)], axis=0)


def _attention(q_ref, k_refs, qt_ref, fillers):
    n_rows = q_ref.shape[0]
    chunks = _key_chunks(k_refs)
    steps = [(row, hd, c) for row in range(n_rows) for pair in range(MLA_HEADS // 2)
             for c in range(len(chunks)) for hd in (2 * pair, 2 * pair + 1)]

    for row in range(n_rows):
        for hd in range(MLA_HEADS):
            qt_ref[row * MLA_HEADS + hd] = q_ref[row, :, hd * QK_WIDTH:(hd + 1) * QK_WIDTH].T

    def logits(row, hd, c):
        k_ref, lo, size = chunks[c]
        return jnp.dot(k_ref[row, lo:lo + size, :], qt_ref[row * MLA_HEADS + hd],
                       preferred_element_type=_F32)

    pending = [logits(*step) for step in steps[:SCORE_LOOKAHEAD]]
    values_t = [{id(k_ref): _values_t(k_ref, row) for k_ref in k_refs} for row in range(n_rows)]
    outs = {}
    state = {}
    for j, (row, hd, c) in enumerate(steps):
        m, acc = state.get((row, hd), (None, None))
        if j + SCORE_LOOKAHEAD < len(steps):
            pending.append(logits(*steps[j + SCORE_LOOKAHEAD]))
        s, pending[j] = pending[j], None
        k_ref, lo, size = chunks[c]
        chunk_max = jnp.max(s, axis=0, keepdims=True)
        if c > 0:
            m_new = jnp.maximum(m, chunk_max)
            acc = acc * jnp.exp2(m - m_new)
            m = m_new
        else:
            m = chunk_max
        p = jnp.exp2(s - m).astype(_BF16)
        part = jnp.dot(values_t[row][id(k_ref)][:, lo:lo + size], p, preferred_element_type=_F32)
        acc = acc + part if c > 0 else part
        state[row, hd] = (m, acc)
        if j >= len(steps) - len(fillers):
            fillers[j - len(steps) + len(fillers)]()
        if c == len(chunks) - 1:
            outs[row, hd] = (acc[0:KV_RANK, :] / acc[KV_RANK:KV_RANK + 1, :]).T
    return outs


def _attention_single(q_ref, k_ref, fillers):
    n_rows, tq = q_ref.shape[0], q_ref.shape[1]
    groups = [(row, pair) for row in range(n_rows) for pair in range(MLA_HEADS // 2)]
    scores = {}
    for row, pair in groups:
        q = jnp.concatenate([q_ref[row, :, hd * QK_WIDTH:(hd + 1) * QK_WIDTH]
                             for hd in (2 * pair, 2 * pair + 1)], axis=0)
        scores[row, pair] = lax.dot_general(q, k_ref[row], _NT_DIMS, preferred_element_type=_F32)
    for fill in fillers:
        fill()
    outs = {}
    for row, pair in groups:
        s = scores[row, pair]
        ckv = k_ref[row, :, 0:KV_RANK]
        values = jnp.concatenate([ckv, jnp.ones_like(ckv)], axis=-1)
        p = jnp.exp2(s - jnp.max(s, axis=1, keepdims=True)).astype(_BF16)
        acc = jnp.dot(p, values, preferred_element_type=_F32)
        o = acc[:, 0:KV_RANK] / acc[:, KV_RANK:2 * KV_RANK]
        outs[row, 2 * pair], outs[row, 2 * pair + 1] = o[0:tq, :], o[tq:2 * tq, :]
    return outs


def _back_body(*refs, has_ctx):
    if has_ctx:
        (q_ref, klat_ref, kctx_ref, a_ref, gb_ref, x_ref, mod_ref, wuv_ref, wo_ref, fg_ref,
         y_ref, qt_ref, ya_ref) = refs
        k_refs = (kctx_ref, klat_ref)
    else:
        (q_ref, klat_ref, a_ref, gb_ref, x_ref, mod_ref, wuv_ref, wo_ref, fg_ref, y_ref,
         qt_ref, ya_ref) = refs
        k_refs = (klat_ref,)
    n_rows, tq = q_ref.shape[0], q_ref.shape[1]

    block = min(OUT_ROWS, tq)
    blocks = [(row, lo) for row in range(n_rows) for lo in range(0, tq, block)]

    def a_part(row, lo):
        def run():
            ya_ref[row, lo:lo + block, :] = jnp.dot(
                a_ref[row, lo:lo + block, :], wo_ref[0:A_WIDTH, :], preferred_element_type=_F32)
        return run

    fillers = [a_part(*b) for b in blocks]
    if len(_key_chunks(k_refs)) == 1:
        outs = _attention_single(q_ref, k_refs[0], fillers)
    else:
        outs = _attention(q_ref, k_refs, qt_ref, fillers)

    gate = mod_ref[0:1, 2 * D_MODEL:3 * D_MODEL]
    for row, lo in blocks:
        rows = slice(lo, lo + block)
        attn_cols = []
        for pair in range(MLA_HEADS // 2):
            o2 = jnp.concatenate([outs[row, 2 * pair][rows, :],
                                  outs[row, 2 * pair + 1][rows, :]], axis=1)
            attn_cols.append(jnp.dot(o2.astype(_BF16), wuv_ref[pair],
                                     preferred_element_type=_F32))
        attn = jnp.concatenate(attn_cols, axis=1) * gb_ref[row, rows, :]
        y = ya_ref[row, rows, :] + jnp.dot(
            attn.astype(_BF16), wo_ref[A_WIDTH:A_WIDTH + B_WIDTH, :], preferred_element_type=_F32)
        out = x_ref[row, rows, :] + gate * y
        y_ref[row, rows, :] = _rmsnorm(out, fg_ref[...])


def _back_call(packed, kctx, gb, x, mod, mod_index, wuv2, wo_bf, final_g, rows_per_step):
    batch, seq, _ = x.shape
    tq = min(TOKEN_TILE, seq)
    has_ctx = kctx is not None
    rows = rows_per_step
    assert batch % rows == 0 and (rows == 1 or seq == tq)
    full = lambda shape: pl.BlockSpec(shape, lambda b, i: (0,) * len(shape))
    keys = lambda arr: pl.BlockSpec((rows, arr.shape[1], arr.shape[2]), lambda b, i: (b, 0, 0))
    in_specs = [
        pl.BlockSpec((rows, tq, MLA_HEADS * QK_WIDTH),
                     lambda b, i: (b, i, PK_Q // (MLA_HEADS * QK_WIDTH))),
        pl.BlockSpec((rows, seq, QK_WIDTH), lambda b, i: (b, 0, PK_K // QK_WIDTH)),
    ]
    args = [packed, packed]
    if has_ctx:
        in_specs += [keys(kctx)]
        args += [kctx]
    in_specs += [
        pl.BlockSpec((rows, tq, A_WIDTH), lambda b, i: (b, i, PK_A // A_WIDTH)),
        pl.BlockSpec((rows, tq, B_WIDTH), lambda b, i: (b, i, 0)),
        pl.BlockSpec((rows, tq, D_MODEL), lambda b, i: (b, i, 0)),
        pl.BlockSpec((SUBLANES, 3 * D_MODEL), lambda b, i: (mod_index(b * rows), 0)),
        full((MLA_HEADS // 2, 2 * KV_RANK, 2 * V_HEAD)),
        full((D_MODEL, D_MODEL)),
        full((1, D_MODEL)),
    ]
    args += [packed, gb, x, mod, wuv2, wo_bf, final_g]
    return pl.pallas_call(
        functools.partial(_back_body, has_ctx=has_ctx),
        grid=(batch // rows, seq // tq),
        in_specs=in_specs,
        out_specs=pl.BlockSpec((rows, tq, D_MODEL), lambda b, i: (b, i, 0)),
        out_shape=jax.ShapeDtypeStruct((batch, seq, D_MODEL), _F32),
        scratch_shapes=[
            pltpu.VMEM((rows * MLA_HEADS, QK_WIDTH, tq), _BF16),
            pltpu.VMEM((rows, tq, D_MODEL), _F32),
        ],
        compiler_params=pltpu.CompilerParams(
            dimension_semantics=("arbitrary", "arbitrary"),
            vmem_limit_bytes=VMEM_LIMIT_BYTES),
        name="back_lat" if has_ctx else "back_ctx",
    )(*args)


def _rope_tables(n_tokens, rotate):
    if rotate:
        rows = n_tokens // GRID_W
        inv = ROPE_THETA ** (-jnp.arange(AXIS_PAIRS, dtype=_F32) / AXIS_PAIRS)
        ang_r = jnp.arange(rows, dtype=_F32)[:, None] * inv
        ang_c = jnp.arange(GRID_W, dtype=_F32)[:, None] * inv

        def grid_table(fn):
            tr = jnp.broadcast_to(fn(ang_r)[:, None, :], (rows, GRID_W, AXIS_PAIRS))
            tc = jnp.broadcast_to(fn(ang_c)[None, :, :], (rows, GRID_W, AXIS_PAIRS))
            half = jnp.concatenate([tr, tc], axis=-1).reshape(n_tokens, 2 * AXIS_PAIRS)
            return jnp.concatenate([half, half], axis=-1)

        cos, sin = grid_table(jnp.cos), grid_table(jnp.sin)
    else:
        cos, sin = jnp.ones((n_tokens, QK_ROPE), _F32), jnp.zeros((n_tokens, QK_ROPE), _F32)
    return jnp.concatenate([cos, sin], axis=-1)


def kernel(x_prompt, x_sample, cache_ckv, cache_krope, c, c_ctx, norm_g, w_ada, b_ada, w_in, w_s,
           b_s, g_v, q_norm_g, w_uq, kv_norm_g, w_ukv, w_o, final_g):
    depth = norm_g.shape[0]
    assert depth == 1 and w_in.shape[2] == IN_WIDTH
    dec_batch = x_sample.shape[0]
    xp, xs = x_prompt, x_sample
    new_ckv, new_kr = [], []
    for l in range(depth):
        cond = jnp.concatenate(
            [c, c_ctx[None, :], jnp.zeros((16 - dec_batch - 1, D_MODEL), _F32)], axis=0)
        mod, win2, wq2, wuv2 = _weights_call(cond, w_ada[l], b_ada[l], jnp.transpose(w_in[l]),
                                             w_uq[l], w_ukv[l])
        wo_bf = w_o[l].astype(_BF16)
        ws_bf = w_s[l].astype(_BF16)
        bs_t = b_s[l].T
        gv_row = g_v[l].reshape(1, A_WIDTH)
        shared = (norm_g[l].reshape(1, D_MODEL), win2, ws_bf, bs_t, gv_row,
                  q_norm_g[l].reshape(1, Q_RANK), wq2, kv_norm_g[l].reshape(1, KV_RANK))
        fg = final_g.reshape(1, D_MODEL)

        ctx_index = lambda b: dec_batch
        ctx_batch, ctx_seq, _ = xp.shape
        pack = 2 * SUB_TILE // ctx_seq
        pk_c, gb_c, ckv_c, kr_c = _front_call(
            xp.reshape(ctx_batch // pack, pack * ctx_seq, D_MODEL), mod, ctx_index, *shared,
            _rope_tables(ctx_seq, False), True)
        unpack = lambda arr: arr.reshape(ctx_batch, ctx_seq, arr.shape[-1])
        xp = _back_call(unpack(pk_c), None, unpack(gb_c), xp, mod, ctx_index, wuv2, wo_bf, fg,
                        rows_per_step=pack)
        new_ckv.append(ckv_c.reshape(ctx_batch, 1, ctx_seq, KV_RANK))
        assert ctx_seq == SUB_TILE
        new_kr.append(jnp.swapaxes(kr_c, 2, 3))

        lat_index = lambda b: b
        pk_s, gb_s = _front_call(
            xs, mod, lat_index, *shared, _rope_tables(xs.shape[1], True), False)
        cache_k = jnp.concatenate(
            [cache_ckv[:, l], jnp.zeros(cache_krope[:, l].shape, _F32), cache_krope[:, l]],
            axis=-1).astype(_BF16)
        xs = _back_call(pk_s, cache_k, gb_s, xs, mod, lat_index, wuv2, wo_bf, fg, rows_per_step=1)
    return (xp, xs, jnp.concatenate(new_ckv, axis=1), jnp.concatenate(new_kr, axis=1))
```

```python
import functools
import math

import jax
import jax.numpy as jnp
from jax import lax
from jax.experimental import pallas as pl
from jax.experimental.pallas import tpu as pltpu

D_MODEL = 1024
GRID_W = 64
EPS = 1e-6
A_HEADS = 4
A_HEAD_DIM = 128
A_WIDTH = A_HEADS * A_HEAD_DIM
CHUNK = 128
MLA_HEADS = 4
QK_NOPE = 128
QK_ROPE = 64
V_HEAD = 128
B_WIDTH = MLA_HEADS * V_HEAD
Q_RANK = 256
KV_RANK = 128
AXIS_PAIRS = QK_ROPE // 4
ROPE_THETA = 10000.0
ATTN_SCALE = 1.0 / math.sqrt(QK_NOPE + QK_ROPE)
LOG2_E = 1.4426950408889634

QK_WIDTH = KV_RANK + 2 * QK_ROPE
OFF_U = 0
OFF_V = OFF_U + A_WIDTH
OFF_GA = OFF_V + A_WIDTH
OFF_CQ = OFF_GA + A_WIDTH
OFF_CKV = OFF_CQ + Q_RANK
OFF_KR = OFF_CKV + KV_RANK
OFF_GB = OFF_KR + 2 * QK_ROPE
IN_WIDTH = OFF_GB - QK_ROPE + B_WIDTH
IN_WIDTH2 = OFF_GB + B_WIDTH

PK_Q = 0
PK_A = PK_Q + MLA_HEADS * QK_WIDTH
PK_K = PK_A + A_WIDTH
PK_WIDTH = PK_K + QK_WIDTH

LANES = 128
SUBLANES = 8
ONES_ROWS = 16
SUB_TILE = 256
TOKEN_TILE = 512
PROJ_BLOCK = 256
OUT_ROWS = 256
KEY_CHUNK = 512
SCORE_LOOKAHEAD = 2
VMEM_LIMIT_BYTES = 56 * 1024 * 1024

_F32 = jnp.float32
_BF16 = jnp.bfloat16
_NT_DIMS = (((1,), (1,)), ((), ()))


def _silu(x):
    hx = 0.5 * x
    return hx + hx * jnp.tanh(hx)


def _gelu_tanh(x):
    return x * (0.5 * (1.0 + jnp.tanh(0.7978845608028654 * (x + 0.044715 * (x * x * x)))))


def _rmsnorm(x, g):
    ms = jnp.mean(x * x, axis=-1, keepdims=True)
    return (x * lax.rsqrt(ms + EPS)) * g


def _mod_body(cond_ref, wa_ref, wb_ref, b_ref, o_ref):
    s = _silu(cond_ref[...]).astype(_BF16)
    halves = [jnp.dot(s, w_ref[...].astype(_BF16), preferred_element_type=_F32)
              for w_ref in (wa_ref, wb_ref)]
    mod = jnp.concatenate(halves, axis=1) + b_ref[...]
    rows, cols = mod.shape
    o_ref[...] = jnp.broadcast_to(mod[:, None, :], (rows, SUBLANES, cols)).reshape(
        rows * SUBLANES, cols)


def _fold_body(wq_ref, wkv_ref, q_ref, uv_ref):
    half, quarter = LANES // 2, LANES // 4
    assert QK_NOPE == LANES and QK_ROPE == half and V_HEAD == LANES
    lane = lax.broadcasted_iota(jnp.int32, (Q_RANK, LANES), 1)
    low = lane < half

    def cols(block):
        return wq_ref[:, block * LANES:(block + 1) * LANES]

    for hd in range(MLA_HEADS):
        start = hd * (QK_NOPE + QK_ROPE)
        b0, odd = divmod(start, LANES)
        if odd == 0:
            nope = cols(b0)
            rope_blk = cols(b0 + 1)
            rope_lo = rope_blk
            rope_hi = pltpu.roll(rope_blk, half, 1)
        else:
            assert odd == half
            nope = jnp.where(low, pltpu.roll(cols(b0), half, 1), pltpu.roll(cols(b0 + 1), half, 1))
            rope_blk = cols(b0 + 1)
            rope_lo = pltpu.roll(rope_blk, half, 1)
            rope_hi = rope_blk
        partner = jnp.where(lane < half + quarter,
                            -pltpu.roll(rope_hi, LANES - quarter, 1),
                            pltpu.roll(rope_hi, quarter, 1))
        w_uk = wkv_ref[:, hd * (QK_NOPE + V_HEAD):hd * (QK_NOPE + V_HEAD) + QK_NOPE]
        absorbed = lax.dot_general(nope, w_uk, _NT_DIMS, precision=lax.Precision.HIGHEST,
                                   preferred_element_type=_F32)
        base = hd * QK_WIDTH
        q_ref[:, base:base + KV_RANK] = absorbed.astype(_BF16)
        q_ref[:, base + KV_RANK:base + QK_WIDTH] = jnp.where(low, rope_lo, partner).astype(_BF16)

    uv_ref[...] = jnp.zeros(uv_ref.shape, _BF16)
    for hd in range(MLA_HEADS):
        w_uv = wkv_ref[:, hd * (QK_NOPE + V_HEAD) + QK_NOPE:(hd + 1) * (QK_NOPE + V_HEAD)]
        pair, pos = divmod(hd, 2)
        uv_ref[pair, pos * KV_RANK:(pos + 1) * KV_RANK, pos * V_HEAD:(pos + 1) * V_HEAD] = (
            w_uv.astype(_BF16))


_KR_BLOCK = OFF_KR // LANES


_PREP_SPLIT = IN_WIDTH2 // 2


def _prep_body(step, w_ref, o_ref):
    half = QK_ROPE // 2
    blocks_per_step = _PREP_SPLIT // LANES
    base = step * _PREP_SPLIT
    for j in range(step * blocks_per_step, (step + 1) * blocks_per_step):
        if j < _KR_BLOCK:
            blk = w_ref[j * LANES - base:(j + 1) * LANES - base, :]
        elif j == _KR_BLOCK:
            kr = w_ref[OFF_KR - base:OFF_KR - base + QK_ROPE, :]
            partner = jnp.concatenate([-kr[half:, :], kr[:half, :]], axis=0)
            blk = jnp.concatenate([kr, partner], axis=0)
        else:
            lo = j * LANES - QK_ROPE - base
            blk = w_ref[lo:lo + LANES, :]
        o_ref[:, j * LANES - base:(j + 1) * LANES - base] = blk.T.astype(_BF16)


def _weights_body(cond_ref, wa_ref, wb_ref, b_ref, win_ref, wq_ref, wkv_ref,
                  mod_ref, win2_ref, q_ref, uv_ref):
    j = pl.program_id(0)
    _mod_body(cond_ref, wa_ref, wb_ref, b_ref, mod_ref)
    n_prep = IN_WIDTH2 // _PREP_SPLIT
    for step in range(n_prep):
        pl.when(j == step)(functools.partial(_prep_body, step, win_ref, win2_ref))
    pl.when(j == n_prep)(functools.partial(_fold_body, wq_ref, wkv_ref, q_ref, uv_ref))


def _weights_call(cond, w_ada, b_ada, w_in_t, w_uq, w_ukv):
    rows = cond.shape[0]
    n_out = w_ada.shape[1]
    col_block = D_MODEL
    n_steps = n_out // col_block
    n_prep = IN_WIDTH2 // _PREP_SPLIT
    assert _KR_BLOCK * LANES == OFF_KR and OFF_KR >= _PREP_SPLIT and n_steps == n_prep + 1
    prep_step = lambda j: jnp.minimum(j, n_prep - 1)
    full = lambda shape: pl.BlockSpec(shape, lambda j: (0,) * len(shape))
    uv_shape = (MLA_HEADS // 2, 2 * KV_RANK, 2 * V_HEAD)
    return pl.pallas_call(
        _weights_body,
        grid=(n_steps,),
        in_specs=[
            full((rows, D_MODEL)),
            pl.BlockSpec((D_MODEL, col_block // 2), lambda j: (0, 2 * j)),
            pl.BlockSpec((D_MODEL, col_block // 2), lambda j: (0, 2 * j + 1)),
            pl.BlockSpec((1, col_block), lambda j: (0, j)),
            pl.BlockSpec((_PREP_SPLIT, D_MODEL), lambda j: (prep_step(j), 0)),
            full(w_uq.shape),
            full(w_ukv.shape),
        ],
        out_specs=[
            pl.BlockSpec((rows * SUBLANES, col_block), lambda j: (0, j)),
            pl.BlockSpec((D_MODEL, _PREP_SPLIT), lambda j: (0, prep_step(j))),
            full((Q_RANK, MLA_HEADS * QK_WIDTH)),
            full(uv_shape),
        ],
        out_shape=[
            jax.ShapeDtypeStruct((rows * SUBLANES, n_out), _F32),
            jax.ShapeDtypeStruct((D_MODEL, IN_WIDTH2), _BF16),
            jax.ShapeDtypeStruct((Q_RANK, MLA_HEADS * QK_WIDTH), _BF16),
            jax.ShapeDtypeStruct(uv_shape, _BF16),
        ],
        compiler_params=pltpu.CompilerParams(
            dimension_semantics=("arbitrary",), vmem_limit_bytes=VMEM_LIMIT_BYTES),
        name="weights",
    )(cond, w_ada, w_ada, b_ada.reshape(1, n_out), w_in_t, w_uq, w_ukv)


def _front_substep(x_ref, mod_ref, ng_ref, win_ref, ws_ref, bs_ref, gv_ref, qg_ref, wq_ref,
                   kvg_ref, tables, out_refs, row0, h_write, h_read, z_write, z_read, emit_cache):
    if emit_cache:
        pk_ref, gb_ref, ckv_ref, kr_ref = out_refs
    else:
        pk_ref, gb_ref = out_refs
    first_half = lax.broadcasted_iota(jnp.int32, tables.shape, 1) < QK_ROPE
    t1 = jnp.where(first_half, tables, 0.0)
    t2k = jnp.where(first_half, pltpu.roll(tables, QK_ROPE, 1), 0.0)
    t2q = jnp.where(first_half, t2k, 1.0)
    rows = slice(row0, row0 + SUB_TILE)
    half = SUB_TILE // 2
    assert IN_WIDTH2 == 10 * PROJ_BLOCK

    def project(j):
        lo = j * PROJ_BLOCK
        z_write[:, lo:lo + PROJ_BLOCK] = jnp.dot(h_read[...], win_ref[:, lo:lo + PROJ_BLOCK],
                                                 preferred_element_type=_F32)

    shift = mod_ref[0:1, 0:D_MODEL]
    gain = ng_ref[...] * (1.0 + mod_ref[0:1, D_MODEL:2 * D_MODEL])

    def norm_rows(lo):
        x = x_ref[0, row0 + lo:row0 + lo + half, :]
        ms = jnp.mean(x * x, axis=-1, keepdims=True)
        h_write[lo:lo + half, :] = ((x * lax.rsqrt(ms + EPS)) * gain + shift).astype(_BF16)

    def vn_head(hd):
        lo = hd * A_HEAD_DIM
        v = _gelu_tanh(z_read[:, OFF_V + lo:OFF_V + lo + A_HEAD_DIM])
        return _rmsnorm(v, gv_ref[:, lo:lo + A_HEAD_DIM]).astype(_BF16)

    def mix_head(hd, vn):
        bias = bs_ref[:, hd:hd + 1]
        parts = [jnp.dot(ws_ref[hd], vn[c * CHUNK:(c + 1) * CHUNK, :],
                         preferred_element_type=_F32) + bias for c in range(SUB_TILE // CHUNK)]
        return jnp.concatenate(parts, axis=0)

    def gate_a(hd, mixed):
        lo = hd * A_HEAD_DIM
        u = _gelu_tanh(z_read[:, OFF_U + lo:OFF_U + lo + A_HEAD_DIM])
        g = _silu(z_read[:, OFF_GA + lo:OFF_GA + lo + A_HEAD_DIM])
        pk_ref[0, rows, PK_A + lo:PK_A + lo + A_HEAD_DIM] = (u * mixed * g).astype(_BF16)

    def gate_b(lo, width):
        gb_ref[0, rows, lo:lo + width] = _silu(z_read[:, OFF_GB + lo:OFF_GB + lo + width])

    def rope_q(qa, hd):
        base = hd * QK_WIDTH
        pair = qa[:, base + KV_RANK:base + QK_WIDTH]
        rope = pair * t1 + pltpu.roll(pair, QK_ROPE, 1) * t2q
        pk_ref[0, rows, PK_Q + base:PK_Q + base + QK_WIDTH] = jnp.concatenate(
            [qa[:, base:base + KV_RANK], rope], axis=1).astype(_BF16)

    project(0)
    cq = _rmsnorm(z_read[:, OFF_CQ:OFF_CQ + Q_RANK],
                  qg_ref[...] * (ATTN_SCALE * LOG2_E)).astype(_BF16)
    vn0 = vn_head(0)
    project(1)
    vn1 = vn_head(1)
    gate_b(0, B_WIDTH // 2)
    project(2)
    vn2 = vn_head(2)
    gate_b(B_WIDTH // 2, B_WIDTH // 2)
    project(3)
    vn3 = vn_head(3)
    ckv = _rmsnorm(z_read[:, OFF_CKV:OFF_CKV + KV_RANK], kvg_ref[...])
    kpair = z_read[:, OFF_KR:OFF_KR + 2 * QK_ROPE]
    krot = kpair * t1 + pltpu.roll(kpair, QK_ROPE, 1) * t2k
    pk_ref[0, rows, PK_K:PK_K + QK_WIDTH] = jnp.concatenate([ckv, krot], axis=1).astype(_BF16)
    if emit_cache:
        ckv_ref[0, 0, rows, :] = ckv
        kr_ref[row0 // SUB_TILE, 0] = kpair.T[0:QK_ROPE, :]
    qa = jnp.dot(cq, wq_ref[...], preferred_element_type=_F32)
    norm_rows(0)
    project(4)
    rope_q(qa, 0)
    rope_q(qa, 1)
    mixed0 = mix_head(0, vn0)
    mixed1 = mix_head(1, vn1)
    rope_q(qa, 2)
    rope_q(qa, 3)
    project(5)
    gate_a(0, mixed0)
    mixed2 = mix_head(2, vn2)
    mixed3 = mix_head(3, vn3)
    project(6)
    gate_a(1, mixed1)
    norm_rows(half)
    project(7)
    gate_a(2, mixed2)
    project(8)
    gate_a(3, mixed3)
    project(9)


def _front_body(x_ref, mod_ref, ng_ref, win_ref, ws_ref, bs_ref, gv_ref, qg_ref, wq_ref,
                kvg_ref, tab_ref, *rest, emit_cache, tiles_per_row):
    out_refs, (h_a, h_b, z_a, z_b) = rest[:-4], rest[-4:]
    t = pl.program_id(0)
    tile = x_ref.shape[1]
    assert tile == 2 * SUB_TILE
    n_pos = tab_ref.shape[0]

    @pl.when(t == 0)
    def _():
        h_b[...] = jnp.zeros(h_b.shape, _BF16)
        z_b[...] = jnp.zeros(z_b.shape, _F32)

    pos_base = (jnp.maximum(t - 1, 0) % tiles_per_row) * tile
    for sub, (h_write, h_read, z_write, z_read) in enumerate(
            ((h_a, h_b, z_a, z_b), (h_b, h_a, z_b, z_a))):
        row0 = sub * SUB_TILE
        pos = pl.multiple_of((pos_base + row0) % n_pos, SUB_TILE)
        tables = tab_ref[pl.ds(pos, SUB_TILE), :]
        _front_substep(x_ref, mod_ref, ng_ref, win_ref, ws_ref, bs_ref, gv_ref, qg_ref, wq_ref,
                       kvg_ref, tables, out_refs, row0, h_write, h_read, z_write, z_read,
                       emit_cache)


def _front_call(x, mod, mod_index, norm_g, win2, ws_bf, bs_t, g_v, q_norm_g, wq2, kv_norm_g,
                tables, emit_cache):
    batch, seq, _ = x.shape
    tile = 2 * SUB_TILE
    tiles_per_row = seq // tile
    n_tiles = batch * tiles_per_row
    assert seq % tile == 0 and tables.shape[0] % SUB_TILE == 0

    def in_tile(t):
        tt = jnp.minimum(t, n_tiles - 1)
        return tt // tiles_per_row, tt % tiles_per_row

    def out_tile(t):
        tt = jnp.maximum(t - 1, 0)
        return tt // tiles_per_row, tt % tiles_per_row

    full = lambda shape: pl.BlockSpec(shape, lambda t: (0,) * len(shape))
    in_specs = [
        pl.BlockSpec((1, tile, D_MODEL), lambda t: (*in_tile(t), 0)),
        pl.BlockSpec((SUBLANES, 3 * D_MODEL), lambda t: (mod_index(in_tile(t)[0]), 0)),
        full((1, D_MODEL)),
        full((D_MODEL, IN_WIDTH2)),
        full((A_HEADS, CHUNK, CHUNK)),
        full((CHUNK, A_HEADS)),
        full((1, A_WIDTH)),
        full((1, Q_RANK)),
        full((Q_RANK, MLA_HEADS * QK_WIDTH)),
        full((1, KV_RANK)),
        full(tables.shape),
    ]
    out_shape = [
        jax.ShapeDtypeStruct((batch, seq, PK_WIDTH), _BF16),
        jax.ShapeDtypeStruct((batch, seq, B_WIDTH), _F32),
    ]
    out_specs = [
        pl.BlockSpec((1, tile, PK_WIDTH), lambda t: (*out_tile(t), 0)),
        pl.BlockSpec((1, tile, B_WIDTH), lambda t: (*out_tile(t), 0)),
    ]
    if emit_cache:
        out_shape += [
            jax.ShapeDtypeStruct((batch, 1, seq, KV_RANK), _F32),
            jax.ShapeDtypeStruct((n_tiles * 2, 1, QK_ROPE, SUB_TILE), _F32),
        ]
        out_specs += [
            pl.BlockSpec((1, 1, tile, KV_RANK), lambda t: (out_tile(t)[0], 0, out_tile(t)[1], 0)),
            pl.BlockSpec((2, 1, QK_ROPE, SUB_TILE), lambda t: (jnp.maximum(t - 1, 0), 0, 0, 0)),
        ]
    return pl.pallas_call(
        functools.partial(_front_body, emit_cache=emit_cache, tiles_per_row=tiles_per_row),
        grid=(n_tiles + 1,),
        in_specs=in_specs,
        out_specs=out_specs,
        out_shape=out_shape,
        scratch_shapes=[pltpu.VMEM((SUB_TILE, D_MODEL), _BF16), pltpu.VMEM((SUB_TILE, D_MODEL), _BF16),
                        pltpu.VMEM((SUB_TILE, IN_WIDTH2), _F32),
                        pltpu.VMEM((SUB_TILE, IN_WIDTH2), _F32)],
        compiler_params=pltpu.CompilerParams(
            dimension_semantics=("arbitrary",),
            vmem_limit_bytes=VMEM_LIMIT_BYTES),
        name="front_ctx" if emit_cache else "front_lat",
    )(x, mod, norm_g, win2, ws_bf, bs_t, g_v, q_norm_g, wq2, kv_norm_g, tables)


def _key_chunks(k_refs):
    chunks = []
    for k_ref in k_refs:
        size = min(KEY_CHUNK, k_ref.shape[1])
        assert k_ref.shape[1] % size == 0
        chunks += [(k_ref, lo, size) for lo in range(0, k_ref.shape[1], size)]
    return chunks


def _values_t(k_ref, row):
    ckv_t = k_ref[row, :, 0:KV_RANK].T
    return jnp.concatenate([ckv_t, jnp.ones((ONES_ROWS, ckv_t.shape[1]), _BF16)], axis=0)


def _attention(q_ref, k_refs, qt_ref, fillers):
    n_rows = q_ref.shape[0]
    chunks = _key_chunks(k_refs)
    steps = [(row, hd, c) for row in range(n_rows) for pair in range(MLA_HEADS // 2)
             for c in range(len(chunks)) for hd in (2 * pair, 2 * pair + 1)]

    for row in range(n_rows):
        for hd in range(MLA_HEADS):
            qt_ref[row * MLA_HEADS + hd] = q_ref[row, :, hd * QK_WIDTH:(hd + 1) * QK_WIDTH].T

    def logits(row, hd, c):
        k_ref, lo, size = chunks[c]
        return jnp.dot(k_ref[row, lo:lo + size, :], qt_ref[row * MLA_HEADS + hd],
                       preferred_element_type=_F32)

    pending = [logits(*step) for step in steps[:SCORE_LOOKAHEAD]]
    values_t = [{id(k_ref): _values_t(k_ref, row) for k_ref in k_refs} for row in range(n_rows)]
    outs = {}
    state = {}
    for j, (row, hd, c) in enumerate(steps):
        m, acc = state.get((row, hd), (None, None))
        if j + SCORE_LOOKAHEAD < len(steps):
            pending.append(logits(*steps[j + SCORE_LOOKAHEAD]))
        s, pending[j] = pending[j], None
        k_ref, lo, size = chunks[c]
        chunk_max = jnp.max(s, axis=0, keepdims=True)
        if c > 0:
            m_new = jnp.maximum(m, chunk_max)
            acc = acc * jnp.exp2(m - m_new)
            m = m_new
        else:
            m = chunk_max
        p = jnp.exp2(s - m).astype(_BF16)
        part = jnp.dot(values_t[row][id(k_ref)][:, lo:lo + size], p, preferred_element_type=_F32)
        acc = acc + part if c > 0 else part
        state[row, hd] = (m, acc)
        if j >= len(steps) - len(fillers):
            fillers[j - len(steps) + len(fillers)]()
        if c == len(chunks) - 1:
            outs[row, hd] = (acc[0:KV_RANK, :] / acc[KV_RANK:KV_RANK + 1, :]).T
    return outs


def _attention_single(q_ref, k_ref, fillers):
    n_rows, tq = q_ref.shape[0], q_ref.shape[1]
    groups = [(row, pair) for row in range(n_rows) for pair in range(MLA_HEADS // 2)]
    scores = {}
    for row, pair in groups:
        q = jnp.concatenate([q_ref[row, :, hd * QK_WIDTH:(hd + 1) * QK_WIDTH]
                             for hd in (2 * pair, 2 * pair + 1)], axis=0)
        scores[row, pair] = lax.dot_general(q, k_ref[row], _NT_DIMS, preferred_element_type=_F32)
    for fill in fillers:
        fill()
    outs = {}
    for row, pair in groups:
        s = scores[row, pair]
        ckv = k_ref[row, :, 0:KV_RANK]
        values = jnp.concatenate([ckv, jnp.ones_like(ckv)], axis=-1)
        p = jnp.exp2(s - jnp.max(s, axis=1, keepdims=True)).astype(_BF16)
        acc = jnp.dot(p, values, preferred_element_type=_F32)
        o = acc[:, 0:KV_RANK] / acc[:, KV_RANK:2 * KV_RANK]
        outs[row, 2 * pair], outs[row, 2 * pair + 1] = o[0:tq, :], o[tq:2 * tq, :]
    return outs


def _back_body(*refs, has_ctx):
    if has_ctx:
        (q_ref, klat_ref, kctx_ref, a_ref, gb_ref, x_ref, mod_ref, wuv_ref, wo_ref, fg_ref,
         y_ref, qt_ref, ya_ref) = refs
        k_refs = (kctx_ref, klat_ref)
    else:
        (q_ref, klat_ref, a_ref, gb_ref, x_ref, mod_ref, wuv_ref, wo_ref, fg_ref, y_ref,
         qt_ref, ya_ref) = refs
        k_refs = (klat_ref,)
    n_rows, tq = q_ref.shape[0], q_ref.shape[1]

    block = min(OUT_ROWS, tq)
    blocks = [(row, lo) for row in range(n_rows) for lo in range(0, tq, block)]

    def a_part(row, lo):
        def run():
            ya_ref[row, lo:lo + block, :] = jnp.dot(
                a_ref[row, lo:lo + block, :], wo_ref[0:A_WIDTH, :], preferred_element_type=_F32)
        return run

    fillers = [a_part(*b) for b in blocks]
    if len(_key_chunks(k_refs)) == 1:
        outs = _attention_single(q_ref, k_refs[0], fillers)
    else:
        outs = _attention(q_ref, k_refs, qt_ref, fillers)

    gate = mod_ref[0:1, 2 * D_MODEL:3 * D_MODEL]
    for row, lo in blocks:
        rows = slice(lo, lo + block)
        attn_cols = []
        for pair in range(MLA_HEADS // 2):
            o2 = jnp.concatenate([outs[row, 2 * pair][rows, :],
                                  outs[row, 2 * pair + 1][rows, :]], axis=1)
            attn_cols.append(jnp.dot(o2.astype(_BF16), wuv_ref[pair],
                                     preferred_element_type=_F32))
        attn = jnp.concatenate(attn_cols, axis=1) * gb_ref[row, rows, :]
        y = ya_ref[row, rows, :] + jnp.dot(
            attn.astype(_BF16), wo_ref[A_WIDTH:A_WIDTH + B_WIDTH, :], preferred_element_type=_F32)
        out = x_ref[row, rows, :] + gate * y
        y_ref[row, rows, :] = _rmsnorm(out, fg_ref[...])


def _back_call(packed, kctx, gb, x, mod, mod_index, wuv2, wo_bf, final_g, rows_per_step):
    batch, seq, _ = x.shape
    tq = min(TOKEN_TILE, seq)
    has_ctx = kctx is not None
    rows = rows_per_step
    assert batch % rows == 0 and (rows == 1 or seq == tq)
    full = lambda shape: pl.BlockSpec(shape, lambda b, i: (0,) * len(shape))
    keys = lambda arr: pl.BlockSpec((rows, arr.shape[1], arr.shape[2]), lambda b, i: (b, 0, 0))
    in_specs = [
        pl.BlockSpec((rows, tq, MLA_HEADS * QK_WIDTH),
                     lambda b, i: (b, i, PK_Q // (MLA_HEADS * QK_WIDTH))),
        pl.BlockSpec((rows, seq, QK_WIDTH), lambda b, i: (b, 0, PK_K // QK_WIDTH)),
    ]
    args = [packed, packed]
    if has_ctx:
        in_specs += [keys(kctx)]
        args += [kctx]
    in_specs += [
        pl.BlockSpec((rows, tq, A_WIDTH), lambda b, i: (b, i, PK_A // A_WIDTH)),
        pl.BlockSpec((rows, tq, B_WIDTH), lambda b, i: (b, i, 0)),
        pl.BlockSpec((rows, tq, D_MODEL), lambda b, i: (b, i, 0)),
        pl.BlockSpec((SUBLANES, 3 * D_MODEL), lambda b, i: (mod_index(b * rows), 0)),
        full((MLA_HEADS // 2, 2 * KV_RANK, 2 * V_HEAD)),
        full((D_MODEL, D_MODEL)),
        full((1, D_MODEL)),
    ]
    args += [packed, gb, x, mod, wuv2, wo_bf, final_g]
    return pl.pallas_call(
        functools.partial(_back_body, has_ctx=has_ctx),
        grid=(batch // rows, seq // tq),
        in_specs=in_specs,
        out_specs=pl.BlockSpec((rows, tq, D_MODEL), lambda b, i: (b, i, 0)),
        out_shape=jax.ShapeDtypeStruct((batch, seq, D_MODEL), _F32),
        scratch_shapes=[
            pltpu.VMEM((rows * MLA_HEADS, QK_WIDTH, tq), _BF16),
            pltpu.VMEM((rows, tq, D_MODEL), _F32),
        ],
        compiler_params=pltpu.CompilerParams(
            dimension_semantics=("arbitrary", "arbitrary"),
            vmem_limit_bytes=VMEM_LIMIT_BYTES),
        name="back_lat" if has_ctx else "back_ctx",
    )(*args)


def _rope_tables(n_tokens, rotate):
    if rotate:
        rows = n_tokens // GRID_W
        inv = ROPE_THETA ** (-jnp.arange(AXIS_PAIRS, dtype=_F32) / AXIS_PAIRS)
        ang_r = jnp.arange(rows, dtype=_F32)[:, None] * inv
        ang_c = jnp.arange(GRID_W, dtype=_F32)[:, None] * inv

        def grid_table(fn):
            tr = jnp.broadcast_to(fn(ang_r)[:, None, :], (rows, GRID_W, AXIS_PAIRS))
            tc = jnp.broadcast_to(fn(ang_c)[None, :, :], (rows, GRID_W, AXIS_PAIRS))
            half = jnp.concatenate([tr, tc], axis=-1).reshape(n_tokens, 2 * AXIS_PAIRS)
            return jnp.concatenate([half, half], axis=-1)

        cos, sin = grid_table(jnp.cos), grid_table(jnp.sin)
    else:
        cos, sin = jnp.ones((n_tokens, QK_ROPE), _F32), jnp.zeros((n_tokens, QK_ROPE), _F32)
    return jnp.concatenate([cos, sin], axis=-1)


def kernel(x_prompt, x_sample, cache_ckv, cache_krope, c, c_ctx, norm_g, w_ada, b_ada, w_in, w_s,
           b_s, g_v, q_norm_g, w_uq, kv_norm_g, w_ukv, w_o, final_g):
    depth = norm_g.shape[0]
    assert depth == 1 and w_in.shape[2] == IN_WIDTH
    dec_batch = x_sample.shape[0]
    xp, xs = x_prompt, x_sample
    new_ckv, new_kr = [], []
    for l in range(depth):
        cond = jnp.concatenate(
            [c, c_ctx[None, :], jnp.zeros((16 - dec_batch - 1, D_MODEL), _F32)], axis=0)
        mod, win2, wq2, wuv2 = _weights_call(cond, w_ada[l], b_ada[l], jnp.transpose(w_in[l]),
                                             w_uq[l], w_ukv[l])
        wo_bf = w_o[l].astype(_BF16)
        ws_bf = w_s[l].astype(_BF16)
        bs_t = b_s[l].T
        gv_row = g_v[l].reshape(1, A_WIDTH)
        shared = (norm_g[l].reshape(1, D_MODEL), win2, ws_bf, bs_t, gv_row,
                  q_norm_g[l].reshape(1, Q_RANK), wq2, kv_norm_g[l].reshape(1, KV_RANK))
        fg = final_g.reshape(1, D_MODEL)

        ctx_index = lambda b: dec_batch
        ctx_batch, ctx_seq, _ = xp.shape
        pack = 2 * SUB_TILE // ctx_seq
        pk_c, gb_c, ckv_c, kr_c = _front_call(
            xp.reshape(ctx_batch // pack, pack * ctx_seq, D_MODEL), mod, ctx_index, *shared,
            _rope_tables(ctx_seq, False), True)
        unpack = lambda arr: arr.reshape(ctx_batch, ctx_seq, arr.shape[-1])
        xp = _back_call(unpack(pk_c), None, unpack(gb_c), xp, mod, ctx_index, wuv2, wo_bf, fg,
                        rows_per_step=pack)
        new_ckv.append(ckv_c.reshape(ctx_batch, 1, ctx_seq, KV_RANK))
        assert ctx_seq == SUB_TILE
        new_kr.append(jnp.swapaxes(kr_c, 2, 3))

        lat_index = lambda b: b
        pk_s, gb_s = _front_call(
            xs, mod, lat_index, *shared, _rope_tables(xs.shape[1], True), False)
        cache_k = jnp.concatenate(
            [cache_ckv[:, l], jnp.zeros(cache_krope[:, l].shape, _F32), cache_krope[:, l]],
            axis=-1).astype(_BF16)
        xs = _back_call(pk_s, cache_k, gb_s, xs, mod, lat_index, wuv2, wo_bf, fg, rows_per_step=1)
    return (xp, xs, jnp.concatenate(new_ckv, axis=1), jnp.concatenate(new_kr, axis=1))
```

```python
import functools
import math

import jax
import jax.numpy as jnp
from jax import lax
from jax.experimental import pallas as pl
from jax.experimental.pallas import tpu as pltpu

D_MODEL = 1024
GRID_W = 64
EPS = 1e-6
A_HEADS = 4
A_HEAD_DIM = 128
A_WIDTH = A_HEADS * A_HEAD_DIM
CHUNK = 128
MLA_HEADS = 4
QK_NOPE = 128
QK_ROPE = 64
V_HEAD = 128
B_WIDTH = MLA_HEADS * V_HEAD
Q_RANK = 256
KV_RANK = 128
AXIS_PAIRS = QK_ROPE // 4
ROPE_THETA = 10000.0
ATTN_SCALE = 1.0 / math.sqrt(QK_NOPE + QK_ROPE)
LOG2_E = 1.4426950408889634

QK_WIDTH = KV_RANK + 2 * QK_ROPE
OFF_U = 0
OFF_V = OFF_U + A_WIDTH
OFF_GA = OFF_V + A_WIDTH
OFF_CQ = OFF_GA + A_WIDTH
OFF_CKV = OFF_CQ + Q_RANK
OFF_KR = OFF_CKV + KV_RANK
OFF_GB = OFF_KR + 2 * QK_ROPE
IN_WIDTH = OFF_GB - QK_ROPE + B_WIDTH
IN_WIDTH2 = OFF_GB + B_WIDTH

PK_Q = 0
PK_A = PK_Q + MLA_HEADS * QK_WIDTH
PK_K = PK_A + A_WIDTH
PK_WIDTH = PK_K + QK_WIDTH

LANES = 128
SUBLANES = 8
ONES_ROWS = 16
SUB_TILE = 256
TOKEN_TILE = 512
PROJ_BLOCK = 256
OUT_ROWS = 256
KEY_CHUNK = 256
SCORE_LOOKAHEAD = 4
VMEM_LIMIT_BYTES = 56 * 1024 * 1024

_F32 = jnp.float32
_BF16 = jnp.bfloat16
_NT_DIMS = (((1,), (1,)), ((), ()))


def _silu(x):
    hx = 0.5 * x
    return hx + hx * jnp.tanh(hx)


def _gelu_tanh(x):
    return x * (0.5 * (1.0 + jnp.tanh(0.7978845608028654 * (x + 0.044715 * (x * x * x)))))


def _rmsnorm(x, g):
    ms = jnp.mean(x * x, axis=-1, keepdims=True)
    return (x * lax.rsqrt(ms + EPS)) * g


def _mod_body(cond_ref, wa_ref, wb_ref, b_ref, o_ref):
    s = _silu(cond_ref[...]).astype(_BF16)
    halves = [jnp.dot(s, w_ref[...].astype(_BF16), preferred_element_type=_F32)
              for w_ref in (wa_ref, wb_ref)]
    mod = jnp.concatenate(halves, axis=1) + b_ref[...]
    rows, cols = mod.shape
    o_ref[...] = jnp.broadcast_to(mod[:, None, :], (rows, SUBLANES, cols)).reshape(
        rows * SUBLANES, cols)


def _fold_body(wq_ref, wkv_ref, q_ref, uv_ref):
    half, quarter = LANES // 2, LANES // 4
    assert QK_NOPE == LANES and QK_ROPE == half and V_HEAD == LANES
    lane = lax.broadcasted_iota(jnp.int32, (Q_RANK, LANES), 1)
    low = lane < half

    def cols(block):
        return wq_ref[:, block * LANES:(block + 1) * LANES]

    for hd in range(MLA_HEADS):
        start = hd * (QK_NOPE + QK_ROPE)
        b0, odd = divmod(start, LANES)
        if odd == 0:
            nope = cols(b0)
            rope_blk = cols(b0 + 1)
            rope_lo = rope_blk
            rope_hi = pltpu.roll(rope_blk, half, 1)
        else:
            assert odd == half
            nope = jnp.where(low, pltpu.roll(cols(b0), half, 1), pltpu.roll(cols(b0 + 1), half, 1))
            rope_blk = cols(b0 + 1)
            rope_lo = pltpu.roll(rope_blk, half, 1)
            rope_hi = rope_blk
        partner = jnp.where(lane < half + quarter,
                            -pltpu.roll(rope_hi, LANES - quarter, 1),
                            pltpu.roll(rope_hi, quarter, 1))
        w_uk = wkv_ref[:, hd * (QK_NOPE + V_HEAD):hd * (QK_NOPE + V_HEAD) + QK_NOPE]
        absorbed = lax.dot_general(nope, w_uk, _NT_DIMS, precision=lax.Precision.HIGHEST,
                                   preferred_element_type=_F32)
        base = hd * QK_WIDTH
        q_ref[:, base:base + KV_RANK] = absorbed.astype(_BF16)
        q_ref[:, base + KV_RANK:base + QK_WIDTH] = jnp.where(low, rope_lo, partner).astype(_BF16)

    uv_ref[...] = jnp.zeros(uv_ref.shape, _BF16)
    for hd in range(MLA_HEADS):
        w_uv = wkv_ref[:, hd * (QK_NOPE + V_HEAD) + QK_NOPE:(hd + 1) * (QK_NOPE + V_HEAD)]
        pair, pos = divmod(hd, 2)
        uv_ref[pair, pos * KV_RANK:(pos + 1) * KV_RANK, pos * V_HEAD:(pos + 1) * V_HEAD] = (
            w_uv.astype(_BF16))


_KR_BLOCK = OFF_KR // LANES


_PREP_SPLIT = IN_WIDTH2 // 2


def _prep_body(step, w_ref, o_ref):
    half = QK_ROPE // 2
    blocks_per_step = _PREP_SPLIT // LANES
    base = step * _PREP_SPLIT
    for j in range(step * blocks_per_step, (step + 1) * blocks_per_step):
        if j < _KR_BLOCK:
            blk = w_ref[j * LANES - base:(j + 1) * LANES - base, :]
        elif j == _KR_BLOCK:
            kr = w_ref[OFF_KR - base:OFF_KR - base + QK_ROPE, :]
            partner = jnp.concatenate([-kr[half:, :], kr[:half, :]], axis=0)
            blk = jnp.concatenate([kr, partner], axis=0)
        else:
            lo = j * LANES - QK_ROPE - base
            blk = w_ref[lo:lo + LANES, :]
        o_ref[:, j * LANES - base:(j + 1) * LANES - base] = blk.T.astype(_BF16)


def _weights_body(cond_ref, wa_ref, wb_ref, b_ref, win_ref, wq_ref, wkv_ref,
                  mod_ref, win2_ref, q_ref, uv_ref):
    j = pl.program_id(0)
    _mod_body(cond_ref, wa_ref, wb_ref, b_ref, mod_ref)
    n_prep = IN_WIDTH2 // _PREP_SPLIT
    for step in range(n_prep):
        pl.when(j == step)(functools.partial(_prep_body, step, win_ref, win2_ref))
    pl.when(j == n_prep)(functools.partial(_fold_body, wq_ref, wkv_ref, q_ref, uv_ref))


def _weights_call(cond, w_ada, b_ada, w_in_t, w_uq, w_ukv):
    rows = cond.shape[0]
    n_out = w_ada.shape[1]
    col_block = D_MODEL
    n_steps = n_out // col_block
    n_prep = IN_WIDTH2 // _PREP_SPLIT
    assert _KR_BLOCK * LANES == OFF_KR and OFF_KR >= _PREP_SPLIT and n_steps == n_prep + 1
    prep_step = lambda j: jnp.minimum(j, n_prep - 1)
    full = lambda shape: pl.BlockSpec(shape, lambda j: (0,) * len(shape))
    uv_shape = (MLA_HEADS // 2, 2 * KV_RANK, 2 * V_HEAD)
    return pl.pallas_call(
        _weights_body,
        grid=(n_steps,),
        in_specs=[
            full((rows, D_MODEL)),
            pl.BlockSpec((D_MODEL, col_block // 2), lambda j: (0, 2 * j)),
            pl.BlockSpec((D_MODEL, col_block // 2), lambda j: (0, 2 * j + 1)),
            pl.BlockSpec((1, col_block), lambda j: (0, j)),
            pl.BlockSpec((_PREP_SPLIT, D_MODEL), lambda j: (prep_step(j), 0)),
            full(w_uq.shape),
            full(w_ukv.shape),
        ],
        out_specs=[
            pl.BlockSpec((rows * SUBLANES, col_block), lambda j: (0, j)),
            pl.BlockSpec((D_MODEL, _PREP_SPLIT), lambda j: (0, prep_step(j))),
            full((Q_RANK, MLA_HEADS * QK_WIDTH)),
            full(uv_shape),
        ],
        out_shape=[
            jax.ShapeDtypeStruct((rows * SUBLANES, n_out), _F32),
            jax.ShapeDtypeStruct((D_MODEL, IN_WIDTH2), _BF16),
            jax.ShapeDtypeStruct((Q_RANK, MLA_HEADS * QK_WIDTH), _BF16),
            jax.ShapeDtypeStruct(uv_shape, _BF16),
        ],
        compiler_params=pltpu.CompilerParams(
            dimension_semantics=("arbitrary",), vmem_limit_bytes=VMEM_LIMIT_BYTES),
        name="weights",
    )(cond, w_ada, w_ada, b_ada.reshape(1, n_out), w_in_t, w_uq, w_ukv)


def _front_substep(x_ref, mod_ref, ng_ref, win_ref, ws_ref, bs_ref, gv_ref, qg_ref, wq_ref,
                   kvg_ref, tables, out_refs, row0, h_write, h_read, z_write, z_read, emit_cache):
    if emit_cache:
        pk_ref, gb_ref, ckv_ref, kr_ref = out_refs
    else:
        pk_ref, gb_ref = out_refs
    first_half = lax.broadcasted_iota(jnp.int32, tables.shape, 1) < QK_ROPE
    t1 = jnp.where(first_half, tables, 0.0)
    t2k = jnp.where(first_half, pltpu.roll(tables, QK_ROPE, 1), 0.0)
    t2q = jnp.where(first_half, t2k, 1.0)
    rows = slice(row0, row0 + SUB_TILE)
    half = SUB_TILE // 2
    assert IN_WIDTH2 == 10 * PROJ_BLOCK

    def project(j):
        lo = j * PROJ_BLOCK
        z_write[:, lo:lo + PROJ_BLOCK] = jnp.dot(h_read[...], win_ref[:, lo:lo + PROJ_BLOCK],
                                                 preferred_element_type=_F32)

    shift = mod_ref[0:1, 0:D_MODEL]
    gain = ng_ref[...] * (1.0 + mod_ref[0:1, D_MODEL:2 * D_MODEL])

    def norm_rows(lo):
        x = x_ref[0, row0 + lo:row0 + lo + half, :]
        ms = jnp.mean(x * x, axis=-1, keepdims=True)
        h_write[lo:lo + half, :] = ((x * lax.rsqrt(ms + EPS)) * gain + shift).astype(_BF16)

    def vn_head(hd):
        lo = hd * A_HEAD_DIM
        v = _gelu_tanh(z_read[:, OFF_V + lo:OFF_V + lo + A_HEAD_DIM])
        return _rmsnorm(v, gv_ref[:, lo:lo + A_HEAD_DIM]).astype(_BF16)

    def mix_head(hd, vn):
        bias = bs_ref[:, hd:hd + 1]
        parts = [jnp.dot(ws_ref[hd], vn[c * CHUNK:(c + 1) * CHUNK, :],
                         preferred_element_type=_F32) + bias for c in range(SUB_TILE // CHUNK)]
        return jnp.concatenate(parts, axis=0)

    def gate_a(hd, mixed):
        lo = hd * A_HEAD_DIM
        u = _gelu_tanh(z_read[:, OFF_U + lo:OFF_U + lo + A_HEAD_DIM])
        g = _silu(z_read[:, OFF_GA + lo:OFF_GA + lo + A_HEAD_DIM])
        pk_ref[0, rows, PK_A + lo:PK_A + lo + A_HEAD_DIM] = (u * mixed * g).astype(_BF16)

    def gate_b(lo, width):
        gb_ref[0, rows, lo:lo + width] = _silu(z_read[:, OFF_GB + lo:OFF_GB + lo + width])

    def rope_q(qa, hd):
        base = hd * QK_WIDTH
        pair = qa[:, base + KV_RANK:base + QK_WIDTH]
        rope = pair * t1 + pltpu.roll(pair, QK_ROPE, 1) * t2q
        pk_ref[0, rows, PK_Q + base:PK_Q + base + QK_WIDTH] = jnp.concatenate(
            [qa[:, base:base + KV_RANK], rope], axis=1).astype(_BF16)

    project(0)
    cq = _rmsnorm(z_read[:, OFF_CQ:OFF_CQ + Q_RANK],
                  qg_ref[...] * (ATTN_SCALE * LOG2_E)).astype(_BF16)
    vn0 = vn_head(0)
    project(1)
    vn1 = vn_head(1)
    gate_b(0, B_WIDTH // 2)
    project(2)
    vn2 = vn_head(2)
    gate_b(B_WIDTH // 2, B_WIDTH // 2)
    project(3)
    vn3 = vn_head(3)
    ckv = _rmsnorm(z_read[:, OFF_CKV:OFF_CKV + KV_RANK], kvg_ref[...])
    kpair = z_read[:, OFF_KR:OFF_KR + 2 * QK_ROPE]
    krot = kpair * t1 + pltpu.roll(kpair, QK_ROPE, 1) * t2k
    pk_ref[0, rows, PK_K:PK_K + QK_WIDTH] = jnp.concatenate([ckv, krot], axis=1).astype(_BF16)
    if emit_cache:
        ckv_ref[0, 0, rows, :] = ckv
        kr_ref[row0 // SUB_TILE, 0] = kpair.T[0:QK_ROPE, :]
    qa = jnp.dot(cq, wq_ref[...], preferred_element_type=_F32)
    norm_rows(0)
    project(4)
    rope_q(qa, 0)
    rope_q(qa, 1)
    mixed0 = mix_head(0, vn0)
    mixed1 = mix_head(1, vn1)
    rope_q(qa, 2)
    rope_q(qa, 3)
    project(5)
    gate_a(0, mixed0)
    mixed2 = mix_head(2, vn2)
    mixed3 = mix_head(3, vn3)
    project(6)
    gate_a(1, mixed1)
    norm_rows(half)
    project(7)
    gate_a(2, mixed2)
    project(8)
    gate_a(3, mixed3)
    project(9)


def _front_body(x_ref, mod_ref, ng_ref, win_ref, ws_ref, bs_ref, gv_ref, qg_ref, wq_ref,
                kvg_ref, tab_ref, *rest, emit_cache, tiles_per_row):
    out_refs, (h_a, h_b, z_a, z_b) = rest[:-4], rest[-4:]
    t = pl.program_id(0)
    tile = x_ref.shape[1]
    assert tile == 2 * SUB_TILE
    n_pos = tab_ref.shape[0]

    @pl.when(t == 0)
    def _():
        h_b[...] = jnp.zeros(h_b.shape, _BF16)
        z_b[...] = jnp.zeros(z_b.shape, _F32)

    pos_base = (jnp.maximum(t - 1, 0) % tiles_per_row) * tile
    for sub, (h_write, h_read, z_write, z_read) in enumerate(
            ((h_a, h_b, z_a, z_b), (h_b, h_a, z_b, z_a))):
        row0 = sub * SUB_TILE
        pos = pl.multiple_of((pos_base + row0) % n_pos, SUB_TILE)
        tables = tab_ref[pl.ds(pos, SUB_TILE), :]
        _front_substep(x_ref, mod_ref, ng_ref, win_ref, ws_ref, bs_ref, gv_ref, qg_ref, wq_ref,
                       kvg_ref, tables, out_refs, row0, h_write, h_read, z_write, z_read,
                       emit_cache)


def _front_call(x, mod, mod_index, norm_g, win2, ws_bf, bs_t, g_v, q_norm_g, wq2, kv_norm_g,
                tables, emit_cache):
    batch, seq, _ = x.shape
    tile = 2 * SUB_TILE
    tiles_per_row = seq // tile
    n_tiles = batch * tiles_per_row
    assert seq % tile == 0 and tables.shape[0] % SUB_TILE == 0

    def in_tile(t):
        tt = jnp.minimum(t, n_tiles - 1)
        return tt // tiles_per_row, tt % tiles_per_row

    def out_tile(t):
        tt = jnp.maximum(t - 1, 0)
        return tt // tiles_per_row, tt % tiles_per_row

    full = lambda shape: pl.BlockSpec(shape, lambda t: (0,) * len(shape))
    in_specs = [
        pl.BlockSpec((1, tile, D_MODEL), lambda t: (*in_tile(t), 0)),
        pl.BlockSpec((SUBLANES, 3 * D_MODEL), lambda t: (mod_index(in_tile(t)[0]), 0)),
        full((1, D_MODEL)),
        full((D_MODEL, IN_WIDTH2)),
        full((A_HEADS, CHUNK, CHUNK)),
        full((CHUNK, A_HEADS)),
        full((1, A_WIDTH)),
        full((1, Q_RANK)),
        full((Q_RANK, MLA_HEADS * QK_WIDTH)),
        full((1, KV_RANK)),
        full(tables.shape),
    ]
    out_shape = [
        jax.ShapeDtypeStruct((batch, seq, PK_WIDTH), _BF16),
        jax.ShapeDtypeStruct((batch, seq, B_WIDTH), _F32),
    ]
    out_specs = [
        pl.BlockSpec((1, tile, PK_WIDTH), lambda t: (*out_tile(t), 0)),
        pl.BlockSpec((1, tile, B_WIDTH), lambda t: (*out_tile(t), 0)),
    ]
    if emit_cache:
        out_shape += [
            jax.ShapeDtypeStruct((batch, 1, seq, KV_RANK), _F32),
            jax.ShapeDtypeStruct((n_tiles * 2, 1, QK_ROPE, SUB_TILE), _F32),
        ]
        out_specs += [
            pl.BlockSpec((1, 1, tile, KV_RANK), lambda t: (out_tile(t)[0], 0, out_tile(t)[1], 0)),
            pl.BlockSpec((2, 1, QK_ROPE, SUB_TILE), lambda t: (jnp.maximum(t - 1, 0), 0, 0, 0)),
        ]
    return pl.pallas_call(
        functools.partial(_front_body, emit_cache=emit_cache, tiles_per_row=tiles_per_row),
        grid=(n_tiles + 1,),
        in_specs=in_specs,
        out_specs=out_specs,
        out_shape=out_shape,
        scratch_shapes=[pltpu.VMEM((SUB_TILE, D_MODEL), _BF16), pltpu.VMEM((SUB_TILE, D_MODEL), _BF16),
                        pltpu.VMEM((SUB_TILE, IN_WIDTH2), _F32),
                        pltpu.VMEM((SUB_TILE, IN_WIDTH2), _F32)],
        compiler_params=pltpu.CompilerParams(
            dimension_semantics=("arbitrary",),
            vmem_limit_bytes=VMEM_LIMIT_BYTES),
        name="front_ctx" if emit_cache else "front_lat",
    )(x, mod, norm_g, win2, ws_bf, bs_t, g_v, q_norm_g, wq2, kv_norm_g, tables)


def _key_chunks(k_refs):
    chunks = []
    for k_ref in k_refs:
        size = min(KEY_CHUNK, k_ref.shape[1])
        assert k_ref.shape[1] % size == 0
        chunks += [(k_ref, lo, size) for lo in range(0, k_ref.shape[1], size)]
    return chunks


def _values_t(k_ref, row):
    ckv_t = k_ref[row, :, 0:KV_RANK].T
    return jnp.concatenate([ckv_t, jnp.ones((ONES_ROWS, ckv_t.shape[1]), _BF16)], axis=0)


def _attention(q_ref, k_refs, qt_ref, fillers):
    n_rows = q_ref.shape[0]
    chunks = _key_chunks(k_refs)
    steps = [(row, hd, c) for row in range(n_rows) for pair in range(MLA_HEADS // 2)
             for c in range(len(chunks)) for hd in (2 * pair, 2 * pair + 1)]

    for row in range(n_rows):
        for hd in range(MLA_HEADS):
            qt_ref[row * MLA_HEADS + hd] = q_ref[row, :, hd * QK_WIDTH:(hd + 1) * QK_WIDTH].T

    def logits(row, hd, c):
        k_ref, lo, size = chunks[c]
        return jnp.dot(k_ref[row, lo:lo + size, :], qt_ref[row * MLA_HEADS + hd],
                       preferred_element_type=_F32)

    pending = [logits(*step) for step in steps[:SCORE_LOOKAHEAD]]
    values_t = [{id(k_ref): _values_t(k_ref, row) for k_ref in k_refs} for row in range(n_rows)]
    outs = {}
    state = {}
    for j, (row, hd, c) in enumerate(steps):
        m, acc = state.get((row, hd), (None, None))
        if j + SCORE_LOOKAHEAD < len(steps):
            pending.append(logits(*steps[j + SCORE_LOOKAHEAD]))
        s, pending[j] = pending[j], None
        k_ref, lo, size = chunks[c]
        chunk_max = jnp.max(s, axis=0, keepdims=True)
        if c > 0:
            m_new = jnp.maximum(m, chunk_max)
            acc = acc * jnp.exp2(m - m_new)
            m = m_new
        else:
            m = chunk_max
        p = jnp.exp2(s - m).astype(_BF16)
        part = jnp.dot(values_t[row][id(k_ref)][:, lo:lo + size], p, preferred_element_type=_F32)
        acc = acc + part if c > 0 else part
        state[row, hd] = (m, acc)
        if j >= len(steps) - len(fillers):
            fillers[j - len(steps) + len(fillers)]()
        if c == len(chunks) - 1:
            outs[row, hd] = (acc[0:KV_RANK, :] / acc[KV_RANK:KV_RANK + 1, :]).T
    return outs


def _attention_single(q_ref, k_ref, fillers):
    n_rows, tq = q_ref.shape[0], q_ref.shape[1]
    groups = [(row, pair) for row in range(n_rows) for pair in range(MLA_HEADS // 2)]
    scores = {}
    for row, pair in groups:
        q = jnp.concatenate([q_ref[row, :, hd * QK_WIDTH:(hd + 1) * QK_WIDTH]
                             for hd in (2 * pair, 2 * pair + 1)], axis=0)
        scores[row, pair] = lax.dot_general(q, k_ref[row], _NT_DIMS, preferred_element_type=_F32)
    for fill in fillers:
        fill()
    outs = {}
    for row, pair in groups:
        s = scores[row, pair]
        ckv = k_ref[row, :, 0:KV_RANK]
        values = jnp.concatenate([ckv, jnp.ones_like(ckv)], axis=-1)
        p = jnp.exp2(s - jnp.max(s, axis=1, keepdims=True)).astype(_BF16)
        acc = jnp.dot(p, values, preferred_element_type=_F32)
        o = acc[:, 0:KV_RANK] / acc[:, KV_RANK:2 * KV_RANK]
        outs[row, 2 * pair], outs[row, 2 * pair + 1] = o[0:tq, :], o[tq:2 * tq, :]
    return outs


def _back_body(*refs, has_ctx):
    if has_ctx:
        (q_ref, klat_ref, kctx_ref, a_ref, gb_ref, x_ref, mod_ref, wuv_ref, wo_ref, fg_ref,
         y_ref, qt_ref, ya_ref) = refs
        k_refs = (kctx_ref, klat_ref)
    else:
        (q_ref, klat_ref, a_ref, gb_ref, x_ref, mod_ref, wuv_ref, wo_ref, fg_ref, y_ref,
         qt_ref, ya_ref) = refs
        k_refs = (klat_ref,)
    n_rows, tq = q_ref.shape[0], q_ref.shape[1]

    block = min(OUT_ROWS, tq)
    blocks = [(row, lo) for row in range(n_rows) for lo in range(0, tq, block)]

    def a_part(row, lo):
        def run():
            ya_ref[row, lo:lo + block, :] = jnp.dot(
                a_ref[row, lo:lo + block, :], wo_ref[0:A_WIDTH, :], preferred_element_type=_F32)
        return run

    fillers = [a_part(*b) for b in blocks]
    if len(_key_chunks(k_refs)) == 1:
        outs = _attention_single(q_ref, k_refs[0], fillers)
    else:
        outs = _attention(q_ref, k_refs, qt_ref, fillers)

    gate = mod_ref[0:1, 2 * D_MODEL:3 * D_MODEL]
    for row, lo in blocks:
        rows = slice(lo, lo + block)
        attn_cols = []
        for pair in range(MLA_HEADS // 2):
            o2 = jnp.concatenate([outs[row, 2 * pair][rows, :],
                                  outs[row, 2 * pair + 1][rows, :]], axis=1)
            attn_cols.append(jnp.dot(o2.astype(_BF16), wuv_ref[pair],
                                     preferred_element_type=_F32))
        attn = jnp.concatenate(attn_cols, axis=1) * gb_ref[row, rows, :]
        y = ya_ref[row, rows, :] + jnp.dot(
            attn.astype(_BF16), wo_ref[A_WIDTH:A_WIDTH + B_WIDTH, :], preferred_element_type=_F32)
        out = x_ref[row, rows, :] + gate * y
        y_ref[row, rows, :] = _rmsnorm(out, fg_ref[...])


def _back_call(packed, kctx, gb, x, mod, mod_index, wuv2, wo_bf, final_g, rows_per_step):
    batch, seq, _ = x.shape
    tq = min(TOKEN_TILE, seq)
    has_ctx = kctx is not None
    rows = rows_per_step
    assert batch % rows == 0 and (rows == 1 or seq == tq)
    full = lambda shape: pl.BlockSpec(shape, lambda b, i: (0,) * len(shape))
    keys = lambda arr: pl.BlockSpec((rows, arr.shape[1], arr.shape[2]), lambda b, i: (b, 0, 0))
    in_specs = [
        pl.BlockSpec((rows, tq, MLA_HEADS * QK_WIDTH),
                     lambda b, i: (b, i, PK_Q // (MLA_HEADS * QK_WIDTH))),
        pl.BlockSpec((rows, seq, QK_WIDTH), lambda b, i: (b, 0, PK_K // QK_WIDTH)),
    ]
    args = [packed, packed]
    if has_ctx:
        in_specs += [keys(kctx)]
        args += [kctx]
    in_specs += [
        pl.BlockSpec((rows, tq, A_WIDTH), lambda b, i: (b, i, PK_A // A_WIDTH)),
        pl.BlockSpec((rows, tq, B_WIDTH), lambda b, i: (b, i, 0)),
        pl.BlockSpec((rows, tq, D_MODEL), lambda b, i: (b, i, 0)),
        pl.BlockSpec((SUBLANES, 3 * D_MODEL), lambda b, i: (mod_index(b * rows), 0)),
        full((MLA_HEADS // 2, 2 * KV_RANK, 2 * V_HEAD)),
        full((D_MODEL, D_MODEL)),
        full((1, D_MODEL)),
    ]
    args += [packed, gb, x, mod, wuv2, wo_bf, final_g]
    return pl.pallas_call(
        functools.partial(_back_body, has_ctx=has_ctx),
        grid=(batch // rows, seq // tq),
        in_specs=in_specs,
        out_specs=pl.BlockSpec((rows, tq, D_MODEL), lambda b, i: (b, i, 0)),
        out_shape=jax.ShapeDtypeStruct((batch, seq, D_MODEL), _F32),
        scratch_shapes=[
            pltpu.VMEM((rows * MLA_HEADS, QK_WIDTH, tq), _BF16),
            pltpu.VMEM((rows, tq, D_MODEL), _F32),
        ],
        compiler_params=pltpu.CompilerParams(
            dimension_semantics=("arbitrary", "arbitrary"),
            vmem_limit_bytes=VMEM_LIMIT_BYTES),
        name="back_lat" if has_ctx else "back_ctx",
    )(*args)


def _rope_tables(n_tokens, rotate):
    if rotate:
        rows = n_tokens // GRID_W
        inv = ROPE_THETA ** (-jnp.arange(AXIS_PAIRS, dtype=_F32) / AXIS_PAIRS)
        ang_r = jnp.arange(rows, dtype=_F32)[:, None] * inv
        ang_c = jnp.arange(GRID_W, dtype=_F32)[:, None] * inv

        def grid_table(fn):
            tr = jnp.broadcast_to(fn(ang_r)[:, None, :], (rows, GRID_W, AXIS_PAIRS))
            tc = jnp.broadcast_to(fn(ang_c)[None, :, :], (rows, GRID_W, AXIS_PAIRS))
            half = jnp.concatenate([tr, tc], axis=-1).reshape(n_tokens, 2 * AXIS_PAIRS)
            return jnp.concatenate([half, half], axis=-1)

        cos, sin = grid_table(jnp.cos), grid_table(jnp.sin)
    else:
        cos, sin = jnp.ones((n_tokens, QK_ROPE), _F32), jnp.zeros((n_tokens, QK_ROPE), _F32)
    return jnp.concatenate([cos, sin], axis=-1)


def kernel(x_prompt, x_sample, cache_ckv, cache_krope, c, c_ctx, norm_g, w_ada, b_ada, w_in, w_s,
           b_s, g_v, q_norm_g, w_uq, kv_norm_g, w_ukv, w_o, final_g):
    depth = norm_g.shape[0]
    assert depth == 1 and w_in.shape[2] == IN_WIDTH
    dec_batch = x_sample.shape[0]
    xp, xs = x_prompt, x_sample
    new_ckv, new_kr = [], []
    for l in range(depth):
        cond = jnp.concatenate(
            [c, c_ctx[None, :], jnp.zeros((16 - dec_batch - 1, D_MODEL), _F32)], axis=0)
        mod, win2, wq2, wuv2 = _weights_call(cond, w_ada[l], b_ada[l], jnp.transpose(w_in[l]),
                                             w_uq[l], w_ukv[l])
        wo_bf = w_o[l].astype(_BF16)
        ws_bf = w_s[l].astype(_BF16)
        bs_t = b_s[l].T
        gv_row = g_v[l].reshape(1, A_WIDTH)
        shared = (norm_g[l].reshape(1, D_MODEL), win2, ws_bf, bs_t, gv_row,
                  q_norm_g[l].reshape(1, Q_RANK), wq2, kv_norm_g[l].reshape(1, KV_RANK))
        fg = final_g.reshape(1, D_MODEL)

        ctx_index = lambda b: dec_batch
        ctx_batch, ctx_seq, _ = xp.shape
        pack = 2 * SUB_TILE // ctx_seq
        pk_c, gb_c, ckv_c, kr_c = _front_call(
            xp.reshape(ctx_batch // pack, pack * ctx_seq, D_MODEL), mod, ctx_index, *shared,
            _rope_tables(ctx_seq, False), True)
        unpack = lambda arr: arr.reshape(ctx_batch, ctx_seq, arr.shape[-1])
        xp = _back_call(unpack(pk_c), None, unpack(gb_c), xp, mod, ctx_index, wuv2, wo_bf, fg,
                        rows_per_step=pack)
        new_ckv.append(ckv_c.reshape(ctx_batch, 1, ctx_seq, KV_RANK))
        assert ctx_seq == SUB_TILE
        new_kr.append(jnp.swapaxes(kr_c, 2, 3))

        lat_index = lambda b: b
        pk_s, gb_s = _front_call(
            xs, mod, lat_index, *shared, _rope_tables(xs.shape[1], True), False)
        cache_k = jnp.concatenate(
            [cache_ckv[:, l], jnp.zeros(cache_krope[:, l].shape, _F32), cache_krope[:, l]],
            axis=-1).astype(_BF16)
        xs = _back_call(pk_s, cache_k, gb_s, xs, mod, lat_index, wuv2, wo_bf, fg, rows_per_step=1)
    return (xp, xs, jnp.concatenate(new_ckv, axis=1), jnp.concatenate(new_kr, axis=1))
```

```python
import functools
import math

import jax
import jax.numpy as jnp
from jax import lax
from jax.experimental import pallas as pl
from jax.experimental.pallas import tpu as pltpu

D_MODEL = 1024
GRID_W = 64
EPS = 1e-6
A_HEADS = 4
A_HEAD_DIM = 128
A_WIDTH = A_HEADS * A_HEAD_DIM
CHUNK = 128
MLA_HEADS = 4
QK_NOPE = 128
QK_ROPE = 64
V_HEAD = 128
B_WIDTH = MLA_HEADS * V_HEAD
Q_RANK = 256
KV_RANK = 128
AXIS_PAIRS = QK_ROPE // 4
ROPE_THETA = 10000.0
ATTN_SCALE = 1.0 / math.sqrt(QK_NOPE + QK_ROPE)
LOG2_E = 1.4426950408889634

QK_WIDTH = KV_RANK + 2 * QK_ROPE
OFF_U = 0
OFF_V = OFF_U + A_WIDTH
OFF_GA = OFF_V + A_WIDTH
OFF_CQ = OFF_GA + A_WIDTH
OFF_CKV = OFF_CQ + Q_RANK
OFF_KR = OFF_CKV + KV_RANK
OFF_GB = OFF_KR + 2 * QK_ROPE
IN_WIDTH = OFF_GB - QK_ROPE + B_WIDTH
IN_WIDTH2 = OFF_GB + B_WIDTH

PK_Q = 0
PK_A = PK_Q + MLA_HEADS * QK_WIDTH
PK_K = PK_A + A_WIDTH
PK_WIDTH = PK_K + QK_WIDTH

LANES = 128
SUBLANES = 8
ONES_ROWS = 16
SUB_TILE = 256
TOKEN_TILE = 512
PROJ_BLOCK = 256
OUT_ROWS = 256
X_SLOTS = 3
KEY_CHUNK = 256
SCORE_LOOKAHEAD = 3
VMEM_LIMIT_BYTES = 56 * 1024 * 1024

_F32 = jnp.float32
_BF16 = jnp.bfloat16
_NT_DIMS = (((1,), (1,)), ((), ()))


def _silu(x):
    hx = 0.5 * x
    return hx + hx * jnp.tanh(hx)


def _gelu_tanh(x):
    return x * (0.5 * (1.0 + jnp.tanh(0.7978845608028654 * (x + 0.044715 * (x * x * x)))))


def _rmsnorm(x, g):
    ms = jnp.mean(x * x, axis=-1, keepdims=True)
    return (x * lax.rsqrt(ms + EPS)) * g


def _mod_body(cond_ref, wa_ref, wb_ref, b_ref, o_ref):
    s = _silu(cond_ref[...]).astype(_BF16)
    halves = [jnp.dot(s, w_ref[...].astype(_BF16), preferred_element_type=_F32)
              for w_ref in (wa_ref, wb_ref)]
    mod = jnp.concatenate(halves, axis=1) + b_ref[...]
    rows, cols = mod.shape
    o_ref[...] = jnp.broadcast_to(mod[:, None, :], (rows, SUBLANES, cols)).reshape(
        rows * SUBLANES, cols)


def _fold_body(wq_ref, wkv_ref, q_ref, uv_ref):
    half, quarter = LANES // 2, LANES // 4
    assert QK_NOPE == LANES and QK_ROPE == half and V_HEAD == LANES
    lane = lax.broadcasted_iota(jnp.int32, (Q_RANK, LANES), 1)
    low = lane < half

    def cols(block):
        return wq_ref[:, block * LANES:(block + 1) * LANES]

    for hd in range(MLA_HEADS):
        start = hd * (QK_NOPE + QK_ROPE)
        b0, odd = divmod(start, LANES)
        if odd == 0:
            nope = cols(b0)
            rope_blk = cols(b0 + 1)
            rope_lo = rope_blk
            rope_hi = pltpu.roll(rope_blk, half, 1)
        else:
            assert odd == half
            nope = jnp.where(low, pltpu.roll(cols(b0), half, 1), pltpu.roll(cols(b0 + 1), half, 1))
            rope_blk = cols(b0 + 1)
            rope_lo = pltpu.roll(rope_blk, half, 1)
            rope_hi = rope_blk
        partner = jnp.where(lane < half + quarter,
                            -pltpu.roll(rope_hi, LANES - quarter, 1),
                            pltpu.roll(rope_hi, quarter, 1))
        w_uk = wkv_ref[:, hd * (QK_NOPE + V_HEAD):hd * (QK_NOPE + V_HEAD) + QK_NOPE]
        absorbed = lax.dot_general(nope, w_uk, _NT_DIMS, precision=lax.Precision.HIGHEST,
                                   preferred_element_type=_F32)
        base = hd * QK_WIDTH
        q_ref[:, base:base + KV_RANK] = absorbed.astype(_BF16)
        q_ref[:, base + KV_RANK:base + QK_WIDTH] = jnp.where(low, rope_lo, partner).astype(_BF16)

    uv_ref[...] = jnp.zeros(uv_ref.shape, _BF16)
    for hd in range(MLA_HEADS):
        w_uv = wkv_ref[:, hd * (QK_NOPE + V_HEAD) + QK_NOPE:(hd + 1) * (QK_NOPE + V_HEAD)]
        pair, pos = divmod(hd, 2)
        uv_ref[pair, pos * KV_RANK:(pos + 1) * KV_RANK, pos * V_HEAD:(pos + 1) * V_HEAD] = (
            w_uv.astype(_BF16))


_KR_BLOCK = OFF_KR // LANES


_PREP_SPLIT = IN_WIDTH2 // 2


def _prep_body(step, w_ref, o_ref):
    half = QK_ROPE // 2
    blocks_per_step = _PREP_SPLIT // LANES
    base = step * _PREP_SPLIT
    for j in range(step * blocks_per_step, (step + 1) * blocks_per_step):
        if j < _KR_BLOCK:
            blk = w_ref[j * LANES - base:(j + 1) * LANES - base, :]
        elif j == _KR_BLOCK:
            kr = w_ref[OFF_KR - base:OFF_KR - base + QK_ROPE, :]
            partner = jnp.concatenate([-kr[half:, :], kr[:half, :]], axis=0)
            blk = jnp.concatenate([kr, partner], axis=0)
        else:
            lo = j * LANES - QK_ROPE - base
            blk = w_ref[lo:lo + LANES, :]
        o_ref[:, j * LANES - base:(j + 1) * LANES - base] = blk.T.astype(_BF16)


def _weights_body(cond_ref, wa_ref, wb_ref, b_ref, win_ref, wq_ref, wkv_ref,
                  mod_ref, win2_ref, q_ref, uv_ref):
    j = pl.program_id(0)
    _mod_body(cond_ref, wa_ref, wb_ref, b_ref, mod_ref)
    n_prep = IN_WIDTH2 // _PREP_SPLIT
    for step in range(n_prep):
        pl.when(j == step)(functools.partial(_prep_body, step, win_ref, win2_ref))
    pl.when(j == n_prep)(functools.partial(_fold_body, wq_ref, wkv_ref, q_ref, uv_ref))


def _weights_call(cond, w_ada, b_ada, w_in_t, w_uq, w_ukv):
    rows = cond.shape[0]
    n_out = w_ada.shape[1]
    col_block = D_MODEL
    n_steps = n_out // col_block
    n_prep = IN_WIDTH2 // _PREP_SPLIT
    assert _KR_BLOCK * LANES == OFF_KR and OFF_KR >= _PREP_SPLIT and n_steps == n_prep + 1
    prep_step = lambda j: jnp.minimum(j, n_prep - 1)
    full = lambda shape: pl.BlockSpec(shape, lambda j: (0,) * len(shape))
    uv_shape = (MLA_HEADS // 2, 2 * KV_RANK, 2 * V_HEAD)
    return pl.pallas_call(
        _weights_body,
        grid=(n_steps,),
        in_specs=[
            full((rows, D_MODEL)),
            pl.BlockSpec((D_MODEL, col_block // 2), lambda j: (0, 2 * j)),
            pl.BlockSpec((D_MODEL, col_block // 2), lambda j: (0, 2 * j + 1)),
            pl.BlockSpec((1, col_block), lambda j: (0, j)),
            pl.BlockSpec((_PREP_SPLIT, D_MODEL), lambda j: (prep_step(j), 0)),
            full(w_uq.shape),
            full(w_ukv.shape),
        ],
        out_specs=[
            pl.BlockSpec((rows * SUBLANES, col_block), lambda j: (0, j)),
            pl.BlockSpec((D_MODEL, _PREP_SPLIT), lambda j: (0, prep_step(j))),
            full((Q_RANK, MLA_HEADS * QK_WIDTH)),
            full(uv_shape),
        ],
        out_shape=[
            jax.ShapeDtypeStruct((rows * SUBLANES, n_out), _F32),
            jax.ShapeDtypeStruct((D_MODEL, IN_WIDTH2), _BF16),
            jax.ShapeDtypeStruct((Q_RANK, MLA_HEADS * QK_WIDTH), _BF16),
            jax.ShapeDtypeStruct(uv_shape, _BF16),
        ],
        compiler_params=pltpu.CompilerParams(
            dimension_semantics=("arbitrary",), vmem_limit_bytes=VMEM_LIMIT_BYTES),
        name="weights",
    )(cond, w_ada, w_ada, b_ada.reshape(1, n_out), w_in_t, w_uq, w_ukv)


def _front_substep(x_ref, mod_ref, ng_ref, win_ref, ws_ref, bs_ref, gv_ref, qg_ref, wq_ref,
                   kvg_ref, tables, out_refs, row0, h_write, h_read, z_write, z_read, emit_cache):
    if emit_cache:
        pk_ref, gb_ref, ckv_ref, kr_ref = out_refs
    else:
        pk_ref, gb_ref = out_refs
    first_half = lax.broadcasted_iota(jnp.int32, tables.shape, 1) < QK_ROPE
    t1 = jnp.where(first_half, tables, 0.0)
    t2k = jnp.where(first_half, pltpu.roll(tables, QK_ROPE, 1), 0.0)
    t2q = jnp.where(first_half, t2k, 1.0)
    rows = slice(row0, row0 + SUB_TILE)
    half = SUB_TILE // 2
    assert IN_WIDTH2 == 10 * PROJ_BLOCK

    def project(j):
        lo = j * PROJ_BLOCK
        z_write[:, lo:lo + PROJ_BLOCK] = jnp.dot(h_read[...], win_ref[:, lo:lo + PROJ_BLOCK],
                                                 preferred_element_type=_F32)

    shift = mod_ref[0:1, 0:D_MODEL]
    gain = ng_ref[...] * (1.0 + mod_ref[0:1, D_MODEL:2 * D_MODEL])

    def norm_rows(lo):
        x = x_ref[0, row0 + lo:row0 + lo + half, :]
        ms = jnp.mean(x * x, axis=-1, keepdims=True)
        h_write[lo:lo + half, :] = ((x * lax.rsqrt(ms + EPS)) * gain + shift).astype(_BF16)

    def vn_head(hd):
        lo = hd * A_HEAD_DIM
        v = _gelu_tanh(z_read[:, OFF_V + lo:OFF_V + lo + A_HEAD_DIM])
        return _rmsnorm(v, gv_ref[:, lo:lo + A_HEAD_DIM]).astype(_BF16)

    def mix_head(hd, vn):
        bias = bs_ref[:, hd:hd + 1]
        parts = [jnp.dot(ws_ref[hd], vn[c * CHUNK:(c + 1) * CHUNK, :],
                         preferred_element_type=_F32) + bias for c in range(SUB_TILE // CHUNK)]
        return jnp.concatenate(parts, axis=0)

    def gate_a(hd, mixed):
        lo = hd * A_HEAD_DIM
        u = _gelu_tanh(z_read[:, OFF_U + lo:OFF_U + lo + A_HEAD_DIM])
        g = _silu(z_read[:, OFF_GA + lo:OFF_GA + lo + A_HEAD_DIM])
        pk_ref[0, rows, PK_A + lo:PK_A + lo + A_HEAD_DIM] = (u * mixed * g).astype(_BF16)

    def gate_b(lo, width):
        gb_ref[0, rows, lo:lo + width] = _silu(z_read[:, OFF_GB + lo:OFF_GB + lo + width])

    def rope_q(qa, hd):
        base = hd * QK_WIDTH
        pair = qa[:, base + KV_RANK:base + QK_WIDTH]
        rope = pair * t1 + pltpu.roll(pair, QK_ROPE, 1) * t2q
        pk_ref[0, rows, PK_Q + base:PK_Q + base + QK_WIDTH] = jnp.concatenate(
            [qa[:, base:base + KV_RANK], rope], axis=1).astype(_BF16)

    project(0)
    cq = _rmsnorm(z_read[:, OFF_CQ:OFF_CQ + Q_RANK],
                  qg_ref[...] * (ATTN_SCALE * LOG2_E)).astype(_BF16)
    vn0 = vn_head(0)
    project(1)
    vn1 = vn_head(1)
    gate_b(0, B_WIDTH // 2)
    project(2)
    vn2 = vn_head(2)
    gate_b(B_WIDTH // 2, B_WIDTH // 2)
    project(3)
    vn3 = vn_head(3)
    ckv = _rmsnorm(z_read[:, OFF_CKV:OFF_CKV + KV_RANK], kvg_ref[...])
    kpair = z_read[:, OFF_KR:OFF_KR + 2 * QK_ROPE]
    krot = kpair * t1 + pltpu.roll(kpair, QK_ROPE, 1) * t2k
    pk_ref[0, rows, PK_K:PK_K + QK_WIDTH] = jnp.concatenate([ckv, krot], axis=1).astype(_BF16)
    if emit_cache:
        ckv_ref[0, 0, rows, :] = ckv
        kr_ref[row0 // SUB_TILE, 0] = kpair.T[0:QK_ROPE, :]
    qa = jnp.dot(cq, wq_ref[...], preferred_element_type=_F32)
    norm_rows(0)
    project(4)
    rope_q(qa, 0)
    rope_q(qa, 1)
    mixed0 = mix_head(0, vn0)
    mixed1 = mix_head(1, vn1)
    rope_q(qa, 2)
    rope_q(qa, 3)
    project(5)
    gate_a(0, mixed0)
    mixed2 = mix_head(2, vn2)
    mixed3 = mix_head(3, vn3)
    project(6)
    gate_a(1, mixed1)
    norm_rows(half)
    project(7)
    gate_a(2, mixed2)
    project(8)
    gate_a(3, mixed3)
    project(9)


def _front_body(x_ref, mod_ref, ng_ref, win_ref, ws_ref, bs_ref, gv_ref, qg_ref, wq_ref,
                kvg_ref, tab_ref, *rest, emit_cache, tiles_per_row):
    out_refs, (h_a, h_b, z_a, z_b) = rest[:-4], rest[-4:]
    t = pl.program_id(0)
    tile = x_ref.shape[1]
    assert tile == 2 * SUB_TILE
    n_pos = tab_ref.shape[0]

    @pl.when(t == 0)
    def _():
        h_b[...] = jnp.zeros(h_b.shape, _BF16)
        z_b[...] = jnp.zeros(z_b.shape, _F32)

    pos_base = (jnp.maximum(t - 1, 0) % tiles_per_row) * tile
    for sub, (h_write, h_read, z_write, z_read) in enumerate(
            ((h_a, h_b, z_a, z_b), (h_b, h_a, z_b, z_a))):
        row0 = sub * SUB_TILE
        pos = pl.multiple_of((pos_base + row0) % n_pos, SUB_TILE)
        tables = tab_ref[pl.ds(pos, SUB_TILE), :]
        _front_substep(x_ref, mod_ref, ng_ref, win_ref, ws_ref, bs_ref, gv_ref, qg_ref, wq_ref,
                       kvg_ref, tables, out_refs, row0, h_write, h_read, z_write, z_read,
                       emit_cache)


def _front_call(x, mod, mod_index, norm_g, win2, ws_bf, bs_t, g_v, q_norm_g, wq2, kv_norm_g,
                tables, emit_cache):
    batch, seq, _ = x.shape
    tile = 2 * SUB_TILE
    tiles_per_row = seq // tile
    n_tiles = batch * tiles_per_row
    assert seq % tile == 0 and tables.shape[0] % SUB_TILE == 0

    def in_tile(t):
        tt = jnp.minimum(t, n_tiles - 1)
        return tt // tiles_per_row, tt % tiles_per_row

    def out_tile(t):
        tt = jnp.maximum(t - 1, 0)
        return tt // tiles_per_row, tt % tiles_per_row

    full = lambda shape: pl.BlockSpec(shape, lambda t: (0,) * len(shape))
    in_specs = [
        pl.BlockSpec((1, tile, D_MODEL), lambda t: (*in_tile(t), 0)),
        pl.BlockSpec((SUBLANES, 3 * D_MODEL), lambda t: (mod_index(in_tile(t)[0]), 0)),
        full((1, D_MODEL)),
        full((D_MODEL, IN_WIDTH2)),
        full((A_HEADS, CHUNK, CHUNK)),
        full((CHUNK, A_HEADS)),
        full((1, A_WIDTH)),
        full((1, Q_RANK)),
        full((Q_RANK, MLA_HEADS * QK_WIDTH)),
        full((1, KV_RANK)),
        full(tables.shape),
    ]
    out_shape = [
        jax.ShapeDtypeStruct((batch, seq, PK_WIDTH), _BF16),
        jax.ShapeDtypeStruct((batch, seq, B_WIDTH), _F32),
    ]
    out_specs = [
        pl.BlockSpec((1, tile, PK_WIDTH), lambda t: (*out_tile(t), 0)),
        pl.BlockSpec((1, tile, B_WIDTH), lambda t: (*out_tile(t), 0)),
    ]
    if emit_cache:
        out_shape += [
            jax.ShapeDtypeStruct((batch, 1, seq, KV_RANK), _F32),
            jax.ShapeDtypeStruct((n_tiles * 2, 1, QK_ROPE, SUB_TILE), _F32),
        ]
        out_specs += [
            pl.BlockSpec((1, 1, tile, KV_RANK), lambda t: (out_tile(t)[0], 0, out_tile(t)[1], 0)),
            pl.BlockSpec((2, 1, QK_ROPE, SUB_TILE), lambda t: (jnp.maximum(t - 1, 0), 0, 0, 0)),
        ]
    return pl.pallas_call(
        functools.partial(_front_body, emit_cache=emit_cache, tiles_per_row=tiles_per_row),
        grid=(n_tiles + 1,),
        in_specs=in_specs,
        out_specs=out_specs,
        out_shape=out_shape,
        scratch_shapes=[pltpu.VMEM((SUB_TILE, D_MODEL), _BF16), pltpu.VMEM((SUB_TILE, D_MODEL), _BF16),
                        pltpu.VMEM((SUB_TILE, IN_WIDTH2), _F32),
                        pltpu.VMEM((SUB_TILE, IN_WIDTH2), _F32)],
        compiler_params=pltpu.CompilerParams(
            dimension_semantics=("arbitrary",),
            vmem_limit_bytes=VMEM_LIMIT_BYTES),
        name="front_ctx" if emit_cache else "front_lat",
    )(x, mod, norm_g, win2, ws_bf, bs_t, g_v, q_norm_g, wq2, kv_norm_g, tables)


def _key_chunks(k_refs):
    chunks = []
    for k_ref in k_refs:
        size = min(KEY_CHUNK, k_ref.shape[1])
        assert k_ref.shape[1] % size == 0
        chunks += [(k_ref, lo, size) for lo in range(0, k_ref.shape[1], size)]
    return chunks


def _values_t(k_ref, row):
    ckv_t = k_ref[row, :, 0:KV_RANK].T
    return jnp.concatenate([ckv_t, jnp.ones((ONES_ROWS, ckv_t.shape[1]), _BF16)], axis=0)


def _attention(q_ref, k_refs, qt_ref, fillers):
    n_rows = q_ref.shape[0]
    chunks = _key_chunks(k_refs)
    steps = [(row, hd, c) for row in range(n_rows) for pair in range(MLA_HEADS // 2)
             for c in range(len(chunks)) for hd in (2 * pair, 2 * pair + 1)]

    for row in range(n_rows):
        for hd in range(MLA_HEADS):
            qt_ref[row * MLA_HEADS + hd] = q_ref[row, :, hd * QK_WIDTH:(hd + 1) * QK_WIDTH].T

    def logits(row, hd, c):
        k_ref, lo, size = chunks[c]
        return jnp.dot(k_ref[row, lo:lo + size, :], qt_ref[row * MLA_HEADS + hd],
                       preferred_element_type=_F32)

    pending = [logits(*step) for step in steps[:SCORE_LOOKAHEAD]]
    values_t = [{id(k_ref): _values_t(k_ref, row) for k_ref in k_refs} for row in range(n_rows)]
    outs = {}
    state = {}
    for j, (row, hd, c) in enumerate(steps):
        m, acc = state.get((row, hd), (None, None))
        if j + SCORE_LOOKAHEAD < len(steps):
            pending.append(logits(*steps[j + SCORE_LOOKAHEAD]))
        s, pending[j] = pending[j], None
        k_ref, lo, size = chunks[c]
        chunk_max = jnp.max(s, axis=0, keepdims=True)
        if c > 0:
            m_new = jnp.maximum(m, chunk_max)
            acc = acc * jnp.exp2(m - m_new)
            m = m_new
        else:
            m = chunk_max
        p = jnp.exp2(s - m).astype(_BF16)
        part = jnp.dot(values_t[row][id(k_ref)][:, lo:lo + size], p, preferred_element_type=_F32)
        acc = acc + part if c > 0 else part
        state[row, hd] = (m, acc)
        if j >= len(steps) - len(fillers):
            fillers[j - len(steps) + len(fillers)]()
        if c == len(chunks) - 1:
            outs[row, hd] = (acc[0:KV_RANK, :] / acc[KV_RANK:KV_RANK + 1, :]).T
    return outs


def _attention_single(q_ref, k_ref, fillers):
    n_rows, tq = q_ref.shape[0], q_ref.shape[1]
    groups = [(row, pair) for row in range(n_rows) for pair in range(MLA_HEADS // 2)]
    scores = {}
    for row, pair in groups:
        q = jnp.concatenate([q_ref[row, :, hd * QK_WIDTH:(hd + 1) * QK_WIDTH]
                             for hd in (2 * pair, 2 * pair + 1)], axis=0)
        scores[row, pair] = lax.dot_general(q, k_ref[row], _NT_DIMS, preferred_element_type=_F32)
    for fill in fillers:
        fill()
    outs = {}
    for row, pair in groups:
        s = scores[row, pair]
        ckv = k_ref[row, :, 0:KV_RANK]
        values = jnp.concatenate([ckv, jnp.ones_like(ckv)], axis=-1)
        p = jnp.exp2(s - jnp.max(s, axis=1, keepdims=True)).astype(_BF16)
        acc = jnp.dot(p, values, preferred_element_type=_F32)
        o = acc[:, 0:KV_RANK] / acc[:, KV_RANK:2 * KV_RANK]
        outs[row, 2 * pair], outs[row, 2 * pair + 1] = o[0:tq, :], o[tq:2 * tq, :]
    return outs


def _x_copy(x_hbm, xbuf, sem, step):
    _, n_rows, tq, _ = xbuf.shape
    tiles = x_hbm.shape[1] // tq
    slot = step % X_SLOTS
    return pltpu.make_async_copy(
        x_hbm.at[pl.ds((step // tiles) * n_rows, n_rows), pl.ds((step % tiles) * tq, tq)],
        xbuf.at[slot], sem.at[slot])


def _back_body(*refs, has_ctx):
    if has_ctx:
        (q_ref, klat_ref, kctx_ref, a_ref, gb_ref, x_ref, mod_ref, wuv_ref, wo_ref, fg_ref,
         y_ref, qt_ref, ya_ref) = refs
        k_refs = (kctx_ref, klat_ref)
    else:
        (q_ref, klat_ref, a_ref, gb_ref, x_hbm, mod_ref, wuv_ref, wo_ref, fg_ref, y_ref,
         qt_ref, ya_ref, xbuf, xsem) = refs
        k_refs = (klat_ref,)
    n_rows, tq = q_ref.shape[0], q_ref.shape[1]

    ring = not has_ctx
    if ring:
        step = pl.program_id(0) * pl.num_programs(1) + pl.program_id(1)
        n_steps = pl.num_programs(0) * pl.num_programs(1)

        @pl.when(step == 0)
        def _():
            for ahead in range(X_SLOTS - 1):
                _x_copy(x_hbm, xbuf, xsem, ahead).start()

        @pl.when(step + X_SLOTS - 1 < n_steps)
        def _():
            _x_copy(x_hbm, xbuf, xsem, step + X_SLOTS - 1).start()

    block = min(OUT_ROWS, tq)
    blocks = [(row, lo) for row in range(n_rows) for lo in range(0, tq, block)]

    def a_part(row, lo):
        def run():
            ya_ref[row, lo:lo + block, :] = jnp.dot(
                a_ref[row, lo:lo + block, :], wo_ref[0:A_WIDTH, :], preferred_element_type=_F32)
        return run

    fillers = [a_part(*b) for b in blocks]
    if len(_key_chunks(k_refs)) == 1:
        outs = _attention_single(q_ref, k_refs[0], fillers)
    else:
        outs = _attention(q_ref, k_refs, qt_ref, fillers)

    gate = mod_ref[0:1, 2 * D_MODEL:3 * D_MODEL]
    if ring:
        _x_copy(x_hbm, xbuf, xsem, step).wait()
        x_ref = xbuf.at[step % X_SLOTS]
    for row, lo in blocks:
        rows = slice(lo, lo + block)
        attn_cols = []
        for pair in range(MLA_HEADS // 2):
            o2 = jnp.concatenate([outs[row, 2 * pair][rows, :],
                                  outs[row, 2 * pair + 1][rows, :]], axis=1)
            attn_cols.append(jnp.dot(o2.astype(_BF16), wuv_ref[pair],
                                     preferred_element_type=_F32))
        attn = jnp.concatenate(attn_cols, axis=1) * gb_ref[row, rows, :]
        y = ya_ref[row, rows, :] + jnp.dot(
            attn.astype(_BF16), wo_ref[A_WIDTH:A_WIDTH + B_WIDTH, :], preferred_element_type=_F32)
        out = x_ref[row, rows, :] + gate * y
        y_ref[row, rows, :] = _rmsnorm(out, fg_ref[...])


def _back_call(packed, kctx, gb, x, mod, mod_index, wuv2, wo_bf, final_g, rows_per_step):
    batch, seq, _ = x.shape
    tq = min(TOKEN_TILE, seq)
    has_ctx = kctx is not None
    rows = rows_per_step
    assert batch % rows == 0 and (rows == 1 or seq == tq)
    assert (batch // rows) * (seq // tq) >= X_SLOTS - 1
    full = lambda shape: pl.BlockSpec(shape, lambda b, i: (0,) * len(shape))
    keys = lambda arr: pl.BlockSpec((rows, arr.shape[1], arr.shape[2]), lambda b, i: (b, 0, 0))
    in_specs = [
        pl.BlockSpec((rows, tq, MLA_HEADS * QK_WIDTH),
                     lambda b, i: (b, i, PK_Q // (MLA_HEADS * QK_WIDTH))),
        pl.BlockSpec((rows, seq, QK_WIDTH), lambda b, i: (b, 0, PK_K // QK_WIDTH)),
    ]
    args = [packed, packed]
    if has_ctx:
        in_specs += [keys(kctx)]
        args += [kctx]
    in_specs += [
        pl.BlockSpec((rows, tq, A_WIDTH), lambda b, i: (b, i, PK_A // A_WIDTH)),
        pl.BlockSpec((rows, tq, B_WIDTH), lambda b, i: (b, i, 0)),
        (pl.BlockSpec((rows, tq, D_MODEL), lambda b, i: (b, i, 0)) if has_ctx
         else pl.BlockSpec(memory_space=pl.ANY)),
        pl.BlockSpec((SUBLANES, 3 * D_MODEL), lambda b, i: (mod_index(b * rows), 0)),
        full((MLA_HEADS // 2, 2 * KV_RANK, 2 * V_HEAD)),
        full((D_MODEL, D_MODEL)),
        full((1, D_MODEL)),
    ]
    args += [packed, gb, x, mod, wuv2, wo_bf, final_g]
    return pl.pallas_call(
        functools.partial(_back_body, has_ctx=has_ctx),
        grid=(batch // rows, seq // tq),
        in_specs=in_specs,
        out_specs=pl.BlockSpec((rows, tq, D_MODEL), lambda b, i: (b, i, 0)),
        out_shape=jax.ShapeDtypeStruct((batch, seq, D_MODEL), _F32),
        scratch_shapes=[
            pltpu.VMEM((rows * MLA_HEADS, QK_WIDTH, tq), _BF16),
            pltpu.VMEM((rows, tq, D_MODEL), _F32),
        ] + ([] if has_ctx else [
            pltpu.VMEM((X_SLOTS, rows, tq, D_MODEL), _F32),
            pltpu.SemaphoreType.DMA((X_SLOTS,)),
        ]),
        compiler_params=pltpu.CompilerParams(
            dimension_semantics=("arbitrary", "arbitrary"),
            vmem_limit_bytes=VMEM_LIMIT_BYTES),
        name="back_lat" if has_ctx else "back_ctx",
    )(*args)


def _rope_tables(n_tokens, rotate):
    if rotate:
        rows = n_tokens // GRID_W
        inv = ROPE_THETA ** (-jnp.arange(AXIS_PAIRS, dtype=_F32) / AXIS_PAIRS)
        ang_r = jnp.arange(rows, dtype=_F32)[:, None] * inv
        ang_c = jnp.arange(GRID_W, dtype=_F32)[:, None] * inv

        def grid_table(fn):
            tr = jnp.broadcast_to(fn(ang_r)[:, None, :], (rows, GRID_W, AXIS_PAIRS))
            tc = jnp.broadcast_to(fn(ang_c)[None, :, :], (rows, GRID_W, AXIS_PAIRS))
            half = jnp.concatenate([tr, tc], axis=-1).reshape(n_tokens, 2 * AXIS_PAIRS)
            return jnp.concatenate([half, half], axis=-1)

        cos, sin = grid_table(jnp.cos), grid_table(jnp.sin)
    else:
        cos, sin = jnp.ones((n_tokens, QK_ROPE), _F32), jnp.zeros((n_tokens, QK_ROPE), _F32)
    return jnp.concatenate([cos, sin], axis=-1)


def kernel(x_prompt, x_sample, cache_ckv, cache_krope, c, c_ctx, norm_g, w_ada, b_ada, w_in, w_s,
           b_s, g_v, q_norm_g, w_uq, kv_norm_g, w_ukv, w_o, final_g):
    depth = norm_g.shape[0]
    assert depth == 1 and w_in.shape[2] == IN_WIDTH
    dec_batch = x_sample.shape[0]
    xp, xs = x_prompt, x_sample
    new_ckv, new_kr = [], []
    for l in range(depth):
        cond = jnp.concatenate(
            [c, c_ctx[None, :], jnp.zeros((16 - dec_batch - 1, D_MODEL), _F32)], axis=0)
        mod, win2, wq2, wuv2 = _weights_call(cond, w_ada[l], b_ada[l], jnp.transpose(w_in[l]),
                                             w_uq[l], w_ukv[l])
        wo_bf = w_o[l].astype(_BF16)
        ws_bf = w_s[l].astype(_BF16)
        bs_t = b_s[l].T
        gv_row = g_v[l].reshape(1, A_WIDTH)
        shared = (norm_g[l].reshape(1, D_MODEL), win2, ws_bf, bs_t, gv_row,
                  q_norm_g[l].reshape(1, Q_RANK), wq2, kv_norm_g[l].reshape(1, KV_RANK))
        fg = final_g.reshape(1, D_MODEL)

        ctx_index = lambda b: dec_batch
        ctx_batch, ctx_seq, _ = xp.shape
        pack = 2 * SUB_TILE // ctx_seq
        pk_c, gb_c, ckv_c, kr_c = _front_call(
            xp.reshape(ctx_batch // pack, pack * ctx_seq, D_MODEL), mod, ctx_index, *shared,
            _rope_tables(ctx_seq, False), True)
        unpack = lambda arr: arr.reshape(ctx_batch, ctx_seq, arr.shape[-1])
        xp = _back_call(unpack(pk_c), None, unpack(gb_c), xp, mod, ctx_index, wuv2, wo_bf, fg,
                        rows_per_step=pack)
        new_ckv.append(ckv_c.reshape(ctx_batch, 1, ctx_seq, KV_RANK))
        assert ctx_seq == SUB_TILE
        new_kr.append(jnp.swapaxes(kr_c, 2, 3))

        lat_index = lambda b: b
        pk_s, gb_s = _front_call(
            xs, mod, lat_index, *shared, _rope_tables(xs.shape[1], True), False)
        cache_k = jnp.concatenate(
            [cache_ckv[:, l], jnp.zeros(cache_krope[:, l].shape, _F32), cache_krope[:, l]],
            axis=-1).astype(_BF16)
        xs = _back_call(pk_s, cache_k, gb_s, xs, mod, lat_index, wuv2, wo_bf, fg, rows_per_step=1)
    return (xp, xs, jnp.concatenate(new_ckv, axis=1), jnp.concatenate(new_kr, axis=1))
```

```python
import functools
import math

import jax
import jax.numpy as jnp
from jax import lax
from jax.experimental import pallas as pl
from jax.experimental.pallas import tpu as pltpu

D_MODEL = 1024
GRID_W = 64
EPS = 1e-6
A_HEADS = 4
A_HEAD_DIM = 128
A_WIDTH = A_HEADS * A_HEAD_DIM
CHUNK = 128
MLA_HEADS = 4
QK_NOPE = 128
QK_ROPE = 64
V_HEAD = 128
B_WIDTH = MLA_HEADS * V_HEAD
Q_RANK = 256
KV_RANK = 128
AXIS_PAIRS = QK_ROPE // 4
ROPE_THETA = 10000.0
ATTN_SCALE = 1.0 / math.sqrt(QK_NOPE + QK_ROPE)
LOG2_E = 1.4426950408889634

QK_WIDTH = KV_RANK + 2 * QK_ROPE
OFF_U = 0
OFF_V = OFF_U + A_WIDTH
OFF_GA = OFF_V + A_WIDTH
OFF_CQ = OFF_GA + A_WIDTH
OFF_CKV = OFF_CQ + Q_RANK
OFF_KR = OFF_CKV + KV_RANK
OFF_GB = OFF_KR + 2 * QK_ROPE
IN_WIDTH = OFF_GB - QK_ROPE + B_WIDTH
IN_WIDTH2 = OFF_GB + B_WIDTH

PK_Q = 0
PK_A = PK_Q + MLA_HEADS * QK_WIDTH
PK_K = PK_A + A_WIDTH
PK_WIDTH = PK_K + QK_WIDTH

LANES = 128
SUBLANES = 8
ONES_ROWS = 16
SUB_TILE = 256
TOKEN_TILE = 512
PROJ_BLOCK = 256
OUT_ROWS = 256
X_SLOTS = 3
KEY_CHUNK = 256
SCORE_LOOKAHEAD = 3
VMEM_LIMIT_BYTES = 56 * 1024 * 1024

_F32 = jnp.float32
_BF16 = jnp.bfloat16
_NT_DIMS = (((1,), (1,)), ((), ()))


def _silu(x):
    hx = 0.5 * x
    return hx + hx * jnp.tanh(hx)


def _gelu_tanh(x):
    return x * (0.5 * (1.0 + jnp.tanh(0.7978845608028654 * (x + 0.044715 * (x * x * x)))))


def _rmsnorm(x, g):
    ms = jnp.mean(x * x, axis=-1, keepdims=True)
    return (x * lax.rsqrt(ms + EPS)) * g


def _mod_body(cond_ref, wa_ref, wb_ref, b_ref, o_ref):
    s = _silu(cond_ref[...]).astype(_BF16)
    halves = [jnp.dot(s, w_ref[...].astype(_BF16), preferred_element_type=_F32)
              for w_ref in (wa_ref, wb_ref)]
    mod = jnp.concatenate(halves, axis=1) + b_ref[...]
    rows, cols = mod.shape
    o_ref[...] = jnp.broadcast_to(mod[:, None, :], (rows, SUBLANES, cols)).reshape(
        rows * SUBLANES, cols)


def _fold_body(wq_ref, wkv_ref, q_ref, uv_ref):
    half, quarter = LANES // 2, LANES // 4
    assert QK_NOPE == LANES and QK_ROPE == half and V_HEAD == LANES
    lane = lax.broadcasted_iota(jnp.int32, (Q_RANK, LANES), 1)
    low = lane < half

    def cols(block):
        return wq_ref[:, block * LANES:(block + 1) * LANES]

    for hd in range(MLA_HEADS):
        start = hd * (QK_NOPE + QK_ROPE)
        b0, odd = divmod(start, LANES)
        if odd == 0:
            nope = cols(b0)
            rope_blk = cols(b0 + 1)
            rope_lo = rope_blk
            rope_hi = pltpu.roll(rope_blk, half, 1)
        else:
            assert odd == half
            nope = jnp.where(low, pltpu.roll(cols(b0), half, 1), pltpu.roll(cols(b0 + 1), half, 1))
            rope_blk = cols(b0 + 1)
            rope_lo = pltpu.roll(rope_blk, half, 1)
            rope_hi = rope_blk
        partner = jnp.where(lane < half + quarter,
                            -pltpu.roll(rope_hi, LANES - quarter, 1),
                            pltpu.roll(rope_hi, quarter, 1))
        w_uk = wkv_ref[:, hd * (QK_NOPE + V_HEAD):hd * (QK_NOPE + V_HEAD) + QK_NOPE]
        absorbed = lax.dot_general(nope, w_uk, _NT_DIMS, precision=lax.Precision.HIGHEST,
                                   preferred_element_type=_F32)
        base = hd * QK_WIDTH
        q_ref[:, base:base + KV_RANK] = absorbed.astype(_BF16)
        q_ref[:, base + KV_RANK:base + QK_WIDTH] = jnp.where(low, rope_lo, partner).astype(_BF16)

    uv_ref[...] = jnp.zeros(uv_ref.shape, _BF16)
    for hd in range(MLA_HEADS):
        w_uv = wkv_ref[:, hd * (QK_NOPE + V_HEAD) + QK_NOPE:(hd + 1) * (QK_NOPE + V_HEAD)]
        pair, pos = divmod(hd, 2)
        uv_ref[pair, pos * KV_RANK:(pos + 1) * KV_RANK, pos * V_HEAD:(pos + 1) * V_HEAD] = (
            w_uv.astype(_BF16))


_KR_BLOCK = OFF_KR // LANES


_PREP_SPLIT = IN_WIDTH2 // 2


def _prep_body(step, w_ref, o_ref):
    half = QK_ROPE // 2
    blocks_per_step = _PREP_SPLIT // LANES
    base = step * _PREP_SPLIT
    for j in range(step * blocks_per_step, (step + 1) * blocks_per_step):
        if j < _KR_BLOCK:
            blk = w_ref[j * LANES - base:(j + 1) * LANES - base, :]
        elif j == _KR_BLOCK:
            kr = w_ref[OFF_KR - base:OFF_KR - base + QK_ROPE, :]
            partner = jnp.concatenate([-kr[half:, :], kr[:half, :]], axis=0)
            blk = jnp.concatenate([kr, partner], axis=0)
        else:
            lo = j * LANES - QK_ROPE - base
            blk = w_ref[lo:lo + LANES, :]
        o_ref[:, j * LANES - base:(j + 1) * LANES - base] = blk.T.astype(_BF16)


def _weights_body(cond_ref, wa_ref, wb_ref, b_ref, win_ref, wq_ref, wkv_ref,
                  mod_ref, win2_ref, q_ref, uv_ref):
    j = pl.program_id(0)
    _mod_body(cond_ref, wa_ref, wb_ref, b_ref, mod_ref)
    n_prep = IN_WIDTH2 // _PREP_SPLIT
    for step in range(n_prep):
        pl.when(j == step)(functools.partial(_prep_body, step, win_ref, win2_ref))
    pl.when(j == n_prep)(functools.partial(_fold_body, wq_ref, wkv_ref, q_ref, uv_ref))


def _weights_call(cond, w_ada, b_ada, w_in_t, w_uq, w_ukv):
    rows = cond.shape[0]
    n_out = w_ada.shape[1]
    col_block = D_MODEL
    n_steps = n_out // col_block
    n_prep = IN_WIDTH2 // _PREP_SPLIT
    assert _KR_BLOCK * LANES == OFF_KR and OFF_KR >= _PREP_SPLIT and n_steps == n_prep + 1
    prep_step = lambda j: jnp.minimum(j, n_prep - 1)
    full = lambda shape: pl.BlockSpec(shape, lambda j: (0,) * len(shape))
    uv_shape = (MLA_HEADS // 2, 2 * KV_RANK, 2 * V_HEAD)
    return pl.pallas_call(
        _weights_body,
        grid=(n_steps,),
        in_specs=[
            full((rows, D_MODEL)),
            pl.BlockSpec((D_MODEL, col_block // 2), lambda j: (0, 2 * j)),
            pl.BlockSpec((D_MODEL, col_block // 2), lambda j: (0, 2 * j + 1)),
            pl.BlockSpec((1, col_block), lambda j: (0, j)),
            pl.BlockSpec((_PREP_SPLIT, D_MODEL), lambda j: (prep_step(j), 0)),
            full(w_uq.shape),
            full(w_ukv.shape),
        ],
        out_specs=[
            pl.BlockSpec((rows * SUBLANES, col_block), lambda j: (0, j)),
            pl.BlockSpec((D_MODEL, _PREP_SPLIT), lambda j: (0, prep_step(j))),
            full((Q_RANK, MLA_HEADS * QK_WIDTH)),
            full(uv_shape),
        ],
        out_shape=[
            jax.ShapeDtypeStruct((rows * SUBLANES, n_out), _F32),
            jax.ShapeDtypeStruct((D_MODEL, IN_WIDTH2), _BF16),
            jax.ShapeDtypeStruct((Q_RANK, MLA_HEADS * QK_WIDTH), _BF16),
            jax.ShapeDtypeStruct(uv_shape, _BF16),
        ],
        compiler_params=pltpu.CompilerParams(
            dimension_semantics=("arbitrary",), vmem_limit_bytes=VMEM_LIMIT_BYTES),
        name="weights",
    )(cond, w_ada, w_ada, b_ada.reshape(1, n_out), w_in_t, w_uq, w_ukv)


def _front_substep(x_ref, mod_ref, ng_ref, win_ref, ws_ref, bs_ref, gv_ref, qg_ref, wq_ref,
                   kvg_ref, tables, out_refs, row0, h_write, h_read, z_write, z_read, emit_cache):
    if emit_cache:
        pk_ref, gb_ref, ckv_ref, kr_ref = out_refs
    else:
        pk_ref, gb_ref = out_refs
    first_half = lax.broadcasted_iota(jnp.int32, tables.shape, 1) < QK_ROPE
    t1 = jnp.where(first_half, tables, 0.0)
    t2k = jnp.where(first_half, pltpu.roll(tables, QK_ROPE, 1), 0.0)
    t2q = jnp.where(first_half, t2k, 1.0)
    rows = slice(row0, row0 + SUB_TILE)
    half = SUB_TILE // 2
    assert IN_WIDTH2 == 10 * PROJ_BLOCK

    def project(j):
        lo = j * PROJ_BLOCK
        z_write[:, lo:lo + PROJ_BLOCK] = jnp.dot(h_read[...], win_ref[:, lo:lo + PROJ_BLOCK],
                                                 preferred_element_type=_F32)

    shift = mod_ref[0:1, 0:D_MODEL]
    gain = ng_ref[...] * (1.0 + mod_ref[0:1, D_MODEL:2 * D_MODEL])

    def norm_rows(lo):
        x = x_ref[0, row0 + lo:row0 + lo + half, :]
        ms = jnp.mean(x * x, axis=-1, keepdims=True)
        h_write[lo:lo + half, :] = ((x * lax.rsqrt(ms + EPS)) * gain + shift).astype(_BF16)

    def vn_head(hd):
        lo = hd * A_HEAD_DIM
        v = _gelu_tanh(z_read[:, OFF_V + lo:OFF_V + lo + A_HEAD_DIM])
        return _rmsnorm(v, gv_ref[:, lo:lo + A_HEAD_DIM]).astype(_BF16)

    def mix_head(hd, vn):
        bias = bs_ref[:, hd:hd + 1]
        parts = [jnp.dot(ws_ref[hd], vn[c * CHUNK:(c + 1) * CHUNK, :],
                         preferred_element_type=_F32) + bias for c in range(SUB_TILE // CHUNK)]
        return jnp.concatenate(parts, axis=0)

    def gate_a(hd, mixed):
        lo = hd * A_HEAD_DIM
        u = _gelu_tanh(z_read[:, OFF_U + lo:OFF_U + lo + A_HEAD_DIM])
        g = _silu(z_read[:, OFF_GA + lo:OFF_GA + lo + A_HEAD_DIM])
        pk_ref[0, rows, PK_A + lo:PK_A + lo + A_HEAD_DIM] = (u * mixed * g).astype(_BF16)

    def gate_b(lo, width):
        gb_ref[0, rows, lo:lo + width] = _silu(z_read[:, OFF_GB + lo:OFF_GB + lo + width])

    def rope_q(qa, hd):
        base = hd * QK_WIDTH
        pair = qa[:, base + KV_RANK:base + QK_WIDTH]
        rope = pair * t1 + pltpu.roll(pair, QK_ROPE, 1) * t2q
        pk_ref[0, rows, PK_Q + base:PK_Q + base + QK_WIDTH] = jnp.concatenate(
            [qa[:, base:base + KV_RANK], rope], axis=1).astype(_BF16)

    project(0)
    cq = _rmsnorm(z_read[:, OFF_CQ:OFF_CQ + Q_RANK],
                  qg_ref[...] * (ATTN_SCALE * LOG2_E)).astype(_BF16)
    vn0 = vn_head(0)
    project(1)
    vn1 = vn_head(1)
    gate_b(0, B_WIDTH // 2)
    project(2)
    vn2 = vn_head(2)
    gate_b(B_WIDTH // 2, B_WIDTH // 2)
    project(3)
    vn3 = vn_head(3)
    ckv = _rmsnorm(z_read[:, OFF_CKV:OFF_CKV + KV_RANK], kvg_ref[...])
    kpair = z_read[:, OFF_KR:OFF_KR + 2 * QK_ROPE]
    krot = kpair * t1 + pltpu.roll(kpair, QK_ROPE, 1) * t2k
    pk_ref[0, rows, PK_K:PK_K + QK_WIDTH] = jnp.concatenate([ckv, krot], axis=1).astype(_BF16)
    if emit_cache:
        ckv_ref[0, 0, rows, :] = ckv
        kr_ref[row0 // SUB_TILE, 0] = kpair.T[0:QK_ROPE, :]
    qa = jnp.dot(cq, wq_ref[...], preferred_element_type=_F32)
    norm_rows(0)
    project(4)
    rope_q(qa, 0)
    rope_q(qa, 1)
    mixed0 = mix_head(0, vn0)
    mixed1 = mix_head(1, vn1)
    rope_q(qa, 2)
    rope_q(qa, 3)
    project(5)
    gate_a(0, mixed0)
    mixed2 = mix_head(2, vn2)
    mixed3 = mix_head(3, vn3)
    project(6)
    gate_a(1, mixed1)
    norm_rows(half)
    project(7)
    gate_a(2, mixed2)
    project(8)
    gate_a(3, mixed3)
    project(9)


def _front_body(x_ref, mod_ref, ng_ref, win_ref, ws_ref, bs_ref, gv_ref, qg_ref, wq_ref,
                kvg_ref, tab_ref, *rest, emit_cache, tiles_per_row):
    out_refs, (h_a, h_b, z_a, z_b) = rest[:-4], rest[-4:]
    t = pl.program_id(0)
    tile = x_ref.shape[1]
    assert tile == 2 * SUB_TILE
    n_pos = tab_ref.shape[0]

    @pl.when(t == 0)
    def _():
        h_b[...] = jnp.zeros(h_b.shape, _BF16)
        z_b[...] = jnp.zeros(z_b.shape, _F32)

    pos_base = (jnp.maximum(t - 1, 0) % tiles_per_row) * tile
    for sub, (h_write, h_read, z_write, z_read) in enumerate(
            ((h_a, h_b, z_a, z_b), (h_b, h_a, z_b, z_a))):
        row0 = sub * SUB_TILE
        pos = pl.multiple_of((pos_base + row0) % n_pos, SUB_TILE)
        tables = tab_ref[pl.ds(pos, SUB_TILE), :]
        _front_substep(x_ref, mod_ref, ng_ref, win_ref, ws_ref, bs_ref, gv_ref, qg_ref, wq_ref,
                       kvg_ref, tables, out_refs, row0, h_write, h_read, z_write, z_read,
                       emit_cache)


def _front_call(x, mod, mod_index, norm_g, win2, ws_bf, bs_t, g_v, q_norm_g, wq2, kv_norm_g,
                tables, emit_cache):
    batch, seq, _ = x.shape
    tile = 2 * SUB_TILE
    tiles_per_row = seq // tile
    n_tiles = batch * tiles_per_row
    assert seq % tile == 0 and tables.shape[0] % SUB_TILE == 0

    def in_tile(t):
        tt = jnp.minimum(t, n_tiles - 1)
        return tt // tiles_per_row, tt % tiles_per_row

    def out_tile(t):
        tt = jnp.maximum(t - 1, 0)
        return tt // tiles_per_row, tt % tiles_per_row

    full = lambda shape: pl.BlockSpec(shape, lambda t: (0,) * len(shape))
    in_specs = [
        pl.BlockSpec((1, tile, D_MODEL), lambda t: (*in_tile(t), 0)),
        pl.BlockSpec((SUBLANES, 3 * D_MODEL), lambda t: (mod_index(in_tile(t)[0]), 0)),
        full((1, D_MODEL)),
        full((D_MODEL, IN_WIDTH2)),
        full((A_HEADS, CHUNK, CHUNK)),
        full((CHUNK, A_HEADS)),
        full((1, A_WIDTH)),
        full((1, Q_RANK)),
        full((Q_RANK, MLA_HEADS * QK_WIDTH)),
        full((1, KV_RANK)),
        full(tables.shape),
    ]
    out_shape = [
        jax.ShapeDtypeStruct((batch, seq, PK_WIDTH), _BF16),
        jax.ShapeDtypeStruct((batch, seq, B_WIDTH), _F32),
    ]
    out_specs = [
        pl.BlockSpec((1, tile, PK_WIDTH), lambda t: (*out_tile(t), 0)),
        pl.BlockSpec((1, tile, B_WIDTH), lambda t: (*out_tile(t), 0)),
    ]
    if emit_cache:
        out_shape += [
            jax.ShapeDtypeStruct((batch, 1, seq, KV_RANK), _F32),
            jax.ShapeDtypeStruct((n_tiles * 2, 1, QK_ROPE, SUB_TILE), _F32),
        ]
        out_specs += [
            pl.BlockSpec((1, 1, tile, KV_RANK), lambda t: (out_tile(t)[0], 0, out_tile(t)[1], 0)),
            pl.BlockSpec((2, 1, QK_ROPE, SUB_TILE), lambda t: (jnp.maximum(t - 1, 0), 0, 0, 0)),
        ]
    return pl.pallas_call(
        functools.partial(_front_body, emit_cache=emit_cache, tiles_per_row=tiles_per_row),
        grid=(n_tiles + 1,),
        in_specs=in_specs,
        out_specs=out_specs,
        out_shape=out_shape,
        scratch_shapes=[pltpu.VMEM((SUB_TILE, D_MODEL), _BF16), pltpu.VMEM((SUB_TILE, D_MODEL), _BF16),
                        pltpu.VMEM((SUB_TILE, IN_WIDTH2), _F32),
                        pltpu.VMEM((SUB_TILE, IN_WIDTH2), _F32)],
        compiler_params=pltpu.CompilerParams(
            dimension_semantics=("arbitrary",),
            vmem_limit_bytes=VMEM_LIMIT_BYTES),
        name="front_ctx" if emit_cache else "front_lat",
    )(x, mod, norm_g, win2, ws_bf, bs_t, g_v, q_norm_g, wq2, kv_norm_g, tables)


def _key_chunks(k_refs):
    chunks = []
    for k_ref in k_refs:
        size = min(KEY_CHUNK, k_ref.shape[1])
        assert k_ref.shape[1] % size == 0
        chunks += [(k_ref, lo, size) for lo in range(0, k_ref.shape[1], size)]
    return chunks


def _values_t(k_ref, row):
    ckv_t = k_ref[row, :, 0:KV_RANK].T
    return jnp.concatenate([ckv_t, jnp.ones((ONES_ROWS, ckv_t.shape[1]), _BF16)], axis=0)


def _attention(q_ref, k_refs, qt_ref, fillers):
    n_rows = q_ref.shape[0]
    chunks = _key_chunks(k_refs)
    steps = [(row, hd, c) for row in range(n_rows) for pair in range(MLA_HEADS // 2)
             for c in range(len(chunks)) for hd in (2 * pair, 2 * pair + 1)]

    for row in range(n_rows):
        for hd in range(MLA_HEADS):
            qt_ref[row * MLA_HEADS + hd] = q_ref[row, :, hd * QK_WIDTH:(hd + 1) * QK_WIDTH].T

    def logits(row, hd, c):
        k_ref, lo, size = chunks[c]
        return jnp.dot(k_ref[row, lo:lo + size, :], qt_ref[row * MLA_HEADS + hd],
                       preferred_element_type=_F32)

    pending = [logits(*step) for step in steps[:SCORE_LOOKAHEAD]]
    values_t = [{id(k_ref): _values_t(k_ref, row) for k_ref in k_refs} for row in range(n_rows)]
    outs = {}
    state = {}
    for j, (row, hd, c) in enumerate(steps):
        m, acc = state.get((row, hd), (None, None))
        if j + SCORE_LOOKAHEAD < len(steps):
            pending.append(logits(*steps[j + SCORE_LOOKAHEAD]))
        s, pending[j] = pending[j], None
        k_ref, lo, size = chunks[c]
        chunk_max = jnp.max(s, axis=0, keepdims=True)
        if c > 0:
            m_new = jnp.maximum(m, chunk_max)
            acc = acc * jnp.exp2(m - m_new)
            m = m_new
        else:
            m = chunk_max
        p = jnp.exp2(s - m).astype(_BF16)
        part = jnp.dot(values_t[row][id(k_ref)][:, lo:lo + size], p, preferred_element_type=_F32)
        acc = acc + part if c > 0 else part
        state[row, hd] = (m, acc)
        if j >= len(steps) - len(fillers):
            fillers[j - len(steps) + len(fillers)]()
        if c == len(chunks) - 1:
            outs[row, hd] = (acc[0:KV_RANK, :] / acc[KV_RANK:KV_RANK + 1, :]).T
    return outs


def _attention_single(q_ref, k_ref, fillers):
    n_rows, tq = q_ref.shape[0], q_ref.shape[1]
    groups = [(row, pair) for row in range(n_rows) for pair in range(MLA_HEADS // 2)]
    scores = {}
    for row, pair in groups:
        q = jnp.concatenate([q_ref[row, :, hd * QK_WIDTH:(hd + 1) * QK_WIDTH]
                             for hd in (2 * pair, 2 * pair + 1)], axis=0)
        scores[row, pair] = lax.dot_general(q, k_ref[row], _NT_DIMS, preferred_element_type=_F32)
    for fill in fillers:
        fill()
    outs = {}
    for row, pair in groups:
        s = scores[row, pair]
        ckv = k_ref[row, :, 0:KV_RANK]
        values = jnp.concatenate([ckv, jnp.ones_like(ckv)], axis=-1)
        p = jnp.exp2(s - jnp.max(s, axis=1, keepdims=True)).astype(_BF16)
        acc = jnp.dot(p, values, preferred_element_type=_F32)
        o = acc[:, 0:KV_RANK] / acc[:, KV_RANK:2 * KV_RANK]
        outs[row, 2 * pair], outs[row, 2 * pair + 1] = o[0:tq, :], o[tq:2 * tq, :]
    return outs


def _ring_copies(streams, sem, step):
    slot = step % X_SLOTS
    copies = []
    for k, (hbm, buf) in enumerate(streams):
        _, n_rows, tq, _ = buf.shape
        tiles = hbm.shape[1] // tq
        copies.append(pltpu.make_async_copy(
            hbm.at[pl.ds((step // tiles) * n_rows, n_rows), pl.ds((step % tiles) * tq, tq)],
            buf.at[slot], sem.at[k, slot]))
    return copies


def _back_body(*refs, has_ctx):
    if has_ctx:
        (q_ref, klat_ref, kctx_ref, a_ref, gb_ref, x_ref, mod_ref, wuv_ref, wo_ref, fg_ref,
         y_ref, qt_ref, ya_ref) = refs
        k_refs = (kctx_ref, klat_ref)
    else:
        (q_ref, klat_ref, a_ref, gb_hbm, x_hbm, mod_ref, wuv_ref, wo_ref, fg_ref, y_ref,
         qt_ref, ya_ref, gbuf, xbuf, ring_sem) = refs
        k_refs = (klat_ref,)
        streams = ((gb_hbm, gbuf), (x_hbm, xbuf))
    n_rows, tq = q_ref.shape[0], q_ref.shape[1]

    ring = not has_ctx
    if ring:
        step = pl.program_id(0) * pl.num_programs(1) + pl.program_id(1)
        n_steps = pl.num_programs(0) * pl.num_programs(1)

        @pl.when(step == 0)
        def _():
            for ahead in range(X_SLOTS - 1):
                for copy in _ring_copies(streams, ring_sem, ahead):
                    copy.start()

        @pl.when(step + X_SLOTS - 1 < n_steps)
        def _():
            for copy in _ring_copies(streams, ring_sem, step + X_SLOTS - 1):
                copy.start()

    block = min(OUT_ROWS, tq)
    blocks = [(row, lo) for row in range(n_rows) for lo in range(0, tq, block)]

    def a_part(row, lo):
        def run():
            ya_ref[row, lo:lo + block, :] = jnp.dot(
                a_ref[row, lo:lo + block, :], wo_ref[0:A_WIDTH, :], preferred_element_type=_F32)
        return run

    fillers = [a_part(*b) for b in blocks]
    if len(_key_chunks(k_refs)) == 1:
        outs = _attention_single(q_ref, k_refs[0], fillers)
    else:
        outs = _attention(q_ref, k_refs, qt_ref, fillers)

    gate = mod_ref[0:1, 2 * D_MODEL:3 * D_MODEL]
    if ring:
        for copy in _ring_copies(streams, ring_sem, step):
            copy.wait()
        gb_ref, x_ref = gbuf.at[step % X_SLOTS], xbuf.at[step % X_SLOTS]
    for row, lo in blocks:
        rows = slice(lo, lo + block)
        attn_cols = []
        for pair in range(MLA_HEADS // 2):
            o2 = jnp.concatenate([outs[row, 2 * pair][rows, :],
                                  outs[row, 2 * pair + 1][rows, :]], axis=1)
            attn_cols.append(jnp.dot(o2.astype(_BF16), wuv_ref[pair],
                                     preferred_element_type=_F32))
        attn = jnp.concatenate(attn_cols, axis=1) * gb_ref[row, rows, :]
        y = ya_ref[row, rows, :] + jnp.dot(
            attn.astype(_BF16), wo_ref[A_WIDTH:A_WIDTH + B_WIDTH, :], preferred_element_type=_F32)
        out = x_ref[row, rows, :] + gate * y
        y_ref[row, rows, :] = _rmsnorm(out, fg_ref[...])


def _back_call(packed, kctx, gb, x, mod, mod_index, wuv2, wo_bf, final_g, rows_per_step):
    batch, seq, _ = x.shape
    tq = min(TOKEN_TILE, seq)
    has_ctx = kctx is not None
    rows = rows_per_step
    assert batch % rows == 0 and (rows == 1 or seq == tq)
    assert (batch // rows) * (seq // tq) >= X_SLOTS - 1
    full = lambda shape: pl.BlockSpec(shape, lambda b, i: (0,) * len(shape))
    keys = lambda arr: pl.BlockSpec((rows, arr.shape[1], arr.shape[2]), lambda b, i: (b, 0, 0))
    in_specs = [
        pl.BlockSpec((rows, tq, MLA_HEADS * QK_WIDTH),
                     lambda b, i: (b, i, PK_Q // (MLA_HEADS * QK_WIDTH))),
        pl.BlockSpec((rows, seq, QK_WIDTH), lambda b, i: (b, 0, PK_K // QK_WIDTH)),
    ]
    args = [packed, packed]
    if has_ctx:
        in_specs += [keys(kctx)]
        args += [kctx]
    in_specs += [
        pl.BlockSpec((rows, tq, A_WIDTH), lambda b, i: (b, i, PK_A // A_WIDTH)),
        (pl.BlockSpec((rows, tq, B_WIDTH), lambda b, i: (b, i, 0)) if has_ctx
         else pl.BlockSpec(memory_space=pl.ANY)),
        (pl.BlockSpec((rows, tq, D_MODEL), lambda b, i: (b, i, 0)) if has_ctx
         else pl.BlockSpec(memory_space=pl.ANY)),
        pl.BlockSpec((SUBLANES, 3 * D_MODEL), lambda b, i: (mod_index(b * rows), 0)),
        full((MLA_HEADS // 2, 2 * KV_RANK, 2 * V_HEAD)),
        full((D_MODEL, D_MODEL)),
        full((1, D_MODEL)),
    ]
    args += [packed, gb, x, mod, wuv2, wo_bf, final_g]
    return pl.pallas_call(
        functools.partial(_back_body, has_ctx=has_ctx),
        grid=(batch // rows, seq // tq),
        in_specs=in_specs,
        out_specs=pl.BlockSpec((rows, tq, D_MODEL), lambda b, i: (b, i, 0)),
        out_shape=jax.ShapeDtypeStruct((batch, seq, D_MODEL), _F32),
        scratch_shapes=[
            pltpu.VMEM((rows * MLA_HEADS, QK_WIDTH, tq), _BF16),
            pltpu.VMEM((rows, tq, D_MODEL), _F32),
        ] + ([] if has_ctx else [
            pltpu.VMEM((X_SLOTS, rows, tq, B_WIDTH), _F32),
            pltpu.VMEM((X_SLOTS, rows, tq, D_MODEL), _F32),
            pltpu.SemaphoreType.DMA((2, X_SLOTS)),
        ]),
        compiler_params=pltpu.CompilerParams(
            dimension_semantics=("arbitrary", "arbitrary"),
            vmem_limit_bytes=VMEM_LIMIT_BYTES),
        name="back_lat" if has_ctx else "back_ctx",
    )(*args)


def _rope_tables(n_tokens, rotate):
    if rotate:
        rows = n_tokens // GRID_W
        inv = ROPE_THETA ** (-jnp.arange(AXIS_PAIRS, dtype=_F32) / AXIS_PAIRS)
        ang_r = jnp.arange(rows, dtype=_F32)[:, None] * inv
        ang_c = jnp.arange(GRID_W, dtype=_F32)[:, None] * inv

        def grid_table(fn):
            tr = jnp.broadcast_to(fn(ang_r)[:, None, :], (rows, GRID_W, AXIS_PAIRS))
            tc = jnp.broadcast_to(fn(ang_c)[None, :, :], (rows, GRID_W, AXIS_PAIRS))
            half = jnp.concatenate([tr, tc], axis=-1).reshape(n_tokens, 2 * AXIS_PAIRS)
            return jnp.concatenate([half, half], axis=-1)

        cos, sin = grid_table(jnp.cos), grid_table(jnp.sin)
    else:
        cos, sin = jnp.ones((n_tokens, QK_ROPE), _F32), jnp.zeros((n_tokens, QK_ROPE), _F32)
    return jnp.concatenate([cos, sin], axis=-1)


def kernel(x_prompt, x_sample, cache_ckv, cache_krope, c, c_ctx, norm_g, w_ada, b_ada, w_in, w_s,
           b_s, g_v, q_norm_g, w_uq, kv_norm_g, w_ukv, w_o, final_g):
    depth = norm_g.shape[0]
    assert depth == 1 and w_in.shape[2] == IN_WIDTH
    dec_batch = x_sample.shape[0]
    xp, xs = x_prompt, x_sample
    new_ckv, new_kr = [], []
    for l in range(depth):
        cond = jnp.concatenate(
            [c, c_ctx[None, :], jnp.zeros((16 - dec_batch - 1, D_MODEL), _F32)], axis=0)
        mod, win2, wq2, wuv2 = _weights_call(cond, w_ada[l], b_ada[l], jnp.transpose(w_in[l]),
                                             w_uq[l], w_ukv[l])
        wo_bf = w_o[l].astype(_BF16)
        ws_bf = w_s[l].astype(_BF16)
        bs_t = b_s[l].T
        gv_row = g_v[l].reshape(1, A_WIDTH)
        shared = (norm_g[l].reshape(1, D_MODEL), win2, ws_bf, bs_t, gv_row,
                  q_norm_g[l].reshape(1, Q_RANK), wq2, kv_norm_g[l].reshape(1, KV_RANK))
        fg = final_g.reshape(1, D_MODEL)

        ctx_index = lambda b: dec_batch
        ctx_batch, ctx_seq, _ = xp.shape
        pack = 2 * SUB_TILE // ctx_seq
        pk_c, gb_c, ckv_c, kr_c = _front_call(
            xp.reshape(ctx_batch // pack, pack * ctx_seq, D_MODEL), mod, ctx_index, *shared,
            _rope_tables(ctx_seq, False), True)
        unpack = lambda arr: arr.reshape(ctx_batch, ctx_seq, arr.shape[-1])
        xp = _back_call(unpack(pk_c), None, unpack(gb_c), xp, mod, ctx_index, wuv2, wo_bf, fg,
                        rows_per_step=pack)
        new_ckv.append(ckv_c.reshape(ctx_batch, 1, ctx_seq, KV_RANK))
        assert ctx_seq == SUB_TILE
        new_kr.append(jnp.swapaxes(kr_c, 2, 3))

        lat_index = lambda b: b
        pk_s, gb_s = _front_call(
            xs, mod, lat_index, *shared, _rope_tables(xs.shape[1], True), False)
        cache_k = jnp.concatenate(
            [cache_ckv[:, l], jnp.zeros(cache_krope[:, l].shape, _F32), cache_krope[:, l]],
            axis=-1).astype(_BF16)
        xs = _back_call(pk_s, cache_k, gb_s, xs, mod, lat_index, wuv2, wo_bf, fg, rows_per_step=1)
    return (xp, xs, jnp.concatenate(new_ckv, axis=1), jnp.concatenate(new_kr, axis=1))
```
